```python
import jax, jax.numpy as jnp
from jax import lax
import numpy as np

D_MODEL = 1024
BATCH = 8
SEQ = 16384
DEPTH = 2

SC_WIDTH = D_MODEL // 4
SC_GROUPS = 4
SC_KERNEL = 3
SB_HEAD_DIM = 64
SB_HEADS = (D_MODEL // 4) // SB_HEAD_DIM
SB_WIDTH = SB_HEADS * SB_HEAD_DIM
SB_BLOCK = 128
SSM_INNER = D_MODEL // 2
SSM_HEAD_DIM = 64
SSM_HEADS = SSM_INNER // SSM_HEAD_DIM
SSM_GROUPS = 2
SSM_STATE = 64
SSM_CONV = 4
SSM_CHUNK = 256
SSM_CONV_DIM = SSM_INNER + 2 * SSM_GROUPS * SSM_STATE
N_BRANCH = 3
FFN_HIDDEN = -(-8 * D_MODEL // (3 * 256)) * 256
NORM_EPS = 1e-6
N_MOD = 6
PROJ_SIZES = (SC_WIDTH, SC_WIDTH, SC_WIDTH,
              SB_WIDTH, SB_WIDTH, SB_WIDTH,
              SSM_INNER, SSM_CONV_DIM, SSM_HEADS,
              D_MODEL, D_MODEL, D_MODEL)
IN_PROJ = sum(PROJ_SIZES)

kernel_name = "hybrid_shortconv_stickbreak_ssd_block"


def rms_norm(x, g):
    x32 = x.astype(jnp.float32)
    y = x32 * lax.rsqrt(jnp.mean(x32 * x32, axis=-1, keepdims=True) + NORM_EPS)
    return (y * g.astype(jnp.float32)).astype(x.dtype)


def causal_depthwise_conv(x, w):
    k = w.shape[0]
    return lax.conv_general_dilated(
        x, w[:, None, :].astype(x.dtype), window_strides=(1,), padding=[(k - 1, 0)],
        dimension_numbers=("NWC", "WIO", "NWC"), feature_group_count=x.shape[-1])


def split_columns(p):
    offsets = [int(o) for o in np.cumsum(PROJ_SIZES)[:-1]]
    return jnp.split(p, offsets, axis=-1)


def short_conv_mixer(b_gate, c_gate, xa, w_conv):
    return b_gate * causal_depthwise_conv(c_gate * xa, w_conv)


def stick_breaking_attention(q, k, v):
    bsz, seq, heads, dh = q.shape
    n_blk = seq // SB_BLOCK
    scale = dh ** -0.5
    qh = q.transpose(0, 2, 1, 3)
    kh = k.transpose(0, 2, 1, 3)
    vh = v.transpose(0, 2, 1, 3)
    strict = jnp.tril(jnp.ones((SB_BLOCK, SB_BLOCK), jnp.float32), -1)
    outs = []
    for i in range(n_blk):
        start, end = i * SB_BLOCK, (i + 1) * SB_BLOCK
        z = jnp.einsum("bhqd,bhkd->bhqk", qh[:, :, start:end], kh[:, :, :end],
                       preferred_element_type=jnp.float32) * scale
        mask = jnp.arange(end)[None, :] < (start + jnp.arange(SB_BLOCK))[:, None]
        log_keep = jnp.where(mask, jax.nn.log_sigmoid(-z), 0.0)
        lk = log_keep.reshape(bsz, heads, SB_BLOCK, i + 1, SB_BLOCK)
        within = jnp.einsum("bhqcj,js->bhqcs", lk, strict)
        blk_tot = jnp.sum(lk, axis=-1)
        later = lax.cumsum(blk_tot, axis=3, reverse=True) - blk_tot
        log_rest = (within + later[..., None]).reshape(bsz, heads, SB_BLOCK, end)
        att = jnp.exp(jnp.where(mask, z + log_keep + log_rest, -jnp.inf))
        outs.append(jnp.einsum("bhqk,bhkd->bhqd", att.astype(vh.dtype), vh[:, :, :end]))
    out = jnp.concatenate(outs, axis=2)
    return out.transpose(0, 2, 1, 3).reshape(bsz, seq, heads * dh)


def segsum_exp(a_cs):
    l = a_cs.shape[-1]
    mask = jnp.tril(jnp.ones((l, l), dtype=bool))
    diff = a_cs[..., :, None] - a_cs[..., None, :]
    return jnp.exp(jnp.where(mask, diff, -jnp.inf))


def ssd_scan(xh, dt, a_neg, bm, cm):
    bsz, seq, heads, hd = xh.shape
    reps = heads // bm.shape[2]
    bh = jnp.repeat(bm, reps, axis=2)
    ch = jnp.repeat(cm, reps, axis=2)
    xdt = xh * dt[..., None]
    a = dt * a_neg
    pad = (-seq) % SSM_CHUNK
    if pad:
        pw = ((0, 0), (0, pad), (0, 0), (0, 0))
        xdt, bh, ch = jnp.pad(xdt, pw), jnp.pad(bh, pw), jnp.pad(ch, pw)
        a = jnp.pad(a, ((0, 0), (0, pad), (0, 0)))
    n_c = (seq + pad) // SSM_CHUNK
    xc = xdt.reshape(bsz, n_c, SSM_CHUNK, heads, hd)
    bc = bh.reshape(bsz, n_c, SSM_CHUNK, heads, -1)
    cc = ch.reshape(bsz, n_c, SSM_CHUNK, heads, -1)
    a_cs = jnp.cumsum(a.reshape(bsz, n_c, SSM_CHUNK, heads), axis=2)
    decay_in = segsum_exp(a_cs.transpose(0, 1, 3, 2))
    scores = jnp.einsum("bclhn,bcshn->bchls", cc, bc) * decay_in
    y_diag = jnp.einsum("bchls,bcshp->bclhp", scores, xc)
    decay_to_end = jnp.exp(a_cs[:, :, -1:, :] - a_cs)
    chunk_states = jnp.einsum("bclhn,bclhp->bchpn", bc * decay_to_end[..., None], xc)
    chunk_decay = jnp.exp(a_cs[:, :, -1, :])

    def step(state, inp):
        s_c, d_c = inp
        return state * d_c[..., None, None] + s_c, state

    init = jnp.zeros((bsz, heads, hd, bc.shape[-1]), jnp.float32)
    _, prev = lax.scan(step, init, (chunk_states.transpose(1, 0, 2, 3, 4),
                                    chunk_decay.transpose(1, 0, 2)))
    prev = prev.transpose(1, 0, 2, 3, 4)
    y_off = jnp.einsum("bclhn,bchpn->bclhp", cc, prev) * jnp.exp(a_cs)[..., None]
    return (y_diag + y_off).reshape(bsz, seq + pad, heads, hd)[:, :seq]


def mamba2_mixer(z, xbc, dt_raw, conv_w, conv_b, dt_bias, a_log, d_skip, norm_w):
    xbc = jax.nn.silu(causal_depthwise_conv(xbc, conv_w) + conv_b)
    xs, bm, cm = jnp.split(xbc, [SSM_INNER, SSM_INNER + SSM_GROUPS * SSM_STATE], axis=-1)
    bsz, seq, _ = xs.shape
    xh = xs.reshape(bsz, seq, SSM_HEADS, SSM_HEAD_DIM).astype(jnp.float32)
    bm = bm.reshape(bsz, seq, SSM_GROUPS, SSM_STATE).astype(jnp.float32)
    cm = cm.reshape(bsz, seq, SSM_GROUPS, SSM_STATE).astype(jnp.float32)
    dt = jax.nn.softplus(dt_raw.astype(jnp.float32) + dt_bias.astype(jnp.float32))
    a_neg = -jnp.exp(a_log.astype(jnp.float32))
    y = ssd_scan(xh, dt, a_neg, bm, cm) + xh * d_skip.astype(jnp.float32)[:, None]
    y = y.reshape(bsz, seq, SSM_INNER) * jax.nn.silu(z.astype(jnp.float32))
    yg = y.reshape(bsz, seq, SSM_GROUPS, SSM_INNER // SSM_GROUPS)
    yg = yg * lax.rsqrt(jnp.mean(yg * yg, axis=-1, keepdims=True) + NORM_EPS)
    return (yg.reshape(bsz, seq, SSM_INNER) * norm_w.astype(jnp.float32)).astype(z.dtype)


def hybrid_layer(x, c, mod_w, mod_b, g_pre_mix, g_post_mix, g_pre_ffn, g_post_ffn,
                 w_in, sc_conv_w, ssm_conv_w, ssm_conv_b, ssm_dt_bias, ssm_a_log, ssm_d,
                 ssm_norm_w, w_sc_out, w_sb_out, w_ssm_out, w_o, w_ffn_in, w_ffn_out):
    bsz, seq, _ = x.shape
    mod = jax.nn.silu(c) @ mod_w + mod_b
    shift1, scale1, gate1, shift2, scale2, gate2 = [m[:, None, :] for m in jnp.split(mod, N_MOD, axis=-1)]

    h = rms_norm(x, g_pre_mix) * (1 + scale1) + shift1
    (sc_b, sc_c, sc_x, q, k, v, z, xbc, dt_raw,
     gl_a, gl_b, gl_c) = split_columns(h @ w_in)
    y_a = short_conv_mixer(sc_b, sc_c, sc_x, sc_conv_w) @ w_sc_out
    y_b = stick_breaking_attention(
        q.reshape(bsz, seq, SB_HEADS, SB_HEAD_DIM),
        k.reshape(bsz, seq, SB_HEADS, SB_HEAD_DIM),
        v.reshape(bsz, seq, SB_HEADS, SB_HEAD_DIM)) @ w_sb_out
    y_c = mamba2_mixer(z, xbc, dt_raw, ssm_conv_w, ssm_conv_b, ssm_dt_bias, ssm_a_log,
                       ssm_d, ssm_norm_w) @ w_ssm_out
    merged = (jax.nn.sigmoid(gl_a) * y_a + jax.nn.sigmoid(gl_b) * y_b
              + jax.nn.sigmoid(gl_c) * y_c)
    mix_out = merged @ w_o
    x = x + (gate1 * rms_norm(mix_out, g_post_mix)).astype(x.dtype)

    h2 = rms_norm(x, g_pre_ffn) * (1 + scale2) + shift2
    gt, up = jnp.split(h2 @ w_ffn_in, 2, axis=-1)
    f = (jax.nn.silu(gt) * up) @ w_ffn_out
    x = x + (gate2 * rms_norm(f, g_post_ffn)).astype(x.dtype)
    return x


def _fwd_setup_inputs(seed: int = 0) -> dict:
    key = jax.random.key(seed)
    ks = jax.random.split(key, 24)
    f32 = jnp.float32

    def nrm(k, shape, fan_in):
        return jax.random.normal(k, shape, f32) * (fan_in ** -0.5)

    def gain(k, shape):
        return 1.0 + 0.05 * jax.random.normal(k, shape, f32)

    dt0 = jnp.exp(jax.random.uniform(ks[14], (DEPTH, SSM_HEADS), f32,
                                     jnp.log(1e-3), jnp.log(1e-1)))
    return {
        "x": jax.random.normal(ks[0], (BATCH, SEQ, D_MODEL), f32),
        "c": jax.random.normal(ks[1], (BATCH, D_MODEL), f32),
        "mod_w": nrm(ks[2], (DEPTH, D_MODEL, N_MOD * D_MODEL), D_MODEL),
        "mod_b": 0.02 * jax.random.normal(ks[3], (DEPTH, N_MOD * D_MODEL), f32),
        "g_pre_mix": gain(ks[4], (DEPTH, D_MODEL)),
        "g_post_mix": gain(ks[5], (DEPTH, D_MODEL)),
        "g_pre_ffn": gain(ks[6], (DEPTH, D_MODEL)),
        "g_post_ffn": gain(ks[7], (DEPTH, D_MODEL)),
        "w_in": nrm(ks[8], (DEPTH, D_MODEL, IN_PROJ), D_MODEL),
        "sc_conv_w": nrm(ks[9], (DEPTH, SC_KERNEL, SC_WIDTH), SC_KERNEL),
        "ssm_conv_w": nrm(ks[10], (DEPTH, SSM_CONV, SSM_CONV_DIM), SSM_CONV),
        "ssm_conv_b": 0.02 * jax.random.normal(ks[11], (DEPTH, SSM_CONV_DIM), f32),
        "ssm_dt_bias": dt0 + jnp.log(-jnp.expm1(-dt0)),
        "ssm_a_log": jnp.log(jax.random.uniform(ks[12], (DEPTH, SSM_HEADS), f32, 1.0, 16.0)),
        "ssm_d": 1.0 + 0.1 * jax.random.normal(ks[13], (DEPTH, SSM_HEADS), f32),
        "ssm_norm_w": gain(ks[15], (DEPTH, SSM_INNER)),
        "w_sc_out": nrm(ks[16], (DEPTH, SC_WIDTH, D_MODEL), SC_WIDTH),
        "w_sb_out": nrm(ks[17], (DEPTH, SB_WIDTH, D_MODEL), SB_WIDTH),
        "w_ssm_out": nrm(ks[18], (DEPTH, SSM_INNER, D_MODEL), SSM_INNER),
        "w_o": nrm(ks[19], (DEPTH, D_MODEL, D_MODEL), D_MODEL),
        "w_ffn_in": nrm(ks[20], (DEPTH, D_MODEL, 2 * FFN_HIDDEN), D_MODEL),
        "w_ffn_out": nrm(ks[21], (DEPTH, FFN_HIDDEN, D_MODEL), FFN_HIDDEN),
    }


def _fwd_reference(x, c, mod_w, mod_b, g_pre_mix, g_post_mix, g_pre_ffn, g_post_ffn, w_in,
              sc_conv_w, ssm_conv_w, ssm_conv_b, ssm_dt_bias, ssm_a_log, ssm_d, ssm_norm_w,
              w_sc_out, w_sb_out, w_ssm_out, w_o, w_ffn_in, w_ffn_out):
    for l in range(DEPTH):
        x = hybrid_layer(x, c, mod_w[l], mod_b[l], g_pre_mix[l], g_post_mix[l],
                         g_pre_ffn[l], g_post_ffn[l], w_in[l], sc_conv_w[l], ssm_conv_w[l],
                         ssm_conv_b[l], ssm_dt_bias[l], ssm_a_log[l], ssm_d[l], ssm_norm_w[l],
                         w_sc_out[l], w_sb_out[l], w_ssm_out[l], w_o[l], w_ffn_in[l],
                         w_ffn_out[l])
    return x


import jax as _jax
import jax.numpy as _jnp

TWIN_FORMAT = 'train_step'
FWD_PARAMS = ['x', 'c', 'mod_w', 'mod_b', 'g_pre_mix', 'g_post_mix', 'g_pre_ffn', 'g_post_ffn', 'w_in', 'sc_conv_w', 'ssm_conv_w', 'ssm_conv_b', 'ssm_dt_bias', 'ssm_a_log', 'ssm_d', 'ssm_norm_w', 'w_sc_out', 'w_sb_out', 'w_ssm_out', 'w_o', 'w_ffn_in', 'w_ffn_out']
TWIN_WEIGHTS = ['mod_w', 'mod_b', 'g_pre_mix', 'g_post_mix', 'g_pre_ffn', 'g_post_ffn', 'w_in', 'sc_conv_w', 'ssm_conv_w', 'ssm_conv_b', 'ssm_dt_bias', 'ssm_a_log', 'ssm_d', 'ssm_norm_w', 'w_sc_out', 'w_sb_out', 'w_ssm_out', 'w_o', 'w_ffn_in', 'w_ffn_out']
TWIN_DIFF_INPUT = 'x'
TWIN_INPUTS = ['x', 'c', 'mod_w', 'mod_b', 'g_pre_mix', 'g_post_mix', 'g_pre_ffn', 'g_post_ffn', 'w_in', 'sc_conv_w', 'ssm_conv_w', 'ssm_conv_b', 'ssm_dt_bias', 'ssm_a_log', 'ssm_d', 'ssm_norm_w', 'w_sc_out', 'w_sb_out', 'w_ssm_out', 'w_o', 'w_ffn_in', 'w_ffn_out', 'loss_target', 'm_mod_w', 'm_mod_b', 'm_g_pre_mix', 'm_g_post_mix', 'm_g_pre_ffn', 'm_g_post_ffn', 'm_w_in', 'm_sc_conv_w', 'm_ssm_conv_w', 'm_ssm_conv_b', 'm_ssm_dt_bias', 'm_ssm_a_log', 'm_ssm_d', 'm_ssm_norm_w', 'm_w_sc_out', 'm_w_sb_out', 'm_w_ssm_out', 'm_w_o', 'm_w_ffn_in', 'm_w_ffn_out', 'v_mod_w', 'v_mod_b', 'v_g_pre_mix', 'v_g_post_mix', 'v_g_pre_ffn', 'v_g_post_ffn', 'v_w_in', 'v_sc_conv_w', 'v_ssm_conv_w', 'v_ssm_conv_b', 'v_ssm_dt_bias', 'v_ssm_a_log', 'v_ssm_d', 'v_ssm_norm_w', 'v_w_sc_out', 'v_w_sb_out', 'v_w_ssm_out', 'v_w_o', 'v_w_ffn_in', 'v_w_ffn_out']
TWIN_OUTPUTS = ['loss', 'grad_x', 'grad_mod_w', 'grad_mod_b', 'grad_g_pre_mix', 'grad_g_post_mix', 'grad_g_pre_ffn', 'grad_g_post_ffn', 'grad_w_in', 'grad_sc_conv_w', 'grad_ssm_conv_w', 'grad_ssm_conv_b', 'grad_ssm_dt_bias', 'grad_ssm_a_log', 'grad_ssm_d', 'grad_ssm_norm_w', 'grad_w_sc_out', 'grad_w_sb_out', 'grad_w_ssm_out', 'grad_w_o', 'grad_w_ffn_in', 'grad_w_ffn_out', 'delta_mod_w', 'delta_mod_b', 'delta_g_pre_mix', 'delta_g_post_mix', 'delta_g_pre_ffn', 'delta_g_post_ffn', 'delta_w_in', 'delta_sc_conv_w', 'delta_ssm_conv_w', 'delta_ssm_conv_b', 'delta_ssm_dt_bias', 'delta_ssm_a_log', 'delta_ssm_d', 'delta_ssm_norm_w', 'delta_w_sc_out', 'delta_w_sb_out', 'delta_w_ssm_out', 'delta_w_o', 'delta_w_ffn_in', 'delta_w_ffn_out', 'new_m_mod_w', 'new_m_mod_b', 'new_m_g_pre_mix', 'new_m_g_post_mix', 'new_m_g_pre_ffn', 'new_m_g_post_ffn', 'new_m_w_in', 'new_m_sc_conv_w', 'new_m_ssm_conv_w', 'new_m_ssm_conv_b', 'new_m_ssm_dt_bias', 'new_m_ssm_a_log', 'new_m_ssm_d', 'new_m_ssm_norm_w', 'new_m_w_sc_out', 'new_m_w_sb_out', 'new_m_w_ssm_out', 'new_m_w_o', 'new_m_w_ffn_in', 'new_m_w_ffn_out', 'new_v_mod_w', 'new_v_mod_b', 'new_v_g_pre_mix', 'new_v_g_post_mix', 'new_v_g_pre_ffn', 'new_v_g_post_ffn', 'new_v_w_in', 'new_v_sc_conv_w', 'new_v_ssm_conv_w', 'new_v_ssm_conv_b', 'new_v_ssm_dt_bias', 'new_v_ssm_a_log', 'new_v_ssm_d', 'new_v_ssm_norm_w', 'new_v_w_sc_out', 'new_v_w_sb_out', 'new_v_w_ssm_out', 'new_v_w_o', 'new_v_w_ffn_in', 'new_v_w_ffn_out']
TWIN_LEAF_KINDS = {'loss': 'loss', 'grad_x': 'grad_x', 'grad_mod_w': 'grad_w', 'grad_mod_b': 'grad_w', 'grad_g_pre_mix': 'grad_w', 'grad_g_post_mix': 'grad_w', 'grad_g_pre_ffn': 'grad_w', 'grad_g_post_ffn': 'grad_w', 'grad_w_in': 'grad_w', 'grad_sc_conv_w': 'grad_w', 'grad_ssm_conv_w': 'grad_w', 'grad_ssm_conv_b': 'grad_w', 'grad_ssm_dt_bias': 'grad_w', 'grad_ssm_a_log': 'grad_w', 'grad_ssm_d': 'grad_w', 'grad_ssm_norm_w': 'grad_w', 'grad_w_sc_out': 'grad_w', 'grad_w_sb_out': 'grad_w', 'grad_w_ssm_out': 'grad_w', 'grad_w_o': 'grad_w', 'grad_w_ffn_in': 'grad_w', 'grad_w_ffn_out': 'grad_w', 'delta_mod_w': 'delta_w', 'delta_mod_b': 'delta_w', 'delta_g_pre_mix': 'delta_w', 'delta_g_post_mix': 'delta_w', 'delta_g_pre_ffn': 'delta_w', 'delta_g_post_ffn': 'delta_w', 'delta_w_in': 'delta_w', 'delta_sc_conv_w': 'delta_w', 'delta_ssm_conv_w': 'delta_w', 'delta_ssm_conv_b': 'delta_w', 'delta_ssm_dt_bias': 'delta_w', 'delta_ssm_a_log': 'delta_w', 'delta_ssm_d': 'delta_w', 'delta_ssm_norm_w': 'delta_w', 'delta_w_sc_out': 'delta_w', 'delta_w_sb_out': 'delta_w', 'delta_w_ssm_out': 'delta_w', 'delta_w_o': 'delta_w', 'delta_w_ffn_in': 'delta_w', 'delta_w_ffn_out': 'delta_w', 'new_m_mod_w': 'new_m', 'new_m_mod_b': 'new_m', 'new_m_g_pre_mix': 'new_m', 'new_m_g_post_mix': 'new_m', 'new_m_g_pre_ffn': 'new_m', 'new_m_g_post_ffn': 'new_m', 'new_m_w_in': 'new_m', 'new_m_sc_conv_w': 'new_m', 'new_m_ssm_conv_w': 'new_m', 'new_m_ssm_conv_b': 'new_m', 'new_m_ssm_dt_bias': 'new_m', 'new_m_ssm_a_log': 'new_m', 'new_m_ssm_d': 'new_m', 'new_m_ssm_norm_w': 'new_m', 'new_m_w_sc_out': 'new_m', 'new_m_w_sb_out': 'new_m', 'new_m_w_ssm_out': 'new_m', 'new_m_w_o': 'new_m', 'new_m_w_ffn_in': 'new_m', 'new_m_w_ffn_out': 'new_m', 'new_v_mod_w': 'new_v', 'new_v_mod_b': 'new_v', 'new_v_g_pre_mix': 'new_v', 'new_v_g_post_mix': 'new_v', 'new_v_g_pre_ffn': 'new_v', 'new_v_g_post_ffn': 'new_v', 'new_v_w_in': 'new_v', 'new_v_sc_conv_w': 'new_v', 'new_v_ssm_conv_w': 'new_v', 'new_v_ssm_conv_b': 'new_v', 'new_v_ssm_dt_bias': 'new_v', 'new_v_ssm_a_log': 'new_v', 'new_v_ssm_d': 'new_v', 'new_v_ssm_norm_w': 'new_v', 'new_v_w_sc_out': 'new_v', 'new_v_w_sb_out': 'new_v', 'new_v_w_ssm_out': 'new_v', 'new_v_w_o': 'new_v', 'new_v_w_ffn_in': 'new_v', 'new_v_w_ffn_out': 'new_v'}


def _forward(args):
    return _fwd_reference(*[args[k] for k in FWD_PARAMS])


def _output_shape():
    def fwd():
        inp = _fwd_setup_inputs(0)
        return _fwd_reference(*[inp[k] for k in FWD_PARAMS])
    out = _jax.eval_shape(fwd)
    return out.shape, out.dtype

N_MICROBATCH = 1
ADAM_LR = 0.001
ADAM_B1 = 0.9
ADAM_B2 = 0.999
ADAM_EPS = 1e-08
ADAM_WD = 0.01
ADAM_STEP = 10
PER_EXAMPLE_BATCH_AXIS = {'x': 0, 'c': 0, 'loss_target': 0}
SHARED_INPUTS = []
_WEIGHT_DTYPES = {'mod_w': _jnp.float32, 'mod_b': _jnp.float32, 'g_pre_mix': _jnp.float32, 'g_post_mix': _jnp.float32, 'g_pre_ffn': _jnp.float32, 'g_post_ffn': _jnp.float32, 'w_in': _jnp.float32, 'sc_conv_w': _jnp.float32, 'ssm_conv_w': _jnp.float32, 'ssm_conv_b': _jnp.float32, 'ssm_dt_bias': _jnp.float32, 'ssm_a_log': _jnp.float32, 'ssm_d': _jnp.float32, 'ssm_norm_w': _jnp.float32, 'w_sc_out': _jnp.float32, 'w_sb_out': _jnp.float32, 'w_ssm_out': _jnp.float32, 'w_o': _jnp.float32, 'w_ffn_in': _jnp.float32, 'w_ffn_out': _jnp.float32}
MOMENT_SCALE = {'mod_w': 9.735831e+00, 'mod_b': 2.125643e+01, 'g_pre_mix': 1.029289e+00, 'g_post_mix': 5.274978e+01, 'g_pre_ffn': 1.016670e+00, 'g_post_ffn': 5.249778e+01, 'w_in': 7.792904e-01, 'sc_conv_w': 1.763214e+00, 'ssm_conv_w': 8.968385e-01, 'ssm_conv_b': 1.364499e+00, 'ssm_dt_bias': 6.016399e-01, 'ssm_a_log': 8.368230e+00, 'ssm_d': 5.946256e+00, 'ssm_norm_w': 1.192520e+00, 'w_sc_out': 9.370143e-01, 'w_sb_out': 1.664712e+00, 'w_ssm_out': 9.569485e-01, 'w_o': 1.967890e+00, 'w_ffn_in': 9.805839e-01, 'w_ffn_out': 1.927665e+00}


def _to_microbatches(a, axis):
    t = _jnp.moveaxis(a, axis, 0)
    t = t.reshape((N_MICROBATCH, t.shape[0] // N_MICROBATCH) + t.shape[1:])
    return _jnp.moveaxis(t, 1, axis + 1)


def setup_inputs(seed: int = 0) -> dict:
    inp = _fwd_setup_inputs(seed)
    key = _jax.random.fold_in(_jax.random.key(seed), 7919)
    shape, _ = _output_shape()
    out = dict(inp)
    out["loss_target"] = _jax.random.normal(_jax.random.fold_in(key, 0), shape, _jnp.float32)
    for i, name in enumerate(TWIN_WEIGHTS):
        w = inp[name].astype(_jnp.float32)
        if MOMENT_SCALE is None:
            s = _jnp.sqrt(_jnp.mean(_jnp.square(w)) + 1e-30)
        else:
            s = MOMENT_SCALE[name]
        km, kv = _jax.random.split(_jax.random.fold_in(key, i + 1))
        out[name] = w
        out["m_" + name] = s * _jax.random.normal(km, w.shape, _jnp.float32)
        out["v_" + name] = (s * s) * _jax.random.uniform(kv, w.shape, _jnp.float32, 0.5, 1.5)
    if N_MICROBATCH > 1:
        for name, axis in PER_EXAMPLE_BATCH_AXIS.items():
            out[name] = _to_microbatches(out[name], axis)
    return {'x': out['x'], 'c': out['c'], 'mod_w': out['mod_w'], 'mod_b': out['mod_b'], 'g_pre_mix': out['g_pre_mix'], 'g_post_mix': out['g_post_mix'], 'g_pre_ffn': out['g_pre_ffn'], 'g_post_ffn': out['g_post_ffn'], 'w_in': out['w_in'], 'sc_conv_w': out['sc_conv_w'], 'ssm_conv_w': out['ssm_conv_w'], 'ssm_conv_b': out['ssm_conv_b'], 'ssm_dt_bias': out['ssm_dt_bias'], 'ssm_a_log': out['ssm_a_log'], 'ssm_d': out['ssm_d'], 'ssm_norm_w': out['ssm_norm_w'], 'w_sc_out': out['w_sc_out'], 'w_sb_out': out['w_sb_out'], 'w_ssm_out': out['w_ssm_out'], 'w_o': out['w_o'], 'w_ffn_in': out['w_ffn_in'], 'w_ffn_out': out['w_ffn_out'], 'loss_target': out['loss_target'], 'm_mod_w': out['m_mod_w'], 'm_mod_b': out['m_mod_b'], 'm_g_pre_mix': out['m_g_pre_mix'], 'm_g_post_mix': out['m_g_post_mix'], 'm_g_pre_ffn': out['m_g_pre_ffn'], 'm_g_post_ffn': out['m_g_post_ffn'], 'm_w_in': out['m_w_in'], 'm_sc_conv_w': out['m_sc_conv_w'], 'm_ssm_conv_w': out['m_ssm_conv_w'], 'm_ssm_conv_b': out['m_ssm_conv_b'], 'm_ssm_dt_bias': out['m_ssm_dt_bias'], 'm_ssm_a_log': out['m_ssm_a_log'], 'm_ssm_d': out['m_ssm_d'], 'm_ssm_norm_w': out['m_ssm_norm_w'], 'm_w_sc_out': out['m_w_sc_out'], 'm_w_sb_out': out['m_w_sb_out'], 'm_w_ssm_out': out['m_w_ssm_out'], 'm_w_o': out['m_w_o'], 'm_w_ffn_in': out['m_w_ffn_in'], 'm_w_ffn_out': out['m_w_ffn_out'], 'v_mod_w': out['v_mod_w'], 'v_mod_b': out['v_mod_b'], 'v_g_pre_mix': out['v_g_pre_mix'], 'v_g_post_mix': out['v_g_post_mix'], 'v_g_pre_ffn': out['v_g_pre_ffn'], 'v_g_post_ffn': out['v_g_post_ffn'], 'v_w_in': out['v_w_in'], 'v_sc_conv_w': out['v_sc_conv_w'], 'v_ssm_conv_w': out['v_ssm_conv_w'], 'v_ssm_conv_b': out['v_ssm_conv_b'], 'v_ssm_dt_bias': out['v_ssm_dt_bias'], 'v_ssm_a_log': out['v_ssm_a_log'], 'v_ssm_d': out['v_ssm_d'], 'v_ssm_norm_w': out['v_ssm_norm_w'], 'v_w_sc_out': out['v_w_sc_out'], 'v_w_sb_out': out['v_w_sb_out'], 'v_w_ssm_out': out['v_w_ssm_out'], 'v_w_o': out['v_w_o'], 'v_w_ffn_in': out['v_w_ffn_in'], 'v_w_ffn_out': out['v_w_ffn_out']}


def _loss(weights, diff, rest, loss_target):
    with _jax.named_scope("forward"):
        args = {**rest, TWIN_DIFF_INPUT: diff, **{k: w.astype(_WEIGHT_DTYPES[k]) for k, w in weights.items()}}
        y = _forward(args)
    with _jax.named_scope("loss_head"):
        err = _jnp.square(y.astype(_jnp.float32) - loss_target)
        return 0.5 * _jnp.sum(_jnp.mean(err, axis=-1)) if err.ndim else 0.5 * err


def _adamw(w, g, m, v):
    m = ADAM_B1 * m + (1.0 - ADAM_B1) * g
    v = ADAM_B2 * v + (1.0 - ADAM_B2) * _jnp.square(g)
    m_hat = m / (1.0 - ADAM_B1 ** ADAM_STEP)
    v_hat = v / (1.0 - ADAM_B2 ** ADAM_STEP)
    delta = -ADAM_LR * (m_hat / (_jnp.sqrt(v_hat) + ADAM_EPS) + ADAM_WD * w)
    return delta, m, v


def reference(x, c, mod_w, mod_b, g_pre_mix, g_post_mix, g_pre_ffn, g_post_ffn, w_in, sc_conv_w, ssm_conv_w, ssm_conv_b, ssm_dt_bias, ssm_a_log, ssm_d, ssm_norm_w, w_sc_out, w_sb_out, w_ssm_out, w_o, w_ffn_in, w_ffn_out, loss_target, m_mod_w, m_mod_b, m_g_pre_mix, m_g_post_mix, m_g_pre_ffn, m_g_post_ffn, m_w_in, m_sc_conv_w, m_ssm_conv_w, m_ssm_conv_b, m_ssm_dt_bias, m_ssm_a_log, m_ssm_d, m_ssm_norm_w, m_w_sc_out, m_w_sb_out, m_w_ssm_out, m_w_o, m_w_ffn_in, m_w_ffn_out, v_mod_w, v_mod_b, v_g_pre_mix, v_g_post_mix, v_g_pre_ffn, v_g_post_ffn, v_w_in, v_sc_conv_w, v_ssm_conv_w, v_ssm_conv_b, v_ssm_dt_bias, v_ssm_a_log, v_ssm_d, v_ssm_norm_w, v_w_sc_out, v_w_sb_out, v_w_ssm_out, v_w_o, v_w_ffn_in, v_w_ffn_out):
    given = dict(x=x, c=c, mod_w=mod_w, mod_b=mod_b, g_pre_mix=g_pre_mix, g_post_mix=g_post_mix, g_pre_ffn=g_pre_ffn, g_post_ffn=g_post_ffn, w_in=w_in, sc_conv_w=sc_conv_w, ssm_conv_w=ssm_conv_w, ssm_conv_b=ssm_conv_b, ssm_dt_bias=ssm_dt_bias, ssm_a_log=ssm_a_log, ssm_d=ssm_d, ssm_norm_w=ssm_norm_w, w_sc_out=w_sc_out, w_sb_out=w_sb_out, w_ssm_out=w_ssm_out, w_o=w_o, w_ffn_in=w_ffn_in, w_ffn_out=w_ffn_out, loss_target=loss_target, m_mod_w=m_mod_w, m_mod_b=m_mod_b, m_g_pre_mix=m_g_pre_mix, m_g_post_mix=m_g_post_mix, m_g_pre_ffn=m_g_pre_ffn, m_g_post_ffn=m_g_post_ffn, m_w_in=m_w_in, m_sc_conv_w=m_sc_conv_w, m_ssm_conv_w=m_ssm_conv_w, m_ssm_conv_b=m_ssm_conv_b, m_ssm_dt_bias=m_ssm_dt_bias, m_ssm_a_log=m_ssm_a_log, m_ssm_d=m_ssm_d, m_ssm_norm_w=m_ssm_norm_w, m_w_sc_out=m_w_sc_out, m_w_sb_out=m_w_sb_out, m_w_ssm_out=m_w_ssm_out, m_w_o=m_w_o, m_w_ffn_in=m_w_ffn_in, m_w_ffn_out=m_w_ffn_out, v_mod_w=v_mod_w, v_mod_b=v_mod_b, v_g_pre_mix=v_g_pre_mix, v_g_post_mix=v_g_post_mix, v_g_pre_ffn=v_g_pre_ffn, v_g_post_ffn=v_g_post_ffn, v_w_in=v_w_in, v_sc_conv_w=v_sc_conv_w, v_ssm_conv_w=v_ssm_conv_w, v_ssm_conv_b=v_ssm_conv_b, v_ssm_dt_bias=v_ssm_dt_bias, v_ssm_a_log=v_ssm_a_log, v_ssm_d=v_ssm_d, v_ssm_norm_w=v_ssm_norm_w, v_w_sc_out=v_w_sc_out, v_w_sb_out=v_w_sb_out, v_w_ssm_out=v_w_ssm_out, v_w_o=v_w_o, v_w_ffn_in=v_w_ffn_in, v_w_ffn_out=v_w_ffn_out)
    weights = {n: given[n] for n in TWIN_WEIGHTS}
    shared = {n: given[n] for n in SHARED_INPUTS}
    per_example = {n: given[n] for n in ['x', 'c']}
    grad_fn = _jax.value_and_grad(_loss, argnums=(0, 1))

    def one_microbatch(ex, loss_target):
        ex = dict(ex)
        diff = ex.pop(TWIN_DIFF_INPUT)
        return grad_fn(weights, diff, {**shared, **ex}, loss_target)

    if N_MICROBATCH == 1:
        loss, (grad_w, grad_x) = one_microbatch(per_example, given["loss_target"])
    else:
        def body(carry, xs):
            loss_sum, grad_sum = carry
            l_k, (gw_k, gx_k) = one_microbatch(xs[0], xs[1])
            with _jax.named_scope("update"):
                return (loss_sum + l_k, _jax.tree.map(_jnp.add, grad_sum, gw_k)), gx_k

        init = (_jnp.zeros((), _jnp.float32), _jax.tree.map(_jnp.zeros_like, weights))
        (loss, grad_w), grad_x = _jax.lax.scan(body, init, (per_example, given["loss_target"]))
    with _jax.named_scope("update"):
        delta_w, new_m, new_v = {}, {}, {}
        for n in TWIN_WEIGHTS:
            delta_w[n], new_m[n], new_v[n] = _adamw(weights[n], grad_w[n], given["m_" + n], given["v_" + n])
    return (loss, grad_x, *[grad_w[n] for n in TWIN_WEIGHTS], *[delta_w[n] for n in TWIN_WEIGHTS],
            *[new_m[n] for n in TWIN_WEIGHTS], *[new_v[n] for n in TWIN_WEIGHTS])
```

```python
import functools
import math

import jax
import jax.numpy as jnp
from jax import lax
from jax.experimental import pallas as pl
from jax.experimental.pallas import tpu as pltpu

F32 = jnp.float32
BF16 = jnp.bfloat16
HI = lax.Precision.HIGHEST

N_DEV = 8
D_MODEL = 1024
DEPTH = 2
SC_WIDTH = 256
SB_WIDTH = 256
SB_HEAD_DIM = 64
SSM_INNER = 512
SSM_HEADS = 8
SSM_HEAD_DIM = 64
SSM_GROUPS = 2
SSM_STATE = 64
SSM_CHUNK = 256
SSM_CONV_DIM = 768
FFN_HIDDEN = 2816
NORM_EPS = 1e-6
IN_PROJ = 5896
LANES = 128
VMEM_LIMIT = 56 * 1024 * 1024

OFF_GATES = 0
OFF_SC = 3072
OFF_QKV = 3840
OFF_XBC = 4608
OFF_DT = 5376
OFF_Z = 5632
IN_PAD = 6144

ADAM_LR = 0.001
ADAM_B1 = 0.9
ADAM_B2 = 0.999
ADAM_EPS = 1e-08
ADAM_WD = 0.01
ADAM_STEP = 10

MESH_ID = pl.DeviceIdType.MESH


def _cparams(sem=None):
    return pltpu.CompilerParams(dimension_semantics=sem, vmem_limit_bytes=VMEM_LIMIT)


def _my_pos():
    return lax.axis_index("x"), lax.axis_index("y"), lax.axis_index("c")


def _peer(k, x, y, c):
    return (x ^ ((k >> 2) & 1), y ^ ((k >> 1) & 1), c ^ (k & 1))


def all_gather_rows(block, name):
    rows, lanes = block.shape

    def body(x_ref, out_ref, send_sems, recv_sems, local_sem):
        x, y, c = _my_pos()
        me, sibling = (x, y, c), (x, y, 1 - c)
        chips = [(1 - x, y), (x, 1 - y), (1 - x, 1 - y)]

        def slot(px, py, pc):
            return out_ref.at[4 * px + 2 * py + pc]

        def copy(k, blk, to, src=None):
            return pltpu.make_async_remote_copy(
                src_ref=slot(*blk) if src is None else src, dst_ref=slot(*blk),
                send_sem=send_sems.at[k], recv_sem=recv_sems.at[k], device_id=to, device_id_type=MESH_ID)

        mine = pltpu.make_async_copy(x_ref, slot(*me), local_sem)
        mine.start()
        first = [copy(0, me, sibling, src=x_ref)]
        first += [copy(1 + j, me, (*chip, c), src=x_ref) for j, chip in enumerate(chips)]
        for cp in first:
            cp.start()
        passed = [copy(4 + j, (*chip, c), sibling) for j, chip in enumerate(chips)]
        for j, chip in enumerate(chips):
            copy(1 + j, (*chip, c), me).wait_recv()
            passed[j].start()
        copy(0, sibling, me).wait_recv()
        for j, chip in enumerate(chips):
            copy(4 + j, (*chip, 1 - c), me).wait_recv()
        for cp in first + passed:
            cp.wait_send()
        mine.wait()

    return pl.pallas_call(
        body, name=name,
        out_shape=jax.ShapeDtypeStruct((N_DEV, rows, lanes), block.dtype),
        in_specs=[pl.BlockSpec(memory_space=pl.ANY)],
        out_specs=pl.BlockSpec(memory_space=pl.ANY),
        scratch_shapes=[pltpu.SemaphoreType.DMA((7,)), pltpu.SemaphoreType.DMA((7,)), pltpu.SemaphoreType.DMA],
    )(block)


def exchange_slots(send, name):
    _, rows, lanes = send.shape

    def body(s_ref, r_ref, send_sems, recv_sems, local_sem):
        x, y, c = _my_pos()
        me = 4 * x + 2 * y + c
        mine = pltpu.make_async_copy(s_ref.at[me], r_ref.at[me], local_sem)
        mine.start()
        copies = []
        for k in range(1, N_DEV):
            px, py, pc = _peer(k, x, y, c)
            cp = pltpu.make_async_remote_copy(
                src_ref=s_ref.at[4 * px + 2 * py + pc], dst_ref=r_ref.at[me],
                send_sem=send_sems.at[k - 1], recv_sem=recv_sems.at[k - 1],
                device_id=(px, py, pc), device_id_type=MESH_ID)
            cp.start()
            copies.append(cp)
        for cp in copies:
            cp.wait_recv()
        for cp in copies:
            cp.wait_send()
        mine.wait()

    return pl.pallas_call(
        body, name=name,
        out_shape=jax.ShapeDtypeStruct((N_DEV, rows, lanes), send.dtype),
        in_specs=[pl.BlockSpec(memory_space=pl.ANY)],
        out_specs=pl.BlockSpec(memory_space=pl.ANY),
        scratch_shapes=[pltpu.SemaphoreType.DMA((7,)), pltpu.SemaphoreType.DMA((7,)), pltpu.SemaphoreType.DMA],
    )(send)


def _pack_rows(parts, dtype, row_multiple):
    flat = [p.astype(dtype).reshape(-1) for p in parts]
    sizes = [f.shape[0] for f in flat]
    total = sum(sizes)
    quantum = LANES * row_multiple
    padded = -(-total // quantum) * quantum
    if padded > total:
        flat.append(jnp.zeros((padded - total,), dtype))
    return jnp.concatenate(flat).reshape(padded // LANES, LANES), sizes


def _unpack(flat, sizes, shapes):
    out, off = [], 0
    lead = flat.shape[:-1]
    for n, shp in zip(sizes, shapes):
        out.append(flat[..., off:off + n].reshape(lead + tuple(shp)))
        off += n
    return out


def rows_call(name, body, n_rows, tr, ins, outs, scratch=()):
    n_tiles = n_rows // tr
    assert n_tiles * tr == n_rows
    in_specs, arrays = [], []
    for arr, kind in ins:
        arrays.append(arr)
        if kind == "row":
            in_specs.append(pl.BlockSpec((tr, arr.shape[1]), lambda i: (i, 0)))
        elif kind == "full":
            in_specs.append(pl.BlockSpec(arr.shape, lambda i, nd=arr.ndim: (0,) * nd))
        elif kind[0] == "row":
            _, w, ci = kind
            in_specs.append(pl.BlockSpec((tr, w), lambda i, ci=ci: (i, ci)))
        elif kind[0] == "prev8":
            _, w, ci = kind
            in_specs.append(pl.BlockSpec((8, w), lambda i, ci=ci: (jnp.maximum(i * (tr // 8) - 1, 0), ci)))
        elif kind[0] == "next8":
            _, w, ci = kind
            last = n_rows // 8 - 1
            in_specs.append(pl.BlockSpec((8, w), lambda i, ci=ci, last=last: (jnp.minimum((i + 1) * (tr // 8), last), ci)))
        else:
            raise ValueError(kind)
    out_specs, out_shapes = [], []
    for shape, dtype, kind in outs:
        out_shapes.append(jax.ShapeDtypeStruct(shape, dtype))
        if kind == "row":
            out_specs.append(pl.BlockSpec((tr, shape[1]), lambda i: (i, 0)))
        else:
            out_specs.append(pl.BlockSpec(shape, lambda i, nd=len(shape): (0,) * nd))
    has_acc = any(k == "acc" for _, _, k in outs)
    return pl.pallas_call(
        body, name=name, grid=(n_tiles,), in_specs=in_specs, out_specs=out_specs, out_shape=out_shapes,
        scratch_shapes=list(scratch),
        compiler_params=_cparams(("arbitrary",) if has_acc else ("parallel",)),
    )(*arrays)


def _acc(ref, val):
    @pl.when(pl.program_id(0) == 0)
    def _():
        ref[...] = jnp.zeros_like(ref)
    ref[...] += val


def _rstd(x):
    return lax.rsqrt(jnp.mean(x * x, axis=-1, keepdims=True) + NORM_EPS)


def _sigmoid(x):
    return 1.0 / (1.0 + jnp.exp(-x))


def _silu(x):
    return x * _sigmoid(x)


def _dsilu(x):
    s = _sigmoid(x)
    return s * (1.0 + x * (1.0 - s))


def _softplus(x):
    return jnp.maximum(x, 0.0) + jnp.log(1.0 + jnp.exp(-jnp.abs(x)))


def normmod_fwd(x, g, scale, shift, name):
    t, d = x.shape

    def body(x_ref, g_ref, sc_ref, sh_ref, h_ref):
        xv = x_ref[...]
        h = xv * _rstd(xv) * g_ref[...] * (1.0 + sc_ref[...]) + sh_ref[...]
        h_ref[...] = h.astype(BF16)

    return rows_call(name, body, t, 512, [(x, "row"), (g, "full"), (scale, "full"), (shift, "full")],
                     [((t, d), BF16, "row")])[0]


def resid_normmod_fwd(x, f, gate, g_post, g_pre, scale, shift, name):
    t, d = x.shape

    def body(x_ref, f_ref, gate_ref, gp_ref, g_ref, sc_ref, sh_ref, xo_ref, h_ref):
        fv = f_ref[...]
        xn = x_ref[...] + gate_ref[...] * (fv * _rstd(fv) * gp_ref[...])
        xo_ref[...] = xn
        h = xn * _rstd(xn) * g_ref[...] * (1.0 + sc_ref[...]) + sh_ref[...]
        h_ref[...] = h.astype(BF16)

    return rows_call(name, body, t, 512,
                     [(x, "row"), (f, "row"), (gate, "full"), (g_post, "full"), (g_pre, "full"), (scale, "full"),
                      (shift, "full")],
                     [((t, d), F32, "row"), ((t, d), BF16, "row")])


def resid_loss(x, f, gate, g_post, target, name):
    t, d = x.shape

    def body(x_ref, f_ref, gate_ref, gp_ref, tg_ref, dy_ref, loss_ref):
        fv = f_ref[...]
        yv = x_ref[...] + gate_ref[...] * (fv * _rstd(fv) * gp_ref[...])
        err = yv - tg_ref[...]
        dy_ref[...] = err * (1.0 / d)
        part = 0.5 * jnp.sum(jnp.mean(err * err, axis=-1, keepdims=True), axis=0, keepdims=True)
        _acc(loss_ref, jnp.broadcast_to(part, loss_ref.shape))

    return rows_call(name, body, t, 512,
                     [(x, "row"), (f, "row"), (gate, "full"), (g_post, "full"), (target, "row")],
                     [((t, d), F32, "row"), ((8, LANES), F32, "acc")])


def resid_bwd(dx, f, gate, g_post, name):
    t, d = dx.shape

    def body(dx_ref, f_ref, gate_ref, gp_ref, df_ref, dgate_ref, dg_ref):
        fv, dxv, gp = f_ref[...], dx_ref[...], gp_ref[...]
        r = _rstd(fv)
        fn = fv * r
        _acc(dgate_ref, jnp.sum(dxv * (fn * gp), axis=0, keepdims=True))
        dn = dxv * gate_ref[...]
        _acc(dg_ref, jnp.sum(dn * fn, axis=0, keepdims=True))
        u = dn * gp
        df = r * (u - fn * jnp.mean(fn * u, axis=-1, keepdims=True))
        df_ref[...] = df.astype(BF16)

    return rows_call(name, body, t, 512, [(dx, "row"), (f, "row"), (gate, "full"), (g_post, "full")],
                     [((t, d), BF16, "row"), ((1, d), F32, "acc"), ((1, d), F32, "acc")])


def normmod_bwd(dh, x, dx_in, g, scale, name):
    t, d = x.shape

    def body(dh_ref, x_ref, dxi_ref, g_ref, sc_ref, dx_ref, dsc_ref, dsh_ref, dg_ref):
        xv, dhv, gv = x_ref[...], dh_ref[...], g_ref[...]
        r = _rstd(xv)
        xn = xv * r
        _acc(dsc_ref, jnp.sum(dhv * (xn * gv), axis=0, keepdims=True))
        _acc(dsh_ref, jnp.sum(dhv, axis=0, keepdims=True))
        dn = dhv * (1.0 + sc_ref[...])
        _acc(dg_ref, jnp.sum(dn * xn, axis=0, keepdims=True))
        u = dn * gv
        dx_ref[...] = dxi_ref[...] + r * (u - xn * jnp.mean(xn * u, axis=-1, keepdims=True))

    return rows_call(name, body, t, 512, [(dh, "row"), (x, "row"), (dx_in, "row"), (g, "full"), (scale, "full")],
                     [((t, d), F32, "row"), ((1, d), F32, "acc"), ((1, d), F32, "acc"), ((1, d), F32, "acc")])


def _pick(n, prefs):
    for p in prefs:
        if n % p == 0:
            return p
    return n


def mm_nn(a_list, b_list, out_dtype, name, tm=512, tn=None, tk=None):
    m, k = a_list[0].shape
    n = b_list[0].shape[1]
    tn = tn or _pick(n, (1024, 768, 512, 256))
    tk = tk or _pick(k, (1024, 1408, 512, 256))
    nk = k // tk
    npair = len(a_list)

    def body(*refs):
        a_refs, b_refs = refs[:npair], refs[npair:2 * npair]
        o_ref, acc = refs[2 * npair], refs[2 * npair + 1]
        kk = pl.program_id(2)

        @pl.when(kk == 0)
        def _():
            acc[...] = jnp.zeros_like(acc)

        s = acc[...]
        for a_ref, b_ref in zip(a_refs, b_refs):
            s = s + jnp.dot(a_ref[...], b_ref[...], preferred_element_type=F32)
        acc[...] = s

        @pl.when(kk == nk - 1)
        def _():
            o_ref[...] = acc[...].astype(o_ref.dtype)

    return pl.pallas_call(
        body, name=name, grid=(m // tm, n // tn, nk),
        in_specs=[pl.BlockSpec((tm, tk), lambda i, j, kk: (i, kk))] * npair
        + [pl.BlockSpec((tk, tn), lambda i, j, kk: (kk, j))] * npair,
        out_specs=pl.BlockSpec((tm, tn), lambda i, j, kk: (i, j)),
        out_shape=jax.ShapeDtypeStruct((m, n), out_dtype),
        scratch_shapes=[pltpu.VMEM((tm, tn), F32)],
        compiler_params=_cparams(("parallel", "parallel", "arbitrary")),
    )(*a_list, *b_list)


def mm_tn(a, b, name, tt=512):
    t, ka = a.shape
    n = b.shape[1]
    ta = _pick(ka, (1024, 1408, 512, 256))
    tn = _pick(n, (1024, 1408, 512, 256))
    nt = t // tt

    def body(a_ref, b_ref, o_ref):
        @pl.when(pl.program_id(2) == 0)
        def _():
            o_ref[...] = jnp.zeros_like(o_ref)

        o_ref[...] += lax.dot_general(a_ref[...], b_ref[...], (((0,), (0,)), ((), ())), preferred_element_type=F32)

    return pl.pallas_call(
        body, name=name, grid=(ka // ta, n // tn, nt),
        in_specs=[pl.BlockSpec((tt, ta), lambda i, j, s: (s, i)), pl.BlockSpec((tt, tn), lambda i, j, s: (s, j))],
        out_specs=pl.BlockSpec((ta, tn), lambda i, j, s: (i, j)),
        out_shape=jax.ShapeDtypeStruct((ka, n), F32),
        compiler_params=_cparams(("parallel", "parallel", "arbitrary")),
    )(a, b)


def mm_swiglu_fwd(h, w_ffn_in, name, tm=512, tn=1408):
    m, k = h.shape
    nh = FFN_HIDDEN // tn

    def body(h_ref, wg_ref, wu_ref, gt_ref, up_ref, a_ref):
        hv = h_ref[...]
        gt = jnp.dot(hv, wg_ref[...], preferred_element_type=F32)
        up = jnp.dot(hv, wu_ref[...], preferred_element_type=F32)
        gt_ref[...] = gt.astype(BF16)
        up_ref[...] = up.astype(BF16)
        a_ref[...] = (_silu(gt) * up).astype(BF16)

    shp = jax.ShapeDtypeStruct((m, FFN_HIDDEN), BF16)
    ospec = pl.BlockSpec((tm, tn), lambda i, j: (i, j))
    return pl.pallas_call(
        body, name=name, grid=(m // tm, nh),
        in_specs=[pl.BlockSpec((tm, k), lambda i, j: (i, 0)), pl.BlockSpec((k, tn), lambda i, j: (0, j)),
                  pl.BlockSpec((k, tn), lambda i, j: (0, j + nh))],
        out_specs=[ospec, ospec, ospec], out_shape=[shp, shp, shp],
        compiler_params=_cparams(("parallel", "parallel")),
    )(h, w_ffn_in, w_ffn_in)


def mm_swiglu_bwd(df, w_out_t, gt, up, name, tm=512, tn=1408):
    m, k = df.shape

    def body(df_ref, w_ref, gt_ref, up_ref, dgt_ref, dup_ref):
        da = jnp.dot(df_ref[...], w_ref[...], preferred_element_type=F32)
        gtv = gt_ref[...].astype(F32)
        upv = up_ref[...].astype(F32)
        dgt_ref[...] = (da * upv * _dsilu(gtv)).astype(BF16)
        dup_ref[...] = (da * _silu(gtv)).astype(BF16)

    shp = jax.ShapeDtypeStruct((m, FFN_HIDDEN), BF16)
    tile = pl.BlockSpec((tm, tn), lambda i, j: (i, j))
    return pl.pallas_call(
        body, name=name, grid=(m // tm, FFN_HIDDEN // tn),
        in_specs=[pl.BlockSpec((tm, k), lambda i, j: (i, 0)), pl.BlockSpec((k, tn), lambda i, j: (0, j)), tile, tile],
        out_specs=[tile, tile], out_shape=[shp, shp],
        compiler_params=_cparams(("parallel", "parallel")),
    )(df, w_out_t, gt, up)


def _shift_down(x, prev8, j):
    if j == 0:
        return x
    xr = pltpu.roll(x, j, 0)
    pr = pltpu.roll(prev8, j, 0)
    row = lax.broadcasted_iota(jnp.int32, (8, x.shape[1]), 0)
    head = jnp.where(row < j, pr, xr[:8])
    return head if x.shape[0] == 8 else jnp.concatenate([head, xr[8:]], axis=0)


def _shift_up(x, next8, j):
    if j == 0:
        return x
    n = x.shape[0]
    xr = pltpu.roll(x, n - j, 0)
    nr = pltpu.roll(next8, 8 - j, 0)
    row = lax.broadcasted_iota(jnp.int32, (8, x.shape[1]), 0)
    return jnp.concatenate([xr[:n - 8], jnp.where(row >= 8 - j, nr, xr[n - 8:])], axis=0)


def _conv_taps(x, prev8, w_ref, taps):
    out = None
    for k in range(taps):
        term = w_ref[k:k + 1, :] * _shift_down(x, prev8, taps - 1 - k)
        out = term if out is None else out + term
    return out


def post_inproj(p, sc_w, ssm_w, ssm_b, name, tr=512):
    t = p.shape[0]

    def body(sc_ref, scp_ref, qkv_ref, xbc_ref, xbcp_ref, scw_ref, sw_ref, sb_ref, ya_ref, qkvo_ref, act_ref):
        first = (pl.program_id(0) > 0).astype(F32)
        sc = sc_ref[...]
        scp = scp_ref[...] * first
        u = sc[:, 256:512] * sc[:, 512:768]
        up = scp[:, 256:512] * scp[:, 512:768]
        ya_ref[...] = (sc[:, 0:256] * _conv_taps(u, up, scw_ref, 3)).astype(BF16)
        qkv = qkv_ref[...]
        qkvo_ref[:, 0:256] = (qkv[:, 0:256] * 0.125).astype(BF16)
        qkvo_ref[:, 256:768] = qkv[:, 256:768].astype(BF16)
        xc = _conv_taps(xbc_ref[...], xbcp_ref[...] * first, sw_ref, 4) + sb_ref[...]
        act_ref[...] = _silu(xc)

    return rows_call(
        name, body, t, tr,
        [(p, ("row", 768, OFF_SC // 768)), (p, ("prev8", 768, OFF_SC // 768)), (p, ("row", 768, OFF_QKV // 768)),
         (p, ("row", 768, OFF_XBC // 768)), (p, ("prev8", 768, OFF_XBC // 768)),
         (sc_w, "full"), (ssm_w, "full"), (ssm_b, "full")],
        [((t, 256), BF16, "row"), ((t, 768), BF16, "row"), ((t, 768), F32, "row")])


def branch_out_fwd(ya, yb, yc, p, w_cat, name, tr=256):
    t = p.shape[0]

    def body(ya_ref, yb_ref, yc_ref, gl_ref, w_ref, o_ref):
        y_a = jnp.dot(ya_ref[...], w_ref[0:256, :], preferred_element_type=F32)
        y_b = jnp.dot(yb_ref[...].astype(BF16), w_ref[256:512, :], preferred_element_type=F32)
        y_c = jnp.dot(yc_ref[...], w_ref[512:1024, :], preferred_element_type=F32)
        m = (_sigmoid(gl_ref[:, 0:1024]) * y_a + _sigmoid(gl_ref[:, 1024:2048]) * y_b
             + _sigmoid(gl_ref[:, 2048:3072]) * y_c)
        o_ref[...] = m.astype(BF16)

    return rows_call(name, body, t, tr,
                     [(ya, "row"), (yb, "row"), (yc, "row"), (p, ("row", 3072, 0)), (w_cat, "full")],
                     [((t, D_MODEL), BF16, "row")])[0]


def branch_out_bwd(dm, ya, yb, yc, p, w_cat, w_cat_t, name, tr=256):
    t = p.shape[0]
    tn_dims = (((0,), (0,)), ((), ()))

    def body(dm_ref, ya_ref, yb_ref, yc_ref, gl_ref, w_ref, wt_ref, dgl_ref, dya_ref, dyb_ref, dyc_ref, dw_ref):
        @pl.when(pl.program_id(0) == 0)
        def _():
            dw_ref[...] = jnp.zeros_like(dw_ref)

        dmv = dm_ref[...]
        ins = (ya_ref[...], yb_ref[...].astype(BF16), yc_ref[...])
        rows = ((0, 256), (256, 512), (512, 1024))
        outs = (dya_ref, dyb_ref, dyc_ref)
        for i in range(3):
            r0, r1 = rows[i]
            y = jnp.dot(ins[i], w_ref[r0:r1, :], preferred_element_type=F32)
            s = _sigmoid(gl_ref[:, 1024 * i:1024 * (i + 1)])
            dgl_ref[:, 1024 * i:1024 * (i + 1)] = (dmv * y * s * (1.0 - s)).astype(BF16)
            dy = (dmv * s).astype(BF16)
            outs[i][...] = jnp.dot(dy, wt_ref[:, r0:r1], preferred_element_type=F32)
            dw_ref[r0:r1, :] += lax.dot_general(ins[i], dy, tn_dims, preferred_element_type=F32)

    return rows_call(name, body, t, tr,
                     [(dm, "row"), (ya, "row"), (yb, "row"), (yc, "row"), (p, ("row", 3072, 0)), (w_cat, "full"),
                      (w_cat_t, "full")],
                     [((t, 3072), BF16, "row"), ((t, 256), F32, "row"), ((t, 256), F32, "row"), ((t, 512), F32, "row"),
                      ((D_MODEL, D_MODEL), F32, "acc")])


def assemble_dp(dgl, dya, p, sc_w, dq, dk, dv, dact, ssm_w, ssm_b, ddt, dz, name, tr=256):
    t = p.shape[0]
    n_tiles = t // tr
    sci, xi = OFF_SC // 768, OFF_XBC // 768

    def body(dgl_ref, dya_ref, dyan_ref, sc_ref, scp_ref, scn_ref, scw_ref, dq_ref, dk_ref, dv_ref,
             dact_ref, dactn_ref, xbc_ref, xbcp_ref, xbcn_ref, sw_ref, sb_ref, ddt_ref, dz_ref,
             o_ref, dscw_ref, dsw_ref, dsb_ref):
        i = pl.program_id(0)

        @pl.when(i == 0)
        def _():
            dscw_ref[...] = jnp.zeros_like(dscw_ref)
            dsw_ref[...] = jnp.zeros_like(dsw_ref)
            dsb_ref[...] = jnp.zeros_like(dsb_ref)

        first = (i > 0).astype(F32)
        last = (i < n_tiles - 1).astype(F32)
        o_ref[:, 0:3072] = dgl_ref[...]
        sc = sc_ref[...]
        scp = scp_ref[...] * first
        scn = scn_ref[...] * last
        u = sc[:, 256:512] * sc[:, 512:768]
        up = scp[:, 256:512] * scp[:, 512:768]
        dya_v = dya_ref[...]
        cv = _conv_taps(u, up, scw_ref, 3)
        o_ref[:, OFF_SC:OFF_SC + 256] = (dya_v * cv).astype(BF16)
        dcv = dya_v * sc[:, 0:256]
        dcvn = dyan_ref[...] * last * scn[:, 0:256]
        du = None
        for k in range(3):
            sh = 2 - k
            term = scw_ref[k:k + 1, :] * _shift_up(dcv, dcvn, sh)
            du = term if du is None else du + term
            dscw_ref[k:k + 1, :] += jnp.sum(dcv * _shift_down(u, up, sh), axis=0, keepdims=True)
        o_ref[:, OFF_SC + 256:OFF_SC + 512] = (du * sc[:, 512:768]).astype(BF16)
        o_ref[:, OFF_SC + 512:OFF_SC + 768] = (du * sc[:, 256:512]).astype(BF16)
        o_ref[:, OFF_QKV:OFF_QKV + 256] = (dq_ref[...] * 0.125).astype(BF16)
        o_ref[:, OFF_QKV + 256:OFF_QKV + 512] = dk_ref[...].astype(BF16)
        o_ref[:, OFF_QKV + 512:OFF_QKV + 768] = dv_ref[...].astype(BF16)
        xb = xbc_ref[...]
        xbp = xbcp_ref[...] * first
        xbn = xbcn_ref[...]
        xc = _conv_taps(xb, xbp, sw_ref, 4) + sb_ref[...]
        xcn = _conv_taps(xbn, xb[tr - 8:, :], sw_ref, 4) + sb_ref[...]
        dxc = dact_ref[...] * _dsilu(xc)
        dxcn = dactn_ref[...] * _dsilu(xcn) * last
        dxb = None
        for k in range(4):
            sh = 3 - k
            term = sw_ref[k:k + 1, :] * _shift_up(dxc, dxcn, sh)
            dxb = term if dxb is None else dxb + term
            dsw_ref[k:k + 1, :] += jnp.sum(dxc * _shift_down(xb, xbp, sh), axis=0, keepdims=True)
        dsb_ref[...] += jnp.sum(dxc, axis=0, keepdims=True)
        o_ref[:, OFF_XBC:OFF_XBC + 768] = dxb.astype(BF16)
        o_ref[:, OFF_DT:OFF_DT + 128] = ddt_ref[...].astype(BF16)
        o_ref[:, OFF_DT + 128:OFF_Z] = jnp.zeros((tr, OFF_Z - OFF_DT - 128), BF16)
        o_ref[:, OFF_Z:IN_PAD] = dz_ref[...].astype(BF16)

    return rows_call(
        name, body, t, tr,
        [(dgl, "row"), (dya, "row"), (dya, ("next8", 256, 0)),
         (p, ("row", 768, sci)), (p, ("prev8", 768, sci)), (p, ("next8", 768, sci)), (sc_w, "full"),
         (dq, "row"), (dk, "row"), (dv, "row"),
         (dact, "row"), (dact, ("next8", 768, 0)),
         (p, ("row", 768, xi)), (p, ("prev8", 768, xi)), (p, ("next8", 768, xi)), (ssm_w, "full"), (ssm_b, "full"),
         (ddt, "row"), (dz, "row")],
        [((t, IN_PAD), BF16, "row"), ((8, 256), F32, "acc"), ((8, 768), F32, "acc"), ((1, 768), F32, "acc")])


def adamw_flat(slots, w, m, v, name, tr=512):
    n_slots, rows, lanes = slots.shape
    tr = min(tr, rows)
    bc1 = 1.0 - ADAM_B1 ** ADAM_STEP
    bc2 = 1.0 - ADAM_B2 ** ADAM_STEP

    def body(s_ref, w_ref, m_ref, v_ref, g_ref, d_ref, mo_ref, vo_ref):
        g = s_ref[0].astype(F32)
        for k in range(1, n_slots):
            g = g + s_ref[k].astype(F32)
        mn = ADAM_B1 * m_ref[...] + (1.0 - ADAM_B1) * g
        vn = ADAM_B2 * v_ref[...] + (1.0 - ADAM_B2) * (g * g)
        m_hat = mn / bc1
        v_hat = vn / bc2
        g_ref[...] = g
        d_ref[...] = -ADAM_LR * (m_hat / (jnp.sqrt(v_hat) + ADAM_EPS) + ADAM_WD * w_ref[...])
        mo_ref[...] = mn
        vo_ref[...] = vn

    tile = pl.BlockSpec((tr, lanes), lambda i: (i, 0))
    shp = jax.ShapeDtypeStruct((rows, lanes), F32)
    return pl.pallas_call(
        body, name=name, grid=(rows // tr,),
        in_specs=[pl.BlockSpec((n_slots, tr, lanes), lambda i: (0, i, 0)), tile, tile, tile],
        out_specs=[tile] * 4, out_shape=[shp] * 4,
        compiler_params=_cparams(("parallel",)),
    )(slots, w, m, v)


def _split_dot(x, tri):
    hi = x.astype(BF16)
    lo = (x - hi.astype(F32)).astype(BF16)
    return jnp.dot(hi, tri, preferred_element_type=F32) + jnp.dot(lo, tri, preferred_element_type=F32)


_NT = (((1,), (1,)), ((), ()))
_TN = (((0,), (0,)), ((), ()))


def sba_fwd(qkv, name, bq=256, bk=256):
    t = qkv.shape[0]
    ratio = bq // bk
    assert t // bk <= LANES

    def body(q_ref, k_ref, v_ref, o_ref, runs_ref):
        i = pl.program_id(1)
        lane = lax.broadcasted_iota(jnp.int32, (1, LANES), 1)
        qi = lax.broadcasted_iota(jnp.int32, (bq, bk), 0) + i * bq
        kj = lax.broadcasted_iota(jnp.int32, (bq, bk), 1)
        strict = (lax.broadcasted_iota(jnp.int32, (bk, bk), 0) > lax.broadcasted_iota(jnp.int32, (bk, bk), 1)).astype(BF16)
        qv = q_ref[...]
        n_kb = (i + 1) * ratio
        accs = []
        for hh in range(2):
            hm = (lane < 64) if hh == 0 else (lane >= 64)
            qm = jnp.where(hm, qv, jnp.zeros_like(qv))

            def step(n, carry, qm=qm):
                acc, run = carry
                j = n_kb - 1 - n
                start = pl.multiple_of(j * bk, bk)
                kb = k_ref[pl.ds(start, bk), :]
                vb = v_ref[pl.ds(start, bk), :]
                mask = (kj + j * bk) < qi
                s = lax.dot_general(qm, kb, _NT, preferred_element_type=F32)
                lk = jnp.where(mask, -_softplus(s), 0.0)
                w = _split_dot(lk, strict)
                a = jnp.where(mask, jnp.exp(s + lk + w + run), 0.0)
                acc = acc + jnp.dot(a.astype(BF16), vb, preferred_element_type=F32)
                runs_ref[hh] += run * (lane == j).astype(F32)
                run = run + jnp.sum(lk, axis=1, keepdims=True)
                return acc, run

            runs_ref[hh] = jnp.zeros((bq, LANES), F32)
            acc, _ = lax.fori_loop(0, n_kb, step, (jnp.zeros((bq, LANES), F32), jnp.zeros((bq, 1), F32)))
            accs.append(acc)
        o_ref[...] = jnp.where(lane < 64, accs[0], accs[1])

    return pl.pallas_call(
        body, name=name, grid=(2, t // bq),
        in_specs=[pl.BlockSpec((bq, LANES), lambda p, i: (i, p)), pl.BlockSpec((t, LANES), lambda p, i: (0, 2 + p)),
                  pl.BlockSpec((t, LANES), lambda p, i: (0, 4 + p))],
        out_specs=[pl.BlockSpec((bq, LANES), lambda p, i: (i, p)), pl.BlockSpec((2, bq, LANES), lambda p, i: (p, i, 0))],
        out_shape=[jax.ShapeDtypeStruct((t, SB_WIDTH), F32), jax.ShapeDtypeStruct((4, t, LANES), F32)],
        compiler_params=_cparams(("parallel", "parallel")),
    )(qkv, qkv, qkv)


def sba_bwd(qkv, runs, do, name, bq=256, bk=256):
    t = qkv.shape[0]
    ratio = bq // bk
    nq = t // bq

    def body(q_ref, k_ref, v_ref, runs_ref, do_ref, dq_ref, dk_hbm, dv_hbm, dk_s, dv_s, sem):
        p = pl.program_id(0)
        i = pl.program_id(1)

        @pl.when(i == 0)
        def _():
            dk_s[...] = jnp.zeros_like(dk_s)
            dv_s[...] = jnp.zeros_like(dv_s)

        lane = lax.broadcasted_iota(jnp.int32, (1, LANES), 1)
        qi = lax.broadcasted_iota(jnp.int32, (bq, bk), 0) + i * bq
        kj = lax.broadcasted_iota(jnp.int32, (bq, bk), 1)
        r2 = lax.broadcasted_iota(jnp.int32, (bk, bk), 0)
        c2 = lax.broadcasted_iota(jnp.int32, (bk, bk), 1)
        later = (r2 > c2).astype(BF16)
        earlier = (r2 < c2).astype(BF16)
        qv = q_ref[...]
        dov = do_ref[...]
        n_kb = (i + 1) * ratio
        dqs = []
        for hh in range(2):
            hm = (lane < 64) if hh == 0 else (lane >= 64)
            qm = jnp.where(hm, qv, jnp.zeros_like(qv))
            dom = jnp.where(hm, dov, 0.0).astype(BF16)
            runs = runs_ref[hh]

            def step(j, carry, qm=qm, dom=dom, runs=runs):
                dq, run_g = carry
                start = pl.multiple_of(j * bk, bk)
                kb = k_ref[pl.ds(start, bk), :]
                vb = v_ref[pl.ds(start, bk), :]
                mask = (kj + j * bk) < qi
                run = jnp.sum(jnp.where(lane == j, runs, 0.0), axis=1, keepdims=True)
                s = lax.dot_general(qm, kb, _NT, preferred_element_type=F32)
                lk_raw = -_softplus(s)
                lk = jnp.where(mask, lk_raw, 0.0)
                w = _split_dot(lk, later)
                a = jnp.where(mask, jnp.exp(s + lk + w + run), 0.0)
                da = lax.dot_general(dom, vb, _NT, preferred_element_type=F32)
                g = a * da
                c = run_g + _split_dot(g, earlier)
                beta = jnp.exp(s + lk_raw)
                dz = jnp.where(mask, g - beta * (g + c), 0.0).astype(BF16)
                dq = dq + jnp.dot(dz, kb, preferred_element_type=F32)
                dk_s[pl.ds(start, bk), :] += lax.dot_general(dz, qm, _TN, preferred_element_type=F32)
                dv_s[pl.ds(start, bk), :] += lax.dot_general(a.astype(BF16), dom, _TN, preferred_element_type=F32)
                run_g = run_g + jnp.sum(g, axis=1, keepdims=True)
                return dq, run_g

            dq, _ = lax.fori_loop(0, n_kb, step, (jnp.zeros((bq, LANES), F32), jnp.zeros((bq, 1), F32)))
            dqs.append(dq)
        dq_ref[...] = jnp.where(lane < 64, dqs[0], dqs[1])

        @pl.when(i == nq - 1)
        def _():
            col = pl.multiple_of(p * LANES, LANES)
            ck = pltpu.make_async_copy(dk_s, dk_hbm.at[:, pl.ds(col, LANES)], sem.at[0])
            cv = pltpu.make_async_copy(dv_s, dv_hbm.at[:, pl.ds(col, LANES)], sem.at[1])
            ck.start()
            cv.start()
            ck.wait()
            cv.wait()

    shp = jax.ShapeDtypeStruct((t, SB_WIDTH), F32)
    tile = pl.BlockSpec((bq, LANES), lambda p, i: (i, p))
    return pl.pallas_call(
        body, name=name, grid=(2, nq),
        in_specs=[tile, pl.BlockSpec((t, LANES), lambda p, i: (0, 2 + p)), pl.BlockSpec((t, LANES), lambda p, i: (0, 4 + p)),
                  pl.BlockSpec((2, bq, LANES), lambda p, i: (p, i, 0)), tile],
        out_specs=[tile, pl.BlockSpec(memory_space=pl.ANY), pl.BlockSpec(memory_space=pl.ANY)],
        out_shape=[shp, shp, shp],
        scratch_shapes=[pltpu.VMEM((t, LANES), F32), pltpu.VMEM((t, LANES), F32), pltpu.SemaphoreType.DMA((2,))],
        compiler_params=_cparams(("arbitrary", "arbitrary")),
    )(qkv, qkv, qkv, runs, do)


def _ssd_consts():
    ln = SSM_CHUNK
    ri = lax.broadcasted_iota(jnp.int32, (ln, ln), 0)
    ci = lax.broadcasted_iota(jnp.int32, (ln, ln), 1)
    eh = lax.broadcasted_iota(jnp.int32, (LANES, SSM_INNER), 0)
    el = lax.broadcasted_iota(jnp.int32, (LANES, SSM_INNER), 1)
    expand = (jnp.right_shift(el, 6) == eh).astype(F32)
    th = lax.broadcasted_iota(jnp.int32, (SSM_INNER, LANES), 1)
    tl = lax.broadcasted_iota(jnp.int32, (SSM_INNER, LANES), 0)
    reduce = (jnp.right_shift(tl, 6) == th).astype(F32)
    return ri, ci, expand, reduce


def _dot_hi(a, b):
    return jnp.dot(a, b, precision=HI, preferred_element_type=F32)


def _ssd_prelude(xbc_ref, dt_ref, dtt_ref, hpr_ref, hpc_ref, ri, ci, expand):
    ln = SSM_CHUNK
    xs = xbc_ref[:, 0:512]
    bm = xbc_ref[:, 512:640]
    cm = xbc_ref[:, 640:768]
    dtb_r = hpr_ref[0:1, :]
    aneg_r = -jnp.exp(hpr_ref[1:2, :])
    pre = dt_ref[...] + dtb_r
    dt = _softplus(pre)
    a = dt * aneg_r
    dtt = _softplus(dtt_ref[...] + hpc_ref[0:8, :])
    att = dtt * (-jnp.exp(hpc_ref[8:16, :]))
    tril = (ri >= ci).astype(F32)
    triu = (ri <= ci).astype(F32)
    acs = _dot_hi(tril, a)
    acst = _dot_hi(att, triu)
    acs_e = _dot_hi(acs, expand)
    dt_e = _dot_hi(dt, expand)
    last_e = acs_e[ln - 1:ln, :]
    e_e = jnp.exp(acs_e)
    w_e = jnp.exp(last_e - acs_e)
    dec_e = jnp.exp(last_e)
    xdt = xs * dt_e
    return dict(xs=xs, bm=bm, cm=cm, pre=pre, dt=dt, aneg_r=aneg_r, acs=acs, acst=acst, dt_e=dt_e, e_e=e_e,
                w_e=w_e, dec_e=dec_e, xdt=xdt, triu=triu)


def ssd_fwd(act, p, dtt, hp_rows, hp_cols, d_e, norm_w, name):
    t = act.shape[0]
    ln = SSM_CHUNK
    nc = t // ln

    def body(xbc_ref, dt_ref, z_ref, dtt_ref, hpr_ref, hpc_ref, d_ref, nw_ref, yc_ref, y_ref, sto_ref, st):
        @pl.when(pl.program_id(0) == 0)
        def _():
            st[...] = jnp.zeros_like(st)

        ri, ci, expand, _ = _ssd_consts()
        q = _ssd_prelude(xbc_ref, dt_ref, dtt_ref, hpr_ref, hpc_ref, ri, ci, expand)
        lane = lax.broadcasted_iota(jnp.int32, (1, LANES), 1)
        rown = lax.broadcasted_iota(jnp.int32, (LANES, 1), 0)
        low = lane < 64
        mask = ri >= ci
        xdt_b = q["xdt"].astype(BF16)
        xw_b = (q["xdt"] * q["w_e"]).astype(BF16)
        bt = q["bm"].T.astype(BF16)
        cb_ = q["cm"].astype(BF16)
        y_pairs = []
        for g in range(2):
            gm = low if g == 0 else jnp.logical_not(low)
            rm = (rown < 64) if g == 0 else (rown >= 64)
            cg = jnp.where(gm, cb_, jnp.zeros_like(cb_))
            cb = jnp.dot(cg, bt, preferred_element_type=F32)
            for pp in range(2):
                pi = 2 * g + pp
                sl = slice(LANES * pi, LANES * (pi + 1))
                xp = xdt_b[:, sl]
                yd = []
                for hh in range(2):
                    h = 2 * pi + hh
                    diff = q["acs"][:, h:h + 1] - q["acst"][h:h + 1, :]
                    lam = jnp.exp(jnp.where(mask, diff, -jnp.inf))
                    yd.append(jnp.dot((cb * lam).astype(BF16), xp, preferred_element_type=F32))
                sp = st[pi]
                sto_ref[0, pi] = sp
                yoff = jnp.dot(cg, sp.astype(BF16), preferred_element_type=F32) * q["e_e"][:, sl]
                upd = jnp.dot(bt, xw_b[:, sl], preferred_element_type=F32)
                st[pi] = q["dec_e"][:, sl] * sp + jnp.where(rm, upd, 0.0)
                y_pairs.append(jnp.where(low, yd[0], yd[1]) + yoff)
        y = jnp.concatenate(y_pairs, axis=1) + q["xs"] * d_ref[...]
        y_ref[...] = y
        yg = y * _silu(z_ref[...])
        for g in range(2):
            sl = slice(256 * g, 256 * (g + 1))
            seg = yg[:, sl]
            yc_ref[:, sl] = (seg * _rstd(seg) * nw_ref[:, sl]).astype(BF16)

    return pl.pallas_call(
        body, name=name, grid=(nc,),
        in_specs=[pl.BlockSpec((ln, 768), lambda c: (c, 0)), pl.BlockSpec((ln, LANES), lambda c: (c, OFF_DT // LANES)),
                  pl.BlockSpec((ln, 512), lambda c: (c, OFF_Z // 512)), pl.BlockSpec((8, ln), lambda c: (0, c)),
                  pl.BlockSpec((8, LANES), lambda c: (0, 0)), pl.BlockSpec((16, ln), lambda c: (0, 0)),
                  pl.BlockSpec((1, 512), lambda c: (0, 0)), pl.BlockSpec((1, 512), lambda c: (0, 0))],
        out_specs=[pl.BlockSpec((ln, 512), lambda c: (c, 0)), pl.BlockSpec((ln, 512), lambda c: (c, 0)),
                   pl.BlockSpec((1, 4, LANES, LANES), lambda c: (c, 0, 0, 0))],
        out_shape=[jax.ShapeDtypeStruct((t, 512), BF16), jax.ShapeDtypeStruct((t, 512), F32),
                   jax.ShapeDtypeStruct((nc, 4, LANES, LANES), F32)],
        scratch_shapes=[pltpu.VMEM((4, LANES, LANES), F32)],
        compiler_params=_cparams(("arbitrary",)),
    )(act, p, p, dtt, hp_rows, hp_cols, d_e, norm_w)


def ssd_bwd(dyc, y, act, p, dtt, states, hp_rows, hp_cols, d_e, norm_w, name):
    t = act.shape[0]
    ln = SSM_CHUNK
    nc = t // ln

    def body(dyc_ref, y_ref, xbc_ref, dt_ref, z_ref, dtt_ref, st_ref, hpr_ref, hpc_ref, d_ref, nw_ref,
             dz_ref, dact_ref, ddt_ref, dnw_ref, dd_ref, dhp_ref, ds):
        @pl.when(pl.program_id(0) == 0)
        def _():
            ds[...] = jnp.zeros_like(ds)
            dnw_ref[...] = jnp.zeros_like(dnw_ref)
            dd_ref[...] = jnp.zeros_like(dd_ref)
            dhp_ref[...] = jnp.zeros_like(dhp_ref)

        ri, ci, expand, reduce = _ssd_consts()
        q = _ssd_prelude(xbc_ref, dt_ref, dtt_ref, hpr_ref, hpc_ref, ri, ci, expand)
        lane = lax.broadcasted_iota(jnp.int32, (1, LANES), 1)
        rown = lax.broadcasted_iota(jnp.int32, (LANES, 1), 0)
        low = lane < 64
        mask = ri >= ci
        mask_t = ci >= ri
        xs, xdt, acs, acst = q["xs"], q["xdt"], q["acs"], q["acst"]
        yv, zv, nw = y_ref[...], z_ref[...], nw_ref[...]
        sg = _sigmoid(zv)
        zz = zv * sg
        yg = yv * zz
        dycv = dyc_ref[...]
        u = dycv * nw
        dyg_parts, dnw_parts = [], []
        for g in range(2):
            sl = slice(256 * g, 256 * (g + 1))
            seg = yg[:, sl]
            rr = _rstd(seg)
            nrm = seg * rr
            dyg_parts.append(rr * (u[:, sl] - nrm * jnp.mean(nrm * u[:, sl], axis=-1, keepdims=True)))
            dnw_parts.append(jnp.sum(dycv[:, sl] * nrm, axis=0, keepdims=True))
        dyg = jnp.concatenate(dyg_parts, axis=1)
        dnw_ref[...] += jnp.concatenate(dnw_parts, axis=1)
        dy = dyg * zz
        dz_ref[...] = dyg * yv * (sg * (1.0 + zv * (1.0 - sg)))
        dd_ref[...] += jnp.sum(dy * xs, axis=0, keepdims=True)
        dxs = dy * d_ref[...]
        dy_b = dy.astype(BF16)
        xdt_b = xdt.astype(BF16)
        xw_b = (xdt * q["w_e"]).astype(BF16)
        bt = q["bm"].T.astype(BF16)
        ct = q["cm"].T.astype(BF16)
        cb_ = q["cm"].astype(BF16)
        bb_ = q["bm"].astype(BF16)
        dacs = jnp.zeros((ln, LANES), F32)
        dc = jnp.zeros((ln, LANES), F32)
        db = jnp.zeros((ln, LANES), F32)
        dxdt_pairs, yoffdy_pairs, dwe_pairs, ddec_pairs = [], [], [], []
        for g in range(2):
            gm = low if g == 0 else jnp.logical_not(low)
            rm = (rown < 64) if g == 0 else (rown >= 64)
            cg = jnp.where(gm, cb_, jnp.zeros_like(cb_))
            bg = jnp.where(gm, bb_, jnp.zeros_like(bb_))
            cb = jnp.dot(cg, bt, preferred_element_type=F32)
            cbt = jnp.dot(bg, ct, preferred_element_type=F32)
            dcb = jnp.zeros((ln, ln), F32)
            dcbt = jnp.zeros((ln, ln), F32)
            for pp in range(2):
                pi = 2 * g + pp
                sl = slice(LANES * pi, LANES * (pi + 1))
                xp = xdt_b[:, sl]
                dyp = dy_b[:, sl]
                xpt = xdt[:, sl].T.astype(BF16)
                dypt = dy[:, sl].T.astype(BF16)
                dxdt_p = jnp.zeros((ln, LANES), F32)
                for hh in range(2):
                    h = 2 * pi + hh
                    hm = low if hh == 0 else jnp.logical_not(low)
                    col = acs[:, h:h + 1]
                    row = acst[h:h + 1, :]
                    lam = jnp.exp(jnp.where(mask, col - row, -jnp.inf))
                    lam_t = jnp.exp(jnp.where(mask_t, row - col, -jnp.inf))
                    m = cb * lam
                    m_t = cbt * lam_t
                    dyh = jnp.where(hm, dyp, jnp.zeros_like(dyp))
                    xh = jnp.where(hm, xp, jnp.zeros_like(xp))
                    dm = jnp.dot(dyh, xpt, preferred_element_type=F32)
                    dm_t = jnp.dot(xh, dypt, preferred_element_type=F32)
                    dcb = dcb + dm * lam
                    dcbt = dcbt + dm_t * lam_t
                    rs = jnp.sum(dm * m, axis=1, keepdims=True) - jnp.sum(dm_t * m_t, axis=1, keepdims=True)
                    dacs = dacs + jnp.where(lane == h, rs, 0.0)
                    dxdt_p = dxdt_p + jnp.dot(m_t.astype(BF16), dyh, preferred_element_type=F32)
                sp = st_ref[0, pi]
                sp_b = sp.astype(BF16)
                dsn = ds[pi]
                dsn_b = dsn.astype(BF16)
                e_p, w_p, dec_p = q["e_e"][:, sl], q["w_e"][:, sl], q["dec_e"][:, sl]
                yoff = jnp.dot(cg, sp_b, preferred_element_type=F32) * e_p
                dyo = dy[:, sl] * e_p
                dyo_b = dyo.astype(BF16)
                dc = dc + lax.dot_general(dyo_b, sp_b, _NT, preferred_element_type=F32)
                ds_prev = dec_p * dsn + jnp.where(rm, jnp.dot(ct, dyo_b, preferred_element_type=F32), 0.0)
                yoffdy_pairs.append(dy[:, sl] * yoff)
                dxw = jnp.dot(bg, dsn_b, preferred_element_type=F32)
                db = db + lax.dot_general(xw_b[:, sl], dsn_b, _NT, preferred_element_type=F32)
                dxdt_p = dxdt_p + dxw * w_p
                dwe_pairs.append(dxw * xdt[:, sl])
                ddec_pairs.append(jnp.sum(dsn * sp, axis=0, keepdims=True))
                ds[pi] = ds_prev
                dxdt_pairs.append(dxdt_p)
            dc = dc + jnp.dot(dcb.astype(BF16), bg, preferred_element_type=F32)
            db = db + jnp.dot(dcbt.astype(BF16), cg, preferred_element_type=F32)
        dxdt = jnp.concatenate(dxdt_pairs, axis=1)
        yoffdy = jnp.concatenate(yoffdy_pairs, axis=1)
        dwe = jnp.concatenate(dwe_pairs, axis=1)
        ddec_e = jnp.broadcast_to(jnp.concatenate(ddec_pairs, axis=1), (8, SSM_INNER))
        last = acs[ln - 1:ln, :]
        w_col = jnp.exp(last - acs)
        dw_col = _dot_hi(dwe, reduce) * w_col
        dacs = dacs + _dot_hi(yoffdy, reduce) - dw_col
        dlast = jnp.sum(dw_col, axis=0, keepdims=True) + jnp.exp(last) * _dot_hi(ddec_e, reduce)[0:1, :]
        rowi = lax.broadcasted_iota(jnp.int32, (ln, 1), 0)
        dacs = dacs + jnp.where(rowi == ln - 1, dlast, 0.0)
        da = _dot_hi(q["triu"], dacs)
        ddt = da * q["aneg_r"] + _dot_hi(dxdt * xs, reduce)
        ddt_raw = jnp.where(lane < SSM_HEADS, ddt * _sigmoid(q["pre"]), 0.0)
        ddt_ref[...] = ddt_raw
        dhp_ref[0:1, :] += jnp.sum(ddt_raw, axis=0, keepdims=True)
        dhp_ref[1:2, :] += jnp.where(lane < SSM_HEADS, jnp.sum(da * q["dt"], axis=0, keepdims=True) * q["aneg_r"], 0.0)
        dact_ref[:, 0:512] = dxs + dxdt * q["dt_e"]
        dact_ref[:, 512:640] = db
        dact_ref[:, 640:768] = dc

    rev = lambda c: nc - 1 - c
    return pl.pallas_call(
        body, name=name, grid=(nc,),
        in_specs=[pl.BlockSpec((ln, 512), lambda c: (rev(c), 0)), pl.BlockSpec((ln, 512), lambda c: (rev(c), 0)),
                  pl.BlockSpec((ln, 768), lambda c: (rev(c), 0)),
                  pl.BlockSpec((ln, LANES), lambda c: (rev(c), OFF_DT // LANES)),
                  pl.BlockSpec((ln, 512), lambda c: (rev(c), OFF_Z // 512)), pl.BlockSpec((8, ln), lambda c: (0, rev(c))),
                  pl.BlockSpec((1, 4, LANES, LANES), lambda c: (rev(c), 0, 0, 0)),
                  pl.BlockSpec((8, LANES), lambda c: (0, 0)), pl.BlockSpec((16, ln), lambda c: (0, 0)),
                  pl.BlockSpec((1, 512), lambda c: (0, 0)), pl.BlockSpec((1, 512), lambda c: (0, 0))],
        out_specs=[pl.BlockSpec((ln, 512), lambda c: (rev(c), 0)), pl.BlockSpec((ln, 768), lambda c: (rev(c), 0)),
                   pl.BlockSpec((ln, LANES), lambda c: (rev(c), 0)), pl.BlockSpec((1, 512), lambda c: (0, 0)),
                   pl.BlockSpec((1, 512), lambda c: (0, 0)), pl.BlockSpec((8, LANES), lambda c: (0, 0))],
        out_shape=[jax.ShapeDtypeStruct((t, 512), F32), jax.ShapeDtypeStruct((t, 768), F32),
                   jax.ShapeDtypeStruct((t, LANES), F32), jax.ShapeDtypeStruct((1, 512), F32),
                   jax.ShapeDtypeStruct((1, 512), F32), jax.ShapeDtypeStruct((8, LANES), F32)],
        scratch_shapes=[pltpu.VMEM((4, LANES, LANES), F32)],
        compiler_params=_cparams(("arbitrary",)),
    )(dyc, y, act, p, p, dtt, states, hp_rows, hp_cols, d_e, norm_w)


def mod_shard_fwd(c_all, mod_w, mod_b_shard, name):
    def body(c_ref, w_ref, b_ref, o_ref):
        sc = _silu(c_ref[...])
        for l in range(DEPTH):
            o_ref[l] = _dot_hi(sc, w_ref[l]) + b_ref[l]

    return pl.pallas_call(body, name=name, out_shape=jax.ShapeDtypeStruct((DEPTH, N_DEV, mod_w.shape[2]), F32),
                          compiler_params=_cparams())(c_all, mod_w, mod_b_shard)


def mod_w_grad(c_all, dmod_shard, name):
    def body(c_ref, d_ref, o_ref):
        sc = _silu(c_ref[...])
        for l in range(DEPTH):
            o_ref[l] = lax.dot_general(sc, d_ref[l], _TN, precision=HI, preferred_element_type=F32)

    return pl.pallas_call(body, name=name, out_shape=jax.ShapeDtypeStruct((DEPTH, D_MODEL, dmod_shard.shape[2]), F32),
                          compiler_params=_cparams())(c_all, dmod_shard)


_BIG = ("w_in", "sc_conv_w", "ssm_conv_w", "w_sc_out", "w_sb_out", "w_ssm_out", "w_o", "w_ffn_in", "w_ffn_out")
_ROW_SHARDED = ("w_o", "w_ffn_out")
_SMALL = ("mod_b", "g_pre_mix", "g_post_mix", "g_pre_ffn", "g_post_ffn", "ssm_conv_b", "ssm_dt_bias", "ssm_a_log",
          "ssm_d", "ssm_norm_w")
_WEIGHTS = ("mod_w", "mod_b", "g_pre_mix", "g_post_mix", "g_pre_ffn", "g_post_ffn", "w_in", "sc_conv_w", "ssm_conv_w",
            "ssm_conv_b", "ssm_dt_bias", "ssm_a_log", "ssm_d", "ssm_norm_w", "w_sc_out", "w_sb_out", "w_ssm_out", "w_o",
            "w_ffn_in", "w_ffn_out")


def _gathered_to_full(g, row_sharded):
    _, dep, r, c = g.shape
    if row_sharded:
        return g.transpose(1, 0, 2, 3).reshape(dep, N_DEV * r, c)
    return g.transpose(1, 2, 0, 3).reshape(dep, r, N_DEV * c)


def _full_to_slots(w, row_sharded):
    dep, r, c = w.shape
    if row_sharded:
        return w.reshape(dep, N_DEV, r // N_DEV, c).transpose(1, 0, 2, 3).reshape(N_DEV, -1)
    return w.reshape(dep, r, N_DEV, c // N_DEV).transpose(2, 0, 1, 3).reshape(N_DEV, -1)


def _pad_in_proj(w):
    sc, qkv, z, xbc, dt, gates = (w[:, 0:768], w[:, 768:1536], w[:, 1536:2048], w[:, 2048:2816], w[:, 2816:2824],
                                  w[:, 2824:5896])
    pad = jnp.zeros((w.shape[0], OFF_Z - OFF_DT - 8), w.dtype)
    return jnp.concatenate([gates, sc, qkv, xbc, dt, pad, z], axis=1)


def _unpad_in_proj(w):
    return jnp.concatenate([w[:, OFF_SC:OFF_SC + 768], w[:, OFF_QKV:OFF_QKV + 768], w[:, OFF_Z:OFF_Z + 512],
                            w[:, OFF_XBC:OFF_XBC + 768], w[:, OFF_DT:OFF_DT + 8], w[:, 0:3072]], axis=1)


def _row(v):
    return v.reshape(1, -1)


def _local_step(x, target, mod, small, conv, big):
    lw, saved = [], []
    for l in range(DEPTH):
        w_in_p = _pad_in_proj(big["w_in"][l])
        w_cat = jnp.concatenate([big["w_sc_out"][l], big["w_sb_out"][l], big["w_ssm_out"][l]], axis=0)
        w_ffn_in_t = big["w_ffn_in"][l].T
        hp_rows = jnp.zeros((8, LANES), F32).at[0, :SSM_HEADS].set(small["ssm_dt_bias"][l]).at[1, :SSM_HEADS].set(
            small["ssm_a_log"][l])
        hp_cols = jnp.concatenate([jnp.broadcast_to(small["ssm_dt_bias"][l][:, None], (SSM_HEADS, SSM_CHUNK)),
                                   jnp.broadcast_to(small["ssm_a_log"][l][:, None], (SSM_HEADS, SSM_CHUNK))], axis=0)
        lw.append(dict(
            w_in_p=w_in_p, w_in_pt=w_in_p.T, w_cat=w_cat, w_cat_t=w_cat.T, w_o=big["w_o"][l], w_o_t=big["w_o"][l].T,
            w_ffn_in=big["w_ffn_in"][l], wg_t=w_ffn_in_t[:FFN_HIDDEN], wu_t=w_ffn_in_t[FFN_HIDDEN:],
            w_ffn_out=big["w_ffn_out"][l], w_ffn_out_t=big["w_ffn_out"][l].T,
            sc_w8=jnp.pad(conv["sc_conv_w"][l], ((0, 5), (0, 0))), ssm_w8=jnp.pad(conv["ssm_conv_w"][l], ((0, 4), (0, 0))),
            ssm_b=_row(small["ssm_conv_b"][l]), hp_rows=hp_rows, hp_cols=hp_cols,
            d_e=_row(jnp.repeat(small["ssm_d"][l], SSM_HEAD_DIM)), norm_w=_row(small["ssm_norm_w"][l]),
            g_pre_mix=_row(small["g_pre_mix"][l]), g_post_mix=_row(small["g_post_mix"][l]),
            g_pre_ffn=_row(small["g_pre_ffn"][l]), g_post_ffn=_row(small["g_post_ffn"][l]),
            shift1=mod[l, 0:1], scale1=mod[l, 1:2], gate1=mod[l, 2:3], shift2=mod[l, 3:4], scale2=mod[l, 4:5],
            gate2=mod[l, 5:6]))

    xl = x
    h = normmod_fwd(xl, lw[0]["g_pre_mix"], lw[0]["scale1"], lw[0]["shift1"], "normmod_fwd_0")
    dy = loss = None
    for l in range(DEPTH):
        w = lw[l]
        p = mm_nn([h], [w["w_in_p"]], F32, f"in_proj_{l}")
        ya, qkv, act = post_inproj(p, w["sc_w8"], w["ssm_w8"], w["ssm_b"], f"post_inproj_{l}")
        o, runs = sba_fwd(qkv, f"sba_fwd_{l}")
        dtt = p[:, OFF_DT:OFF_DT + 8].T
        yc, ypre, states = ssd_fwd(act, p, dtt, w["hp_rows"], w["hp_cols"], w["d_e"], w["norm_w"], f"ssd_fwd_{l}")
        merged = branch_out_fwd(ya, o, yc, p, w["w_cat"], f"branch_fwd_{l}")
        mix = mm_nn([merged], [w["w_o"]], F32, f"out_proj_{l}")
        x1, h2 = resid_normmod_fwd(xl, mix, w["gate1"], w["g_post_mix"], w["g_pre_ffn"], w["scale2"], w["shift2"],
                                   f"resid_mix_{l}")
        gt, up, a = mm_swiglu_fwd(h2, w["w_ffn_in"], f"ffn_in_{l}")
        f = mm_nn([a], [w["w_ffn_out"]], F32, f"ffn_out_{l}")
        saved.append(dict(x=xl, h=h, p=p, ya=ya, qkv=qkv, act=act, o=o, runs=runs, dtt=dtt, yc=yc, ypre=ypre, states=states,
                          merged=merged, mix=mix, x1=x1, h2=h2, gt=gt, up=up, a=a, f=f))
        if l + 1 < DEPTH:
            nw = lw[l + 1]
            xl, h = resid_normmod_fwd(x1, f, w["gate2"], w["g_post_ffn"], nw["g_pre_mix"], nw["scale1"], nw["shift1"],
                                      f"resid_ffn_{l}")
        else:
            dy, loss = resid_loss(x1, f, w["gate2"], w["g_post_ffn"], target, "resid_loss")

    dmod = [None] * DEPTH
    gs = {k: [None] * DEPTH for k in _SMALL + _BIG}
    dxo = dy
    for l in reversed(range(DEPTH)):
        w, s = lw[l], saved[l]
        df, dgate2, gs["g_post_ffn"][l] = resid_bwd(dxo, s["f"], w["gate2"], w["g_post_ffn"], f"resid_ffn_bwd_{l}")
        dgt, dup = mm_swiglu_bwd(df, w["w_ffn_out_t"], s["gt"], s["up"], f"ffn_out_bwd_{l}")
        gs["w_ffn_out"][l] = mm_tn(s["a"], df, f"dw_ffn_out_{l}")
        dh2 = mm_nn([dgt, dup], [w["wg_t"], w["wu_t"]], F32, f"ffn_in_bwd_{l}")
        gs["w_ffn_in"][l] = jnp.concatenate([mm_tn(s["h2"], dgt, f"dw_ffn_gate_{l}"), mm_tn(s["h2"], dup, f"dw_ffn_up_{l}")],
                                            axis=1)
        dx1, dscale2, dshift2, gs["g_pre_ffn"][l] = normmod_bwd(dh2, s["x1"], dxo, w["g_pre_ffn"], w["scale2"],
                                                                f"normmod_ffn_bwd_{l}")
        dmix, dgate1, gs["g_post_mix"][l] = resid_bwd(dx1, s["mix"], w["gate1"], w["g_post_mix"], f"resid_mix_bwd_{l}")
        dmerged = mm_nn([dmix], [w["w_o_t"]], F32, f"out_proj_bwd_{l}")
        gs["w_o"][l] = mm_tn(s["merged"], dmix, f"dw_o_{l}")
        dgl, dya, dyb, dyc, dw_cat = branch_out_bwd(dmerged, s["ya"], s["o"], s["yc"], s["p"], w["w_cat"], w["w_cat_t"],
                                                    f"branch_bwd_{l}")
        gs["w_sc_out"][l], gs["w_sb_out"][l], gs["w_ssm_out"][l] = dw_cat[0:256], dw_cat[256:512], dw_cat[512:1024]
        dz, dact, ddt, dnw, dd_e, dhp = ssd_bwd(dyc, s["ypre"], s["act"], s["p"], s["dtt"], s["states"], w["hp_rows"],
                                                w["hp_cols"], w["d_e"], w["norm_w"], f"ssd_bwd_{l}")
        gs["ssm_norm_w"][l] = dnw[0]
        gs["ssm_d"][l] = dd_e.reshape(SSM_HEADS, SSM_HEAD_DIM).sum(axis=1)
        gs["ssm_dt_bias"][l] = dhp[0, :SSM_HEADS]
        gs["ssm_a_log"][l] = dhp[1, :SSM_HEADS]
        dq, dk, dv = sba_bwd(s["qkv"], s["runs"], dyb, f"sba_bwd_{l}")
        dp, dscw, dssw, dssb = assemble_dp(dgl, dya, s["p"], w["sc_w8"], dq, dk, dv, dact, w["ssm_w8"], w["ssm_b"], ddt, dz,
                                           f"assemble_dp_{l}")
        gs["sc_conv_w"][l], gs["ssm_conv_w"][l], gs["ssm_conv_b"][l] = dscw[0:3], dssw[0:4], dssb[0]
        dh = mm_nn([dp], [w["w_in_pt"]], F32, f"in_proj_bwd_{l}")
        gs["w_in"][l] = _unpad_in_proj(mm_tn(s["h"], dp, f"dw_in_{l}"))
        dxo, dscale1, dshift1, gs["g_pre_mix"][l] = normmod_bwd(dh, s["x"], dx1, w["g_pre_mix"], w["scale1"],
                                                                f"normmod_mix_bwd_{l}")
        dmod[l] = jnp.concatenate([dshift1, dscale1, dgate1, dshift2, dscale2, dgate2], axis=0)
    for k in ("g_pre_mix", "g_post_mix", "g_pre_ffn", "g_post_ffn"):
        gs[k] = [g[0] for g in gs[k]]
    grads = {k: jnp.stack(v) for k, v in gs.items() if k != "mod_b"}
    return loss[0, 0], dxo, jnp.stack(dmod), grads


def kernel(x, c, mod_w, mod_b, g_pre_mix, g_post_mix, g_pre_ffn, g_post_ffn, w_in, sc_conv_w, ssm_conv_w, ssm_conv_b, ssm_dt_bias, ssm_a_log, ssm_d, ssm_norm_w, w_sc_out, w_sb_out, w_ssm_out, w_o, w_ffn_in, w_ffn_out, loss_target, m_mod_w, m_mod_b, m_g_pre_mix, m_g_post_mix, m_g_pre_ffn, m_g_post_ffn, m_w_in, m_sc_conv_w, m_ssm_conv_w, m_ssm_conv_b, m_ssm_dt_bias, m_ssm_a_log, m_ssm_d, m_ssm_norm_w, m_w_sc_out, m_w_sb_out, m_w_ssm_out, m_w_o, m_w_ffn_in, m_w_ffn_out, v_mod_w, v_mod_b, v_g_pre_mix, v_g_post_mix, v_g_pre_ffn, v_g_post_ffn, v_w_in, v_sc_conv_w, v_ssm_conv_w, v_ssm_conv_b, v_ssm_dt_bias, v_ssm_a_log, v_ssm_d, v_ssm_norm_w, v_w_sc_out, v_w_sb_out, v_w_ssm_out, v_w_o, v_w_ffn_in, v_w_ffn_out):
    wts = dict(mod_w=mod_w, mod_b=mod_b, g_pre_mix=g_pre_mix, g_post_mix=g_post_mix, g_pre_ffn=g_pre_ffn,
               g_post_ffn=g_post_ffn, w_in=w_in, sc_conv_w=sc_conv_w, ssm_conv_w=ssm_conv_w, ssm_conv_b=ssm_conv_b,
               ssm_dt_bias=ssm_dt_bias, ssm_a_log=ssm_a_log, ssm_d=ssm_d, ssm_norm_w=ssm_norm_w, w_sc_out=w_sc_out,
               w_sb_out=w_sb_out, w_ssm_out=w_ssm_out, w_o=w_o, w_ffn_in=w_ffn_in, w_ffn_out=w_ffn_out)
    ms = dict(mod_w=m_mod_w, mod_b=m_mod_b, g_pre_mix=m_g_pre_mix, g_post_mix=m_g_post_mix, g_pre_ffn=m_g_pre_ffn,
              g_post_ffn=m_g_post_ffn, w_in=m_w_in, sc_conv_w=m_sc_conv_w, ssm_conv_w=m_ssm_conv_w,
              ssm_conv_b=m_ssm_conv_b, ssm_dt_bias=m_ssm_dt_bias, ssm_a_log=m_ssm_a_log, ssm_d=m_ssm_d,
              ssm_norm_w=m_ssm_norm_w, w_sc_out=m_w_sc_out, w_sb_out=m_w_sb_out, w_ssm_out=m_w_ssm_out, w_o=m_w_o,
              w_ffn_in=m_w_ffn_in, w_ffn_out=m_w_ffn_out)
    vs = dict(mod_w=v_mod_w, mod_b=v_mod_b, g_pre_mix=v_g_pre_mix, g_post_mix=v_g_post_mix, g_pre_ffn=v_g_pre_ffn,
              g_post_ffn=v_g_post_ffn, w_in=v_w_in, sc_conv_w=v_sc_conv_w, ssm_conv_w=v_ssm_conv_w,
              ssm_conv_b=v_ssm_conv_b, ssm_dt_bias=v_ssm_dt_bias, ssm_a_log=v_ssm_a_log, ssm_d=v_ssm_d,
              ssm_norm_w=v_ssm_norm_w, w_sc_out=v_w_sc_out, w_sb_out=v_w_sb_out, w_ssm_out=v_w_ssm_out, w_o=v_w_o,
              w_ffn_in=v_w_ffn_in, w_ffn_out=v_w_ffn_out)
    me = 4 * lax.axis_index("x") + 2 * lax.axis_index("y") + lax.axis_index("c")
    mod_cols = mod_w.shape[2]

    pack1, sizes1 = _pack_rows([c, sc_conv_w, ssm_conv_w], F32, 8)
    got1 = all_gather_rows(pack1, "gather_c_conv").reshape(N_DEV, -1)
    c_all, sc_g, ssm_g = _unpack(got1, sizes1, [(D_MODEL,), sc_conv_w.shape, ssm_conv_w.shape])
    conv = dict(sc_conv_w=_gathered_to_full(sc_g, False), ssm_conv_w=_gathered_to_full(ssm_g, False))

    mod_b_shard = lax.dynamic_slice_in_dim(mod_b, me * mod_cols, mod_cols, axis=1).reshape(DEPTH, 1, mod_cols)
    mod_sh = mod_shard_fwd(c_all, mod_w, mod_b_shard, "mod_shard_fwd")
    pack2, sizes2 = _pack_rows([mod_sh], F32, 8)
    got2 = all_gather_rows(pack2, "gather_mod").reshape(N_DEV, -1)
    mod_all = _unpack(got2, sizes2, [mod_sh.shape])[0]
    mod_mine = lax.dynamic_index_in_dim(mod_all, me, axis=2, keepdims=False)
    mod = mod_mine.transpose(1, 0, 2).reshape(DEPTH, 6, D_MODEL)

    ag_names = [k for k in _BIG if k not in ("sc_conv_w", "ssm_conv_w")]
    pack3, sizes3 = _pack_rows([wts[k] for k in ag_names], BF16, 16)
    got3 = all_gather_rows(pack3, "gather_weights").reshape(N_DEV, -1)
    parts3 = _unpack(got3, sizes3, [wts[k].shape for k in ag_names])
    big = {k: _gathered_to_full(g, k in _ROW_SHARDED) for k, g in zip(ag_names, parts3)}

    small = {k: wts[k] for k in _SMALL}
    loss_part, dx, dmod, grads = _local_step(x[0], loss_target[0], mod, small, conv, big)
    loss = lax.psum(loss_part, ("x", "y", "c"))

    small_parts = [dmod.reshape(DEPTH, 6 * D_MODEL)] + [grads[k] for k in _SMALL[1:]]
    pack5, sizes5 = _pack_rows(small_parts, F32, 8)
    got5 = all_gather_rows(pack5, "gather_small_grads")
    w5, _ = _pack_rows([wts[k] for k in _SMALL], F32, 8)
    m5, _ = _pack_rows([ms[k] for k in _SMALL], F32, 8)
    v5, _ = _pack_rows([vs[k] for k in _SMALL], F32, 8)
    res5 = adamw_flat(got5, w5, m5, v5, "adamw_small")
    small_out = [_unpack(r.reshape(-1), sizes5, [wts[k].shape for k in _SMALL]) for r in res5]

    dmod_all = got5.reshape(N_DEV, -1)[:, :DEPTH * 6 * D_MODEL].reshape(N_DEV, DEPTH, 6 * D_MODEL)
    dmod_shard = lax.dynamic_slice_in_dim(dmod_all, me * mod_cols, mod_cols, axis=2).transpose(1, 0, 2)
    g_mod_w = mod_w_grad(c_all, dmod_shard, "mod_w_grad")
    packw, sizesw = _pack_rows([g_mod_w], F32, 512)
    res_mw = adamw_flat(packw[None], _pack_rows([mod_w], F32, 512)[0], _pack_rows([m_mod_w], F32, 512)[0],
                        _pack_rows([v_mod_w], F32, 512)[0], "adamw_mod_w")
    mod_w_out = [_unpack(r.reshape(-1), sizesw, [mod_w.shape])[0] for r in res_mw]

    slot_parts = [_full_to_slots(grads[k], k in _ROW_SHARDED) for k in _BIG]
    slot_sizes = [s.shape[1] for s in slot_parts]
    total = sum(slot_sizes)
    quantum = LANES * 512
    padded = -(-total // quantum) * quantum
    send = jnp.concatenate(slot_parts + [jnp.zeros((N_DEV, padded - total), F32)], axis=1).reshape(N_DEV, -1, LANES)
    recv = exchange_slots(send, "exchange_grads")
    w7, _ = _pack_rows([wts[k] for k in _BIG], F32, 512)
    m7, _ = _pack_rows([ms[k] for k in _BIG], F32, 512)
    v7, _ = _pack_rows([vs[k] for k in _BIG], F32, 512)
    res7 = adamw_flat(recv, w7, m7, v7, "adamw_big")
    big_out = [_unpack(r.reshape(-1), slot_sizes, [wts[k].shape for k in _BIG]) for r in res7]

    outs = []
    for kind in range(4):
        by_name = {"mod_w": mod_w_out[kind]}
        by_name.update(zip(_SMALL, small_out[kind]))
        by_name.update(zip(_BIG, big_out[kind]))
        outs.extend(by_name[k] for k in _WEIGHTS)
    return (loss, dx[None], *outs)
```

```python
import functools
import math

import jax
import jax.numpy as jnp
from jax import lax
from jax.experimental import pallas as pl
from jax.experimental.pallas import tpu as pltpu

F32 = jnp.float32
BF16 = jnp.bfloat16
HI = lax.Precision.HIGHEST

N_DEV = 8
D_MODEL = 1024
DEPTH = 2
SC_WIDTH = 256
SB_WIDTH = 256
SB_HEAD_DIM = 64
SSM_INNER = 512
SSM_HEADS = 8
SSM_HEAD_DIM = 64
SSM_GROUPS = 2
SSM_STATE = 64
SSM_CHUNK = 256
SSM_CONV_DIM = 768
FFN_HIDDEN = 2816
NORM_EPS = 1e-6
IN_PROJ = 5896
LANES = 128
VMEM_LIMIT = 56 * 1024 * 1024

OFF_GATES = 0
OFF_SC = 3072
OFF_QKV = 3840
OFF_XBC = 4608
OFF_DT = 5376
OFF_Z = 5632
IN_PAD = 6144

ADAM_LR = 0.001
ADAM_B1 = 0.9
ADAM_B2 = 0.999
ADAM_EPS = 1e-08
ADAM_WD = 0.01
ADAM_STEP = 10

MESH_ID = pl.DeviceIdType.MESH


def _cparams(sem=None):
    return pltpu.CompilerParams(dimension_semantics=sem, vmem_limit_bytes=VMEM_LIMIT)


def _my_pos():
    return lax.axis_index("x"), lax.axis_index("y"), lax.axis_index("c")


def _peer(k, x, y, c):
    return (x ^ ((k >> 2) & 1), y ^ ((k >> 1) & 1), c ^ (k & 1))


def all_gather_rows(block, name):
    rows, lanes = block.shape

    def body(x_ref, out_ref, send_sems, recv_sems, local_sem):
        x, y, c = _my_pos()
        me, sibling = (x, y, c), (x, y, 1 - c)
        chips = [(1 - x, y), (x, 1 - y), (1 - x, 1 - y)]

        def slot(px, py, pc):
            return out_ref.at[4 * px + 2 * py + pc]

        def copy(k, blk, to, src=None):
            return pltpu.make_async_remote_copy(
                src_ref=slot(*blk) if src is None else src, dst_ref=slot(*blk),
                send_sem=send_sems.at[k], recv_sem=recv_sems.at[k], device_id=to, device_id_type=MESH_ID)

        mine = pltpu.make_async_copy(x_ref, slot(*me), local_sem)
        mine.start()
        first = [copy(0, me, sibling, src=x_ref)]
        first += [copy(1 + j, me, (*chip, c), src=x_ref) for j, chip in enumerate(chips)]
        for cp in first:
            cp.start()
        passed = [copy(4 + j, (*chip, c), sibling) for j, chip in enumerate(chips)]
        for j, chip in enumerate(chips):
            copy(1 + j, (*chip, c), me).wait_recv()
            passed[j].start()
        copy(0, sibling, me).wait_recv()
        for j, chip in enumerate(chips):
            copy(4 + j, (*chip, 1 - c), me).wait_recv()
        for cp in first + passed:
            cp.wait_send()
        mine.wait()

    return pl.pallas_call(
        body, name=name,
        out_shape=jax.ShapeDtypeStruct((N_DEV, rows, lanes), block.dtype),
        in_specs=[pl.BlockSpec(memory_space=pl.ANY)],
        out_specs=pl.BlockSpec(memory_space=pl.ANY),
        scratch_shapes=[pltpu.SemaphoreType.DMA((7,)), pltpu.SemaphoreType.DMA((7,)), pltpu.SemaphoreType.DMA],
    )(block)


def exchange_slots(send, name):
    _, rows, lanes = send.shape

    def body(s_ref, r_ref, send_sems, recv_sems, local_sem):
        x, y, c = _my_pos()
        me = 4 * x + 2 * y + c
        mine = pltpu.make_async_copy(s_ref.at[me], r_ref.at[me], local_sem)
        mine.start()
        copies = []
        for k in range(1, N_DEV):
            px, py, pc = _peer(k, x, y, c)
            cp = pltpu.make_async_remote_copy(
                src_ref=s_ref.at[4 * px + 2 * py + pc], dst_ref=r_ref.at[me],
                send_sem=send_sems.at[k - 1], recv_sem=recv_sems.at[k - 1],
                device_id=(px, py, pc), device_id_type=MESH_ID)
            cp.start()
            copies.append(cp)
        for cp in copies:
            cp.wait_recv()
        for cp in copies:
            cp.wait_send()
        mine.wait()

    return pl.pallas_call(
        body, name=name,
        out_shape=jax.ShapeDtypeStruct((N_DEV, rows, lanes), send.dtype),
        in_specs=[pl.BlockSpec(memory_space=pl.ANY)],
        out_specs=pl.BlockSpec(memory_space=pl.ANY),
        scratch_shapes=[pltpu.SemaphoreType.DMA((7,)), pltpu.SemaphoreType.DMA((7,)), pltpu.SemaphoreType.DMA],
    )(send)


def _pack_rows(parts, dtype, row_multiple):
    flat = [p.astype(dtype).reshape(-1) for p in parts]
    sizes = [f.shape[0] for f in flat]
    total = sum(sizes)
    quantum = LANES * row_multiple
    padded = -(-total // quantum) * quantum
    if padded > total:
        flat.append(jnp.zeros((padded - total,), dtype))
    return jnp.concatenate(flat).reshape(padded // LANES, LANES), sizes


def _unpack(flat, sizes, shapes):
    out, off = [], 0
    lead = flat.shape[:-1]
    for n, shp in zip(sizes, shapes):
        out.append(flat[..., off:off + n].reshape(lead + tuple(shp)))
        off += n
    return out


def rows_call(name, body, n_rows, tr, ins, outs, scratch=()):
    n_tiles = n_rows // tr
    assert n_tiles * tr == n_rows
    in_specs, arrays = [], []
    for arr, kind in ins:
        arrays.append(arr)
        if kind == "row":
            in_specs.append(pl.BlockSpec((tr, arr.shape[1]), lambda i: (i, 0)))
        elif kind == "full":
            in_specs.append(pl.BlockSpec(arr.shape, lambda i, nd=arr.ndim: (0,) * nd))
        elif kind[0] == "row":
            _, w, ci = kind
            in_specs.append(pl.BlockSpec((tr, w), lambda i, ci=ci: (i, ci)))
        elif kind[0] == "prev8":
            _, w, ci = kind
            in_specs.append(pl.BlockSpec((8, w), lambda i, ci=ci: (jnp.maximum(i * (tr // 8) - 1, 0), ci)))
        elif kind[0] == "next8":
            _, w, ci = kind
            last = n_rows // 8 - 1
            in_specs.append(pl.BlockSpec((8, w), lambda i, ci=ci, last=last: (jnp.minimum((i + 1) * (tr // 8), last), ci)))
        else:
            raise ValueError(kind)
    out_specs, out_shapes = [], []
    for shape, dtype, kind in outs:
        out_shapes.append(jax.ShapeDtypeStruct(shape, dtype))
        if kind == "row":
            out_specs.append(pl.BlockSpec((tr, shape[1]), lambda i: (i, 0)))
        else:
            out_specs.append(pl.BlockSpec(shape, lambda i, nd=len(shape): (0,) * nd))
    has_acc = any(k == "acc" for _, _, k in outs)
    return pl.pallas_call(
        body, name=name, grid=(n_tiles,), in_specs=in_specs, out_specs=out_specs, out_shape=out_shapes,
        scratch_shapes=list(scratch),
        compiler_params=_cparams(("arbitrary",) if has_acc else ("parallel",)),
    )(*arrays)


def _acc(ref, val):
    @pl.when(pl.program_id(0) == 0)
    def _():
        ref[...] = jnp.zeros_like(ref)
    ref[...] += val


def _rstd(x):
    return lax.rsqrt(jnp.mean(x * x, axis=-1, keepdims=True) + NORM_EPS)


def _sigmoid(x):
    return 1.0 / (1.0 + jnp.exp(-x))


def _silu(x):
    return x * _sigmoid(x)


def _dsilu(x):
    s = _sigmoid(x)
    return s * (1.0 + x * (1.0 - s))


def _softplus(x):
    return jnp.maximum(x, 0.0) + jnp.log(1.0 + jnp.exp(-jnp.abs(x)))


def _log_sigmoid_neg(x):
    t = -x
    return jnp.minimum(t, 0.0) - jnp.log(1.0 + jnp.exp(jnp.minimum(x, t)))


def normmod_fwd(x, g, scale, shift, name):
    t, d = x.shape

    def body(x_ref, g_ref, sc_ref, sh_ref, h_ref):
        xv = x_ref[...]
        h = xv * _rstd(xv) * g_ref[...] * (1.0 + sc_ref[...]) + sh_ref[...]
        h_ref[...] = h.astype(BF16)

    return rows_call(name, body, t, 512, [(x, "row"), (g, "full"), (scale, "full"), (shift, "full")],
                     [((t, d), BF16, "row")])[0]


def resid_normmod_fwd(x, f, gate, g_post, g_pre, scale, shift, name):
    t, d = x.shape

    def body(x_ref, f_ref, gate_ref, gp_ref, g_ref, sc_ref, sh_ref, xo_ref, h_ref):
        fv = f_ref[...]
        xn = x_ref[...] + gate_ref[...] * (fv * _rstd(fv) * gp_ref[...])
        xo_ref[...] = xn
        h = xn * _rstd(xn) * g_ref[...] * (1.0 + sc_ref[...]) + sh_ref[...]
        h_ref[...] = h.astype(BF16)

    return rows_call(name, body, t, 512,
                     [(x, "row"), (f, "row"), (gate, "full"), (g_post, "full"), (g_pre, "full"), (scale, "full"),
                      (shift, "full")],
                     [((t, d), F32, "row"), ((t, d), BF16, "row")])


def resid_loss(x, f, gate, g_post, target, name):
    t, d = x.shape

    def body(x_ref, f_ref, gate_ref, gp_ref, tg_ref, dy_ref, loss_ref):
        fv = f_ref[...]
        yv = x_ref[...] + gate_ref[...] * (fv * _rstd(fv) * gp_ref[...])
        err = yv - tg_ref[...]
        dy_ref[...] = err * (1.0 / d)
        part = 0.5 * jnp.sum(jnp.mean(err * err, axis=-1, keepdims=True), axis=0, keepdims=True)
        _acc(loss_ref, jnp.broadcast_to(part, loss_ref.shape))

    return rows_call(name, body, t, 512,
                     [(x, "row"), (f, "row"), (gate, "full"), (g_post, "full"), (target, "row")],
                     [((t, d), F32, "row"), ((8, LANES), F32, "acc")])


def resid_bwd(dx, f, gate, g_post, name):
    t, d = dx.shape

    def body(dx_ref, f_ref, gate_ref, gp_ref, df_ref, dgate_ref, dg_ref):
        fv, dxv, gp = f_ref[...], dx_ref[...], gp_ref[...]
        r = _rstd(fv)
        fn = fv * r
        _acc(dgate_ref, jnp.sum(dxv * (fn * gp), axis=0, keepdims=True))
        dn = dxv * gate_ref[...]
        _acc(dg_ref, jnp.sum(dn * fn, axis=0, keepdims=True))
        u = dn * gp
        df = r * (u - fn * jnp.mean(fn * u, axis=-1, keepdims=True))
        df_ref[...] = df.astype(BF16)

    return rows_call(name, body, t, 512, [(dx, "row"), (f, "row"), (gate, "full"), (g_post, "full")],
                     [((t, d), BF16, "row"), ((1, d), F32, "acc"), ((1, d), F32, "acc")])


def normmod_bwd(dh, x, dx_in, g, scale, name):
    t, d = x.shape

    def body(dh_ref, x_ref, dxi_ref, g_ref, sc_ref, dx_ref, dsc_ref, dsh_ref, dg_ref):
        xv, dhv, gv = x_ref[...], dh_ref[...], g_ref[...]
        r = _rstd(xv)
        xn = xv * r
        _acc(dsc_ref, jnp.sum(dhv * (xn * gv), axis=0, keepdims=True))
        _acc(dsh_ref, jnp.sum(dhv, axis=0, keepdims=True))
        dn = dhv * (1.0 + sc_ref[...])
        _acc(dg_ref, jnp.sum(dn * xn, axis=0, keepdims=True))
        u = dn * gv
        dx_ref[...] = dxi_ref[...] + r * (u - xn * jnp.mean(xn * u, axis=-1, keepdims=True))

    return rows_call(name, body, t, 512, [(dh, "row"), (x, "row"), (dx_in, "row"), (g, "full"), (scale, "full")],
                     [((t, d), F32, "row"), ((1, d), F32, "acc"), ((1, d), F32, "acc"), ((1, d), F32, "acc")])


def _pick(n, prefs):
    for p in prefs:
        if n % p == 0:
            return p
    return n


def mm_nn(a_list, b_list, out_dtype, name, tm=512, tn=None, tk=None):
    m, k = a_list[0].shape
    n = b_list[0].shape[1]
    tn = tn or _pick(n, (1024, 768, 512, 256))
    tk = tk or _pick(k, (1024, 1408, 512, 256))
    nk = k // tk
    npair = len(a_list)

    def body(*refs):
        a_refs, b_refs = refs[:npair], refs[npair:2 * npair]
        o_ref, acc = refs[2 * npair], refs[2 * npair + 1]
        kk = pl.program_id(2)

        @pl.when(kk == 0)
        def _():
            acc[...] = jnp.zeros_like(acc)

        s = acc[...]
        for a_ref, b_ref in zip(a_refs, b_refs):
            s = s + jnp.dot(a_ref[...], b_ref[...], preferred_element_type=F32)
        acc[...] = s

        @pl.when(kk == nk - 1)
        def _():
            o_ref[...] = acc[...].astype(o_ref.dtype)

    return pl.pallas_call(
        body, name=name, grid=(m // tm, n // tn, nk),
        in_specs=[pl.BlockSpec((tm, tk), lambda i, j, kk: (i, kk))] * npair
        + [pl.BlockSpec((tk, tn), lambda i, j, kk: (kk, j))] * npair,
        out_specs=pl.BlockSpec((tm, tn), lambda i, j, kk: (i, j)),
        out_shape=jax.ShapeDtypeStruct((m, n), out_dtype),
        scratch_shapes=[pltpu.VMEM((tm, tn), F32)],
        compiler_params=_cparams(("parallel", "parallel", "arbitrary")),
    )(*a_list, *b_list)


def mm_tn(a, b, name, tt=512):
    t, ka = a.shape
    n = b.shape[1]
    ta = _pick(ka, (1024, 1408, 512, 256))
    tn = _pick(n, (1024, 1408, 512, 256))
    nt = t // tt

    def body(a_ref, b_ref, o_ref):
        @pl.when(pl.program_id(2) == 0)
        def _():
            o_ref[...] = jnp.zeros_like(o_ref)

        o_ref[...] += lax.dot_general(a_ref[...], b_ref[...], (((0,), (0,)), ((), ())), preferred_element_type=F32)

    return pl.pallas_call(
        body, name=name, grid=(ka // ta, n // tn, nt),
        in_specs=[pl.BlockSpec((tt, ta), lambda i, j, s: (s, i)), pl.BlockSpec((tt, tn), lambda i, j, s: (s, j))],
        out_specs=pl.BlockSpec((ta, tn), lambda i, j, s: (i, j)),
        out_shape=jax.ShapeDtypeStruct((ka, n), F32),
        compiler_params=_cparams(("parallel", "parallel", "arbitrary")),
    )(a, b)


def mm_swiglu_fwd(h, w_ffn_in, name, tm=512, tn=1408):
    m, k = h.shape
    nh = FFN_HIDDEN // tn

    def body(h_ref, wg_ref, wu_ref, gt_ref, up_ref, a_ref):
        hv = h_ref[...]
        gt = jnp.dot(hv, wg_ref[...], preferred_element_type=F32)
        up = jnp.dot(hv, wu_ref[...], preferred_element_type=F32)
        gt_ref[...] = gt.astype(BF16)
        up_ref[...] = up.astype(BF16)
        a_ref[...] = (_silu(gt) * up).astype(BF16)

    shp = jax.ShapeDtypeStruct((m, FFN_HIDDEN), BF16)
    ospec = pl.BlockSpec((tm, tn), lambda i, j: (i, j))
    return pl.pallas_call(
        body, name=name, grid=(m // tm, nh),
        in_specs=[pl.BlockSpec((tm, k), lambda i, j: (i, 0)), pl.BlockSpec((k, tn), lambda i, j: (0, j)),
                  pl.BlockSpec((k, tn), lambda i, j: (0, j + nh))],
        out_specs=[ospec, ospec, ospec], out_shape=[shp, shp, shp],
        compiler_params=_cparams(("parallel", "parallel")),
    )(h, w_ffn_in, w_ffn_in)


def mm_swiglu_bwd(df, w_out_t, gt, up, name, tm=512, tn=1408):
    m, k = df.shape

    def body(df_ref, w_ref, gt_ref, up_ref, dgt_ref, dup_ref):
        da = jnp.dot(df_ref[...], w_ref[...], preferred_element_type=F32)
        gtv = gt_ref[...].astype(F32)
        upv = up_ref[...].astype(F32)
        dgt_ref[...] = (da * upv * _dsilu(gtv)).astype(BF16)
        dup_ref[...] = (da * _silu(gtv)).astype(BF16)

    shp = jax.ShapeDtypeStruct((m, FFN_HIDDEN), BF16)
    tile = pl.BlockSpec((tm, tn), lambda i, j: (i, j))
    return pl.pallas_call(
        body, name=name, grid=(m // tm, FFN_HIDDEN // tn),
        in_specs=[pl.BlockSpec((tm, k), lambda i, j: (i, 0)), pl.BlockSpec((k, tn), lambda i, j: (0, j)), tile, tile],
        out_specs=[tile, tile], out_shape=[shp, shp],
        compiler_params=_cparams(("parallel", "parallel")),
    )(df, w_out_t, gt, up)


def _shift_down(x, prev8, j):
    if j == 0:
        return x
    xr = pltpu.roll(x, j, 0)
    pr = pltpu.roll(prev8, j, 0)
    row = lax.broadcasted_iota(jnp.int32, (8, x.shape[1]), 0)
    head = jnp.where(row < j, pr, xr[:8])
    return head if x.shape[0] == 8 else jnp.concatenate([head, xr[8:]], axis=0)


def _shift_up(x, next8, j):
    if j == 0:
        return x
    n = x.shape[0]
    xr = pltpu.roll(x, n - j, 0)
    nr = pltpu.roll(next8, 8 - j, 0)
    row = lax.broadcasted_iota(jnp.int32, (8, x.shape[1]), 0)
    return jnp.concatenate([xr[:n - 8], jnp.where(row >= 8 - j, nr, xr[n - 8:])], axis=0)


def _conv_taps(x, prev8, w_ref, taps):
    out = None
    for k in range(taps):
        term = w_ref[k:k + 1, :] * _shift_down(x, prev8, taps - 1 - k)
        out = term if out is None else out + term
    return out


def post_inproj(p, sc_w, ssm_w, ssm_b, name, tr=512):
    t = p.shape[0]

    def body(sc_ref, scp_ref, qkv_ref, xbc_ref, xbcp_ref, scw_ref, sw_ref, sb_ref, ya_ref, qkvo_ref, act_ref):
        first = (pl.program_id(0) > 0).astype(F32)
        sc = sc_ref[...]
        scp = scp_ref[...] * first
        u = sc[:, 256:512] * sc[:, 512:768]
        up = scp[:, 256:512] * scp[:, 512:768]
        ya_ref[...] = (sc[:, 0:256] * _conv_taps(u, up, scw_ref, 3)).astype(BF16)
        qkv = qkv_ref[...]
        qkvo_ref[:, 0:256] = (qkv[:, 0:256] * 0.125).astype(BF16)
        qkvo_ref[:, 256:768] = qkv[:, 256:768].astype(BF16)
        xc = _conv_taps(xbc_ref[...], xbcp_ref[...] * first, sw_ref, 4) + sb_ref[...]
        act_ref[...] = _silu(xc)

    return rows_call(
        name, body, t, tr,
        [(p, ("row", 768, OFF_SC // 768)), (p, ("prev8", 768, OFF_SC // 768)), (p, ("row", 768, OFF_QKV // 768)),
         (p, ("row", 768, OFF_XBC // 768)), (p, ("prev8", 768, OFF_XBC // 768)),
         (sc_w, "full"), (ssm_w, "full"), (ssm_b, "full")],
        [((t, 256), BF16, "row"), ((t, 768), BF16, "row"), ((t, 768), F32, "row")])


def branch_out_fwd(ya, yb, yc, p, w_cat, name, tr=256):
    t = p.shape[0]

    def body(ya_ref, yb_ref, yc_ref, gl_ref, w_ref, o_ref):
        y_a = jnp.dot(ya_ref[...], w_ref[0:256, :], preferred_element_type=F32)
        y_b = jnp.dot(yb_ref[...].astype(BF16), w_ref[256:512, :], preferred_element_type=F32)
        y_c = jnp.dot(yc_ref[...], w_ref[512:1024, :], preferred_element_type=F32)
        m = (_sigmoid(gl_ref[:, 0:1024]) * y_a + _sigmoid(gl_ref[:, 1024:2048]) * y_b
             + _sigmoid(gl_ref[:, 2048:3072]) * y_c)
        o_ref[...] = m.astype(BF16)

    return rows_call(name, body, t, tr,
                     [(ya, "row"), (yb, "row"), (yc, "row"), (p, ("row", 3072, 0)), (w_cat, "full")],
                     [((t, D_MODEL), BF16, "row")])[0]


def branch_out_bwd(dm, ya, yb, yc, p, w_cat, w_cat_t, name, tr=256):
    t = p.shape[0]
    tn_dims = (((0,), (0,)), ((), ()))

    def body(dm_ref, ya_ref, yb_ref, yc_ref, gl_ref, w_ref, wt_ref, dgl_ref, dya_ref, dyb_ref, dyc_ref, dw_ref):
        @pl.when(pl.program_id(0) == 0)
        def _():
            dw_ref[...] = jnp.zeros_like(dw_ref)

        dmv = dm_ref[...]
        ins = (ya_ref[...], yb_ref[...].astype(BF16), yc_ref[...])
        rows = ((0, 256), (256, 512), (512, 1024))
        outs = (dya_ref, dyb_ref, dyc_ref)
        for i in range(3):
            r0, r1 = rows[i]
            y = jnp.dot(ins[i], w_ref[r0:r1, :], preferred_element_type=F32)
            s = _sigmoid(gl_ref[:, 1024 * i:1024 * (i + 1)])
            dgl_ref[:, 1024 * i:1024 * (i + 1)] = (dmv * y * s * (1.0 - s)).astype(BF16)
            dy = (dmv * s).astype(BF16)
            outs[i][...] = jnp.dot(dy, wt_ref[:, r0:r1], preferred_element_type=F32)
            dw_ref[r0:r1, :] += lax.dot_general(ins[i], dy, tn_dims, preferred_element_type=F32)

    return rows_call(name, body, t, tr,
                     [(dm, "row"), (ya, "row"), (yb, "row"), (yc, "row"), (p, ("row", 3072, 0)), (w_cat, "full"),
                      (w_cat_t, "full")],
                     [((t, 3072), BF16, "row"), ((t, 256), F32, "row"), ((t, 256), F32, "row"), ((t, 512), F32, "row"),
                      ((D_MODEL, D_MODEL), F32, "acc")])


def assemble_dp(dgl, dya, p, sc_w, dq, dk, dv, dact, ssm_w, ssm_b, ddt, dz, name, tr=256):
    t = p.shape[0]
    n_tiles = t // tr
    sci, xi = OFF_SC // 768, OFF_XBC // 768

    def body(dgl_ref, dya_ref, dyan_ref, sc_ref, scp_ref, scn_ref, scw_ref, dq_ref, dk_ref, dv_ref,
             dact_ref, dactn_ref, xbc_ref, xbcp_ref, xbcn_ref, sw_ref, sb_ref, ddt_ref, dz_ref,
             o_ref, dscw_ref, dsw_ref, dsb_ref):
        i = pl.program_id(0)

        @pl.when(i == 0)
        def _():
            dscw_ref[...] = jnp.zeros_like(dscw_ref)
            dsw_ref[...] = jnp.zeros_like(dsw_ref)
            dsb_ref[...] = jnp.zeros_like(dsb_ref)

        first = (i > 0).astype(F32)
        last = (i < n_tiles - 1).astype(F32)
        o_ref[:, 0:3072] = dgl_ref[...]
        sc = sc_ref[...]
        scp = scp_ref[...] * first
        scn = scn_ref[...] * last
        u = sc[:, 256:512] * sc[:, 512:768]
        up = scp[:, 256:512] * scp[:, 512:768]
        dya_v = dya_ref[...]
        cv = _conv_taps(u, up, scw_ref, 3)
        o_ref[:, OFF_SC:OFF_SC + 256] = (dya_v * cv).astype(BF16)
        dcv = dya_v * sc[:, 0:256]
        dcvn = dyan_ref[...] * last * scn[:, 0:256]
        du = None
        for k in range(3):
            sh = 2 - k
            term = scw_ref[k:k + 1, :] * _shift_up(dcv, dcvn, sh)
            du = term if du is None else du + term
            dscw_ref[k:k + 1, :] += jnp.sum(dcv * _shift_down(u, up, sh), axis=0, keepdims=True)
        o_ref[:, OFF_SC + 256:OFF_SC + 512] = (du * sc[:, 512:768]).astype(BF16)
        o_ref[:, OFF_SC + 512:OFF_SC + 768] = (du * sc[:, 256:512]).astype(BF16)
        o_ref[:, OFF_QKV:OFF_QKV + 256] = (dq_ref[...] * 0.125).astype(BF16)
        o_ref[:, OFF_QKV + 256:OFF_QKV + 512] = dk_ref[...].astype(BF16)
        o_ref[:, OFF_QKV + 512:OFF_QKV + 768] = dv_ref[...].astype(BF16)
        xb = xbc_ref[...]
        xbp = xbcp_ref[...] * first
        xbn = xbcn_ref[...]
        xc = _conv_taps(xb, xbp, sw_ref, 4) + sb_ref[...]
        xcn = _conv_taps(xbn, xb[tr - 8:, :], sw_ref, 4) + sb_ref[...]
        dxc = dact_ref[...] * _dsilu(xc)
        dxcn = dactn_ref[...] * _dsilu(xcn) * last
        dxb = None
        for k in range(4):
            sh = 3 - k
            term = sw_ref[k:k + 1, :] * _shift_up(dxc, dxcn, sh)
            dxb = term if dxb is None else dxb + term
            dsw_ref[k:k + 1, :] += jnp.sum(dxc * _shift_down(xb, xbp, sh), axis=0, keepdims=True)
        dsb_ref[...] += jnp.sum(dxc, axis=0, keepdims=True)
        o_ref[:, OFF_XBC:OFF_XBC + 768] = dxb.astype(BF16)
        o_ref[:, OFF_DT:OFF_DT + 128] = ddt_ref[...].astype(BF16)
        o_ref[:, OFF_DT + 128:OFF_Z] = jnp.zeros((tr, OFF_Z - OFF_DT - 128), BF16)
        o_ref[:, OFF_Z:IN_PAD] = dz_ref[...].astype(BF16)

    return rows_call(
        name, body, t, tr,
        [(dgl, "row"), (dya, "row"), (dya, ("next8", 256, 0)),
         (p, ("row", 768, sci)), (p, ("prev8", 768, sci)), (p, ("next8", 768, sci)), (sc_w, "full"),
         (dq, "row"), (dk, "row"), (dv, "row"),
         (dact, "row"), (dact, ("next8", 768, 0)),
         (p, ("row", 768, xi)), (p, ("prev8", 768, xi)), (p, ("next8", 768, xi)), (ssm_w, "full"), (ssm_b, "full"),
         (ddt, "row"), (dz, "row")],
        [((t, IN_PAD), BF16, "row"), ((8, 256), F32, "acc"), ((8, 768), F32, "acc"), ((1, 768), F32, "acc")])


def adamw_flat(slots, w, m, v, name, tr=512):
    n_slots, rows, lanes = slots.shape
    tr = min(tr, rows)
    bc1 = 1.0 - ADAM_B1 ** ADAM_STEP
    bc2 = 1.0 - ADAM_B2 ** ADAM_STEP

    def body(s_ref, w_ref, m_ref, v_ref, g_ref, d_ref, mo_ref, vo_ref):
        g = s_ref[0].astype(F32)
        for k in range(1, n_slots):
            g = g + s_ref[k].astype(F32)
        mn = ADAM_B1 * m_ref[...] + (1.0 - ADAM_B1) * g
        vn = ADAM_B2 * v_ref[...] + (1.0 - ADAM_B2) * (g * g)
        m_hat = mn / bc1
        v_hat = vn / bc2
        g_ref[...] = g
        d_ref[...] = -ADAM_LR * (m_hat / (jnp.sqrt(v_hat) + ADAM_EPS) + ADAM_WD * w_ref[...])
        mo_ref[...] = mn
        vo_ref[...] = vn

    tile = pl.BlockSpec((tr, lanes), lambda i: (i, 0))
    shp = jax.ShapeDtypeStruct((rows, lanes), F32)
    return pl.pallas_call(
        body, name=name, grid=(rows // tr,),
        in_specs=[pl.BlockSpec((n_slots, tr, lanes), lambda i: (0, i, 0)), tile, tile, tile],
        out_specs=[tile] * 4, out_shape=[shp] * 4,
        compiler_params=_cparams(("parallel",)),
    )(slots, w, m, v)


def _split_dot(x, tri):
    hi = x.astype(BF16)
    lo = (x - hi.astype(F32)).astype(BF16)
    return jnp.dot(hi, tri, preferred_element_type=F32) + jnp.dot(lo, tri, preferred_element_type=F32)


_NT = (((1,), (1,)), ((), ()))
_TN = (((0,), (0,)), ((), ()))

SBA_EXP_ZERO = -104.0
SBA_SKIPPED = -1e30


def sba_fwd(qkv, name, bq=256, bk=256):
    t = qkv.shape[0]
    ratio = bq // bk
    assert t // bk <= LANES

    def body(q_ref, k_ref, v_ref, o_ref, runs_ref, acc_s, run_s):
        i = pl.program_id(1)
        lane = lax.broadcasted_iota(jnp.int32, (1, LANES), 1)
        lane_q = lax.broadcasted_iota(jnp.int32, (bq, LANES), 1)
        qi = lax.broadcasted_iota(jnp.int32, (bq, bk), 0) + i * bq
        kj = lax.broadcasted_iota(jnp.int32, (bq, bk), 1)
        later = (lax.broadcasted_iota(jnp.int32, (bk, bk), 0) > lax.broadcasted_iota(jnp.int32, (bk, bk), 1)).astype(BF16)
        qv = q_ref[...]
        qms = [jnp.where(hm, qv, jnp.zeros_like(qv)) for hm in (lane < 64, lane >= 64)]
        acc_s[...] = jnp.zeros_like(acc_s)
        run_s[...] = jnp.zeros_like(run_s)
        runs_ref[...] = jnp.full(runs_ref.shape, SBA_SKIPPED, F32)

        def tile(j, masked):
            start = pl.multiple_of(j * bk, bk)
            kb = k_ref[pl.ds(start, bk), :]
            vb = v_ref[pl.ds(start, bk), :]
            heads = range(2)
            mask = (kj + j * bk) < qi if masked else None
            s = [lax.dot_general(qms[hh], kb, _NT, preferred_element_type=F32) for hh in heads]
            lk = [_log_sigmoid_neg(s[hh]) for hh in heads]
            if masked:
                lk = [jnp.where(mask, lk[hh], 0.0) for hh in heads]
            w = [jnp.dot(lk[hh].astype(BF16), later, preferred_element_type=F32) for hh in heads]
            run = [run_s[hh] for hh in heads]
            a = [jnp.exp(s[hh] + lk[hh] + w[hh] + run[hh]) for hh in heads]
            if masked:
                a = [jnp.where(mask, a[hh], 0.0) for hh in heads]
            top = None
            for hh in heads:
                acc_s[hh] += jnp.dot(a[hh].astype(BF16), vb, preferred_element_type=F32)
                runs_ref[hh] = jnp.where(lane_q == j, run[hh], runs_ref[hh])
                new_run = run[hh] + jnp.sum(lk[hh], axis=1, keepdims=True)
                run_s[hh] = new_run
                top = jnp.max(new_run) if top is None else jnp.maximum(top, jnp.max(new_run))
            return top

        n_kb = (i + 1) * ratio
        top = None
        for d in range(ratio):
            top = tile(n_kb - 1 - d, True)

        def cond(state):
            n, live = state
            return jnp.logical_and(n < i * ratio, live)

        def step(state):
            n, _ = state
            return n + 1, tile(i * ratio - 1 - n, False) >= SBA_EXP_ZERO

        lax.while_loop(cond, step, (jnp.int32(0), top >= SBA_EXP_ZERO))
        o_ref[...] = jnp.where(lane < 64, acc_s[0], acc_s[1])

    return pl.pallas_call(
        body, name=name, grid=(2, t // bq),
        in_specs=[pl.BlockSpec((bq, LANES), lambda p, i: (i, p)), pl.BlockSpec((t, LANES), lambda p, i: (0, 2 + p)),
                  pl.BlockSpec((t, LANES), lambda p, i: (0, 4 + p))],
        out_specs=[pl.BlockSpec((bq, LANES), lambda p, i: (i, p)), pl.BlockSpec((2, bq, LANES), lambda p, i: (p, i, 0))],
        out_shape=[jax.ShapeDtypeStruct((t, SB_WIDTH), F32), jax.ShapeDtypeStruct((4, t, LANES), F32)],
        scratch_shapes=[pltpu.VMEM((2, bq, LANES), F32), pltpu.VMEM((2, bq, 1), F32)],
        compiler_params=_cparams(("parallel", "parallel")),
    )(qkv, qkv, qkv)


def sba_bwd(qkv, runs, do, name, bq=256, bk=256):
    t = qkv.shape[0]
    ratio = bq // bk
    nq = t // bq

    def body(q_ref, k_ref, v_ref, runs_ref, do_ref, dq_ref, dk_hbm, dv_hbm, dk_s, dv_s, sem, dq_s, rg_s):
        p = pl.program_id(0)
        i = pl.program_id(1)

        @pl.when(i == 0)
        def _():
            dk_s[...] = jnp.zeros_like(dk_s)
            dv_s[...] = jnp.zeros_like(dv_s)

        lane = lax.broadcasted_iota(jnp.int32, (1, LANES), 1)
        qi = lax.broadcasted_iota(jnp.int32, (bq, bk), 0) + i * bq
        kj = lax.broadcasted_iota(jnp.int32, (bq, bk), 1)
        r2 = lax.broadcasted_iota(jnp.int32, (bk, bk), 0)
        c2 = lax.broadcasted_iota(jnp.int32, (bk, bk), 1)
        later = (r2 > c2).astype(BF16)
        earlier = (r2 < c2).astype(BF16)
        qv = q_ref[...]
        dov = do_ref[...]
        heads = range(2)
        hms = (lane < 64, lane >= 64)
        qms = [jnp.where(hm, qv, jnp.zeros_like(qv)) for hm in hms]
        doms = [jnp.where(hm, dov, 0.0).astype(BF16) for hm in hms]
        runs = [runs_ref[hh] for hh in heads]
        dq_s[...] = jnp.zeros_like(dq_s)
        rg_s[...] = jnp.zeros_like(rg_s)

        def tile(j, masked):
            start = pl.multiple_of(j * bk, bk)
            kb = k_ref[pl.ds(start, bk), :]
            vb = v_ref[pl.ds(start, bk), :]
            mask = (kj + j * bk) < qi if masked else None
            s = [lax.dot_general(qms[hh], kb, _NT, preferred_element_type=F32) for hh in heads]
            da = [lax.dot_general(doms[hh], vb, _NT, preferred_element_type=F32) for hh in heads]
            lk_raw = [_log_sigmoid_neg(s[hh]) for hh in heads]
            lk = [jnp.where(mask, lk_raw[hh], 0.0) for hh in heads] if masked else lk_raw
            w = [jnp.dot(lk[hh].astype(BF16), later, preferred_element_type=F32) for hh in heads]
            run = [jnp.sum(jnp.where(lane == j, runs[hh], 0.0), axis=1, keepdims=True) for hh in heads]
            a = [jnp.exp(s[hh] + lk[hh] + w[hh] + run[hh]) for hh in heads]
            if masked:
                a = [jnp.where(mask, a[hh], 0.0) for hh in heads]
            g = [a[hh] * da[hh] for hh in heads]
            rg = [rg_s[hh] for hh in heads]
            c = [rg[hh] + _split_dot(g[hh], earlier) for hh in heads]
            dz = [g[hh] - jnp.exp(s[hh] + lk_raw[hh]) * (g[hh] + c[hh]) for hh in heads]
            if masked:
                dz = [jnp.where(mask, dz[hh], 0.0) for hh in heads]
            dz = [dz[hh].astype(BF16) for hh in heads]
            for hh in heads:
                dq_s[hh] += jnp.dot(dz[hh], kb, preferred_element_type=F32)
                rg_s[hh] = rg[hh] + jnp.sum(g[hh], axis=1, keepdims=True)
            dk = lax.dot_general(dz[0], qms[0], _TN, preferred_element_type=F32)
            dk_s[pl.ds(start, bk), :] += dk + lax.dot_general(dz[1], qms[1], _TN, preferred_element_type=F32)
            dv = lax.dot_general(a[0].astype(BF16), doms[0], _TN, preferred_element_type=F32)
            dv_s[pl.ds(start, bk), :] += dv + lax.dot_general(a[1].astype(BF16), doms[1], _TN, preferred_element_type=F32)

        live = jnp.maximum(jnp.max(runs[0], axis=0, keepdims=True), jnp.max(runs[1], axis=0, keepdims=True)) >= SBA_EXP_ZERO
        first = jnp.minimum(jnp.min(jnp.where(live, lane, LANES)), i * ratio)

        def step(j, carry):
            tile(j, False)
            return carry

        lax.fori_loop(first, i * ratio, step, 0)
        for d in range(ratio):
            tile(i * ratio + d, True)
        dq_ref[...] = jnp.where(lane < 64, dq_s[0], dq_s[1])

        @pl.when(i == nq - 1)
        def _():
            col = pl.multiple_of(p * LANES, LANES)
            ck = pltpu.make_async_copy(dk_s, dk_hbm.at[:, pl.ds(col, LANES)], sem.at[0])
            cv = pltpu.make_async_copy(dv_s, dv_hbm.at[:, pl.ds(col, LANES)], sem.at[1])
            ck.start()
            cv.start()
            ck.wait()
            cv.wait()

    shp = jax.ShapeDtypeStruct((t, SB_WIDTH), F32)
    tile = pl.BlockSpec((bq, LANES), lambda p, i: (i, p))
    return pl.pallas_call(
        body, name=name, grid=(2, nq),
        in_specs=[tile, pl.BlockSpec((t, LANES), lambda p, i: (0, 2 + p)), pl.BlockSpec((t, LANES), lambda p, i: (0, 4 + p)),
                  pl.BlockSpec((2, bq, LANES), lambda p, i: (p, i, 0)), tile],
        out_specs=[tile, pl.BlockSpec(memory_space=pl.ANY), pl.BlockSpec(memory_space=pl.ANY)],
        out_shape=[shp, shp, shp],
        scratch_shapes=[pltpu.VMEM((t, LANES), F32), pltpu.VMEM((t, LANES), F32), pltpu.SemaphoreType.DMA((2,)),
                        pltpu.VMEM((2, bq, LANES), F32), pltpu.VMEM((2, bq, 1), F32)],
        compiler_params=_cparams(("arbitrary", "arbitrary")),
    )(qkv, qkv, qkv, runs, do)


def _ssd_consts():
    ln = SSM_CHUNK
    ri = lax.broadcasted_iota(jnp.int32, (ln, ln), 0)
    ci = lax.broadcasted_iota(jnp.int32, (ln, ln), 1)
    eh = lax.broadcasted_iota(jnp.int32, (LANES, SSM_INNER), 0)
    el = lax.broadcasted_iota(jnp.int32, (LANES, SSM_INNER), 1)
    expand = (jnp.right_shift(el, 6) == eh).astype(F32)
    th = lax.broadcasted_iota(jnp.int32, (SSM_INNER, LANES), 1)
    tl = lax.broadcasted_iota(jnp.int32, (SSM_INNER, LANES), 0)
    reduce = (jnp.right_shift(tl, 6) == th).astype(F32)
    return ri, ci, expand, reduce


def _dot_hi(a, b):
    return jnp.dot(a, b, precision=HI, preferred_element_type=F32)


def _ssd_prelude(xbc_ref, dt_ref, dtt_ref, hpr_ref, hpc_ref, ri, ci, expand):
    ln = SSM_CHUNK
    xs = xbc_ref[:, 0:512]
    bm = xbc_ref[:, 512:640]
    cm = xbc_ref[:, 640:768]
    dtb_r = hpr_ref[0:1, :]
    aneg_r = -jnp.exp(hpr_ref[1:2, :])
    pre = dt_ref[...] + dtb_r
    dt = _softplus(pre)
    a = dt * aneg_r
    dtt = _softplus(dtt_ref[...] + hpc_ref[0:8, :])
    att = dtt * (-jnp.exp(hpc_ref[8:16, :]))
    tril = (ri >= ci).astype(F32)
    triu = (ri <= ci).astype(F32)
    acs = _dot_hi(tril, a)
    acst = _dot_hi(att, triu)
    acs_e = _dot_hi(acs, expand)
    dt_e = _dot_hi(dt, expand)
    last_e = acs_e[ln - 1:ln, :]
    e_e = jnp.exp(acs_e)
    w_e = jnp.exp(last_e - acs_e)
    dec_e = jnp.exp(last_e)
    xdt = xs * dt_e
    return dict(xs=xs, bm=bm, cm=cm, pre=pre, dt=dt, aneg_r=aneg_r, acs=acs, acst=acst, dt_e=dt_e, e_e=e_e,
                w_e=w_e, dec_e=dec_e, xdt=xdt, triu=triu)


def ssd_fwd(act, p, dtt, hp_rows, hp_cols, d_e, norm_w, name):
    t = act.shape[0]
    ln = SSM_CHUNK
    nc = t // ln

    def body(xbc_ref, dt_ref, z_ref, dtt_ref, hpr_ref, hpc_ref, d_ref, nw_ref, yc_ref, y_ref, sto_ref, st):
        @pl.when(pl.program_id(0) == 0)
        def _():
            st[...] = jnp.zeros_like(st)

        ri, ci, expand, _ = _ssd_consts()
        q = _ssd_prelude(xbc_ref, dt_ref, dtt_ref, hpr_ref, hpc_ref, ri, ci, expand)
        lane = lax.broadcasted_iota(jnp.int32, (1, LANES), 1)
        rown = lax.broadcasted_iota(jnp.int32, (LANES, 1), 0)
        low = lane < 64
        mask = ri >= ci
        xdt_b = q["xdt"].astype(BF16)
        xw_b = (q["xdt"] * q["w_e"]).astype(BF16)
        bt = q["bm"].T.astype(BF16)
        cb_ = q["cm"].astype(BF16)
        y_pairs = []
        for g in range(2):
            gm = low if g == 0 else jnp.logical_not(low)
            rm = (rown < 64) if g == 0 else (rown >= 64)
            cg = jnp.where(gm, cb_, jnp.zeros_like(cb_))
            cb = jnp.dot(cg, bt, preferred_element_type=F32)
            for pp in range(2):
                pi = 2 * g + pp
                sl = slice(LANES * pi, LANES * (pi + 1))
                xp = xdt_b[:, sl]
                yd = []
                for hh in range(2):
                    h = 2 * pi + hh
                    diff = q["acs"][:, h:h + 1] - q["acst"][h:h + 1, :]
                    lam = jnp.exp(jnp.where(mask, diff, -jnp.inf))
                    yd.append(jnp.dot((cb * lam).astype(BF16), xp, preferred_element_type=F32))
                sp = st[pi]
                sto_ref[0, pi] = sp
                yoff = jnp.dot(cg, sp.astype(BF16), preferred_element_type=F32) * q["e_e"][:, sl]
                upd = jnp.dot(bt, xw_b[:, sl], preferred_element_type=F32)
                st[pi] = q["dec_e"][:, sl] * sp + jnp.where(rm, upd, 0.0)
                y_pairs.append(jnp.where(low, yd[0], yd[1]) + yoff)
        y = jnp.concatenate(y_pairs, axis=1) + q["xs"] * d_ref[...]
        y_ref[...] = y
        yg = y * _silu(z_ref[...])
        for g in range(2):
            sl = slice(256 * g, 256 * (g + 1))
            seg = yg[:, sl]
            yc_ref[:, sl] = (seg * _rstd(seg) * nw_ref[:, sl]).astype(BF16)

    return pl.pallas_call(
        body, name=name, grid=(nc,),
        in_specs=[pl.BlockSpec((ln, 768), lambda c: (c, 0)), pl.BlockSpec((ln, LANES), lambda c: (c, OFF_DT // LANES)),
                  pl.BlockSpec((ln, 512), lambda c: (c, OFF_Z // 512)), pl.BlockSpec((8, ln), lambda c: (0, c)),
                  pl.BlockSpec((8, LANES), lambda c: (0, 0)), pl.BlockSpec((16, ln), lambda c: (0, 0)),
                  pl.BlockSpec((1, 512), lambda c: (0, 0)), pl.BlockSpec((1, 512), lambda c: (0, 0))],
        out_specs=[pl.BlockSpec((ln, 512), lambda c: (c, 0)), pl.BlockSpec((ln, 512), lambda c: (c, 0)),
                   pl.BlockSpec((1, 4, LANES, LANES), lambda c: (c, 0, 0, 0))],
        out_shape=[jax.ShapeDtypeStruct((t, 512), BF16), jax.ShapeDtypeStruct((t, 512), F32),
                   jax.ShapeDtypeStruct((nc, 4, LANES, LANES), F32)],
        scratch_shapes=[pltpu.VMEM((4, LANES, LANES), F32)],
        compiler_params=_cparams(("arbitrary",)),
    )(act, p, p, dtt, hp_rows, hp_cols, d_e, norm_w)


def ssd_bwd(dyc, y, act, p, dtt, states, hp_rows, hp_cols, d_e, norm_w, name):
    t = act.shape[0]
    ln = SSM_CHUNK
    nc = t // ln

    def body(dyc_ref, y_ref, xbc_ref, dt_ref, z_ref, dtt_ref, st_ref, hpr_ref, hpc_ref, d_ref, nw_ref,
             dz_ref, dact_ref, ddt_ref, dnw_ref, dd_ref, dhp_ref, ds):
        @pl.when(pl.program_id(0) == 0)
        def _():
            ds[...] = jnp.zeros_like(ds)
            dnw_ref[...] = jnp.zeros_like(dnw_ref)
            dd_ref[...] = jnp.zeros_like(dd_ref)
            dhp_ref[...] = jnp.zeros_like(dhp_ref)

        ri, ci, expand, reduce = _ssd_consts()
        q = _ssd_prelude(xbc_ref, dt_ref, dtt_ref, hpr_ref, hpc_ref, ri, ci, expand)
        lane = lax.broadcasted_iota(jnp.int32, (1, LANES), 1)
        rown = lax.broadcasted_iota(jnp.int32, (LANES, 1), 0)
        low = lane < 64
        mask = ri >= ci
        mask_t = ci >= ri
        xs, xdt, acs, acst = q["xs"], q["xdt"], q["acs"], q["acst"]
        yv, zv, nw = y_ref[...], z_ref[...], nw_ref[...]
        sg = _sigmoid(zv)
        zz = zv * sg
        yg = yv * zz
        dycv = dyc_ref[...]
        u = dycv * nw
        dyg_parts, dnw_parts = [], []
        for g in range(2):
            sl = slice(256 * g, 256 * (g + 1))
            seg = yg[:, sl]
            rr = _rstd(seg)
            nrm = seg * rr
            dyg_parts.append(rr * (u[:, sl] - nrm * jnp.mean(nrm * u[:, sl], axis=-1, keepdims=True)))
            dnw_parts.append(jnp.sum(dycv[:, sl] * nrm, axis=0, keepdims=True))
        dyg = jnp.concatenate(dyg_parts, axis=1)
        dnw_ref[...] += jnp.concatenate(dnw_parts, axis=1)
        dy = dyg * zz
        dz_ref[...] = dyg * yv * (sg * (1.0 + zv * (1.0 - sg)))
        dd_ref[...] += jnp.sum(dy * xs, axis=0, keepdims=True)
        dxs = dy * d_ref[...]
        dy_b = dy.astype(BF16)
        xdt_b = xdt.astype(BF16)
        xw_b = (xdt * q["w_e"]).astype(BF16)
        bt = q["bm"].T.astype(BF16)
        ct = q["cm"].T.astype(BF16)
        cb_ = q["cm"].astype(BF16)
        bb_ = q["bm"].astype(BF16)
        dacs = jnp.zeros((ln, LANES), F32)
        dc = jnp.zeros((ln, LANES), F32)
        db = jnp.zeros((ln, LANES), F32)
        dxdt_pairs, yoffdy_pairs, dwe_pairs, ddec_pairs = [], [], [], []
        for g in range(2):
            gm = low if g == 0 else jnp.logical_not(low)
            rm = (rown < 64) if g == 0 else (rown >= 64)
            cg = jnp.where(gm, cb_, jnp.zeros_like(cb_))
            bg = jnp.where(gm, bb_, jnp.zeros_like(bb_))
            cb = jnp.dot(cg, bt, preferred_element_type=F32)
            cbt = jnp.dot(bg, ct, preferred_element_type=F32)
            dcb = jnp.zeros((ln, ln), F32)
            dcbt = jnp.zeros((ln, ln), F32)
            for pp in range(2):
                pi = 2 * g + pp
                sl = slice(LANES * pi, LANES * (pi + 1))
                xp = xdt_b[:, sl]
                dyp = dy_b[:, sl]
                xpt = xdt[:, sl].T.astype(BF16)
                dypt = dy[:, sl].T.astype(BF16)
                dxdt_p = jnp.zeros((ln, LANES), F32)
                for hh in range(2):
                    h = 2 * pi + hh
                    hm = low if hh == 0 else jnp.logical_not(low)
                    col = acs[:, h:h + 1]
                    row = acst[h:h + 1, :]
                    lam = jnp.exp(jnp.where(mask, col - row, -jnp.inf))
                    lam_t = jnp.exp(jnp.where(mask_t, row - col, -jnp.inf))
                    m = cb * lam
                    m_t = cbt * lam_t
                    dyh = jnp.where(hm, dyp, jnp.zeros_like(dyp))
                    xh = jnp.where(hm, xp, jnp.zeros_like(xp))
                    dm = jnp.dot(dyh, xpt, preferred_element_type=F32)
                    dm_t = jnp.dot(xh, dypt, preferred_element_type=F32)
                    dcb = dcb + dm * lam
                    dcbt = dcbt + dm_t * lam_t
                    rs = jnp.sum(dm * m, axis=1, keepdims=True) - jnp.sum(dm_t * m_t, axis=1, keepdims=True)
                    dacs = dacs + jnp.where(lane == h, rs, 0.0)
                    dxdt_p = dxdt_p + jnp.dot(m_t.astype(BF16), dyh, preferred_element_type=F32)
                sp = st_ref[0, pi]
                sp_b = sp.astype(BF16)
                dsn = ds[pi]
                dsn_b = dsn.astype(BF16)
                e_p, w_p, dec_p = q["e_e"][:, sl], q["w_e"][:, sl], q["dec_e"][:, sl]
                yoff = jnp.dot(cg, sp_b, preferred_element_type=F32) * e_p
                dyo = dy[:, sl] * e_p
                dyo_b = dyo.astype(BF16)
                dc = dc + lax.dot_general(dyo_b, sp_b, _NT, preferred_element_type=F32)
                ds_prev = dec_p * dsn + jnp.where(rm, jnp.dot(ct, dyo_b, preferred_element_type=F32), 0.0)
                yoffdy_pairs.append(dy[:, sl] * yoff)
                dxw = jnp.dot(bg, dsn_b, preferred_element_type=F32)
                db = db + lax.dot_general(xw_b[:, sl], dsn_b, _NT, preferred_element_type=F32)
                dxdt_p = dxdt_p + dxw * w_p
                dwe_pairs.append(dxw * xdt[:, sl])
                ddec_pairs.append(jnp.sum(dsn * sp, axis=0, keepdims=True))
                ds[pi] = ds_prev
                dxdt_pairs.append(dxdt_p)
            dc = dc + jnp.dot(dcb.astype(BF16), bg, preferred_element_type=F32)
            db = db + jnp.dot(dcbt.astype(BF16), cg, preferred_element_type=F32)
        dxdt = jnp.concatenate(dxdt_pairs, axis=1)
        yoffdy = jnp.concatenate(yoffdy_pairs, axis=1)
        dwe = jnp.concatenate(dwe_pairs, axis=1)
        ddec_e = jnp.broadcast_to(jnp.concatenate(ddec_pairs, axis=1), (8, SSM_INNER))
        last = acs[ln - 1:ln, :]
        w_col = jnp.exp(last - acs)
        dw_col = _dot_hi(dwe, reduce) * w_col
        dacs = dacs + _dot_hi(yoffdy, reduce) - dw_col
        dlast = jnp.sum(dw_col, axis=0, keepdims=True) + jnp.exp(last) * _dot_hi(ddec_e, reduce)[0:1, :]
        rowi = lax.broadcasted_iota(jnp.int32, (ln, 1), 0)
        dacs = dacs + jnp.where(rowi == ln - 1, dlast, 0.0)
        da = _dot_hi(q["triu"], dacs)
        ddt = da * q["aneg_r"] + _dot_hi(dxdt * xs, reduce)
        ddt_raw = jnp.where(lane < SSM_HEADS, ddt * _sigmoid(q["pre"]), 0.0)
        ddt_ref[...] = ddt_raw
        dhp_ref[0:1, :] += jnp.sum(ddt_raw, axis=0, keepdims=True)
        dhp_ref[1:2, :] += jnp.where(lane < SSM_HEADS, jnp.sum(da * q["dt"], axis=0, keepdims=True) * q["aneg_r"], 0.0)
        dact_ref[:, 0:512] = dxs + dxdt * q["dt_e"]
        dact_ref[:, 512:640] = db
        dact_ref[:, 640:768] = dc

    rev = lambda c: nc - 1 - c
    return pl.pallas_call(
        body, name=name, grid=(nc,),
        in_specs=[pl.BlockSpec((ln, 512), lambda c: (rev(c), 0)), pl.BlockSpec((ln, 512), lambda c: (rev(c), 0)),
                  pl.BlockSpec((ln, 768), lambda c: (rev(c), 0)),
                  pl.BlockSpec((ln, LANES), lambda c: (rev(c), OFF_DT // LANES)),
                  pl.BlockSpec((ln, 512), lambda c: (rev(c), OFF_Z // 512)), pl.BlockSpec((8, ln), lambda c: (0, rev(c))),
                  pl.BlockSpec((1, 4, LANES, LANES), lambda c: (rev(c), 0, 0, 0)),
                  pl.BlockSpec((8, LANES), lambda c: (0, 0)), pl.BlockSpec((16, ln), lambda c: (0, 0)),
                  pl.BlockSpec((1, 512), lambda c: (0, 0)), pl.BlockSpec((1, 512), lambda c: (0, 0))],
        out_specs=[pl.BlockSpec((ln, 512), lambda c: (rev(c), 0)), pl.BlockSpec((ln, 768), lambda c: (rev(c), 0)),
                   pl.BlockSpec((ln, LANES), lambda c: (rev(c), 0)), pl.BlockSpec((1, 512), lambda c: (0, 0)),
                   pl.BlockSpec((1, 512), lambda c: (0, 0)), pl.BlockSpec((8, LANES), lambda c: (0, 0))],
        out_shape=[jax.ShapeDtypeStruct((t, 512), F32), jax.ShapeDtypeStruct((t, 768), F32),
                   jax.ShapeDtypeStruct((t, LANES), F32), jax.ShapeDtypeStruct((1, 512), F32),
                   jax.ShapeDtypeStruct((1, 512), F32), jax.ShapeDtypeStruct((8, LANES), F32)],
        scratch_shapes=[pltpu.VMEM((4, LANES, LANES), F32)],
        compiler_params=_cparams(("arbitrary",)),
    )(dyc, y, act, p, p, dtt, states, hp_rows, hp_cols, d_e, norm_w)


def mod_shard_fwd(c_all, mod_w, mod_b_shard, name):
    def body(c_ref, w_ref, b_ref, o_ref):
        sc = _silu(c_ref[...])
        for l in range(DEPTH):
            o_ref[l] = _dot_hi(sc, w_ref[l]) + b_ref[l]

    return pl.pallas_call(body, name=name, out_shape=jax.ShapeDtypeStruct((DEPTH, N_DEV, mod_w.shape[2]), F32),
                          compiler_params=_cparams())(c_all, mod_w, mod_b_shard)


def mod_w_grad(c_all, dmod_shard, name):
    def body(c_ref, d_ref, o_ref):
        sc = _silu(c_ref[...])
        for l in range(DEPTH):
            o_ref[l] = lax.dot_general(sc, d_ref[l], _TN, precision=HI, preferred_element_type=F32)

    return pl.pallas_call(body, name=name, out_shape=jax.ShapeDtypeStruct((DEPTH, D_MODEL, dmod_shard.shape[2]), F32),
                          compiler_params=_cparams())(c_all, dmod_shard)


_BIG = ("w_in", "sc_conv_w", "ssm_conv_w", "w_sc_out", "w_sb_out", "w_ssm_out", "w_o", "w_ffn_in", "w_ffn_out")
_ROW_SHARDED = ("w_o", "w_ffn_out")
_SMALL = ("mod_b", "g_pre_mix", "g_post_mix", "g_pre_ffn", "g_post_ffn", "ssm_conv_b", "ssm_dt_bias", "ssm_a_log",
          "ssm_d", "ssm_norm_w")
_WEIGHTS = ("mod_w", "mod_b", "g_pre_mix", "g_post_mix", "g_pre_ffn", "g_post_ffn", "w_in", "sc_conv_w", "ssm_conv_w",
            "ssm_conv_b", "ssm_dt_bias", "ssm_a_log", "ssm_d", "ssm_norm_w", "w_sc_out", "w_sb_out", "w_ssm_out", "w_o",
            "w_ffn_in", "w_ffn_out")


def _gathered_to_full(g, row_sharded):
    _, dep, r, c = g.shape
    if row_sharded:
        return g.transpose(1, 0, 2, 3).reshape(dep, N_DEV * r, c)
    return g.transpose(1, 2, 0, 3).reshape(dep, r, N_DEV * c)


def _full_to_slots(w, row_sharded):
    dep, r, c = w.shape
    if row_sharded:
        return w.reshape(dep, N_DEV, r // N_DEV, c).transpose(1, 0, 2, 3).reshape(N_DEV, -1)
    return w.reshape(dep, r, N_DEV, c // N_DEV).transpose(2, 0, 1, 3).reshape(N_DEV, -1)


def _pad_in_proj(w):
    sc, qkv, z, xbc, dt, gates = (w[:, 0:768], w[:, 768:1536], w[:, 1536:2048], w[:, 2048:2816], w[:, 2816:2824],
                                  w[:, 2824:5896])
    pad = jnp.zeros((w.shape[0], OFF_Z - OFF_DT - 8), w.dtype)
    return jnp.concatenate([gates, sc, qkv, xbc, dt, pad, z], axis=1)


def _unpad_in_proj(w):
    return jnp.concatenate([w[:, OFF_SC:OFF_SC + 768], w[:, OFF_QKV:OFF_QKV + 768], w[:, OFF_Z:OFF_Z + 512],
                            w[:, OFF_XBC:OFF_XBC + 768], w[:, OFF_DT:OFF_DT + 8], w[:, 0:3072]], axis=1)


def _row(v):
    return v.reshape(1, -1)


def _local_step(x, target, mod, small, conv, big):
    lw, saved = [], []
    for l in range(DEPTH):
        w_in_p = _pad_in_proj(big["w_in"][l])
        w_cat = jnp.concatenate([big["w_sc_out"][l], big["w_sb_out"][l], big["w_ssm_out"][l]], axis=0)
        w_ffn_in_t = big["w_ffn_in"][l].T
        hp_rows = jnp.zeros((8, LANES), F32).at[0, :SSM_HEADS].set(small["ssm_dt_bias"][l]).at[1, :SSM_HEADS].set(
            small["ssm_a_log"][l])
        hp_cols = jnp.concatenate([jnp.broadcast_to(small["ssm_dt_bias"][l][:, None], (SSM_HEADS, SSM_CHUNK)),
                                   jnp.broadcast_to(small["ssm_a_log"][l][:, None], (SSM_HEADS, SSM_CHUNK))], axis=0)
        lw.append(dict(
            w_in_p=w_in_p, w_in_pt=w_in_p.T, w_cat=w_cat, w_cat_t=w_cat.T, w_o=big["w_o"][l], w_o_t=big["w_o"][l].T,
            w_ffn_in=big["w_ffn_in"][l], wg_t=w_ffn_in_t[:FFN_HIDDEN], wu_t=w_ffn_in_t[FFN_HIDDEN:],
            w_ffn_out=big["w_ffn_out"][l], w_ffn_out_t=big["w_ffn_out"][l].T,
            sc_w8=jnp.pad(conv["sc_conv_w"][l], ((0, 5), (0, 0))), ssm_w8=jnp.pad(conv["ssm_conv_w"][l], ((0, 4), (0, 0))),
            ssm_b=_row(small["ssm_conv_b"][l]), hp_rows=hp_rows, hp_cols=hp_cols,
            d_e=_row(jnp.repeat(small["ssm_d"][l], SSM_HEAD_DIM)), norm_w=_row(small["ssm_norm_w"][l]),
            g_pre_mix=_row(small["g_pre_mix"][l]), g_post_mix=_row(small["g_post_mix"][l]),
            g_pre_ffn=_row(small["g_pre_ffn"][l]), g_post_ffn=_row(small["g_post_ffn"][l]),
            shift1=mod[l, 0:1], scale1=mod[l, 1:2], gate1=mod[l, 2:3], shift2=mod[l, 3:4], scale2=mod[l, 4:5],
            gate2=mod[l, 5:6]))

    xl = x
    h = normmod_fwd(xl, lw[0]["g_pre_mix"], lw[0]["scale1"], lw[0]["shift1"], "normmod_fwd_0")
    dy = loss = None
    for l in range(DEPTH):
        w = lw[l]
        p = mm_nn([h], [w["w_in_p"]], F32, f"in_proj_{l}")
        ya, qkv, act = post_inproj(p, w["sc_w8"], w["ssm_w8"], w["ssm_b"], f"post_inproj_{l}")
        o, runs = sba_fwd(qkv, f"sba_fwd_{l}")
        dtt = p[:, OFF_DT:OFF_DT + 8].T
        yc, ypre, states = ssd_fwd(act, p, dtt, w["hp_rows"], w["hp_cols"], w["d_e"], w["norm_w"], f"ssd_fwd_{l}")
        merged = branch_out_fwd(ya, o, yc, p, w["w_cat"], f"branch_fwd_{l}")
        mix = mm_nn([merged], [w["w_o"]], F32, f"out_proj_{l}")
        x1, h2 = resid_normmod_fwd(xl, mix, w["gate1"], w["g_post_mix"], w["g_pre_ffn"], w["scale2"], w["shift2"],
                                   f"resid_mix_{l}")
        gt, up, a = mm_swiglu_fwd(h2, w["w_ffn_in"], f"ffn_in_{l}")
        f = mm_nn([a], [w["w_ffn_out"]], F32, f"ffn_out_{l}")
        saved.append(dict(x=xl, h=h, p=p, ya=ya, qkv=qkv, act=act, o=o, runs=runs, dtt=dtt, yc=yc, ypre=ypre, states=states,
                          merged=merged, mix=mix, x1=x1, h2=h2, gt=gt, up=up, a=a, f=f))
        if l + 1 < DEPTH:
            nw = lw[l + 1]
            xl, h = resid_normmod_fwd(x1, f, w["gate2"], w["g_post_ffn"], nw["g_pre_mix"], nw["scale1"], nw["shift1"],
                                      f"resid_ffn_{l}")
        else:
            dy, loss = resid_loss(x1, f, w["gate2"], w["g_post_ffn"], target, "resid_loss")

    dmod = [None] * DEPTH
    gs = {k: [None] * DEPTH for k in _SMALL + _BIG}
    dxo = dy
    for l in reversed(range(DEPTH)):
        w, s = lw[l], saved[l]
        df, dgate2, gs["g_post_ffn"][l] = resid_bwd(dxo, s["f"], w["gate2"], w["g_post_ffn"], f"resid_ffn_bwd_{l}")
        dgt, dup = mm_swiglu_bwd(df, w["w_ffn_out_t"], s["gt"], s["up"], f"ffn_out_bwd_{l}")
        gs["w_ffn_out"][l] = mm_tn(s["a"], df, f"dw_ffn_out_{l}")
        dh2 = mm_nn([dgt, dup], [w["wg_t"], w["wu_t"]], F32, f"ffn_in_bwd_{l}")
        gs["w_ffn_in"][l] = jnp.concatenate([mm_tn(s["h2"], dgt, f"dw_ffn_gate_{l}"), mm_tn(s["h2"], dup, f"dw_ffn_up_{l}")],
                                            axis=1)
        dx1, dscale2, dshift2, gs["g_pre_ffn"][l] = normmod_bwd(dh2, s["x1"], dxo, w["g_pre_ffn"], w["scale2"],
                                                                f"normmod_ffn_bwd_{l}")
        dmix, dgate1, gs["g_post_mix"][l] = resid_bwd(dx1, s["mix"], w["gate1"], w["g_post_mix"], f"resid_mix_bwd_{l}")
        dmerged = mm_nn([dmix], [w["w_o_t"]], F32, f"out_proj_bwd_{l}")
        gs["w_o"][l] = mm_tn(s["merged"], dmix, f"dw_o_{l}")
        dgl, dya, dyb, dyc, dw_cat = branch_out_bwd(dmerged, s["ya"], s["o"], s["yc"], s["p"], w["w_cat"], w["w_cat_t"],
                                                    f"branch_bwd_{l}")
        gs["w_sc_out"][l], gs["w_sb_out"][l], gs["w_ssm_out"][l] = dw_cat[0:256], dw_cat[256:512], dw_cat[512:1024]
        dz, dact, ddt, dnw, dd_e, dhp = ssd_bwd(dyc, s["ypre"], s["act"], s["p"], s["dtt"], s["states"], w["hp_rows"],
                                                w["hp_cols"], w["d_e"], w["norm_w"], f"ssd_bwd_{l}")
        gs["ssm_norm_w"][l] = dnw[0]
        gs["ssm_d"][l] = dd_e.reshape(SSM_HEADS, SSM_HEAD_DIM).sum(axis=1)
        gs["ssm_dt_bias"][l] = dhp[0, :SSM_HEADS]
        gs["ssm_a_log"][l] = dhp[1, :SSM_HEADS]
        dq, dk, dv = sba_bwd(s["qkv"], s["runs"], dyb, f"sba_bwd_{l}")
        dp, dscw, dssw, dssb = assemble_dp(dgl, dya, s["p"], w["sc_w8"], dq, dk, dv, dact, w["ssm_w8"], w["ssm_b"], ddt, dz,
                                           f"assemble_dp_{l}")
        gs["sc_conv_w"][l], gs["ssm_conv_w"][l], gs["ssm_conv_b"][l] = dscw[0:3], dssw[0:4], dssb[0]
        dh = mm_nn([dp], [w["w_in_pt"]], F32, f"in_proj_bwd_{l}")
        gs["w_in"][l] = _unpad_in_proj(mm_tn(s["h"], dp, f"dw_in_{l}"))
        dxo, dscale1, dshift1, gs["g_pre_mix"][l] = normmod_bwd(dh, s["x"], dx1, w["g_pre_mix"], w["scale1"],
                                                                f"normmod_mix_bwd_{l}")
        dmod[l] = jnp.concatenate([dshift1, dscale1, dgate1, dshift2, dscale2, dgate2], axis=0)
    for k in ("g_pre_mix", "g_post_mix", "g_pre_ffn", "g_post_ffn"):
        gs[k] = [g[0] for g in gs[k]]
    grads = {k: jnp.stack(v) for k, v in gs.items() if k != "mod_b"}
    return loss[0, 0], dxo, jnp.stack(dmod), grads


def kernel(x, c, mod_w, mod_b, g_pre_mix, g_post_mix, g_pre_ffn, g_post_ffn, w_in, sc_conv_w, ssm_conv_w, ssm_conv_b, ssm_dt_bias, ssm_a_log, ssm_d, ssm_norm_w, w_sc_out, w_sb_out, w_ssm_out, w_o, w_ffn_in, w_ffn_out, loss_target, m_mod_w, m_mod_b, m_g_pre_mix, m_g_post_mix, m_g_pre_ffn, m_g_post_ffn, m_w_in, m_sc_conv_w, m_ssm_conv_w, m_ssm_conv_b, m_ssm_dt_bias, m_ssm_a_log, m_ssm_d, m_ssm_norm_w, m_w_sc_out, m_w_sb_out, m_w_ssm_out, m_w_o, m_w_ffn_in, m_w_ffn_out, v_mod_w, v_mod_b, v_g_pre_mix, v_g_post_mix, v_g_pre_ffn, v_g_post_ffn, v_w_in, v_sc_conv_w, v_ssm_conv_w, v_ssm_conv_b, v_ssm_dt_bias, v_ssm_a_log, v_ssm_d, v_ssm_norm_w, v_w_sc_out, v_w_sb_out, v_w_ssm_out, v_w_o, v_w_ffn_in, v_w_ffn_out):
    wts = dict(mod_w=mod_w, mod_b=mod_b, g_pre_mix=g_pre_mix, g_post_mix=g_post_mix, g_pre_ffn=g_pre_ffn,
               g_post_ffn=g_post_ffn, w_in=w_in, sc_conv_w=sc_conv_w, ssm_conv_w=ssm_conv_w, ssm_conv_b=ssm_conv_b,
               ssm_dt_bias=ssm_dt_bias, ssm_a_log=ssm_a_log, ssm_d=ssm_d, ssm_norm_w=ssm_norm_w, w_sc_out=w_sc_out,
               w_sb_out=w_sb_out, w_ssm_out=w_ssm_out, w_o=w_o, w_ffn_in=w_ffn_in, w_ffn_out=w_ffn_out)
    ms = dict(mod_w=m_mod_w, mod_b=m_mod_b, g_pre_mix=m_g_pre_mix, g_post_mix=m_g_post_mix, g_pre_ffn=m_g_pre_ffn,
              g_post_ffn=m_g_post_ffn, w_in=m_w_in, sc_conv_w=m_sc_conv_w, ssm_conv_w=m_ssm_conv_w,
              ssm_conv_b=m_ssm_conv_b, ssm_dt_bias=m_ssm_dt_bias, ssm_a_log=m_ssm_a_log, ssm_d=m_ssm_d,
              ssm_norm_w=m_ssm_norm_w, w_sc_out=m_w_sc_out, w_sb_out=m_w_sb_out, w_ssm_out=m_w_ssm_out, w_o=m_w_o,
              w_ffn_in=m_w_ffn_in, w_ffn_out=m_w_ffn_out)
    vs = dict(mod_w=v_mod_w, mod_b=v_mod_b, g_pre_mix=v_g_pre_mix, g_post_mix=v_g_post_mix, g_pre_ffn=v_g_pre_ffn,
              g_post_ffn=v_g_post_ffn, w_in=v_w_in, sc_conv_w=v_sc_conv_w, ssm_conv_w=v_ssm_conv_w,
              ssm_conv_b=v_ssm_conv_b, ssm_dt_bias=v_ssm_dt_bias, ssm_a_log=v_ssm_a_log, ssm_d=v_ssm_d,
              ssm_norm_w=v_ssm_norm_w, w_sc_out=v_w_sc_out, w_sb_out=v_w_sb_out, w_ssm_out=v_w_ssm_out, w_o=v_w_o,
              w_ffn_in=v_w_ffn_in, w_ffn_out=v_w_ffn_out)
    me = 4 * lax.axis_index("x") + 2 * lax.axis_index("y") + lax.axis_index("c")
    mod_cols = mod_w.shape[2]

    pack1, sizes1 = _pack_rows([c, sc_conv_w, ssm_conv_w], F32, 8)
    got1 = all_gather_rows(pack1, "gather_c_conv").reshape(N_DEV, -1)
    c_all, sc_g, ssm_g = _unpack(got1, sizes1, [(D_MODEL,), sc_conv_w.shape, ssm_conv_w.shape])
    conv = dict(sc_conv_w=_gathered_to_full(sc_g, False), ssm_conv_w=_gathered_to_full(ssm_g, False))

    mod_b_shard = lax.dynamic_slice_in_dim(mod_b, me * mod_cols, mod_cols, axis=1).reshape(DEPTH, 1, mod_cols)
    mod_sh = mod_shard_fwd(c_all, mod_w, mod_b_shard, "mod_shard_fwd")
    pack2, sizes2 = _pack_rows([mod_sh], F32, 8)
    got2 = all_gather_rows(pack2, "gather_mod").reshape(N_DEV, -1)
    mod_all = _unpack(got2, sizes2, [mod_sh.shape])[0]
    mod_mine = lax.dynamic_index_in_dim(mod_all, me, axis=2, keepdims=False)
    mod = mod_mine.transpose(1, 0, 2).reshape(DEPTH, 6, D_MODEL)

    ag_names = [k for k in _BIG if k not in ("sc_conv_w", "ssm_conv_w")]
    pack3, sizes3 = _pack_rows([wts[k] for k in ag_names], BF16, 16)
    got3 = all_gather_rows(pack3, "gather_weights").reshape(N_DEV, -1)
    parts3 = _unpack(got3, sizes3, [wts[k].shape for k in ag_names])
    big = {k: _gathered_to_full(g, k in _ROW_SHARDED) for k, g in zip(ag_names, parts3)}

    small = {k: wts[k] for k in _SMALL}
    loss_part, dx, dmod, grads = _local_step(x[0], loss_target[0], mod, small, conv, big)
    loss = lax.psum(loss_part, ("x", "y", "c"))

    small_parts = [dmod.reshape(DEPTH, 6 * D_MODEL)] + [grads[k] for k in _SMALL[1:]]
    pack5, sizes5 = _pack_rows(small_parts, F32, 8)
    got5 = all_gather_rows(pack5, "gather_small_grads")
    w5, _ = _pack_rows([wts[k] for k in _SMALL], F32, 8)
    m5, _ = _pack_rows([ms[k] for k in _SMALL], F32, 8)
    v5, _ = _pack_rows([vs[k] for k in _SMALL], F32, 8)
    res5 = adamw_flat(got5, w5, m5, v5, "adamw_small")
    small_out = [_unpack(r.reshape(-1), sizes5, [wts[k].shape for k in _SMALL]) for r in res5]

    dmod_all = got5.reshape(N_DEV, -1)[:, :DEPTH * 6 * D_MODEL].reshape(N_DEV, DEPTH, 6 * D_MODEL)
    dmod_shard = lax.dynamic_slice_in_dim(dmod_all, me * mod_cols, mod_cols, axis=2).transpose(1, 0, 2)
    g_mod_w = mod_w_grad(c_all, dmod_shard, "mod_w_grad")
    packw, sizesw = _pack_rows([g_mod_w], F32, 512)
    res_mw = adamw_flat(packw[None], _pack_rows([mod_w], F32, 512)[0], _pack_rows([m_mod_w], F32, 512)[0],
                        _pack_rows([v_mod_w], F32, 512)[0], "adamw_mod_w")
    mod_w_out = [_unpack(r.reshape(-1), sizesw, [mod_w.shape])[0] for r in res_mw]

    slot_parts = [_full_to_slots(grads[k], k in _ROW_SHARDED) for k in _BIG]
    slot_sizes = [s.shape[1] for s in slot_parts]
    total = sum(slot_sizes)
    quantum = LANES * 512
    padded = -(-total // quantum) * quantum
    send = jnp.concatenate(slot_parts + [jnp.zeros((N_DEV, padded - total), F32)], axis=1).reshape(N_DEV, -1, LANES)
    recv = exchange_slots(send, "exchange_grads")
    w7, _ = _pack_rows([wts[k] for k in _BIG], F32, 512)
    m7, _ = _pack_rows([ms[k] for k in _BIG], F32, 512)
    v7, _ = _pack_rows([vs[k] for k in _BIG], F32, 512)
    res7 = adamw_flat(recv, w7, m7, v7, "adamw_big")
    big_out = [_unpack(r.reshape(-1), slot_sizes, [wts[k].shape for k in _BIG]) for r in res7]

    outs = []
    for kind in range(4):
        by_name = {"mod_w": mod_w_out[kind]}
        by_name.update(zip(_SMALL, small_out[kind]))
        by_name.update(zip(_BIG, big_out[kind]))
        outs.extend(by_name[k] for k in _WEIGHTS)
    return (loss, dx[None], *outs)
```

```python
import functools
import math

import jax
import jax.numpy as jnp
from jax import lax
from jax.experimental import pallas as pl
from jax.experimental.pallas import tpu as pltpu

F32 = jnp.float32
BF16 = jnp.bfloat16
HI = lax.Precision.HIGHEST

N_DEV = 8
D_MODEL = 1024
DEPTH = 2
SC_WIDTH = 256
SB_WIDTH = 256
SB_HEAD_DIM = 64
SSM_INNER = 512
SSM_HEADS = 8
SSM_HEAD_DIM = 64
SSM_GROUPS = 2
SSM_STATE = 64
SSM_CHUNK = 256
SSM_CONV_DIM = 768
FFN_HIDDEN = 2816
NORM_EPS = 1e-6
IN_PROJ = 5896
LANES = 128
VMEM_LIMIT = 56 * 1024 * 1024

OFF_GATES = 0
OFF_SC = 3072
OFF_QKV = 3840
OFF_XBC = 4608
OFF_DT = 5376
OFF_Z = 5632
IN_PAD = 6144

ADAM_LR = 0.001
ADAM_B1 = 0.9
ADAM_B2 = 0.999
ADAM_EPS = 1e-08
ADAM_WD = 0.01
ADAM_STEP = 10

MESH_ID = pl.DeviceIdType.MESH


def _cparams(sem=None):
    return pltpu.CompilerParams(dimension_semantics=sem, vmem_limit_bytes=VMEM_LIMIT)


def _my_pos():
    return lax.axis_index("x"), lax.axis_index("y"), lax.axis_index("c")


def _peer(k, x, y, c):
    return (x ^ ((k >> 2) & 1), y ^ ((k >> 1) & 1), c ^ (k & 1))


def all_gather_rows(block, name):
    rows, lanes = block.shape

    def body(x_ref, out_ref, send_sems, recv_sems, local_sem):
        x, y, c = _my_pos()
        me, sibling = (x, y, c), (x, y, 1 - c)
        chips = [(1 - x, y), (x, 1 - y), (1 - x, 1 - y)]

        def slot(px, py, pc):
            return out_ref.at[4 * px + 2 * py + pc]

        def copy(k, blk, to, src=None):
            return pltpu.make_async_remote_copy(
                src_ref=slot(*blk) if src is None else src, dst_ref=slot(*blk),
                send_sem=send_sems.at[k], recv_sem=recv_sems.at[k], device_id=to, device_id_type=MESH_ID)

        mine = pltpu.make_async_copy(x_ref, slot(*me), local_sem)
        mine.start()
        first = [copy(0, me, sibling, src=x_ref)]
        first += [copy(1 + j, me, (*chip, c), src=x_ref) for j, chip in enumerate(chips)]
        for cp in first:
            cp.start()
        passed = [copy(4 + j, (*chip, c), sibling) for j, chip in enumerate(chips)]
        for j, chip in enumerate(chips):
            copy(1 + j, (*chip, c), me).wait_recv()
            passed[j].start()
        copy(0, sibling, me).wait_recv()
        for j, chip in enumerate(chips):
            copy(4 + j, (*chip, 1 - c), me).wait_recv()
        for cp in first + passed:
            cp.wait_send()
        mine.wait()

    return pl.pallas_call(
        body, name=name,
        out_shape=jax.ShapeDtypeStruct((N_DEV, rows, lanes), block.dtype),
        in_specs=[pl.BlockSpec(memory_space=pl.ANY)],
        out_specs=pl.BlockSpec(memory_space=pl.ANY),
        scratch_shapes=[pltpu.SemaphoreType.DMA((7,)), pltpu.SemaphoreType.DMA((7,)), pltpu.SemaphoreType.DMA],
    )(block)


def exchange_slots(send, name):
    _, rows, lanes = send.shape

    def body(s_ref, r_ref, send_sems, recv_sems, local_sem):
        x, y, c = _my_pos()
        me = 4 * x + 2 * y + c
        mine = pltpu.make_async_copy(s_ref.at[me], r_ref.at[me], local_sem)
        mine.start()
        copies = []
        for k in range(1, N_DEV):
            px, py, pc = _peer(k, x, y, c)
            cp = pltpu.make_async_remote_copy(
                src_ref=s_ref.at[4 * px + 2 * py + pc], dst_ref=r_ref.at[me],
                send_sem=send_sems.at[k - 1], recv_sem=recv_sems.at[k - 1],
                device_id=(px, py, pc), device_id_type=MESH_ID)
            cp.start()
            copies.append(cp)
        for cp in copies:
            cp.wait_recv()
        for cp in copies:
            cp.wait_send()
        mine.wait()

    return pl.pallas_call(
        body, name=name,
        out_shape=jax.ShapeDtypeStruct((N_DEV, rows, lanes), send.dtype),
        in_specs=[pl.BlockSpec(memory_space=pl.ANY)],
        out_specs=pl.BlockSpec(memory_space=pl.ANY),
        scratch_shapes=[pltpu.SemaphoreType.DMA((7,)), pltpu.SemaphoreType.DMA((7,)), pltpu.SemaphoreType.DMA],
    )(send)


def all_gather_multi(blocks, name):
    n = len(blocks)

    def body(*refs):
        x_refs, o_refs = refs[:n], refs[n:2 * n]
        send_sems, recv_sems, local_sems = refs[2 * n:]
        x, y, c = _my_pos()
        me, sibling = (x, y, c), (x, y, 1 - c)
        chips = [(1 - x, y), (x, 1 - y), (1 - x, 1 - y)]

        def slot(a, px, py, pc):
            return o_refs[a].at[4 * px + 2 * py + pc]

        def copy(a, k, blk, to, src=None):
            return pltpu.make_async_remote_copy(
                src_ref=slot(a, *blk) if src is None else src, dst_ref=slot(a, *blk),
                send_sem=send_sems.at[7 * a + k], recv_sem=recv_sems.at[7 * a + k], device_id=to, device_id_type=MESH_ID)

        mine = [pltpu.make_async_copy(x_refs[a], slot(a, *me), local_sems.at[a]) for a in range(n)]
        for cp in mine:
            cp.start()
        first = [copy(a, 1 + j, me, (*chip, c), src=x_refs[a]) for j, chip in enumerate(chips) for a in range(n)]
        first += [copy(a, 0, me, sibling, src=x_refs[a]) for a in range(n)]
        for cp in first:
            cp.start()
        passed = []
        for j, chip in enumerate(chips):
            for a in range(n):
                copy(a, 1 + j, (*chip, c), me).wait_recv()
                fwd = copy(a, 4 + j, (*chip, c), sibling)
                fwd.start()
                passed.append(fwd)
        for a in range(n):
            copy(a, 0, sibling, me).wait_recv()
            for j, chip in enumerate(chips):
                copy(a, 4 + j, (*chip, 1 - c), me).wait_recv()
        for cp in first + passed:
            cp.wait_send()
        for cp in mine:
            cp.wait()

    any_spec = pl.BlockSpec(memory_space=pl.ANY)
    return pl.pallas_call(
        body, name=name,
        out_shape=[jax.ShapeDtypeStruct((N_DEV,) + b.shape, b.dtype) for b in blocks],
        in_specs=[any_spec] * n, out_specs=[any_spec] * n,
        scratch_shapes=[pltpu.SemaphoreType.DMA((7 * n,)), pltpu.SemaphoreType.DMA((7 * n,)), pltpu.SemaphoreType.DMA((n,))],
    )(*blocks)


def exchange_multi(sends, name):
    n = len(sends)

    def body(*refs):
        s_refs, r_refs = refs[:n], refs[n:2 * n]
        send_sems, recv_sems, local_sems = refs[2 * n:]
        x, y, c = _my_pos()
        me = 4 * x + 2 * y + c
        mine = [pltpu.make_async_copy(s_refs[a].at[me], r_refs[a].at[me], local_sems.at[a]) for a in range(n)]
        for cp in mine:
            cp.start()
        copies = []
        for k in (2, 4, 6, 3, 5, 7, 1):
            px, py, pc = _peer(k, x, y, c)
            for a in range(n):
                cp = pltpu.make_async_remote_copy(
                    src_ref=s_refs[a].at[4 * px + 2 * py + pc], dst_ref=r_refs[a].at[me],
                    send_sem=send_sems.at[7 * a + k - 1], recv_sem=recv_sems.at[7 * a + k - 1],
                    device_id=(px, py, pc), device_id_type=MESH_ID)
                cp.start()
                copies.append(cp)
        for cp in copies:
            cp.wait_recv()
        for cp in copies:
            cp.wait_send()
        for cp in mine:
            cp.wait()

    any_spec = pl.BlockSpec(memory_space=pl.ANY)
    return pl.pallas_call(
        body, name=name,
        out_shape=[jax.ShapeDtypeStruct(s.shape, s.dtype) for s in sends],
        in_specs=[any_spec] * n, out_specs=[any_spec] * n,
        scratch_shapes=[pltpu.SemaphoreType.DMA((7 * n,)), pltpu.SemaphoreType.DMA((7 * n,)), pltpu.SemaphoreType.DMA((n,))],
    )(*sends)


def _pack_rows(parts, dtype, row_multiple):
    flat = [p.astype(dtype).reshape(-1) for p in parts]
    sizes = [f.shape[0] for f in flat]
    total = sum(sizes)
    quantum = LANES * row_multiple
    padded = -(-total // quantum) * quantum
    if padded > total:
        flat.append(jnp.zeros((padded - total,), dtype))
    return jnp.concatenate(flat).reshape(padded // LANES, LANES), sizes


def _unpack(flat, sizes, shapes):
    out, off = [], 0
    lead = flat.shape[:-1]
    for n, shp in zip(sizes, shapes):
        out.append(flat[..., off:off + n].reshape(lead + tuple(shp)))
        off += n
    return out


def rows_call(name, body, n_rows, tr, ins, outs, scratch=()):
    n_tiles = n_rows // tr
    assert n_tiles * tr == n_rows
    in_specs, arrays = [], []
    for arr, kind in ins:
        arrays.append(arr)
        if kind == "row":
            in_specs.append(pl.BlockSpec((tr, arr.shape[1]), lambda i: (i, 0)))
        elif kind == "full":
            in_specs.append(pl.BlockSpec(arr.shape, lambda i, nd=arr.ndim: (0,) * nd))
        elif kind[0] == "row":
            _, w, ci = kind
            in_specs.append(pl.BlockSpec((tr, w), lambda i, ci=ci: (i, ci)))
        elif kind[0] == "prev8":
            _, w, ci = kind
            in_specs.append(pl.BlockSpec((8, w), lambda i, ci=ci: (jnp.maximum(i * (tr // 8) - 1, 0), ci)))
        elif kind[0] == "next8":
            _, w, ci = kind
            last = n_rows // 8 - 1
            in_specs.append(pl.BlockSpec((8, w), lambda i, ci=ci, last=last: (jnp.minimum((i + 1) * (tr // 8), last), ci)))
        else:
            raise ValueError(kind)
    out_specs, out_shapes = [], []
    for shape, dtype, kind in outs:
        out_shapes.append(jax.ShapeDtypeStruct(shape, dtype))
        if kind == "row":
            out_specs.append(pl.BlockSpec((tr, shape[1]), lambda i: (i, 0)))
        else:
            out_specs.append(pl.BlockSpec(shape, lambda i, nd=len(shape): (0,) * nd))
    has_acc = any(k == "acc" for _, _, k in outs)
    return pl.pallas_call(
        body, name=name, grid=(n_tiles,), in_specs=in_specs, out_specs=out_specs, out_shape=out_shapes,
        scratch_shapes=list(scratch),
        compiler_params=_cparams(("arbitrary",) if has_acc else ("parallel",)),
    )(*arrays)


def _acc(ref, val):
    @pl.when(pl.program_id(0) == 0)
    def _():
        ref[...] = jnp.zeros_like(ref)
    ref[...] += val


def _rstd(x):
    return lax.rsqrt(jnp.mean(x * x, axis=-1, keepdims=True) + NORM_EPS)


def _sigmoid(x):
    return 1.0 / (1.0 + jnp.exp(-x))


def _silu(x):
    return x * _sigmoid(x)


def _dsilu(x):
    s = _sigmoid(x)
    return s * (1.0 + x * (1.0 - s))


def _softplus(x):
    return jnp.maximum(x, 0.0) + jnp.log(1.0 + jnp.exp(-jnp.abs(x)))


def _log_sigmoid_neg(x):
    t = -x
    return jnp.minimum(t, 0.0) - jnp.log(1.0 + jnp.exp(jnp.minimum(x, t)))


def normmod_fwd(x, g, scale, shift, name):
    t, d = x.shape

    def body(x_ref, g_ref, sc_ref, sh_ref, h_ref):
        xv = x_ref[...]
        h = xv * _rstd(xv) * g_ref[...] * (1.0 + sc_ref[...]) + sh_ref[...]
        h_ref[...] = h.astype(BF16)

    return rows_call(name, body, t, 512, [(x, "row"), (g, "full"), (scale, "full"), (shift, "full")],
                     [((t, d), BF16, "row")])[0]


def resid_normmod_fwd(x, f, gate, g_post, g_pre, scale, shift, name):
    t, d = x.shape

    def body(x_ref, f_ref, gate_ref, gp_ref, g_ref, sc_ref, sh_ref, xo_ref, h_ref):
        fv = f_ref[...]
        xn = x_ref[...] + gate_ref[...] * (fv * _rstd(fv) * gp_ref[...])
        xo_ref[...] = xn
        h = xn * _rstd(xn) * g_ref[...] * (1.0 + sc_ref[...]) + sh_ref[...]
        h_ref[...] = h.astype(BF16)

    return rows_call(name, body, t, 512,
                     [(x, "row"), (f, "row"), (gate, "full"), (g_post, "full"), (g_pre, "full"), (scale, "full"),
                      (shift, "full")],
                     [((t, d), F32, "row"), ((t, d), BF16, "row")])


def resid_loss(x, f, gate, g_post, target, name):
    t, d = x.shape

    def body(x_ref, f_ref, gate_ref, gp_ref, tg_ref, dy_ref, loss_ref):
        fv = f_ref[...]
        yv = x_ref[...] + gate_ref[...] * (fv * _rstd(fv) * gp_ref[...])
        err = yv - tg_ref[...]
        dy_ref[...] = err * (1.0 / d)
        part = 0.5 * jnp.sum(jnp.mean(err * err, axis=-1, keepdims=True), axis=0, keepdims=True)
        _acc(loss_ref, jnp.broadcast_to(part, loss_ref.shape))

    return rows_call(name, body, t, 512,
                     [(x, "row"), (f, "row"), (gate, "full"), (g_post, "full"), (target, "row")],
                     [((t, d), F32, "row"), ((8, LANES), F32, "acc")])


def resid_bwd(dx, f, gate, g_post, name):
    t, d = dx.shape

    def body(dx_ref, f_ref, gate_ref, gp_ref, df_ref, dgate_ref, dg_ref):
        fv, dxv, gp = f_ref[...], dx_ref[...], gp_ref[...]
        r = _rstd(fv)
        fn = fv * r
        _acc(dgate_ref, jnp.sum(dxv * (fn * gp), axis=0, keepdims=True))
        dn = dxv * gate_ref[...]
        _acc(dg_ref, jnp.sum(dn * fn, axis=0, keepdims=True))
        u = dn * gp
        df = r * (u - fn * jnp.mean(fn * u, axis=-1, keepdims=True))
        df_ref[...] = df.astype(BF16)

    return rows_call(name, body, t, 512, [(dx, "row"), (f, "row"), (gate, "full"), (g_post, "full")],
                     [((t, d), BF16, "row"), ((1, d), F32, "acc"), ((1, d), F32, "acc")])


def normmod_bwd(dh, x, dx_in, g, scale, name):
    t, d = x.shape

    def body(dh_ref, x_ref, dxi_ref, g_ref, sc_ref, dx_ref, dsc_ref, dsh_ref, dg_ref):
        xv, dhv, gv = x_ref[...], dh_ref[...], g_ref[...]
        r = _rstd(xv)
        xn = xv * r
        _acc(dsc_ref, jnp.sum(dhv * (xn * gv), axis=0, keepdims=True))
        _acc(dsh_ref, jnp.sum(dhv, axis=0, keepdims=True))
        dn = dhv * (1.0 + sc_ref[...])
        _acc(dg_ref, jnp.sum(dn * xn, axis=0, keepdims=True))
        u = dn * gv
        dx_ref[...] = dxi_ref[...] + r * (u - xn * jnp.mean(xn * u, axis=-1, keepdims=True))

    return rows_call(name, body, t, 512, [(dh, "row"), (x, "row"), (dx_in, "row"), (g, "full"), (scale, "full")],
                     [((t, d), F32, "row"), ((1, d), F32, "acc"), ((1, d), F32, "acc"), ((1, d), F32, "acc")])


def _pick(n, prefs):
    for p in prefs:
        if n % p == 0:
            return p
    return n


def mm_nn(a_list, b_list, out_dtype, name, tm=512, tn=None, tk=None):
    m, k = a_list[0].shape
    n = b_list[0].shape[1]
    tn = tn or _pick(n, (1024, 768, 512, 256))
    tk = tk or _pick(k, (1024, 1408, 512, 256))
    nk = k // tk
    npair = len(a_list)

    def body(*refs):
        a_refs, b_refs = refs[:npair], refs[npair:2 * npair]
        o_ref, acc = refs[2 * npair], refs[2 * npair + 1]
        kk = pl.program_id(2)

        @pl.when(kk == 0)
        def _():
            acc[...] = jnp.zeros_like(acc)

        s = acc[...]
        for a_ref, b_ref in zip(a_refs, b_refs):
            s = s + jnp.dot(a_ref[...], b_ref[...], preferred_element_type=F32)
        acc[...] = s

        @pl.when(kk == nk - 1)
        def _():
            o_ref[...] = acc[...].astype(o_ref.dtype)

    return pl.pallas_call(
        body, name=name, grid=(m // tm, n // tn, nk),
        in_specs=[pl.BlockSpec((tm, tk), lambda i, j, kk: (i, kk))] * npair
        + [pl.BlockSpec((tk, tn), lambda i, j, kk: (kk, j))] * npair,
        out_specs=pl.BlockSpec((tm, tn), lambda i, j, kk: (i, j)),
        out_shape=jax.ShapeDtypeStruct((m, n), out_dtype),
        scratch_shapes=[pltpu.VMEM((tm, tn), F32)],
        compiler_params=_cparams(("parallel", "parallel", "arbitrary")),
    )(*a_list, *b_list)


def mm_tn(a, b, name, tt=512):
    t, ka = a.shape
    n = b.shape[1]
    ta = _pick(ka, (1024, 1408, 512, 256))
    tn = _pick(n, (1024, 1408, 512, 256))
    nt = t // tt

    def body(a_ref, b_ref, o_ref):
        @pl.when(pl.program_id(2) == 0)
        def _():
            o_ref[...] = jnp.zeros_like(o_ref)

        o_ref[...] += lax.dot_general(a_ref[...], b_ref[...], (((0,), (0,)), ((), ())), preferred_element_type=F32)

    return pl.pallas_call(
        body, name=name, grid=(ka // ta, n // tn, nt),
        in_specs=[pl.BlockSpec((tt, ta), lambda i, j, s: (s, i)), pl.BlockSpec((tt, tn), lambda i, j, s: (s, j))],
        out_specs=pl.BlockSpec((ta, tn), lambda i, j, s: (i, j)),
        out_shape=jax.ShapeDtypeStruct((ka, n), F32),
        compiler_params=_cparams(("parallel", "parallel", "arbitrary")),
    )(a, b)


def mm_swiglu_fwd(h, w_ffn_in, name, tm=512, tn=1408):
    m, k = h.shape
    nh = FFN_HIDDEN // tn

    def body(h_ref, wg_ref, wu_ref, gt_ref, up_ref, a_ref):
        hv = h_ref[...]
        gt = jnp.dot(hv, wg_ref[...], preferred_element_type=F32)
        up = jnp.dot(hv, wu_ref[...], preferred_element_type=F32)
        gt_ref[...] = gt.astype(BF16)
        up_ref[...] = up.astype(BF16)
        a_ref[...] = (_silu(gt) * up).astype(BF16)

    shp = jax.ShapeDtypeStruct((m, FFN_HIDDEN), BF16)
    ospec = pl.BlockSpec((tm, tn), lambda i, j: (i, j))
    return pl.pallas_call(
        body, name=name, grid=(m // tm, nh),
        in_specs=[pl.BlockSpec((tm, k), lambda i, j: (i, 0)), pl.BlockSpec((k, tn), lambda i, j: (0, j)),
                  pl.BlockSpec((k, tn), lambda i, j: (0, j + nh))],
        out_specs=[ospec, ospec, ospec], out_shape=[shp, shp, shp],
        compiler_params=_cparams(("parallel", "parallel")),
    )(h, w_ffn_in, w_ffn_in)


def mm_swiglu_bwd(df, w_out_t, gt, up, name, tm=512, tn=1408):
    m, k = df.shape

    def body(df_ref, w_ref, gt_ref, up_ref, dgt_ref, dup_ref):
        da = jnp.dot(df_ref[...], w_ref[...], preferred_element_type=F32)
        gtv = gt_ref[...].astype(F32)
        upv = up_ref[...].astype(F32)
        dgt_ref[...] = (da * upv * _dsilu(gtv)).astype(BF16)
        dup_ref[...] = (da * _silu(gtv)).astype(BF16)

    shp = jax.ShapeDtypeStruct((m, FFN_HIDDEN), BF16)
    tile = pl.BlockSpec((tm, tn), lambda i, j: (i, j))
    return pl.pallas_call(
        body, name=name, grid=(m // tm, FFN_HIDDEN // tn),
        in_specs=[pl.BlockSpec((tm, k), lambda i, j: (i, 0)), pl.BlockSpec((k, tn), lambda i, j: (0, j)), tile, tile],
        out_specs=[tile, tile], out_shape=[shp, shp],
        compiler_params=_cparams(("parallel", "parallel")),
    )(df, w_out_t, gt, up)


def _shift_down(x, prev8, j):
    if j == 0:
        return x
    xr = pltpu.roll(x, j, 0)
    pr = pltpu.roll(prev8, j, 0)
    row = lax.broadcasted_iota(jnp.int32, (8, x.shape[1]), 0)
    head = jnp.where(row < j, pr, xr[:8])
    return head if x.shape[0] == 8 else jnp.concatenate([head, xr[8:]], axis=0)


def _shift_up(x, next8, j):
    if j == 0:
        return x
    n = x.shape[0]
    xr = pltpu.roll(x, n - j, 0)
    nr = pltpu.roll(next8, 8 - j, 0)
    row = lax.broadcasted_iota(jnp.int32, (8, x.shape[1]), 0)
    return jnp.concatenate([xr[:n - 8], jnp.where(row >= 8 - j, nr, xr[n - 8:])], axis=0)


def _conv_taps(x, prev8, w_ref, taps):
    out = None
    for k in range(taps):
        term = w_ref[k:k + 1, :] * _shift_down(x, prev8, taps - 1 - k)
        out = term if out is None else out + term
    return out


def post_inproj(p, sc_w, ssm_w, ssm_b, name, tr=512):
    t = p.shape[0]

    def body(sc_ref, scp_ref, qkv_ref, xbc_ref, xbcp_ref, scw_ref, sw_ref, sb_ref, ya_ref, qkvo_ref, act_ref):
        first = (pl.program_id(0) > 0).astype(F32)
        sc = sc_ref[...]
        scp = scp_ref[...] * first
        u = sc[:, 256:512] * sc[:, 512:768]
        up = scp[:, 256:512] * scp[:, 512:768]
        ya_ref[...] = (sc[:, 0:256] * _conv_taps(u, up, scw_ref, 3)).astype(BF16)
        qkv = qkv_ref[...]
        qkvo_ref[:, 0:256] = (qkv[:, 0:256] * 0.125).astype(BF16)
        qkvo_ref[:, 256:768] = qkv[:, 256:768].astype(BF16)
        xc = _conv_taps(xbc_ref[...], xbcp_ref[...] * first, sw_ref, 4) + sb_ref[...]
        act_ref[...] = _silu(xc)

    return rows_call(
        name, body, t, tr,
        [(p, ("row", 768, OFF_SC // 768)), (p, ("prev8", 768, OFF_SC // 768)), (p, ("row", 768, OFF_QKV // 768)),
         (p, ("row", 768, OFF_XBC // 768)), (p, ("prev8", 768, OFF_XBC // 768)),
         (sc_w, "full"), (ssm_w, "full"), (ssm_b, "full")],
        [((t, 256), BF16, "row"), ((t, 768), BF16, "row"), ((t, 768), F32, "row")])


def branch_out_fwd(ya, yb, yc, p, w_cat, name, tr=256):
    t = p.shape[0]

    def body(ya_ref, yb_ref, yc_ref, gl_ref, w_ref, o_ref):
        y_a = jnp.dot(ya_ref[...], w_ref[0:256, :], preferred_element_type=F32)
        y_b = jnp.dot(yb_ref[...].astype(BF16), w_ref[256:512, :], preferred_element_type=F32)
        y_c = jnp.dot(yc_ref[...], w_ref[512:1024, :], preferred_element_type=F32)
        m = (_sigmoid(gl_ref[:, 0:1024]) * y_a + _sigmoid(gl_ref[:, 1024:2048]) * y_b
             + _sigmoid(gl_ref[:, 2048:3072]) * y_c)
        o_ref[...] = m.astype(BF16)

    return rows_call(name, body, t, tr,
                     [(ya, "row"), (yb, "row"), (yc, "row"), (p, ("row", 3072, 0)), (w_cat, "full")],
                     [((t, D_MODEL), BF16, "row")])[0]


def branch_out_bwd(dm, ya, yb, yc, p, w_cat, w_cat_t, name, tr=256):
    t = p.shape[0]
    tn_dims = (((0,), (0,)), ((), ()))

    def body(dm_ref, ya_ref, yb_ref, yc_ref, gl_ref, w_ref, wt_ref, dgl_ref, dya_ref, dyb_ref, dyc_ref, dw_ref):
        @pl.when(pl.program_id(0) == 0)
        def _():
            dw_ref[...] = jnp.zeros_like(dw_ref)

        dmv = dm_ref[...]
        ins = (ya_ref[...], yb_ref[...].astype(BF16), yc_ref[...])
        rows = ((0, 256), (256, 512), (512, 1024))
        outs = (dya_ref, dyb_ref, dyc_ref)
        for i in range(3):
            r0, r1 = rows[i]
            y = jnp.dot(ins[i], w_ref[r0:r1, :], preferred_element_type=F32)
            s = _sigmoid(gl_ref[:, 1024 * i:1024 * (i + 1)])
            dgl_ref[:, 1024 * i:1024 * (i + 1)] = (dmv * y * s * (1.0 - s)).astype(BF16)
            dy = (dmv * s).astype(BF16)
            outs[i][...] = jnp.dot(dy, wt_ref[:, r0:r1], preferred_element_type=F32)
            dw_ref[r0:r1, :] += lax.dot_general(ins[i], dy, tn_dims, preferred_element_type=F32)

    return rows_call(name, body, t, tr,
                     [(dm, "row"), (ya, "row"), (yb, "row"), (yc, "row"), (p, ("row", 3072, 0)), (w_cat, "full"),
                      (w_cat_t, "full")],
                     [((t, 3072), BF16, "row"), ((t, 256), F32, "row"), ((t, 256), F32, "row"), ((t, 512), F32, "row"),
                      ((D_MODEL, D_MODEL), F32, "acc")])


def assemble_dp(dgl, dya, p, sc_w, dq, dk, dv, dact, ssm_w, ssm_b, ddt, dz, name, tr=256):
    t = p.shape[0]
    n_tiles = t // tr
    sci, xi = OFF_SC // 768, OFF_XBC // 768

    def body(dgl_ref, dya_ref, dyan_ref, sc_ref, scp_ref, scn_ref, scw_ref, dq_ref, dk_ref, dv_ref,
             dact_ref, dactn_ref, xbc_ref, xbcp_ref, xbcn_ref, sw_ref, sb_ref, ddt_ref, dz_ref,
             o_ref, dscw_ref, dsw_ref, dsb_ref):
        i = pl.program_id(0)

        @pl.when(i == 0)
        def _():
            dscw_ref[...] = jnp.zeros_like(dscw_ref)
            dsw_ref[...] = jnp.zeros_like(dsw_ref)
            dsb_ref[...] = jnp.zeros_like(dsb_ref)

        first = (i > 0).astype(F32)
        last = (i < n_tiles - 1).astype(F32)
        o_ref[:, 0:3072] = dgl_ref[...]
        sc = sc_ref[...]
        scp = scp_ref[...] * first
        scn = scn_ref[...] * last
        u = sc[:, 256:512] * sc[:, 512:768]
        up = scp[:, 256:512] * scp[:, 512:768]
        dya_v = dya_ref[...]
        cv = _conv_taps(u, up, scw_ref, 3)
        o_ref[:, OFF_SC:OFF_SC + 256] = (dya_v * cv).astype(BF16)
        dcv = dya_v * sc[:, 0:256]
        dcvn = dyan_ref[...] * last * scn[:, 0:256]
        du = None
        for k in range(3):
            sh = 2 - k
            term = scw_ref[k:k + 1, :] * _shift_up(dcv, dcvn, sh)
            du = term if du is None else du + term
            dscw_ref[k:k + 1, :] += jnp.sum(dcv * _shift_down(u, up, sh), axis=0, keepdims=True)
        o_ref[:, OFF_SC + 256:OFF_SC + 512] = (du * sc[:, 512:768]).astype(BF16)
        o_ref[:, OFF_SC + 512:OFF_SC + 768] = (du * sc[:, 256:512]).astype(BF16)
        o_ref[:, OFF_QKV:OFF_QKV + 256] = (dq_ref[...] * 0.125).astype(BF16)
        o_ref[:, OFF_QKV + 256:OFF_QKV + 512] = dk_ref[...].astype(BF16)
        o_ref[:, OFF_QKV + 512:OFF_QKV + 768] = dv_ref[...].astype(BF16)
        xb = xbc_ref[...]
        xbp = xbcp_ref[...] * first
        xbn = xbcn_ref[...]
        xc = _conv_taps(xb, xbp, sw_ref, 4) + sb_ref[...]
        xcn = _conv_taps(xbn, xb[tr - 8:, :], sw_ref, 4) + sb_ref[...]
        dxc = dact_ref[...] * _dsilu(xc)
        dxcn = dactn_ref[...] * _dsilu(xcn) * last
        dxb = None
        for k in range(4):
            sh = 3 - k
            term = sw_ref[k:k + 1, :] * _shift_up(dxc, dxcn, sh)
            dxb = term if dxb is None else dxb + term
            dsw_ref[k:k + 1, :] += jnp.sum(dxc * _shift_down(xb, xbp, sh), axis=0, keepdims=True)
        dsb_ref[...] += jnp.sum(dxc, axis=0, keepdims=True)
        o_ref[:, OFF_XBC:OFF_XBC + 768] = dxb.astype(BF16)
        o_ref[:, OFF_DT:OFF_DT + 128] = ddt_ref[...].astype(BF16)
        o_ref[:, OFF_DT + 128:OFF_Z] = jnp.zeros((tr, OFF_Z - OFF_DT - 128), BF16)
        o_ref[:, OFF_Z:IN_PAD] = dz_ref[...].astype(BF16)

    return rows_call(
        name, body, t, tr,
        [(dgl, "row"), (dya, "row"), (dya, ("next8", 256, 0)),
         (p, ("row", 768, sci)), (p, ("prev8", 768, sci)), (p, ("next8", 768, sci)), (sc_w, "full"),
         (dq, "row"), (dk, "row"), (dv, "row"),
         (dact, "row"), (dact, ("next8", 768, 0)),
         (p, ("row", 768, xi)), (p, ("prev8", 768, xi)), (p, ("next8", 768, xi)), (ssm_w, "full"), (ssm_b, "full"),
         (ddt, "row"), (dz, "row")],
        [((t, IN_PAD), BF16, "row"), ((8, 256), F32, "acc"), ((8, 768), F32, "acc"), ((1, 768), F32, "acc")])


def adamw_flat(slots, w, m, v, name, tr=512):
    n_slots, rows, lanes = slots.shape
    tr = max(d for d in range(8, min(tr, rows) + 1, 8) if rows % d == 0) if rows % 8 == 0 else rows
    bc1 = 1.0 - ADAM_B1 ** ADAM_STEP
    bc2 = 1.0 - ADAM_B2 ** ADAM_STEP

    def body(s_ref, w_ref, m_ref, v_ref, g_ref, d_ref, mo_ref, vo_ref):
        g = s_ref[0].astype(F32)
        for k in range(1, n_slots):
            g = g + s_ref[k].astype(F32)
        mn = ADAM_B1 * m_ref[...] + (1.0 - ADAM_B1) * g
        vn = ADAM_B2 * v_ref[...] + (1.0 - ADAM_B2) * (g * g)
        m_hat = mn / bc1
        v_hat = vn / bc2
        g_ref[...] = g
        d_ref[...] = -ADAM_LR * (m_hat / (jnp.sqrt(v_hat) + ADAM_EPS) + ADAM_WD * w_ref[...])
        mo_ref[...] = mn
        vo_ref[...] = vn

    tile = pl.BlockSpec((tr, lanes), lambda i: (i, 0))
    shp = jax.ShapeDtypeStruct((rows, lanes), F32)
    return pl.pallas_call(
        body, name=name, grid=(rows // tr,),
        in_specs=[pl.BlockSpec((n_slots, tr, lanes), lambda i: (0, i, 0)), tile, tile, tile],
        out_specs=[tile] * 4, out_shape=[shp] * 4,
        compiler_params=_cparams(("parallel",)),
    )(slots, w, m, v)


def _split_dot(x, tri):
    hi = x.astype(BF16)
    lo = (x - hi.astype(F32)).astype(BF16)
    return jnp.dot(hi, tri, preferred_element_type=F32) + jnp.dot(lo, tri, preferred_element_type=F32)


_NT = (((1,), (1,)), ((), ()))
_TN = (((0,), (0,)), ((), ()))

SBA_EXP_ZERO = -104.0
SBA_SKIPPED = -1e30


def sba_fwd(qkv, name, bq=256, bk=256):
    t = qkv.shape[0]
    ratio = bq // bk
    assert t // bk <= LANES

    def body(q_ref, k_ref, v_ref, o_ref, runs_ref, acc_s, run_s):
        i = pl.program_id(1)
        lane = lax.broadcasted_iota(jnp.int32, (1, LANES), 1)
        lane_q = lax.broadcasted_iota(jnp.int32, (bq, LANES), 1)
        qi = lax.broadcasted_iota(jnp.int32, (bq, bk), 0) + i * bq
        kj = lax.broadcasted_iota(jnp.int32, (bq, bk), 1)
        later = (lax.broadcasted_iota(jnp.int32, (bk, bk), 0) > lax.broadcasted_iota(jnp.int32, (bk, bk), 1)).astype(BF16)
        qv = q_ref[...]
        qms = [jnp.where(hm, qv, jnp.zeros_like(qv)) for hm in (lane < 64, lane >= 64)]
        acc_s[...] = jnp.zeros_like(acc_s)
        run_s[...] = jnp.zeros_like(run_s)
        runs_ref[...] = jnp.full(runs_ref.shape, SBA_SKIPPED, F32)

        def tile(j, masked):
            start = pl.multiple_of(j * bk, bk)
            kb = k_ref[pl.ds(start, bk), :]
            vb = v_ref[pl.ds(start, bk), :]
            heads = range(2)
            mask = (kj + j * bk) < qi if masked else None
            s = [lax.dot_general(qms[hh], kb, _NT, preferred_element_type=F32) for hh in heads]
            lk = [_log_sigmoid_neg(s[hh]) for hh in heads]
            if masked:
                lk = [jnp.where(mask, lk[hh], 0.0) for hh in heads]
            w = [jnp.dot(lk[hh].astype(BF16), later, preferred_element_type=F32) for hh in heads]
            run = [run_s[hh] for hh in heads]
            a = [jnp.exp(s[hh] + lk[hh] + w[hh] + run[hh]) for hh in heads]
            if masked:
                a = [jnp.where(mask, a[hh], 0.0) for hh in heads]
            top = None
            for hh in heads:
                acc_s[hh] += jnp.dot(a[hh].astype(BF16), vb, preferred_element_type=F32)
                runs_ref[hh] = jnp.where(lane_q == j, run[hh], runs_ref[hh])
                new_run = run[hh] + jnp.sum(lk[hh], axis=1, keepdims=True)
                run_s[hh] = new_run
                top = jnp.max(new_run) if top is None else jnp.maximum(top, jnp.max(new_run))
            return top

        n_kb = (i + 1) * ratio
        top = None
        for d in range(ratio):
            top = tile(n_kb - 1 - d, True)

        def cond(state):
            n, live = state
            return jnp.logical_and(n < i * ratio, live)

        def step(state):
            n, _ = state
            return n + 1, tile(i * ratio - 1 - n, False) >= SBA_EXP_ZERO

        lax.while_loop(cond, step, (jnp.int32(0), top >= SBA_EXP_ZERO))
        o_ref[...] = jnp.where(lane < 64, acc_s[0], acc_s[1])

    return pl.pallas_call(
        body, name=name, grid=(2, t // bq),
        in_specs=[pl.BlockSpec((bq, LANES), lambda p, i: (i, p)), pl.BlockSpec((t, LANES), lambda p, i: (0, 2 + p)),
                  pl.BlockSpec((t, LANES), lambda p, i: (0, 4 + p))],
        out_specs=[pl.BlockSpec((bq, LANES), lambda p, i: (i, p)), pl.BlockSpec((2, bq, LANES), lambda p, i: (p, i, 0))],
        out_shape=[jax.ShapeDtypeStruct((t, SB_WIDTH), F32), jax.ShapeDtypeStruct((4, t, LANES), F32)],
        scratch_shapes=[pltpu.VMEM((2, bq, LANES), F32), pltpu.VMEM((2, bq, 1), F32)],
        compiler_params=_cparams(("parallel", "parallel")),
    )(qkv, qkv, qkv)


def sba_bwd(qkv, runs, do, name, bq=256, bk=256):
    t = qkv.shape[0]
    ratio = bq // bk
    nq = t // bq

    def body(q_ref, k_ref, v_ref, runs_ref, do_ref, dq_ref, dk_hbm, dv_hbm, dk_s, dv_s, sem, dq_s, rg_s):
        p = pl.program_id(0)
        i = pl.program_id(1)

        @pl.when(i == 0)
        def _():
            dk_s[...] = jnp.zeros_like(dk_s)
            dv_s[...] = jnp.zeros_like(dv_s)

        lane = lax.broadcasted_iota(jnp.int32, (1, LANES), 1)
        qi = lax.broadcasted_iota(jnp.int32, (bq, bk), 0) + i * bq
        kj = lax.broadcasted_iota(jnp.int32, (bq, bk), 1)
        r2 = lax.broadcasted_iota(jnp.int32, (bk, bk), 0)
        c2 = lax.broadcasted_iota(jnp.int32, (bk, bk), 1)
        later = (r2 > c2).astype(BF16)
        earlier = (r2 < c2).astype(BF16)
        qv = q_ref[...]
        dov = do_ref[...]
        heads = range(2)
        hms = (lane < 64, lane >= 64)
        qms = [jnp.where(hm, qv, jnp.zeros_like(qv)) for hm in hms]
        doms = [jnp.where(hm, dov, 0.0).astype(BF16) for hm in hms]
        runs = [runs_ref[hh] for hh in heads]
        dq_s[...] = jnp.zeros_like(dq_s)
        rg_s[...] = jnp.zeros_like(rg_s)

        def tile(j, masked):
            start = pl.multiple_of(j * bk, bk)
            kb = k_ref[pl.ds(start, bk), :]
            vb = v_ref[pl.ds(start, bk), :]
            mask = (kj + j * bk) < qi if masked else None
            s = [lax.dot_general(qms[hh], kb, _NT, preferred_element_type=F32) for hh in heads]
            da = [lax.dot_general(doms[hh], vb, _NT, preferred_element_type=F32) for hh in heads]
            lk_raw = [_log_sigmoid_neg(s[hh]) for hh in heads]
            lk = [jnp.where(mask, lk_raw[hh], 0.0) for hh in heads] if masked else lk_raw
            w = [jnp.dot(lk[hh].astype(BF16), later, preferred_element_type=F32) for hh in heads]
            run = [jnp.sum(jnp.where(lane == j, runs[hh], 0.0), axis=1, keepdims=True) for hh in heads]
            a = [jnp.exp(s[hh] + lk[hh] + w[hh] + run[hh]) for hh in heads]
            if masked:
                a = [jnp.where(mask, a[hh], 0.0) for hh in heads]
            g = [a[hh] * da[hh] for hh in heads]
            rg = [rg_s[hh] for hh in heads]
            c = [rg[hh] + _split_dot(g[hh], earlier) for hh in heads]
            dz = [g[hh] - jnp.exp(s[hh] + lk_raw[hh]) * (g[hh] + c[hh]) for hh in heads]
            if masked:
                dz = [jnp.where(mask, dz[hh], 0.0) for hh in heads]
            dz = [dz[hh].astype(BF16) for hh in heads]
            for hh in heads:
                dq_s[hh] += jnp.dot(dz[hh], kb, preferred_element_type=F32)
                rg_s[hh] = rg[hh] + jnp.sum(g[hh], axis=1, keepdims=True)
            dk = lax.dot_general(dz[0], qms[0], _TN, preferred_element_type=F32)
            dk_s[pl.ds(start, bk), :] += dk + lax.dot_general(dz[1], qms[1], _TN, preferred_element_type=F32)
            dv = lax.dot_general(a[0].astype(BF16), doms[0], _TN, preferred_element_type=F32)
            dv_s[pl.ds(start, bk), :] += dv + lax.dot_general(a[1].astype(BF16), doms[1], _TN, preferred_element_type=F32)

        live = jnp.maximum(jnp.max(runs[0], axis=0, keepdims=True), jnp.max(runs[1], axis=0, keepdims=True)) >= SBA_EXP_ZERO
        first = jnp.minimum(jnp.min(jnp.where(live, lane, LANES)), i * ratio)

        def step(j, carry):
            tile(j, False)
            return carry

        lax.fori_loop(first, i * ratio, step, 0)
        for d in range(ratio):
            tile(i * ratio + d, True)
        dq_ref[...] = jnp.where(lane < 64, dq_s[0], dq_s[1])

        @pl.when(i == nq - 1)
        def _():
            col = pl.multiple_of(p * LANES, LANES)
            ck = pltpu.make_async_copy(dk_s, dk_hbm.at[:, pl.ds(col, LANES)], sem.at[0])
            cv = pltpu.make_async_copy(dv_s, dv_hbm.at[:, pl.ds(col, LANES)], sem.at[1])
            ck.start()
            cv.start()
            ck.wait()
            cv.wait()

    shp = jax.ShapeDtypeStruct((t, SB_WIDTH), F32)
    tile = pl.BlockSpec((bq, LANES), lambda p, i: (i, p))
    return pl.pallas_call(
        body, name=name, grid=(2, nq),
        in_specs=[tile, pl.BlockSpec((t, LANES), lambda p, i: (0, 2 + p)), pl.BlockSpec((t, LANES), lambda p, i: (0, 4 + p)),
                  pl.BlockSpec((2, bq, LANES), lambda p, i: (p, i, 0)), tile],
        out_specs=[tile, pl.BlockSpec(memory_space=pl.ANY), pl.BlockSpec(memory_space=pl.ANY)],
        out_shape=[shp, shp, shp],
        scratch_shapes=[pltpu.VMEM((t, LANES), F32), pltpu.VMEM((t, LANES), F32), pltpu.SemaphoreType.DMA((2,)),
                        pltpu.VMEM((2, bq, LANES), F32), pltpu.VMEM((2, bq, 1), F32)],
        compiler_params=_cparams(("arbitrary", "arbitrary")),
    )(qkv, qkv, qkv, runs, do)


def _ssd_consts():
    ln = SSM_CHUNK
    ri = lax.broadcasted_iota(jnp.int32, (ln, ln), 0)
    ci = lax.broadcasted_iota(jnp.int32, (ln, ln), 1)
    eh = lax.broadcasted_iota(jnp.int32, (LANES, SSM_INNER), 0)
    el = lax.broadcasted_iota(jnp.int32, (LANES, SSM_INNER), 1)
    expand = (jnp.right_shift(el, 6) == eh).astype(F32)
    th = lax.broadcasted_iota(jnp.int32, (SSM_INNER, LANES), 1)
    tl = lax.broadcasted_iota(jnp.int32, (SSM_INNER, LANES), 0)
    reduce = (jnp.right_shift(tl, 6) == th).astype(F32)
    return ri, ci, expand, reduce


def _dot_hi(a, b):
    return jnp.dot(a, b, precision=HI, preferred_element_type=F32)


def _ssd_prelude(xbc_ref, dt_ref, dtt_ref, hpr_ref, hpc_ref, ri, ci, expand):
    ln = SSM_CHUNK
    xs = xbc_ref[:, 0:512]
    bm = xbc_ref[:, 512:640]
    cm = xbc_ref[:, 640:768]
    dtb_r = hpr_ref[0:1, :]
    aneg_r = -jnp.exp(hpr_ref[1:2, :])
    pre = dt_ref[...] + dtb_r
    dt = _softplus(pre)
    a = dt * aneg_r
    dtt = _softplus(dtt_ref[...] + hpc_ref[0:8, :])
    att = dtt * (-jnp.exp(hpc_ref[8:16, :]))
    tril = (ri >= ci).astype(F32)
    triu = (ri <= ci).astype(F32)
    acs = _dot_hi(tril, a)
    acst = _dot_hi(att, triu)
    acs_e = _dot_hi(acs, expand)
    dt_e = _dot_hi(dt, expand)
    last_e = acs_e[ln - 1:ln, :]
    e_e = jnp.exp(acs_e)
    w_e = jnp.exp(last_e - acs_e)
    dec_e = jnp.exp(last_e)
    xdt = xs * dt_e
    return dict(xs=xs, bm=bm, cm=cm, pre=pre, dt=dt, aneg_r=aneg_r, acs=acs, acst=acst, dt_e=dt_e, e_e=e_e,
                w_e=w_e, dec_e=dec_e, xdt=xdt, triu=triu)


def ssd_fwd(act, p, dtt, hp_rows, hp_cols, d_e, norm_w, name):
    t = act.shape[0]
    ln = SSM_CHUNK
    nc = t // ln

    def body(xbc_ref, dt_ref, z_ref, dtt_ref, hpr_ref, hpc_ref, d_ref, nw_ref, yc_ref, y_ref, sto_ref, st):
        @pl.when(pl.program_id(0) == 0)
        def _():
            st[...] = jnp.zeros_like(st)

        ri, ci, expand, _ = _ssd_consts()
        q = _ssd_prelude(xbc_ref, dt_ref, dtt_ref, hpr_ref, hpc_ref, ri, ci, expand)
        lane = lax.broadcasted_iota(jnp.int32, (1, LANES), 1)
        rown = lax.broadcasted_iota(jnp.int32, (LANES, 1), 0)
        low = lane < 64
        mask = ri >= ci
        xdt_b = q["xdt"].astype(BF16)
        xw_b = (q["xdt"] * q["w_e"]).astype(BF16)
        bt = q["bm"].T.astype(BF16)
        cb_ = q["cm"].astype(BF16)
        y_pairs = []
        for g in range(2):
            gm = low if g == 0 else jnp.logical_not(low)
            rm = (rown < 64) if g == 0 else (rown >= 64)
            cg = jnp.where(gm, cb_, jnp.zeros_like(cb_))
            cb = jnp.dot(cg, bt, preferred_element_type=F32)
            for pp in range(2):
                pi = 2 * g + pp
                sl = slice(LANES * pi, LANES * (pi + 1))
                xp = xdt_b[:, sl]
                yd = []
                for hh in range(2):
                    h = 2 * pi + hh
                    diff = q["acs"][:, h:h + 1] - q["acst"][h:h + 1, :]
                    lam = jnp.exp(jnp.where(mask, diff, -jnp.inf))
                    yd.append(jnp.dot((cb * lam).astype(BF16), xp, preferred_element_type=F32))
                sp = st[pi]
                sto_ref[0, pi] = sp
                yoff = jnp.dot(cg, sp.astype(BF16), preferred_element_type=F32) * q["e_e"][:, sl]
                upd = jnp.dot(bt, xw_b[:, sl], preferred_element_type=F32)
                st[pi] = q["dec_e"][:, sl] * sp + jnp.where(rm, upd, 0.0)
                y_pairs.append(jnp.where(low, yd[0], yd[1]) + yoff)
        y = jnp.concatenate(y_pairs, axis=1) + q["xs"] * d_ref[...]
        y_ref[...] = y
        yg = y * _silu(z_ref[...])
        for g in range(2):
            sl = slice(256 * g, 256 * (g + 1))
            seg = yg[:, sl]
            yc_ref[:, sl] = (seg * _rstd(seg) * nw_ref[:, sl]).astype(BF16)

    return pl.pallas_call(
        body, name=name, grid=(nc,),
        in_specs=[pl.BlockSpec((ln, 768), lambda c: (c, 0)), pl.BlockSpec((ln, LANES), lambda c: (c, OFF_DT // LANES)),
                  pl.BlockSpec((ln, 512), lambda c: (c, OFF_Z // 512)), pl.BlockSpec((8, ln), lambda c: (0, c)),
                  pl.BlockSpec((8, LANES), lambda c: (0, 0)), pl.BlockSpec((16, ln), lambda c: (0, 0)),
                  pl.BlockSpec((1, 512), lambda c: (0, 0)), pl.BlockSpec((1, 512), lambda c: (0, 0))],
        out_specs=[pl.BlockSpec((ln, 512), lambda c: (c, 0)), pl.BlockSpec((ln, 512), lambda c: (c, 0)),
                   pl.BlockSpec((1, 4, LANES, LANES), lambda c: (c, 0, 0, 0))],
        out_shape=[jax.ShapeDtypeStruct((t, 512), BF16), jax.ShapeDtypeStruct((t, 512), F32),
                   jax.ShapeDtypeStruct((nc, 4, LANES, LANES), F32)],
        scratch_shapes=[pltpu.VMEM((4, LANES, LANES), F32)],
        compiler_params=_cparams(("arbitrary",)),
    )(act, p, p, dtt, hp_rows, hp_cols, d_e, norm_w)


def ssd_bwd(dyc, y, act, p, dtt, states, hp_rows, hp_cols, d_e, norm_w, name):
    t = act.shape[0]
    ln = SSM_CHUNK
    nc = t // ln

    def body(dyc_ref, y_ref, xbc_ref, dt_ref, z_ref, dtt_ref, st_ref, hpr_ref, hpc_ref, d_ref, nw_ref,
             dz_ref, dact_ref, ddt_ref, dnw_ref, dd_ref, dhp_ref, ds):
        @pl.when(pl.program_id(0) == 0)
        def _():
            ds[...] = jnp.zeros_like(ds)
            dnw_ref[...] = jnp.zeros_like(dnw_ref)
            dd_ref[...] = jnp.zeros_like(dd_ref)
            dhp_ref[...] = jnp.zeros_like(dhp_ref)

        ri, ci, expand, reduce = _ssd_consts()
        q = _ssd_prelude(xbc_ref, dt_ref, dtt_ref, hpr_ref, hpc_ref, ri, ci, expand)
        lane = lax.broadcasted_iota(jnp.int32, (1, LANES), 1)
        rown = lax.broadcasted_iota(jnp.int32, (LANES, 1), 0)
        low = lane < 64
        mask = ri >= ci
        mask_t = ci >= ri
        xs, xdt, acs, acst = q["xs"], q["xdt"], q["acs"], q["acst"]
        yv, zv, nw = y_ref[...], z_ref[...], nw_ref[...]
        sg = _sigmoid(zv)
        zz = zv * sg
        yg = yv * zz
        dycv = dyc_ref[...]
        u = dycv * nw
        dyg_parts, dnw_parts = [], []
        for g in range(2):
            sl = slice(256 * g, 256 * (g + 1))
            seg = yg[:, sl]
            rr = _rstd(seg)
            nrm = seg * rr
            dyg_parts.append(rr * (u[:, sl] - nrm * jnp.mean(nrm * u[:, sl], axis=-1, keepdims=True)))
            dnw_parts.append(jnp.sum(dycv[:, sl] * nrm, axis=0, keepdims=True))
        dyg = jnp.concatenate(dyg_parts, axis=1)
        dnw_ref[...] += jnp.concatenate(dnw_parts, axis=1)
        dy = dyg * zz
        dz_ref[...] = dyg * yv * (sg * (1.0 + zv * (1.0 - sg)))
        dd_ref[...] += jnp.sum(dy * xs, axis=0, keepdims=True)
        dxs = dy * d_ref[...]
        dy_b = dy.astype(BF16)
        xdt_b = xdt.astype(BF16)
        xw_b = (xdt * q["w_e"]).astype(BF16)
        bt = q["bm"].T.astype(BF16)
        ct = q["cm"].T.astype(BF16)
        cb_ = q["cm"].astype(BF16)
        bb_ = q["bm"].astype(BF16)
        dacs = jnp.zeros((ln, LANES), F32)
        dc = jnp.zeros((ln, LANES), F32)
        db = jnp.zeros((ln, LANES), F32)
        dxdt_pairs, yoffdy_pairs, dwe_pairs, ddec_pairs = [], [], [], []
        for g in range(2):
            gm = low if g == 0 else jnp.logical_not(low)
            rm = (rown < 64) if g == 0 else (rown >= 64)
            cg = jnp.where(gm, cb_, jnp.zeros_like(cb_))
            bg = jnp.where(gm, bb_, jnp.zeros_like(bb_))
            cb = jnp.dot(cg, bt, preferred_element_type=F32)
            cbt = jnp.dot(bg, ct, preferred_element_type=F32)
            dcb = jnp.zeros((ln, ln), F32)
            dcbt = jnp.zeros((ln, ln), F32)
            for pp in range(2):
                pi = 2 * g + pp
                sl = slice(LANES * pi, LANES * (pi + 1))
                xp = xdt_b[:, sl]
                dyp = dy_b[:, sl]
                xpt = xdt[:, sl].T.astype(BF16)
                dypt = dy[:, sl].T.astype(BF16)
                dxdt_p = jnp.zeros((ln, LANES), F32)
                for hh in range(2):
                    h = 2 * pi + hh
                    hm = low if hh == 0 else jnp.logical_not(low)
                    col = acs[:, h:h + 1]
                    row = acst[h:h + 1, :]
                    lam = jnp.exp(jnp.where(mask, col - row, -jnp.inf))
                    lam_t = jnp.exp(jnp.where(mask_t, row - col, -jnp.inf))
                    m = cb * lam
                    m_t = cbt * lam_t
                    dyh = jnp.where(hm, dyp, jnp.zeros_like(dyp))
                    xh = jnp.where(hm, xp, jnp.zeros_like(xp))
                    dm = jnp.dot(dyh, xpt, preferred_element_type=F32)
                    dm_t = jnp.dot(xh, dypt, preferred_element_type=F32)
                    dcb = dcb + dm * lam
                    dcbt = dcbt + dm_t * lam_t
                    rs = jnp.sum(dm * m, axis=1, keepdims=True) - jnp.sum(dm_t * m_t, axis=1, keepdims=True)
                    dacs = dacs + jnp.where(lane == h, rs, 0.0)
                    dxdt_p = dxdt_p + jnp.dot(m_t.astype(BF16), dyh, preferred_element_type=F32)
                sp = st_ref[0, pi]
                sp_b = sp.astype(BF16)
                dsn = ds[pi]
                dsn_b = dsn.astype(BF16)
                e_p, w_p, dec_p = q["e_e"][:, sl], q["w_e"][:, sl], q["dec_e"][:, sl]
                yoff = jnp.dot(cg, sp_b, preferred_element_type=F32) * e_p
                dyo = dy[:, sl] * e_p
                dyo_b = dyo.astype(BF16)
                dc = dc + lax.dot_general(dyo_b, sp_b, _NT, preferred_element_type=F32)
                ds_prev = dec_p * dsn + jnp.where(rm, jnp.dot(ct, dyo_b, preferred_element_type=F32), 0.0)
                yoffdy_pairs.append(dy[:, sl] * yoff)
                dxw = jnp.dot(bg, dsn_b, preferred_element_type=F32)
                db = db + lax.dot_general(xw_b[:, sl], dsn_b, _NT, preferred_element_type=F32)
                dxdt_p = dxdt_p + dxw * w_p
                dwe_pairs.append(dxw * xdt[:, sl])
                ddec_pairs.append(jnp.sum(dsn * sp, axis=0, keepdims=True))
                ds[pi] = ds_prev
                dxdt_pairs.append(dxdt_p)
            dc = dc + jnp.dot(dcb.astype(BF16), bg, preferred_element_type=F32)
            db = db + jnp.dot(dcbt.astype(BF16), cg, preferred_element_type=F32)
        dxdt = jnp.concatenate(dxdt_pairs, axis=1)
        yoffdy = jnp.concatenate(yoffdy_pairs, axis=1)
        dwe = jnp.concatenate(dwe_pairs, axis=1)
        ddec_e = jnp.broadcast_to(jnp.concatenate(ddec_pairs, axis=1), (8, SSM_INNER))
        last = acs[ln - 1:ln, :]
        w_col = jnp.exp(last - acs)
        dw_col = _dot_hi(dwe, reduce) * w_col
        dacs = dacs + _dot_hi(yoffdy, reduce) - dw_col
        dlast = jnp.sum(dw_col, axis=0, keepdims=True) + jnp.exp(last) * _dot_hi(ddec_e, reduce)[0:1, :]
        rowi = lax.broadcasted_iota(jnp.int32, (ln, 1), 0)
        dacs = dacs + jnp.where(rowi == ln - 1, dlast, 0.0)
        da = _dot_hi(q["triu"], dacs)
        ddt = da * q["aneg_r"] + _dot_hi(dxdt * xs, reduce)
        ddt_raw = jnp.where(lane < SSM_HEADS, ddt * _sigmoid(q["pre"]), 0.0)
        ddt_ref[...] = ddt_raw
        dhp_ref[0:1, :] += jnp.sum(ddt_raw, axis=0, keepdims=True)
        dhp_ref[1:2, :] += jnp.where(lane < SSM_HEADS, jnp.sum(da * q["dt"], axis=0, keepdims=True) * q["aneg_r"], 0.0)
        dact_ref[:, 0:512] = dxs + dxdt * q["dt_e"]
        dact_ref[:, 512:640] = db
        dact_ref[:, 640:768] = dc

    rev = lambda c: nc - 1 - c
    return pl.pallas_call(
        body, name=name, grid=(nc,),
        in_specs=[pl.BlockSpec((ln, 512), lambda c: (rev(c), 0)), pl.BlockSpec((ln, 512), lambda c: (rev(c), 0)),
                  pl.BlockSpec((ln, 768), lambda c: (rev(c), 0)),
                  pl.BlockSpec((ln, LANES), lambda c: (rev(c), OFF_DT // LANES)),
                  pl.BlockSpec((ln, 512), lambda c: (rev(c), OFF_Z // 512)), pl.BlockSpec((8, ln), lambda c: (0, rev(c))),
                  pl.BlockSpec((1, 4, LANES, LANES), lambda c: (rev(c), 0, 0, 0)),
                  pl.BlockSpec((8, LANES), lambda c: (0, 0)), pl.BlockSpec((16, ln), lambda c: (0, 0)),
                  pl.BlockSpec((1, 512), lambda c: (0, 0)), pl.BlockSpec((1, 512), lambda c: (0, 0))],
        out_specs=[pl.BlockSpec((ln, 512), lambda c: (rev(c), 0)), pl.BlockSpec((ln, 768), lambda c: (rev(c), 0)),
                   pl.BlockSpec((ln, LANES), lambda c: (rev(c), 0)), pl.BlockSpec((1, 512), lambda c: (0, 0)),
                   pl.BlockSpec((1, 512), lambda c: (0, 0)), pl.BlockSpec((8, LANES), lambda c: (0, 0))],
        out_shape=[jax.ShapeDtypeStruct((t, 512), F32), jax.ShapeDtypeStruct((t, 768), F32),
                   jax.ShapeDtypeStruct((t, LANES), F32), jax.ShapeDtypeStruct((1, 512), F32),
                   jax.ShapeDtypeStruct((1, 512), F32), jax.ShapeDtypeStruct((8, LANES), F32)],
        scratch_shapes=[pltpu.VMEM((4, LANES, LANES), F32)],
        compiler_params=_cparams(("arbitrary",)),
    )(dyc, y, act, p, p, dtt, states, hp_rows, hp_cols, d_e, norm_w)


def mod_shard_fwd(c_all, mod_w, mod_b_shard, name):
    def body(c_ref, w_ref, b_ref, o_ref):
        sc = _silu(c_ref[...])
        for l in range(DEPTH):
            o_ref[l] = _dot_hi(sc, w_ref[l]) + b_ref[l]

    return pl.pallas_call(body, name=name, out_shape=jax.ShapeDtypeStruct((DEPTH, N_DEV, mod_w.shape[2]), F32),
                          compiler_params=_cparams())(c_all, mod_w, mod_b_shard)


def mod_w_grad(c_all, dmod_shard, name):
    def body(c_ref, d_ref, o_ref):
        sc = _silu(c_ref[...])
        for l in range(DEPTH):
            o_ref[l] = lax.dot_general(sc, d_ref[l], _TN, precision=HI, preferred_element_type=F32)

    return pl.pallas_call(body, name=name, out_shape=jax.ShapeDtypeStruct((DEPTH, D_MODEL, dmod_shard.shape[2]), F32),
                          compiler_params=_cparams())(c_all, dmod_shard)


_BIG = ("w_in", "sc_conv_w", "ssm_conv_w", "w_sc_out", "w_sb_out", "w_ssm_out", "w_o", "w_ffn_in", "w_ffn_out")
_ROW_SHARDED = ("w_o", "w_ffn_out")
_CONV = ("sc_conv_w", "ssm_conv_w")
_SMALL = ("mod_b", "g_pre_mix", "g_post_mix", "g_pre_ffn", "g_post_ffn", "ssm_conv_b", "ssm_dt_bias", "ssm_a_log",
          "ssm_d", "ssm_norm_w")
_WEIGHTS = ("mod_w", "mod_b", "g_pre_mix", "g_post_mix", "g_pre_ffn", "g_post_ffn", "w_in", "sc_conv_w", "ssm_conv_w",
            "ssm_conv_b", "ssm_dt_bias", "ssm_a_log", "ssm_d", "ssm_norm_w", "w_sc_out", "w_sb_out", "w_ssm_out", "w_o",
            "w_ffn_in", "w_ffn_out")


def _gathered_to_full(g, row_sharded):
    _, dep, r, c = g.shape
    if row_sharded:
        return g.transpose(1, 0, 2, 3).reshape(dep, N_DEV * r, c)
    return g.transpose(1, 2, 0, 3).reshape(dep, r, N_DEV * c)


def _full_to_slots(w, row_sharded):
    dep, r, c = w.shape
    if row_sharded:
        return w.reshape(dep, N_DEV, r // N_DEV, c).transpose(1, 0, 2, 3).reshape(N_DEV, dep * (r // N_DEV), c)
    return w.reshape(dep, r, N_DEV, c // N_DEV).transpose(2, 0, 1, 3).reshape(N_DEV, dep * r, c // N_DEV)


def _pad_in_proj(w):
    sc, qkv, z, xbc, dt, gates = (w[:, 0:768], w[:, 768:1536], w[:, 1536:2048], w[:, 2048:2816], w[:, 2816:2824],
                                  w[:, 2824:5896])
    pad = jnp.zeros((w.shape[0], OFF_Z - OFF_DT - 8), w.dtype)
    return jnp.concatenate([gates, sc, qkv, xbc, dt, pad, z], axis=1)


def _unpad_in_proj(w):
    return jnp.concatenate([w[:, OFF_SC:OFF_SC + 768], w[:, OFF_QKV:OFF_QKV + 768], w[:, OFF_Z:OFF_Z + 512],
                            w[:, OFF_XBC:OFF_XBC + 768], w[:, OFF_DT:OFF_DT + 8], w[:, 0:3072]], axis=1)


def _row(v):
    return v.reshape(1, -1)


def _local_step(x, target, mod, small, conv, big):
    lw, saved = [], []
    for l in range(DEPTH):
        w_in_p = _pad_in_proj(big["w_in"][l])
        w_cat = jnp.concatenate([big["w_sc_out"][l], big["w_sb_out"][l], big["w_ssm_out"][l]], axis=0)
        w_ffn_in_t = big["w_ffn_in"][l].T
        hp_rows = jnp.zeros((8, LANES), F32).at[0, :SSM_HEADS].set(small["ssm_dt_bias"][l]).at[1, :SSM_HEADS].set(
            small["ssm_a_log"][l])
        hp_cols = jnp.concatenate([jnp.broadcast_to(small["ssm_dt_bias"][l][:, None], (SSM_HEADS, SSM_CHUNK)),
                                   jnp.broadcast_to(small["ssm_a_log"][l][:, None], (SSM_HEADS, SSM_CHUNK))], axis=0)
        lw.append(dict(
            w_in_p=w_in_p, w_in_pt=w_in_p.T, w_cat=w_cat, w_cat_t=w_cat.T, w_o=big["w_o"][l], w_o_t=big["w_o"][l].T,
            w_ffn_in=big["w_ffn_in"][l], wg_t=w_ffn_in_t[:FFN_HIDDEN], wu_t=w_ffn_in_t[FFN_HIDDEN:],
            w_ffn_out=big["w_ffn_out"][l], w_ffn_out_t=big["w_ffn_out"][l].T,
            sc_w8=jnp.pad(conv["sc_conv_w"][l], ((0, 5), (0, 0))), ssm_w8=jnp.pad(conv["ssm_conv_w"][l], ((0, 4), (0, 0))),
            ssm_b=_row(small["ssm_conv_b"][l]), hp_rows=hp_rows, hp_cols=hp_cols,
            d_e=_row(jnp.repeat(small["ssm_d"][l], SSM_HEAD_DIM)), norm_w=_row(small["ssm_norm_w"][l]),
            g_pre_mix=_row(small["g_pre_mix"][l]), g_post_mix=_row(small["g_post_mix"][l]),
            g_pre_ffn=_row(small["g_pre_ffn"][l]), g_post_ffn=_row(small["g_post_ffn"][l]),
            shift1=mod[l, 0:1], scale1=mod[l, 1:2], gate1=mod[l, 2:3], shift2=mod[l, 3:4], scale2=mod[l, 4:5],
            gate2=mod[l, 5:6]))

    xl = x
    h = normmod_fwd(xl, lw[0]["g_pre_mix"], lw[0]["scale1"], lw[0]["shift1"], "normmod_fwd_0")
    dy = loss = None
    for l in range(DEPTH):
        w = lw[l]
        p = mm_nn([h], [w["w_in_p"]], F32, f"in_proj_{l}")
        ya, qkv, act = post_inproj(p, w["sc_w8"], w["ssm_w8"], w["ssm_b"], f"post_inproj_{l}")
        o, runs = sba_fwd(qkv, f"sba_fwd_{l}")
        dtt = p[:, OFF_DT:OFF_DT + 8].T
        yc, ypre, states = ssd_fwd(act, p, dtt, w["hp_rows"], w["hp_cols"], w["d_e"], w["norm_w"], f"ssd_fwd_{l}")
        merged = branch_out_fwd(ya, o, yc, p, w["w_cat"], f"branch_fwd_{l}")
        mix = mm_nn([merged], [w["w_o"]], F32, f"out_proj_{l}")
        x1, h2 = resid_normmod_fwd(xl, mix, w["gate1"], w["g_post_mix"], w["g_pre_ffn"], w["scale2"], w["shift2"],
                                   f"resid_mix_{l}")
        gt, up, a = mm_swiglu_fwd(h2, w["w_ffn_in"], f"ffn_in_{l}")
        f = mm_nn([a], [w["w_ffn_out"]], F32, f"ffn_out_{l}")
        saved.append(dict(x=xl, h=h, p=p, ya=ya, qkv=qkv, act=act, o=o, runs=runs, dtt=dtt, yc=yc, ypre=ypre, states=states,
                          merged=merged, mix=mix, x1=x1, h2=h2, gt=gt, up=up, a=a, f=f))
        if l + 1 < DEPTH:
            nw = lw[l + 1]
            xl, h = resid_normmod_fwd(x1, f, w["gate2"], w["g_post_ffn"], nw["g_pre_mix"], nw["scale1"], nw["shift1"],
                                      f"resid_ffn_{l}")
        else:
            dy, loss = resid_loss(x1, f, w["gate2"], w["g_post_ffn"], target, "resid_loss")

    dmod = [None] * DEPTH
    gs = {k: [None] * DEPTH for k in _SMALL + _BIG}
    dxo = dy
    for l in reversed(range(DEPTH)):
        w, s = lw[l], saved[l]
        df, dgate2, gs["g_post_ffn"][l] = resid_bwd(dxo, s["f"], w["gate2"], w["g_post_ffn"], f"resid_ffn_bwd_{l}")
        dgt, dup = mm_swiglu_bwd(df, w["w_ffn_out_t"], s["gt"], s["up"], f"ffn_out_bwd_{l}")
        gs["w_ffn_out"][l] = mm_tn(s["a"], df, f"dw_ffn_out_{l}")
        dh2 = mm_nn([dgt, dup], [w["wg_t"], w["wu_t"]], F32, f"ffn_in_bwd_{l}")
        gs["w_ffn_in"][l] = jnp.concatenate([mm_tn(s["h2"], dgt, f"dw_ffn_gate_{l}"), mm_tn(s["h2"], dup, f"dw_ffn_up_{l}")],
                                            axis=1)
        dx1, dscale2, dshift2, gs["g_pre_ffn"][l] = normmod_bwd(dh2, s["x1"], dxo, w["g_pre_ffn"], w["scale2"],
                                                                f"normmod_ffn_bwd_{l}")
        dmix, dgate1, gs["g_post_mix"][l] = resid_bwd(dx1, s["mix"], w["gate1"], w["g_post_mix"], f"resid_mix_bwd_{l}")
        dmerged = mm_nn([dmix], [w["w_o_t"]], F32, f"out_proj_bwd_{l}")
        gs["w_o"][l] = mm_tn(s["merged"], dmix, f"dw_o_{l}")
        dgl, dya, dyb, dyc, dw_cat = branch_out_bwd(dmerged, s["ya"], s["o"], s["yc"], s["p"], w["w_cat"], w["w_cat_t"],
                                                    f"branch_bwd_{l}")
        gs["w_sc_out"][l], gs["w_sb_out"][l], gs["w_ssm_out"][l] = dw_cat[0:256], dw_cat[256:512], dw_cat[512:1024]
        dz, dact, ddt, dnw, dd_e, dhp = ssd_bwd(dyc, s["ypre"], s["act"], s["p"], s["dtt"], s["states"], w["hp_rows"],
                                                w["hp_cols"], w["d_e"], w["norm_w"], f"ssd_bwd_{l}")
        gs["ssm_norm_w"][l] = dnw[0]
        gs["ssm_d"][l] = dd_e.reshape(SSM_HEADS, SSM_HEAD_DIM).sum(axis=1)
        gs["ssm_dt_bias"][l] = dhp[0, :SSM_HEADS]
        gs["ssm_a_log"][l] = dhp[1, :SSM_HEADS]
        dq, dk, dv = sba_bwd(s["qkv"], s["runs"], dyb, f"sba_bwd_{l}")
        dp, dscw, dssw, dssb = assemble_dp(dgl, dya, s["p"], w["sc_w8"], dq, dk, dv, dact, w["ssm_w8"], w["ssm_b"], ddt, dz,
                                           f"assemble_dp_{l}")
        gs["sc_conv_w"][l], gs["ssm_conv_w"][l], gs["ssm_conv_b"][l] = dscw[0:3], dssw[0:4], dssb[0]
        dh = mm_nn([dp], [w["w_in_pt"]], F32, f"in_proj_bwd_{l}")
        gs["w_in"][l] = _unpad_in_proj(mm_tn(s["h"], dp, f"dw_in_{l}"))
        dxo, dscale1, dshift1, gs["g_pre_mix"][l] = normmod_bwd(dh, s["x"], dx1, w["g_pre_mix"], w["scale1"],
                                                                f"normmod_mix_bwd_{l}")
        dmod[l] = jnp.concatenate([dshift1, dscale1, dgate1, dshift2, dscale2, dgate2], axis=0)
    for k in ("g_pre_mix", "g_post_mix", "g_pre_ffn", "g_post_ffn"):
        gs[k] = [g[0] for g in gs[k]]
    grads = {k: jnp.stack(v) for k, v in gs.items() if k != "mod_b"}
    return loss[0, 0], dxo, jnp.stack(dmod), grads


def kernel(x, c, mod_w, mod_b, g_pre_mix, g_post_mix, g_pre_ffn, g_post_ffn, w_in, sc_conv_w, ssm_conv_w, ssm_conv_b, ssm_dt_bias, ssm_a_log, ssm_d, ssm_norm_w, w_sc_out, w_sb_out, w_ssm_out, w_o, w_ffn_in, w_ffn_out, loss_target, m_mod_w, m_mod_b, m_g_pre_mix, m_g_post_mix, m_g_pre_ffn, m_g_post_ffn, m_w_in, m_sc_conv_w, m_ssm_conv_w, m_ssm_conv_b, m_ssm_dt_bias, m_ssm_a_log, m_ssm_d, m_ssm_norm_w, m_w_sc_out, m_w_sb_out, m_w_ssm_out, m_w_o, m_w_ffn_in, m_w_ffn_out, v_mod_w, v_mod_b, v_g_pre_mix, v_g_post_mix, v_g_pre_ffn, v_g_post_ffn, v_w_in, v_sc_conv_w, v_ssm_conv_w, v_ssm_conv_b, v_ssm_dt_bias, v_ssm_a_log, v_ssm_d, v_ssm_norm_w, v_w_sc_out, v_w_sb_out, v_w_ssm_out, v_w_o, v_w_ffn_in, v_w_ffn_out):
    wts = dict(mod_w=mod_w, mod_b=mod_b, g_pre_mix=g_pre_mix, g_post_mix=g_post_mix, g_pre_ffn=g_pre_ffn,
               g_post_ffn=g_post_ffn, w_in=w_in, sc_conv_w=sc_conv_w, ssm_conv_w=ssm_conv_w, ssm_conv_b=ssm_conv_b,
               ssm_dt_bias=ssm_dt_bias, ssm_a_log=ssm_a_log, ssm_d=ssm_d, ssm_norm_w=ssm_norm_w, w_sc_out=w_sc_out,
               w_sb_out=w_sb_out, w_ssm_out=w_ssm_out, w_o=w_o, w_ffn_in=w_ffn_in, w_ffn_out=w_ffn_out)
    ms = dict(mod_w=m_mod_w, mod_b=m_mod_b, g_pre_mix=m_g_pre_mix, g_post_mix=m_g_post_mix, g_pre_ffn=m_g_pre_ffn,
              g_post_ffn=m_g_post_ffn, w_in=m_w_in, sc_conv_w=m_sc_conv_w, ssm_conv_w=m_ssm_conv_w,
              ssm_conv_b=m_ssm_conv_b, ssm_dt_bias=m_ssm_dt_bias, ssm_a_log=m_ssm_a_log, ssm_d=m_ssm_d,
              ssm_norm_w=m_ssm_norm_w, w_sc_out=m_w_sc_out, w_sb_out=m_w_sb_out, w_ssm_out=m_w_ssm_out, w_o=m_w_o,
              w_ffn_in=m_w_ffn_in, w_ffn_out=m_w_ffn_out)
    vs = dict(mod_w=v_mod_w, mod_b=v_mod_b, g_pre_mix=v_g_pre_mix, g_post_mix=v_g_post_mix, g_pre_ffn=v_g_pre_ffn,
              g_post_ffn=v_g_post_ffn, w_in=v_w_in, sc_conv_w=v_sc_conv_w, ssm_conv_w=v_ssm_conv_w,
              ssm_conv_b=v_ssm_conv_b, ssm_dt_bias=v_ssm_dt_bias, ssm_a_log=v_ssm_a_log, ssm_d=v_ssm_d,
              ssm_norm_w=v_ssm_norm_w, w_sc_out=v_w_sc_out, w_sb_out=v_w_sb_out, w_ssm_out=v_w_ssm_out, w_o=v_w_o,
              w_ffn_in=v_w_ffn_in, w_ffn_out=v_w_ffn_out)
    me = 4 * lax.axis_index("x") + 2 * lax.axis_index("y") + lax.axis_index("c")
    mod_cols = mod_w.shape[2]

    pack1, sizes1 = _pack_rows([c, sc_conv_w, ssm_conv_w], F32, 8)
    got1 = all_gather_rows(pack1, "gather_c_conv").reshape(N_DEV, -1)
    c_all, sc_g, ssm_g = _unpack(got1, sizes1, [(D_MODEL,), sc_conv_w.shape, ssm_conv_w.shape])
    conv = dict(sc_conv_w=_gathered_to_full(sc_g, False), ssm_conv_w=_gathered_to_full(ssm_g, False))

    mod_b_shard = lax.dynamic_slice_in_dim(mod_b, me * mod_cols, mod_cols, axis=1).reshape(DEPTH, 1, mod_cols)
    mod_sh = mod_shard_fwd(c_all, mod_w, mod_b_shard, "mod_shard_fwd")
    pack2, sizes2 = _pack_rows([mod_sh], F32, 8)
    got2 = all_gather_rows(pack2, "gather_mod").reshape(N_DEV, -1)
    mod_all = _unpack(got2, sizes2, [mod_sh.shape])[0]
    mod_mine = lax.dynamic_index_in_dim(mod_all, me, axis=2, keepdims=False)
    mod = mod_mine.transpose(1, 0, 2).reshape(DEPTH, 6, D_MODEL)

    mm_names = [k for k in _BIG if k not in _CONV]
    gathered = all_gather_multi([wts[k].astype(BF16) for k in mm_names], "gather_weights")
    big = {k: _gathered_to_full(g, k in _ROW_SHARDED) for k, g in zip(mm_names, gathered)}

    small = {k: wts[k] for k in _SMALL}
    loss_part, dx, dmod, grads = _local_step(x[0], loss_target[0], mod, small, conv, big)
    loss = lax.psum(loss_part, ("x", "y", "c"))

    small_parts = [dmod.reshape(DEPTH, 6 * D_MODEL)] + [grads[k] for k in _SMALL[1:]]
    pack5, sizes5 = _pack_rows(small_parts, F32, 8)
    pack_conv, sizes_conv = _pack_rows([grads[k] for k in _CONV], F32, 8)
    got5, got_conv = all_gather_multi([pack5, pack_conv], "gather_small_grads")
    w5, _ = _pack_rows([wts[k] for k in _SMALL], F32, 8)
    m5, _ = _pack_rows([ms[k] for k in _SMALL], F32, 8)
    v5, _ = _pack_rows([vs[k] for k in _SMALL], F32, 8)
    res5 = adamw_flat(got5, w5, m5, v5, "adamw_small")
    small_out = [_unpack(r.reshape(-1), sizes5, [wts[k].shape for k in _SMALL]) for r in res5]

    conv_full = _unpack(got_conv.reshape(N_DEV, -1), sizes_conv, [grads[k].shape for k in _CONV])
    conv_mine = [lax.dynamic_slice_in_dim(g, me * wts[k].shape[2], wts[k].shape[2], axis=3)
                 for k, g in zip(_CONV, conv_full)]
    conv_slot_sizes = [math.prod(wts[k].shape) for k in _CONV]
    conv_slots = jnp.concatenate([g.reshape(N_DEV, -1) for g in conv_mine], axis=1)
    pad_c = -conv_slots.shape[1] % (8 * LANES)
    conv_slots = jnp.pad(conv_slots, ((0, 0), (0, pad_c))).reshape(N_DEV, -1, LANES)
    wc, _ = _pack_rows([wts[k] for k in _CONV], F32, 8)
    mc, _ = _pack_rows([ms[k] for k in _CONV], F32, 8)
    vc, _ = _pack_rows([vs[k] for k in _CONV], F32, 8)
    res_c = adamw_flat(conv_slots, wc, mc, vc, "adamw_conv")
    conv_out = [_unpack(r.reshape(-1), conv_slot_sizes, [wts[k].shape for k in _CONV]) for r in res_c]

    dmod_all = got5.reshape(N_DEV, -1)[:, :DEPTH * 6 * D_MODEL].reshape(N_DEV, DEPTH, 6 * D_MODEL)
    dmod_shard = lax.dynamic_slice_in_dim(dmod_all, me * mod_cols, mod_cols, axis=2).transpose(1, 0, 2)
    g_mod_w = mod_w_grad(c_all, dmod_shard, "mod_w_grad")
    rows2 = lambda a: a.reshape(a.shape[0] * a.shape[1], a.shape[2])
    res_mw = adamw_flat(rows2(g_mod_w)[None], rows2(mod_w), rows2(m_mod_w), rows2(v_mod_w), "adamw_mod_w")
    mod_w_out = [r.reshape(mod_w.shape) for r in res_mw]

    sends = [_full_to_slots(grads[k], k in _ROW_SHARDED).astype(BF16) for k in mm_names]
    recvs = exchange_multi(sends, "exchange_grads")
    big_out = {}
    for k, recv in zip(mm_names, recvs):
        res = adamw_flat(recv, rows2(wts[k]), rows2(ms[k]), rows2(vs[k]), f"adamw_{k}")
        big_out[k] = [r.reshape(wts[k].shape) for r in res]

    outs = []
    for kind in range(4):
        by_name = {"mod_w": mod_w_out[kind]}
        by_name.update(zip(_SMALL, small_out[kind]))
        by_name.update(zip(_CONV, conv_out[kind]))
        by_name.update({k: v[kind] for k, v in big_out.items()})
        outs.extend(by_name[k] for k in _WEIGHTS)
    return (loss, dx[None], *outs)
```

```python
import functools
import math

import jax
import jax.numpy as jnp
from jax import lax
from jax.experimental import pallas as pl
from jax.experimental.pallas import tpu as pltpu

F32 = jnp.float32
BF16 = jnp.bfloat16
HI = lax.Precision.HIGHEST

N_DEV = 8
D_MODEL = 1024
DEPTH = 2
SC_WIDTH = 256
SB_WIDTH = 256
SB_HEAD_DIM = 64
SSM_INNER = 512
SSM_HEADS = 8
SSM_HEAD_DIM = 64
SSM_GROUPS = 2
SSM_STATE = 64
SSM_CHUNK = 256
SSM_CONV_DIM = 768
FFN_HIDDEN = 2816
NORM_EPS = 1e-6
IN_PROJ = 5896
LANES = 128
VMEM_LIMIT = 56 * 1024 * 1024

OFF_GATES = 0
OFF_SC = 3072
OFF_QKV = 3840
OFF_XBC = 4608
OFF_DT = 5376
OFF_Z = 5632
IN_PAD = 6144

ADAM_LR = 0.001
ADAM_B1 = 0.9
ADAM_B2 = 0.999
ADAM_EPS = 1e-08
ADAM_WD = 0.01
ADAM_STEP = 10

MESH_ID = pl.DeviceIdType.MESH


def _cparams(sem=None):
    return pltpu.CompilerParams(dimension_semantics=sem, vmem_limit_bytes=VMEM_LIMIT)


def _my_pos():
    return lax.axis_index("x"), lax.axis_index("y"), lax.axis_index("c")


def _peer(k, x, y, c):
    return (x ^ ((k >> 2) & 1), y ^ ((k >> 1) & 1), c ^ (k & 1))


def all_gather_rows(block, name):
    rows, lanes = block.shape

    def body(x_ref, out_ref, send_sems, recv_sems, local_sem):
        x, y, c = _my_pos()
        me, sibling = (x, y, c), (x, y, 1 - c)
        chips = [(1 - x, y), (x, 1 - y), (1 - x, 1 - y)]

        def slot(px, py, pc):
            return out_ref.at[4 * px + 2 * py + pc]

        def copy(k, blk, to, src=None):
            return pltpu.make_async_remote_copy(
                src_ref=slot(*blk) if src is None else src, dst_ref=slot(*blk),
                send_sem=send_sems.at[k], recv_sem=recv_sems.at[k], device_id=to, device_id_type=MESH_ID)

        mine = pltpu.make_async_copy(x_ref, slot(*me), local_sem)
        mine.start()
        first = [copy(0, me, sibling, src=x_ref)]
        first += [copy(1 + j, me, (*chip, c), src=x_ref) for j, chip in enumerate(chips)]
        for cp in first:
            cp.start()
        passed = [copy(4 + j, (*chip, c), sibling) for j, chip in enumerate(chips)]
        for j, chip in enumerate(chips):
            copy(1 + j, (*chip, c), me).wait_recv()
            passed[j].start()
        copy(0, sibling, me).wait_recv()
        for j, chip in enumerate(chips):
            copy(4 + j, (*chip, 1 - c), me).wait_recv()
        for cp in first + passed:
            cp.wait_send()
        mine.wait()

    return pl.pallas_call(
        body, name=name,
        out_shape=jax.ShapeDtypeStruct((N_DEV, rows, lanes), block.dtype),
        in_specs=[pl.BlockSpec(memory_space=pl.ANY)],
        out_specs=pl.BlockSpec(memory_space=pl.ANY),
        scratch_shapes=[pltpu.SemaphoreType.DMA((7,)), pltpu.SemaphoreType.DMA((7,)), pltpu.SemaphoreType.DMA],
    )(block)


def exchange_slots(send, name):
    _, rows, lanes = send.shape

    def body(s_ref, r_ref, send_sems, recv_sems, local_sem):
        x, y, c = _my_pos()
        me = 4 * x + 2 * y + c
        mine = pltpu.make_async_copy(s_ref.at[me], r_ref.at[me], local_sem)
        mine.start()
        copies = []
        for k in range(1, N_DEV):
            px, py, pc = _peer(k, x, y, c)
            cp = pltpu.make_async_remote_copy(
                src_ref=s_ref.at[4 * px + 2 * py + pc], dst_ref=r_ref.at[me],
                send_sem=send_sems.at[k - 1], recv_sem=recv_sems.at[k - 1],
                device_id=(px, py, pc), device_id_type=MESH_ID)
            cp.start()
            copies.append(cp)
        for cp in copies:
            cp.wait_recv()
        for cp in copies:
            cp.wait_send()
        mine.wait()

    return pl.pallas_call(
        body, name=name,
        out_shape=jax.ShapeDtypeStruct((N_DEV, rows, lanes), send.dtype),
        in_specs=[pl.BlockSpec(memory_space=pl.ANY)],
        out_specs=pl.BlockSpec(memory_space=pl.ANY),
        scratch_shapes=[pltpu.SemaphoreType.DMA((7,)), pltpu.SemaphoreType.DMA((7,)), pltpu.SemaphoreType.DMA],
    )(send)


def all_gather_multi(blocks, name):
    n = len(blocks)

    def body(*refs):
        x_refs, o_refs = refs[:n], refs[n:2 * n]
        send_sems, recv_sems, local_sems = refs[2 * n:]
        x, y, c = _my_pos()
        me, sibling = (x, y, c), (x, y, 1 - c)
        chips = [(1 - x, y), (x, 1 - y), (1 - x, 1 - y)]

        def slot(a, px, py, pc):
            return o_refs[a].at[4 * px + 2 * py + pc]

        def copy(a, k, blk, to, src=None):
            return pltpu.make_async_remote_copy(
                src_ref=slot(a, *blk) if src is None else src, dst_ref=slot(a, *blk),
                send_sem=send_sems.at[7 * a + k], recv_sem=recv_sems.at[7 * a + k], device_id=to, device_id_type=MESH_ID)

        mine = [pltpu.make_async_copy(x_refs[a], slot(a, *me), local_sems.at[a]) for a in range(n)]
        for cp in mine:
            cp.start()
        first = [copy(a, 1 + j, me, (*chip, c), src=x_refs[a]) for j, chip in enumerate(chips) for a in range(n)]
        first += [copy(a, 0, me, sibling, src=x_refs[a]) for a in range(n)]
        for cp in first:
            cp.start()
        passed = []
        for j, chip in enumerate(chips):
            for a in range(n):
                copy(a, 1 + j, (*chip, c), me).wait_recv()
                fwd = copy(a, 4 + j, (*chip, c), sibling)
                fwd.start()
                passed.append(fwd)
        for a in range(n):
            copy(a, 0, sibling, me).wait_recv()
            for j, chip in enumerate(chips):
                copy(a, 4 + j, (*chip, 1 - c), me).wait_recv()
        for cp in first + passed:
            cp.wait_send()
        for cp in mine:
            cp.wait()

    any_spec = pl.BlockSpec(memory_space=pl.ANY)
    return pl.pallas_call(
        body, name=name,
        out_shape=[jax.ShapeDtypeStruct((N_DEV,) + b.shape, b.dtype) for b in blocks],
        in_specs=[any_spec] * n, out_specs=[any_spec] * n,
        scratch_shapes=[pltpu.SemaphoreType.DMA((7 * n,)), pltpu.SemaphoreType.DMA((7 * n,)), pltpu.SemaphoreType.DMA((n,))],
    )(*blocks)


def exchange_multi(sends, name):
    n = len(sends)

    def body(*refs):
        s_refs, r_refs = refs[:n], refs[n:2 * n]
        send_sems, recv_sems, local_sems = refs[2 * n:]
        x, y, c = _my_pos()
        me = 4 * x + 2 * y + c
        mine = [pltpu.make_async_copy(s_refs[a].at[me], r_refs[a].at[me], local_sems.at[a]) for a in range(n)]
        for cp in mine:
            cp.start()
        copies = []
        for k in (2, 4, 6, 3, 5, 7, 1):
            px, py, pc = _peer(k, x, y, c)
            for a in range(n):
                cp = pltpu.make_async_remote_copy(
                    src_ref=s_refs[a].at[4 * px + 2 * py + pc], dst_ref=r_refs[a].at[me],
                    send_sem=send_sems.at[7 * a + k - 1], recv_sem=recv_sems.at[7 * a + k - 1],
                    device_id=(px, py, pc), device_id_type=MESH_ID)
                cp.start()
                copies.append(cp)
        for cp in copies:
            cp.wait_recv()
        for cp in copies:
            cp.wait_send()
        for cp in mine:
            cp.wait()

    any_spec = pl.BlockSpec(memory_space=pl.ANY)
    return pl.pallas_call(
        body, name=name,
        out_shape=[jax.ShapeDtypeStruct(s.shape, s.dtype) for s in sends],
        in_specs=[any_spec] * n, out_specs=[any_spec] * n,
        scratch_shapes=[pltpu.SemaphoreType.DMA((7 * n,)), pltpu.SemaphoreType.DMA((7 * n,)), pltpu.SemaphoreType.DMA((n,))],
    )(*sends)


def swap_with_sibling(gives, name):
    n = len(gives)

    def body(*refs):
        g_refs, r_refs = refs[:n], refs[n:2 * n]
        send_sems, recv_sems = refs[2 * n:]
        x, y, c = _my_pos()
        copies = [pltpu.make_async_remote_copy(
            src_ref=g_refs[a], dst_ref=r_refs[a], send_sem=send_sems.at[a], recv_sem=recv_sems.at[a],
            device_id=(x, y, 1 - c), device_id_type=MESH_ID) for a in range(n)]
        for cp in copies:
            cp.start()
        for cp in copies:
            cp.wait_recv()
        for cp in copies:
            cp.wait_send()

    any_spec = pl.BlockSpec(memory_space=pl.ANY)
    return pl.pallas_call(
        body, name=name, out_shape=[jax.ShapeDtypeStruct(g.shape, g.dtype) for g in gives],
        in_specs=[any_spec] * n, out_specs=[any_spec] * n,
        scratch_shapes=[pltpu.SemaphoreType.DMA((n,)), pltpu.SemaphoreType.DMA((n,))],
    )(*gives)


def exchange_chips(sends, name):
    n = len(sends)

    def body(*refs):
        s_refs, r_refs = refs[:n], refs[n:2 * n]
        send_sems, recv_sems, local_sems = refs[2 * n:]
        x, y, c = _my_pos()
        me = 2 * x + y
        mine = [pltpu.make_async_copy(s_refs[a].at[me], r_refs[a].at[me], local_sems.at[a]) for a in range(n)]
        for cp in mine:
            cp.start()
        copies = []
        for k in (2, 1, 3):
            px, py = x ^ (k >> 1), y ^ (k & 1)
            for a in range(n):
                cp = pltpu.make_async_remote_copy(
                    src_ref=s_refs[a].at[2 * px + py], dst_ref=r_refs[a].at[me],
                    send_sem=send_sems.at[3 * a + k - 1], recv_sem=recv_sems.at[3 * a + k - 1],
                    device_id=(px, py, c), device_id_type=MESH_ID)
                cp.start()
                copies.append(cp)
        for cp in copies:
            cp.wait_recv()
        for cp in copies:
            cp.wait_send()
        for cp in mine:
            cp.wait()

    any_spec = pl.BlockSpec(memory_space=pl.ANY)
    return pl.pallas_call(
        body, name=name, out_shape=[jax.ShapeDtypeStruct(s.shape, s.dtype) for s in sends],
        in_specs=[any_spec] * n, out_specs=[any_spec] * n,
        scratch_shapes=[pltpu.SemaphoreType.DMA((3 * n,)), pltpu.SemaphoreType.DMA((3 * n,)), pltpu.SemaphoreType.DMA((n,))],
    )(*sends)


def add_pairs(a, b, name, tr=512):
    rows, cols = a.shape
    tr = max(d for d in range(16, min(tr, rows) + 1, 16) if rows % d == 0)

    def body(a_ref, b_ref, o_ref):
        o_ref[...] = (a_ref[...].astype(F32) + b_ref[...].astype(F32)).astype(BF16)

    tile = pl.BlockSpec((tr, cols), lambda i: (i, 0))
    return pl.pallas_call(body, name=name, grid=(rows // tr,), in_specs=[tile, tile], out_specs=tile,
                          out_shape=jax.ShapeDtypeStruct((rows, cols), BF16), compiler_params=_cparams(("parallel",)))(a, b)


def _pack_rows(parts, dtype, row_multiple):
    flat = [p.astype(dtype).reshape(-1) for p in parts]
    sizes = [f.shape[0] for f in flat]
    total = sum(sizes)
    quantum = LANES * row_multiple
    padded = -(-total // quantum) * quantum
    if padded > total:
        flat.append(jnp.zeros((padded - total,), dtype))
    return jnp.concatenate(flat).reshape(padded // LANES, LANES), sizes


def _unpack(flat, sizes, shapes):
    out, off = [], 0
    lead = flat.shape[:-1]
    for n, shp in zip(sizes, shapes):
        out.append(flat[..., off:off + n].reshape(lead + tuple(shp)))
        off += n
    return out


def rows_call(name, body, n_rows, tr, ins, outs, scratch=()):
    n_tiles = n_rows // tr
    assert n_tiles * tr == n_rows
    in_specs, arrays = [], []
    for arr, kind in ins:
        arrays.append(arr)
        if kind == "row":
            in_specs.append(pl.BlockSpec((tr, arr.shape[1]), lambda i: (i, 0)))
        elif kind == "full":
            in_specs.append(pl.BlockSpec(arr.shape, lambda i, nd=arr.ndim: (0,) * nd))
        elif kind[0] == "row":
            _, w, ci = kind
            in_specs.append(pl.BlockSpec((tr, w), lambda i, ci=ci: (i, ci)))
        elif kind[0] == "prev8":
            _, w, ci = kind
            hr = 8 * (4 // arr.dtype.itemsize)
            in_specs.append(pl.BlockSpec((hr, w), lambda i, ci=ci, hr=hr: (jnp.maximum(i * (tr // hr) - 1, 0), ci)))
        elif kind[0] == "next8":
            _, w, ci = kind
            hr = 8 * (4 // arr.dtype.itemsize)
            last = n_rows // hr - 1
            in_specs.append(pl.BlockSpec((hr, w), lambda i, ci=ci, last=last, hr=hr: (jnp.minimum((i + 1) * (tr // hr), last), ci)))
        else:
            raise ValueError(kind)
    out_specs, out_shapes = [], []
    for shape, dtype, kind in outs:
        out_shapes.append(jax.ShapeDtypeStruct(shape, dtype))
        if kind == "row":
            out_specs.append(pl.BlockSpec((tr, shape[1]), lambda i: (i, 0)))
        else:
            out_specs.append(pl.BlockSpec(shape, lambda i, nd=len(shape): (0,) * nd))
    has_acc = any(k == "acc" for _, _, k in outs)
    return pl.pallas_call(
        body, name=name, grid=(n_tiles,), in_specs=in_specs, out_specs=out_specs, out_shape=out_shapes,
        scratch_shapes=list(scratch),
        compiler_params=_cparams(("arbitrary",) if has_acc else ("parallel",)),
    )(*arrays)


def _prev8(ref):
    n = ref.shape[0]
    return ref[n - 8:n, :].astype(F32)


def _next8(ref):
    return ref[0:8, :].astype(F32)


def _acc(ref, val):
    @pl.when(pl.program_id(0) == 0)
    def _():
        ref[...] = jnp.zeros_like(ref)
    ref[...] += val


def _rstd(x):
    return lax.rsqrt(jnp.mean(x * x, axis=-1, keepdims=True) + NORM_EPS)


def _sigmoid(x):
    return 1.0 / (1.0 + jnp.exp(-x))


def _silu(x):
    return x * _sigmoid(x)


def _dsilu(x):
    s = _sigmoid(x)
    return s * (1.0 + x * (1.0 - s))


def _softplus(x):
    return jnp.maximum(x, 0.0) + jnp.log(1.0 + jnp.exp(-jnp.abs(x)))


def _log_sigmoid_neg(x):
    t = -x
    return jnp.minimum(t, 0.0) - jnp.log(1.0 + jnp.exp(jnp.minimum(x, t)))


def normmod_fwd(x, g, scale, shift, name):
    t, d = x.shape

    def body(x_ref, g_ref, sc_ref, sh_ref, h_ref):
        xv = x_ref[...]
        h = xv * _rstd(xv) * g_ref[...] * (1.0 + sc_ref[...]) + sh_ref[...]
        h_ref[...] = h.astype(BF16)

    return rows_call(name, body, t, 512, [(x, "row"), (g, "full"), (scale, "full"), (shift, "full")],
                     [((t, d), BF16, "row")])[0]


def resid_normmod_fwd(x, f, gate, g_post, g_pre, scale, shift, name):
    t, d = x.shape

    def body(x_ref, f_ref, gate_ref, gp_ref, g_ref, sc_ref, sh_ref, xo_ref, h_ref):
        fv = f_ref[...]
        xn = x_ref[...] + gate_ref[...] * (fv * _rstd(fv) * gp_ref[...])
        xo_ref[...] = xn
        h = xn * _rstd(xn) * g_ref[...] * (1.0 + sc_ref[...]) + sh_ref[...]
        h_ref[...] = h.astype(BF16)

    return rows_call(name, body, t, 512,
                     [(x, "row"), (f, "row"), (gate, "full"), (g_post, "full"), (g_pre, "full"), (scale, "full"),
                      (shift, "full")],
                     [((t, d), F32, "row"), ((t, d), BF16, "row")])


def resid_loss(x, f, gate, g_post, target, name):
    t, d = x.shape

    def body(x_ref, f_ref, gate_ref, gp_ref, tg_ref, dy_ref, loss_ref):
        fv = f_ref[...]
        yv = x_ref[...] + gate_ref[...] * (fv * _rstd(fv) * gp_ref[...])
        err = yv - tg_ref[...]
        dy_ref[...] = err * (1.0 / d)
        part = 0.5 * jnp.sum(jnp.mean(err * err, axis=-1, keepdims=True), axis=0, keepdims=True)
        _acc(loss_ref, jnp.broadcast_to(part, loss_ref.shape))

    return rows_call(name, body, t, 512,
                     [(x, "row"), (f, "row"), (gate, "full"), (g_post, "full"), (target, "row")],
                     [((t, d), F32, "row"), ((8, LANES), F32, "acc")])


def resid_bwd(dx, f, gate, g_post, name):
    t, d = dx.shape

    def body(dx_ref, f_ref, gate_ref, gp_ref, df_ref, dgate_ref, dg_ref):
        fv, dxv, gp = f_ref[...], dx_ref[...], gp_ref[...]
        r = _rstd(fv)
        fn = fv * r
        _acc(dgate_ref, jnp.sum(dxv * (fn * gp), axis=0, keepdims=True))
        dn = dxv * gate_ref[...]
        _acc(dg_ref, jnp.sum(dn * fn, axis=0, keepdims=True))
        u = dn * gp
        df = r * (u - fn * jnp.mean(fn * u, axis=-1, keepdims=True))
        df_ref[...] = df.astype(BF16)

    return rows_call(name, body, t, 512, [(dx, "row"), (f, "row"), (gate, "full"), (g_post, "full")],
                     [((t, d), BF16, "row"), ((1, d), F32, "acc"), ((1, d), F32, "acc")])


def normmod_bwd(dh, x, dx_in, g, scale, name):
    t, d = x.shape

    def body(dh_ref, x_ref, dxi_ref, g_ref, sc_ref, dx_ref, dsc_ref, dsh_ref, dg_ref):
        xv, dhv, gv = x_ref[...], dh_ref[...], g_ref[...]
        r = _rstd(xv)
        xn = xv * r
        _acc(dsc_ref, jnp.sum(dhv * (xn * gv), axis=0, keepdims=True))
        _acc(dsh_ref, jnp.sum(dhv, axis=0, keepdims=True))
        dn = dhv * (1.0 + sc_ref[...])
        _acc(dg_ref, jnp.sum(dn * xn, axis=0, keepdims=True))
        u = dn * gv
        dx_ref[...] = dxi_ref[...] + r * (u - xn * jnp.mean(xn * u, axis=-1, keepdims=True))

    return rows_call(name, body, t, 512, [(dh, "row"), (x, "row"), (dx_in, "row"), (g, "full"), (scale, "full")],
                     [((t, d), F32, "row"), ((1, d), F32, "acc"), ((1, d), F32, "acc"), ((1, d), F32, "acc")])


def _pick(n, prefs):
    for p in prefs:
        if n % p == 0:
            return p
    return n


def mm_nn(a_list, b_list, out_dtype, name, tm=1024, tn=None, tk=None):
    m, k = a_list[0].shape
    n = b_list[0].shape[1]
    tm = min(tm, m)
    tn = tn or _pick(n, (1024, 768, 512, 256, 128))
    tk = tk or _pick(k, (1024, 1408, 512, 256))
    nk = k // tk
    npair = len(a_list)

    if nk == 1 and npair == 1:
        def body1(a_ref, b_ref, o_ref):
            o_ref[...] = jnp.dot(a_ref[...], b_ref[...], preferred_element_type=F32).astype(o_ref.dtype)

        return pl.pallas_call(
            body1, name=name, grid=(m // tm, n // tn),
            in_specs=[pl.BlockSpec((tm, k), lambda i, j: (i, 0)), pl.BlockSpec((k, tn), lambda i, j: (0, j))],
            out_specs=pl.BlockSpec((tm, tn), lambda i, j: (i, j)),
            out_shape=jax.ShapeDtypeStruct((m, n), out_dtype),
            compiler_params=_cparams(("parallel", "parallel")),
        )(a_list[0], b_list[0])

    def body(*refs):
        a_refs, b_refs = refs[:npair], refs[npair:2 * npair]
        o_ref, acc = refs[2 * npair], refs[2 * npair + 1]
        kk = pl.program_id(2)

        @pl.when(kk == 0)
        def _():
            acc[...] = jnp.zeros_like(acc)

        s = acc[...]
        for a_ref, b_ref in zip(a_refs, b_refs):
            s = s + jnp.dot(a_ref[...], b_ref[...], preferred_element_type=F32)
        acc[...] = s

        @pl.when(kk == nk - 1)
        def _():
            o_ref[...] = acc[...].astype(o_ref.dtype)

    return pl.pallas_call(
        body, name=name, grid=(m // tm, n // tn, nk),
        in_specs=[pl.BlockSpec((tm, tk), lambda i, j, kk: (i, kk))] * npair
        + [pl.BlockSpec((tk, tn), lambda i, j, kk: (kk, j))] * npair,
        out_specs=pl.BlockSpec((tm, tn), lambda i, j, kk: (i, j)),
        out_shape=jax.ShapeDtypeStruct((m, n), out_dtype),
        scratch_shapes=[pltpu.VMEM((tm, tn), F32)],
        compiler_params=_cparams(("parallel", "parallel", "arbitrary")),
    )(*a_list, *b_list)


def mm_tn(a, b, name, tt=512):
    t, ka = a.shape
    n = b.shape[1]
    ta = _pick(ka, (1024, 1408, 512, 256))
    tn = _pick(n, (2048, 1024, 1408, 512, 256))
    nt = t // tt

    def body(a_ref, b_ref, o_ref):
        @pl.when(pl.program_id(2) == 0)
        def _():
            o_ref[...] = jnp.zeros_like(o_ref)

        o_ref[...] += lax.dot_general(a_ref[...], b_ref[...], (((0,), (0,)), ((), ())), preferred_element_type=F32)

    return pl.pallas_call(
        body, name=name, grid=(ka // ta, n // tn, nt),
        in_specs=[pl.BlockSpec((tt, ta), lambda i, j, s: (s, i)), pl.BlockSpec((tt, tn), lambda i, j, s: (s, j))],
        out_specs=pl.BlockSpec((ta, tn), lambda i, j, s: (i, j)),
        out_shape=jax.ShapeDtypeStruct((ka, n), F32),
        compiler_params=_cparams(("parallel", "parallel", "arbitrary")),
    )(a, b)


def mm_swiglu_fwd(h, w_ffn_in, name, tm=512, tn=1408):
    m, k = h.shape
    nh = FFN_HIDDEN // tn

    def body(h_ref, wg_ref, wu_ref, gt_ref, up_ref, a_ref):
        hv = h_ref[...]
        gt = jnp.dot(hv, wg_ref[...], preferred_element_type=F32)
        up = jnp.dot(hv, wu_ref[...], preferred_element_type=F32)
        gt_ref[...] = gt.astype(BF16)
        up_ref[...] = up.astype(BF16)
        a_ref[...] = (_silu(gt) * up).astype(BF16)

    shp = jax.ShapeDtypeStruct((m, FFN_HIDDEN), BF16)
    ospec = pl.BlockSpec((tm, tn), lambda i, j: (i, j))
    return pl.pallas_call(
        body, name=name, grid=(m // tm, nh),
        in_specs=[pl.BlockSpec((tm, k), lambda i, j: (i, 0)), pl.BlockSpec((k, tn), lambda i, j: (0, j)),
                  pl.BlockSpec((k, tn), lambda i, j: (0, j + nh))],
        out_specs=[ospec, ospec, ospec], out_shape=[shp, shp, shp],
        compiler_params=_cparams(("parallel", "parallel")),
    )(h, w_ffn_in, w_ffn_in)


def mm_swiglu_bwd(df, w_out_t, gt, up, name, tm=512, tn=1408):
    m, k = df.shape

    def body(df_ref, w_ref, gt_ref, up_ref, dgt_ref, dup_ref):
        da = jnp.dot(df_ref[...], w_ref[...], preferred_element_type=F32)
        gtv = gt_ref[...].astype(F32)
        upv = up_ref[...].astype(F32)
        dgt_ref[...] = (da * upv * _dsilu(gtv)).astype(BF16)
        dup_ref[...] = (da * _silu(gtv)).astype(BF16)

    shp = jax.ShapeDtypeStruct((m, FFN_HIDDEN), BF16)
    tile = pl.BlockSpec((tm, tn), lambda i, j: (i, j))
    return pl.pallas_call(
        body, name=name, grid=(m // tm, FFN_HIDDEN // tn),
        in_specs=[pl.BlockSpec((tm, k), lambda i, j: (i, 0)), pl.BlockSpec((k, tn), lambda i, j: (0, j)), tile, tile],
        out_specs=[tile, tile], out_shape=[shp, shp],
        compiler_params=_cparams(("parallel", "parallel")),
    )(df, w_out_t, gt, up)


def _shift_down(x, prev8, j):
    if j == 0:
        return x
    xr = pltpu.roll(x, j, 0)
    pr = pltpu.roll(prev8, j, 0)
    row = lax.broadcasted_iota(jnp.int32, (8, x.shape[1]), 0)
    head = jnp.where(row < j, pr, xr[:8])
    return head if x.shape[0] == 8 else jnp.concatenate([head, xr[8:]], axis=0)


def _shift_up(x, next8, j):
    if j == 0:
        return x
    n = x.shape[0]
    xr = pltpu.roll(x, n - j, 0)
    nr = pltpu.roll(next8, 8 - j, 0)
    row = lax.broadcasted_iota(jnp.int32, (8, x.shape[1]), 0)
    return jnp.concatenate([xr[:n - 8], jnp.where(row >= 8 - j, nr, xr[n - 8:])], axis=0)


def _conv_taps(x, prev8, w_ref, taps):
    out = None
    for k in range(taps):
        term = w_ref[k:k + 1, :] * _shift_down(x, prev8, taps - 1 - k)
        out = term if out is None else out + term
    return out


def post_inproj(p, sc_w, ssm_w, ssm_b, name, tr=512):
    t = p.shape[0]

    def body(sc_ref, scp_ref, qkv_ref, xbc_ref, xbcp_ref, scw_ref, sw_ref, sb_ref, ya_ref, qkvo_ref, act_ref):
        first = (pl.program_id(0) > 0).astype(F32)
        sc = sc_ref[...].astype(F32)
        scp = _prev8(scp_ref) * first
        u = sc[:, 256:512] * sc[:, 512:768]
        up = scp[:, 256:512] * scp[:, 512:768]
        ya_ref[...] = (sc[:, 0:256] * _conv_taps(u, up, scw_ref, 3)).astype(BF16)
        qkv = qkv_ref[...]
        qkvo_ref[:, 0:256] = (qkv[:, 0:256].astype(F32) * 0.125).astype(BF16)
        qkvo_ref[:, 256:768] = qkv[:, 256:768].astype(BF16)
        xc = _conv_taps(xbc_ref[...].astype(F32), _prev8(xbcp_ref) * first, sw_ref, 4) + sb_ref[...]
        act_ref[...] = _silu(xc)

    return rows_call(
        name, body, t, tr,
        [(p, ("row", 768, OFF_SC // 768)), (p, ("prev8", 768, OFF_SC // 768)), (p, ("row", 768, OFF_QKV // 768)),
         (p, ("row", 768, OFF_XBC // 768)), (p, ("prev8", 768, OFF_XBC // 768)),
         (sc_w, "full"), (ssm_w, "full"), (ssm_b, "full")],
        [((t, 256), BF16, "row"), ((t, 768), BF16, "row"), ((t, 768), F32, "row")])


def branch_out_fwd(ya, yb, yc, p, w_cat, name, tr=256):
    t = p.shape[0]

    def body(ya_ref, yb_ref, yc_ref, gl_ref, w_ref, o_ref):
        y_a = jnp.dot(ya_ref[...], w_ref[0:256, :], preferred_element_type=F32)
        y_b = jnp.dot(yb_ref[...].astype(BF16), w_ref[256:512, :], preferred_element_type=F32)
        y_c = jnp.dot(yc_ref[...], w_ref[512:1024, :], preferred_element_type=F32)
        m = (_sigmoid(gl_ref[:, 0:1024].astype(F32)) * y_a + _sigmoid(gl_ref[:, 1024:2048].astype(F32)) * y_b
             + _sigmoid(gl_ref[:, 2048:3072].astype(F32)) * y_c)
        o_ref[...] = m.astype(BF16)

    return rows_call(name, body, t, tr,
                     [(ya, "row"), (yb, "row"), (yc, "row"), (p, ("row", 3072, 0)), (w_cat, "full")],
                     [((t, D_MODEL), BF16, "row")])[0]


def branch_out_bwd(dm, ya, yb, yc, p, w_cat, w_cat_t, name, tr=256):
    t = p.shape[0]
    tn_dims = (((0,), (0,)), ((), ()))

    def body(dm_ref, ya_ref, yb_ref, yc_ref, gl_ref, w_ref, wt_ref, dgl_ref, dya_ref, dyb_ref, dyc_ref, dw_ref):
        @pl.when(pl.program_id(0) == 0)
        def _():
            dw_ref[...] = jnp.zeros_like(dw_ref)

        dmv = dm_ref[...]
        ins = (ya_ref[...], yb_ref[...].astype(BF16), yc_ref[...])
        rows = ((0, 256), (256, 512), (512, 1024))
        outs = (dya_ref, dyb_ref, dyc_ref)
        for i in range(3):
            r0, r1 = rows[i]
            y = jnp.dot(ins[i], w_ref[r0:r1, :], preferred_element_type=F32)
            s = _sigmoid(gl_ref[:, 1024 * i:1024 * (i + 1)].astype(F32))
            dgl_ref[:, 1024 * i:1024 * (i + 1)] = (dmv * y * s * (1.0 - s)).astype(BF16)
            dy = (dmv * s).astype(BF16)
            outs[i][...] = jnp.dot(dy, wt_ref[:, r0:r1], preferred_element_type=F32)
            dw_ref[r0:r1, :] += lax.dot_general(ins[i], dy, tn_dims, preferred_element_type=F32)

    return rows_call(name, body, t, tr,
                     [(dm, "row"), (ya, "row"), (yb, "row"), (yc, "row"), (p, ("row", 3072, 0)), (w_cat, "full"),
                      (w_cat_t, "full")],
                     [((t, 3072), BF16, "row"), ((t, 256), F32, "row"), ((t, 256), F32, "row"), ((t, 512), F32, "row"),
                      ((D_MODEL, D_MODEL), F32, "acc")])


def assemble_dp(dgl, dya, p, sc_w, dq, dk, dv, dact, ssm_w, ssm_b, ddt, dz, name, tr=256):
    t = p.shape[0]
    n_tiles = t // tr
    sci, xi = OFF_SC // 768, OFF_XBC // 768

    def body(dgl_ref, dya_ref, dyan_ref, sc_ref, scp_ref, scn_ref, scw_ref, dq_ref, dk_ref, dv_ref,
             dact_ref, dactn_ref, xbc_ref, xbcp_ref, xbcn_ref, sw_ref, sb_ref, ddt_ref, dz_ref,
             o_ref, dscw_ref, dsw_ref, dsb_ref):
        i = pl.program_id(0)

        @pl.when(i == 0)
        def _():
            dscw_ref[...] = jnp.zeros_like(dscw_ref)
            dsw_ref[...] = jnp.zeros_like(dsw_ref)
            dsb_ref[...] = jnp.zeros_like(dsb_ref)

        first = (i > 0).astype(F32)
        last = (i < n_tiles - 1).astype(F32)
        o_ref[:, 0:3072] = dgl_ref[...]
        sc = sc_ref[...].astype(F32)
        scp = _prev8(scp_ref) * first
        scn = _next8(scn_ref) * last
        u = sc[:, 256:512] * sc[:, 512:768]
        up = scp[:, 256:512] * scp[:, 512:768]
        dya_v = dya_ref[...]
        cv = _conv_taps(u, up, scw_ref, 3)
        o_ref[:, OFF_SC:OFF_SC + 256] = (dya_v * cv).astype(BF16)
        dcv = dya_v * sc[:, 0:256]
        dcvn = _next8(dyan_ref) * last * scn[:, 0:256]
        du = None
        for k in range(3):
            sh = 2 - k
            term = scw_ref[k:k + 1, :] * _shift_up(dcv, dcvn, sh)
            du = term if du is None else du + term
            dscw_ref[k:k + 1, :] += jnp.sum(dcv * _shift_down(u, up, sh), axis=0, keepdims=True)
        o_ref[:, OFF_SC + 256:OFF_SC + 512] = (du * sc[:, 512:768]).astype(BF16)
        o_ref[:, OFF_SC + 512:OFF_SC + 768] = (du * sc[:, 256:512]).astype(BF16)
        o_ref[:, OFF_QKV:OFF_QKV + 256] = (dq_ref[...] * 0.125).astype(BF16)
        o_ref[:, OFF_QKV + 256:OFF_QKV + 512] = dk_ref[...].astype(BF16)
        o_ref[:, OFF_QKV + 512:OFF_QKV + 768] = dv_ref[...].astype(BF16)
        xb = xbc_ref[...].astype(F32)
        xbp = _prev8(xbcp_ref) * first
        xbn = _next8(xbcn_ref)
        xc = _conv_taps(xb, xbp, sw_ref, 4) + sb_ref[...]
        xcn = _conv_taps(xbn, xb[tr - 8:, :], sw_ref, 4) + sb_ref[...]
        dxc = dact_ref[...] * _dsilu(xc)
        dxcn = _next8(dactn_ref) * _dsilu(xcn) * last
        dxb = None
        for k in range(4):
            sh = 3 - k
            term = sw_ref[k:k + 1, :] * _shift_up(dxc, dxcn, sh)
            dxb = term if dxb is None else dxb + term
            dsw_ref[k:k + 1, :] += jnp.sum(dxc * _shift_down(xb, xbp, sh), axis=0, keepdims=True)
        dsb_ref[...] += jnp.sum(dxc, axis=0, keepdims=True)
        o_ref[:, OFF_XBC:OFF_XBC + 768] = dxb.astype(BF16)
        o_ref[:, OFF_DT:OFF_DT + 128] = ddt_ref[...].astype(BF16)
        o_ref[:, OFF_DT + 128:OFF_Z] = jnp.zeros((tr, OFF_Z - OFF_DT - 128), BF16)
        o_ref[:, OFF_Z:IN_PAD] = dz_ref[...].astype(BF16)

    return rows_call(
        name, body, t, tr,
        [(dgl, "row"), (dya, "row"), (dya, ("next8", 256, 0)),
         (p, ("row", 768, sci)), (p, ("prev8", 768, sci)), (p, ("next8", 768, sci)), (sc_w, "full"),
         (dq, "row"), (dk, "row"), (dv, "row"),
         (dact, "row"), (dact, ("next8", 768, 0)),
         (p, ("row", 768, xi)), (p, ("prev8", 768, xi)), (p, ("next8", 768, xi)), (ssm_w, "full"), (ssm_b, "full"),
         (ddt, "row"), (dz, "row")],
        [((t, IN_PAD), BF16, "row"), ((8, 256), F32, "acc"), ((8, 768), F32, "acc"), ((1, 768), F32, "acc")])


def adamw_flat(slots, w, m, v, name, tr=512):
    n_slots, rows, lanes = slots.shape
    tr = max(d for d in range(8, min(tr, rows) + 1, 8) if rows % d == 0) if rows % 8 == 0 else rows
    bc1 = 1.0 - ADAM_B1 ** ADAM_STEP
    bc2 = 1.0 - ADAM_B2 ** ADAM_STEP

    def body(s_ref, w_ref, m_ref, v_ref, g_ref, d_ref, mo_ref, vo_ref):
        g = s_ref[0].astype(F32)
        for k in range(1, n_slots):
            g = g + s_ref[k].astype(F32)
        mn = ADAM_B1 * m_ref[...] + (1.0 - ADAM_B1) * g
        vn = ADAM_B2 * v_ref[...] + (1.0 - ADAM_B2) * (g * g)
        m_hat = mn / bc1
        v_hat = vn / bc2
        g_ref[...] = g
        d_ref[...] = -ADAM_LR * (m_hat / (jnp.sqrt(v_hat) + ADAM_EPS) + ADAM_WD * w_ref[...])
        mo_ref[...] = mn
        vo_ref[...] = vn

    tile = pl.BlockSpec((tr, lanes), lambda i: (i, 0))
    shp = jax.ShapeDtypeStruct((rows, lanes), F32)
    return pl.pallas_call(
        body, name=name, grid=(rows // tr,),
        in_specs=[pl.BlockSpec((n_slots, tr, lanes), lambda i: (0, i, 0)), tile, tile, tile],
        out_specs=[tile] * 4, out_shape=[shp] * 4,
        compiler_params=_cparams(("parallel",)),
    )(slots, w, m, v)


def _split_dot(x, tri):
    hi = x.astype(BF16)
    lo = (x - hi.astype(F32)).astype(BF16)
    return jnp.dot(hi, tri, preferred_element_type=F32) + jnp.dot(lo, tri, preferred_element_type=F32)


_NT = (((1,), (1,)), ((), ()))
_TN = (((0,), (0,)), ((), ()))

SBA_EXP_ZERO = -104.0
SBA_SKIPPED = -1e30


def sba_fwd(qkv, name, bq=256, bk=256):
    t = qkv.shape[0]
    ratio = bq // bk
    assert t // bk <= LANES

    def body(q_ref, k_ref, v_ref, o_ref, runs_ref, acc_s, run_s):
        i = pl.program_id(1)
        lane = lax.broadcasted_iota(jnp.int32, (1, LANES), 1)
        lane_q = lax.broadcasted_iota(jnp.int32, (bq, LANES), 1)
        qi = lax.broadcasted_iota(jnp.int32, (bq, bk), 0) + i * bq
        kj = lax.broadcasted_iota(jnp.int32, (bq, bk), 1)
        later = (lax.broadcasted_iota(jnp.int32, (bk, bk), 0) > lax.broadcasted_iota(jnp.int32, (bk, bk), 1)).astype(BF16)
        qv = q_ref[...]
        qms = [jnp.where(hm, qv, jnp.zeros_like(qv)) for hm in (lane < 64, lane >= 64)]
        acc_s[...] = jnp.zeros_like(acc_s)
        run_s[...] = jnp.zeros_like(run_s)
        runs_ref[...] = jnp.full(runs_ref.shape, SBA_SKIPPED, F32)

        def tile(j, masked):
            start = pl.multiple_of(j * bk, bk)
            kb = k_ref[pl.ds(start, bk), :]
            vb = v_ref[pl.ds(start, bk), :]
            heads = range(2)
            mask = (kj + j * bk) < qi if masked else None
            s = [lax.dot_general(qms[hh], kb, _NT, preferred_element_type=F32) for hh in heads]
            lk = [_log_sigmoid_neg(s[hh]) for hh in heads]
            if masked:
                lk = [jnp.where(mask, lk[hh], 0.0) for hh in heads]
            w = [jnp.dot(lk[hh].astype(BF16), later, preferred_element_type=F32) for hh in heads]
            run = [run_s[hh] for hh in heads]
            a = [jnp.exp(s[hh] + lk[hh] + w[hh] + run[hh]) for hh in heads]
            if masked:
                a = [jnp.where(mask, a[hh], 0.0) for hh in heads]
            top = None
            for hh in heads:
                acc_s[hh] += jnp.dot(a[hh].astype(BF16), vb, preferred_element_type=F32)
                runs_ref[hh] = jnp.where(lane_q == j, run[hh], runs_ref[hh])
                new_run = run[hh] + jnp.sum(lk[hh], axis=1, keepdims=True)
                run_s[hh] = new_run
                top = jnp.max(new_run) if top is None else jnp.maximum(top, jnp.max(new_run))
            return top

        n_kb = (i + 1) * ratio
        top = None
        for d in range(ratio):
            top = tile(n_kb - 1 - d, True)

        def cond(state):
            n, live = state
            return jnp.logical_and(n < i * ratio, live)

        def step(state):
            n, _ = state
            return n + 1, tile(i * ratio - 1 - n, False) >= SBA_EXP_ZERO

        lax.while_loop(cond, step, (jnp.int32(0), top >= SBA_EXP_ZERO))
        o_ref[...] = jnp.where(lane < 64, acc_s[0], acc_s[1])

    return pl.pallas_call(
        body, name=name, grid=(2, t // bq),
        in_specs=[pl.BlockSpec((bq, LANES), lambda p, i: (i, p)), pl.BlockSpec((t, LANES), lambda p, i: (0, 2 + p)),
                  pl.BlockSpec((t, LANES), lambda p, i: (0, 4 + p))],
        out_specs=[pl.BlockSpec((bq, LANES), lambda p, i: (i, p)), pl.BlockSpec((2, bq, LANES), lambda p, i: (p, i, 0))],
        out_shape=[jax.ShapeDtypeStruct((t, SB_WIDTH), F32), jax.ShapeDtypeStruct((4, t, LANES), F32)],
        scratch_shapes=[pltpu.VMEM((2, bq, LANES), F32), pltpu.VMEM((2, bq, 1), F32)],
        compiler_params=_cparams(("parallel", "parallel")),
    )(qkv, qkv, qkv)


def sba_bwd(qkv, runs, do, name, bq=256, bk=256):
    t = qkv.shape[0]
    ratio = bq // bk
    nq = t // bq

    def body(q_ref, k_ref, v_ref, runs_ref, do_ref, dq_ref, dk_hbm, dv_hbm, dk_s, dv_s, sem, dq_s, rg_s):
        p = pl.program_id(0)
        i = pl.program_id(1)

        @pl.when(i == 0)
        def _():
            dk_s[...] = jnp.zeros_like(dk_s)
            dv_s[...] = jnp.zeros_like(dv_s)

        lane = lax.broadcasted_iota(jnp.int32, (1, LANES), 1)
        qi = lax.broadcasted_iota(jnp.int32, (bq, bk), 0) + i * bq
        kj = lax.broadcasted_iota(jnp.int32, (bq, bk), 1)
        r2 = lax.broadcasted_iota(jnp.int32, (bk, bk), 0)
        c2 = lax.broadcasted_iota(jnp.int32, (bk, bk), 1)
        later = (r2 > c2).astype(BF16)
        earlier = (r2 < c2).astype(BF16)
        qv = q_ref[...]
        dov = do_ref[...]
        heads = range(2)
        hms = (lane < 64, lane >= 64)
        qms = [jnp.where(hm, qv, jnp.zeros_like(qv)) for hm in hms]
        doms = [jnp.where(hm, dov, 0.0).astype(BF16) for hm in hms]
        runs = [runs_ref[hh] for hh in heads]
        dq_s[...] = jnp.zeros_like(dq_s)
        rg_s[...] = jnp.zeros_like(rg_s)

        def tile(j, masked):
            start = pl.multiple_of(j * bk, bk)
            kb = k_ref[pl.ds(start, bk), :]
            vb = v_ref[pl.ds(start, bk), :]
            mask = (kj + j * bk) < qi if masked else None
            s = [lax.dot_general(qms[hh], kb, _NT, preferred_element_type=F32) for hh in heads]
            da = [lax.dot_general(doms[hh], vb, _NT, preferred_element_type=F32) for hh in heads]
            lk_raw = [_log_sigmoid_neg(s[hh]) for hh in heads]
            lk = [jnp.where(mask, lk_raw[hh], 0.0) for hh in heads] if masked else lk_raw
            w = [jnp.dot(lk[hh].astype(BF16), later, preferred_element_type=F32) for hh in heads]
            run = [jnp.sum(jnp.where(lane == j, runs[hh], 0.0), axis=1, keepdims=True) for hh in heads]
            a = [jnp.exp(s[hh] + lk[hh] + w[hh] + run[hh]) for hh in heads]
            if masked:
                a = [jnp.where(mask, a[hh], 0.0) for hh in heads]
            g = [a[hh] * da[hh] for hh in heads]
            rg = [rg_s[hh] for hh in heads]
            c = [rg[hh] + _split_dot(g[hh], earlier) for hh in heads]
            dz = [g[hh] - jnp.exp(s[hh] + lk_raw[hh]) * (g[hh] + c[hh]) for hh in heads]
            if masked:
                dz = [jnp.where(mask, dz[hh], 0.0) for hh in heads]
            dz = [dz[hh].astype(BF16) for hh in heads]
            for hh in heads:
                dq_s[hh] += jnp.dot(dz[hh], kb, preferred_element_type=F32)
                rg_s[hh] = rg[hh] + jnp.sum(g[hh], axis=1, keepdims=True)
            dk = lax.dot_general(dz[0], qms[0], _TN, preferred_element_type=F32)
            dk_s[pl.ds(start, bk), :] += dk + lax.dot_general(dz[1], qms[1], _TN, preferred_element_type=F32)
            dv = lax.dot_general(a[0].astype(BF16), doms[0], _TN, preferred_element_type=F32)
            dv_s[pl.ds(start, bk), :] += dv + lax.dot_general(a[1].astype(BF16), doms[1], _TN, preferred_element_type=F32)

        live = jnp.maximum(jnp.max(runs[0], axis=0, keepdims=True), jnp.max(runs[1], axis=0, keepdims=True)) >= SBA_EXP_ZERO
        first = jnp.minimum(jnp.min(jnp.where(live, lane, LANES)), i * ratio)

        def step(j, carry):
            tile(j, False)
            return carry

        lax.fori_loop(first, i * ratio, step, 0)
        for d in range(ratio):
            tile(i * ratio + d, True)
        dq_ref[...] = jnp.where(lane < 64, dq_s[0], dq_s[1])

        @pl.when(i == nq - 1)
        def _():
            col = pl.multiple_of(p * LANES, LANES)
            ck = pltpu.make_async_copy(dk_s, dk_hbm.at[:, pl.ds(col, LANES)], sem.at[0])
            cv = pltpu.make_async_copy(dv_s, dv_hbm.at[:, pl.ds(col, LANES)], sem.at[1])
            ck.start()
            cv.start()
            ck.wait()
            cv.wait()

    shp = jax.ShapeDtypeStruct((t, SB_WIDTH), F32)
    tile = pl.BlockSpec((bq, LANES), lambda p, i: (i, p))
    return pl.pallas_call(
        body, name=name, grid=(2, nq),
        in_specs=[tile, pl.BlockSpec((t, LANES), lambda p, i: (0, 2 + p)), pl.BlockSpec((t, LANES), lambda p, i: (0, 4 + p)),
                  pl.BlockSpec((2, bq, LANES), lambda p, i: (p, i, 0)), tile],
        out_specs=[tile, pl.BlockSpec(memory_space=pl.ANY), pl.BlockSpec(memory_space=pl.ANY)],
        out_shape=[shp, shp, shp],
        scratch_shapes=[pltpu.VMEM((t, LANES), F32), pltpu.VMEM((t, LANES), F32), pltpu.SemaphoreType.DMA((2,)),
                        pltpu.VMEM((2, bq, LANES), F32), pltpu.VMEM((2, bq, 1), F32)],
        compiler_params=_cparams(("arbitrary", "arbitrary")),
    )(qkv, qkv, qkv, runs, do)


def _ssd_consts():
    ln = SSM_CHUNK
    ri = lax.broadcasted_iota(jnp.int32, (ln, ln), 0)
    ci = lax.broadcasted_iota(jnp.int32, (ln, ln), 1)
    eh = lax.broadcasted_iota(jnp.int32, (LANES, SSM_INNER), 0)
    el = lax.broadcasted_iota(jnp.int32, (LANES, SSM_INNER), 1)
    expand = (jnp.right_shift(el, 6) == eh).astype(BF16)
    th = lax.broadcasted_iota(jnp.int32, (SSM_INNER, LANES), 1)
    tl = lax.broadcasted_iota(jnp.int32, (SSM_INNER, LANES), 0)
    reduce = (jnp.right_shift(tl, 6) == th).astype(BF16)
    return ri, ci, expand, reduce


def _dot_f32(a, b):
    return jnp.dot(a, b, precision=HI, preferred_element_type=F32)


def _split3(x):
    hi = x.astype(BF16)
    r1 = x - hi.astype(F32)
    mid = r1.astype(BF16)
    lo = (r1 - mid.astype(F32)).astype(BF16)
    return hi, mid, lo


def _dot_hi(a, b):
    if a.dtype == BF16:
        return sum(jnp.dot(a, t, preferred_element_type=F32) for t in _split3(b))
    return sum(jnp.dot(t, b, preferred_element_type=F32) for t in _split3(a))


def _ssd_prelude(xbc_ref, dt_ref, dtt_ref, hpr_ref, hpc_ref, ri, ci, expand):
    ln = SSM_CHUNK
    xs = xbc_ref[:, 0:512]
    bm = xbc_ref[:, 512:640]
    cm = xbc_ref[:, 640:768]
    dtb_r = hpr_ref[0:1, :]
    aneg_r = -jnp.exp(hpr_ref[1:2, :])
    pre = dt_ref[...] + dtb_r
    dt = _softplus(pre)
    a = dt * aneg_r
    dtt = _softplus(dtt_ref[...] + hpc_ref[0:8, :])
    att = dtt * (-jnp.exp(hpc_ref[8:16, :]))
    tril = (ri >= ci).astype(BF16)
    triu = (ri <= ci).astype(BF16)
    acs = _dot_hi(tril, a)
    acst = _dot_hi(att, triu)
    acs_e = _dot_hi(acs, expand)
    dt_e = _dot_hi(dt, expand)
    last_e = acs_e[ln - 1:ln, :]
    e_e = jnp.exp(acs_e)
    w_e = jnp.exp(last_e - acs_e)
    dec_e = jnp.exp(last_e)
    xdt = xs * dt_e
    return dict(xs=xs, bm=bm, cm=cm, pre=pre, dt=dt, aneg_r=aneg_r, acs=acs, acst=acst, dt_e=dt_e, e_e=e_e,
                w_e=w_e, dec_e=dec_e, xdt=xdt, triu=triu)


def ssd_fwd(act, p, dt32, dtt, hp_rows, hp_cols, d_e, norm_w, name):
    t = act.shape[0]
    ln = SSM_CHUNK
    nc = t // ln

    def body(xbc_ref, dt_ref, z_ref, dtt_ref, hpr_ref, hpc_ref, d_ref, nw_ref, yc_ref, y_ref, sto_ref, st):
        @pl.when(pl.program_id(0) == 0)
        def _():
            st[...] = jnp.zeros_like(st)

        ri, ci, expand, _ = _ssd_consts()
        q = _ssd_prelude(xbc_ref, dt_ref, dtt_ref, hpr_ref, hpc_ref, ri, ci, expand)
        lane = lax.broadcasted_iota(jnp.int32, (1, LANES), 1)
        rown = lax.broadcasted_iota(jnp.int32, (LANES, 1), 0)
        low = lane < 64
        mask = ri >= ci
        xdt_b = q["xdt"].astype(BF16)
        xw_b = (q["xdt"] * q["w_e"]).astype(BF16)
        bt = q["bm"].T.astype(BF16)
        cb_ = q["cm"].astype(BF16)
        y_pairs = []
        for g in range(2):
            gm = low if g == 0 else jnp.logical_not(low)
            rm = (rown < 64) if g == 0 else (rown >= 64)
            cg = jnp.where(gm, cb_, jnp.zeros_like(cb_))
            cb = jnp.dot(cg, bt, preferred_element_type=F32)
            for pp in range(2):
                pi = 2 * g + pp
                sl = slice(LANES * pi, LANES * (pi + 1))
                xp = xdt_b[:, sl]
                yd = []
                for hh in range(2):
                    h = 2 * pi + hh
                    diff = q["acs"][:, h:h + 1] - q["acst"][h:h + 1, :]
                    lam = jnp.exp(jnp.where(mask, diff, -jnp.inf))
                    yd.append(jnp.dot((cb * lam).astype(BF16), xp, preferred_element_type=F32))
                sp = st[pi]
                sto_ref[0, pi] = sp
                yoff = jnp.dot(cg, sp.astype(BF16), preferred_element_type=F32) * q["e_e"][:, sl]
                upd = jnp.dot(bt, xw_b[:, sl], preferred_element_type=F32)
                st[pi] = q["dec_e"][:, sl] * sp + jnp.where(rm, upd, 0.0)
                y_pairs.append(jnp.where(low, yd[0], yd[1]) + yoff)
        y = jnp.concatenate(y_pairs, axis=1) + q["xs"] * d_ref[...]
        y_ref[...] = y
        yg = y * _silu(z_ref[...].astype(F32))
        for g in range(2):
            sl = slice(256 * g, 256 * (g + 1))
            seg = yg[:, sl]
            yc_ref[:, sl] = (seg * _rstd(seg) * nw_ref[:, sl]).astype(BF16)

    return pl.pallas_call(
        body, name=name, grid=(nc,),
        in_specs=[pl.BlockSpec((ln, 768), lambda c: (c, 0)), pl.BlockSpec((ln, LANES), lambda c: (c, 0)),
                  pl.BlockSpec((ln, 512), lambda c: (c, OFF_Z // 512)), pl.BlockSpec((8, ln), lambda c: (0, c)),
                  pl.BlockSpec((8, LANES), lambda c: (0, 0)), pl.BlockSpec((16, ln), lambda c: (0, 0)),
                  pl.BlockSpec((1, 512), lambda c: (0, 0)), pl.BlockSpec((1, 512), lambda c: (0, 0))],
        out_specs=[pl.BlockSpec((ln, 512), lambda c: (c, 0)), pl.BlockSpec((ln, 512), lambda c: (c, 0)),
                   pl.BlockSpec((1, 4, LANES, LANES), lambda c: (c, 0, 0, 0))],
        out_shape=[jax.ShapeDtypeStruct((t, 512), BF16), jax.ShapeDtypeStruct((t, 512), F32),
                   jax.ShapeDtypeStruct((nc, 4, LANES, LANES), F32)],
        scratch_shapes=[pltpu.VMEM((4, LANES, LANES), F32)],
        compiler_params=_cparams(("arbitrary",)),
    )(act, dt32, p, dtt, hp_rows, hp_cols, d_e, norm_w)


def ssd_bwd(dyc, y, act, p, dt32, dtt, states, hp_rows, hp_cols, d_e, norm_w, name):
    t = act.shape[0]
    ln = SSM_CHUNK
    nc = t // ln

    def body(dyc_ref, y_ref, xbc_ref, dt_ref, z_ref, dtt_ref, st_ref, hpr_ref, hpc_ref, d_ref, nw_ref,
             dz_ref, dact_ref, ddt_ref, dnw_ref, dd_ref, dhp_ref, ds):
        @pl.when(pl.program_id(0) == 0)
        def _():
            ds[...] = jnp.zeros_like(ds)
            dnw_ref[...] = jnp.zeros_like(dnw_ref)
            dd_ref[...] = jnp.zeros_like(dd_ref)
            dhp_ref[...] = jnp.zeros_like(dhp_ref)

        ri, ci, expand, reduce = _ssd_consts()
        q = _ssd_prelude(xbc_ref, dt_ref, dtt_ref, hpr_ref, hpc_ref, ri, ci, expand)
        lane = lax.broadcasted_iota(jnp.int32, (1, LANES), 1)
        rown = lax.broadcasted_iota(jnp.int32, (LANES, 1), 0)
        low = lane < 64
        mask = ri >= ci
        mask_t = ci >= ri
        xs, xdt, acs, acst = q["xs"], q["xdt"], q["acs"], q["acst"]
        yv, zv, nw = y_ref[...], z_ref[...].astype(F32), nw_ref[...]
        sg = _sigmoid(zv)
        zz = zv * sg
        yg = yv * zz
        dycv = dyc_ref[...]
        u = dycv * nw
        dyg_parts, dnw_parts = [], []
        for g in range(2):
            sl = slice(256 * g, 256 * (g + 1))
            seg = yg[:, sl]
            rr = _rstd(seg)
            nrm = seg * rr
            dyg_parts.append(rr * (u[:, sl] - nrm * jnp.mean(nrm * u[:, sl], axis=-1, keepdims=True)))
            dnw_parts.append(jnp.sum(dycv[:, sl] * nrm, axis=0, keepdims=True))
        dyg = jnp.concatenate(dyg_parts, axis=1)
        dnw_ref[...] += jnp.concatenate(dnw_parts, axis=1)
        dy = dyg * zz
        dz_ref[...] = dyg * yv * (sg * (1.0 + zv * (1.0 - sg)))
        dd_ref[...] += jnp.sum(dy * xs, axis=0, keepdims=True)
        dxs = dy * d_ref[...]
        dy_b = dy.astype(BF16)
        xdt_b = xdt.astype(BF16)
        xw_b = (xdt * q["w_e"]).astype(BF16)
        bt = q["bm"].T.astype(BF16)
        ct = q["cm"].T.astype(BF16)
        cb_ = q["cm"].astype(BF16)
        bb_ = q["bm"].astype(BF16)
        dacs = jnp.zeros((ln, LANES), F32)
        dc = jnp.zeros((ln, LANES), F32)
        db = jnp.zeros((ln, LANES), F32)
        dxdt_pairs, yoffdy_pairs, dwe_pairs, ddec_pairs = [], [], [], []
        for g in range(2):
            gm = low if g == 0 else jnp.logical_not(low)
            rm = (rown < 64) if g == 0 else (rown >= 64)
            cg = jnp.where(gm, cb_, jnp.zeros_like(cb_))
            bg = jnp.where(gm, bb_, jnp.zeros_like(bb_))
            cb = jnp.dot(cg, bt, preferred_element_type=F32)
            cbt = jnp.dot(bg, ct, preferred_element_type=F32)
            dcb = jnp.zeros((ln, ln), F32)
            dcbt = jnp.zeros((ln, ln), F32)
            for pp in range(2):
                pi = 2 * g + pp
                sl = slice(LANES * pi, LANES * (pi + 1))
                xp = xdt_b[:, sl]
                dyp = dy_b[:, sl]
                xpt = xdt[:, sl].T.astype(BF16)
                dypt = dy[:, sl].T.astype(BF16)
                dxdt_p = jnp.zeros((ln, LANES), F32)
                for hh in range(2):
                    h = 2 * pi + hh
                    hm = low if hh == 0 else jnp.logical_not(low)
                    col = acs[:, h:h + 1]
                    row = acst[h:h + 1, :]
                    lam = jnp.exp(jnp.where(mask, col - row, -jnp.inf))
                    lam_t = jnp.exp(jnp.where(mask_t, row - col, -jnp.inf))
                    m = cb * lam
                    m_t = cbt * lam_t
                    dyh = jnp.where(hm, dyp, jnp.zeros_like(dyp))
                    xh = jnp.where(hm, xp, jnp.zeros_like(xp))
                    dm = jnp.dot(dyh, xpt, preferred_element_type=F32)
                    dm_t = jnp.dot(xh, dypt, preferred_element_type=F32)
                    dcb = dcb + dm * lam
                    dcbt = dcbt + dm_t * lam_t
                    rs = jnp.sum(dm * m, axis=1, keepdims=True) - jnp.sum(dm_t * m_t, axis=1, keepdims=True)
                    dacs = dacs + jnp.where(lane == h, rs, 0.0)
                    dxdt_p = dxdt_p + jnp.dot(m_t.astype(BF16), dyh, preferred_element_type=F32)
                sp = st_ref[0, pi]
                sp_b = sp.astype(BF16)
                dsn = ds[pi]
                dsn_b = dsn.astype(BF16)
                e_p, w_p, dec_p = q["e_e"][:, sl], q["w_e"][:, sl], q["dec_e"][:, sl]
                yoff = jnp.dot(cg, sp_b, preferred_element_type=F32) * e_p
                dyo = dy[:, sl] * e_p
                dyo_b = dyo.astype(BF16)
                dc = dc + lax.dot_general(dyo_b, sp_b, _NT, preferred_element_type=F32)
                ds_prev = dec_p * dsn + jnp.where(rm, jnp.dot(ct, dyo_b, preferred_element_type=F32), 0.0)
                yoffdy_pairs.append(dy[:, sl] * yoff)
                dxw = jnp.dot(bg, dsn_b, preferred_element_type=F32)
                db = db + lax.dot_general(xw_b[:, sl], dsn_b, _NT, preferred_element_type=F32)
                dxdt_p = dxdt_p + dxw * w_p
                dwe_pairs.append(dxw * xdt[:, sl])
                ddec_pairs.append(jnp.sum(dsn * sp, axis=0, keepdims=True))
                ds[pi] = ds_prev
                dxdt_pairs.append(dxdt_p)
            dc = dc + jnp.dot(dcb.astype(BF16), bg, preferred_element_type=F32)
            db = db + jnp.dot(dcbt.astype(BF16), cg, preferred_element_type=F32)
        dxdt = jnp.concatenate(dxdt_pairs, axis=1)
        yoffdy = jnp.concatenate(yoffdy_pairs, axis=1)
        dwe = jnp.concatenate(dwe_pairs, axis=1)
        ddec_e = jnp.broadcast_to(jnp.concatenate(ddec_pairs, axis=1), (8, SSM_INNER))
        last = acs[ln - 1:ln, :]
        w_col = jnp.exp(last - acs)
        dw_col = _dot_hi(dwe, reduce) * w_col
        dacs = dacs + _dot_hi(yoffdy, reduce) - dw_col
        dlast = jnp.sum(dw_col, axis=0, keepdims=True) + jnp.exp(last) * _dot_hi(ddec_e, reduce)[0:1, :]
        rowi = lax.broadcasted_iota(jnp.int32, (ln, 1), 0)
        dacs = dacs + jnp.where(rowi == ln - 1, dlast, 0.0)
        da = _dot_hi(q["triu"], dacs)
        ddt = da * q["aneg_r"] + _dot_hi(dxdt * xs, reduce)
        ddt_raw = jnp.where(lane < SSM_HEADS, ddt * _sigmoid(q["pre"]), 0.0)
        ddt_ref[...] = ddt_raw
        dhp_ref[0:1, :] += jnp.sum(ddt_raw, axis=0, keepdims=True)
        dhp_ref[1:2, :] += jnp.where(lane < SSM_HEADS, jnp.sum(da * q["dt"], axis=0, keepdims=True) * q["aneg_r"], 0.0)
        dact_ref[:, 0:512] = dxs + dxdt * q["dt_e"]
        dact_ref[:, 512:640] = db
        dact_ref[:, 640:768] = dc

    rev = lambda c: nc - 1 - c
    return pl.pallas_call(
        body, name=name, grid=(nc,),
        in_specs=[pl.BlockSpec((ln, 512), lambda c: (rev(c), 0)), pl.BlockSpec((ln, 512), lambda c: (rev(c), 0)),
                  pl.BlockSpec((ln, 768), lambda c: (rev(c), 0)),
                  pl.BlockSpec((ln, LANES), lambda c: (rev(c), 0)),
                  pl.BlockSpec((ln, 512), lambda c: (rev(c), OFF_Z // 512)), pl.BlockSpec((8, ln), lambda c: (0, rev(c))),
                  pl.BlockSpec((1, 4, LANES, LANES), lambda c: (rev(c), 0, 0, 0)),
                  pl.BlockSpec((8, LANES), lambda c: (0, 0)), pl.BlockSpec((16, ln), lambda c: (0, 0)),
                  pl.BlockSpec((1, 512), lambda c: (0, 0)), pl.BlockSpec((1, 512), lambda c: (0, 0))],
        out_specs=[pl.BlockSpec((ln, 512), lambda c: (rev(c), 0)), pl.BlockSpec((ln, 768), lambda c: (rev(c), 0)),
                   pl.BlockSpec((ln, LANES), lambda c: (rev(c), 0)), pl.BlockSpec((1, 512), lambda c: (0, 0)),
                   pl.BlockSpec((1, 512), lambda c: (0, 0)), pl.BlockSpec((8, LANES), lambda c: (0, 0))],
        out_shape=[jax.ShapeDtypeStruct((t, 512), F32), jax.ShapeDtypeStruct((t, 768), F32),
                   jax.ShapeDtypeStruct((t, LANES), F32), jax.ShapeDtypeStruct((1, 512), F32),
                   jax.ShapeDtypeStruct((1, 512), F32), jax.ShapeDtypeStruct((8, LANES), F32)],
        scratch_shapes=[pltpu.VMEM((4, LANES, LANES), F32)],
        compiler_params=_cparams(("arbitrary",)),
    )(dyc, y, act, dt32, p, dtt, states, hp_rows, hp_cols, d_e, norm_w)


def mod_shard_fwd(c_all, mod_w, mod_b_shard, name):
    def body(c_ref, w_ref, b_ref, o_ref):
        sc = _silu(c_ref[...])
        for l in range(DEPTH):
            o_ref[l] = _dot_f32(sc, w_ref[l]) + b_ref[l]

    return pl.pallas_call(body, name=name, out_shape=jax.ShapeDtypeStruct((DEPTH, N_DEV, mod_w.shape[2]), F32),
                          compiler_params=_cparams())(c_all, mod_w, mod_b_shard)


def mod_w_grad(c_all, dmod_shard, name):
    def body(c_ref, d_ref, o_ref):
        sc = _silu(c_ref[...])
        for l in range(DEPTH):
            o_ref[l] = lax.dot_general(sc, d_ref[l], _TN, precision=HI, preferred_element_type=F32)

    return pl.pallas_call(body, name=name, out_shape=jax.ShapeDtypeStruct((DEPTH, D_MODEL, dmod_shard.shape[2]), F32),
                          compiler_params=_cparams())(c_all, dmod_shard)


_BIG = ("w_in", "sc_conv_w", "ssm_conv_w", "w_sc_out", "w_sb_out", "w_ssm_out", "w_o", "w_ffn_in", "w_ffn_out")
_ROW_SHARDED = ("w_o", "w_ffn_out")
_CONV = ("sc_conv_w", "ssm_conv_w")
_SMALL = ("mod_b", "g_pre_mix", "g_post_mix", "g_pre_ffn", "g_post_ffn", "ssm_conv_b", "ssm_dt_bias", "ssm_a_log",
          "ssm_d", "ssm_norm_w")
_WEIGHTS = ("mod_w", "mod_b", "g_pre_mix", "g_post_mix", "g_pre_ffn", "g_post_ffn", "w_in", "sc_conv_w", "ssm_conv_w",
            "ssm_conv_b", "ssm_dt_bias", "ssm_a_log", "ssm_d", "ssm_norm_w", "w_sc_out", "w_sb_out", "w_ssm_out", "w_o",
            "w_ffn_in", "w_ffn_out")


def _gathered_to_full(g, row_sharded):
    _, dep, r, c = g.shape
    if row_sharded:
        return g.transpose(1, 0, 2, 3).reshape(dep, N_DEV * r, c)
    return g.transpose(1, 2, 0, 3).reshape(dep, r, N_DEV * c)


def _full_to_slots(w, row_sharded):
    dep, r, c = w.shape
    if row_sharded:
        return w.reshape(dep, N_DEV, r // N_DEV, c).transpose(1, 0, 2, 3).reshape(N_DEV, dep * (r // N_DEV), c)
    return w.reshape(dep, r, N_DEV, c // N_DEV).transpose(2, 0, 1, 3).reshape(N_DEV, dep * r, c // N_DEV)


def _pad_in_proj(w):
    sc, qkv, z, xbc, dt, gates = (w[:, 0:768], w[:, 768:1536], w[:, 1536:2048], w[:, 2048:2816], w[:, 2816:2824],
                                  w[:, 2824:5896])
    pad = jnp.zeros((w.shape[0], OFF_Z - OFF_DT - 8), w.dtype)
    return jnp.concatenate([gates, sc, qkv, xbc, dt, pad, z], axis=1)


def _unpad_in_proj(w):
    return jnp.concatenate([w[:, OFF_SC:OFF_SC + 768], w[:, OFF_QKV:OFF_QKV + 768], w[:, OFF_Z:OFF_Z + 512],
                            w[:, OFF_XBC:OFF_XBC + 768], w[:, OFF_DT:OFF_DT + 8], w[:, 0:3072]], axis=1)


def _row(v):
    return v.reshape(1, -1)


def _local_step(x, target, mod, small, conv, big):
    lw, saved = [], []
    for l in range(DEPTH):
        w_in_p = _pad_in_proj(big["w_in"][l])
        w_cat = jnp.concatenate([big["w_sc_out"][l], big["w_sb_out"][l], big["w_ssm_out"][l]], axis=0)
        w_ffn_in_t = big["w_ffn_in"][l].T
        hp_rows = jnp.zeros((8, LANES), F32).at[0, :SSM_HEADS].set(small["ssm_dt_bias"][l]).at[1, :SSM_HEADS].set(
            small["ssm_a_log"][l])
        hp_cols = jnp.concatenate([jnp.broadcast_to(small["ssm_dt_bias"][l][:, None], (SSM_HEADS, SSM_CHUNK)),
                                   jnp.broadcast_to(small["ssm_a_log"][l][:, None], (SSM_HEADS, SSM_CHUNK))], axis=0)
        lw.append(dict(
            w_in_p=w_in_p, w_in_pt=w_in_p.T, w_cat=w_cat, w_cat_t=w_cat.T, w_o=big["w_o"][l], w_o_t=big["w_o"][l].T,
            w_ffn_in=big["w_ffn_in"][l], wg_t=w_ffn_in_t[:FFN_HIDDEN], wu_t=w_ffn_in_t[FFN_HIDDEN:],
            w_ffn_out=big["w_ffn_out"][l], w_ffn_out_t=big["w_ffn_out"][l].T,
            sc_w8=jnp.pad(conv["sc_conv_w"][l], ((0, 5), (0, 0))), ssm_w8=jnp.pad(conv["ssm_conv_w"][l], ((0, 4), (0, 0))),
            ssm_b=_row(small["ssm_conv_b"][l]), hp_rows=hp_rows, hp_cols=hp_cols,
            d_e=_row(jnp.repeat(small["ssm_d"][l], SSM_HEAD_DIM)), norm_w=_row(small["ssm_norm_w"][l]),
            g_pre_mix=_row(small["g_pre_mix"][l]), g_post_mix=_row(small["g_post_mix"][l]),
            g_pre_ffn=_row(small["g_pre_ffn"][l]), g_post_ffn=_row(small["g_post_ffn"][l]),
            shift1=mod[l, 0:1], scale1=mod[l, 1:2], gate1=mod[l, 2:3], shift2=mod[l, 3:4], scale2=mod[l, 4:5],
            gate2=mod[l, 5:6]))

    xl = x
    h = normmod_fwd(xl, lw[0]["g_pre_mix"], lw[0]["scale1"], lw[0]["shift1"], "normmod_fwd_0")
    dy = loss = None
    for l in range(DEPTH):
        w = lw[l]
        p = mm_nn([h], [w["w_in_p"]], BF16, f"in_proj_{l}")
        dt32 = mm_nn([h], [w["w_in_p"][:, OFF_DT:OFF_DT + LANES]], F32, f"in_proj_dt_{l}")
        ya, qkv, act = post_inproj(p, w["sc_w8"], w["ssm_w8"], w["ssm_b"], f"post_inproj_{l}")
        o, runs = sba_fwd(qkv, f"sba_fwd_{l}")
        dtt = dt32[:, :SSM_HEADS].T
        yc, ypre, states = ssd_fwd(act, p, dt32, dtt, w["hp_rows"], w["hp_cols"], w["d_e"], w["norm_w"], f"ssd_fwd_{l}")
        merged = branch_out_fwd(ya, o, yc, p, w["w_cat"], f"branch_fwd_{l}")
        mix = mm_nn([merged], [w["w_o"]], F32, f"out_proj_{l}")
        x1, h2 = resid_normmod_fwd(xl, mix, w["gate1"], w["g_post_mix"], w["g_pre_ffn"], w["scale2"], w["shift2"],
                                   f"resid_mix_{l}")
        gt, up, a = mm_swiglu_fwd(h2, w["w_ffn_in"], f"ffn_in_{l}")
        f = mm_nn([a], [w["w_ffn_out"]], F32, f"ffn_out_{l}")
        saved.append(dict(x=xl, h=h, p=p, ya=ya, qkv=qkv, act=act, o=o, runs=runs, dt32=dt32, dtt=dtt, yc=yc, ypre=ypre, states=states,
                          merged=merged, mix=mix, x1=x1, h2=h2, gt=gt, up=up, a=a, f=f))
        if l + 1 < DEPTH:
            nw = lw[l + 1]
            xl, h = resid_normmod_fwd(x1, f, w["gate2"], w["g_post_ffn"], nw["g_pre_mix"], nw["scale1"], nw["shift1"],
                                      f"resid_ffn_{l}")
        else:
            dy, loss = resid_loss(x1, f, w["gate2"], w["g_post_ffn"], target, "resid_loss")

    dmod = [None] * DEPTH
    gs = {k: [None] * DEPTH for k in _SMALL + _BIG}
    dxo = dy
    for l in reversed(range(DEPTH)):
        w, s = lw[l], saved[l]
        df, dgate2, gs["g_post_ffn"][l] = resid_bwd(dxo, s["f"], w["gate2"], w["g_post_ffn"], f"resid_ffn_bwd_{l}")
        dgt, dup = mm_swiglu_bwd(df, w["w_ffn_out_t"], s["gt"], s["up"], f"ffn_out_bwd_{l}")
        gs["w_ffn_out"][l] = mm_tn(s["a"], df, f"dw_ffn_out_{l}")
        dh2 = mm_nn([dgt, dup], [w["wg_t"], w["wu_t"]], F32, f"ffn_in_bwd_{l}")
        gs["w_ffn_in"][l] = jnp.concatenate([mm_tn(s["h2"], dgt, f"dw_ffn_gate_{l}"), mm_tn(s["h2"], dup, f"dw_ffn_up_{l}")],
                                            axis=1)
        dx1, dscale2, dshift2, gs["g_pre_ffn"][l] = normmod_bwd(dh2, s["x1"], dxo, w["g_pre_ffn"], w["scale2"],
                                                                f"normmod_ffn_bwd_{l}")
        dmix, dgate1, gs["g_post_mix"][l] = resid_bwd(dx1, s["mix"], w["gate1"], w["g_post_mix"], f"resid_mix_bwd_{l}")
        dmerged = mm_nn([dmix], [w["w_o_t"]], F32, f"out_proj_bwd_{l}")
        gs["w_o"][l] = mm_tn(s["merged"], dmix, f"dw_o_{l}")
        dgl, dya, dyb, dyc, dw_cat = branch_out_bwd(dmerged, s["ya"], s["o"], s["yc"], s["p"], w["w_cat"], w["w_cat_t"],
                                                    f"branch_bwd_{l}")
        gs["w_sc_out"][l], gs["w_sb_out"][l], gs["w_ssm_out"][l] = dw_cat[0:256], dw_cat[256:512], dw_cat[512:1024]
        dz, dact, ddt, dnw, dd_e, dhp = ssd_bwd(dyc, s["ypre"], s["act"], s["p"], s["dt32"], s["dtt"], s["states"], w["hp_rows"],
                                                w["hp_cols"], w["d_e"], w["norm_w"], f"ssd_bwd_{l}")
        gs["ssm_norm_w"][l] = dnw[0]
        gs["ssm_d"][l] = dd_e.reshape(SSM_HEADS, SSM_HEAD_DIM).sum(axis=1)
        gs["ssm_dt_bias"][l] = dhp[0, :SSM_HEADS]
        gs["ssm_a_log"][l] = dhp[1, :SSM_HEADS]
        dq, dk, dv = sba_bwd(s["qkv"], s["runs"], dyb, f"sba_bwd_{l}")
        dp, dscw, dssw, dssb = assemble_dp(dgl, dya, s["p"], w["sc_w8"], dq, dk, dv, dact, w["ssm_w8"], w["ssm_b"], ddt, dz,
                                           f"assemble_dp_{l}")
        gs["sc_conv_w"][l], gs["ssm_conv_w"][l], gs["ssm_conv_b"][l] = dscw[0:3], dssw[0:4], dssb[0]
        dh = mm_nn([dp], [w["w_in_pt"]], F32, f"in_proj_bwd_{l}")
        gs["w_in"][l] = _unpad_in_proj(mm_tn(s["h"], dp, f"dw_in_{l}"))
        dxo, dscale1, dshift1, gs["g_pre_mix"][l] = normmod_bwd(dh, s["x"], dx1, w["g_pre_mix"], w["scale1"],
                                                                f"normmod_mix_bwd_{l}")
        dmod[l] = jnp.concatenate([dshift1, dscale1, dgate1, dshift2, dscale2, dgate2], axis=0)
    for k in ("g_pre_mix", "g_post_mix", "g_pre_ffn", "g_post_ffn"):
        gs[k] = [g[0] for g in gs[k]]
    grads = {k: jnp.stack(v) for k, v in gs.items() if k != "mod_b"}
    return loss[0, 0], dxo, jnp.stack(dmod), grads


def kernel(x, c, mod_w, mod_b, g_pre_mix, g_post_mix, g_pre_ffn, g_post_ffn, w_in, sc_conv_w, ssm_conv_w, ssm_conv_b, ssm_dt_bias, ssm_a_log, ssm_d, ssm_norm_w, w_sc_out, w_sb_out, w_ssm_out, w_o, w_ffn_in, w_ffn_out, loss_target, m_mod_w, m_mod_b, m_g_pre_mix, m_g_post_mix, m_g_pre_ffn, m_g_post_ffn, m_w_in, m_sc_conv_w, m_ssm_conv_w, m_ssm_conv_b, m_ssm_dt_bias, m_ssm_a_log, m_ssm_d, m_ssm_norm_w, m_w_sc_out, m_w_sb_out, m_w_ssm_out, m_w_o, m_w_ffn_in, m_w_ffn_out, v_mod_w, v_mod_b, v_g_pre_mix, v_g_post_mix, v_g_pre_ffn, v_g_post_ffn, v_w_in, v_sc_conv_w, v_ssm_conv_w, v_ssm_conv_b, v_ssm_dt_bias, v_ssm_a_log, v_ssm_d, v_ssm_norm_w, v_w_sc_out, v_w_sb_out, v_w_ssm_out, v_w_o, v_w_ffn_in, v_w_ffn_out):
    wts = dict(mod_w=mod_w, mod_b=mod_b, g_pre_mix=g_pre_mix, g_post_mix=g_post_mix, g_pre_ffn=g_pre_ffn,
               g_post_ffn=g_post_ffn, w_in=w_in, sc_conv_w=sc_conv_w, ssm_conv_w=ssm_conv_w, ssm_conv_b=ssm_conv_b,
               ssm_dt_bias=ssm_dt_bias, ssm_a_log=ssm_a_log, ssm_d=ssm_d, ssm_norm_w=ssm_norm_w, w_sc_out=w_sc_out,
               w_sb_out=w_sb_out, w_ssm_out=w_ssm_out, w_o=w_o, w_ffn_in=w_ffn_in, w_ffn_out=w_ffn_out)
    ms = dict(mod_w=m_mod_w, mod_b=m_mod_b, g_pre_mix=m_g_pre_mix, g_post_mix=m_g_post_mix, g_pre_ffn=m_g_pre_ffn,
              g_post_ffn=m_g_post_ffn, w_in=m_w_in, sc_conv_w=m_sc_conv_w, ssm_conv_w=m_ssm_conv_w,
              ssm_conv_b=m_ssm_conv_b, ssm_dt_bias=m_ssm_dt_bias, ssm_a_log=m_ssm_a_log, ssm_d=m_ssm_d,
              ssm_norm_w=m_ssm_norm_w, w_sc_out=m_w_sc_out, w_sb_out=m_w_sb_out, w_ssm_out=m_w_ssm_out, w_o=m_w_o,
              w_ffn_in=m_w_ffn_in, w_ffn_out=m_w_ffn_out)
    vs = dict(mod_w=v_mod_w, mod_b=v_mod_b, g_pre_mix=v_g_pre_mix, g_post_mix=v_g_post_mix, g_pre_ffn=v_g_pre_ffn,
              g_post_ffn=v_g_post_ffn, w_in=v_w_in, sc_conv_w=v_sc_conv_w, ssm_conv_w=v_ssm_conv_w,
              ssm_conv_b=v_ssm_conv_b, ssm_dt_bias=v_ssm_dt_bias, ssm_a_log=v_ssm_a_log, ssm_d=v_ssm_d,
              ssm_norm_w=v_ssm_norm_w, w_sc_out=v_w_sc_out, w_sb_out=v_w_sb_out, w_ssm_out=v_w_ssm_out, w_o=v_w_o,
              w_ffn_in=v_w_ffn_in, w_ffn_out=v_w_ffn_out)
    me = 4 * lax.axis_index("x") + 2 * lax.axis_index("y") + lax.axis_index("c")
    mod_cols = mod_w.shape[2]

    pack1, sizes1 = _pack_rows([c, sc_conv_w, ssm_conv_w], F32, 8)
    got1 = all_gather_rows(pack1, "gather_c_conv").reshape(N_DEV, -1)
    c_all, sc_g, ssm_g = _unpack(got1, sizes1, [(D_MODEL,), sc_conv_w.shape, ssm_conv_w.shape])
    conv = dict(sc_conv_w=_gathered_to_full(sc_g, False), ssm_conv_w=_gathered_to_full(ssm_g, False))

    mod_b_shard = lax.dynamic_slice_in_dim(mod_b, me * mod_cols, mod_cols, axis=1).reshape(DEPTH, 1, mod_cols)
    mod_sh = mod_shard_fwd(c_all, mod_w, mod_b_shard, "mod_shard_fwd")
    pack2, sizes2 = _pack_rows([mod_sh], F32, 8)
    got2 = all_gather_rows(pack2, "gather_mod").reshape(N_DEV, -1)
    mod_all = _unpack(got2, sizes2, [mod_sh.shape])[0]
    mod_mine = lax.dynamic_index_in_dim(mod_all, me, axis=2, keepdims=False)
    mod = mod_mine.transpose(1, 0, 2).reshape(DEPTH, 6, D_MODEL)

    mm_names = [k for k in _BIG if k not in _CONV]
    gathered = all_gather_multi([wts[k].astype(BF16) for k in mm_names], "gather_weights")
    big = {k: _gathered_to_full(g, k in _ROW_SHARDED) for k, g in zip(mm_names, gathered)}

    small = {k: wts[k] for k in _SMALL}
    loss_part, dx, dmod, grads = _local_step(x[0], loss_target[0], mod, small, conv, big)
    loss = lax.psum(loss_part, ("x", "y", "c"))

    small_parts = [dmod.reshape(DEPTH, 6 * D_MODEL)] + [grads[k] for k in _SMALL[1:]]
    pack5, sizes5 = _pack_rows(small_parts, F32, 8)
    pack_conv, sizes_conv = _pack_rows([grads[k] for k in _CONV], F32, 8)
    got5, got_conv = all_gather_multi([pack5, pack_conv], "gather_small_grads")
    w5, _ = _pack_rows([wts[k] for k in _SMALL], F32, 8)
    m5, _ = _pack_rows([ms[k] for k in _SMALL], F32, 8)
    v5, _ = _pack_rows([vs[k] for k in _SMALL], F32, 8)
    res5 = adamw_flat(got5, w5, m5, v5, "adamw_small")
    small_out = [_unpack(r.reshape(-1), sizes5, [wts[k].shape for k in _SMALL]) for r in res5]

    conv_full = _unpack(got_conv.reshape(N_DEV, -1), sizes_conv, [grads[k].shape for k in _CONV])
    conv_mine = [lax.dynamic_slice_in_dim(g, me * wts[k].shape[2], wts[k].shape[2], axis=3)
                 for k, g in zip(_CONV, conv_full)]
    conv_slot_sizes = [math.prod(wts[k].shape) for k in _CONV]
    conv_slots = jnp.concatenate([g.reshape(N_DEV, -1) for g in conv_mine], axis=1)
    pad_c = -conv_slots.shape[1] % (8 * LANES)
    conv_slots = jnp.pad(conv_slots, ((0, 0), (0, pad_c))).reshape(N_DEV, -1, LANES)
    wc, _ = _pack_rows([wts[k] for k in _CONV], F32, 8)
    mc, _ = _pack_rows([ms[k] for k in _CONV], F32, 8)
    vc, _ = _pack_rows([vs[k] for k in _CONV], F32, 8)
    res_c = adamw_flat(conv_slots, wc, mc, vc, "adamw_conv")
    conv_out = [_unpack(r.reshape(-1), conv_slot_sizes, [wts[k].shape for k in _CONV]) for r in res_c]

    dmod_all = got5.reshape(N_DEV, -1)[:, :DEPTH * 6 * D_MODEL].reshape(N_DEV, DEPTH, 6 * D_MODEL)
    dmod_shard = lax.dynamic_slice_in_dim(dmod_all, me * mod_cols, mod_cols, axis=2).transpose(1, 0, 2)
    g_mod_w = mod_w_grad(c_all, dmod_shard, "mod_w_grad")
    rows2 = lambda a: a.reshape(a.shape[0] * a.shape[1], a.shape[2])
    res_mw = adamw_flat(rows2(g_mod_w)[None], rows2(mod_w), rows2(m_mod_w), rows2(v_mod_w), "adamw_mod_w")
    mod_w_out = [r.reshape(mod_w.shape) for r in res_mw]

    cidx = lax.axis_index("c")
    keeps, gives = [], []
    for k in mm_names:
        slots = _full_to_slots(grads[k], k in _ROW_SHARDED)
        by_core = slots.reshape(4, 2, *slots.shape[1:]).transpose(1, 0, 2, 3).astype(BF16)
        keeps.append(lax.dynamic_index_in_dim(by_core, cidx, 0, keepdims=False))
        gives.append(lax.dynamic_index_in_dim(by_core, 1 - cidx, 0, keepdims=False))
    gots = swap_with_sibling(gives, "swap_grads")
    pairs = []
    for k, keep, got in zip(mm_names, keeps, gots):
        rows4 = (4 * keep.shape[1], keep.shape[2])
        pairs.append(add_pairs(keep.reshape(rows4), got.reshape(rows4), f"add_pairs_{k}").reshape(keep.shape))
    recvs = exchange_chips(pairs, "exchange_grads")
    big_out = {}
    for k, recv in zip(mm_names, recvs):
        res = adamw_flat(recv, rows2(wts[k]), rows2(ms[k]), rows2(vs[k]), f"adamw_{k}")
        big_out[k] = [r.reshape(wts[k].shape) for r in res]

    outs = []
    for kind in range(4):
        by_name = {"mod_w": mod_w_out[kind]}
        by_name.update(zip(_SMALL, small_out[kind]))
        by_name.update(zip(_CONV, conv_out[kind]))
        by_name.update({k: v[kind] for k, v in big_out.items()})
        outs.extend(by_name[k] for k in _WEIGHTS)
    return (loss, dx[None], *outs)
```

```python
import functools
import math

import jax
import jax.numpy as jnp
from jax import lax
from jax.experimental import pallas as pl
from jax.experimental.pallas import tpu as pltpu

F32 = jnp.float32
BF16 = jnp.bfloat16
HI = lax.Precision.HIGHEST

N_DEV = 8
D_MODEL = 1024
DEPTH = 2
SC_WIDTH = 256
SB_WIDTH = 256
SB_HEAD_DIM = 64
SSM_INNER = 512
SSM_HEADS = 8
SSM_HEAD_DIM = 64
SSM_GROUPS = 2
SSM_STATE = 64
SSM_CHUNK = 256
SSM_CONV_DIM = 768
FFN_HIDDEN = 2816
NORM_EPS = 1e-6
IN_PROJ = 5896
LANES = 128
VMEM_LIMIT = 56 * 1024 * 1024

OFF_GATES = 0
OFF_SC = 3072
OFF_QKV = 3840
OFF_XBC = 4608
OFF_DT = 5376
OFF_Z = 5632
IN_PAD = 6144

ADAM_LR = 0.001
ADAM_B1 = 0.9
ADAM_B2 = 0.999
ADAM_EPS = 1e-08
ADAM_WD = 0.01
ADAM_STEP = 10

MESH_ID = pl.DeviceIdType.MESH


def _cparams(sem=None):
    return pltpu.CompilerParams(dimension_semantics=sem, vmem_limit_bytes=VMEM_LIMIT)


def _my_pos():
    return lax.axis_index("x"), lax.axis_index("y"), lax.axis_index("c")


def _peer(k, x, y, c):
    return (x ^ ((k >> 2) & 1), y ^ ((k >> 1) & 1), c ^ (k & 1))


def all_gather_rows(block, name):
    rows, lanes = block.shape

    def body(x_ref, out_ref, send_sems, recv_sems, local_sem):
        x, y, c = _my_pos()
        me, sibling = (x, y, c), (x, y, 1 - c)
        chips = [(1 - x, y), (x, 1 - y), (1 - x, 1 - y)]

        def slot(px, py, pc):
            return out_ref.at[4 * px + 2 * py + pc]

        def copy(k, blk, to, src=None):
            return pltpu.make_async_remote_copy(
                src_ref=slot(*blk) if src is None else src, dst_ref=slot(*blk),
                send_sem=send_sems.at[k], recv_sem=recv_sems.at[k], device_id=to, device_id_type=MESH_ID)

        mine = pltpu.make_async_copy(x_ref, slot(*me), local_sem)
        mine.start()
        first = [copy(0, me, sibling, src=x_ref)]
        first += [copy(1 + j, me, (*chip, c), src=x_ref) for j, chip in enumerate(chips)]
        for cp in first:
            cp.start()
        passed = [copy(4 + j, (*chip, c), sibling) for j, chip in enumerate(chips)]
        for j, chip in enumerate(chips):
            copy(1 + j, (*chip, c), me).wait_recv()
            passed[j].start()
        copy(0, sibling, me).wait_recv()
        for j, chip in enumerate(chips):
            copy(4 + j, (*chip, 1 - c), me).wait_recv()
        for cp in first + passed:
            cp.wait_send()
        mine.wait()

    return pl.pallas_call(
        body, name=name,
        out_shape=jax.ShapeDtypeStruct((N_DEV, rows, lanes), block.dtype),
        in_specs=[pl.BlockSpec(memory_space=pl.ANY)],
        out_specs=pl.BlockSpec(memory_space=pl.ANY),
        scratch_shapes=[pltpu.SemaphoreType.DMA((7,)), pltpu.SemaphoreType.DMA((7,)), pltpu.SemaphoreType.DMA],
    )(block)


def exchange_slots(send, name):
    _, rows, lanes = send.shape

    def body(s_ref, r_ref, send_sems, recv_sems, local_sem):
        x, y, c = _my_pos()
        me = 4 * x + 2 * y + c
        mine = pltpu.make_async_copy(s_ref.at[me], r_ref.at[me], local_sem)
        mine.start()
        copies = []
        for k in range(1, N_DEV):
            px, py, pc = _peer(k, x, y, c)
            cp = pltpu.make_async_remote_copy(
                src_ref=s_ref.at[4 * px + 2 * py + pc], dst_ref=r_ref.at[me],
                send_sem=send_sems.at[k - 1], recv_sem=recv_sems.at[k - 1],
                device_id=(px, py, pc), device_id_type=MESH_ID)
            cp.start()
            copies.append(cp)
        for cp in copies:
            cp.wait_recv()
        for cp in copies:
            cp.wait_send()
        mine.wait()

    return pl.pallas_call(
        body, name=name,
        out_shape=jax.ShapeDtypeStruct((N_DEV, rows, lanes), send.dtype),
        in_specs=[pl.BlockSpec(memory_space=pl.ANY)],
        out_specs=pl.BlockSpec(memory_space=pl.ANY),
        scratch_shapes=[pltpu.SemaphoreType.DMA((7,)), pltpu.SemaphoreType.DMA((7,)), pltpu.SemaphoreType.DMA],
    )(send)


def all_gather_multi(blocks, name):
    n = len(blocks)

    def body(*refs):
        x_refs, o_refs = refs[:n], refs[n:2 * n]
        send_sems, recv_sems, local_sems = refs[2 * n:]
        x, y, c = _my_pos()
        me, sibling = (x, y, c), (x, y, 1 - c)
        chips = [(1 - x, y), (x, 1 - y), (1 - x, 1 - y)]

        def slot(a, px, py, pc):
            return o_refs[a].at[4 * px + 2 * py + pc]

        def copy(a, k, blk, to, src=None):
            return pltpu.make_async_remote_copy(
                src_ref=slot(a, *blk) if src is None else src, dst_ref=slot(a, *blk),
                send_sem=send_sems.at[7 * a + k], recv_sem=recv_sems.at[7 * a + k], device_id=to, device_id_type=MESH_ID)

        mine = [pltpu.make_async_copy(x_refs[a], slot(a, *me), local_sems.at[a]) for a in range(n)]
        for cp in mine:
            cp.start()
        first = [copy(a, 1 + j, me, (*chip, c), src=x_refs[a]) for j, chip in enumerate(chips) for a in range(n)]
        first += [copy(a, 0, me, sibling, src=x_refs[a]) for a in range(n)]
        for cp in first:
            cp.start()
        passed = []
        for j, chip in enumerate(chips):
            for a in range(n):
                copy(a, 1 + j, (*chip, c), me).wait_recv()
                fwd = copy(a, 4 + j, (*chip, c), sibling)
                fwd.start()
                passed.append(fwd)
        for a in range(n):
            copy(a, 0, sibling, me).wait_recv()
            for j, chip in enumerate(chips):
                copy(a, 4 + j, (*chip, 1 - c), me).wait_recv()
        for cp in first + passed:
            cp.wait_send()
        for cp in mine:
            cp.wait()

    any_spec = pl.BlockSpec(memory_space=pl.ANY)
    return pl.pallas_call(
        body, name=name,
        out_shape=[jax.ShapeDtypeStruct((N_DEV,) + b.shape, b.dtype) for b in blocks],
        in_specs=[any_spec] * n, out_specs=[any_spec] * n,
        scratch_shapes=[pltpu.SemaphoreType.DMA((7 * n,)), pltpu.SemaphoreType.DMA((7 * n,)), pltpu.SemaphoreType.DMA((n,))],
    )(*blocks)


def exchange_multi(sends, name):
    n = len(sends)

    def body(*refs):
        s_refs, r_refs = refs[:n], refs[n:2 * n]
        send_sems, recv_sems, local_sems = refs[2 * n:]
        x, y, c = _my_pos()
        me = 4 * x + 2 * y + c
        mine = [pltpu.make_async_copy(s_refs[a].at[me], r_refs[a].at[me], local_sems.at[a]) for a in range(n)]
        for cp in mine:
            cp.start()
        copies = []
        for k in (2, 4, 6, 3, 5, 7, 1):
            px, py, pc = _peer(k, x, y, c)
            for a in range(n):
                cp = pltpu.make_async_remote_copy(
                    src_ref=s_refs[a].at[4 * px + 2 * py + pc], dst_ref=r_refs[a].at[me],
                    send_sem=send_sems.at[7 * a + k - 1], recv_sem=recv_sems.at[7 * a + k - 1],
                    device_id=(px, py, pc), device_id_type=MESH_ID)
                cp.start()
                copies.append(cp)
        for cp in copies:
            cp.wait_recv()
        for cp in copies:
            cp.wait_send()
        for cp in mine:
            cp.wait()

    any_spec = pl.BlockSpec(memory_space=pl.ANY)
    return pl.pallas_call(
        body, name=name,
        out_shape=[jax.ShapeDtypeStruct(s.shape, s.dtype) for s in sends],
        in_specs=[any_spec] * n, out_specs=[any_spec] * n,
        scratch_shapes=[pltpu.SemaphoreType.DMA((7 * n,)), pltpu.SemaphoreType.DMA((7 * n,)), pltpu.SemaphoreType.DMA((n,))],
    )(*sends)


def swap_with_sibling(gives, name):
    n = len(gives)

    def body(*refs):
        g_refs, r_refs = refs[:n], refs[n:2 * n]
        send_sems, recv_sems = refs[2 * n:]
        x, y, c = _my_pos()
        copies = [pltpu.make_async_remote_copy(
            src_ref=g_refs[a], dst_ref=r_refs[a], send_sem=send_sems.at[a], recv_sem=recv_sems.at[a],
            device_id=(x, y, 1 - c), device_id_type=MESH_ID) for a in range(n)]
        for cp in copies:
            cp.start()
        for cp in copies:
            cp.wait_recv()
        for cp in copies:
            cp.wait_send()

    any_spec = pl.BlockSpec(memory_space=pl.ANY)
    return pl.pallas_call(
        body, name=name, out_shape=[jax.ShapeDtypeStruct(g.shape, g.dtype) for g in gives],
        in_specs=[any_spec] * n, out_specs=[any_spec] * n,
        scratch_shapes=[pltpu.SemaphoreType.DMA((n,)), pltpu.SemaphoreType.DMA((n,))],
    )(*gives)


def exchange_chips(sends, name):
    n = len(sends)

    def body(*refs):
        s_refs, r_refs = refs[:n], refs[n:2 * n]
        send_sems, recv_sems, local_sems = refs[2 * n:]
        x, y, c = _my_pos()
        me = 2 * x + y
        mine = [pltpu.make_async_copy(s_refs[a].at[me], r_refs[a].at[me], local_sems.at[a]) for a in range(n)]
        for cp in mine:
            cp.start()
        copies = []
        for k in (2, 1, 3):
            px, py = x ^ (k >> 1), y ^ (k & 1)
            for a in range(n):
                cp = pltpu.make_async_remote_copy(
                    src_ref=s_refs[a].at[2 * px + py], dst_ref=r_refs[a].at[me],
                    send_sem=send_sems.at[3 * a + k - 1], recv_sem=recv_sems.at[3 * a + k - 1],
                    device_id=(px, py, c), device_id_type=MESH_ID)
                cp.start()
                copies.append(cp)
        for cp in copies:
            cp.wait_recv()
        for cp in copies:
            cp.wait_send()
        for cp in mine:
            cp.wait()

    any_spec = pl.BlockSpec(memory_space=pl.ANY)
    return pl.pallas_call(
        body, name=name, out_shape=[jax.ShapeDtypeStruct(s.shape, s.dtype) for s in sends],
        in_specs=[any_spec] * n, out_specs=[any_spec] * n,
        scratch_shapes=[pltpu.SemaphoreType.DMA((3 * n,)), pltpu.SemaphoreType.DMA((3 * n,)), pltpu.SemaphoreType.DMA((n,))],
    )(*sends)


def add_pairs(a, b, name, tr=512):
    rows, cols = a.shape
    tr = max(d for d in range(16, min(tr, rows) + 1, 16) if rows % d == 0)

    def body(a_ref, b_ref, o_ref):
        o_ref[...] = (a_ref[...].astype(F32) + b_ref[...].astype(F32)).astype(BF16)

    tile = pl.BlockSpec((tr, cols), lambda i: (i, 0))
    return pl.pallas_call(body, name=name, grid=(rows // tr,), in_specs=[tile, tile], out_specs=tile,
                          out_shape=jax.ShapeDtypeStruct((rows, cols), BF16), compiler_params=_cparams(("parallel",)))(a, b)


def _pack_rows(parts, dtype, row_multiple):
    flat = [p.astype(dtype).reshape(-1) for p in parts]
    sizes = [f.shape[0] for f in flat]
    total = sum(sizes)
    quantum = LANES * row_multiple
    padded = -(-total // quantum) * quantum
    if padded > total:
        flat.append(jnp.zeros((padded - total,), dtype))
    return jnp.concatenate(flat).reshape(padded // LANES, LANES), sizes


def _unpack(flat, sizes, shapes):
    out, off = [], 0
    lead = flat.shape[:-1]
    for n, shp in zip(sizes, shapes):
        out.append(flat[..., off:off + n].reshape(lead + tuple(shp)))
        off += n
    return out


def rows_call(name, body, n_rows, tr, ins, outs, scratch=(), aliases=None):
    n_tiles = n_rows // tr
    assert n_tiles * tr == n_rows
    in_specs, arrays = [], []
    for arr, kind in ins:
        arrays.append(arr)
        if kind == "row":
            in_specs.append(pl.BlockSpec((tr, arr.shape[1]), lambda i: (i, 0)))
        elif kind == "any":
            in_specs.append(pl.BlockSpec(memory_space=pl.ANY))
        elif kind == "full":
            in_specs.append(pl.BlockSpec(arr.shape, lambda i, nd=arr.ndim: (0,) * nd))
        elif kind[0] == "row":
            _, w, ci = kind
            in_specs.append(pl.BlockSpec((tr, w), lambda i, ci=ci: (i, ci)))
        elif kind[0] == "prev8":
            _, w, ci = kind
            hr = 8 * (4 // arr.dtype.itemsize)
            in_specs.append(pl.BlockSpec((hr, w), lambda i, ci=ci, hr=hr: (jnp.maximum(i * (tr // hr) - 1, 0), ci)))
        elif kind[0] == "next8":
            _, w, ci = kind
            hr = 8 * (4 // arr.dtype.itemsize)
            last = n_rows // hr - 1
            in_specs.append(pl.BlockSpec((hr, w), lambda i, ci=ci, last=last, hr=hr: (jnp.minimum((i + 1) * (tr // hr), last), ci)))
        else:
            raise ValueError(kind)
    out_specs, out_shapes = [], []
    for shape, dtype, kind in outs:
        out_shapes.append(jax.ShapeDtypeStruct(shape, dtype))
        if kind == "row":
            out_specs.append(pl.BlockSpec((tr, shape[1]), lambda i: (i, 0)))
        elif kind[0] == "row":
            _, w, ci = kind
            out_specs.append(pl.BlockSpec((tr, w), lambda i, ci=ci: (i, ci)))
        else:
            out_specs.append(pl.BlockSpec(shape, lambda i, nd=len(shape): (0,) * nd))
    has_acc = any(k == "acc" for _, _, k in outs)
    return pl.pallas_call(
        body, name=name, grid=(n_tiles,), in_specs=in_specs, out_specs=out_specs, out_shape=out_shapes,
        scratch_shapes=list(scratch), input_output_aliases=dict(aliases or {}),
        compiler_params=_cparams(("arbitrary",) if has_acc else ("parallel",)),
    )(*arrays)


def _prev8(ref):
    n = ref.shape[0]
    return ref[n - 8:n, :].astype(F32)


def _next8(ref):
    return ref[0:8, :].astype(F32)


def _acc(ref, val):
    @pl.when(pl.program_id(0) == 0)
    def _():
        ref[...] = jnp.zeros_like(ref)
    ref[...] += val


def _rstd(x):
    return lax.rsqrt(jnp.mean(x * x, axis=-1, keepdims=True) + NORM_EPS)


def _sigmoid(x):
    return 1.0 / (1.0 + jnp.exp(-x))


def _silu(x):
    return x * _sigmoid(x)


def _dsilu(x):
    s = _sigmoid(x)
    return s * (1.0 + x * (1.0 - s))


def _softplus(x):
    return jnp.maximum(x, 0.0) + jnp.log(1.0 + jnp.exp(-jnp.abs(x)))


def _log_sigmoid_neg(x):
    t = -x
    return jnp.minimum(t, 0.0) - jnp.log(1.0 + jnp.exp(jnp.minimum(x, t)))


def normmod_fwd(x, g, scale, shift, name):
    t, d = x.shape

    def body(x_ref, g_ref, sc_ref, sh_ref, h_ref):
        xv = x_ref[...]
        h = xv * _rstd(xv) * g_ref[...] * (1.0 + sc_ref[...]) + sh_ref[...]
        h_ref[...] = h.astype(BF16)

    return rows_call(name, body, t, 512, [(x, "row"), (g, "full"), (scale, "full"), (shift, "full")],
                     [((t, d), BF16, "row")])[0]


def resid_normmod_fwd(x, f, gate, g_post, g_pre, scale, shift, name):
    t, d = x.shape

    def body(x_ref, f_ref, gate_ref, gp_ref, g_ref, sc_ref, sh_ref, xo_ref, h_ref):
        fv = f_ref[...]
        xn = x_ref[...] + gate_ref[...] * (fv * _rstd(fv) * gp_ref[...])
        xo_ref[...] = xn
        h = xn * _rstd(xn) * g_ref[...] * (1.0 + sc_ref[...]) + sh_ref[...]
        h_ref[...] = h.astype(BF16)

    return rows_call(name, body, t, 512,
                     [(x, "row"), (f, "row"), (gate, "full"), (g_post, "full"), (g_pre, "full"), (scale, "full"),
                      (shift, "full")],
                     [((t, d), F32, "row"), ((t, d), BF16, "row")])


def resid_loss(x, f, gate, g_post, target, name):
    t, d = x.shape

    def body(x_ref, f_ref, gate_ref, gp_ref, tg_ref, dy_ref, loss_ref):
        fv = f_ref[...]
        yv = x_ref[...] + gate_ref[...] * (fv * _rstd(fv) * gp_ref[...])
        err = yv - tg_ref[...]
        dy_ref[...] = err * (1.0 / d)
        part = 0.5 * jnp.sum(jnp.mean(err * err, axis=-1, keepdims=True), axis=0, keepdims=True)
        _acc(loss_ref, jnp.broadcast_to(part, loss_ref.shape))

    return rows_call(name, body, t, 512,
                     [(x, "row"), (f, "row"), (gate, "full"), (g_post, "full"), (target, "row")],
                     [((t, d), F32, "row"), ((8, LANES), F32, "acc")])


def resid_bwd(dx, f, gate, g_post, name):
    t, d = dx.shape

    def body(dx_ref, f_ref, gate_ref, gp_ref, df_ref, dgate_ref, dg_ref):
        fv, dxv, gp = f_ref[...], dx_ref[...], gp_ref[...]
        r = _rstd(fv)
        fn = fv * r
        _acc(dgate_ref, jnp.sum(dxv * (fn * gp), axis=0, keepdims=True))
        dn = dxv * gate_ref[...]
        _acc(dg_ref, jnp.sum(dn * fn, axis=0, keepdims=True))
        u = dn * gp
        df = r * (u - fn * jnp.mean(fn * u, axis=-1, keepdims=True))
        df_ref[...] = df.astype(BF16)

    return rows_call(name, body, t, 512, [(dx, "row"), (f, "row"), (gate, "full"), (g_post, "full")],
                     [((t, d), BF16, "row"), ((1, d), F32, "acc"), ((1, d), F32, "acc")])


def normmod_bwd(dh, x, dx_in, g, scale, name):
    t, d = x.shape

    def body(dh_ref, x_ref, dxi_ref, g_ref, sc_ref, dx_ref, dsc_ref, dsh_ref, dg_ref):
        xv, dhv, gv = x_ref[...], dh_ref[...], g_ref[...]
        r = _rstd(xv)
        xn = xv * r
        _acc(dsc_ref, jnp.sum(dhv * (xn * gv), axis=0, keepdims=True))
        _acc(dsh_ref, jnp.sum(dhv, axis=0, keepdims=True))
        dn = dhv * (1.0 + sc_ref[...])
        _acc(dg_ref, jnp.sum(dn * xn, axis=0, keepdims=True))
        u = dn * gv
        dx_ref[...] = dxi_ref[...] + r * (u - xn * jnp.mean(xn * u, axis=-1, keepdims=True))

    return rows_call(name, body, t, 512, [(dh, "row"), (x, "row"), (dx_in, "row"), (g, "full"), (scale, "full")],
                     [((t, d), F32, "row"), ((1, d), F32, "acc"), ((1, d), F32, "acc"), ((1, d), F32, "acc")])


def _pick(n, prefs):
    for p in prefs:
        if n % p == 0:
            return p
    return n


def mm_nn(a_list, b_list, out_dtype, name, tm=1024, tn=None, tk=None):
    m, k = a_list[0].shape
    n = b_list[0].shape[1]
    tm = min(tm, m)
    tn = tn or _pick(n, (1024, 768, 512, 256, 128))
    tk = tk or _pick(k, (1024, 1408, 512, 256))
    nk = k // tk
    npair = len(a_list)

    if nk == 1 and npair == 1:
        def body1(a_ref, b_ref, o_ref):
            o_ref[...] = jnp.dot(a_ref[...], b_ref[...], preferred_element_type=F32).astype(o_ref.dtype)

        return pl.pallas_call(
            body1, name=name, grid=(m // tm, n // tn),
            in_specs=[pl.BlockSpec((tm, k), lambda i, j: (i, 0)), pl.BlockSpec((k, tn), lambda i, j: (0, j))],
            out_specs=pl.BlockSpec((tm, tn), lambda i, j: (i, j)),
            out_shape=jax.ShapeDtypeStruct((m, n), out_dtype),
            compiler_params=_cparams(("parallel", "parallel")),
        )(a_list[0], b_list[0])

    def body(*refs):
        a_refs, b_refs = refs[:npair], refs[npair:2 * npair]
        o_ref, acc = refs[2 * npair], refs[2 * npair + 1]
        kk = pl.program_id(2)

        @pl.when(kk == 0)
        def _():
            acc[...] = jnp.zeros_like(acc)

        s = acc[...]
        for a_ref, b_ref in zip(a_refs, b_refs):
            s = s + jnp.dot(a_ref[...], b_ref[...], preferred_element_type=F32)
        acc[...] = s

        @pl.when(kk == nk - 1)
        def _():
            o_ref[...] = acc[...].astype(o_ref.dtype)

    return pl.pallas_call(
        body, name=name, grid=(m // tm, n // tn, nk),
        in_specs=[pl.BlockSpec((tm, tk), lambda i, j, kk: (i, kk))] * npair
        + [pl.BlockSpec((tk, tn), lambda i, j, kk: (kk, j))] * npair,
        out_specs=pl.BlockSpec((tm, tn), lambda i, j, kk: (i, j)),
        out_shape=jax.ShapeDtypeStruct((m, n), out_dtype),
        scratch_shapes=[pltpu.VMEM((tm, tn), F32)],
        compiler_params=_cparams(("parallel", "parallel", "arbitrary")),
    )(*a_list, *b_list)


def mm_nt(a_list, b_list, b_koff, out_dtype, name, tm=1024):
    m, k = a_list[0].shape
    n = b_list[0].shape[0]
    tm = min(tm, m)
    tn = _pick(n, (1024, 512, 256))
    tk = _pick(k, (1024, 1408, 512, 256))
    nk = k // tk
    npair = len(a_list)
    koff = [o // tk for o in b_koff]
    nt_dims = (((1,), (1,)), ((), ()))

    def body(*refs):
        a_refs, b_refs = refs[:npair], refs[npair:2 * npair]
        o_ref, acc = refs[2 * npair], refs[2 * npair + 1]
        kk = pl.program_id(2)

        @pl.when(kk == 0)
        def _():
            acc[...] = jnp.zeros_like(acc)

        s = acc[...]
        for a_ref, b_ref in zip(a_refs, b_refs):
            s = s + lax.dot_general(a_ref[...], b_ref[...], nt_dims, preferred_element_type=F32)
        acc[...] = s

        @pl.when(kk == nk - 1)
        def _():
            o_ref[...] = acc[...].astype(o_ref.dtype)

    return pl.pallas_call(
        body, name=name, grid=(m // tm, n // tn, nk),
        in_specs=[pl.BlockSpec((tm, tk), lambda i, j, kk: (i, kk))] * npair
        + [pl.BlockSpec((tn, tk), lambda i, j, kk, o=o: (j, kk + o)) for o in koff],
        out_specs=pl.BlockSpec((tm, tn), lambda i, j, kk: (i, j)),
        out_shape=jax.ShapeDtypeStruct((m, n), out_dtype),
        scratch_shapes=[pltpu.VMEM((tm, tn), F32)],
        compiler_params=_cparams(("parallel", "parallel", "arbitrary")),
    )(*a_list, *b_list)


def mm_tn(a, b, name, tt=512):
    t, ka = a.shape
    n = b.shape[1]
    ta = _pick(ka, (1024, 1408, 512, 256))
    tn = _pick(n, (2048, 1024, 1408, 512, 256))
    nt = t // tt

    def body(a_ref, b_ref, o_ref):
        @pl.when(pl.program_id(2) == 0)
        def _():
            o_ref[...] = jnp.zeros_like(o_ref)

        o_ref[...] += lax.dot_general(a_ref[...], b_ref[...], (((0,), (0,)), ((), ())), preferred_element_type=F32)

    return pl.pallas_call(
        body, name=name, grid=(ka // ta, n // tn, nt),
        in_specs=[pl.BlockSpec((tt, ta), lambda i, j, s: (s, i)), pl.BlockSpec((tt, tn), lambda i, j, s: (s, j))],
        out_specs=pl.BlockSpec((ta, tn), lambda i, j, s: (i, j)),
        out_shape=jax.ShapeDtypeStruct((ka, n), F32),
        compiler_params=_cparams(("parallel", "parallel", "arbitrary")),
    )(a, b)


def mm_swiglu_fwd(h, w_ffn_in, name, tm=512, tn=1408):
    m, k = h.shape
    nh = FFN_HIDDEN // tn

    def body(h_ref, wg_ref, wu_ref, gt_ref, up_ref, a_ref):
        hv = h_ref[...]
        gt = jnp.dot(hv, wg_ref[...], preferred_element_type=F32)
        up = jnp.dot(hv, wu_ref[...], preferred_element_type=F32)
        gt_ref[...] = gt.astype(BF16)
        up_ref[...] = up.astype(BF16)
        a_ref[...] = (_silu(gt) * up).astype(BF16)

    shp = jax.ShapeDtypeStruct((m, FFN_HIDDEN), BF16)
    ospec = pl.BlockSpec((tm, tn), lambda i, j: (i, j))
    return pl.pallas_call(
        body, name=name, grid=(m // tm, nh),
        in_specs=[pl.BlockSpec((tm, k), lambda i, j: (i, 0)), pl.BlockSpec((k, tn), lambda i, j: (0, j)),
                  pl.BlockSpec((k, tn), lambda i, j: (0, j + nh))],
        out_specs=[ospec, ospec, ospec], out_shape=[shp, shp, shp],
        compiler_params=_cparams(("parallel", "parallel")),
    )(h, w_ffn_in, w_ffn_in)


def mm_swiglu_bwd(df, w_out, gt, up, name, tm=512, tn=1408):
    m, k = df.shape

    def body(df_ref, w_ref, gt_ref, up_ref, dgt_ref, dup_ref):
        da = lax.dot_general(df_ref[...], w_ref[...], (((1,), (1,)), ((), ())), preferred_element_type=F32)
        gtv = gt_ref[...].astype(F32)
        sg = _sigmoid(gtv)
        dgt_ref[...] = (da * up_ref[...].astype(F32) * (sg * (1.0 + gtv * (1.0 - sg)))).astype(BF16)
        dup_ref[...] = (da * (gtv * sg)).astype(BF16)

    shp = jax.ShapeDtypeStruct((m, FFN_HIDDEN), BF16)
    tile = pl.BlockSpec((tm, tn), lambda i, j: (i, j))
    return pl.pallas_call(
        body, name=name, grid=(m // tm, FFN_HIDDEN // tn),
        in_specs=[pl.BlockSpec((tm, k), lambda i, j: (i, 0)), pl.BlockSpec((tn, k), lambda i, j: (j, 0)), tile, tile],
        out_specs=[tile, tile], out_shape=[shp, shp],
        compiler_params=_cparams(("parallel", "parallel")),
    )(df, w_out, gt, up)


def _shift_down(x, prev8, j):
    if j == 0:
        return x
    xr = pltpu.roll(x, j, 0)
    pr = pltpu.roll(prev8, j, 0)
    row = lax.broadcasted_iota(jnp.int32, (8, x.shape[1]), 0)
    head = jnp.where(row < j, pr, xr[:8])
    return head if x.shape[0] == 8 else jnp.concatenate([head, xr[8:]], axis=0)


def _shift_up(x, next8, j):
    if j == 0:
        return x
    n = x.shape[0]
    xr = pltpu.roll(x, n - j, 0)
    nr = pltpu.roll(next8, 8 - j, 0)
    row = lax.broadcasted_iota(jnp.int32, (8, x.shape[1]), 0)
    return jnp.concatenate([xr[:n - 8], jnp.where(row >= 8 - j, nr, xr[n - 8:])], axis=0)


def _conv_taps(x, prev8, w_ref, taps):
    out = None
    for k in range(taps):
        term = w_ref[k:k + 1, :] * _shift_down(x, prev8, taps - 1 - k)
        out = term if out is None else out + term
    return out


def post_inproj(p, sc_w, ssm_w, ssm_b, name, tr=512):
    t = p.shape[0]

    def body(sc_ref, scp_ref, qkv_ref, xbc_ref, xbcp_ref, scw_ref, sw_ref, sb_ref, ya_ref, qkvo_ref, act_ref):
        first = (pl.program_id(0) > 0).astype(F32)
        sc = sc_ref[...].astype(F32)
        scp = _prev8(scp_ref) * first
        u = sc[:, 256:512] * sc[:, 512:768]
        up = scp[:, 256:512] * scp[:, 512:768]
        ya_ref[...] = (sc[:, 0:256] * _conv_taps(u, up, scw_ref, 3)).astype(BF16)
        qkv = qkv_ref[...]
        qkvo_ref[:, 0:256] = (qkv[:, 0:256].astype(F32) * 0.125).astype(BF16)
        qkvo_ref[:, 256:768] = qkv[:, 256:768].astype(BF16)
        xc = _conv_taps(xbc_ref[...].astype(F32), _prev8(xbcp_ref) * first, sw_ref, 4) + sb_ref[...]
        act_ref[...] = _silu(xc)

    return rows_call(
        name, body, t, tr,
        [(p, ("row", 768, OFF_SC // 768)), (p, ("prev8", 768, OFF_SC // 768)), (p, ("row", 768, OFF_QKV // 768)),
         (p, ("row", 768, OFF_XBC // 768)), (p, ("prev8", 768, OFF_XBC // 768)),
         (sc_w, "full"), (ssm_w, "full"), (ssm_b, "full")],
        [((t, 256), BF16, "row"), ((t, 768), BF16, "row"), ((t, 768), F32, "row")])


def branch_out_fwd(ya, yb, yc, p, w_cat, name, tr=256):
    t = p.shape[0]

    def body(ya_ref, yb_ref, yc_ref, gl_ref, w_ref, o_ref):
        y_a = jnp.dot(ya_ref[...], w_ref[0:256, :], preferred_element_type=F32)
        y_b = jnp.dot(yb_ref[...].astype(BF16), w_ref[256:512, :], preferred_element_type=F32)
        y_c = jnp.dot(yc_ref[...], w_ref[512:1024, :], preferred_element_type=F32)
        m = (_sigmoid(gl_ref[:, 0:1024].astype(F32)) * y_a + _sigmoid(gl_ref[:, 1024:2048].astype(F32)) * y_b
             + _sigmoid(gl_ref[:, 2048:3072].astype(F32)) * y_c)
        o_ref[...] = m.astype(BF16)

    return rows_call(name, body, t, tr,
                     [(ya, "row"), (yb, "row"), (yc, "row"), (p, ("row", 3072, 0)), (w_cat, "full")],
                     [((t, D_MODEL), BF16, "row")])[0]


def branch_out_bwd(dm, ya, yb, yc, p, w_cat, name, tr=256):
    t = p.shape[0]
    tn_dims = (((0,), (0,)), ((), ()))

    def body(dm_ref, ya_ref, yb_ref, yc_ref, gl_ref, w_ref, dgl_ref, dya_ref, dyb_ref, dyc_ref, dw_ref):
        @pl.when(pl.program_id(0) == 0)
        def _():
            dw_ref[...] = jnp.zeros_like(dw_ref)

        dmv = dm_ref[...]
        ins = (ya_ref[...], yb_ref[...].astype(BF16), yc_ref[...])
        rows = ((0, 256), (256, 512), (512, 1024))
        outs = (dya_ref, dyb_ref, dyc_ref)
        for i in range(3):
            r0, r1 = rows[i]
            y = jnp.dot(ins[i], w_ref[r0:r1, :], preferred_element_type=F32)
            s = _sigmoid(gl_ref[:, 1024 * i:1024 * (i + 1)].astype(F32))
            dgl_ref[:, 1024 * i:1024 * (i + 1)] = (dmv * y * s * (1.0 - s)).astype(BF16)
            dy = (dmv * s).astype(BF16)
            outs[i][...] = lax.dot_general(dy, w_ref[r0:r1, :], _NT, preferred_element_type=F32)
            dw_ref[r0:r1, :] += lax.dot_general(ins[i], dy, tn_dims, preferred_element_type=F32)

    return rows_call(name, body, t, tr,
                     [(dm, "row"), (ya, "row"), (yb, "row"), (yc, "row"), (p, ("row", 3072, 0)), (w_cat, "full")],
                     [((t, IN_PAD), BF16, ("row", 3072, 0)), ((t, 256), F32, "row"), ((t, 256), F32, "row"),
                      ((t, 512), F32, "row"), ((D_MODEL, D_MODEL), F32, "acc")])


def assemble_dp(dp, dya, p, sc_w, dq, dk, dv, dact, ssm_w, ssm_b, ddt, dz, name, tr=256):
    t = p.shape[0]
    n_tiles = t // tr
    sci, xi = OFF_SC // 768, OFF_XBC // 768
    base = OFF_SC
    assert base == IN_PAD - base
    o_sc, o_qkv, o_xbc, o_dt, o_z, o_end = (c - base for c in (OFF_SC, OFF_QKV, OFF_XBC, OFF_DT, OFF_Z, IN_PAD))

    def body(dp_ref, dya_ref, dyan_ref, sc_ref, scp_ref, scn_ref, scw_ref, dq_ref, dk_ref, dv_ref,
             dact_ref, dactn_ref, xbc_ref, xbcp_ref, xbcn_ref, sw_ref, sb_ref, ddt_ref, dz_ref,
             o_ref, dscw_ref, dsw_ref, dsb_ref):
        i = pl.program_id(0)

        @pl.when(i == 0)
        def _():
            dscw_ref[...] = jnp.zeros_like(dscw_ref)
            dsw_ref[...] = jnp.zeros_like(dsw_ref)
            dsb_ref[...] = jnp.zeros_like(dsb_ref)

        first = (i > 0).astype(F32)
        last = (i < n_tiles - 1).astype(F32)
        del dp_ref
        sc = sc_ref[...].astype(F32)
        scp = _prev8(scp_ref) * first
        scn = _next8(scn_ref) * last
        u = sc[:, 256:512] * sc[:, 512:768]
        up = scp[:, 256:512] * scp[:, 512:768]
        dya_v = dya_ref[...]
        cv = _conv_taps(u, up, scw_ref, 3)
        o_ref[:, o_sc:o_sc + 256] = (dya_v * cv).astype(BF16)
        dcv = dya_v * sc[:, 0:256]
        dcvn = _next8(dyan_ref) * last * scn[:, 0:256]
        du = None
        for k in range(3):
            sh = 2 - k
            term = scw_ref[k:k + 1, :] * _shift_up(dcv, dcvn, sh)
            du = term if du is None else du + term
            dscw_ref[k:k + 1, :] += jnp.sum(dcv * _shift_down(u, up, sh), axis=0, keepdims=True)
        o_ref[:, o_sc + 256:o_sc + 512] = (du * sc[:, 512:768]).astype(BF16)
        o_ref[:, o_sc + 512:o_sc + 768] = (du * sc[:, 256:512]).astype(BF16)
        o_ref[:, o_qkv:o_qkv + 256] = (dq_ref[...] * 0.125).astype(BF16)
        o_ref[:, o_qkv + 256:o_qkv + 512] = dk_ref[...].astype(BF16)
        o_ref[:, o_qkv + 512:o_qkv + 768] = dv_ref[...].astype(BF16)
        xb = xbc_ref[...].astype(F32)
        xbp = _prev8(xbcp_ref) * first
        xbn = _next8(xbcn_ref)
        xc = _conv_taps(xb, xbp, sw_ref, 4) + sb_ref[...]
        xcn = _conv_taps(xbn, xb[tr - 8:, :], sw_ref, 4) + sb_ref[...]
        dxc = dact_ref[...] * _dsilu(xc)
        dxcn = _next8(dactn_ref) * _dsilu(xcn) * last
        dxb = None
        for k in range(4):
            sh = 3 - k
            term = sw_ref[k:k + 1, :] * _shift_up(dxc, dxcn, sh)
            dxb = term if dxb is None else dxb + term
            dsw_ref[k:k + 1, :] += jnp.sum(dxc * _shift_down(xb, xbp, sh), axis=0, keepdims=True)
        dsb_ref[...] += jnp.sum(dxc, axis=0, keepdims=True)
        o_ref[:, o_xbc:o_xbc + 768] = dxb.astype(BF16)
        o_ref[:, o_dt:o_dt + 128] = ddt_ref[...].astype(BF16)
        o_ref[:, o_dt + 128:o_z] = jnp.zeros((tr, o_z - o_dt - 128), BF16)
        o_ref[:, o_z:o_end] = dz_ref[...].astype(BF16)

    return rows_call(
        name, body, t, tr,
        [(dp, "any"), (dya, "row"), (dya, ("next8", 256, 0)),
         (p, ("row", 768, sci)), (p, ("prev8", 768, sci)), (p, ("next8", 768, sci)), (sc_w, "full"),
         (dq, "row"), (dk, "row"), (dv, "row"),
         (dact, "row"), (dact, ("next8", 768, 0)),
         (p, ("row", 768, xi)), (p, ("prev8", 768, xi)), (p, ("next8", 768, xi)), (ssm_w, "full"), (ssm_b, "full"),
         (ddt, "row"), (dz, "row")],
        [((t, IN_PAD), BF16, ("row", IN_PAD - base, 1)), ((8, 256), F32, "acc"), ((8, 768), F32, "acc"),
         ((1, 768), F32, "acc")], aliases={0: 0})


def adamw_flat(slots, w, m, v, name, tr=512):
    n_slots, rows, lanes = slots.shape
    tr = max(d for d in range(8, min(tr, rows) + 1, 8) if rows % d == 0) if rows % 8 == 0 else rows
    bc1 = 1.0 - ADAM_B1 ** ADAM_STEP
    bc2 = 1.0 - ADAM_B2 ** ADAM_STEP

    def body(s_ref, w_ref, m_ref, v_ref, g_ref, d_ref, mo_ref, vo_ref):
        g = s_ref[0].astype(F32)
        for k in range(1, n_slots):
            g = g + s_ref[k].astype(F32)
        mn = ADAM_B1 * m_ref[...] + (1.0 - ADAM_B1) * g
        vn = ADAM_B2 * v_ref[...] + (1.0 - ADAM_B2) * (g * g)
        m_hat = mn / bc1
        v_hat = vn / bc2
        g_ref[...] = g
        d_ref[...] = -ADAM_LR * (m_hat / (jnp.sqrt(v_hat) + ADAM_EPS) + ADAM_WD * w_ref[...])
        mo_ref[...] = mn
        vo_ref[...] = vn

    tile = pl.BlockSpec((tr, lanes), lambda i: (i, 0))
    shp = jax.ShapeDtypeStruct((rows, lanes), F32)
    return pl.pallas_call(
        body, name=name, grid=(rows // tr,),
        in_specs=[pl.BlockSpec((n_slots, tr, lanes), lambda i: (0, i, 0)), tile, tile, tile],
        out_specs=[tile] * 4, out_shape=[shp] * 4,
        compiler_params=_cparams(("parallel",)),
    )(slots, w, m, v)


def _split_dot(x, tri):
    hi = x.astype(BF16)
    lo = (x - hi.astype(F32)).astype(BF16)
    return jnp.dot(hi, tri, preferred_element_type=F32) + jnp.dot(lo, tri, preferred_element_type=F32)


_NT = (((1,), (1,)), ((), ()))
_TN = (((0,), (0,)), ((), ()))

SBA_EXP_ZERO = -104.0
SBA_SKIPPED = -1e30


def sba_fwd(qkv, name, bq=256, bk=256):
    t = qkv.shape[0]
    ratio = bq // bk
    assert t // bk <= LANES

    def body(q_ref, k_ref, v_ref, o_ref, runs_ref, acc_s, run_s):
        i = pl.program_id(1)
        lane = lax.broadcasted_iota(jnp.int32, (1, LANES), 1)
        lane_q = lax.broadcasted_iota(jnp.int32, (bq, LANES), 1)
        qi = lax.broadcasted_iota(jnp.int32, (bq, bk), 0) + i * bq
        kj = lax.broadcasted_iota(jnp.int32, (bq, bk), 1)
        later = (lax.broadcasted_iota(jnp.int32, (bk, bk), 0) > lax.broadcasted_iota(jnp.int32, (bk, bk), 1)).astype(BF16)
        qv = q_ref[...]
        qms = [jnp.where(hm, qv, jnp.zeros_like(qv)) for hm in (lane < 64, lane >= 64)]
        acc_s[...] = jnp.zeros_like(acc_s)
        run_s[...] = jnp.zeros_like(run_s)
        runs_ref[...] = jnp.full(runs_ref.shape, SBA_SKIPPED, F32)

        def tile(j, masked):
            start = pl.multiple_of(j * bk, bk)
            kb = k_ref[pl.ds(start, bk), :]
            vb = v_ref[pl.ds(start, bk), :]
            heads = range(2)
            mask = (kj + j * bk) < qi if masked else None
            s = [lax.dot_general(qms[hh], kb, _NT, preferred_element_type=F32) for hh in heads]
            lk = [_log_sigmoid_neg(s[hh]) for hh in heads]
            if masked:
                lk = [jnp.where(mask, lk[hh], 0.0) for hh in heads]
            w = [jnp.dot(lk[hh].astype(BF16), later, preferred_element_type=F32) for hh in heads]
            run = [run_s[hh] for hh in heads]
            a = [jnp.exp(s[hh] + lk[hh] + w[hh] + run[hh]) for hh in heads]
            if masked:
                a = [jnp.where(mask, a[hh], 0.0) for hh in heads]
            top = None
            for hh in heads:
                acc_s[hh] += jnp.dot(a[hh].astype(BF16), vb, preferred_element_type=F32)
                runs_ref[hh] = jnp.where(lane_q == j, run[hh], runs_ref[hh])
                new_run = run[hh] + jnp.sum(lk[hh], axis=1, keepdims=True)
                run_s[hh] = new_run
                top = jnp.max(new_run) if top is None else jnp.maximum(top, jnp.max(new_run))
            return top

        n_kb = (i + 1) * ratio
        top = None
        for d in range(ratio):
            top = tile(n_kb - 1 - d, True)

        def cond(state):
            n, live = state
            return jnp.logical_and(n < i * ratio, live)

        def step(state):
            n, _ = state
            return n + 1, tile(i * ratio - 1 - n, False) >= SBA_EXP_ZERO

        lax.while_loop(cond, step, (jnp.int32(0), top >= SBA_EXP_ZERO))
        o_ref[...] = jnp.where(lane < 64, acc_s[0], acc_s[1])

    return pl.pallas_call(
        body, name=name, grid=(2, t // bq),
        in_specs=[pl.BlockSpec((bq, LANES), lambda p, i: (i, p)), pl.BlockSpec((t, LANES), lambda p, i: (0, 2 + p)),
                  pl.BlockSpec((t, LANES), lambda p, i: (0, 4 + p))],
        out_specs=[pl.BlockSpec((bq, LANES), lambda p, i: (i, p)), pl.BlockSpec((2, bq, LANES), lambda p, i: (p, i, 0))],
        out_shape=[jax.ShapeDtypeStruct((t, SB_WIDTH), F32), jax.ShapeDtypeStruct((4, t, LANES), F32)],
        scratch_shapes=[pltpu.VMEM((2, bq, LANES), F32), pltpu.VMEM((2, bq, 1), F32)],
        compiler_params=_cparams(("parallel", "parallel")),
    )(qkv, qkv, qkv)


def sba_bwd(qkv, runs, do, name, bq=256, bk=256):
    t = qkv.shape[0]
    ratio = bq // bk
    nq = t // bq

    def body(q_ref, k_ref, v_ref, runs_ref, do_ref, dq_ref, dk_hbm, dv_hbm, dk_s, dv_s, sem, dq_s, rg_s):
        p = pl.program_id(0)
        i = pl.program_id(1)

        @pl.when(i == 0)
        def _():
            dk_s[...] = jnp.zeros_like(dk_s)
            dv_s[...] = jnp.zeros_like(dv_s)

        lane = lax.broadcasted_iota(jnp.int32, (1, LANES), 1)
        qi = lax.broadcasted_iota(jnp.int32, (bq, bk), 0) + i * bq
        kj = lax.broadcasted_iota(jnp.int32, (bq, bk), 1)
        r2 = lax.broadcasted_iota(jnp.int32, (bk, bk), 0)
        c2 = lax.broadcasted_iota(jnp.int32, (bk, bk), 1)
        later = (r2 > c2).astype(BF16)
        earlier = (r2 < c2).astype(BF16)
        qv = q_ref[...]
        dov = do_ref[...]
        heads = range(2)
        hms = (lane < 64, lane >= 64)
        qms = [jnp.where(hm, qv, jnp.zeros_like(qv)) for hm in hms]
        doms = [jnp.where(hm, dov, 0.0).astype(BF16) for hm in hms]
        runs = [runs_ref[hh] for hh in heads]
        dq_s[...] = jnp.zeros_like(dq_s)
        rg_s[...] = jnp.zeros_like(rg_s)

        def tile(j, masked):
            start = pl.multiple_of(j * bk, bk)
            kb = k_ref[pl.ds(start, bk), :]
            vb = v_ref[pl.ds(start, bk), :]
            mask = (kj + j * bk) < qi if masked else None
            s = [lax.dot_general(qms[hh], kb, _NT, preferred_element_type=F32) for hh in heads]
            da = [lax.dot_general(doms[hh], vb, _NT, preferred_element_type=F32) for hh in heads]
            lk_raw = [_log_sigmoid_neg(s[hh]) for hh in heads]
            lk = [jnp.where(mask, lk_raw[hh], 0.0) for hh in heads] if masked else lk_raw
            w = [jnp.dot(lk[hh].astype(BF16), later, preferred_element_type=F32) for hh in heads]
            run = [jnp.sum(jnp.where(lane == j, runs[hh], 0.0), axis=1, keepdims=True) for hh in heads]
            a = [jnp.exp(s[hh] + lk[hh] + w[hh] + run[hh]) for hh in heads]
            if masked:
                a = [jnp.where(mask, a[hh], 0.0) for hh in heads]
            g = [a[hh] * da[hh] for hh in heads]
            rg = [rg_s[hh] for hh in heads]
            c = [rg[hh] + _split_dot(g[hh], earlier) for hh in heads]
            dz = [g[hh] - jnp.exp(s[hh] + lk_raw[hh]) * (g[hh] + c[hh]) for hh in heads]
            if masked:
                dz = [jnp.where(mask, dz[hh], 0.0) for hh in heads]
            dz = [dz[hh].astype(BF16) for hh in heads]
            for hh in heads:
                dq_s[hh] += jnp.dot(dz[hh], kb, preferred_element_type=F32)
                rg_s[hh] = rg[hh] + jnp.sum(g[hh], axis=1, keepdims=True)
            dk = lax.dot_general(dz[0], qms[0], _TN, preferred_element_type=F32)
            dk_s[pl.ds(start, bk), :] += dk + lax.dot_general(dz[1], qms[1], _TN, preferred_element_type=F32)
            dv = lax.dot_general(a[0].astype(BF16), doms[0], _TN, preferred_element_type=F32)
            dv_s[pl.ds(start, bk), :] += dv + lax.dot_general(a[1].astype(BF16), doms[1], _TN, preferred_element_type=F32)

        live = jnp.maximum(jnp.max(runs[0], axis=0, keepdims=True), jnp.max(runs[1], axis=0, keepdims=True)) >= SBA_EXP_ZERO
        first = jnp.minimum(jnp.min(jnp.where(live, lane, LANES)), i * ratio)

        def step(j, carry):
            tile(j, False)
            return carry

        lax.fori_loop(first, i * ratio, step, 0)
        for d in range(ratio):
            tile(i * ratio + d, True)
        dq_ref[...] = jnp.where(lane < 64, dq_s[0], dq_s[1])

        @pl.when(i == nq - 1)
        def _():
            col = pl.multiple_of(p * LANES, LANES)
            ck = pltpu.make_async_copy(dk_s, dk_hbm.at[:, pl.ds(col, LANES)], sem.at[0])
            cv = pltpu.make_async_copy(dv_s, dv_hbm.at[:, pl.ds(col, LANES)], sem.at[1])
            ck.start()
            cv.start()
            ck.wait()
            cv.wait()

    shp = jax.ShapeDtypeStruct((t, SB_WIDTH), F32)
    tile = pl.BlockSpec((bq, LANES), lambda p, i: (i, p))
    return pl.pallas_call(
        body, name=name, grid=(2, nq),
        in_specs=[tile, pl.BlockSpec((t, LANES), lambda p, i: (0, 2 + p)), pl.BlockSpec((t, LANES), lambda p, i: (0, 4 + p)),
                  pl.BlockSpec((2, bq, LANES), lambda p, i: (p, i, 0)), tile],
        out_specs=[tile, pl.BlockSpec(memory_space=pl.ANY), pl.BlockSpec(memory_space=pl.ANY)],
        out_shape=[shp, shp, shp],
        scratch_shapes=[pltpu.VMEM((t, LANES), F32), pltpu.VMEM((t, LANES), F32), pltpu.SemaphoreType.DMA((2,)),
                        pltpu.VMEM((2, bq, LANES), F32), pltpu.VMEM((2, bq, 1), F32)],
        compiler_params=_cparams(("arbitrary", "arbitrary")),
    )(qkv, qkv, qkv, runs, do)


def _ssd_consts():
    ln = SSM_CHUNK
    ri = lax.broadcasted_iota(jnp.int32, (ln, ln), 0)
    ci = lax.broadcasted_iota(jnp.int32, (ln, ln), 1)
    eh = lax.broadcasted_iota(jnp.int32, (LANES, SSM_INNER), 0)
    el = lax.broadcasted_iota(jnp.int32, (LANES, SSM_INNER), 1)
    expand = (jnp.right_shift(el, 6) == eh).astype(BF16)
    th = lax.broadcasted_iota(jnp.int32, (SSM_INNER, LANES), 1)
    tl = lax.broadcasted_iota(jnp.int32, (SSM_INNER, LANES), 0)
    reduce = (jnp.right_shift(tl, 6) == th).astype(BF16)
    return ri, ci, expand, reduce


def _dot_f32(a, b):
    return jnp.dot(a, b, precision=HI, preferred_element_type=F32)


def _split3(x):
    hi = x.astype(BF16)
    r1 = x - hi.astype(F32)
    mid = r1.astype(BF16)
    lo = (r1 - mid.astype(F32)).astype(BF16)
    return hi, mid, lo


def _dot_hi(a, b):
    if a.dtype == BF16:
        return sum(jnp.dot(a, t, preferred_element_type=F32) for t in _split3(b))
    return sum(jnp.dot(t, b, preferred_element_type=F32) for t in _split3(a))


def _ssd_prelude(xbc_ref, dt_ref, dtt_ref, hpr_ref, hpc_ref, ri, ci, expand):
    ln = SSM_CHUNK
    xs = xbc_ref[:, 0:512]
    bm = xbc_ref[:, 512:640]
    cm = xbc_ref[:, 640:768]
    dtb_r = hpr_ref[0:1, :]
    aneg_r = -jnp.exp(hpr_ref[1:2, :])
    pre = dt_ref[...] + dtb_r
    dt = _softplus(pre)
    a = dt * aneg_r
    dtt = _softplus(dtt_ref[...] + hpc_ref[0:8, :])
    att = dtt * (-jnp.exp(hpc_ref[8:16, :]))
    tril = (ri >= ci).astype(BF16)
    triu = (ri <= ci).astype(BF16)
    acs = _dot_hi(tril, a)
    acst = _dot_hi(att, triu)
    acs_e = _dot_hi(acs, expand)
    dt_e = _dot_hi(dt, expand)
    last_e = acs_e[ln - 1:ln, :]
    e_e = jnp.exp(acs_e)
    w_e = jnp.exp(last_e - acs_e)
    dec_e = jnp.exp(last_e)
    xdt = xs * dt_e
    return dict(xs=xs, bm=bm, cm=cm, pre=pre, dt=dt, aneg_r=aneg_r, acs=acs, acst=acst, dt_e=dt_e, e_e=e_e,
                w_e=w_e, dec_e=dec_e, xdt=xdt, triu=triu)


def ssd_fwd(act, p, dt32, dtt, hp_rows, hp_cols, d_e, norm_w, name):
    t = act.shape[0]
    ln = SSM_CHUNK
    nc = t // ln

    def body(xbc_ref, dt_ref, z_ref, dtt_ref, hpr_ref, hpc_ref, d_ref, nw_ref, yc_ref, y_ref, sto_ref, st):
        @pl.when(pl.program_id(0) == 0)
        def _():
            st[...] = jnp.zeros_like(st)

        ri, ci, expand, _ = _ssd_consts()
        q = _ssd_prelude(xbc_ref, dt_ref, dtt_ref, hpr_ref, hpc_ref, ri, ci, expand)
        lane = lax.broadcasted_iota(jnp.int32, (1, LANES), 1)
        rown = lax.broadcasted_iota(jnp.int32, (LANES, 1), 0)
        low = lane < 64
        mask = ri >= ci
        xdt_b = q["xdt"].astype(BF16)
        xw_b = (q["xdt"] * q["w_e"]).astype(BF16)
        bt = q["bm"].T.astype(BF16)
        cb_ = q["cm"].astype(BF16)
        y_pairs = []
        for g in range(2):
            gm = low if g == 0 else jnp.logical_not(low)
            rm = (rown < 64) if g == 0 else (rown >= 64)
            cg = jnp.where(gm, cb_, jnp.zeros_like(cb_))
            cb = jnp.dot(cg, bt, preferred_element_type=F32)
            for pp in range(2):
                pi = 2 * g + pp
                sl = slice(LANES * pi, LANES * (pi + 1))
                xp = xdt_b[:, sl]
                yd = []
                for hh in range(2):
                    h = 2 * pi + hh
                    diff = q["acs"][:, h:h + 1] - q["acst"][h:h + 1, :]
                    lam = jnp.exp(jnp.where(mask, diff, -jnp.inf))
                    yd.append(jnp.dot((cb * lam).astype(BF16), xp, preferred_element_type=F32))
                sp = st[pi]
                sto_ref[0, pi] = sp
                yoff = jnp.dot(cg, sp.astype(BF16), preferred_element_type=F32) * q["e_e"][:, sl]
                upd = jnp.dot(bt, xw_b[:, sl], preferred_element_type=F32)
                st[pi] = q["dec_e"][:, sl] * sp + jnp.where(rm, upd, 0.0)
                y_pairs.append(jnp.where(low, yd[0], yd[1]) + yoff)
        y = jnp.concatenate(y_pairs, axis=1) + q["xs"] * d_ref[...]
        y_ref[...] = y
        yg = y * _silu(z_ref[...].astype(F32))
        for g in range(2):
            sl = slice(256 * g, 256 * (g + 1))
            seg = yg[:, sl]
            yc_ref[:, sl] = (seg * _rstd(seg) * nw_ref[:, sl]).astype(BF16)

    return pl.pallas_call(
        body, name=name, grid=(nc,),
        in_specs=[pl.BlockSpec((ln, 768), lambda c: (c, 0)), pl.BlockSpec((ln, LANES), lambda c: (c, 0)),
                  pl.BlockSpec((ln, 512), lambda c: (c, OFF_Z // 512)), pl.BlockSpec((8, ln), lambda c: (0, c)),
                  pl.BlockSpec((8, LANES), lambda c: (0, 0)), pl.BlockSpec((16, ln), lambda c: (0, 0)),
                  pl.BlockSpec((1, 512), lambda c: (0, 0)), pl.BlockSpec((1, 512), lambda c: (0, 0))],
        out_specs=[pl.BlockSpec((ln, 512), lambda c: (c, 0)), pl.BlockSpec((ln, 512), lambda c: (c, 0)),
                   pl.BlockSpec((1, 4, LANES, LANES), lambda c: (c, 0, 0, 0))],
        out_shape=[jax.ShapeDtypeStruct((t, 512), BF16), jax.ShapeDtypeStruct((t, 512), F32),
                   jax.ShapeDtypeStruct((nc, 4, LANES, LANES), F32)],
        scratch_shapes=[pltpu.VMEM((4, LANES, LANES), F32)],
        compiler_params=_cparams(("arbitrary",)),
    )(act, dt32, p, dtt, hp_rows, hp_cols, d_e, norm_w)


def ssd_bwd(dyc, y, act, p, dt32, dtt, states, hp_rows, hp_cols, d_e, norm_w, name):
    t = act.shape[0]
    ln = SSM_CHUNK
    nc = t // ln

    def body(dyc_ref, y_ref, xbc_ref, dt_ref, z_ref, dtt_ref, st_ref, hpr_ref, hpc_ref, d_ref, nw_ref,
             dz_ref, dact_ref, ddt_ref, dnw_ref, dd_ref, dhp_ref, ds):
        @pl.when(pl.program_id(0) == 0)
        def _():
            ds[...] = jnp.zeros_like(ds)
            dnw_ref[...] = jnp.zeros_like(dnw_ref)
            dd_ref[...] = jnp.zeros_like(dd_ref)
            dhp_ref[...] = jnp.zeros_like(dhp_ref)

        ri, ci, expand, reduce = _ssd_consts()
        q = _ssd_prelude(xbc_ref, dt_ref, dtt_ref, hpr_ref, hpc_ref, ri, ci, expand)
        lane = lax.broadcasted_iota(jnp.int32, (1, LANES), 1)
        rown = lax.broadcasted_iota(jnp.int32, (LANES, 1), 0)
        low = lane < 64
        mask = ri >= ci
        mask_t = ci >= ri
        xs, xdt, acs, acst = q["xs"], q["xdt"], q["acs"], q["acst"]
        yv, zv, nw = y_ref[...], z_ref[...].astype(F32), nw_ref[...]
        sg = _sigmoid(zv)
        zz = zv * sg
        yg = yv * zz
        dycv = dyc_ref[...]
        u = dycv * nw
        dyg_parts, dnw_parts = [], []
        for g in range(2):
            sl = slice(256 * g, 256 * (g + 1))
            seg = yg[:, sl]
            rr = _rstd(seg)
            nrm = seg * rr
            dyg_parts.append(rr * (u[:, sl] - nrm * jnp.mean(nrm * u[:, sl], axis=-1, keepdims=True)))
            dnw_parts.append(jnp.sum(dycv[:, sl] * nrm, axis=0, keepdims=True))
        dyg = jnp.concatenate(dyg_parts, axis=1)
        dnw_ref[...] += jnp.concatenate(dnw_parts, axis=1)
        dy = dyg * zz
        dz_ref[...] = dyg * yv * (sg * (1.0 + zv * (1.0 - sg)))
        dd_ref[...] += jnp.sum(dy * xs, axis=0, keepdims=True)
        dxs = dy * d_ref[...]
        dy_b = dy.astype(BF16)
        xdt_b = xdt.astype(BF16)
        xw_b = (xdt * q["w_e"]).astype(BF16)
        bt = q["bm"].T.astype(BF16)
        ct = q["cm"].T.astype(BF16)
        cb_ = q["cm"].astype(BF16)
        bb_ = q["bm"].astype(BF16)
        dacs = jnp.zeros((ln, LANES), F32)
        dc = jnp.zeros((ln, LANES), F32)
        db = jnp.zeros((ln, LANES), F32)
        dxdt_pairs, yoffdy_pairs, dwe_pairs, ddec_pairs = [], [], [], []
        for g in range(2):
            gm = low if g == 0 else jnp.logical_not(low)
            rm = (rown < 64) if g == 0 else (rown >= 64)
            cg = jnp.where(gm, cb_, jnp.zeros_like(cb_))
            bg = jnp.where(gm, bb_, jnp.zeros_like(bb_))
            cb = jnp.dot(cg, bt, preferred_element_type=F32)
            cbt = jnp.dot(bg, ct, preferred_element_type=F32)
            dcb = jnp.zeros((ln, ln), F32)
            dcbt = jnp.zeros((ln, ln), F32)
            for pp in range(2):
                pi = 2 * g + pp
                sl = slice(LANES * pi, LANES * (pi + 1))
                xp = xdt_b[:, sl]
                dyp = dy_b[:, sl]
                xpt = xdt[:, sl].T.astype(BF16)
                dypt = dy[:, sl].T.astype(BF16)
                dxdt_p = jnp.zeros((ln, LANES), F32)
                for hh in range(2):
                    h = 2 * pi + hh
                    hm = low if hh == 0 else jnp.logical_not(low)
                    col = acs[:, h:h + 1]
                    row = acst[h:h + 1, :]
                    lam = jnp.exp(jnp.where(mask, col - row, -jnp.inf))
                    lam_t = jnp.exp(jnp.where(mask_t, row - col, -jnp.inf))
                    m = cb * lam
                    m_t = cbt * lam_t
                    dyh = jnp.where(hm, dyp, jnp.zeros_like(dyp))
                    xh = jnp.where(hm, xp, jnp.zeros_like(xp))
                    dm = jnp.dot(dyh, xpt, preferred_element_type=F32)
                    dm_t = jnp.dot(xh, dypt, preferred_element_type=F32)
                    dcb = dcb + dm * lam
                    dcbt = dcbt + dm_t * lam_t
                    rs = jnp.sum(dm * m, axis=1, keepdims=True) - jnp.sum(dm_t * m_t, axis=1, keepdims=True)
                    dacs = dacs + jnp.where(lane == h, rs, 0.0)
                    dxdt_p = dxdt_p + jnp.dot(m_t.astype(BF16), dyh, preferred_element_type=F32)
                sp = st_ref[0, pi]
                sp_b = sp.astype(BF16)
                dsn = ds[pi]
                dsn_b = dsn.astype(BF16)
                e_p, w_p, dec_p = q["e_e"][:, sl], q["w_e"][:, sl], q["dec_e"][:, sl]
                yoff = jnp.dot(cg, sp_b, preferred_element_type=F32) * e_p
                dyo = dy[:, sl] * e_p
                dyo_b = dyo.astype(BF16)
                dc = dc + lax.dot_general(dyo_b, sp_b, _NT, preferred_element_type=F32)
                ds_prev = dec_p * dsn + jnp.where(rm, jnp.dot(ct, dyo_b, preferred_element_type=F32), 0.0)
                yoffdy_pairs.append(dy[:, sl] * yoff)
                dxw = jnp.dot(bg, dsn_b, preferred_element_type=F32)
                db = db + lax.dot_general(xw_b[:, sl], dsn_b, _NT, preferred_element_type=F32)
                dxdt_p = dxdt_p + dxw * w_p
                dwe_pairs.append(dxw * xdt[:, sl])
                ddec_pairs.append(jnp.sum(dsn * sp, axis=0, keepdims=True))
                ds[pi] = ds_prev
                dxdt_pairs.append(dxdt_p)
            dc = dc + jnp.dot(dcb.astype(BF16), bg, preferred_element_type=F32)
            db = db + jnp.dot(dcbt.astype(BF16), cg, preferred_element_type=F32)
        dxdt = jnp.concatenate(dxdt_pairs, axis=1)
        yoffdy = jnp.concatenate(yoffdy_pairs, axis=1)
        dwe = jnp.concatenate(dwe_pairs, axis=1)
        ddec_e = jnp.broadcast_to(jnp.concatenate(ddec_pairs, axis=1), (8, SSM_INNER))
        last = acs[ln - 1:ln, :]
        w_col = jnp.exp(last - acs)
        dw_col = _dot_hi(dwe, reduce) * w_col
        dacs = dacs + _dot_hi(yoffdy, reduce) - dw_col
        dlast = jnp.sum(dw_col, axis=0, keepdims=True) + jnp.exp(last) * _dot_hi(ddec_e, reduce)[0:1, :]
        rowi = lax.broadcasted_iota(jnp.int32, (ln, 1), 0)
        dacs = dacs + jnp.where(rowi == ln - 1, dlast, 0.0)
        da = _dot_hi(q["triu"], dacs)
        ddt = da * q["aneg_r"] + _dot_hi(dxdt * xs, reduce)
        ddt_raw = jnp.where(lane < SSM_HEADS, ddt * _sigmoid(q["pre"]), 0.0)
        ddt_ref[...] = ddt_raw
        dhp_ref[0:1, :] += jnp.sum(ddt_raw, axis=0, keepdims=True)
        dhp_ref[1:2, :] += jnp.where(lane < SSM_HEADS, jnp.sum(da * q["dt"], axis=0, keepdims=True) * q["aneg_r"], 0.0)
        dact_ref[:, 0:512] = dxs + dxdt * q["dt_e"]
        dact_ref[:, 512:640] = db
        dact_ref[:, 640:768] = dc

    rev = lambda c: nc - 1 - c
    return pl.pallas_call(
        body, name=name, grid=(nc,),
        in_specs=[pl.BlockSpec((ln, 512), lambda c: (rev(c), 0)), pl.BlockSpec((ln, 512), lambda c: (rev(c), 0)),
                  pl.BlockSpec((ln, 768), lambda c: (rev(c), 0)),
                  pl.BlockSpec((ln, LANES), lambda c: (rev(c), 0)),
                  pl.BlockSpec((ln, 512), lambda c: (rev(c), OFF_Z // 512)), pl.BlockSpec((8, ln), lambda c: (0, rev(c))),
                  pl.BlockSpec((1, 4, LANES, LANES), lambda c: (rev(c), 0, 0, 0)),
                  pl.BlockSpec((8, LANES), lambda c: (0, 0)), pl.BlockSpec((16, ln), lambda c: (0, 0)),
                  pl.BlockSpec((1, 512), lambda c: (0, 0)), pl.BlockSpec((1, 512), lambda c: (0, 0))],
        out_specs=[pl.BlockSpec((ln, 512), lambda c: (rev(c), 0)), pl.BlockSpec((ln, 768), lambda c: (rev(c), 0)),
                   pl.BlockSpec((ln, LANES), lambda c: (rev(c), 0)), pl.BlockSpec((1, 512), lambda c: (0, 0)),
                   pl.BlockSpec((1, 512), lambda c: (0, 0)), pl.BlockSpec((8, LANES), lambda c: (0, 0))],
        out_shape=[jax.ShapeDtypeStruct((t, 512), F32), jax.ShapeDtypeStruct((t, 768), F32),
                   jax.ShapeDtypeStruct((t, LANES), F32), jax.ShapeDtypeStruct((1, 512), F32),
                   jax.ShapeDtypeStruct((1, 512), F32), jax.ShapeDtypeStruct((8, LANES), F32)],
        scratch_shapes=[pltpu.VMEM((4, LANES, LANES), F32)],
        compiler_params=_cparams(("arbitrary",)),
    )(dyc, y, act, dt32, p, dtt, states, hp_rows, hp_cols, d_e, norm_w)


def mod_shard_fwd(c_all, mod_w, mod_b_shard, name):
    def body(c_ref, w_ref, b_ref, o_ref):
        sc = _silu(c_ref[...])
        for l in range(DEPTH):
            o_ref[l] = _dot_f32(sc, w_ref[l]) + b_ref[l]

    return pl.pallas_call(body, name=name, out_shape=jax.ShapeDtypeStruct((DEPTH, N_DEV, mod_w.shape[2]), F32),
                          compiler_params=_cparams())(c_all, mod_w, mod_b_shard)


def mod_w_grad(c_all, dmod_shard, name):
    def body(c_ref, d_ref, o_ref):
        sc = _silu(c_ref[...])
        for l in range(DEPTH):
            o_ref[l] = lax.dot_general(sc, d_ref[l], _TN, precision=HI, preferred_element_type=F32)

    return pl.pallas_call(body, name=name, out_shape=jax.ShapeDtypeStruct((DEPTH, D_MODEL, dmod_shard.shape[2]), F32),
                          compiler_params=_cparams())(c_all, dmod_shard)


_BIG = ("w_in", "sc_conv_w", "ssm_conv_w", "w_sc_out", "w_sb_out", "w_ssm_out", "w_o", "w_ffn_in", "w_ffn_out")
_ROW_SHARDED = ("w_o", "w_ffn_out")
_CONV = ("sc_conv_w", "ssm_conv_w")
_SMALL = ("mod_b", "g_pre_mix", "g_post_mix", "g_pre_ffn", "g_post_ffn", "ssm_conv_b", "ssm_dt_bias", "ssm_a_log",
          "ssm_d", "ssm_norm_w")
_WEIGHTS = ("mod_w", "mod_b", "g_pre_mix", "g_post_mix", "g_pre_ffn", "g_post_ffn", "w_in", "sc_conv_w", "ssm_conv_w",
            "ssm_conv_b", "ssm_dt_bias", "ssm_a_log", "ssm_d", "ssm_norm_w", "w_sc_out", "w_sb_out", "w_ssm_out", "w_o",
            "w_ffn_in", "w_ffn_out")


def _gathered_to_full(g, row_sharded):
    _, dep, r, c = g.shape
    if row_sharded:
        return g.transpose(1, 0, 2, 3).reshape(dep, N_DEV * r, c)
    return g.transpose(1, 2, 0, 3).reshape(dep, r, N_DEV * c)


def _full_to_slots(w, row_sharded):
    dep, r, c = w.shape
    if row_sharded:
        return w.reshape(dep, N_DEV, r // N_DEV, c).transpose(1, 0, 2, 3).reshape(N_DEV, dep * (r // N_DEV), c)
    return w.reshape(dep, r, N_DEV, c // N_DEV).transpose(2, 0, 1, 3).reshape(N_DEV, dep * r, c // N_DEV)


def _pad_in_proj(w):
    sc, qkv, z, xbc, dt, gates = (w[:, 0:768], w[:, 768:1536], w[:, 1536:2048], w[:, 2048:2816], w[:, 2816:2824],
                                  w[:, 2824:5896])
    pad = jnp.zeros((w.shape[0], OFF_Z - OFF_DT - 8), w.dtype)
    return jnp.concatenate([gates, sc, qkv, xbc, dt, pad, z], axis=1)


def _unpad_in_proj(w):
    return jnp.concatenate([w[:, OFF_SC:OFF_SC + 768], w[:, OFF_QKV:OFF_QKV + 768], w[:, OFF_Z:OFF_Z + 512],
                            w[:, OFF_XBC:OFF_XBC + 768], w[:, OFF_DT:OFF_DT + 8], w[:, 0:3072]], axis=1)


def _row(v):
    return v.reshape(1, -1)


def _local_step(x, target, mod, small, conv, big):
    lw, saved = [], []
    for l in range(DEPTH):
        w_in_p = _pad_in_proj(big["w_in"][l])
        w_cat = jnp.concatenate([big["w_sc_out"][l], big["w_sb_out"][l], big["w_ssm_out"][l]], axis=0)
        hp_rows = jnp.zeros((8, LANES), F32).at[0, :SSM_HEADS].set(small["ssm_dt_bias"][l]).at[1, :SSM_HEADS].set(
            small["ssm_a_log"][l])
        hp_cols = jnp.concatenate([jnp.broadcast_to(small["ssm_dt_bias"][l][:, None], (SSM_HEADS, SSM_CHUNK)),
                                   jnp.broadcast_to(small["ssm_a_log"][l][:, None], (SSM_HEADS, SSM_CHUNK))], axis=0)
        lw.append(dict(
            w_in_p=w_in_p, w_cat=w_cat, w_o=big["w_o"][l], w_ffn_in=big["w_ffn_in"][l], w_ffn_out=big["w_ffn_out"][l],
            sc_w8=jnp.pad(conv["sc_conv_w"][l], ((0, 5), (0, 0))), ssm_w8=jnp.pad(conv["ssm_conv_w"][l], ((0, 4), (0, 0))),
            ssm_b=_row(small["ssm_conv_b"][l]), hp_rows=hp_rows, hp_cols=hp_cols,
            d_e=_row(jnp.repeat(small["ssm_d"][l], SSM_HEAD_DIM)), norm_w=_row(small["ssm_norm_w"][l]),
            g_pre_mix=_row(small["g_pre_mix"][l]), g_post_mix=_row(small["g_post_mix"][l]),
            g_pre_ffn=_row(small["g_pre_ffn"][l]), g_post_ffn=_row(small["g_post_ffn"][l]),
            shift1=mod[l, 0:1], scale1=mod[l, 1:2], gate1=mod[l, 2:3], shift2=mod[l, 3:4], scale2=mod[l, 4:5],
            gate2=mod[l, 5:6]))

    xl = x
    h = normmod_fwd(xl, lw[0]["g_pre_mix"], lw[0]["scale1"], lw[0]["shift1"], "normmod_fwd_0")
    dy = loss = None
    for l in range(DEPTH):
        w = lw[l]
        p = mm_nn([h], [w["w_in_p"]], BF16, f"in_proj_{l}")
        dt32 = mm_nn([h], [w["w_in_p"][:, OFF_DT:OFF_DT + LANES]], F32, f"in_proj_dt_{l}")
        ya, qkv, act = post_inproj(p, w["sc_w8"], w["ssm_w8"], w["ssm_b"], f"post_inproj_{l}")
        o, runs = sba_fwd(qkv, f"sba_fwd_{l}")
        dtt = dt32[:, :SSM_HEADS].T
        yc, ypre, states = ssd_fwd(act, p, dt32, dtt, w["hp_rows"], w["hp_cols"], w["d_e"], w["norm_w"], f"ssd_fwd_{l}")
        merged = branch_out_fwd(ya, o, yc, p, w["w_cat"], f"branch_fwd_{l}")
        mix = mm_nn([merged], [w["w_o"]], F32, f"out_proj_{l}")
        x1, h2 = resid_normmod_fwd(xl, mix, w["gate1"], w["g_post_mix"], w["g_pre_ffn"], w["scale2"], w["shift2"],
                                   f"resid_mix_{l}")
        gt, up, a = mm_swiglu_fwd(h2, w["w_ffn_in"], f"ffn_in_{l}")
        f = mm_nn([a], [w["w_ffn_out"]], F32, f"ffn_out_{l}")
        saved.append(dict(x=xl, h=h, p=p, ya=ya, qkv=qkv, act=act, o=o, runs=runs, dt32=dt32, dtt=dtt, yc=yc, ypre=ypre, states=states,
                          merged=merged, mix=mix, x1=x1, h2=h2, gt=gt, up=up, a=a, f=f))
        if l + 1 < DEPTH:
            nw = lw[l + 1]
            xl, h = resid_normmod_fwd(x1, f, w["gate2"], w["g_post_ffn"], nw["g_pre_mix"], nw["scale1"], nw["shift1"],
                                      f"resid_ffn_{l}")
        else:
            dy, loss = resid_loss(x1, f, w["gate2"], w["g_post_ffn"], target, "resid_loss")

    dmod = [None] * DEPTH
    gs = {k: [None] * DEPTH for k in _SMALL + _BIG}
    dxo = dy
    for l in reversed(range(DEPTH)):
        w, s = lw[l], saved[l]
        df, dgate2, gs["g_post_ffn"][l] = resid_bwd(dxo, s["f"], w["gate2"], w["g_post_ffn"], f"resid_ffn_bwd_{l}")
        dgt, dup = mm_swiglu_bwd(df, w["w_ffn_out"], s["gt"], s["up"], f"ffn_out_bwd_{l}")
        gs["w_ffn_out"][l] = mm_tn(s["a"], df, f"dw_ffn_out_{l}")
        dh2 = mm_nt([dgt, dup], [w["w_ffn_in"], w["w_ffn_in"]], [0, FFN_HIDDEN], F32, f"ffn_in_bwd_{l}")
        gs["w_ffn_in"][l] = jnp.concatenate([mm_tn(s["h2"], dgt, f"dw_ffn_gate_{l}"), mm_tn(s["h2"], dup, f"dw_ffn_up_{l}")],
                                            axis=1)
        dx1, dscale2, dshift2, gs["g_pre_ffn"][l] = normmod_bwd(dh2, s["x1"], dxo, w["g_pre_ffn"], w["scale2"],
                                                                f"normmod_ffn_bwd_{l}")
        dmix, dgate1, gs["g_post_mix"][l] = resid_bwd(dx1, s["mix"], w["gate1"], w["g_post_mix"], f"resid_mix_bwd_{l}")
        dmerged = mm_nt([dmix], [w["w_o"]], [0], F32, f"out_proj_bwd_{l}")
        gs["w_o"][l] = mm_tn(s["merged"], dmix, f"dw_o_{l}")
        dp_gates, dya, dyb, dyc, dw_cat = branch_out_bwd(dmerged, s["ya"], s["o"], s["yc"], s["p"], w["w_cat"],
                                                         f"branch_bwd_{l}")
        gs["w_sc_out"][l], gs["w_sb_out"][l], gs["w_ssm_out"][l] = dw_cat[0:256], dw_cat[256:512], dw_cat[512:1024]
        dz, dact, ddt, dnw, dd_e, dhp = ssd_bwd(dyc, s["ypre"], s["act"], s["p"], s["dt32"], s["dtt"], s["states"], w["hp_rows"],
                                                w["hp_cols"], w["d_e"], w["norm_w"], f"ssd_bwd_{l}")
        gs["ssm_norm_w"][l] = dnw[0]
        gs["ssm_d"][l] = dd_e.reshape(SSM_HEADS, SSM_HEAD_DIM).sum(axis=1)
        gs["ssm_dt_bias"][l] = dhp[0, :SSM_HEADS]
        gs["ssm_a_log"][l] = dhp[1, :SSM_HEADS]
        dq, dk, dv = sba_bwd(s["qkv"], s["runs"], dyb, f"sba_bwd_{l}")
        dp, dscw, dssw, dssb = assemble_dp(dp_gates, dya, s["p"], w["sc_w8"], dq, dk, dv, dact, w["ssm_w8"], w["ssm_b"], ddt,
                                           dz, f"assemble_dp_{l}")
        gs["sc_conv_w"][l], gs["ssm_conv_w"][l], gs["ssm_conv_b"][l] = dscw[0:3], dssw[0:4], dssb[0]
        dh = mm_nt([dp], [w["w_in_p"]], [0], F32, f"in_proj_bwd_{l}")
        gs["w_in"][l] = _unpad_in_proj(mm_tn(s["h"], dp, f"dw_in_{l}"))
        dxo, dscale1, dshift1, gs["g_pre_mix"][l] = normmod_bwd(dh, s["x"], dx1, w["g_pre_mix"], w["scale1"],
                                                                f"normmod_mix_bwd_{l}")
        dmod[l] = jnp.concatenate([dshift1, dscale1, dgate1, dshift2, dscale2, dgate2], axis=0)
    for k in ("g_pre_mix", "g_post_mix", "g_pre_ffn", "g_post_ffn"):
        gs[k] = [g[0] for g in gs[k]]
    grads = {k: jnp.stack(v) for k, v in gs.items() if k != "mod_b"}
    return loss[0, 0], dxo, jnp.stack(dmod), grads


def kernel(x, c, mod_w, mod_b, g_pre_mix, g_post_mix, g_pre_ffn, g_post_ffn, w_in, sc_conv_w, ssm_conv_w, ssm_conv_b, ssm_dt_bias, ssm_a_log, ssm_d, ssm_norm_w, w_sc_out, w_sb_out, w_ssm_out, w_o, w_ffn_in, w_ffn_out, loss_target, m_mod_w, m_mod_b, m_g_pre_mix, m_g_post_mix, m_g_pre_ffn, m_g_post_ffn, m_w_in, m_sc_conv_w, m_ssm_conv_w, m_ssm_conv_b, m_ssm_dt_bias, m_ssm_a_log, m_ssm_d, m_ssm_norm_w, m_w_sc_out, m_w_sb_out, m_w_ssm_out, m_w_o, m_w_ffn_in, m_w_ffn_out, v_mod_w, v_mod_b, v_g_pre_mix, v_g_post_mix, v_g_pre_ffn, v_g_post_ffn, v_w_in, v_sc_conv_w, v_ssm_conv_w, v_ssm_conv_b, v_ssm_dt_bias, v_ssm_a_log, v_ssm_d, v_ssm_norm_w, v_w_sc_out, v_w_sb_out, v_w_ssm_out, v_w_o, v_w_ffn_in, v_w_ffn_out):
    wts = dict(mod_w=mod_w, mod_b=mod_b, g_pre_mix=g_pre_mix, g_post_mix=g_post_mix, g_pre_ffn=g_pre_ffn,
               g_post_ffn=g_post_ffn, w_in=w_in, sc_conv_w=sc_conv_w, ssm_conv_w=ssm_conv_w, ssm_conv_b=ssm_conv_b,
               ssm_dt_bias=ssm_dt_bias, ssm_a_log=ssm_a_log, ssm_d=ssm_d, ssm_norm_w=ssm_norm_w, w_sc_out=w_sc_out,
               w_sb_out=w_sb_out, w_ssm_out=w_ssm_out, w_o=w_o, w_ffn_in=w_ffn_in, w_ffn_out=w_ffn_out)
    ms = dict(mod_w=m_mod_w, mod_b=m_mod_b, g_pre_mix=m_g_pre_mix, g_post_mix=m_g_post_mix, g_pre_ffn=m_g_pre_ffn,
              g_post_ffn=m_g_post_ffn, w_in=m_w_in, sc_conv_w=m_sc_conv_w, ssm_conv_w=m_ssm_conv_w,
              ssm_conv_b=m_ssm_conv_b, ssm_dt_bias=m_ssm_dt_bias, ssm_a_log=m_ssm_a_log, ssm_d=m_ssm_d,
              ssm_norm_w=m_ssm_norm_w, w_sc_out=m_w_sc_out, w_sb_out=m_w_sb_out, w_ssm_out=m_w_ssm_out, w_o=m_w_o,
              w_ffn_in=m_w_ffn_in, w_ffn_out=m_w_ffn_out)
    vs = dict(mod_w=v_mod_w, mod_b=v_mod_b, g_pre_mix=v_g_pre_mix, g_post_mix=v_g_post_mix, g_pre_ffn=v_g_pre_ffn,
              g_post_ffn=v_g_post_ffn, w_in=v_w_in, sc_conv_w=v_sc_conv_w, ssm_conv_w=v_ssm_conv_w,
              ssm_conv_b=v_ssm_conv_b, ssm_dt_bias=v_ssm_dt_bias, ssm_a_log=v_ssm_a_log, ssm_d=v_ssm_d,
              ssm_norm_w=v_ssm_norm_w, w_sc_out=v_w_sc_out, w_sb_out=v_w_sb_out, w_ssm_out=v_w_ssm_out, w_o=v_w_o,
              w_ffn_in=v_w_ffn_in, w_ffn_out=v_w_ffn_out)
    me = 4 * lax.axis_index("x") + 2 * lax.axis_index("y") + lax.axis_index("c")
    mod_cols = mod_w.shape[2]

    pack1, sizes1 = _pack_rows([c, sc_conv_w, ssm_conv_w], F32, 8)
    got1 = all_gather_rows(pack1, "gather_c_conv").reshape(N_DEV, -1)
    c_all, sc_g, ssm_g = _unpack(got1, sizes1, [(D_MODEL,), sc_conv_w.shape, ssm_conv_w.shape])
    conv = dict(sc_conv_w=_gathered_to_full(sc_g, False), ssm_conv_w=_gathered_to_full(ssm_g, False))

    mod_b_shard = lax.dynamic_slice_in_dim(mod_b, me * mod_cols, mod_cols, axis=1).reshape(DEPTH, 1, mod_cols)
    mod_sh = mod_shard_fwd(c_all, mod_w, mod_b_shard, "mod_shard_fwd")
    pack2, sizes2 = _pack_rows([mod_sh], F32, 8)
    got2 = all_gather_rows(pack2, "gather_mod").reshape(N_DEV, -1)
    mod_all = _unpack(got2, sizes2, [mod_sh.shape])[0]
    mod_mine = lax.dynamic_index_in_dim(mod_all, me, axis=2, keepdims=False)
    mod = mod_mine.transpose(1, 0, 2).reshape(DEPTH, 6, D_MODEL)

    mm_names = [k for k in _BIG if k not in _CONV]
    gathered = all_gather_multi([wts[k].astype(BF16) for k in mm_names], "gather_weights")
    big = {k: _gathered_to_full(g, k in _ROW_SHARDED) for k, g in zip(mm_names, gathered)}

    small = {k: wts[k] for k in _SMALL}
    loss_part, dx, dmod, grads = _local_step(x[0], loss_target[0], mod, small, conv, big)
    loss = lax.psum(loss_part, ("x", "y", "c"))

    small_parts = [dmod.reshape(DEPTH, 6 * D_MODEL)] + [grads[k] for k in _SMALL[1:]]
    pack5, sizes5 = _pack_rows(small_parts, F32, 8)
    pack_conv, sizes_conv = _pack_rows([grads[k] for k in _CONV], F32, 8)
    got5, got_conv = all_gather_multi([pack5, pack_conv], "gather_small_grads")
    w5, _ = _pack_rows([wts[k] for k in _SMALL], F32, 8)
    m5, _ = _pack_rows([ms[k] for k in _SMALL], F32, 8)
    v5, _ = _pack_rows([vs[k] for k in _SMALL], F32, 8)
    res5 = adamw_flat(got5, w5, m5, v5, "adamw_small")
    small_out = [_unpack(r.reshape(-1), sizes5, [wts[k].shape for k in _SMALL]) for r in res5]

    conv_full = _unpack(got_conv.reshape(N_DEV, -1), sizes_conv, [grads[k].shape for k in _CONV])
    conv_mine = [lax.dynamic_slice_in_dim(g, me * wts[k].shape[2], wts[k].shape[2], axis=3)
                 for k, g in zip(_CONV, conv_full)]
    conv_slot_sizes = [math.prod(wts[k].shape) for k in _CONV]
    conv_slots = jnp.concatenate([g.reshape(N_DEV, -1) for g in conv_mine], axis=1)
    pad_c = -conv_slots.shape[1] % (8 * LANES)
    conv_slots = jnp.pad(conv_slots, ((0, 0), (0, pad_c))).reshape(N_DEV, -1, LANES)
    wc, _ = _pack_rows([wts[k] for k in _CONV], F32, 8)
    mc, _ = _pack_rows([ms[k] for k in _CONV], F32, 8)
    vc, _ = _pack_rows([vs[k] for k in _CONV], F32, 8)
    res_c = adamw_flat(conv_slots, wc, mc, vc, "adamw_conv")
    conv_out = [_unpack(r.reshape(-1), conv_slot_sizes, [wts[k].shape for k in _CONV]) for r in res_c]

    dmod_all = got5.reshape(N_DEV, -1)[:, :DEPTH * 6 * D_MODEL].reshape(N_DEV, DEPTH, 6 * D_MODEL)
    dmod_shard = lax.dynamic_slice_in_dim(dmod_all, me * mod_cols, mod_cols, axis=2).transpose(1, 0, 2)
    g_mod_w = mod_w_grad(c_all, dmod_shard, "mod_w_grad")
    rows2 = lambda a: a.reshape(a.shape[0] * a.shape[1], a.shape[2])
    res_mw = adamw_flat(rows2(g_mod_w)[None], rows2(mod_w), rows2(m_mod_w), rows2(v_mod_w), "adamw_mod_w")
    mod_w_out = [r.reshape(mod_w.shape) for r in res_mw]

    cidx = lax.axis_index("c")
    keeps, gives = [], []
    for k in mm_names:
        slots = _full_to_slots(grads[k], k in _ROW_SHARDED)
        by_core = slots.reshape(4, 2, *slots.shape[1:]).transpose(1, 0, 2, 3).astype(BF16)
        keeps.append(lax.dynamic_index_in_dim(by_core, cidx, 0, keepdims=False))
        gives.append(lax.dynamic_index_in_dim(by_core, 1 - cidx, 0, keepdims=False))
    gots = swap_with_sibling(gives, "swap_grads")
    pairs = []
    for k, keep, got in zip(mm_names, keeps, gots):
        rows4 = (4 * keep.shape[1], keep.shape[2])
        pairs.append(add_pairs(keep.reshape(rows4), got.reshape(rows4), f"add_pairs_{k}").reshape(keep.shape))
    recvs = exchange_chips(pairs, "exchange_grads")
    big_out = {}
    for k, recv in zip(mm_names, recvs):
        res = adamw_flat(recv, rows2(wts[k]), rows2(ms[k]), rows2(vs[k]), f"adamw_{k}")
        big_out[k] = [r.reshape(wts[k].shape) for r in res]

    outs = []
    for kind in range(4):
        by_name = {"mod_w": mod_w_out[kind]}
        by_name.update(zip(_SMALL, small_out[kind]))
        by_name.update(zip(_CONV, conv_out[kind]))
        by_name.update({k: v[kind] for k, v in big_out.items()})
        outs.extend(by_name[k] for k in _WEIGHTS)
    return (loss, dx[None], *outs)
```

```python
import functools
import math

import jax
import jax.numpy as jnp
from jax import lax
from jax.experimental import pallas as pl
from jax.experimental.pallas import tpu as pltpu

F32 = jnp.float32
BF16 = jnp.bfloat16
HI = lax.Precision.HIGHEST

N_DEV = 8
D_MODEL = 1024
DEPTH = 2
SC_WIDTH = 256
SB_WIDTH = 256
SB_HEAD_DIM = 64
SSM_INNER = 512
SSM_HEADS = 8
SSM_HEAD_DIM = 64
SSM_GROUPS = 2
SSM_STATE = 64
SSM_CHUNK = 256
SSM_CONV_DIM = 768
FFN_HIDDEN = 2816
NORM_EPS = 1e-6
IN_PROJ = 5896
LANES = 128
VMEM_LIMIT = 56 * 1024 * 1024

OFF_GATES = 0
OFF_SC = 3072
OFF_QKV = 3840
OFF_XBC = 4608
OFF_DT = 5376
OFF_Z = 5632
IN_PAD = 6144

ADAM_LR = 0.001
ADAM_B1 = 0.9
ADAM_B2 = 0.999
ADAM_EPS = 1e-08
ADAM_WD = 0.01
ADAM_STEP = 10

MESH_ID = pl.DeviceIdType.MESH


def _cparams(sem=None):
    return pltpu.CompilerParams(dimension_semantics=sem, vmem_limit_bytes=VMEM_LIMIT)


def _my_pos():
    return lax.axis_index("x"), lax.axis_index("y"), lax.axis_index("c")


def _peer(k, x, y, c):
    return (x ^ ((k >> 2) & 1), y ^ ((k >> 1) & 1), c ^ (k & 1))


def all_gather_rows(block, name):
    rows, lanes = block.shape

    def body(x_ref, out_ref, send_sems, recv_sems, local_sem):
        x, y, c = _my_pos()
        me, sibling = (x, y, c), (x, y, 1 - c)
        chips = [(1 - x, y), (x, 1 - y), (1 - x, 1 - y)]

        def slot(px, py, pc):
            return out_ref.at[4 * px + 2 * py + pc]

        def copy(k, blk, to, src=None):
            return pltpu.make_async_remote_copy(
                src_ref=slot(*blk) if src is None else src, dst_ref=slot(*blk),
                send_sem=send_sems.at[k], recv_sem=recv_sems.at[k], device_id=to, device_id_type=MESH_ID)

        mine = pltpu.make_async_copy(x_ref, slot(*me), local_sem)
        mine.start()
        first = [copy(0, me, sibling, src=x_ref)]
        first += [copy(1 + j, me, (*chip, c), src=x_ref) for j, chip in enumerate(chips)]
        for cp in first:
            cp.start()
        passed = [copy(4 + j, (*chip, c), sibling) for j, chip in enumerate(chips)]
        for j, chip in enumerate(chips):
            copy(1 + j, (*chip, c), me).wait_recv()
            passed[j].start()
        copy(0, sibling, me).wait_recv()
        for j, chip in enumerate(chips):
            copy(4 + j, (*chip, 1 - c), me).wait_recv()
        for cp in first + passed:
            cp.wait_send()
        mine.wait()

    return pl.pallas_call(
        body, name=name,
        out_shape=jax.ShapeDtypeStruct((N_DEV, rows, lanes), block.dtype),
        in_specs=[pl.BlockSpec(memory_space=pl.ANY)],
        out_specs=pl.BlockSpec(memory_space=pl.ANY),
        scratch_shapes=[pltpu.SemaphoreType.DMA((7,)), pltpu.SemaphoreType.DMA((7,)), pltpu.SemaphoreType.DMA],
    )(block)


def exchange_slots(send, name):
    _, rows, lanes = send.shape

    def body(s_ref, r_ref, send_sems, recv_sems, local_sem):
        x, y, c = _my_pos()
        me = 4 * x + 2 * y + c
        mine = pltpu.make_async_copy(s_ref.at[me], r_ref.at[me], local_sem)
        mine.start()
        copies = []
        for k in range(1, N_DEV):
            px, py, pc = _peer(k, x, y, c)
            cp = pltpu.make_async_remote_copy(
                src_ref=s_ref.at[4 * px + 2 * py + pc], dst_ref=r_ref.at[me],
                send_sem=send_sems.at[k - 1], recv_sem=recv_sems.at[k - 1],
                device_id=(px, py, pc), device_id_type=MESH_ID)
            cp.start()
            copies.append(cp)
        for cp in copies:
            cp.wait_recv()
        for cp in copies:
            cp.wait_send()
        mine.wait()

    return pl.pallas_call(
        body, name=name,
        out_shape=jax.ShapeDtypeStruct((N_DEV, rows, lanes), send.dtype),
        in_specs=[pl.BlockSpec(memory_space=pl.ANY)],
        out_specs=pl.BlockSpec(memory_space=pl.ANY),
        scratch_shapes=[pltpu.SemaphoreType.DMA((7,)), pltpu.SemaphoreType.DMA((7,)), pltpu.SemaphoreType.DMA],
    )(send)


def all_gather_multi(blocks, name):
    n = len(blocks)

    def body(*refs):
        x_refs, o_refs = refs[:n], refs[n:2 * n]
        send_sems, recv_sems, local_sems = refs[2 * n:]
        x, y, c = _my_pos()
        me, sibling = (x, y, c), (x, y, 1 - c)
        chips = [(1 - x, y), (x, 1 - y), (1 - x, 1 - y)]

        def slot(a, px, py, pc):
            return o_refs[a].at[4 * px + 2 * py + pc]

        def copy(a, k, blk, to, src=None):
            return pltpu.make_async_remote_copy(
                src_ref=slot(a, *blk) if src is None else src, dst_ref=slot(a, *blk),
                send_sem=send_sems.at[7 * a + k], recv_sem=recv_sems.at[7 * a + k], device_id=to, device_id_type=MESH_ID)

        mine = [pltpu.make_async_copy(x_refs[a], slot(a, *me), local_sems.at[a]) for a in range(n)]
        for cp in mine:
            cp.start()
        first = [copy(a, 1 + j, me, (*chip, c), src=x_refs[a]) for j, chip in enumerate(chips) for a in range(n)]
        first += [copy(a, 0, me, sibling, src=x_refs[a]) for a in range(n)]
        for cp in first:
            cp.start()
        passed = []
        for j, chip in enumerate(chips):
            for a in range(n):
                copy(a, 1 + j, (*chip, c), me).wait_recv()
                fwd = copy(a, 4 + j, (*chip, c), sibling)
                fwd.start()
                passed.append(fwd)
        for a in range(n):
            copy(a, 0, sibling, me).wait_recv()
            for j, chip in enumerate(chips):
                copy(a, 4 + j, (*chip, 1 - c), me).wait_recv()
        for cp in first + passed:
            cp.wait_send()
        for cp in mine:
            cp.wait()

    any_spec = pl.BlockSpec(memory_space=pl.ANY)
    return pl.pallas_call(
        body, name=name,
        out_shape=[jax.ShapeDtypeStruct((N_DEV,) + b.shape, b.dtype) for b in blocks],
        in_specs=[any_spec] * n, out_specs=[any_spec] * n,
        scratch_shapes=[pltpu.SemaphoreType.DMA((7 * n,)), pltpu.SemaphoreType.DMA((7 * n,)), pltpu.SemaphoreType.DMA((n,))],
    )(*blocks)


def exchange_multi(sends, name):
    n = len(sends)

    def body(*refs):
        s_refs, r_refs = refs[:n], refs[n:2 * n]
        send_sems, recv_sems, local_sems = refs[2 * n:]
        x, y, c = _my_pos()
        me = 4 * x + 2 * y + c
        mine = [pltpu.make_async_copy(s_refs[a].at[me], r_refs[a].at[me], local_sems.at[a]) for a in range(n)]
        for cp in mine:
            cp.start()
        copies = []
        for k in (2, 4, 6, 3, 5, 7, 1):
            px, py, pc = _peer(k, x, y, c)
            for a in range(n):
                cp = pltpu.make_async_remote_copy(
                    src_ref=s_refs[a].at[4 * px + 2 * py + pc], dst_ref=r_refs[a].at[me],
                    send_sem=send_sems.at[7 * a + k - 1], recv_sem=recv_sems.at[7 * a + k - 1],
                    device_id=(px, py, pc), device_id_type=MESH_ID)
                cp.start()
                copies.append(cp)
        for cp in copies:
            cp.wait_recv()
        for cp in copies:
            cp.wait_send()
        for cp in mine:
            cp.wait()

    any_spec = pl.BlockSpec(memory_space=pl.ANY)
    return pl.pallas_call(
        body, name=name,
        out_shape=[jax.ShapeDtypeStruct(s.shape, s.dtype) for s in sends],
        in_specs=[any_spec] * n, out_specs=[any_spec] * n,
        scratch_shapes=[pltpu.SemaphoreType.DMA((7 * n,)), pltpu.SemaphoreType.DMA((7 * n,)), pltpu.SemaphoreType.DMA((n,))],
    )(*sends)


def swap_with_sibling(gives, name):
    n = len(gives)

    def body(*refs):
        g_refs, r_refs = refs[:n], refs[n:2 * n]
        send_sems, recv_sems = refs[2 * n:]
        x, y, c = _my_pos()
        copies = [pltpu.make_async_remote_copy(
            src_ref=g_refs[a], dst_ref=r_refs[a], send_sem=send_sems.at[a], recv_sem=recv_sems.at[a],
            device_id=(x, y, 1 - c), device_id_type=MESH_ID) for a in range(n)]
        for cp in copies:
            cp.start()
        for cp in copies:
            cp.wait_recv()
        for cp in copies:
            cp.wait_send()

    any_spec = pl.BlockSpec(memory_space=pl.ANY)
    return pl.pallas_call(
        body, name=name, out_shape=[jax.ShapeDtypeStruct(g.shape, g.dtype) for g in gives],
        in_specs=[any_spec] * n, out_specs=[any_spec] * n,
        scratch_shapes=[pltpu.SemaphoreType.DMA((n,)), pltpu.SemaphoreType.DMA((n,))],
    )(*gives)


def exchange_chips(sends, name):
    n = len(sends)

    def body(*refs):
        s_refs, r_refs = refs[:n], refs[n:2 * n]
        send_sems, recv_sems, local_sems = refs[2 * n:]
        x, y, c = _my_pos()
        me = 2 * x + y
        mine = [pltpu.make_async_copy(s_refs[a].at[me], r_refs[a].at[me], local_sems.at[a]) for a in range(n)]
        for cp in mine:
            cp.start()
        copies = []
        for k in (2, 1, 3):
            px, py = x ^ (k >> 1), y ^ (k & 1)
            for a in range(n):
                cp = pltpu.make_async_remote_copy(
                    src_ref=s_refs[a].at[2 * px + py], dst_ref=r_refs[a].at[me],
                    send_sem=send_sems.at[3 * a + k - 1], recv_sem=recv_sems.at[3 * a + k - 1],
                    device_id=(px, py, c), device_id_type=MESH_ID)
                cp.start()
                copies.append(cp)
        for cp in copies:
            cp.wait_recv()
        for cp in copies:
            cp.wait_send()
        for cp in mine:
            cp.wait()

    any_spec = pl.BlockSpec(memory_space=pl.ANY)
    return pl.pallas_call(
        body, name=name, out_shape=[jax.ShapeDtypeStruct(s.shape, s.dtype) for s in sends],
        in_specs=[any_spec] * n, out_specs=[any_spec] * n,
        scratch_shapes=[pltpu.SemaphoreType.DMA((3 * n,)), pltpu.SemaphoreType.DMA((3 * n,)), pltpu.SemaphoreType.DMA((n,))],
    )(*sends)


def add_pairs(a, b, name, tr=512):
    rows, cols = a.shape
    tr = max(d for d in range(16, min(tr, rows) + 1, 16) if rows % d == 0)

    def body(a_ref, b_ref, o_ref):
        o_ref[...] = (a_ref[...].astype(F32) + b_ref[...].astype(F32)).astype(BF16)

    tile = pl.BlockSpec((tr, cols), lambda i: (i, 0))
    return pl.pallas_call(body, name=name, grid=(rows // tr,), in_specs=[tile, tile], out_specs=tile,
                          out_shape=jax.ShapeDtypeStruct((rows, cols), BF16), compiler_params=_cparams(("parallel",)))(a, b)


def _pack_rows(parts, dtype, row_multiple):
    flat = [p.astype(dtype).reshape(-1) for p in parts]
    sizes = [f.shape[0] for f in flat]
    total = sum(sizes)
    quantum = LANES * row_multiple
    padded = -(-total // quantum) * quantum
    if padded > total:
        flat.append(jnp.zeros((padded - total,), dtype))
    return jnp.concatenate(flat).reshape(padded // LANES, LANES), sizes


def _unpack(flat, sizes, shapes):
    out, off = [], 0
    lead = flat.shape[:-1]
    for n, shp in zip(sizes, shapes):
        out.append(flat[..., off:off + n].reshape(lead + tuple(shp)))
        off += n
    return out


def rows_call(name, body, n_rows, tr, ins, outs, scratch=(), aliases=None):
    n_tiles = n_rows // tr
    assert n_tiles * tr == n_rows
    in_specs, arrays = [], []
    for arr, kind in ins:
        arrays.append(arr)
        if kind == "row":
            in_specs.append(pl.BlockSpec((tr, arr.shape[1]), lambda i: (i, 0)))
        elif kind == "any":
            in_specs.append(pl.BlockSpec(memory_space=pl.ANY))
        elif kind == "full":
            in_specs.append(pl.BlockSpec(arr.shape, lambda i, nd=arr.ndim: (0,) * nd))
        elif kind[0] == "row":
            _, w, ci = kind
            in_specs.append(pl.BlockSpec((tr, w), lambda i, ci=ci: (i, ci)))
        elif kind[0] == "prev8":
            _, w, ci = kind
            hr = 8 * (4 // arr.dtype.itemsize)
            in_specs.append(pl.BlockSpec((hr, w), lambda i, ci=ci, hr=hr: (jnp.maximum(i * (tr // hr) - 1, 0), ci)))
        elif kind[0] == "next8":
            _, w, ci = kind
            hr = 8 * (4 // arr.dtype.itemsize)
            last = n_rows // hr - 1
            in_specs.append(pl.BlockSpec((hr, w), lambda i, ci=ci, last=last, hr=hr: (jnp.minimum((i + 1) * (tr // hr), last), ci)))
        else:
            raise ValueError(kind)
    out_specs, out_shapes = [], []
    for shape, dtype, kind in outs:
        out_shapes.append(jax.ShapeDtypeStruct(shape, dtype))
        if kind == "row":
            out_specs.append(pl.BlockSpec((tr, shape[1]), lambda i: (i, 0)))
        elif kind[0] == "row":
            _, w, ci = kind
            out_specs.append(pl.BlockSpec((tr, w), lambda i, ci=ci: (i, ci)))
        else:
            out_specs.append(pl.BlockSpec(shape, lambda i, nd=len(shape): (0,) * nd))
    has_acc = any(k == "acc" for _, _, k in outs)
    return pl.pallas_call(
        body, name=name, grid=(n_tiles,), in_specs=in_specs, out_specs=out_specs, out_shape=out_shapes,
        scratch_shapes=list(scratch), input_output_aliases=dict(aliases or {}),
        compiler_params=_cparams(("arbitrary",) if has_acc else ("parallel",)),
    )(*arrays)


def _prev8(ref):
    n = ref.shape[0]
    return ref[n - 8:n, :].astype(F32)


def _next8(ref):
    return ref[0:8, :].astype(F32)


def _acc(ref, val):
    @pl.when(pl.program_id(0) == 0)
    def _():
        ref[...] = jnp.zeros_like(ref)
    ref[...] += val


def _rstd(x):
    return lax.rsqrt(jnp.mean(x * x, axis=-1, keepdims=True) + NORM_EPS)


def _sigmoid(x):
    return 1.0 / (1.0 + jnp.exp(-x))


def _silu(x):
    return x * _sigmoid(x)


def _dsilu(x):
    s = _sigmoid(x)
    return s * (1.0 + x * (1.0 - s))


def _softplus(x):
    return jnp.maximum(x, 0.0) + jnp.log(1.0 + jnp.exp(-jnp.abs(x)))


def _log_sigmoid_neg(x):
    t = -x
    return jnp.minimum(t, 0.0) - jnp.log(1.0 + jnp.exp(jnp.minimum(x, t)))


def normmod_fwd(x, g, scale, shift, name):
    t, d = x.shape

    def body(x_ref, g_ref, sc_ref, sh_ref, h_ref):
        xv = x_ref[...]
        h = xv * _rstd(xv) * g_ref[...] * (1.0 + sc_ref[...]) + sh_ref[...]
        h_ref[...] = h.astype(BF16)

    return rows_call(name, body, t, 512, [(x, "row"), (g, "full"), (scale, "full"), (shift, "full")],
                     [((t, d), BF16, "row")])[0]


def resid_normmod_fwd(x, f, gate, g_post, g_pre, scale, shift, name):
    t, d = x.shape

    def body(x_ref, f_ref, gate_ref, gp_ref, g_ref, sc_ref, sh_ref, xo_ref, h_ref):
        fv = f_ref[...]
        xn = x_ref[...] + gate_ref[...] * (fv * _rstd(fv) * gp_ref[...])
        xo_ref[...] = xn
        h = xn * _rstd(xn) * g_ref[...] * (1.0 + sc_ref[...]) + sh_ref[...]
        h_ref[...] = h.astype(BF16)

    return rows_call(name, body, t, 512,
                     [(x, "row"), (f, "row"), (gate, "full"), (g_post, "full"), (g_pre, "full"), (scale, "full"),
                      (shift, "full")],
                     [((t, d), F32, "row"), ((t, d), BF16, "row")])


def resid_loss(x, f, gate, g_post, target, name):
    t, d = x.shape

    def body(x_ref, f_ref, gate_ref, gp_ref, tg_ref, dy_ref, loss_ref):
        fv = f_ref[...]
        yv = x_ref[...] + gate_ref[...] * (fv * _rstd(fv) * gp_ref[...])
        err = yv - tg_ref[...]
        dy_ref[...] = err * (1.0 / d)
        part = 0.5 * jnp.sum(jnp.mean(err * err, axis=-1, keepdims=True), axis=0, keepdims=True)
        _acc(loss_ref, jnp.broadcast_to(part, loss_ref.shape))

    return rows_call(name, body, t, 512,
                     [(x, "row"), (f, "row"), (gate, "full"), (g_post, "full"), (target, "row")],
                     [((t, d), F32, "row"), ((8, LANES), F32, "acc")])


def resid_bwd(dx, f, gate, g_post, name):
    t, d = dx.shape

    def body(dx_ref, f_ref, gate_ref, gp_ref, df_ref, dgate_ref, dg_ref):
        fv, dxv, gp = f_ref[...], dx_ref[...], gp_ref[...]
        r = _rstd(fv)
        fn = fv * r
        _acc(dgate_ref, jnp.sum(dxv * (fn * gp), axis=0, keepdims=True))
        dn = dxv * gate_ref[...]
        _acc(dg_ref, jnp.sum(dn * fn, axis=0, keepdims=True))
        u = dn * gp
        df = r * (u - fn * jnp.mean(fn * u, axis=-1, keepdims=True))
        df_ref[...] = df.astype(BF16)

    return rows_call(name, body, t, 512, [(dx, "row"), (f, "row"), (gate, "full"), (g_post, "full")],
                     [((t, d), BF16, "row"), ((1, d), F32, "acc"), ((1, d), F32, "acc")])


def normmod_bwd(dh, x, dx_in, g, scale, name):
    t, d = x.shape

    def body(dh_ref, x_ref, dxi_ref, g_ref, sc_ref, dx_ref, dsc_ref, dsh_ref, dg_ref):
        xv, dhv, gv = x_ref[...], dh_ref[...], g_ref[...]
        r = _rstd(xv)
        xn = xv * r
        _acc(dsc_ref, jnp.sum(dhv * (xn * gv), axis=0, keepdims=True))
        _acc(dsh_ref, jnp.sum(dhv, axis=0, keepdims=True))
        dn = dhv * (1.0 + sc_ref[...])
        _acc(dg_ref, jnp.sum(dn * xn, axis=0, keepdims=True))
        u = dn * gv
        dx_ref[...] = dxi_ref[...] + r * (u - xn * jnp.mean(xn * u, axis=-1, keepdims=True))

    return rows_call(name, body, t, 512, [(dh, "row"), (x, "row"), (dx_in, "row"), (g, "full"), (scale, "full")],
                     [((t, d), F32, "row"), ((1, d), F32, "acc"), ((1, d), F32, "acc"), ((1, d), F32, "acc")])


def _pick(n, prefs):
    for p in prefs:
        if n % p == 0:
            return p
    return n


def mm_nn(a_list, b_list, out_dtype, name, tm=1024, tn=None, tk=None):
    m, k = a_list[0].shape
    n = b_list[0].shape[1]
    tm = min(tm, m)
    tn = tn or _pick(n, (1024, 768, 512, 256, 128))
    tk = tk or _pick(k, (1024, 1408, 512, 256))
    nk = k // tk
    npair = len(a_list)

    if nk == 1 and npair == 1:
        def body1(a_ref, b_ref, o_ref):
            o_ref[...] = jnp.dot(a_ref[...], b_ref[...], preferred_element_type=F32).astype(o_ref.dtype)

        return pl.pallas_call(
            body1, name=name, grid=(m // tm, n // tn),
            in_specs=[pl.BlockSpec((tm, k), lambda i, j: (i, 0)), pl.BlockSpec((k, tn), lambda i, j: (0, j))],
            out_specs=pl.BlockSpec((tm, tn), lambda i, j: (i, j)),
            out_shape=jax.ShapeDtypeStruct((m, n), out_dtype),
            compiler_params=_cparams(("parallel", "parallel")),
        )(a_list[0], b_list[0])

    def body(*refs):
        a_refs, b_refs = refs[:npair], refs[npair:2 * npair]
        o_ref, acc = refs[2 * npair], refs[2 * npair + 1]
        kk = pl.program_id(2)

        @pl.when(kk == 0)
        def _():
            acc[...] = jnp.zeros_like(acc)

        s = acc[...]
        for a_ref, b_ref in zip(a_refs, b_refs):
            s = s + jnp.dot(a_ref[...], b_ref[...], preferred_element_type=F32)
        acc[...] = s

        @pl.when(kk == nk - 1)
        def _():
            o_ref[...] = acc[...].astype(o_ref.dtype)

    return pl.pallas_call(
        body, name=name, grid=(m // tm, n // tn, nk),
        in_specs=[pl.BlockSpec((tm, tk), lambda i, j, kk: (i, kk))] * npair
        + [pl.BlockSpec((tk, tn), lambda i, j, kk: (kk, j))] * npair,
        out_specs=pl.BlockSpec((tm, tn), lambda i, j, kk: (i, j)),
        out_shape=jax.ShapeDtypeStruct((m, n), out_dtype),
        scratch_shapes=[pltpu.VMEM((tm, tn), F32)],
        compiler_params=_cparams(("parallel", "parallel", "arbitrary")),
    )(*a_list, *b_list)


def mm_nt(a_list, b_list, b_koff, out_dtype, name, tm=1024):
    m, k = a_list[0].shape
    n = b_list[0].shape[0]
    tm = min(tm, m)
    tn = _pick(n, (1024, 512, 256))
    tk = _pick(k, (1024, 1408, 512, 256))
    nk = k // tk
    npair = len(a_list)
    koff = [o // tk for o in b_koff]
    nt_dims = (((1,), (1,)), ((), ()))

    def body(*refs):
        a_refs, b_refs = refs[:npair], refs[npair:2 * npair]
        o_ref, acc = refs[2 * npair], refs[2 * npair + 1]
        kk = pl.program_id(2)

        @pl.when(kk == 0)
        def _():
            acc[...] = jnp.zeros_like(acc)

        s = acc[...]
        for a_ref, b_ref in zip(a_refs, b_refs):
            s = s + lax.dot_general(a_ref[...], b_ref[...], nt_dims, preferred_element_type=F32)
        acc[...] = s

        @pl.when(kk == nk - 1)
        def _():
            o_ref[...] = acc[...].astype(o_ref.dtype)

    return pl.pallas_call(
        body, name=name, grid=(m // tm, n // tn, nk),
        in_specs=[pl.BlockSpec((tm, tk), lambda i, j, kk: (i, kk))] * npair
        + [pl.BlockSpec((tn, tk), lambda i, j, kk, o=o: (j, kk + o)) for o in koff],
        out_specs=pl.BlockSpec((tm, tn), lambda i, j, kk: (i, j)),
        out_shape=jax.ShapeDtypeStruct((m, n), out_dtype),
        scratch_shapes=[pltpu.VMEM((tm, tn), F32)],
        compiler_params=_cparams(("parallel", "parallel", "arbitrary")),
    )(*a_list, *b_list)


def mm_tn(a, b, name, tt=512):
    t, ka = a.shape
    n = b.shape[1]
    ta = _pick(ka, (1024, 1408, 512, 256))
    tn = _pick(n, (2048, 1024, 1408, 512, 256))
    nt = t // tt

    def body(a_ref, b_ref, o_ref, acc):
        s = pl.program_id(2)

        @pl.when(s == 0)
        def _():
            acc[...] = jnp.zeros_like(acc)

        acc[...] += lax.dot_general(a_ref[...], b_ref[...], (((0,), (0,)), ((), ())), preferred_element_type=F32)

        @pl.when(s == nt - 1)
        def _():
            o_ref[...] = acc[...].astype(BF16)

    return pl.pallas_call(
        body, name=name, grid=(ka // ta, n // tn, nt),
        in_specs=[pl.BlockSpec((tt, ta), lambda i, j, s: (s, i)), pl.BlockSpec((tt, tn), lambda i, j, s: (s, j))],
        out_specs=pl.BlockSpec((ta, tn), lambda i, j, s: (i, j)),
        out_shape=jax.ShapeDtypeStruct((ka, n), BF16),
        scratch_shapes=[pltpu.VMEM((ta, tn), F32)],
        compiler_params=_cparams(("parallel", "parallel", "arbitrary")),
    )(a, b)


def mm_swiglu_fwd(h, w_ffn_in, name, tm=512, tn=1408):
    m, k = h.shape
    nh = FFN_HIDDEN // tn

    def body(h_ref, wg_ref, wu_ref, gt_ref, up_ref, a_ref):
        hv = h_ref[...]
        gt = jnp.dot(hv, wg_ref[...], preferred_element_type=F32)
        up = jnp.dot(hv, wu_ref[...], preferred_element_type=F32)
        gt_ref[...] = gt.astype(BF16)
        up_ref[...] = up.astype(BF16)
        a_ref[...] = (_silu(gt) * up).astype(BF16)

    shp = jax.ShapeDtypeStruct((m, FFN_HIDDEN), BF16)
    ospec = pl.BlockSpec((tm, tn), lambda i, j: (i, j))
    return pl.pallas_call(
        body, name=name, grid=(m // tm, nh),
        in_specs=[pl.BlockSpec((tm, k), lambda i, j: (i, 0)), pl.BlockSpec((k, tn), lambda i, j: (0, j)),
                  pl.BlockSpec((k, tn), lambda i, j: (0, j + nh))],
        out_specs=[ospec, ospec, ospec], out_shape=[shp, shp, shp],
        compiler_params=_cparams(("parallel", "parallel")),
    )(h, w_ffn_in, w_ffn_in)


def mm_swiglu_bwd(df, w_out, gt, up, name, tm=256, sub=256):
    m, k = df.shape
    n_sub = FFN_HIDDEN // sub

    def body(df_ref, w_ref, gt_ref, up_ref, dgt_ref, dup_ref):
        dfv = df_ref[...]

        def chunk_dot(c):
            return lax.dot_general(dfv, w_ref[c * sub:(c + 1) * sub, :], (((1,), (1,)), ((), ())), preferred_element_type=F32)

        da_next = chunk_dot(0)
        for c in range(n_sub):
            da = da_next
            if c + 1 < n_sub:
                da_next = chunk_dot(c + 1)
            cols = slice(c * sub, (c + 1) * sub)
            gtv = gt_ref[:, cols].astype(F32)
            sg = _sigmoid(gtv)
            dgt_ref[:, cols] = (da * up_ref[:, cols].astype(F32) * (sg * (1.0 + gtv * (1.0 - sg)))).astype(BF16)
            dup_ref[:, cols] = (da * (gtv * sg)).astype(BF16)

    shp = jax.ShapeDtypeStruct((m, FFN_HIDDEN), BF16)
    tile = pl.BlockSpec((tm, FFN_HIDDEN), lambda i: (i, 0))
    return pl.pallas_call(
        body, name=name, grid=(m // tm,),
        in_specs=[pl.BlockSpec((tm, k), lambda i: (i, 0)), pl.BlockSpec((FFN_HIDDEN, k), lambda i: (0, 0)), tile, tile],
        out_specs=[tile, tile], out_shape=[shp, shp],
        compiler_params=_cparams(("parallel",)),
    )(df, w_out, gt, up)


def _shift_down(x, prev8, j):
    if j == 0:
        return x
    xr = pltpu.roll(x, j, 0)
    pr = pltpu.roll(prev8, j, 0)
    row = lax.broadcasted_iota(jnp.int32, (8, x.shape[1]), 0)
    head = jnp.where(row < j, pr, xr[:8])
    return head if x.shape[0] == 8 else jnp.concatenate([head, xr[8:]], axis=0)


def _shift_up(x, next8, j):
    if j == 0:
        return x
    n = x.shape[0]
    xr = pltpu.roll(x, n - j, 0)
    nr = pltpu.roll(next8, 8 - j, 0)
    row = lax.broadcasted_iota(jnp.int32, (8, x.shape[1]), 0)
    return jnp.concatenate([xr[:n - 8], jnp.where(row >= 8 - j, nr, xr[n - 8:])], axis=0)


def _conv_taps(x, prev8, w_ref, taps):
    out = None
    for k in range(taps):
        term = w_ref[k:k + 1, :] * _shift_down(x, prev8, taps - 1 - k)
        out = term if out is None else out + term
    return out


def post_inproj(p, sc_w, ssm_w, ssm_b, name, tr=512):
    t = p.shape[0]

    def body(sc_ref, scp_ref, qkv_ref, xbc_ref, xbcp_ref, scw_ref, sw_ref, sb_ref, ya_ref, qkvo_ref, act_ref):
        first = (pl.program_id(0) > 0).astype(F32)
        sc = sc_ref[...].astype(F32)
        scp = _prev8(scp_ref) * first
        u = sc[:, 256:512] * sc[:, 512:768]
        up = scp[:, 256:512] * scp[:, 512:768]
        ya_ref[...] = (sc[:, 0:256] * _conv_taps(u, up, scw_ref, 3)).astype(BF16)
        qkv = qkv_ref[...]
        qkvo_ref[:, 0:256] = (qkv[:, 0:256].astype(F32) * 0.125).astype(BF16)
        qkvo_ref[:, 256:768] = qkv[:, 256:768].astype(BF16)
        xc = _conv_taps(xbc_ref[...].astype(F32), _prev8(xbcp_ref) * first, sw_ref, 4) + sb_ref[...]
        act_ref[...] = _silu(xc)

    return rows_call(
        name, body, t, tr,
        [(p, ("row", 768, OFF_SC // 768)), (p, ("prev8", 768, OFF_SC // 768)), (p, ("row", 768, OFF_QKV // 768)),
         (p, ("row", 768, OFF_XBC // 768)), (p, ("prev8", 768, OFF_XBC // 768)),
         (sc_w, "full"), (ssm_w, "full"), (ssm_b, "full")],
        [((t, 256), BF16, "row"), ((t, 768), BF16, "row"), ((t, 768), F32, "row")])


def branch_out_fwd(ya, yb, yc, p, w_cat, name, tr=256):
    t = p.shape[0]

    def body(ya_ref, yb_ref, yc_ref, gl_ref, w_ref, o_ref):
        y_a = jnp.dot(ya_ref[...], w_ref[0:256, :], preferred_element_type=F32)
        y_b = jnp.dot(yb_ref[...].astype(BF16), w_ref[256:512, :], preferred_element_type=F32)
        y_c = jnp.dot(yc_ref[...], w_ref[512:1024, :], preferred_element_type=F32)
        m = (_sigmoid(gl_ref[:, 0:1024].astype(F32)) * y_a + _sigmoid(gl_ref[:, 1024:2048].astype(F32)) * y_b
             + _sigmoid(gl_ref[:, 2048:3072].astype(F32)) * y_c)
        o_ref[...] = m.astype(BF16)

    return rows_call(name, body, t, tr,
                     [(ya, "row"), (yb, "row"), (yc, "row"), (p, ("row", 3072, 0)), (w_cat, "full")],
                     [((t, D_MODEL), BF16, "row")])[0]


def branch_out_bwd(dm, ya, yb, yc, p, w_cat, name, tr=256):
    t = p.shape[0]
    tn_dims = (((0,), (0,)), ((), ()))

    def body(dm_ref, ya_ref, yb_ref, yc_ref, gl_ref, w_ref, dgl_ref, dya_ref, dyb_ref, dyc_ref, dw_ref):
        @pl.when(pl.program_id(0) == 0)
        def _():
            dw_ref[...] = jnp.zeros_like(dw_ref)

        dmv = dm_ref[...]
        ins = (ya_ref[...], yb_ref[...].astype(BF16), yc_ref[...])
        rows = ((0, 256), (256, 512), (512, 1024))
        outs = (dya_ref, dyb_ref, dyc_ref)
        for i in range(3):
            r0, r1 = rows[i]
            y = jnp.dot(ins[i], w_ref[r0:r1, :], preferred_element_type=F32)
            s = _sigmoid(gl_ref[:, 1024 * i:1024 * (i + 1)].astype(F32))
            dgl_ref[:, 1024 * i:1024 * (i + 1)] = (dmv * y * s * (1.0 - s)).astype(BF16)
            dy = (dmv * s).astype(BF16)
            outs[i][...] = lax.dot_general(dy, w_ref[r0:r1, :], _NT, preferred_element_type=F32)
            dw_ref[r0:r1, :] += lax.dot_general(ins[i], dy, tn_dims, preferred_element_type=F32)

    return rows_call(name, body, t, tr,
                     [(dm, "row"), (ya, "row"), (yb, "row"), (yc, "row"), (p, ("row", 3072, 0)), (w_cat, "full")],
                     [((t, IN_PAD), BF16, ("row", 3072, 0)), ((t, 256), F32, "row"), ((t, 256), F32, "row"),
                      ((t, 512), F32, "row"), ((D_MODEL, D_MODEL), F32, "acc")])


def assemble_dp(dp, dya, p, sc_w, dq, dk, dv, dact, ssm_w, ssm_b, ddt, dz, name, tr=256):
    t = p.shape[0]
    n_tiles = t // tr
    sci, xi = OFF_SC // 768, OFF_XBC // 768
    base = OFF_SC
    assert base == IN_PAD - base
    o_sc, o_qkv, o_xbc, o_dt, o_z, o_end = (c - base for c in (OFF_SC, OFF_QKV, OFF_XBC, OFF_DT, OFF_Z, IN_PAD))

    def body(dp_ref, dya_ref, dyan_ref, sc_ref, scp_ref, scn_ref, scw_ref, dq_ref, dk_ref, dv_ref,
             dact_ref, dactn_ref, xbc_ref, xbcp_ref, xbcn_ref, sw_ref, sb_ref, ddt_ref, dz_ref,
             o_ref, dscw_ref, dsw_ref, dsb_ref):
        i = pl.program_id(0)

        @pl.when(i == 0)
        def _():
            dscw_ref[...] = jnp.zeros_like(dscw_ref)
            dsw_ref[...] = jnp.zeros_like(dsw_ref)
            dsb_ref[...] = jnp.zeros_like(dsb_ref)

        first = (i > 0).astype(F32)
        last = (i < n_tiles - 1).astype(F32)
        del dp_ref
        sc = sc_ref[...].astype(F32)
        scp = _prev8(scp_ref) * first
        scn = _next8(scn_ref) * last
        u = sc[:, 256:512] * sc[:, 512:768]
        up = scp[:, 256:512] * scp[:, 512:768]
        dya_v = dya_ref[...]
        cv = _conv_taps(u, up, scw_ref, 3)
        o_ref[:, o_sc:o_sc + 256] = (dya_v * cv).astype(BF16)
        dcv = dya_v * sc[:, 0:256]
        dcvn = _next8(dyan_ref) * last * scn[:, 0:256]
        du = None
        for k in range(3):
            sh = 2 - k
            term = scw_ref[k:k + 1, :] * _shift_up(dcv, dcvn, sh)
            du = term if du is None else du + term
            dscw_ref[k:k + 1, :] += jnp.sum(dcv * _shift_down(u, up, sh), axis=0, keepdims=True)
        o_ref[:, o_sc + 256:o_sc + 512] = (du * sc[:, 512:768]).astype(BF16)
        o_ref[:, o_sc + 512:o_sc + 768] = (du * sc[:, 256:512]).astype(BF16)
        o_ref[:, o_qkv:o_qkv + 256] = (dq_ref[...] * 0.125).astype(BF16)
        o_ref[:, o_qkv + 256:o_qkv + 512] = dk_ref[...].astype(BF16)
        o_ref[:, o_qkv + 512:o_qkv + 768] = dv_ref[...].astype(BF16)
        xb = xbc_ref[...].astype(F32)
        xbp = _prev8(xbcp_ref) * first
        xbn = _next8(xbcn_ref)
        xc = _conv_taps(xb, xbp, sw_ref, 4) + sb_ref[...]
        xcn = _conv_taps(xbn, xb[tr - 8:, :], sw_ref, 4) + sb_ref[...]
        dxc = dact_ref[...] * _dsilu(xc)
        dxcn = _next8(dactn_ref) * _dsilu(xcn) * last
        dxb = None
        for k in range(4):
            sh = 3 - k
            term = sw_ref[k:k + 1, :] * _shift_up(dxc, dxcn, sh)
            dxb = term if dxb is None else dxb + term
            dsw_ref[k:k + 1, :] += jnp.sum(dxc * _shift_down(xb, xbp, sh), axis=0, keepdims=True)
        dsb_ref[...] += jnp.sum(dxc, axis=0, keepdims=True)
        o_ref[:, o_xbc:o_xbc + 768] = dxb.astype(BF16)
        o_ref[:, o_dt:o_dt + 128] = ddt_ref[...].astype(BF16)
        o_ref[:, o_dt + 128:o_z] = jnp.zeros((tr, o_z - o_dt - 128), BF16)
        o_ref[:, o_z:o_end] = dz_ref[...].astype(BF16)

    return rows_call(
        name, body, t, tr,
        [(dp, "any"), (dya, "row"), (dya, ("next8", 256, 0)),
         (p, ("row", 768, sci)), (p, ("prev8", 768, sci)), (p, ("next8", 768, sci)), (sc_w, "full"),
         (dq, "row"), (dk, "row"), (dv, "row"),
         (dact, "row"), (dact, ("next8", 768, 0)),
         (p, ("row", 768, xi)), (p, ("prev8", 768, xi)), (p, ("next8", 768, xi)), (ssm_w, "full"), (ssm_b, "full"),
         (ddt, "row"), (dz, "row")],
        [((t, IN_PAD), BF16, ("row", IN_PAD - base, 1)), ((8, 256), F32, "acc"), ((8, 768), F32, "acc"),
         ((1, 768), F32, "acc")], aliases={0: 0})


def adamw_flat(slots, w, m, v, name, tr=512):
    n_slots, rows, lanes = slots.shape
    tr = max(d for d in range(8, min(tr, rows) + 1, 8) if rows % d == 0) if rows % 8 == 0 else rows
    bc1 = 1.0 - ADAM_B1 ** ADAM_STEP
    bc2 = 1.0 - ADAM_B2 ** ADAM_STEP

    def body(s_ref, w_ref, m_ref, v_ref, g_ref, d_ref, mo_ref, vo_ref):
        g = s_ref[0].astype(F32)
        for k in range(1, n_slots):
            g = g + s_ref[k].astype(F32)
        mn = ADAM_B1 * m_ref[...] + (1.0 - ADAM_B1) * g
        vn = ADAM_B2 * v_ref[...] + (1.0 - ADAM_B2) * (g * g)
        m_hat = mn / bc1
        v_hat = vn / bc2
        g_ref[...] = g
        d_ref[...] = -ADAM_LR * (m_hat / (jnp.sqrt(v_hat) + ADAM_EPS) + ADAM_WD * w_ref[...])
        mo_ref[...] = mn
        vo_ref[...] = vn

    tile = pl.BlockSpec((tr, lanes), lambda i: (i, 0))
    shp = jax.ShapeDtypeStruct((rows, lanes), F32)
    return pl.pallas_call(
        body, name=name, grid=(rows // tr,),
        in_specs=[pl.BlockSpec((n_slots, tr, lanes), lambda i: (0, i, 0)), tile, tile, tile],
        out_specs=[tile] * 4, out_shape=[shp] * 4,
        compiler_params=_cparams(("parallel",)),
    )(slots, w, m, v)


def _split_dot(x, tri):
    hi = x.astype(BF16)
    lo = (x - hi.astype(F32)).astype(BF16)
    return jnp.dot(hi, tri, preferred_element_type=F32) + jnp.dot(lo, tri, preferred_element_type=F32)


_NT = (((1,), (1,)), ((), ()))
_TN = (((0,), (0,)), ((), ()))

SBA_EXP_ZERO = -104.0
SBA_SKIPPED = -1e30


def sba_fwd(qkv, name, bq=256, bk=256):
    t = qkv.shape[0]
    assert bq == bk and t // bk <= LANES

    def body(q_ref, k_ref, v_ref, o_ref, runs_ref, acc_s, run_s):
        i = pl.program_id(1)
        lane = lax.broadcasted_iota(jnp.int32, (1, LANES), 1)
        lane_q = lax.broadcasted_iota(jnp.int32, (bq, LANES), 1)
        qi = lax.broadcasted_iota(jnp.int32, (bq, bk), 0) + i * bq
        kj = lax.broadcasted_iota(jnp.int32, (bq, bk), 1)
        later = (lax.broadcasted_iota(jnp.int32, (bk, bk), 0) > lax.broadcasted_iota(jnp.int32, (bk, bk), 1)).astype(BF16)
        qv = q_ref[...]
        qms = [jnp.where(hm, qv, jnp.zeros_like(qv)) for hm in (lane < 64, lane >= 64)]
        acc_s[...] = jnp.zeros_like(acc_s)
        run_s[...] = jnp.zeros_like(run_s)
        runs_ref[...] = jnp.full(runs_ref.shape, SBA_SKIPPED, F32)

        def tiles(specs):
            chains = [(ti, hh) for ti in range(len(specs)) for hh in range(2)]
            kb = [k_ref[pl.ds(pl.multiple_of(j * bk, bk), bk), :] for j, _ in specs]
            vb = [v_ref[pl.ds(pl.multiple_of(j * bk, bk), bk), :] for j, _ in specs]
            mask = [(kj + j * bk) < qi if masked else None for j, masked in specs]
            s = {c: lax.dot_general(qms[c[1]], kb[c[0]], _NT, preferred_element_type=F32) for c in chains}
            lk = {c: _log_sigmoid_neg(s[c]) for c in chains}
            lk = {c: lk[c] if mask[c[0]] is None else jnp.where(mask[c[0]], lk[c], 0.0) for c in chains}
            w = {c: jnp.dot(lk[c].astype(BF16), later, preferred_element_type=F32) for c in chains}
            run = {}
            for hh in range(2):
                carry = run_s[hh]
                for ti in range(len(specs)):
                    run[ti, hh] = carry
                    carry = carry + jnp.sum(lk[ti, hh], axis=1, keepdims=True)
                run_s[hh] = carry
            a = {c: jnp.exp(s[c] + lk[c] + w[c] + run[c]) for c in chains}
            a = {c: a[c] if mask[c[0]] is None else jnp.where(mask[c[0]], a[c], 0.0) for c in chains}
            for ti, hh in chains:
                acc_s[hh] += jnp.dot(a[ti, hh].astype(BF16), vb[ti], preferred_element_type=F32)
                runs_ref[hh] = jnp.where(lane_q == specs[ti][0], run[ti, hh], runs_ref[hh])

        @pl.when(i == 0)
        def _():
            tiles([(i, True)])

        @pl.when(i > 0)
        def _():
            tiles([(i, True), (i - 1, False)])

        def live():
            return jnp.maximum(jnp.max(run_s[0]), jnp.max(run_s[1])) >= SBA_EXP_ZERO

        def cond(state):
            n, go = state
            return jnp.logical_and(n < i, go)

        def step(state):
            n, _ = state
            tiles([(i - 1 - n, False)])
            return n + 1, live()

        lax.while_loop(cond, step, (jnp.int32(1), live()))
        o_ref[...] = jnp.where(lane < 64, acc_s[0], acc_s[1])

    return pl.pallas_call(
        body, name=name, grid=(2, t // bq),
        in_specs=[pl.BlockSpec((bq, LANES), lambda p, i: (i, p)), pl.BlockSpec((t, LANES), lambda p, i: (0, 2 + p)),
                  pl.BlockSpec((t, LANES), lambda p, i: (0, 4 + p))],
        out_specs=[pl.BlockSpec((bq, LANES), lambda p, i: (i, p)), pl.BlockSpec((2, bq, LANES), lambda p, i: (p, i, 0))],
        out_shape=[jax.ShapeDtypeStruct((t, SB_WIDTH), F32), jax.ShapeDtypeStruct((4, t, LANES), F32)],
        scratch_shapes=[pltpu.VMEM((2, bq, LANES), F32), pltpu.VMEM((2, bq, 1), F32)],
        compiler_params=_cparams(("parallel", "parallel")),
    )(qkv, qkv, qkv)


def sba_bwd(qkv, runs, do, name, bq=256, bk=256):
    t = qkv.shape[0]
    assert bq == bk
    nq = t // bq

    def body(q_ref, k_ref, v_ref, runs_ref, do_ref, dq_ref, dk_hbm, dv_hbm, dk_s, dv_s, sem, dq_s, rg_s):
        p = pl.program_id(0)
        i = pl.program_id(1)

        @pl.when(i == 0)
        def _():
            dk_s[...] = jnp.zeros_like(dk_s)
            dv_s[...] = jnp.zeros_like(dv_s)

        lane = lax.broadcasted_iota(jnp.int32, (1, LANES), 1)
        qi = lax.broadcasted_iota(jnp.int32, (bq, bk), 0) + i * bq
        kj = lax.broadcasted_iota(jnp.int32, (bq, bk), 1)
        r2 = lax.broadcasted_iota(jnp.int32, (bk, bk), 0)
        c2 = lax.broadcasted_iota(jnp.int32, (bk, bk), 1)
        later = (r2 > c2).astype(BF16)
        earlier = (r2 < c2).astype(BF16)
        qv = q_ref[...]
        dov = do_ref[...]
        heads = range(2)
        hms = (lane < 64, lane >= 64)
        qms = [jnp.where(hm, qv, jnp.zeros_like(qv)) for hm in hms]
        doms = [jnp.where(hm, dov, 0.0).astype(BF16) for hm in hms]
        runs = [runs_ref[hh] for hh in heads]
        dq_s[...] = jnp.zeros_like(dq_s)
        rg_s[...] = jnp.zeros_like(rg_s)

        def tiles(specs):
            nt = len(specs)
            chains = [(ti, hh) for ti in range(nt) for hh in heads]
            starts = [pl.multiple_of(j * bk, bk) for j, _ in specs]
            kb = [k_ref[pl.ds(st, bk), :] for st in starts]
            vb = [v_ref[pl.ds(st, bk), :] for st in starts]
            mask = [(kj + j * bk) < qi if masked else None for j, masked in specs]
            s = {c: lax.dot_general(qms[c[1]], kb[c[0]], _NT, preferred_element_type=F32) for c in chains}
            da = {c: lax.dot_general(doms[c[1]], vb[c[0]], _NT, preferred_element_type=F32) for c in chains}
            lk_raw = {c: _log_sigmoid_neg(s[c]) for c in chains}
            lk = {c: lk_raw[c] if mask[c[0]] is None else jnp.where(mask[c[0]], lk_raw[c], 0.0) for c in chains}
            w = {c: jnp.dot(lk[c].astype(BF16), later, preferred_element_type=F32) for c in chains}
            run = {c: jnp.sum(jnp.where(lane == specs[c[0]][0], runs[c[1]], 0.0), axis=1, keepdims=True) for c in chains}
            a = {c: jnp.exp(s[c] + lk[c] + w[c] + run[c]) for c in chains}
            a = {c: a[c] if mask[c[0]] is None else jnp.where(mask[c[0]], a[c], 0.0) for c in chains}
            g = {c: a[c] * da[c] for c in chains}
            rg = {}
            for hh in heads:
                carry = rg_s[hh]
                for ti in range(nt):
                    rg[ti, hh] = carry
                    carry = carry + jnp.sum(g[ti, hh], axis=1, keepdims=True)
                rg_s[hh] = carry
            cpre = {c: rg[c] + _split_dot(g[c], earlier) for c in chains}
            dz = {c: g[c] - jnp.exp(s[c] + lk_raw[c]) * (g[c] + cpre[c]) for c in chains}
            dz = {c: (dz[c] if mask[c[0]] is None else jnp.where(mask[c[0]], dz[c], 0.0)).astype(BF16) for c in chains}
            for ti, hh in chains:
                dq_s[hh] += jnp.dot(dz[ti, hh], kb[ti], preferred_element_type=F32)
            for ti in range(nt):
                dk = lax.dot_general(dz[ti, 0], qms[0], _TN, preferred_element_type=F32)
                dk_s[pl.ds(starts[ti], bk), :] += dk + lax.dot_general(dz[ti, 1], qms[1], _TN, preferred_element_type=F32)
                dv = lax.dot_general(a[ti, 0].astype(BF16), doms[0], _TN, preferred_element_type=F32)
                dv_s[pl.ds(starts[ti], bk), :] += dv + lax.dot_general(a[ti, 1].astype(BF16), doms[1], _TN,
                                                                       preferred_element_type=F32)

        live = jnp.maximum(jnp.max(runs[0], axis=0, keepdims=True), jnp.max(runs[1], axis=0, keepdims=True)) >= SBA_EXP_ZERO
        first = jnp.minimum(jnp.min(jnp.where(live, lane, LANES)), i)

        def step(j, carry):
            tiles([(j, False)])
            return carry

        lax.fori_loop(first, i - 1, step, 0)

        @pl.when(i == 0)
        def _():
            tiles([(i, True)])

        @pl.when(i > 0)
        def _():
            tiles([(i - 1, False), (i, True)])

        dq_ref[...] = jnp.where(lane < 64, dq_s[0], dq_s[1])

        @pl.when(i == nq - 1)
        def _():
            col = pl.multiple_of(p * LANES, LANES)
            ck = pltpu.make_async_copy(dk_s, dk_hbm.at[:, pl.ds(col, LANES)], sem.at[0])
            cv = pltpu.make_async_copy(dv_s, dv_hbm.at[:, pl.ds(col, LANES)], sem.at[1])
            ck.start()
            cv.start()
            ck.wait()
            cv.wait()

    shp = jax.ShapeDtypeStruct((t, SB_WIDTH), F32)
    tile = pl.BlockSpec((bq, LANES), lambda p, i: (i, p))
    return pl.pallas_call(
        body, name=name, grid=(2, nq),
        in_specs=[tile, pl.BlockSpec((t, LANES), lambda p, i: (0, 2 + p)), pl.BlockSpec((t, LANES), lambda p, i: (0, 4 + p)),
                  pl.BlockSpec((2, bq, LANES), lambda p, i: (p, i, 0)), tile],
        out_specs=[tile, pl.BlockSpec(memory_space=pl.ANY), pl.BlockSpec(memory_space=pl.ANY)],
        out_shape=[shp, shp, shp],
        scratch_shapes=[pltpu.VMEM((t, LANES), F32), pltpu.VMEM((t, LANES), F32), pltpu.SemaphoreType.DMA((2,)),
                        pltpu.VMEM((2, bq, LANES), F32), pltpu.VMEM((2, bq, 1), F32)],
        compiler_params=_cparams(("arbitrary", "arbitrary")),
    )(qkv, qkv, qkv, runs, do)


def _ssd_consts():
    ln = SSM_CHUNK
    ri = lax.broadcasted_iota(jnp.int32, (ln, ln), 0)
    ci = lax.broadcasted_iota(jnp.int32, (ln, ln), 1)
    eh = lax.broadcasted_iota(jnp.int32, (LANES, SSM_INNER), 0)
    el = lax.broadcasted_iota(jnp.int32, (LANES, SSM_INNER), 1)
    expand = (jnp.right_shift(el, 6) == eh).astype(BF16)
    th = lax.broadcasted_iota(jnp.int32, (SSM_INNER, LANES), 1)
    tl = lax.broadcasted_iota(jnp.int32, (SSM_INNER, LANES), 0)
    reduce = (jnp.right_shift(tl, 6) == th).astype(BF16)
    return ri, ci, expand, reduce


def _dot_f32(a, b):
    return jnp.dot(a, b, precision=HI, preferred_element_type=F32)


def _split3(x):
    hi = x.astype(BF16)
    r1 = x - hi.astype(F32)
    mid = r1.astype(BF16)
    lo = (r1 - mid.astype(F32)).astype(BF16)
    return hi, mid, lo


def _dot_hi(a, b):
    if a.dtype == BF16:
        return sum(jnp.dot(a, t, preferred_element_type=F32) for t in _split3(b))
    return sum(jnp.dot(t, b, preferred_element_type=F32) for t in _split3(a))


def _ssd_prelude(xbc_ref, dt_ref, dtt_ref, hpr_ref, hpc_ref, ri, ci, expand):
    ln = SSM_CHUNK
    xs = xbc_ref[:, 0:512]
    bm = xbc_ref[:, 512:640]
    cm = xbc_ref[:, 640:768]
    dtb_r = hpr_ref[0:1, :]
    aneg_r = -jnp.exp(hpr_ref[1:2, :])
    pre = dt_ref[...] + dtb_r
    dt = _softplus(pre)
    a = dt * aneg_r
    dtt = _softplus(dtt_ref[...] + hpc_ref[0:8, :])
    att = dtt * (-jnp.exp(hpc_ref[8:16, :]))
    tril = (ri >= ci).astype(BF16)
    triu = (ri <= ci).astype(BF16)
    acs = _dot_hi(tril, a)
    acst = _dot_hi(att, triu)
    acs_e = _dot_hi(acs, expand)
    dt_e = _dot_hi(dt, expand)
    last_e = acs_e[ln - 1:ln, :]
    e_e = jnp.exp(acs_e)
    w_e = jnp.exp(last_e - acs_e)
    dec_e = jnp.exp(last_e)
    xdt = xs * dt_e
    return dict(xs=xs, bm=bm, cm=cm, pre=pre, dt=dt, aneg_r=aneg_r, acs=acs, acst=acst, dt_e=dt_e, e_e=e_e,
                w_e=w_e, dec_e=dec_e, xdt=xdt, triu=triu)


def ssd_fwd(act, p, dt32, dtt, hp_rows, hp_cols, d_e, norm_w, name):
    t = act.shape[0]
    ln = SSM_CHUNK
    nc = t // ln

    def body(xbc_ref, dt_ref, z_ref, dtt_ref, hpr_ref, hpc_ref, d_ref, nw_ref, yc_ref, y_ref, sto_ref, st):
        @pl.when(pl.program_id(0) == 0)
        def _():
            st[...] = jnp.zeros_like(st)

        ri, ci, expand, _ = _ssd_consts()
        q = _ssd_prelude(xbc_ref, dt_ref, dtt_ref, hpr_ref, hpc_ref, ri, ci, expand)
        lane = lax.broadcasted_iota(jnp.int32, (1, LANES), 1)
        rown = lax.broadcasted_iota(jnp.int32, (LANES, 1), 0)
        low = lane < 64
        mask = ri >= ci
        xdt_b = q["xdt"].astype(BF16)
        xw_b = (q["xdt"] * q["w_e"]).astype(BF16)
        bt = q["bm"].T.astype(BF16)
        cb_ = q["cm"].astype(BF16)
        y_pairs = []
        for g in range(2):
            gm = low if g == 0 else jnp.logical_not(low)
            rm = (rown < 64) if g == 0 else (rown >= 64)
            cg = jnp.where(gm, cb_, jnp.zeros_like(cb_))
            cb = jnp.dot(cg, bt, preferred_element_type=F32)
            for pp in range(2):
                pi = 2 * g + pp
                sl = slice(LANES * pi, LANES * (pi + 1))
                xp = xdt_b[:, sl]
                yd = []
                for hh in range(2):
                    h = 2 * pi + hh
                    diff = q["acs"][:, h:h + 1] - q["acst"][h:h + 1, :]
                    lam = jnp.exp(jnp.where(mask, diff, -jnp.inf))
                    yd.append(jnp.dot((cb * lam).astype(BF16), xp, preferred_element_type=F32))
                sp = st[pi]
                sto_ref[0, pi] = sp
                yoff = jnp.dot(cg, sp.astype(BF16), preferred_element_type=F32) * q["e_e"][:, sl]
                upd = jnp.dot(bt, xw_b[:, sl], preferred_element_type=F32)
                st[pi] = q["dec_e"][:, sl] * sp + jnp.where(rm, upd, 0.0)
                y_pairs.append(jnp.where(low, yd[0], yd[1]) + yoff)
        y = jnp.concatenate(y_pairs, axis=1) + q["xs"] * d_ref[...]
        y_ref[...] = y
        yg = y * _silu(z_ref[...].astype(F32))
        for g in range(2):
            sl = slice(256 * g, 256 * (g + 1))
            seg = yg[:, sl]
            yc_ref[:, sl] = (seg * _rstd(seg) * nw_ref[:, sl]).astype(BF16)

    return pl.pallas_call(
        body, name=name, grid=(nc,),
        in_specs=[pl.BlockSpec((ln, 768), lambda c: (c, 0)), pl.BlockSpec((ln, LANES), lambda c: (c, 0)),
                  pl.BlockSpec((ln, 512), lambda c: (c, OFF_Z // 512)), pl.BlockSpec((8, ln), lambda c: (0, c)),
                  pl.BlockSpec((8, LANES), lambda c: (0, 0)), pl.BlockSpec((16, ln), lambda c: (0, 0)),
                  pl.BlockSpec((1, 512), lambda c: (0, 0)), pl.BlockSpec((1, 512), lambda c: (0, 0))],
        out_specs=[pl.BlockSpec((ln, 512), lambda c: (c, 0)), pl.BlockSpec((ln, 512), lambda c: (c, 0)),
                   pl.BlockSpec((1, 4, LANES, LANES), lambda c: (c, 0, 0, 0))],
        out_shape=[jax.ShapeDtypeStruct((t, 512), BF16), jax.ShapeDtypeStruct((t, 512), F32),
                   jax.ShapeDtypeStruct((nc, 4, LANES, LANES), F32)],
        scratch_shapes=[pltpu.VMEM((4, LANES, LANES), F32)],
        compiler_params=_cparams(("arbitrary",)),
    )(act, dt32, p, dtt, hp_rows, hp_cols, d_e, norm_w)


def ssd_bwd(dyc, y, act, p, dt32, dtt, states, hp_rows, hp_cols, d_e, norm_w, name):
    t = act.shape[0]
    ln = SSM_CHUNK
    nc = t // ln

    def body(dyc_ref, y_ref, xbc_ref, dt_ref, z_ref, dtt_ref, st_ref, hpr_ref, hpc_ref, d_ref, nw_ref,
             dz_ref, dact_ref, ddt_ref, dnw_ref, dd_ref, dhp_ref, ds):
        @pl.when(pl.program_id(0) == 0)
        def _():
            ds[...] = jnp.zeros_like(ds)
            dnw_ref[...] = jnp.zeros_like(dnw_ref)
            dd_ref[...] = jnp.zeros_like(dd_ref)
            dhp_ref[...] = jnp.zeros_like(dhp_ref)

        ri, ci, expand, reduce = _ssd_consts()
        q = _ssd_prelude(xbc_ref, dt_ref, dtt_ref, hpr_ref, hpc_ref, ri, ci, expand)
        lane = lax.broadcasted_iota(jnp.int32, (1, LANES), 1)
        rown = lax.broadcasted_iota(jnp.int32, (LANES, 1), 0)
        low = lane < 64
        mask = ri >= ci
        mask_t = ci >= ri
        xs, xdt, acs, acst = q["xs"], q["xdt"], q["acs"], q["acst"]
        yv, zv, nw = y_ref[...], z_ref[...].astype(F32), nw_ref[...]
        sg = _sigmoid(zv)
        zz = zv * sg
        yg = yv * zz
        dycv = dyc_ref[...]
        u = dycv * nw
        dyg_parts, dnw_parts = [], []
        for g in range(2):
            sl = slice(256 * g, 256 * (g + 1))
            seg = yg[:, sl]
            rr = _rstd(seg)
            nrm = seg * rr
            dyg_parts.append(rr * (u[:, sl] - nrm * jnp.mean(nrm * u[:, sl], axis=-1, keepdims=True)))
            dnw_parts.append(jnp.sum(dycv[:, sl] * nrm, axis=0, keepdims=True))
        dyg = jnp.concatenate(dyg_parts, axis=1)
        dnw_ref[...] += jnp.concatenate(dnw_parts, axis=1)
        dy = dyg * zz
        dz_ref[...] = dyg * yv * (sg * (1.0 + zv * (1.0 - sg)))
        dd_ref[...] += jnp.sum(dy * xs, axis=0, keepdims=True)
        dxs = dy * d_ref[...]
        dy_b = dy.astype(BF16)
        xdt_b = xdt.astype(BF16)
        xw_b = (xdt * q["w_e"]).astype(BF16)
        bt = q["bm"].T.astype(BF16)
        ct = q["cm"].T.astype(BF16)
        cb_ = q["cm"].astype(BF16)
        bb_ = q["bm"].astype(BF16)
        dacs = jnp.zeros((ln, LANES), F32)
        dc = jnp.zeros((ln, LANES), F32)
        db = jnp.zeros((ln, LANES), F32)
        dxdt_pairs, yoffdy_pairs, dwe_pairs, ddec_pairs = [], [], [], []
        for g in range(2):
            gm = low if g == 0 else jnp.logical_not(low)
            rm = (rown < 64) if g == 0 else (rown >= 64)
            cg = jnp.where(gm, cb_, jnp.zeros_like(cb_))
            bg = jnp.where(gm, bb_, jnp.zeros_like(bb_))
            cb = jnp.dot(cg, bt, preferred_element_type=F32)
            cbt = jnp.dot(bg, ct, preferred_element_type=F32)
            dcb = jnp.zeros((ln, ln), F32)
            dcbt = jnp.zeros((ln, ln), F32)
            for pp in range(2):
                pi = 2 * g + pp
                sl = slice(LANES * pi, LANES * (pi + 1))
                xp = xdt_b[:, sl]
                dyp = dy_b[:, sl]
                xpt = xdt[:, sl].T.astype(BF16)
                dypt = dy[:, sl].T.astype(BF16)
                dxdt_p = jnp.zeros((ln, LANES), F32)
                for hh in range(2):
                    h = 2 * pi + hh
                    hm = low if hh == 0 else jnp.logical_not(low)
                    col = acs[:, h:h + 1]
                    row = acst[h:h + 1, :]
                    lam = jnp.exp(jnp.where(mask, col - row, -jnp.inf))
                    lam_t = jnp.exp(jnp.where(mask_t, row - col, -jnp.inf))
                    m = cb * lam
                    m_t = cbt * lam_t
                    dyh = jnp.where(hm, dyp, jnp.zeros_like(dyp))
                    xh = jnp.where(hm, xp, jnp.zeros_like(xp))
                    dm = jnp.dot(dyh, xpt, preferred_element_type=F32)
                    dm_t = jnp.dot(xh, dypt, preferred_element_type=F32)
                    dcb = dcb + dm * lam
                    dcbt = dcbt + dm_t * lam_t
                    rs = jnp.sum(dm * m, axis=1, keepdims=True) - jnp.sum(dm_t * m_t, axis=1, keepdims=True)
                    dacs = dacs + jnp.where(lane == h, rs, 0.0)
                    dxdt_p = dxdt_p + jnp.dot(m_t.astype(BF16), dyh, preferred_element_type=F32)
                sp = st_ref[0, pi]
                sp_b = sp.astype(BF16)
                dsn = ds[pi]
                dsn_b = dsn.astype(BF16)
                e_p, w_p, dec_p = q["e_e"][:, sl], q["w_e"][:, sl], q["dec_e"][:, sl]
                yoff = jnp.dot(cg, sp_b, preferred_element_type=F32) * e_p
                dyo = dy[:, sl] * e_p
                dyo_b = dyo.astype(BF16)
                dc = dc + lax.dot_general(dyo_b, sp_b, _NT, preferred_element_type=F32)
                ds_prev = dec_p * dsn + jnp.where(rm, jnp.dot(ct, dyo_b, preferred_element_type=F32), 0.0)
                yoffdy_pairs.append(dy[:, sl] * yoff)
                dxw = jnp.dot(bg, dsn_b, preferred_element_type=F32)
                db = db + lax.dot_general(xw_b[:, sl], dsn_b, _NT, preferred_element_type=F32)
                dxdt_p = dxdt_p + dxw * w_p
                dwe_pairs.append(dxw * xdt[:, sl])
                ddec_pairs.append(jnp.sum(dsn * sp, axis=0, keepdims=True))
                ds[pi] = ds_prev
                dxdt_pairs.append(dxdt_p)
            dc = dc + jnp.dot(dcb.astype(BF16), bg, preferred_element_type=F32)
            db = db + jnp.dot(dcbt.astype(BF16), cg, preferred_element_type=F32)
        dxdt = jnp.concatenate(dxdt_pairs, axis=1)
        yoffdy = jnp.concatenate(yoffdy_pairs, axis=1)
        dwe = jnp.concatenate(dwe_pairs, axis=1)
        ddec_e = jnp.broadcast_to(jnp.concatenate(ddec_pairs, axis=1), (8, SSM_INNER))
        last = acs[ln - 1:ln, :]
        w_col = jnp.exp(last - acs)
        dw_col = _dot_hi(dwe, reduce) * w_col
        dacs = dacs + _dot_hi(yoffdy, reduce) - dw_col
        dlast = jnp.sum(dw_col, axis=0, keepdims=True) + jnp.exp(last) * _dot_hi(ddec_e, reduce)[0:1, :]
        rowi = lax.broadcasted_iota(jnp.int32, (ln, 1), 0)
        dacs = dacs + jnp.where(rowi == ln - 1, dlast, 0.0)
        da = _dot_hi(q["triu"], dacs)
        ddt = da * q["aneg_r"] + _dot_hi(dxdt * xs, reduce)
        ddt_raw = jnp.where(lane < SSM_HEADS, ddt * _sigmoid(q["pre"]), 0.0)
        ddt_ref[...] = ddt_raw
        dhp_ref[0:1, :] += jnp.sum(ddt_raw, axis=0, keepdims=True)
        dhp_ref[1:2, :] += jnp.where(lane < SSM_HEADS, jnp.sum(da * q["dt"], axis=0, keepdims=True) * q["aneg_r"], 0.0)
        dact_ref[:, 0:512] = dxs + dxdt * q["dt_e"]
        dact_ref[:, 512:640] = db
        dact_ref[:, 640:768] = dc

    rev = lambda c: nc - 1 - c
    return pl.pallas_call(
        body, name=name, grid=(nc,),
        in_specs=[pl.BlockSpec((ln, 512), lambda c: (rev(c), 0)), pl.BlockSpec((ln, 512), lambda c: (rev(c), 0)),
                  pl.BlockSpec((ln, 768), lambda c: (rev(c), 0)),
                  pl.BlockSpec((ln, LANES), lambda c: (rev(c), 0)),
                  pl.BlockSpec((ln, 512), lambda c: (rev(c), OFF_Z // 512)), pl.BlockSpec((8, ln), lambda c: (0, rev(c))),
                  pl.BlockSpec((1, 4, LANES, LANES), lambda c: (rev(c), 0, 0, 0)),
                  pl.BlockSpec((8, LANES), lambda c: (0, 0)), pl.BlockSpec((16, ln), lambda c: (0, 0)),
                  pl.BlockSpec((1, 512), lambda c: (0, 0)), pl.BlockSpec((1, 512), lambda c: (0, 0))],
        out_specs=[pl.BlockSpec((ln, 512), lambda c: (rev(c), 0)), pl.BlockSpec((ln, 768), lambda c: (rev(c), 0)),
                   pl.BlockSpec((ln, LANES), lambda c: (rev(c), 0)), pl.BlockSpec((1, 512), lambda c: (0, 0)),
                   pl.BlockSpec((1, 512), lambda c: (0, 0)), pl.BlockSpec((8, LANES), lambda c: (0, 0))],
        out_shape=[jax.ShapeDtypeStruct((t, 512), F32), jax.ShapeDtypeStruct((t, 768), F32),
                   jax.ShapeDtypeStruct((t, LANES), F32), jax.ShapeDtypeStruct((1, 512), F32),
                   jax.ShapeDtypeStruct((1, 512), F32), jax.ShapeDtypeStruct((8, LANES), F32)],
        scratch_shapes=[pltpu.VMEM((4, LANES, LANES), F32)],
        compiler_params=_cparams(("arbitrary",)),
    )(dyc, y, act, dt32, p, dtt, states, hp_rows, hp_cols, d_e, norm_w)


def mod_shard_fwd(c_all, mod_w, mod_b_shard, name):
    def body(c_ref, w_ref, b_ref, o_ref):
        sc = _silu(c_ref[...])
        for l in range(DEPTH):
            o_ref[l] = _dot_f32(sc, w_ref[l]) + b_ref[l]

    return pl.pallas_call(body, name=name, out_shape=jax.ShapeDtypeStruct((DEPTH, N_DEV, mod_w.shape[2]), F32),
                          compiler_params=_cparams())(c_all, mod_w, mod_b_shard)


def mod_w_grad(c_all, dmod_shard, name):
    def body(c_ref, d_ref, o_ref):
        sc = _silu(c_ref[...])
        for l in range(DEPTH):
            o_ref[l] = lax.dot_general(sc, d_ref[l], _TN, precision=HI, preferred_element_type=F32)

    return pl.pallas_call(body, name=name, out_shape=jax.ShapeDtypeStruct((DEPTH, D_MODEL, dmod_shard.shape[2]), F32),
                          compiler_params=_cparams())(c_all, dmod_shard)


_BIG = ("w_in", "sc_conv_w", "ssm_conv_w", "w_sc_out", "w_sb_out", "w_ssm_out", "w_o", "w_ffn_in", "w_ffn_out")
_ROW_SHARDED = ("w_o", "w_ffn_out")
_CONV = ("sc_conv_w", "ssm_conv_w")
_SMALL = ("mod_b", "g_pre_mix", "g_post_mix", "g_pre_ffn", "g_post_ffn", "ssm_conv_b", "ssm_dt_bias", "ssm_a_log",
          "ssm_d", "ssm_norm_w")
_WEIGHTS = ("mod_w", "mod_b", "g_pre_mix", "g_post_mix", "g_pre_ffn", "g_post_ffn", "w_in", "sc_conv_w", "ssm_conv_w",
            "ssm_conv_b", "ssm_dt_bias", "ssm_a_log", "ssm_d", "ssm_norm_w", "w_sc_out", "w_sb_out", "w_ssm_out", "w_o",
            "w_ffn_in", "w_ffn_out")


def _gathered_to_full(g, row_sharded):
    _, dep, r, c = g.shape
    if row_sharded:
        return g.transpose(1, 0, 2, 3).reshape(dep, N_DEV * r, c)
    return g.transpose(1, 2, 0, 3).reshape(dep, r, N_DEV * c)


def _full_to_slots(w, row_sharded):
    dep, r, c = w.shape
    if row_sharded:
        return w.reshape(dep, N_DEV, r // N_DEV, c).transpose(1, 0, 2, 3).reshape(N_DEV, dep * (r // N_DEV), c)
    return w.reshape(dep, r, N_DEV, c // N_DEV).transpose(2, 0, 1, 3).reshape(N_DEV, dep * r, c // N_DEV)


def _pad_in_proj(w):
    sc, qkv, z, xbc, dt, gates = (w[:, 0:768], w[:, 768:1536], w[:, 1536:2048], w[:, 2048:2816], w[:, 2816:2824],
                                  w[:, 2824:5896])
    pad = jnp.zeros((w.shape[0], OFF_Z - OFF_DT - 8), w.dtype)
    return jnp.concatenate([gates, sc, qkv, xbc, dt, pad, z], axis=1)


def _unpad_in_proj(w):
    return jnp.concatenate([w[:, OFF_SC:OFF_SC + 768], w[:, OFF_QKV:OFF_QKV + 768], w[:, OFF_Z:OFF_Z + 512],
                            w[:, OFF_XBC:OFF_XBC + 768], w[:, OFF_DT:OFF_DT + 8], w[:, 0:3072]], axis=1)


def _row(v):
    return v.reshape(1, -1)


def _local_step(x, target, mod, small, conv, big):
    lw, saved = [], []
    for l in range(DEPTH):
        w_in_p = _pad_in_proj(big["w_in"][l])
        w_cat = jnp.concatenate([big["w_sc_out"][l], big["w_sb_out"][l], big["w_ssm_out"][l]], axis=0)
        hp_rows = jnp.zeros((8, LANES), F32).at[0, :SSM_HEADS].set(small["ssm_dt_bias"][l]).at[1, :SSM_HEADS].set(
            small["ssm_a_log"][l])
        hp_cols = jnp.concatenate([jnp.broadcast_to(small["ssm_dt_bias"][l][:, None], (SSM_HEADS, SSM_CHUNK)),
                                   jnp.broadcast_to(small["ssm_a_log"][l][:, None], (SSM_HEADS, SSM_CHUNK))], axis=0)
        lw.append(dict(
            w_in_p=w_in_p, w_cat=w_cat, w_o=big["w_o"][l], w_ffn_in=big["w_ffn_in"][l], w_ffn_out=big["w_ffn_out"][l],
            sc_w8=jnp.pad(conv["sc_conv_w"][l], ((0, 5), (0, 0))), ssm_w8=jnp.pad(conv["ssm_conv_w"][l], ((0, 4), (0, 0))),
            ssm_b=_row(small["ssm_conv_b"][l]), hp_rows=hp_rows, hp_cols=hp_cols,
            d_e=_row(jnp.repeat(small["ssm_d"][l], SSM_HEAD_DIM)), norm_w=_row(small["ssm_norm_w"][l]),
            g_pre_mix=_row(small["g_pre_mix"][l]), g_post_mix=_row(small["g_post_mix"][l]),
            g_pre_ffn=_row(small["g_pre_ffn"][l]), g_post_ffn=_row(small["g_post_ffn"][l]),
            shift1=mod[l, 0:1], scale1=mod[l, 1:2], gate1=mod[l, 2:3], shift2=mod[l, 3:4], scale2=mod[l, 4:5],
            gate2=mod[l, 5:6]))

    xl = x
    h = normmod_fwd(xl, lw[0]["g_pre_mix"], lw[0]["scale1"], lw[0]["shift1"], "normmod_fwd_0")
    dy = loss = None
    for l in range(DEPTH):
        w = lw[l]
        p = mm_nn([h], [w["w_in_p"]], BF16, f"in_proj_{l}")
        dt32 = mm_nn([h], [w["w_in_p"][:, OFF_DT:OFF_DT + LANES]], F32, f"in_proj_dt_{l}")
        ya, qkv, act = post_inproj(p, w["sc_w8"], w["ssm_w8"], w["ssm_b"], f"post_inproj_{l}")
        o, runs = sba_fwd(qkv, f"sba_fwd_{l}")
        dtt = dt32[:, :SSM_HEADS].T
        yc, ypre, states = ssd_fwd(act, p, dt32, dtt, w["hp_rows"], w["hp_cols"], w["d_e"], w["norm_w"], f"ssd_fwd_{l}")
        merged = branch_out_fwd(ya, o, yc, p, w["w_cat"], f"branch_fwd_{l}")
        mix = mm_nn([merged], [w["w_o"]], F32, f"out_proj_{l}")
        x1, h2 = resid_normmod_fwd(xl, mix, w["gate1"], w["g_post_mix"], w["g_pre_ffn"], w["scale2"], w["shift2"],
                                   f"resid_mix_{l}")
        gt, up, a = mm_swiglu_fwd(h2, w["w_ffn_in"], f"ffn_in_{l}")
        f = mm_nn([a], [w["w_ffn_out"]], F32, f"ffn_out_{l}")
        saved.append(dict(x=xl, h=h, p=p, ya=ya, qkv=qkv, act=act, o=o, runs=runs, dt32=dt32, dtt=dtt, yc=yc, ypre=ypre, states=states,
                          merged=merged, mix=mix, x1=x1, h2=h2, gt=gt, up=up, a=a, f=f))
        if l + 1 < DEPTH:
            nw = lw[l + 1]
            xl, h = resid_normmod_fwd(x1, f, w["gate2"], w["g_post_ffn"], nw["g_pre_mix"], nw["scale1"], nw["shift1"],
                                      f"resid_ffn_{l}")
        else:
            dy, loss = resid_loss(x1, f, w["gate2"], w["g_post_ffn"], target, "resid_loss")

    dmod = [None] * DEPTH
    gs = {k: [None] * DEPTH for k in _SMALL + _BIG}
    dxo = dy
    for l in reversed(range(DEPTH)):
        w, s = lw[l], saved[l]
        df, dgate2, gs["g_post_ffn"][l] = resid_bwd(dxo, s["f"], w["gate2"], w["g_post_ffn"], f"resid_ffn_bwd_{l}")
        dgt, dup = mm_swiglu_bwd(df, w["w_ffn_out"], s["gt"], s["up"], f"ffn_out_bwd_{l}")
        gs["w_ffn_out"][l] = mm_tn(s["a"], df, f"dw_ffn_out_{l}")
        dh2 = mm_nt([dgt, dup], [w["w_ffn_in"], w["w_ffn_in"]], [0, FFN_HIDDEN], F32, f"ffn_in_bwd_{l}")
        gs["w_ffn_in"][l] = jnp.concatenate([mm_tn(s["h2"], dgt, f"dw_ffn_gate_{l}"), mm_tn(s["h2"], dup, f"dw_ffn_up_{l}")],
                                            axis=1)
        dx1, dscale2, dshift2, gs["g_pre_ffn"][l] = normmod_bwd(dh2, s["x1"], dxo, w["g_pre_ffn"], w["scale2"],
                                                                f"normmod_ffn_bwd_{l}")
        dmix, dgate1, gs["g_post_mix"][l] = resid_bwd(dx1, s["mix"], w["gate1"], w["g_post_mix"], f"resid_mix_bwd_{l}")
        dmerged = mm_nt([dmix], [w["w_o"]], [0], F32, f"out_proj_bwd_{l}")
        gs["w_o"][l] = mm_tn(s["merged"], dmix, f"dw_o_{l}")
        dp_gates, dya, dyb, dyc, dw_cat = branch_out_bwd(dmerged, s["ya"], s["o"], s["yc"], s["p"], w["w_cat"],
                                                         f"branch_bwd_{l}")
        gs["w_sc_out"][l], gs["w_sb_out"][l], gs["w_ssm_out"][l] = dw_cat[0:256], dw_cat[256:512], dw_cat[512:1024]
        dz, dact, ddt, dnw, dd_e, dhp = ssd_bwd(dyc, s["ypre"], s["act"], s["p"], s["dt32"], s["dtt"], s["states"], w["hp_rows"],
                                                w["hp_cols"], w["d_e"], w["norm_w"], f"ssd_bwd_{l}")
        gs["ssm_norm_w"][l] = dnw[0]
        gs["ssm_d"][l] = dd_e.reshape(SSM_HEADS, SSM_HEAD_DIM).sum(axis=1)
        gs["ssm_dt_bias"][l] = dhp[0, :SSM_HEADS]
        gs["ssm_a_log"][l] = dhp[1, :SSM_HEADS]
        dq, dk, dv = sba_bwd(s["qkv"], s["runs"], dyb, f"sba_bwd_{l}")
        dp, dscw, dssw, dssb = assemble_dp(dp_gates, dya, s["p"], w["sc_w8"], dq, dk, dv, dact, w["ssm_w8"], w["ssm_b"], ddt,
                                           dz, f"assemble_dp_{l}")
        gs["sc_conv_w"][l], gs["ssm_conv_w"][l], gs["ssm_conv_b"][l] = dscw[0:3], dssw[0:4], dssb[0]
        dh = mm_nt([dp], [w["w_in_p"]], [0], F32, f"in_proj_bwd_{l}")
        gs["w_in"][l] = _unpad_in_proj(mm_tn(s["h"], dp, f"dw_in_{l}"))
        dxo, dscale1, dshift1, gs["g_pre_mix"][l] = normmod_bwd(dh, s["x"], dx1, w["g_pre_mix"], w["scale1"],
                                                                f"normmod_mix_bwd_{l}")
        dmod[l] = jnp.concatenate([dshift1, dscale1, dgate1, dshift2, dscale2, dgate2], axis=0)
    for k in ("g_pre_mix", "g_post_mix", "g_pre_ffn", "g_post_ffn"):
        gs[k] = [g[0] for g in gs[k]]
    grads = {k: jnp.stack(v) for k, v in gs.items() if k != "mod_b"}
    return loss[0, 0], dxo, jnp.stack(dmod), grads


def kernel(x, c, mod_w, mod_b, g_pre_mix, g_post_mix, g_pre_ffn, g_post_ffn, w_in, sc_conv_w, ssm_conv_w, ssm_conv_b, ssm_dt_bias, ssm_a_log, ssm_d, ssm_norm_w, w_sc_out, w_sb_out, w_ssm_out, w_o, w_ffn_in, w_ffn_out, loss_target, m_mod_w, m_mod_b, m_g_pre_mix, m_g_post_mix, m_g_pre_ffn, m_g_post_ffn, m_w_in, m_sc_conv_w, m_ssm_conv_w, m_ssm_conv_b, m_ssm_dt_bias, m_ssm_a_log, m_ssm_d, m_ssm_norm_w, m_w_sc_out, m_w_sb_out, m_w_ssm_out, m_w_o, m_w_ffn_in, m_w_ffn_out, v_mod_w, v_mod_b, v_g_pre_mix, v_g_post_mix, v_g_pre_ffn, v_g_post_ffn, v_w_in, v_sc_conv_w, v_ssm_conv_w, v_ssm_conv_b, v_ssm_dt_bias, v_ssm_a_log, v_ssm_d, v_ssm_norm_w, v_w_sc_out, v_w_sb_out, v_w_ssm_out, v_w_o, v_w_ffn_in, v_w_ffn_out):
    wts = dict(mod_w=mod_w, mod_b=mod_b, g_pre_mix=g_pre_mix, g_post_mix=g_post_mix, g_pre_ffn=g_pre_ffn,
               g_post_ffn=g_post_ffn, w_in=w_in, sc_conv_w=sc_conv_w, ssm_conv_w=ssm_conv_w, ssm_conv_b=ssm_conv_b,
               ssm_dt_bias=ssm_dt_bias, ssm_a_log=ssm_a_log, ssm_d=ssm_d, ssm_norm_w=ssm_norm_w, w_sc_out=w_sc_out,
               w_sb_out=w_sb_out, w_ssm_out=w_ssm_out, w_o=w_o, w_ffn_in=w_ffn_in, w_ffn_out=w_ffn_out)
    ms = dict(mod_w=m_mod_w, mod_b=m_mod_b, g_pre_mix=m_g_pre_mix, g_post_mix=m_g_post_mix, g_pre_ffn=m_g_pre_ffn,
              g_post_ffn=m_g_post_ffn, w_in=m_w_in, sc_conv_w=m_sc_conv_w, ssm_conv_w=m_ssm_conv_w,
              ssm_conv_b=m_ssm_conv_b, ssm_dt_bias=m_ssm_dt_bias, ssm_a_log=m_ssm_a_log, ssm_d=m_ssm_d,
              ssm_norm_w=m_ssm_norm_w, w_sc_out=m_w_sc_out, w_sb_out=m_w_sb_out, w_ssm_out=m_w_ssm_out, w_o=m_w_o,
              w_ffn_in=m_w_ffn_in, w_ffn_out=m_w_ffn_out)
    vs = dict(mod_w=v_mod_w, mod_b=v_mod_b, g_pre_mix=v_g_pre_mix, g_post_mix=v_g_post_mix, g_pre_ffn=v_g_pre_ffn,
              g_post_ffn=v_g_post_ffn, w_in=v_w_in, sc_conv_w=v_sc_conv_w, ssm_conv_w=v_ssm_conv_w,
              ssm_conv_b=v_ssm_conv_b, ssm_dt_bias=v_ssm_dt_bias, ssm_a_log=v_ssm_a_log, ssm_d=v_ssm_d,
              ssm_norm_w=v_ssm_norm_w, w_sc_out=v_w_sc_out, w_sb_out=v_w_sb_out, w_ssm_out=v_w_ssm_out, w_o=v_w_o,
              w_ffn_in=v_w_ffn_in, w_ffn_out=v_w_ffn_out)
    me = 4 * lax.axis_index("x") + 2 * lax.axis_index("y") + lax.axis_index("c")
    mod_cols = mod_w.shape[2]

    pack1, sizes1 = _pack_rows([c, sc_conv_w, ssm_conv_w], F32, 8)
    got1 = all_gather_rows(pack1, "gather_c_conv").reshape(N_DEV, -1)
    c_all, sc_g, ssm_g = _unpack(got1, sizes1, [(D_MODEL,), sc_conv_w.shape, ssm_conv_w.shape])
    conv = dict(sc_conv_w=_gathered_to_full(sc_g, False), ssm_conv_w=_gathered_to_full(ssm_g, False))

    mod_b_shard = lax.dynamic_slice_in_dim(mod_b, me * mod_cols, mod_cols, axis=1).reshape(DEPTH, 1, mod_cols)
    mod_sh = mod_shard_fwd(c_all, mod_w, mod_b_shard, "mod_shard_fwd")
    pack2, sizes2 = _pack_rows([mod_sh], F32, 8)
    got2 = all_gather_rows(pack2, "gather_mod").reshape(N_DEV, -1)
    mod_all = _unpack(got2, sizes2, [mod_sh.shape])[0]
    mod_mine = lax.dynamic_index_in_dim(mod_all, me, axis=2, keepdims=False)
    mod = mod_mine.transpose(1, 0, 2).reshape(DEPTH, 6, D_MODEL)

    mm_names = [k for k in _BIG if k not in _CONV]
    gathered = all_gather_multi([wts[k].astype(BF16) for k in mm_names], "gather_weights")
    big = {k: _gathered_to_full(g, k in _ROW_SHARDED) for k, g in zip(mm_names, gathered)}

    small = {k: wts[k] for k in _SMALL}
    loss_part, dx, dmod, grads = _local_step(x[0], loss_target[0], mod, small, conv, big)
    loss = lax.psum(loss_part, ("x", "y", "c"))

    small_parts = [dmod.reshape(DEPTH, 6 * D_MODEL)] + [grads[k] for k in _SMALL[1:]]
    pack5, sizes5 = _pack_rows(small_parts, F32, 8)
    pack_conv, sizes_conv = _pack_rows([grads[k] for k in _CONV], F32, 8)
    got5, got_conv = all_gather_multi([pack5, pack_conv], "gather_small_grads")
    w5, _ = _pack_rows([wts[k] for k in _SMALL], F32, 8)
    m5, _ = _pack_rows([ms[k] for k in _SMALL], F32, 8)
    v5, _ = _pack_rows([vs[k] for k in _SMALL], F32, 8)
    res5 = adamw_flat(got5, w5, m5, v5, "adamw_small")
    small_out = [_unpack(r.reshape(-1), sizes5, [wts[k].shape for k in _SMALL]) for r in res5]

    conv_full = _unpack(got_conv.reshape(N_DEV, -1), sizes_conv, [grads[k].shape for k in _CONV])
    conv_mine = [lax.dynamic_slice_in_dim(g, me * wts[k].shape[2], wts[k].shape[2], axis=3)
                 for k, g in zip(_CONV, conv_full)]
    conv_slot_sizes = [math.prod(wts[k].shape) for k in _CONV]
    conv_slots = jnp.concatenate([g.reshape(N_DEV, -1) for g in conv_mine], axis=1)
    pad_c = -conv_slots.shape[1] % (8 * LANES)
    conv_slots = jnp.pad(conv_slots, ((0, 0), (0, pad_c))).reshape(N_DEV, -1, LANES)
    wc, _ = _pack_rows([wts[k] for k in _CONV], F32, 8)
    mc, _ = _pack_rows([ms[k] for k in _CONV], F32, 8)
    vc, _ = _pack_rows([vs[k] for k in _CONV], F32, 8)
    res_c = adamw_flat(conv_slots, wc, mc, vc, "adamw_conv")
    conv_out = [_unpack(r.reshape(-1), conv_slot_sizes, [wts[k].shape for k in _CONV]) for r in res_c]

    dmod_all = got5.reshape(N_DEV, -1)[:, :DEPTH * 6 * D_MODEL].reshape(N_DEV, DEPTH, 6 * D_MODEL)
    dmod_shard = lax.dynamic_slice_in_dim(dmod_all, me * mod_cols, mod_cols, axis=2).transpose(1, 0, 2)
    g_mod_w = mod_w_grad(c_all, dmod_shard, "mod_w_grad")
    rows2 = lambda a: a.reshape(a.shape[0] * a.shape[1], a.shape[2])
    res_mw = adamw_flat(rows2(g_mod_w)[None], rows2(mod_w), rows2(m_mod_w), rows2(v_mod_w), "adamw_mod_w")
    mod_w_out = [r.reshape(mod_w.shape) for r in res_mw]

    cidx = lax.axis_index("c")
    keeps, gives = [], []
    for k in mm_names:
        slots = _full_to_slots(grads[k].astype(BF16), k in _ROW_SHARDED)
        by_chip = slots.reshape(4, 2, *slots.shape[1:])
        keeps.append(lax.dynamic_index_in_dim(by_chip, cidx, 1, keepdims=False))
        gives.append(lax.dynamic_index_in_dim(by_chip, 1 - cidx, 1, keepdims=False))
    gots = swap_with_sibling(gives, "swap_grads")
    pairs = []
    for k, keep, got in zip(mm_names, keeps, gots):
        rows4 = (4 * keep.shape[1], keep.shape[2])
        pairs.append(add_pairs(keep.reshape(rows4), got.reshape(rows4), f"add_pairs_{k}").reshape(keep.shape))
    recvs = exchange_chips(pairs, "exchange_grads")
    big_out = {}
    for k, recv in zip(mm_names, recvs):
        res = adamw_flat(recv, rows2(wts[k]), rows2(ms[k]), rows2(vs[k]), f"adamw_{k}")
        big_out[k] = [r.reshape(wts[k].shape) for r in res]

    outs = []
    for kind in range(4):
        by_name = {"mod_w": mod_w_out[kind]}
        by_name.update(zip(_SMALL, small_out[kind]))
        by_name.update(zip(_CONV, conv_out[kind]))
        by_name.update({k: v[kind] for k, v in big_out.items()})
        outs.extend(by_name[k] for k in _WEIGHTS)
    return (loss, dx[None], *outs)
```

```python
import math

import jax
import jax.numpy as jnp
from jax import lax
from jax.experimental import pallas as pl
from jax.experimental.pallas import tpu as pltpu

F32 = jnp.float32
BF16 = jnp.bfloat16
HI = lax.Precision.HIGHEST

N_DEV = 8
D_MODEL = 1024
DEPTH = 2
SC_WIDTH = 256
SB_WIDTH = 256
SB_HEAD_DIM = 64
SSM_INNER = 512
SSM_HEADS = 8
SSM_HEAD_DIM = 64
SSM_GROUPS = 2
SSM_STATE = 64
SSM_CHUNK = 256
SSM_CONV_DIM = 768
FFN_HIDDEN = 2816
NORM_EPS = 1e-6
IN_PROJ = 5896
LANES = 128
VMEM_LIMIT = 56 * 1024 * 1024

OFF_GATES = 0
OFF_SC = 3072
OFF_QKV = 3840
OFF_XBC = 4608
OFF_DT = 5376
OFF_Z = 5632
IN_PAD = 6144

ADAM_LR = 0.001
ADAM_B1 = 0.9
ADAM_B2 = 0.999
ADAM_EPS = 1e-08
ADAM_WD = 0.01
ADAM_STEP = 10

MESH_ID = pl.DeviceIdType.MESH


def _cparams(sem=None):
    return pltpu.CompilerParams(dimension_semantics=sem, vmem_limit_bytes=VMEM_LIMIT)


def _my_pos():
    return lax.axis_index("x"), lax.axis_index("y"), lax.axis_index("c")


def all_gather_multi(blocks, name):
    n = len(blocks)

    def body(*refs):
        x_refs, o_refs = refs[:n], refs[n:2 * n]
        send_sems, recv_sems, local_sems = refs[2 * n:]
        x, y, c = _my_pos()
        me, sibling = (x, y, c), (x, y, 1 - c)
        chips = [(1 - x, y), (x, 1 - y), (1 - x, 1 - y)]

        def slot(a, px, py, pc):
            return o_refs[a].at[4 * px + 2 * py + pc]

        def copy(a, k, blk, to, src=None):
            return pltpu.make_async_remote_copy(
                src_ref=slot(a, *blk) if src is None else src, dst_ref=slot(a, *blk),
                send_sem=send_sems.at[7 * a + k], recv_sem=recv_sems.at[7 * a + k], device_id=to, device_id_type=MESH_ID)

        mine = [pltpu.make_async_copy(x_refs[a], slot(a, *me), local_sems.at[a]) for a in range(n)]
        for cp in mine:
            cp.start()
        first = [copy(a, 1 + j, me, (*chip, c), src=x_refs[a]) for j, chip in enumerate(chips) for a in range(n)]
        first += [copy(a, 0, me, sibling, src=x_refs[a]) for a in range(n)]
        for cp in first:
            cp.start()
        passed = []
        for j, chip in enumerate(chips):
            for a in range(n):
                copy(a, 1 + j, (*chip, c), me).wait_recv()
                fwd = copy(a, 4 + j, (*chip, c), sibling)
                fwd.start()
                passed.append(fwd)
        for a in range(n):
            copy(a, 0, sibling, me).wait_recv()
            for j, chip in enumerate(chips):
                copy(a, 4 + j, (*chip, 1 - c), me).wait_recv()
        for cp in first + passed:
            cp.wait_send()
        for cp in mine:
            cp.wait()

    any_spec = pl.BlockSpec(memory_space=pl.ANY)
    return pl.pallas_call(
        body, name=name,
        out_shape=[jax.ShapeDtypeStruct((N_DEV,) + b.shape, b.dtype) for b in blocks],
        in_specs=[any_spec] * n, out_specs=[any_spec] * n,
        scratch_shapes=[pltpu.SemaphoreType.DMA((7 * n,)), pltpu.SemaphoreType.DMA((7 * n,)), pltpu.SemaphoreType.DMA((n,))],
    )(*blocks)


def swap_with_sibling(gives, name):
    n = len(gives)

    def body(*refs):
        g_refs, r_refs = refs[:n], refs[n:2 * n]
        send_sems, recv_sems = refs[2 * n:]
        x, y, c = _my_pos()
        copies = [pltpu.make_async_remote_copy(
            src_ref=g_refs[a], dst_ref=r_refs[a], send_sem=send_sems.at[a], recv_sem=recv_sems.at[a],
            device_id=(x, y, 1 - c), device_id_type=MESH_ID) for a in range(n)]
        for cp in copies:
            cp.start()
        for cp in copies:
            cp.wait_recv()
        for cp in copies:
            cp.wait_send()

    any_spec = pl.BlockSpec(memory_space=pl.ANY)
    return pl.pallas_call(
        body, name=name, out_shape=[jax.ShapeDtypeStruct(g.shape, g.dtype) for g in gives],
        in_specs=[any_spec] * n, out_specs=[any_spec] * n,
        scratch_shapes=[pltpu.SemaphoreType.DMA((n,)), pltpu.SemaphoreType.DMA((n,))],
    )(*gives)


def exchange_chips(sends, name):
    n = len(sends)

    def body(*refs):
        s_refs, r_refs = refs[:n], refs[n:2 * n]
        send_sems, recv_sems, local_sems = refs[2 * n:]
        x, y, c = _my_pos()
        me = 2 * x + y
        mine = [pltpu.make_async_copy(s_refs[a].at[me], r_refs[a].at[me], local_sems.at[a]) for a in range(n)]
        for cp in mine:
            cp.start()
        copies = []
        for k in (2, 1, 3):
            px, py = x ^ (k >> 1), y ^ (k & 1)
            for a in range(n):
                cp = pltpu.make_async_remote_copy(
                    src_ref=s_refs[a].at[2 * px + py], dst_ref=r_refs[a].at[me],
                    send_sem=send_sems.at[3 * a + k - 1], recv_sem=recv_sems.at[3 * a + k - 1],
                    device_id=(px, py, c), device_id_type=MESH_ID)
                cp.start()
                copies.append(cp)
        for cp in copies:
            cp.wait_recv()
        for cp in copies:
            cp.wait_send()
        for cp in mine:
            cp.wait()

    any_spec = pl.BlockSpec(memory_space=pl.ANY)
    return pl.pallas_call(
        body, name=name, out_shape=[jax.ShapeDtypeStruct(s.shape, s.dtype) for s in sends],
        in_specs=[any_spec] * n, out_specs=[any_spec] * n,
        scratch_shapes=[pltpu.SemaphoreType.DMA((3 * n,)), pltpu.SemaphoreType.DMA((3 * n,)), pltpu.SemaphoreType.DMA((n,))],
    )(*sends)


def add_pairs(a, b, name, tr=512):
    rows, cols = a.shape
    tr = max(d for d in range(16, min(tr, rows) + 1, 16) if rows % d == 0)

    def body(a_ref, b_ref, o_ref):
        o_ref[...] = (a_ref[...].astype(F32) + b_ref[...].astype(F32)).astype(BF16)

    tile = pl.BlockSpec((tr, cols), lambda i: (i, 0))
    return pl.pallas_call(body, name=name, grid=(rows // tr,), in_specs=[tile, tile], out_specs=tile,
                          out_shape=jax.ShapeDtypeStruct((rows, cols), BF16), compiler_params=_cparams(("parallel",)))(a, b)


def _pack_rows(parts, dtype, row_multiple):
    flat = [p.astype(dtype).reshape(-1) for p in parts]
    sizes = [f.shape[0] for f in flat]
    total = sum(sizes)
    quantum = LANES * row_multiple
    padded = -(-total // quantum) * quantum
    if padded > total:
        flat.append(jnp.zeros((padded - total,), dtype))
    return jnp.concatenate(flat).reshape(padded // LANES, LANES), sizes


def _unpack(flat, sizes, shapes):
    out, off = [], 0
    lead = flat.shape[:-1]
    for n, shp in zip(sizes, shapes):
        out.append(flat[..., off:off + n].reshape(lead + tuple(shp)))
        off += n
    return out


def rows_call(name, body, n_rows, tr, ins, outs, scratch=(), aliases=None):
    n_tiles = n_rows // tr
    assert n_tiles * tr == n_rows
    in_specs, arrays = [], []
    for arr, kind in ins:
        arrays.append(arr)
        if kind == "row":
            in_specs.append(pl.BlockSpec((tr, arr.shape[1]), lambda i: (i, 0)))
        elif kind == "any":
            in_specs.append(pl.BlockSpec(memory_space=pl.ANY))
        elif kind == "full":
            in_specs.append(pl.BlockSpec(arr.shape, lambda i, nd=arr.ndim: (0,) * nd))
        elif kind[0] == "row":
            _, w, ci = kind
            in_specs.append(pl.BlockSpec((tr, w), lambda i, ci=ci: (i, ci)))
        elif kind[0] == "prev8":
            _, w, ci = kind
            hr = 8 * (4 // arr.dtype.itemsize)
            in_specs.append(pl.BlockSpec((hr, w), lambda i, ci=ci, hr=hr: (jnp.maximum(i * (tr // hr) - 1, 0), ci)))
        elif kind[0] == "next8":
            _, w, ci = kind
            hr = 8 * (4 // arr.dtype.itemsize)
            last = n_rows // hr - 1
            in_specs.append(pl.BlockSpec((hr, w), lambda i, ci=ci, last=last, hr=hr: (jnp.minimum((i + 1) * (tr // hr), last), ci)))
        else:
            raise ValueError(kind)
    out_specs, out_shapes = [], []
    for shape, dtype, kind in outs:
        out_shapes.append(jax.ShapeDtypeStruct(shape, dtype))
        if kind == "row":
            out_specs.append(pl.BlockSpec((tr, shape[1]), lambda i: (i, 0)))
        elif kind[0] == "row":
            _, w, ci = kind
            out_specs.append(pl.BlockSpec((tr, w), lambda i, ci=ci: (i, ci)))
        else:
            out_specs.append(pl.BlockSpec(shape, lambda i, nd=len(shape): (0,) * nd))
    has_acc = any(k == "acc" for _, _, k in outs)
    return pl.pallas_call(
        body, name=name, grid=(n_tiles,), in_specs=in_specs, out_specs=out_specs, out_shape=out_shapes,
        scratch_shapes=list(scratch), input_output_aliases=dict(aliases or {}),
        compiler_params=_cparams(("arbitrary",) if has_acc else ("parallel",)),
    )(*arrays)


def _prev8(ref):
    n = ref.shape[0]
    return ref[n - 8:n, :].astype(F32)


def _next8(ref):
    return ref[0:8, :].astype(F32)


def _acc(ref, val):
    @pl.when(pl.program_id(0) == 0)
    def _():
        ref[...] = jnp.zeros_like(ref)
    ref[...] += val


def _rstd(x):
    return lax.rsqrt(jnp.mean(x * x, axis=-1, keepdims=True) + NORM_EPS)


def _sigmoid(x):
    return 1.0 / (1.0 + jnp.exp(-x))


def _silu(x):
    return x * _sigmoid(x)


def _dsilu(x):
    s = _sigmoid(x)
    return s * (1.0 + x * (1.0 - s))


def _softplus(x):
    return jnp.maximum(x, 0.0) + jnp.log(1.0 + jnp.exp(-jnp.abs(x)))


def _log_sigmoid_neg(x):
    t = -x
    return jnp.minimum(t, 0.0) - jnp.log(1.0 + jnp.exp(jnp.minimum(x, t)))


def normmod_fwd(x, g, scale, shift, name):
    t, d = x.shape

    def body(x_ref, g_ref, sc_ref, sh_ref, h_ref):
        xv = x_ref[...]
        h = xv * _rstd(xv) * g_ref[...] * (1.0 + sc_ref[...]) + sh_ref[...]
        h_ref[...] = h.astype(BF16)

    return rows_call(name, body, t, 512, [(x, "row"), (g, "full"), (scale, "full"), (shift, "full")],
                     [((t, d), BF16, "row")])[0]


def resid_normmod_fwd(x, f, gate, g_post, g_pre, scale, shift, name):
    t, d = x.shape

    def body(x_ref, f_ref, gate_ref, gp_ref, g_ref, sc_ref, sh_ref, xo_ref, h_ref):
        fv = f_ref[...]
        xn = x_ref[...] + gate_ref[...] * (fv * _rstd(fv) * gp_ref[...])
        xo_ref[...] = xn
        h = xn * _rstd(xn) * g_ref[...] * (1.0 + sc_ref[...]) + sh_ref[...]
        h_ref[...] = h.astype(BF16)

    return rows_call(name, body, t, 512,
                     [(x, "row"), (f, "row"), (gate, "full"), (g_post, "full"), (g_pre, "full"), (scale, "full"),
                      (shift, "full")],
                     [((t, d), F32, "row"), ((t, d), BF16, "row")])


def resid_loss(x, f, gate, g_post, target, name):
    t, d = x.shape

    def body(x_ref, f_ref, gate_ref, gp_ref, tg_ref, dy_ref, loss_ref):
        fv = f_ref[...]
        yv = x_ref[...] + gate_ref[...] * (fv * _rstd(fv) * gp_ref[...])
        err = yv - tg_ref[...]
        dy_ref[...] = err * (1.0 / d)
        part = 0.5 * jnp.sum(jnp.mean(err * err, axis=-1, keepdims=True), axis=0, keepdims=True)
        _acc(loss_ref, jnp.broadcast_to(part, loss_ref.shape))

    return rows_call(name, body, t, 512,
                     [(x, "row"), (f, "row"), (gate, "full"), (g_post, "full"), (target, "row")],
                     [((t, d), F32, "row"), ((8, LANES), F32, "acc")])


def resid_bwd(dx, f, gate, g_post, name):
    t, d = dx.shape

    def body(dx_ref, f_ref, gate_ref, gp_ref, df_ref, dgate_ref, dg_ref):
        fv, dxv, gp = f_ref[...], dx_ref[...], gp_ref[...]
        r = _rstd(fv)
        fn = fv * r
        _acc(dgate_ref, jnp.sum(dxv * (fn * gp), axis=0, keepdims=True))
        dn = dxv * gate_ref[...]
        _acc(dg_ref, jnp.sum(dn * fn, axis=0, keepdims=True))
        u = dn * gp
        df = r * (u - fn * jnp.mean(fn * u, axis=-1, keepdims=True))
        df_ref[...] = df.astype(BF16)

    return rows_call(name, body, t, 512, [(dx, "row"), (f, "row"), (gate, "full"), (g_post, "full")],
                     [((t, d), BF16, "row"), ((1, d), F32, "acc"), ((1, d), F32, "acc")])


def normmod_bwd(dh, x, dx_in, g, scale, name):
    t, d = x.shape

    def body(dh_ref, x_ref, dxi_ref, g_ref, sc_ref, dx_ref, dsc_ref, dsh_ref, dg_ref):
        xv, dhv, gv = x_ref[...], dh_ref[...], g_ref[...]
        r = _rstd(xv)
        xn = xv * r
        _acc(dsc_ref, jnp.sum(dhv * (xn * gv), axis=0, keepdims=True))
        _acc(dsh_ref, jnp.sum(dhv, axis=0, keepdims=True))
        dn = dhv * (1.0 + sc_ref[...])
        _acc(dg_ref, jnp.sum(dn * xn, axis=0, keepdims=True))
        u = dn * gv
        dx_ref[...] = dxi_ref[...] + r * (u - xn * jnp.mean(xn * u, axis=-1, keepdims=True))

    return rows_call(name, body, t, 512, [(dh, "row"), (x, "row"), (dx_in, "row"), (g, "full"), (scale, "full")],
                     [((t, d), F32, "row"), ((1, d), F32, "acc"), ((1, d), F32, "acc"), ((1, d), F32, "acc")])


def normmod_resid_bwd(dh, x, dx_in, g, scale, f, gate, g_post, name):
    t, d = x.shape

    def body(dh_ref, x_ref, dxi_ref, g_ref, sc_ref, f_ref, gate_ref, gp_ref,
             dx_ref, df_ref, dsc_ref, dsh_ref, dg_ref, dgate_ref, dgp_ref):
        xv, dhv, gv = x_ref[...], dh_ref[...], g_ref[...]
        r = _rstd(xv)
        xn = xv * r
        _acc(dsc_ref, jnp.sum(dhv * (xn * gv), axis=0, keepdims=True))
        _acc(dsh_ref, jnp.sum(dhv, axis=0, keepdims=True))
        dn = dhv * (1.0 + sc_ref[...])
        _acc(dg_ref, jnp.sum(dn * xn, axis=0, keepdims=True))
        u = dn * gv
        dxv = dxi_ref[...] + r * (u - xn * jnp.mean(xn * u, axis=-1, keepdims=True))
        dx_ref[...] = dxv
        fv, gp = f_ref[...], gp_ref[...]
        rf = _rstd(fv)
        fn = fv * rf
        _acc(dgate_ref, jnp.sum(dxv * (fn * gp), axis=0, keepdims=True))
        dnf = dxv * gate_ref[...]
        _acc(dgp_ref, jnp.sum(dnf * fn, axis=0, keepdims=True))
        uf = dnf * gp
        df_ref[...] = (rf * (uf - fn * jnp.mean(fn * uf, axis=-1, keepdims=True))).astype(BF16)

    vec = ((1, d), F32, "acc")
    return rows_call(name, body, t, 512,
                     [(dh, "row"), (x, "row"), (dx_in, "row"), (g, "full"), (scale, "full"), (f, "row"), (gate, "full"),
                      (g_post, "full")],
                     [((t, d), F32, "row"), ((t, d), BF16, "row"), vec, vec, vec, vec, vec])


def _pick(n, prefs):
    for p in prefs:
        if n % p == 0:
            return p
    return n


def mm_nn(a_list, b_list, out_dtype, name, tm=1024, tn=None, tk=None):
    m, k = a_list[0].shape
    n = b_list[0].shape[1]
    tm = min(tm, m)
    tn = tn or _pick(n, (1024, 768, 512, 256, 128))
    tk = tk or _pick(k, (1024, 1408, 512, 256))
    nk = k // tk
    npair = len(a_list)

    if nk == 1 and npair == 1:
        def body1(a_ref, b_ref, o_ref):
            o_ref[...] = jnp.dot(a_ref[...], b_ref[...], preferred_element_type=F32).astype(o_ref.dtype)

        return pl.pallas_call(
            body1, name=name, grid=(m // tm, n // tn),
            in_specs=[pl.BlockSpec((tm, k), lambda i, j: (i, 0)), pl.BlockSpec((k, tn), lambda i, j: (0, j))],
            out_specs=pl.BlockSpec((tm, tn), lambda i, j: (i, j)),
            out_shape=jax.ShapeDtypeStruct((m, n), out_dtype),
            compiler_params=_cparams(("parallel", "parallel")),
        )(a_list[0], b_list[0])

    def body(*refs):
        a_refs, b_refs = refs[:npair], refs[npair:2 * npair]
        o_ref, acc = refs[2 * npair], refs[2 * npair + 1]
        kk = pl.program_id(2)

        @pl.when(kk == 0)
        def _():
            acc[...] = jnp.zeros_like(acc)

        s = acc[...]
        for a_ref, b_ref in zip(a_refs, b_refs):
            s = s + jnp.dot(a_ref[...], b_ref[...], preferred_element_type=F32)
        acc[...] = s

        @pl.when(kk == nk - 1)
        def _():
            o_ref[...] = acc[...].astype(o_ref.dtype)

    return pl.pallas_call(
        body, name=name, grid=(m // tm, n // tn, nk),
        in_specs=[pl.BlockSpec((tm, tk), lambda i, j, kk: (i, kk))] * npair
        + [pl.BlockSpec((tk, tn), lambda i, j, kk: (kk, j))] * npair,
        out_specs=pl.BlockSpec((tm, tn), lambda i, j, kk: (i, j)),
        out_shape=jax.ShapeDtypeStruct((m, n), out_dtype),
        scratch_shapes=[pltpu.VMEM((tm, tn), F32)],
        compiler_params=_cparams(("parallel", "parallel", "arbitrary")),
    )(*a_list, *b_list)


def mm_nt(a_list, b_list, b_koff, out_dtype, name, tm=1024):
    m, k = a_list[0].shape
    n = b_list[0].shape[0]
    tm = min(tm, m)
    tn = _pick(n, (1024, 512, 256))
    tk = _pick(k, (1024, 1408, 512, 256))
    nk = k // tk
    npair = len(a_list)
    koff = [o // tk for o in b_koff]
    nt_dims = (((1,), (1,)), ((), ()))

    def body(*refs):
        a_refs, b_refs = refs[:npair], refs[npair:2 * npair]
        o_ref, acc = refs[2 * npair], refs[2 * npair + 1]
        kk = pl.program_id(2)

        @pl.when(kk == 0)
        def _():
            acc[...] = jnp.zeros_like(acc)

        s = acc[...]
        for a_ref, b_ref in zip(a_refs, b_refs):
            s = s + lax.dot_general(a_ref[...], b_ref[...], nt_dims, preferred_element_type=F32)
        acc[...] = s

        @pl.when(kk == nk - 1)
        def _():
            o_ref[...] = acc[...].astype(o_ref.dtype)

    return pl.pallas_call(
        body, name=name, grid=(m // tm, n // tn, nk),
        in_specs=[pl.BlockSpec((tm, tk), lambda i, j, kk: (i, kk))] * npair
        + [pl.BlockSpec((tn, tk), lambda i, j, kk, o=o: (j, kk + o)) for o in koff],
        out_specs=pl.BlockSpec((tm, tn), lambda i, j, kk: (i, j)),
        out_shape=jax.ShapeDtypeStruct((m, n), out_dtype),
        scratch_shapes=[pltpu.VMEM((tm, tn), F32)],
        compiler_params=_cparams(("parallel", "parallel", "arbitrary")),
    )(*a_list, *b_list)


def mm_tn(a, b, name, tt=512):
    t, ka = a.shape
    n = b.shape[1]
    ta = _pick(ka, (1024, 1408, 512, 256))
    tn = _pick(n, (2048, 1024, 1408, 512, 256))
    nt = t // tt

    def body(a_ref, b_ref, o_ref, acc):
        s = pl.program_id(2)

        @pl.when(s == 0)
        def _():
            acc[...] = jnp.zeros_like(acc)

        acc[...] += lax.dot_general(a_ref[...], b_ref[...], (((0,), (0,)), ((), ())), preferred_element_type=F32)

        @pl.when(s == nt - 1)
        def _():
            o_ref[...] = acc[...].astype(BF16)

    return pl.pallas_call(
        body, name=name, grid=(ka // ta, n // tn, nt),
        in_specs=[pl.BlockSpec((tt, ta), lambda i, j, s: (s, i)), pl.BlockSpec((tt, tn), lambda i, j, s: (s, j))],
        out_specs=pl.BlockSpec((ta, tn), lambda i, j, s: (i, j)),
        out_shape=jax.ShapeDtypeStruct((ka, n), BF16),
        scratch_shapes=[pltpu.VMEM((ta, tn), F32)],
        compiler_params=_cparams(("parallel", "parallel", "arbitrary")),
    )(a, b)


def mm_swiglu_fwd(h, w_ffn_in, name, tm=512, tn=1408):
    m, k = h.shape
    nh = FFN_HIDDEN // tn

    def body(h_ref, wg_ref, wu_ref, gt_ref, up_ref, a_ref):
        hv = h_ref[...]
        gt = jnp.dot(hv, wg_ref[...], preferred_element_type=F32)
        up = jnp.dot(hv, wu_ref[...], preferred_element_type=F32)
        gt_ref[...] = gt.astype(BF16)
        up_ref[...] = up.astype(BF16)
        a_ref[...] = (_silu(gt) * up).astype(BF16)

    shp = jax.ShapeDtypeStruct((m, FFN_HIDDEN), BF16)
    ospec = pl.BlockSpec((tm, tn), lambda i, j: (i, j))
    return pl.pallas_call(
        body, name=name, grid=(m // tm, nh),
        in_specs=[pl.BlockSpec((tm, k), lambda i, j: (i, 0)), pl.BlockSpec((k, tn), lambda i, j: (0, j)),
                  pl.BlockSpec((k, tn), lambda i, j: (0, j + nh))],
        out_specs=[ospec, ospec, ospec], out_shape=[shp, shp, shp],
        compiler_params=_cparams(("parallel", "parallel")),
    )(h, w_ffn_in, w_ffn_in)


def mm_swiglu_bwd(df, w_out, gt, up, name, tm=256, sub=256):
    m, k = df.shape
    n_sub = FFN_HIDDEN // sub

    def body(df_ref, w_ref, gt_ref, up_ref, dgt_ref, dup_ref):
        dfv = df_ref[...]

        def chunk_dot(c):
            return lax.dot_general(dfv, w_ref[c * sub:(c + 1) * sub, :], (((1,), (1,)), ((), ())), preferred_element_type=F32)

        da_next = chunk_dot(0)
        for c in range(n_sub):
            da = da_next
            if c + 1 < n_sub:
                da_next = chunk_dot(c + 1)
            cols = slice(c * sub, (c + 1) * sub)
            gtv = gt_ref[:, cols].astype(F32)
            sg = _sigmoid(gtv)
            dgt_ref[:, cols] = (da * up_ref[:, cols].astype(F32) * (sg * (1.0 + gtv * (1.0 - sg)))).astype(BF16)
            dup_ref[:, cols] = (da * (gtv * sg)).astype(BF16)

    shp = jax.ShapeDtypeStruct((m, FFN_HIDDEN), BF16)
    tile = pl.BlockSpec((tm, FFN_HIDDEN), lambda i: (i, 0))
    return pl.pallas_call(
        body, name=name, grid=(m // tm,),
        in_specs=[pl.BlockSpec((tm, k), lambda i: (i, 0)), pl.BlockSpec((FFN_HIDDEN, k), lambda i: (0, 0)), tile, tile],
        out_specs=[tile, tile], out_shape=[shp, shp],
        compiler_params=_cparams(("parallel",)),
    )(df, w_out, gt, up)


def _shift_down(x, prev8, j):
    if j == 0:
        return x
    xr = pltpu.roll(x, j, 0)
    pr = pltpu.roll(prev8, j, 0)
    row = lax.broadcasted_iota(jnp.int32, (8, x.shape[1]), 0)
    head = jnp.where(row < j, pr, xr[:8])
    return head if x.shape[0] == 8 else jnp.concatenate([head, xr[8:]], axis=0)


def _shift_up(x, next8, j):
    if j == 0:
        return x
    n = x.shape[0]
    xr = pltpu.roll(x, n - j, 0)
    nr = pltpu.roll(next8, 8 - j, 0)
    row = lax.broadcasted_iota(jnp.int32, (8, x.shape[1]), 0)
    return jnp.concatenate([xr[:n - 8], jnp.where(row >= 8 - j, nr, xr[n - 8:])], axis=0)


def _conv_taps(x, prev8, w_ref, taps):
    out = None
    for k in range(taps):
        term = w_ref[k:k + 1, :] * _shift_down(x, prev8, taps - 1 - k)
        out = term if out is None else out + term
    return out


def post_inproj(p, sc_w, ssm_w, ssm_b, name, tr=512):
    t = p.shape[0]

    def body(sc_ref, scp_ref, qkv_ref, xbc_ref, xbcp_ref, scw_ref, sw_ref, sb_ref, ya_ref, qkvo_ref, act_ref):
        first = (pl.program_id(0) > 0).astype(F32)
        sc = sc_ref[...].astype(F32)
        scp = _prev8(scp_ref) * first
        u = sc[:, 256:512] * sc[:, 512:768]
        up = scp[:, 256:512] * scp[:, 512:768]
        ya_ref[...] = (sc[:, 0:256] * _conv_taps(u, up, scw_ref, 3)).astype(BF16)
        qkv = qkv_ref[...]
        qkvo_ref[:, 0:256] = (qkv[:, 0:256].astype(F32) * 0.125).astype(BF16)
        qkvo_ref[:, 256:768] = qkv[:, 256:768].astype(BF16)
        xc = _conv_taps(xbc_ref[...].astype(F32), _prev8(xbcp_ref) * first, sw_ref, 4) + sb_ref[...]
        act_ref[...] = _silu(xc)

    return rows_call(
        name, body, t, tr,
        [(p, ("row", 768, OFF_SC // 768)), (p, ("prev8", 768, OFF_SC // 768)), (p, ("row", 768, OFF_QKV // 768)),
         (p, ("row", 768, OFF_XBC // 768)), (p, ("prev8", 768, OFF_XBC // 768)),
         (sc_w, "full"), (ssm_w, "full"), (ssm_b, "full")],
        [((t, 256), BF16, "row"), ((t, 768), BF16, "row"), ((t, 768), F32, "row")])


def branch_out_fwd(ya, yb, yc, p, w_cat, name, tr=256):
    t = p.shape[0]

    def body(ya_ref, yb_ref, yc_ref, gl_ref, w_ref, o_ref):
        y_a = jnp.dot(ya_ref[...], w_ref[0:256, :], preferred_element_type=F32)
        y_b = jnp.dot(yb_ref[...].astype(BF16), w_ref[256:512, :], preferred_element_type=F32)
        y_c = jnp.dot(yc_ref[...], w_ref[512:1024, :], preferred_element_type=F32)
        m = (_sigmoid(gl_ref[:, 0:1024].astype(F32)) * y_a + _sigmoid(gl_ref[:, 1024:2048].astype(F32)) * y_b
             + _sigmoid(gl_ref[:, 2048:3072].astype(F32)) * y_c)
        o_ref[...] = m.astype(BF16)

    return rows_call(name, body, t, tr,
                     [(ya, "row"), (yb, "row"), (yc, "row"), (p, ("row", 3072, 0)), (w_cat, "full")],
                     [((t, D_MODEL), BF16, "row")])[0]


def branch_out_bwd(dm, ya, yb, yc, p, w_cat, name, tr=256):
    t = p.shape[0]
    tn_dims = (((0,), (0,)), ((), ()))

    def body(dm_ref, ya_ref, yb_ref, yc_ref, gl_ref, w_ref, dgl_ref, dya_ref, dyb_ref, dyc_ref, dw_ref):
        @pl.when(pl.program_id(0) == 0)
        def _():
            dw_ref[...] = jnp.zeros_like(dw_ref)

        dmv = dm_ref[...]
        ins = (ya_ref[...], yb_ref[...].astype(BF16), yc_ref[...])
        rows = ((0, 256), (256, 512), (512, 1024))
        outs = (dya_ref, dyb_ref, dyc_ref)
        for i in range(3):
            r0, r1 = rows[i]
            y = jnp.dot(ins[i], w_ref[r0:r1, :], preferred_element_type=F32)
            s = _sigmoid(gl_ref[:, 1024 * i:1024 * (i + 1)].astype(F32))
            dgl_ref[:, 1024 * i:1024 * (i + 1)] = (dmv * y * s * (1.0 - s)).astype(BF16)
            dy = (dmv * s).astype(BF16)
            outs[i][...] = lax.dot_general(dy, w_ref[r0:r1, :], _NT, preferred_element_type=F32)
            dw_ref[r0:r1, :] += lax.dot_general(ins[i], dy, tn_dims, preferred_element_type=F32)

    return rows_call(name, body, t, tr,
                     [(dm, "row"), (ya, "row"), (yb, "row"), (yc, "row"), (p, ("row", 3072, 0)), (w_cat, "full")],
                     [((t, IN_PAD), BF16, ("row", 3072, 0)), ((t, 256), F32, "row"), ((t, 256), F32, "row"),
                      ((t, 512), F32, "row"), ((D_MODEL, D_MODEL), F32, "acc")])


def assemble_dp(dp, dya, p, sc_w, dq, dk, dv, dact, ssm_w, ssm_b, ddt, dz, name, tr=256):
    t = p.shape[0]
    n_tiles = t // tr
    sci, xi = OFF_SC // 768, OFF_XBC // 768
    base = OFF_SC
    assert base == IN_PAD - base
    o_sc, o_qkv, o_xbc, o_dt, o_z, o_end = (c - base for c in (OFF_SC, OFF_QKV, OFF_XBC, OFF_DT, OFF_Z, IN_PAD))

    def body(dp_ref, dya_ref, dyan_ref, sc_ref, scp_ref, scn_ref, scw_ref, dq_ref, dk_ref, dv_ref,
             dact_ref, dactn_ref, xbc_ref, xbcp_ref, xbcn_ref, sw_ref, sb_ref, ddt_ref, dz_ref,
             o_ref, dscw_ref, dsw_ref, dsb_ref):
        i = pl.program_id(0)

        @pl.when(i == 0)
        def _():
            dscw_ref[...] = jnp.zeros_like(dscw_ref)
            dsw_ref[...] = jnp.zeros_like(dsw_ref)
            dsb_ref[...] = jnp.zeros_like(dsb_ref)

        first = (i > 0).astype(F32)
        last = (i < n_tiles - 1).astype(F32)
        del dp_ref
        sc = sc_ref[...].astype(F32)
        scp = _prev8(scp_ref) * first
        scn = _next8(scn_ref) * last
        u = sc[:, 256:512] * sc[:, 512:768]
        up = scp[:, 256:512] * scp[:, 512:768]
        dya_v = dya_ref[...]
        cv = _conv_taps(u, up, scw_ref, 3)
        o_ref[:, o_sc:o_sc + 256] = (dya_v * cv).astype(BF16)
        dcv = dya_v * sc[:, 0:256]
        dcvn = _next8(dyan_ref) * last * scn[:, 0:256]
        du = None
        for k in range(3):
            sh = 2 - k
            term = scw_ref[k:k + 1, :] * _shift_up(dcv, dcvn, sh)
            du = term if du is None else du + term
            dscw_ref[k:k + 1, :] += jnp.sum(dcv * _shift_down(u, up, sh), axis=0, keepdims=True)
        o_ref[:, o_sc + 256:o_sc + 512] = (du * sc[:, 512:768]).astype(BF16)
        o_ref[:, o_sc + 512:o_sc + 768] = (du * sc[:, 256:512]).astype(BF16)
        o_ref[:, o_qkv:o_qkv + 256] = (dq_ref[...] * 0.125).astype(BF16)
        o_ref[:, o_qkv + 256:o_qkv + 512] = dk_ref[...].astype(BF16)
        o_ref[:, o_qkv + 512:o_qkv + 768] = dv_ref[...].astype(BF16)
        xb = xbc_ref[...].astype(F32)
        xbp = _prev8(xbcp_ref) * first
        xbn = _next8(xbcn_ref)
        xc = _conv_taps(xb, xbp, sw_ref, 4) + sb_ref[...]
        xcn = _conv_taps(xbn, xb[tr - 8:, :], sw_ref, 4) + sb_ref[...]
        dxc = dact_ref[...] * _dsilu(xc)
        dxcn = _next8(dactn_ref) * _dsilu(xcn) * last
        dxb = None
        for k in range(4):
            sh = 3 - k
            term = sw_ref[k:k + 1, :] * _shift_up(dxc, dxcn, sh)
            dxb = term if dxb is None else dxb + term
            dsw_ref[k:k + 1, :] += jnp.sum(dxc * _shift_down(xb, xbp, sh), axis=0, keepdims=True)
        dsb_ref[...] += jnp.sum(dxc, axis=0, keepdims=True)
        o_ref[:, o_xbc:o_xbc + 768] = dxb.astype(BF16)
        o_ref[:, o_dt:o_dt + 128] = ddt_ref[...].astype(BF16)
        o_ref[:, o_dt + 128:o_z] = jnp.zeros((tr, o_z - o_dt - 128), BF16)
        o_ref[:, o_z:o_end] = dz_ref[...].astype(BF16)

    return rows_call(
        name, body, t, tr,
        [(dp, "any"), (dya, "row"), (dya, ("next8", 256, 0)),
         (p, ("row", 768, sci)), (p, ("prev8", 768, sci)), (p, ("next8", 768, sci)), (sc_w, "full"),
         (dq, "row"), (dk, "row"), (dv, "row"),
         (dact, "row"), (dact, ("next8", 768, 0)),
         (p, ("row", 768, xi)), (p, ("prev8", 768, xi)), (p, ("next8", 768, xi)), (ssm_w, "full"), (ssm_b, "full"),
         (ddt, "row"), (dz, "row")],
        [((t, IN_PAD), BF16, ("row", IN_PAD - base, 1)), ((8, 256), F32, "acc"), ((8, 768), F32, "acc"),
         ((1, 768), F32, "acc")], aliases={0: 0})


def adamw_flat(slots, w, m, v, name, tr=512):
    n_slots, rows, lanes = slots.shape
    tr = max(d for d in range(8, min(tr, rows) + 1, 8) if rows % d == 0) if rows % 8 == 0 else rows
    bc1 = 1.0 - ADAM_B1 ** ADAM_STEP
    bc2 = 1.0 - ADAM_B2 ** ADAM_STEP

    def body(s_ref, w_ref, m_ref, v_ref, g_ref, d_ref, mo_ref, vo_ref):
        g = s_ref[0].astype(F32)
        for k in range(1, n_slots):
            g = g + s_ref[k].astype(F32)
        mn = ADAM_B1 * m_ref[...] + (1.0 - ADAM_B1) * g
        vn = ADAM_B2 * v_ref[...] + (1.0 - ADAM_B2) * (g * g)
        m_hat = mn / bc1
        v_hat = vn / bc2
        g_ref[...] = g
        d_ref[...] = -ADAM_LR * (m_hat / (jnp.sqrt(v_hat) + ADAM_EPS) + ADAM_WD * w_ref[...])
        mo_ref[...] = mn
        vo_ref[...] = vn

    tile = pl.BlockSpec((tr, lanes), lambda i: (i, 0))
    shp = jax.ShapeDtypeStruct((rows, lanes), F32)
    return pl.pallas_call(
        body, name=name, grid=(rows // tr,),
        in_specs=[pl.BlockSpec((n_slots, tr, lanes), lambda i: (0, i, 0)), tile, tile, tile],
        out_specs=[tile] * 4, out_shape=[shp] * 4,
        compiler_params=_cparams(("parallel",)),
    )(slots, w, m, v)


def _split_dot(x, tri):
    hi = x.astype(BF16)
    lo = (x - hi.astype(F32)).astype(BF16)
    return jnp.dot(hi, tri, preferred_element_type=F32) + jnp.dot(lo, tri, preferred_element_type=F32)


_NT = (((1,), (1,)), ((), ()))
_TN = (((0,), (0,)), ((), ()))

SBA_EXP_ZERO = -104.0
SBA_SKIPPED = -1e30


def sba_fwd(qkv, name, bq=256, bk=256):
    t = qkv.shape[0]
    ratio = bq // bk
    assert bq == ratio * bk and t // bk <= LANES

    def body(q_ref, k_ref, v_ref, o_ref, runs_ref, acc_s, run_s):
        i = pl.program_id(1)
        lane = lax.broadcasted_iota(jnp.int32, (1, LANES), 1)
        lane_q = lax.broadcasted_iota(jnp.int32, (bq, LANES), 1)
        qi = lax.broadcasted_iota(jnp.int32, (bq, bk), 0) + i * bq
        kj = lax.broadcasted_iota(jnp.int32, (bq, bk), 1)
        later = (lax.broadcasted_iota(jnp.int32, (bk, bk), 0) > lax.broadcasted_iota(jnp.int32, (bk, bk), 1)).astype(BF16)
        qv = q_ref[...]
        qms = [jnp.where(hm, qv, jnp.zeros_like(qv)) for hm in (lane < 64, lane >= 64)]
        acc_s[...] = jnp.zeros_like(acc_s)
        run_s[...] = jnp.zeros_like(run_s)
        runs_ref[...] = jnp.full(runs_ref.shape, SBA_SKIPPED, F32)

        def tiles(specs):
            chains = [(ti, hh) for ti in range(len(specs)) for hh in range(2)]
            kb = [k_ref[pl.ds(pl.multiple_of(j * bk, bk), bk), :] for j, _ in specs]
            vb = [v_ref[pl.ds(pl.multiple_of(j * bk, bk), bk), :] for j, _ in specs]
            mask = [(kj + j * bk) < qi if masked else None for j, masked in specs]
            s = {c: lax.dot_general(qms[c[1]], kb[c[0]], _NT, preferred_element_type=F32) for c in chains}
            lk = {c: _log_sigmoid_neg(s[c]) for c in chains}
            lk = {c: lk[c] if mask[c[0]] is None else jnp.where(mask[c[0]], lk[c], 0.0) for c in chains}
            w = {c: jnp.dot(lk[c].astype(BF16), later, preferred_element_type=F32) for c in chains}
            run = {}
            for hh in range(2):
                carry = run_s[hh]
                for ti in range(len(specs)):
                    run[ti, hh] = carry
                    carry = carry + jnp.sum(lk[ti, hh], axis=1, keepdims=True)
                run_s[hh] = carry
            a = {c: jnp.exp(s[c] + lk[c] + w[c] + run[c]) for c in chains}
            a = {c: a[c] if mask[c[0]] is None else jnp.where(mask[c[0]], a[c], 0.0) for c in chains}
            for ti, hh in chains:
                acc_s[hh] += jnp.dot(a[ti, hh].astype(BF16), vb[ti], preferred_element_type=F32)
                runs_ref[hh] = jnp.where(lane_q == specs[ti][0], run[ti, hh], runs_ref[hh])

        for d in range(ratio):
            tiles([((i + 1) * ratio - 1 - d, True)])

        def live():
            return jnp.maximum(jnp.max(run_s[0]), jnp.max(run_s[1])) >= SBA_EXP_ZERO

        def cond(state):
            n, go = state
            return jnp.logical_and(n < i * ratio, go)

        def step(state):
            n, _ = state
            tiles([(i * ratio - 1 - n, False)])
            return n + 1, live()

        lax.while_loop(cond, step, (jnp.int32(0), live()))
        o_ref[...] = jnp.where(lane < 64, acc_s[0], acc_s[1])

    return pl.pallas_call(
        body, name=name, grid=(2, t // bq),
        in_specs=[pl.BlockSpec((bq, LANES), lambda p, i: (i, p)), pl.BlockSpec((t, LANES), lambda p, i: (0, 2 + p)),
                  pl.BlockSpec((t, LANES), lambda p, i: (0, 4 + p))],
        out_specs=[pl.BlockSpec((bq, LANES), lambda p, i: (i, p)), pl.BlockSpec((2, bq, LANES), lambda p, i: (p, i, 0))],
        out_shape=[jax.ShapeDtypeStruct((t, SB_WIDTH), F32), jax.ShapeDtypeStruct((4, t, LANES), F32)],
        scratch_shapes=[pltpu.VMEM((2, bq, LANES), F32), pltpu.VMEM((2, bq, 1), F32)],
        compiler_params=_cparams(("parallel", "parallel")),
    )(qkv, qkv, qkv)


def sba_bwd(qkv, runs, do, name, bq=256, bk=256):
    t = qkv.shape[0]
    ratio = bq // bk
    assert bq == ratio * bk
    nq = t // bq

    def body(q_ref, k_ref, v_ref, runs_ref, do_ref, dq_ref, dk_hbm, dv_hbm, dk_s, dv_s, sem, dq_s, rg_s):
        p = pl.program_id(0)
        i = pl.program_id(1)

        @pl.when(i == 0)
        def _():
            dk_s[...] = jnp.zeros_like(dk_s)
            dv_s[...] = jnp.zeros_like(dv_s)

        lane = lax.broadcasted_iota(jnp.int32, (1, LANES), 1)
        qi = lax.broadcasted_iota(jnp.int32, (bq, bk), 0) + i * bq
        kj = lax.broadcasted_iota(jnp.int32, (bq, bk), 1)
        r2 = lax.broadcasted_iota(jnp.int32, (bk, bk), 0)
        c2 = lax.broadcasted_iota(jnp.int32, (bk, bk), 1)
        later = (r2 > c2).astype(BF16)
        earlier = (r2 < c2).astype(BF16)
        qv = q_ref[...]
        dov = do_ref[...]
        heads = range(2)
        hms = (lane < 64, lane >= 64)
        qms = [jnp.where(hm, qv, jnp.zeros_like(qv)) for hm in hms]
        doms = [jnp.where(hm, dov, 0.0).astype(BF16) for hm in hms]
        runs = [runs_ref[hh] for hh in heads]
        dq_s[...] = jnp.zeros_like(dq_s)
        rg_s[...] = jnp.zeros_like(rg_s)

        def tiles(specs):
            nt = len(specs)
            chains = [(ti, hh) for ti in range(nt) for hh in heads]
            starts = [pl.multiple_of(j * bk, bk) for j, _ in specs]
            kb = [k_ref[pl.ds(st, bk), :] for st in starts]
            vb = [v_ref[pl.ds(st, bk), :] for st in starts]
            mask = [(kj + j * bk) < qi if masked else None for j, masked in specs]
            s = {c: lax.dot_general(qms[c[1]], kb[c[0]], _NT, preferred_element_type=F32) for c in chains}
            da = {c: lax.dot_general(doms[c[1]], vb[c[0]], _NT, preferred_element_type=F32) for c in chains}
            lk_raw = {c: _log_sigmoid_neg(s[c]) for c in chains}
            lk = {c: lk_raw[c] if mask[c[0]] is None else jnp.where(mask[c[0]], lk_raw[c], 0.0) for c in chains}
            w = {c: jnp.dot(lk[c].astype(BF16), later, preferred_element_type=F32) for c in chains}
            run = {c: jnp.sum(jnp.where(lane == specs[c[0]][0], runs[c[1]], 0.0), axis=1, keepdims=True) for c in chains}
            a = {c: jnp.exp(s[c] + lk[c] + w[c] + run[c]) for c in chains}
            a = {c: a[c] if mask[c[0]] is None else jnp.where(mask[c[0]], a[c], 0.0) for c in chains}
            g = {c: a[c] * da[c] for c in chains}
            rg = {}
            for hh in heads:
                carry = rg_s[hh]
                for ti in range(nt):
                    rg[ti, hh] = carry
                    carry = carry + jnp.sum(g[ti, hh], axis=1, keepdims=True)
                rg_s[hh] = carry
            cpre = {c: rg[c] + _split_dot(g[c], earlier) for c in chains}
            dz = {c: g[c] - jnp.exp(s[c] + lk_raw[c]) * (g[c] + cpre[c]) for c in chains}
            dz = {c: (dz[c] if mask[c[0]] is None else jnp.where(mask[c[0]], dz[c], 0.0)).astype(BF16) for c in chains}
            for ti, hh in chains:
                dq_s[hh] += jnp.dot(dz[ti, hh], kb[ti], preferred_element_type=F32)
            for ti in range(nt):
                dk = lax.dot_general(dz[ti, 0], qms[0], _TN, preferred_element_type=F32)
                dk_s[pl.ds(starts[ti], bk), :] += dk + lax.dot_general(dz[ti, 1], qms[1], _TN, preferred_element_type=F32)
                dv = lax.dot_general(a[ti, 0].astype(BF16), doms[0], _TN, preferred_element_type=F32)
                dv_s[pl.ds(starts[ti], bk), :] += dv + lax.dot_general(a[ti, 1].astype(BF16), doms[1], _TN,
                                                                       preferred_element_type=F32)

        live = jnp.maximum(jnp.max(runs[0], axis=0, keepdims=True), jnp.max(runs[1], axis=0, keepdims=True)) >= SBA_EXP_ZERO
        first = jnp.minimum(jnp.min(jnp.where(live, lane, LANES)), i * ratio)

        def step(j, carry):
            tiles([(j, False)])
            return carry

        if ratio == 1:
            lax.fori_loop(first, i - 1, step, 0)

            @pl.when(i == 0)
            def _():
                tiles([(i, True)])

            @pl.when(i > 0)
            def _():
                tiles([(i - 1, False), (i, True)])
        else:
            lax.fori_loop(first, i * ratio, step, 0)
            for d in range(ratio):
                tiles([(i * ratio + d, True)])

        dq_ref[...] = jnp.where(lane < 64, dq_s[0], dq_s[1])

        @pl.when(i == nq - 1)
        def _():
            col = pl.multiple_of(p * LANES, LANES)
            ck = pltpu.make_async_copy(dk_s, dk_hbm.at[:, pl.ds(col, LANES)], sem.at[0])
            cv = pltpu.make_async_copy(dv_s, dv_hbm.at[:, pl.ds(col, LANES)], sem.at[1])
            ck.start()
            cv.start()
            ck.wait()
            cv.wait()

    shp = jax.ShapeDtypeStruct((t, SB_WIDTH), F32)
    tile = pl.BlockSpec((bq, LANES), lambda p, i: (i, p))
    return pl.pallas_call(
        body, name=name, grid=(2, nq),
        in_specs=[tile, pl.BlockSpec((t, LANES), lambda p, i: (0, 2 + p)), pl.BlockSpec((t, LANES), lambda p, i: (0, 4 + p)),
                  pl.BlockSpec((2, bq, LANES), lambda p, i: (p, i, 0)), tile],
        out_specs=[tile, pl.BlockSpec(memory_space=pl.ANY), pl.BlockSpec(memory_space=pl.ANY)],
        out_shape=[shp, shp, shp],
        scratch_shapes=[pltpu.VMEM((t, LANES), F32), pltpu.VMEM((t, LANES), F32), pltpu.SemaphoreType.DMA((2,)),
                        pltpu.VMEM((2, bq, LANES), F32), pltpu.VMEM((2, bq, 1), F32)],
        compiler_params=_cparams(("arbitrary", "arbitrary")),
    )(qkv, qkv, qkv, runs, do)


def _ssd_consts():
    ln = SSM_CHUNK
    ri = lax.broadcasted_iota(jnp.int32, (ln, ln), 0)
    ci = lax.broadcasted_iota(jnp.int32, (ln, ln), 1)
    eh = lax.broadcasted_iota(jnp.int32, (LANES, SSM_INNER), 0)
    el = lax.broadcasted_iota(jnp.int32, (LANES, SSM_INNER), 1)
    expand = (jnp.right_shift(el, 6) == eh).astype(BF16)
    th = lax.broadcasted_iota(jnp.int32, (SSM_INNER, LANES), 1)
    tl = lax.broadcasted_iota(jnp.int32, (SSM_INNER, LANES), 0)
    reduce = (jnp.right_shift(tl, 6) == th).astype(BF16)
    return ri, ci, expand, reduce


def _dot_f32(a, b):
    return jnp.dot(a, b, precision=HI, preferred_element_type=F32)


def _split3(x):
    hi = x.astype(BF16)
    r1 = x - hi.astype(F32)
    mid = r1.astype(BF16)
    lo = (r1 - mid.astype(F32)).astype(BF16)
    return hi, mid, lo


def _dot_hi(a, b):
    if a.dtype == BF16:
        return sum(jnp.dot(a, t, preferred_element_type=F32) for t in _split3(b))
    return sum(jnp.dot(t, b, preferred_element_type=F32) for t in _split3(a))


def _ssd_prelude(xbc_ref, dt_ref, dtt_ref, hpr_ref, hpc_ref, ri, ci, expand):
    ln = SSM_CHUNK
    xs = xbc_ref[:, 0:512]
    bm = xbc_ref[:, 512:640]
    cm = xbc_ref[:, 640:768]
    dtb_r = hpr_ref[0:1, :]
    aneg_r = -jnp.exp(hpr_ref[1:2, :])
    pre = dt_ref[...] + dtb_r
    dt = _softplus(pre)
    a = dt * aneg_r
    dtt = _softplus(dtt_ref[...] + hpc_ref[0:8, :])
    att = dtt * (-jnp.exp(hpc_ref[8:16, :]))
    tril = (ri >= ci).astype(BF16)
    triu = (ri <= ci).astype(BF16)
    acs = _dot_hi(tril, a)
    acst = _dot_hi(att, triu)
    acs_e = _dot_hi(acs, expand)
    dt_e = _dot_hi(dt, expand)
    last_e = acs_e[ln - 1:ln, :]
    e_e = jnp.exp(acs_e)
    w_e = jnp.exp(last_e - acs_e)
    dec_e = jnp.exp(last_e)
    xdt = xs * dt_e
    return dict(xs=xs, bm=bm, cm=cm, pre=pre, dt=dt, aneg_r=aneg_r, acs=acs, acst=acst, dt_e=dt_e, e_e=e_e,
                w_e=w_e, dec_e=dec_e, xdt=xdt, triu=triu)


def ssd_fwd(act, p, dt32, dtt, hp_rows, hp_cols, d_e, norm_w, name):
    t = act.shape[0]
    ln = SSM_CHUNK
    nc = t // ln

    def body(xbc_ref, dt_ref, z_ref, dtt_ref, hpr_ref, hpc_ref, d_ref, nw_ref, yc_ref, y_ref, sto_ref, st):
        @pl.when(pl.program_id(0) == 0)
        def _():
            st[...] = jnp.zeros_like(st)

        ri, ci, expand, _ = _ssd_consts()
        q = _ssd_prelude(xbc_ref, dt_ref, dtt_ref, hpr_ref, hpc_ref, ri, ci, expand)
        lane = lax.broadcasted_iota(jnp.int32, (1, LANES), 1)
        rown = lax.broadcasted_iota(jnp.int32, (LANES, 1), 0)
        low = lane < 64
        mask = ri >= ci
        xdt_b = q["xdt"].astype(BF16)
        xw_b = (q["xdt"] * q["w_e"]).astype(BF16)
        bt = q["bm"].T.astype(BF16)
        cb_ = q["cm"].astype(BF16)
        y_pairs = []
        for g in range(2):
            gm = low if g == 0 else jnp.logical_not(low)
            rm = (rown < 64) if g == 0 else (rown >= 64)
            cg = jnp.where(gm, cb_, jnp.zeros_like(cb_))
            cb = jnp.dot(cg, bt, preferred_element_type=F32)
            for pp in range(2):
                pi = 2 * g + pp
                sl = slice(LANES * pi, LANES * (pi + 1))
                xp = xdt_b[:, sl]
                yd = []
                for hh in range(2):
                    h = 2 * pi + hh
                    diff = q["acs"][:, h:h + 1] - q["acst"][h:h + 1, :]
                    lam = jnp.exp(jnp.where(mask, diff, -jnp.inf))
                    yd.append(jnp.dot((cb * lam).astype(BF16), xp, preferred_element_type=F32))
                sp = st[pi]
                sto_ref[0, pi] = sp
                yoff = jnp.dot(cg, sp.astype(BF16), preferred_element_type=F32) * q["e_e"][:, sl]
                upd = jnp.dot(bt, xw_b[:, sl], preferred_element_type=F32)
                st[pi] = q["dec_e"][:, sl] * sp + jnp.where(rm, upd, 0.0)
                y_pairs.append(jnp.where(low, yd[0], yd[1]) + yoff)
        y = jnp.concatenate(y_pairs, axis=1) + q["xs"] * d_ref[...]
        y_ref[...] = y
        yg = y * _silu(z_ref[...].astype(F32))
        for g in range(2):
            sl = slice(256 * g, 256 * (g + 1))
            seg = yg[:, sl]
            yc_ref[:, sl] = (seg * _rstd(seg) * nw_ref[:, sl]).astype(BF16)

    return pl.pallas_call(
        body, name=name, grid=(nc,),
        in_specs=[pl.BlockSpec((ln, 768), lambda c: (c, 0)), pl.BlockSpec((ln, LANES), lambda c: (c, 0)),
                  pl.BlockSpec((ln, 512), lambda c: (c, OFF_Z // 512)), pl.BlockSpec((8, ln), lambda c: (0, c)),
                  pl.BlockSpec((8, LANES), lambda c: (0, 0)), pl.BlockSpec((16, ln), lambda c: (0, 0)),
                  pl.BlockSpec((1, 512), lambda c: (0, 0)), pl.BlockSpec((1, 512), lambda c: (0, 0))],
        out_specs=[pl.BlockSpec((ln, 512), lambda c: (c, 0)), pl.BlockSpec((ln, 512), lambda c: (c, 0)),
                   pl.BlockSpec((1, 4, LANES, LANES), lambda c: (c, 0, 0, 0))],
        out_shape=[jax.ShapeDtypeStruct((t, 512), BF16), jax.ShapeDtypeStruct((t, 512), F32),
                   jax.ShapeDtypeStruct((nc, 4, LANES, LANES), F32)],
        scratch_shapes=[pltpu.VMEM((4, LANES, LANES), F32)],
        compiler_params=_cparams(("arbitrary",)),
    )(act, dt32, p, dtt, hp_rows, hp_cols, d_e, norm_w)


def ssd_bwd(dyc, y, act, p, dt32, dtt, states, hp_rows, hp_cols, d_e, norm_w, name):
    t = act.shape[0]
    ln = SSM_CHUNK
    nc = t // ln

    def body(dyc_ref, y_ref, xbc_ref, dt_ref, z_ref, dtt_ref, st_ref, hpr_ref, hpc_ref, d_ref, nw_ref,
             dz_ref, dact_ref, ddt_ref, dnw_ref, dd_ref, dhp_ref, ds):
        @pl.when(pl.program_id(0) == 0)
        def _():
            ds[...] = jnp.zeros_like(ds)
            dnw_ref[...] = jnp.zeros_like(dnw_ref)
            dd_ref[...] = jnp.zeros_like(dd_ref)
            dhp_ref[...] = jnp.zeros_like(dhp_ref)

        ri, ci, expand, reduce = _ssd_consts()
        q = _ssd_prelude(xbc_ref, dt_ref, dtt_ref, hpr_ref, hpc_ref, ri, ci, expand)
        lane = lax.broadcasted_iota(jnp.int32, (1, LANES), 1)
        rown = lax.broadcasted_iota(jnp.int32, (LANES, 1), 0)
        low = lane < 64
        mask = ri >= ci
        mask_t = ci >= ri
        xs, xdt, acs, acst = q["xs"], q["xdt"], q["acs"], q["acst"]
        yv, zv, nw = y_ref[...], z_ref[...].astype(F32), nw_ref[...]
        sg = _sigmoid(zv)
        zz = zv * sg
        yg = yv * zz
        dycv = dyc_ref[...]
        u = dycv * nw
        dyg_parts, dnw_parts = [], []
        for g in range(2):
            sl = slice(256 * g, 256 * (g + 1))
            seg = yg[:, sl]
            rr = _rstd(seg)
            nrm = seg * rr
            dyg_parts.append(rr * (u[:, sl] - nrm * jnp.mean(nrm * u[:, sl], axis=-1, keepdims=True)))
            dnw_parts.append(jnp.sum(dycv[:, sl] * nrm, axis=0, keepdims=True))
        dyg = jnp.concatenate(dyg_parts, axis=1)
        dnw_ref[...] += jnp.concatenate(dnw_parts, axis=1)
        dy = dyg * zz
        dz_ref[...] = dyg * yv * (sg * (1.0 + zv * (1.0 - sg)))
        dd_ref[...] += jnp.sum(dy * xs, axis=0, keepdims=True)
        dxs = dy * d_ref[...]
        dy_b = dy.astype(BF16)
        xdt_b = xdt.astype(BF16)
        xw_b = (xdt * q["w_e"]).astype(BF16)
        bt = q["bm"].T.astype(BF16)
        ct = q["cm"].T.astype(BF16)
        cb_ = q["cm"].astype(BF16)
        bb_ = q["bm"].astype(BF16)
        dacs = jnp.zeros((ln, LANES), F32)
        dc = jnp.zeros((ln, LANES), F32)
        db = jnp.zeros((ln, LANES), F32)
        dxdt_pairs, yoffdy_pairs, dwe_pairs, ddec_pairs = [], [], [], []
        for g in range(2):
            gm = low if g == 0 else jnp.logical_not(low)
            rm = (rown < 64) if g == 0 else (rown >= 64)
            cg = jnp.where(gm, cb_, jnp.zeros_like(cb_))
            bg = jnp.where(gm, bb_, jnp.zeros_like(bb_))
            cb = jnp.dot(cg, bt, preferred_element_type=F32)
            cbt = jnp.dot(bg, ct, preferred_element_type=F32)
            dcb = jnp.zeros((ln, ln), F32)
            dcbt = jnp.zeros((ln, ln), F32)
            for pp in range(2):
                pi = 2 * g + pp
                sl = slice(LANES * pi, LANES * (pi + 1))
                xp = xdt_b[:, sl]
                dyp = dy_b[:, sl]
                xpt = xdt[:, sl].T.astype(BF16)
                dypt = dy[:, sl].T.astype(BF16)
                dxdt_p = jnp.zeros((ln, LANES), F32)
                for hh in range(2):
                    h = 2 * pi + hh
                    hm = low if hh == 0 else jnp.logical_not(low)
                    col = acs[:, h:h + 1]
                    row = acst[h:h + 1, :]
                    lam = jnp.exp(jnp.where(mask, col - row, -jnp.inf))
                    lam_t = jnp.exp(jnp.where(mask_t, row - col, -jnp.inf))
                    m = cb * lam
                    m_t = cbt * lam_t
                    dyh = jnp.where(hm, dyp, jnp.zeros_like(dyp))
                    xh = jnp.where(hm, xp, jnp.zeros_like(xp))
                    dm = jnp.dot(dyh, xpt, preferred_element_type=F32)
                    dm_t = jnp.dot(xh, dypt, preferred_element_type=F32)
                    dcb = dcb + dm * lam
                    dcbt = dcbt + dm_t * lam_t
                    rs = jnp.sum(dm * m, axis=1, keepdims=True) - jnp.sum(dm_t * m_t, axis=1, keepdims=True)
                    dacs = dacs + jnp.where(lane == h, rs, 0.0)
                    dxdt_p = dxdt_p + jnp.dot(m_t.astype(BF16), dyh, preferred_element_type=F32)
                sp = st_ref[0, pi]
                sp_b = sp.astype(BF16)
                dsn = ds[pi]
                dsn_b = dsn.astype(BF16)
                e_p, w_p, dec_p = q["e_e"][:, sl], q["w_e"][:, sl], q["dec_e"][:, sl]
                yoff = jnp.dot(cg, sp_b, preferred_element_type=F32) * e_p
                dyo = dy[:, sl] * e_p
                dyo_b = dyo.astype(BF16)
                dc = dc + lax.dot_general(dyo_b, sp_b, _NT, preferred_element_type=F32)
                ds_prev = dec_p * dsn + jnp.where(rm, jnp.dot(ct, dyo_b, preferred_element_type=F32), 0.0)
                yoffdy_pairs.append(dy[:, sl] * yoff)
                dxw = jnp.dot(bg, dsn_b, preferred_element_type=F32)
                db = db + lax.dot_general(xw_b[:, sl], dsn_b, _NT, preferred_element_type=F32)
                dxdt_p = dxdt_p + dxw * w_p
                dwe_pairs.append(dxw * xdt[:, sl])
                ddec_pairs.append(jnp.sum(dsn * sp, axis=0, keepdims=True))
                ds[pi] = ds_prev
                dxdt_pairs.append(dxdt_p)
            dc = dc + jnp.dot(dcb.astype(BF16), bg, preferred_element_type=F32)
            db = db + jnp.dot(dcbt.astype(BF16), cg, preferred_element_type=F32)
        dxdt = jnp.concatenate(dxdt_pairs, axis=1)
        yoffdy = jnp.concatenate(yoffdy_pairs, axis=1)
        dwe = jnp.concatenate(dwe_pairs, axis=1)
        ddec_e = jnp.broadcast_to(jnp.concatenate(ddec_pairs, axis=1), (8, SSM_INNER))
        last = acs[ln - 1:ln, :]
        w_col = jnp.exp(last - acs)
        dw_col = _dot_hi(dwe, reduce) * w_col
        dacs = dacs + _dot_hi(yoffdy, reduce) - dw_col
        dlast = jnp.sum(dw_col, axis=0, keepdims=True) + jnp.exp(last) * _dot_hi(ddec_e, reduce)[0:1, :]
        rowi = lax.broadcasted_iota(jnp.int32, (ln, 1), 0)
        dacs = dacs + jnp.where(rowi == ln - 1, dlast, 0.0)
        da = _dot_hi(q["triu"], dacs)
        ddt = da * q["aneg_r"] + _dot_hi(dxdt * xs, reduce)
        ddt_raw = jnp.where(lane < SSM_HEADS, ddt * _sigmoid(q["pre"]), 0.0)
        ddt_ref[...] = ddt_raw
        dhp_ref[0:1, :] += jnp.sum(ddt_raw, axis=0, keepdims=True)
        dhp_ref[1:2, :] += jnp.where(lane < SSM_HEADS, jnp.sum(da * q["dt"], axis=0, keepdims=True) * q["aneg_r"], 0.0)
        dact_ref[:, 0:512] = dxs + dxdt * q["dt_e"]
        dact_ref[:, 512:640] = db
        dact_ref[:, 640:768] = dc

    rev = lambda c: nc - 1 - c
    return pl.pallas_call(
        body, name=name, grid=(nc,),
        in_specs=[pl.BlockSpec((ln, 512), lambda c: (rev(c), 0)), pl.BlockSpec((ln, 512), lambda c: (rev(c), 0)),
                  pl.BlockSpec((ln, 768), lambda c: (rev(c), 0)),
                  pl.BlockSpec((ln, LANES), lambda c: (rev(c), 0)),
                  pl.BlockSpec((ln, 512), lambda c: (rev(c), OFF_Z // 512)), pl.BlockSpec((8, ln), lambda c: (0, rev(c))),
                  pl.BlockSpec((1, 4, LANES, LANES), lambda c: (rev(c), 0, 0, 0)),
                  pl.BlockSpec((8, LANES), lambda c: (0, 0)), pl.BlockSpec((16, ln), lambda c: (0, 0)),
                  pl.BlockSpec((1, 512), lambda c: (0, 0)), pl.BlockSpec((1, 512), lambda c: (0, 0))],
        out_specs=[pl.BlockSpec((ln, 512), lambda c: (rev(c), 0)), pl.BlockSpec((ln, 768), lambda c: (rev(c), 0)),
                   pl.BlockSpec((ln, LANES), lambda c: (rev(c), 0)), pl.BlockSpec((1, 512), lambda c: (0, 0)),
                   pl.BlockSpec((1, 512), lambda c: (0, 0)), pl.BlockSpec((8, LANES), lambda c: (0, 0))],
        out_shape=[jax.ShapeDtypeStruct((t, 512), F32), jax.ShapeDtypeStruct((t, 768), F32),
                   jax.ShapeDtypeStruct((t, LANES), F32), jax.ShapeDtypeStruct((1, 512), F32),
                   jax.ShapeDtypeStruct((1, 512), F32), jax.ShapeDtypeStruct((8, LANES), F32)],
        scratch_shapes=[pltpu.VMEM((4, LANES, LANES), F32)],
        compiler_params=_cparams(("arbitrary",)),
    )(dyc, y, act, dt32, p, dtt, states, hp_rows, hp_cols, d_e, norm_w)


def mod_shard_fwd(c_all, mod_w, mod_b_shard, name):
    def body(c_ref, w_ref, b_ref, o_ref):
        sc = _silu(c_ref[...])
        for l in range(DEPTH):
            o_ref[l] = _dot_f32(sc, w_ref[l]) + b_ref[l]

    return pl.pallas_call(body, name=name, out_shape=jax.ShapeDtypeStruct((DEPTH, N_DEV, mod_w.shape[2]), F32),
                          compiler_params=_cparams())(c_all, mod_w, mod_b_shard)


def mod_w_grad(c_all, dmod_shard, name):
    def body(c_ref, d_ref, o_ref):
        sc = _silu(c_ref[...])
        for l in range(DEPTH):
            o_ref[l] = lax.dot_general(sc, d_ref[l], _TN, precision=HI, preferred_element_type=F32)

    return pl.pallas_call(body, name=name, out_shape=jax.ShapeDtypeStruct((DEPTH, D_MODEL, dmod_shard.shape[2]), F32),
                          compiler_params=_cparams())(c_all, dmod_shard)


_BIG = ("w_in", "sc_conv_w", "ssm_conv_w", "w_sc_out", "w_sb_out", "w_ssm_out", "w_o", "w_ffn_in", "w_ffn_out")
_ROW_SHARDED = ("w_o", "w_ffn_out")
_CONV = ("sc_conv_w", "ssm_conv_w")
_SMALL = ("mod_b", "g_pre_mix", "g_post_mix", "g_pre_ffn", "g_post_ffn", "ssm_conv_b", "ssm_dt_bias", "ssm_a_log",
          "ssm_d", "ssm_norm_w")
_WEIGHTS = ("mod_w", "mod_b", "g_pre_mix", "g_post_mix", "g_pre_ffn", "g_post_ffn", "w_in", "sc_conv_w", "ssm_conv_w",
            "ssm_conv_b", "ssm_dt_bias", "ssm_a_log", "ssm_d", "ssm_norm_w", "w_sc_out", "w_sb_out", "w_ssm_out", "w_o",
            "w_ffn_in", "w_ffn_out")


def _gathered_to_full(g, row_sharded):
    _, dep, r, c = g.shape
    if row_sharded:
        return g.transpose(1, 0, 2, 3).reshape(dep, N_DEV * r, c)
    return g.transpose(1, 2, 0, 3).reshape(dep, r, N_DEV * c)


def _full_to_slots(w, row_sharded):
    dep, r, c = w.shape
    if row_sharded:
        return w.reshape(dep, N_DEV, r // N_DEV, c).transpose(1, 0, 2, 3).reshape(N_DEV, dep * (r // N_DEV), c)
    return w.reshape(dep, r, N_DEV, c // N_DEV).transpose(2, 0, 1, 3).reshape(N_DEV, dep * r, c // N_DEV)


def _pad_in_proj(w):
    sc, qkv, z, xbc, dt, gates = (w[:, 0:768], w[:, 768:1536], w[:, 1536:2048], w[:, 2048:2816], w[:, 2816:2824],
                                  w[:, 2824:5896])
    pad = jnp.zeros((w.shape[0], OFF_Z - OFF_DT - 8), w.dtype)
    return jnp.concatenate([gates, sc, qkv, xbc, dt, pad, z], axis=1)


def _unpad_in_proj(w):
    return jnp.concatenate([w[:, OFF_SC:OFF_SC + 768], w[:, OFF_QKV:OFF_QKV + 768], w[:, OFF_Z:OFF_Z + 512],
                            w[:, OFF_XBC:OFF_XBC + 768], w[:, OFF_DT:OFF_DT + 8], w[:, 0:3072]], axis=1)


def _row(v):
    return v.reshape(1, -1)


def _local_step(x, target, mod, small, conv, big):
    lw, saved = [], []
    for l in range(DEPTH):
        w_in_p = _pad_in_proj(big["w_in"][l])
        w_cat = jnp.concatenate([big["w_sc_out"][l], big["w_sb_out"][l], big["w_ssm_out"][l]], axis=0)
        hp_rows = jnp.zeros((8, LANES), F32).at[0, :SSM_HEADS].set(small["ssm_dt_bias"][l]).at[1, :SSM_HEADS].set(
            small["ssm_a_log"][l])
        hp_cols = jnp.concatenate([jnp.broadcast_to(small["ssm_dt_bias"][l][:, None], (SSM_HEADS, SSM_CHUNK)),
                                   jnp.broadcast_to(small["ssm_a_log"][l][:, None], (SSM_HEADS, SSM_CHUNK))], axis=0)
        lw.append(dict(
            w_in_p=w_in_p, w_cat=w_cat, w_o=big["w_o"][l], w_ffn_in=big["w_ffn_in"][l], w_ffn_out=big["w_ffn_out"][l],
            sc_w8=jnp.pad(conv["sc_conv_w"][l], ((0, 5), (0, 0))), ssm_w8=jnp.pad(conv["ssm_conv_w"][l], ((0, 4), (0, 0))),
            ssm_b=_row(small["ssm_conv_b"][l]), hp_rows=hp_rows, hp_cols=hp_cols,
            d_e=_row(jnp.repeat(small["ssm_d"][l], SSM_HEAD_DIM)), norm_w=_row(small["ssm_norm_w"][l]),
            g_pre_mix=_row(small["g_pre_mix"][l]), g_post_mix=_row(small["g_post_mix"][l]),
            g_pre_ffn=_row(small["g_pre_ffn"][l]), g_post_ffn=_row(small["g_post_ffn"][l]),
            shift1=mod[l, 0:1], scale1=mod[l, 1:2], gate1=mod[l, 2:3], shift2=mod[l, 3:4], scale2=mod[l, 4:5],
            gate2=mod[l, 5:6]))

    xl = x
    h = normmod_fwd(xl, lw[0]["g_pre_mix"], lw[0]["scale1"], lw[0]["shift1"], "normmod_fwd_0")
    dy = loss = None
    for l in range(DEPTH):
        w = lw[l]
        p = mm_nn([h], [w["w_in_p"]], BF16, f"in_proj_{l}")
        dt32 = mm_nn([h], [w["w_in_p"][:, OFF_DT:OFF_DT + LANES]], F32, f"in_proj_dt_{l}")
        ya, qkv, act = post_inproj(p, w["sc_w8"], w["ssm_w8"], w["ssm_b"], f"post_inproj_{l}")
        o, runs = sba_fwd(qkv, f"sba_fwd_{l}")
        dtt = dt32[:, :SSM_HEADS].T
        yc, ypre, states = ssd_fwd(act, p, dt32, dtt, w["hp_rows"], w["hp_cols"], w["d_e"], w["norm_w"], f"ssd_fwd_{l}")
        merged = branch_out_fwd(ya, o, yc, p, w["w_cat"], f"branch_fwd_{l}")
        mix = mm_nn([merged], [w["w_o"]], F32, f"out_proj_{l}")
        x1, h2 = resid_normmod_fwd(xl, mix, w["gate1"], w["g_post_mix"], w["g_pre_ffn"], w["scale2"], w["shift2"],
                                   f"resid_mix_{l}")
        gt, up, a = mm_swiglu_fwd(h2, w["w_ffn_in"], f"ffn_in_{l}")
        f = mm_nn([a], [w["w_ffn_out"]], F32, f"ffn_out_{l}")
        saved.append(dict(x=xl, h=h, p=p, ya=ya, qkv=qkv, act=act, o=o, runs=runs, dt32=dt32, dtt=dtt, yc=yc, ypre=ypre, states=states,
                          merged=merged, mix=mix, x1=x1, h2=h2, gt=gt, up=up, a=a, f=f))
        if l + 1 < DEPTH:
            nw = lw[l + 1]
            xl, h = resid_normmod_fwd(x1, f, w["gate2"], w["g_post_ffn"], nw["g_pre_mix"], nw["scale1"], nw["shift1"],
                                      f"resid_ffn_{l}")
        else:
            dy, loss = resid_loss(x1, f, w["gate2"], w["g_post_ffn"], target, "resid_loss")

    dmod = [None] * DEPTH
    gs = {k: [None] * DEPTH for k in _SMALL + _BIG}
    dxo = dy
    top, stl = lw[DEPTH - 1], saved[DEPTH - 1]
    df, dgate2, gs["g_post_ffn"][DEPTH - 1] = resid_bwd(dxo, stl["f"], top["gate2"], top["g_post_ffn"],
                                                        f"resid_ffn_bwd_{DEPTH - 1}")
    for l in reversed(range(DEPTH)):
        w, s = lw[l], saved[l]
        dgt, dup = mm_swiglu_bwd(df, w["w_ffn_out"], s["gt"], s["up"], f"ffn_out_bwd_{l}")
        gs["w_ffn_out"][l] = mm_tn(s["a"], df, f"dw_ffn_out_{l}")
        dh2 = mm_nt([dgt, dup], [w["w_ffn_in"], w["w_ffn_in"]], [0, FFN_HIDDEN], F32, f"ffn_in_bwd_{l}")
        gs["w_ffn_in"][l] = jnp.concatenate([mm_tn(s["h2"], dgt, f"dw_ffn_gate_{l}"), mm_tn(s["h2"], dup, f"dw_ffn_up_{l}")],
                                            axis=1)
        dx1, dmix, dscale2, dshift2, gs["g_pre_ffn"][l], dgate1, gs["g_post_mix"][l] = normmod_resid_bwd(
            dh2, s["x1"], dxo, w["g_pre_ffn"], w["scale2"], s["mix"], w["gate1"], w["g_post_mix"], f"ffn_mix_bwd_{l}")
        dmerged = mm_nt([dmix], [w["w_o"]], [0], F32, f"out_proj_bwd_{l}")
        gs["w_o"][l] = mm_tn(s["merged"], dmix, f"dw_o_{l}")
        dp_gates, dya, dyb, dyc, dw_cat = branch_out_bwd(dmerged, s["ya"], s["o"], s["yc"], s["p"], w["w_cat"],
                                                         f"branch_bwd_{l}")
        gs["w_sc_out"][l], gs["w_sb_out"][l], gs["w_ssm_out"][l] = dw_cat[0:256], dw_cat[256:512], dw_cat[512:1024]
        dz, dact, ddt, dnw, dd_e, dhp = ssd_bwd(dyc, s["ypre"], s["act"], s["p"], s["dt32"], s["dtt"], s["states"], w["hp_rows"],
                                                w["hp_cols"], w["d_e"], w["norm_w"], f"ssd_bwd_{l}")
        gs["ssm_norm_w"][l] = dnw[0]
        gs["ssm_d"][l] = dd_e.reshape(SSM_HEADS, SSM_HEAD_DIM).sum(axis=1)
        gs["ssm_dt_bias"][l] = dhp[0, :SSM_HEADS]
        gs["ssm_a_log"][l] = dhp[1, :SSM_HEADS]
        dq, dk, dv = sba_bwd(s["qkv"], s["runs"], dyb, f"sba_bwd_{l}")
        dp, dscw, dssw, dssb = assemble_dp(dp_gates, dya, s["p"], w["sc_w8"], dq, dk, dv, dact, w["ssm_w8"], w["ssm_b"], ddt,
                                           dz, f"assemble_dp_{l}")
        gs["sc_conv_w"][l], gs["ssm_conv_w"][l], gs["ssm_conv_b"][l] = dscw[0:3], dssw[0:4], dssb[0]
        dh = mm_nt([dp], [w["w_in_p"]], [0], F32, f"in_proj_bwd_{l}")
        gs["w_in"][l] = _unpad_in_proj(mm_tn(s["h"], dp, f"dw_in_{l}"))
        if l > 0:
            below, sb = lw[l - 1], saved[l - 1]
            dxo, df, dscale1, dshift1, gs["g_pre_mix"][l], dgate2_below, gs["g_post_ffn"][l - 1] = normmod_resid_bwd(
                dh, s["x"], dx1, w["g_pre_mix"], w["scale1"], sb["f"], below["gate2"], below["g_post_ffn"],
                f"mix_ffn_bwd_{l}")
        else:
            dxo, dscale1, dshift1, gs["g_pre_mix"][l] = normmod_bwd(dh, s["x"], dx1, w["g_pre_mix"], w["scale1"],
                                                                    f"normmod_mix_bwd_{l}")
            dgate2_below = None
        dmod[l] = jnp.concatenate([dshift1, dscale1, dgate1, dshift2, dscale2, dgate2], axis=0)
        dgate2 = dgate2_below
    for k in ("g_pre_mix", "g_post_mix", "g_pre_ffn", "g_post_ffn"):
        gs[k] = [g[0] for g in gs[k]]
    grads = {k: jnp.stack(v) for k, v in gs.items() if k != "mod_b"}
    return loss[0, 0], dxo, jnp.stack(dmod), grads


def kernel(x, c, mod_w, mod_b, g_pre_mix, g_post_mix, g_pre_ffn, g_post_ffn, w_in, sc_conv_w, ssm_conv_w, ssm_conv_b, ssm_dt_bias, ssm_a_log, ssm_d, ssm_norm_w, w_sc_out, w_sb_out, w_ssm_out, w_o, w_ffn_in, w_ffn_out, loss_target, m_mod_w, m_mod_b, m_g_pre_mix, m_g_post_mix, m_g_pre_ffn, m_g_post_ffn, m_w_in, m_sc_conv_w, m_ssm_conv_w, m_ssm_conv_b, m_ssm_dt_bias, m_ssm_a_log, m_ssm_d, m_ssm_norm_w, m_w_sc_out, m_w_sb_out, m_w_ssm_out, m_w_o, m_w_ffn_in, m_w_ffn_out, v_mod_w, v_mod_b, v_g_pre_mix, v_g_post_mix, v_g_pre_ffn, v_g_post_ffn, v_w_in, v_sc_conv_w, v_ssm_conv_w, v_ssm_conv_b, v_ssm_dt_bias, v_ssm_a_log, v_ssm_d, v_ssm_norm_w, v_w_sc_out, v_w_sb_out, v_w_ssm_out, v_w_o, v_w_ffn_in, v_w_ffn_out):
    wts = dict(mod_w=mod_w, mod_b=mod_b, g_pre_mix=g_pre_mix, g_post_mix=g_post_mix, g_pre_ffn=g_pre_ffn,
               g_post_ffn=g_post_ffn, w_in=w_in, sc_conv_w=sc_conv_w, ssm_conv_w=ssm_conv_w, ssm_conv_b=ssm_conv_b,
               ssm_dt_bias=ssm_dt_bias, ssm_a_log=ssm_a_log, ssm_d=ssm_d, ssm_norm_w=ssm_norm_w, w_sc_out=w_sc_out,
               w_sb_out=w_sb_out, w_ssm_out=w_ssm_out, w_o=w_o, w_ffn_in=w_ffn_in, w_ffn_out=w_ffn_out)
    ms = dict(mod_w=m_mod_w, mod_b=m_mod_b, g_pre_mix=m_g_pre_mix, g_post_mix=m_g_post_mix, g_pre_ffn=m_g_pre_ffn,
              g_post_ffn=m_g_post_ffn, w_in=m_w_in, sc_conv_w=m_sc_conv_w, ssm_conv_w=m_ssm_conv_w,
              ssm_conv_b=m_ssm_conv_b, ssm_dt_bias=m_ssm_dt_bias, ssm_a_log=m_ssm_a_log, ssm_d=m_ssm_d,
              ssm_norm_w=m_ssm_norm_w, w_sc_out=m_w_sc_out, w_sb_out=m_w_sb_out, w_ssm_out=m_w_ssm_out, w_o=m_w_o,
              w_ffn_in=m_w_ffn_in, w_ffn_out=m_w_ffn_out)
    vs = dict(mod_w=v_mod_w, mod_b=v_mod_b, g_pre_mix=v_g_pre_mix, g_post_mix=v_g_post_mix, g_pre_ffn=v_g_pre_ffn,
              g_post_ffn=v_g_post_ffn, w_in=v_w_in, sc_conv_w=v_sc_conv_w, ssm_conv_w=v_ssm_conv_w,
              ssm_conv_b=v_ssm_conv_b, ssm_dt_bias=v_ssm_dt_bias, ssm_a_log=v_ssm_a_log, ssm_d=v_ssm_d,
              ssm_norm_w=v_ssm_norm_w, w_sc_out=v_w_sc_out, w_sb_out=v_w_sb_out, w_ssm_out=v_w_ssm_out, w_o=v_w_o,
              w_ffn_in=v_w_ffn_in, w_ffn_out=v_w_ffn_out)
    me = 4 * lax.axis_index("x") + 2 * lax.axis_index("y") + lax.axis_index("c")
    mod_cols = mod_w.shape[2]

    pack1, sizes1 = _pack_rows([c, sc_conv_w, ssm_conv_w], F32, 8)
    got1 = all_gather_multi([pack1], "gather_c_conv")[0].reshape(N_DEV, -1)
    c_all, sc_g, ssm_g = _unpack(got1, sizes1, [(D_MODEL,), sc_conv_w.shape, ssm_conv_w.shape])
    conv = dict(sc_conv_w=_gathered_to_full(sc_g, False), ssm_conv_w=_gathered_to_full(ssm_g, False))

    mod_b_shard = lax.dynamic_slice_in_dim(mod_b, me * mod_cols, mod_cols, axis=1).reshape(DEPTH, 1, mod_cols)
    mod_sh = mod_shard_fwd(c_all, mod_w, mod_b_shard, "mod_shard_fwd")
    pack2, sizes2 = _pack_rows([mod_sh], F32, 8)
    got2 = all_gather_multi([pack2], "gather_mod")[0].reshape(N_DEV, -1)
    mod_all = _unpack(got2, sizes2, [mod_sh.shape])[0]
    mod_mine = lax.dynamic_index_in_dim(mod_all, me, axis=2, keepdims=False)
    mod = mod_mine.transpose(1, 0, 2).reshape(DEPTH, 6, D_MODEL)

    mm_names = [k for k in _BIG if k not in _CONV]
    gathered = all_gather_multi([wts[k].astype(BF16) for k in mm_names], "gather_weights")
    big = {k: _gathered_to_full(g, k in _ROW_SHARDED) for k, g in zip(mm_names, gathered)}

    small = {k: wts[k] for k in _SMALL}
    loss_part, dx, dmod, grads = _local_step(x[0], loss_target[0], mod, small, conv, big)
    loss = lax.psum(loss_part, ("x", "y", "c"))

    small_parts = [dmod.reshape(DEPTH, 6 * D_MODEL)] + [grads[k] for k in _SMALL[1:]]
    pack5, sizes5 = _pack_rows(small_parts, F32, 8)
    pack_conv, sizes_conv = _pack_rows([grads[k] for k in _CONV], F32, 8)
    got5, got_conv = all_gather_multi([pack5, pack_conv], "gather_small_grads")
    w5, _ = _pack_rows([wts[k] for k in _SMALL], F32, 8)
    m5, _ = _pack_rows([ms[k] for k in _SMALL], F32, 8)
    v5, _ = _pack_rows([vs[k] for k in _SMALL], F32, 8)
    res5 = adamw_flat(got5, w5, m5, v5, "adamw_small")
    small_out = [_unpack(r.reshape(-1), sizes5, [wts[k].shape for k in _SMALL]) for r in res5]

    conv_full = _unpack(got_conv.reshape(N_DEV, -1), sizes_conv, [grads[k].shape for k in _CONV])
    conv_mine = [lax.dynamic_slice_in_dim(g, me * wts[k].shape[2], wts[k].shape[2], axis=3)
                 for k, g in zip(_CONV, conv_full)]
    conv_slot_sizes = [math.prod(wts[k].shape) for k in _CONV]
    conv_slots = jnp.concatenate([g.reshape(N_DEV, -1) for g in conv_mine], axis=1)
    pad_c = -conv_slots.shape[1] % (8 * LANES)
    conv_slots = jnp.pad(conv_slots, ((0, 0), (0, pad_c))).reshape(N_DEV, -1, LANES)
    wc, _ = _pack_rows([wts[k] for k in _CONV], F32, 8)
    mc, _ = _pack_rows([ms[k] for k in _CONV], F32, 8)
    vc, _ = _pack_rows([vs[k] for k in _CONV], F32, 8)
    res_c = adamw_flat(conv_slots, wc, mc, vc, "adamw_conv")
    conv_out = [_unpack(r.reshape(-1), conv_slot_sizes, [wts[k].shape for k in _CONV]) for r in res_c]

    dmod_all = got5.reshape(N_DEV, -1)[:, :DEPTH * 6 * D_MODEL].reshape(N_DEV, DEPTH, 6 * D_MODEL)
    dmod_shard = lax.dynamic_slice_in_dim(dmod_all, me * mod_cols, mod_cols, axis=2).transpose(1, 0, 2)
    g_mod_w = mod_w_grad(c_all, dmod_shard, "mod_w_grad")
    rows2 = lambda a: a.reshape(a.shape[0] * a.shape[1], a.shape[2])
    res_mw = adamw_flat(rows2(g_mod_w)[None], rows2(mod_w), rows2(m_mod_w), rows2(v_mod_w), "adamw_mod_w")
    mod_w_out = [r.reshape(mod_w.shape) for r in res_mw]

    cidx = lax.axis_index("c")
    keeps, gives = [], []
    for k in mm_names:
        slots = _full_to_slots(grads[k].astype(BF16), k in _ROW_SHARDED)
        by_chip = slots.reshape(4, 2, *slots.shape[1:])
        keeps.append(lax.dynamic_index_in_dim(by_chip, cidx, 1, keepdims=False))
        gives.append(lax.dynamic_index_in_dim(by_chip, 1 - cidx, 1, keepdims=False))
    gots = swap_with_sibling(gives, "swap_grads")
    pairs = []
    for k, keep, got in zip(mm_names, keeps, gots):
        rows4 = (4 * keep.shape[1], keep.shape[2])
        pairs.append(add_pairs(keep.reshape(rows4), got.reshape(rows4), f"add_pairs_{k}").reshape(keep.shape))
    recvs = exchange_chips(pairs, "exchange_grads")
    big_out = {}
    for k, recv in zip(mm_names, recvs):
        res = adamw_flat(recv, rows2(wts[k]), rows2(ms[k]), rows2(vs[k]), f"adamw_{k}")
        big_out[k] = [r.reshape(wts[k].shape) for r in res]

    outs = []
    for kind in range(4):
        by_name = {"mod_w": mod_w_out[kind]}
        by_name.update(zip(_SMALL, small_out[kind]))
        by_name.update(zip(_CONV, conv_out[kind]))
        by_name.update({k: v[kind] for k, v in big_out.items()})
        outs.extend(by_name[k] for k in _WEIGHTS)
    return (loss, dx[None], *outs)
```

```python
import math

import jax
import jax.numpy as jnp
from jax import lax
from jax.experimental import pallas as pl
from jax.experimental.pallas import tpu as pltpu

F32 = jnp.float32
BF16 = jnp.bfloat16
HI = lax.Precision.HIGHEST

N_DEV = 8
D_MODEL = 1024
DEPTH = 2
SC_WIDTH = 256
SB_WIDTH = 256
SB_HEAD_DIM = 64
SSM_INNER = 512
SSM_HEADS = 8
SSM_HEAD_DIM = 64
SSM_GROUPS = 2
SSM_STATE = 64
SSM_CHUNK = 256
SSM_CONV_DIM = 768
FFN_HIDDEN = 2816
NORM_EPS = 1e-6
IN_PROJ = 5896
LANES = 128
VMEM_LIMIT = 56 * 1024 * 1024

OFF_GATES = 0
OFF_SC = 3072
OFF_QKV = 3840
OFF_XBC = 4608
OFF_DT = 5376
OFF_Z = 5632
IN_PAD = 6144

ADAM_LR = 0.001
ADAM_B1 = 0.9
ADAM_B2 = 0.999
ADAM_EPS = 1e-08
ADAM_WD = 0.01
ADAM_STEP = 10

MESH_ID = pl.DeviceIdType.MESH


def _cparams(sem=None):
    return pltpu.CompilerParams(dimension_semantics=sem, vmem_limit_bytes=VMEM_LIMIT)


def _my_pos():
    return lax.axis_index("x"), lax.axis_index("y"), lax.axis_index("c")


def all_gather_multi(blocks, name):
    n = len(blocks)

    def body(*refs):
        x_refs, o_refs = refs[:n], refs[n:2 * n]
        send_sems, recv_sems, local_sems = refs[2 * n:]
        x, y, c = _my_pos()
        me, sibling = (x, y, c), (x, y, 1 - c)
        chips = [(1 - x, y), (x, 1 - y), (1 - x, 1 - y)]

        def slot(a, px, py, pc):
            return o_refs[a].at[4 * px + 2 * py + pc]

        def copy(a, k, blk, to, src=None):
            return pltpu.make_async_remote_copy(
                src_ref=slot(a, *blk) if src is None else src, dst_ref=slot(a, *blk),
                send_sem=send_sems.at[7 * a + k], recv_sem=recv_sems.at[7 * a + k], device_id=to, device_id_type=MESH_ID)

        mine = [pltpu.make_async_copy(x_refs[a], slot(a, *me), local_sems.at[a]) for a in range(n)]
        for cp in mine:
            cp.start()
        first = [copy(a, 1 + j, me, (*chip, c), src=x_refs[a]) for j, chip in enumerate(chips) for a in range(n)]
        first += [copy(a, 0, me, sibling, src=x_refs[a]) for a in range(n)]
        for cp in first:
            cp.start()
        passed = []
        for j, chip in enumerate(chips):
            for a in range(n):
                copy(a, 1 + j, (*chip, c), me).wait_recv()
                fwd = copy(a, 4 + j, (*chip, c), sibling)
                fwd.start()
                passed.append(fwd)
        for a in range(n):
            copy(a, 0, sibling, me).wait_recv()
            for j, chip in enumerate(chips):
                copy(a, 4 + j, (*chip, 1 - c), me).wait_recv()
        for cp in first + passed:
            cp.wait_send()
        for cp in mine:
            cp.wait()

    any_spec = pl.BlockSpec(memory_space=pl.ANY)
    return pl.pallas_call(
        body, name=name,
        out_shape=[jax.ShapeDtypeStruct((N_DEV,) + b.shape, b.dtype) for b in blocks],
        in_specs=[any_spec] * n, out_specs=[any_spec] * n,
        scratch_shapes=[pltpu.SemaphoreType.DMA((7 * n,)), pltpu.SemaphoreType.DMA((7 * n,)), pltpu.SemaphoreType.DMA((n,))],
    )(*blocks)


def swap_with_sibling(gives, name):
    n = len(gives)

    def body(*refs):
        g_refs, r_refs = refs[:n], refs[n:2 * n]
        send_sems, recv_sems = refs[2 * n:]
        x, y, c = _my_pos()
        copies = [pltpu.make_async_remote_copy(
            src_ref=g_refs[a], dst_ref=r_refs[a], send_sem=send_sems.at[a], recv_sem=recv_sems.at[a],
            device_id=(x, y, 1 - c), device_id_type=MESH_ID) for a in range(n)]
        for cp in copies:
            cp.start()
        for cp in copies:
            cp.wait_recv()
        for cp in copies:
            cp.wait_send()

    any_spec = pl.BlockSpec(memory_space=pl.ANY)
    return pl.pallas_call(
        body, name=name, out_shape=[jax.ShapeDtypeStruct(g.shape, g.dtype) for g in gives],
        in_specs=[any_spec] * n, out_specs=[any_spec] * n,
        scratch_shapes=[pltpu.SemaphoreType.DMA((n,)), pltpu.SemaphoreType.DMA((n,))],
    )(*gives)


def exchange_chips(sends, name):
    n = len(sends)

    def body(*refs):
        s_refs, r_refs = refs[:n], refs[n:2 * n]
        send_sems, recv_sems, local_sems = refs[2 * n:]
        x, y, c = _my_pos()
        me = 2 * x + y
        mine = [pltpu.make_async_copy(s_refs[a].at[me], r_refs[a].at[me], local_sems.at[a]) for a in range(n)]
        for cp in mine:
            cp.start()
        copies = []
        for k in (2, 1, 3):
            px, py = x ^ (k >> 1), y ^ (k & 1)
            for a in range(n):
                cp = pltpu.make_async_remote_copy(
                    src_ref=s_refs[a].at[2 * px + py], dst_ref=r_refs[a].at[me],
                    send_sem=send_sems.at[3 * a + k - 1], recv_sem=recv_sems.at[3 * a + k - 1],
                    device_id=(px, py, c), device_id_type=MESH_ID)
                cp.start()
                copies.append(cp)
        for cp in copies:
            cp.wait_recv()
        for cp in copies:
            cp.wait_send()
        for cp in mine:
            cp.wait()

    any_spec = pl.BlockSpec(memory_space=pl.ANY)
    return pl.pallas_call(
        body, name=name, out_shape=[jax.ShapeDtypeStruct(s.shape, s.dtype) for s in sends],
        in_specs=[any_spec] * n, out_specs=[any_spec] * n,
        scratch_shapes=[pltpu.SemaphoreType.DMA((3 * n,)), pltpu.SemaphoreType.DMA((3 * n,)), pltpu.SemaphoreType.DMA((n,))],
    )(*sends)


def add_pairs(a, b, name, tr=512):
    rows, cols = a.shape
    tr = max(d for d in range(16, min(tr, rows) + 1, 16) if rows % d == 0)

    def body(a_ref, b_ref, o_ref):
        o_ref[...] = (a_ref[...].astype(F32) + b_ref[...].astype(F32)).astype(BF16)

    tile = pl.BlockSpec((tr, cols), lambda i: (i, 0))
    return pl.pallas_call(body, name=name, grid=(rows // tr,), in_specs=[tile, tile], out_specs=tile,
                          out_shape=jax.ShapeDtypeStruct((rows, cols), BF16), compiler_params=_cparams(("parallel",)))(a, b)


def _pack_rows(parts, dtype, row_multiple):
    flat = [p.astype(dtype).reshape(-1) for p in parts]
    sizes = [f.shape[0] for f in flat]
    total = sum(sizes)
    quantum = LANES * row_multiple
    padded = -(-total // quantum) * quantum
    if padded > total:
        flat.append(jnp.zeros((padded - total,), dtype))
    return jnp.concatenate(flat).reshape(padded // LANES, LANES), sizes


def _unpack(flat, sizes, shapes):
    out, off = [], 0
    lead = flat.shape[:-1]
    for n, shp in zip(sizes, shapes):
        out.append(flat[..., off:off + n].reshape(lead + tuple(shp)))
        off += n
    return out


def rows_call(name, body, n_rows, tr, ins, outs, scratch=(), aliases=None):
    n_tiles = n_rows // tr
    assert n_tiles * tr == n_rows
    in_specs, arrays = [], []
    for arr, kind in ins:
        arrays.append(arr)
        if kind == "row":
            in_specs.append(pl.BlockSpec((tr, arr.shape[1]), lambda i: (i, 0)))
        elif kind == "any":
            in_specs.append(pl.BlockSpec(memory_space=pl.ANY))
        elif kind == "full":
            in_specs.append(pl.BlockSpec(arr.shape, lambda i, nd=arr.ndim: (0,) * nd))
        elif kind[0] == "row":
            _, w, ci = kind
            in_specs.append(pl.BlockSpec((tr, w), lambda i, ci=ci: (i, ci)))
        elif kind[0] == "prev8":
            _, w, ci = kind
            hr = 8 * (4 // arr.dtype.itemsize)
            in_specs.append(pl.BlockSpec((hr, w), lambda i, ci=ci, hr=hr: (jnp.maximum(i * (tr // hr) - 1, 0), ci)))
        elif kind[0] == "next8":
            _, w, ci = kind
            hr = 8 * (4 // arr.dtype.itemsize)
            last = n_rows // hr - 1
            in_specs.append(pl.BlockSpec((hr, w), lambda i, ci=ci, last=last, hr=hr: (jnp.minimum((i + 1) * (tr // hr), last), ci)))
        else:
            raise ValueError(kind)
    out_specs, out_shapes = [], []
    for shape, dtype, kind in outs:
        out_shapes.append(jax.ShapeDtypeStruct(shape, dtype))
        if kind == "row":
            out_specs.append(pl.BlockSpec((tr, shape[1]), lambda i: (i, 0)))
        elif kind[0] == "row":
            _, w, ci = kind
            out_specs.append(pl.BlockSpec((tr, w), lambda i, ci=ci: (i, ci)))
        else:
            out_specs.append(pl.BlockSpec(shape, lambda i, nd=len(shape): (0,) * nd))
    has_acc = any(k == "acc" for _, _, k in outs)
    return pl.pallas_call(
        body, name=name, grid=(n_tiles,), in_specs=in_specs, out_specs=out_specs, out_shape=out_shapes,
        scratch_shapes=list(scratch), input_output_aliases=dict(aliases or {}),
        compiler_params=_cparams(("arbitrary",) if has_acc else ("parallel",)),
    )(*arrays)


def _prev8(ref):
    n = ref.shape[0]
    return ref[n - 8:n, :].astype(F32)


def _next8(ref):
    return ref[0:8, :].astype(F32)


def _acc(ref, val):
    @pl.when(pl.program_id(0) == 0)
    def _():
        ref[...] = jnp.zeros_like(ref)
    ref[...] += val


def _rstd(x):
    return lax.rsqrt(jnp.mean(x * x, axis=-1, keepdims=True) + NORM_EPS)


def _sigmoid(x):
    return 1.0 / (1.0 + jnp.exp(-x))


def _silu(x):
    return x * _sigmoid(x)


def _dsilu(x):
    s = _sigmoid(x)
    return s * (1.0 + x * (1.0 - s))


def _softplus(x):
    return jnp.maximum(x, 0.0) + jnp.log(1.0 + jnp.exp(-jnp.abs(x)))


def _log_sigmoid_neg(x):
    t = -x
    return jnp.minimum(t, 0.0) - jnp.log(1.0 + jnp.exp(jnp.minimum(x, t)))


def normmod_fwd(x, g, scale, shift, name):
    t, d = x.shape

    def body(x_ref, g_ref, sc_ref, sh_ref, h_ref):
        xv = x_ref[...]
        h = xv * _rstd(xv) * g_ref[...] * (1.0 + sc_ref[...]) + sh_ref[...]
        h_ref[...] = h.astype(BF16)

    return rows_call(name, body, t, 512, [(x, "row"), (g, "full"), (scale, "full"), (shift, "full")],
                     [((t, d), BF16, "row")])[0]


def resid_normmod_fwd(x, f, gate, g_post, g_pre, scale, shift, name):
    t, d = x.shape

    def body(x_ref, f_ref, gate_ref, gp_ref, g_ref, sc_ref, sh_ref, xo_ref, h_ref):
        fv = f_ref[...]
        xn = x_ref[...] + gate_ref[...] * (fv * _rstd(fv) * gp_ref[...])
        xo_ref[...] = xn
        h = xn * _rstd(xn) * g_ref[...] * (1.0 + sc_ref[...]) + sh_ref[...]
        h_ref[...] = h.astype(BF16)

    return rows_call(name, body, t, 512,
                     [(x, "row"), (f, "row"), (gate, "full"), (g_post, "full"), (g_pre, "full"), (scale, "full"),
                      (shift, "full")],
                     [((t, d), F32, "row"), ((t, d), BF16, "row")])


def resid_loss(x, f, gate, g_post, target, name):
    t, d = x.shape

    def body(x_ref, f_ref, gate_ref, gp_ref, tg_ref, dy_ref, loss_ref):
        fv = f_ref[...]
        yv = x_ref[...] + gate_ref[...] * (fv * _rstd(fv) * gp_ref[...])
        err = yv - tg_ref[...]
        dy_ref[...] = err * (1.0 / d)
        part = 0.5 * jnp.sum(jnp.mean(err * err, axis=-1, keepdims=True), axis=0, keepdims=True)
        _acc(loss_ref, jnp.broadcast_to(part, loss_ref.shape))

    return rows_call(name, body, t, 512,
                     [(x, "row"), (f, "row"), (gate, "full"), (g_post, "full"), (target, "row")],
                     [((t, d), F32, "row"), ((8, LANES), F32, "acc")])


def resid_bwd(dx, f, gate, g_post, name):
    t, d = dx.shape

    def body(dx_ref, f_ref, gate_ref, gp_ref, df_ref, dgate_ref, dg_ref):
        fv, dxv, gp = f_ref[...], dx_ref[...], gp_ref[...]
        r = _rstd(fv)
        fn = fv * r
        _acc(dgate_ref, jnp.sum(dxv * (fn * gp), axis=0, keepdims=True))
        dn = dxv * gate_ref[...]
        _acc(dg_ref, jnp.sum(dn * fn, axis=0, keepdims=True))
        u = dn * gp
        df = r * (u - fn * jnp.mean(fn * u, axis=-1, keepdims=True))
        df_ref[...] = df.astype(BF16)

    return rows_call(name, body, t, 512, [(dx, "row"), (f, "row"), (gate, "full"), (g_post, "full")],
                     [((t, d), BF16, "row"), ((1, d), F32, "acc"), ((1, d), F32, "acc")])


def normmod_bwd(dh, x, dx_in, g, scale, name):
    t, d = x.shape

    def body(dh_ref, x_ref, dxi_ref, g_ref, sc_ref, dx_ref, dsc_ref, dsh_ref, dg_ref):
        xv, dhv, gv = x_ref[...], dh_ref[...], g_ref[...]
        r = _rstd(xv)
        xn = xv * r
        _acc(dsc_ref, jnp.sum(dhv * (xn * gv), axis=0, keepdims=True))
        _acc(dsh_ref, jnp.sum(dhv, axis=0, keepdims=True))
        dn = dhv * (1.0 + sc_ref[...])
        _acc(dg_ref, jnp.sum(dn * xn, axis=0, keepdims=True))
        u = dn * gv
        dx_ref[...] = dxi_ref[...] + r * (u - xn * jnp.mean(xn * u, axis=-1, keepdims=True))

    return rows_call(name, body, t, 512, [(dh, "row"), (x, "row"), (dx_in, "row"), (g, "full"), (scale, "full")],
                     [((t, d), F32, "row"), ((1, d), F32, "acc"), ((1, d), F32, "acc"), ((1, d), F32, "acc")])


def normmod_resid_bwd(dh, x, dx_in, g, scale, f, gate, g_post, name):
    t, d = x.shape

    def body(dh_ref, x_ref, dxi_ref, g_ref, sc_ref, f_ref, gate_ref, gp_ref,
             dx_ref, df_ref, dsc_ref, dsh_ref, dg_ref, dgate_ref, dgp_ref):
        xv, dhv, gv = x_ref[...], dh_ref[...], g_ref[...]
        r = _rstd(xv)
        xn = xv * r
        _acc(dsc_ref, jnp.sum(dhv * (xn * gv), axis=0, keepdims=True))
        _acc(dsh_ref, jnp.sum(dhv, axis=0, keepdims=True))
        dn = dhv * (1.0 + sc_ref[...])
        _acc(dg_ref, jnp.sum(dn * xn, axis=0, keepdims=True))
        u = dn * gv
        dxv = dxi_ref[...] + r * (u - xn * jnp.mean(xn * u, axis=-1, keepdims=True))
        dx_ref[...] = dxv
        fv, gp = f_ref[...], gp_ref[...]
        rf = _rstd(fv)
        fn = fv * rf
        _acc(dgate_ref, jnp.sum(dxv * (fn * gp), axis=0, keepdims=True))
        dnf = dxv * gate_ref[...]
        _acc(dgp_ref, jnp.sum(dnf * fn, axis=0, keepdims=True))
        uf = dnf * gp
        df_ref[...] = (rf * (uf - fn * jnp.mean(fn * uf, axis=-1, keepdims=True))).astype(BF16)

    vec = ((1, d), F32, "acc")
    return rows_call(name, body, t, 512,
                     [(dh, "row"), (x, "row"), (dx_in, "row"), (g, "full"), (scale, "full"), (f, "row"), (gate, "full"),
                      (g_post, "full")],
                     [((t, d), F32, "row"), ((t, d), BF16, "row"), vec, vec, vec, vec, vec])


def _pick(n, prefs):
    for p in prefs:
        if n % p == 0:
            return p
    return n


def mm_nn(a_list, b_list, out_dtype, name, tm=1024, tn=None, tk=None):
    m, k = a_list[0].shape
    n = b_list[0].shape[1]
    tm = min(tm, m)
    tn = tn or _pick(n, (1024, 768, 512, 256, 128))
    tk = tk or _pick(k, (1024, 1408, 512, 256))
    nk = k // tk
    npair = len(a_list)

    if nk == 1 and npair == 1:
        def body1(a_ref, b_ref, o_ref):
            o_ref[...] = jnp.dot(a_ref[...], b_ref[...], preferred_element_type=F32).astype(o_ref.dtype)

        return pl.pallas_call(
            body1, name=name, grid=(m // tm, n // tn),
            in_specs=[pl.BlockSpec((tm, k), lambda i, j: (i, 0)), pl.BlockSpec((k, tn), lambda i, j: (0, j))],
            out_specs=pl.BlockSpec((tm, tn), lambda i, j: (i, j)),
            out_shape=jax.ShapeDtypeStruct((m, n), out_dtype),
            compiler_params=_cparams(("parallel", "parallel")),
        )(a_list[0], b_list[0])

    def body(*refs):
        a_refs, b_refs = refs[:npair], refs[npair:2 * npair]
        o_ref, acc = refs[2 * npair], refs[2 * npair + 1]
        kk = pl.program_id(2)

        @pl.when(kk == 0)
        def _():
            acc[...] = jnp.zeros_like(acc)

        s = acc[...]
        for a_ref, b_ref in zip(a_refs, b_refs):
            s = s + jnp.dot(a_ref[...], b_ref[...], preferred_element_type=F32)
        acc[...] = s

        @pl.when(kk == nk - 1)
        def _():
            o_ref[...] = acc[...].astype(o_ref.dtype)

    return pl.pallas_call(
        body, name=name, grid=(m // tm, n // tn, nk),
        in_specs=[pl.BlockSpec((tm, tk), lambda i, j, kk: (i, kk))] * npair
        + [pl.BlockSpec((tk, tn), lambda i, j, kk: (kk, j))] * npair,
        out_specs=pl.BlockSpec((tm, tn), lambda i, j, kk: (i, j)),
        out_shape=jax.ShapeDtypeStruct((m, n), out_dtype),
        scratch_shapes=[pltpu.VMEM((tm, tn), F32)],
        compiler_params=_cparams(("parallel", "parallel", "arbitrary")),
    )(*a_list, *b_list)


def mm_nt(a_list, b_list, b_koff, out_dtype, name, tm=1024):
    m, k = a_list[0].shape
    n = b_list[0].shape[0]
    tm = min(tm, m)
    tn = _pick(n, (1024, 512, 256))
    tk = _pick(k, (1024, 1408, 512, 256))
    nk = k // tk
    npair = len(a_list)
    koff = [o // tk for o in b_koff]
    nt_dims = (((1,), (1,)), ((), ()))

    def body(*refs):
        a_refs, b_refs = refs[:npair], refs[npair:2 * npair]
        o_ref, acc = refs[2 * npair], refs[2 * npair + 1]
        kk = pl.program_id(2)

        @pl.when(kk == 0)
        def _():
            acc[...] = jnp.zeros_like(acc)

        s = acc[...]
        for a_ref, b_ref in zip(a_refs, b_refs):
            s = s + lax.dot_general(a_ref[...], b_ref[...], nt_dims, preferred_element_type=F32)
        acc[...] = s

        @pl.when(kk == nk - 1)
        def _():
            o_ref[...] = acc[...].astype(o_ref.dtype)

    return pl.pallas_call(
        body, name=name, grid=(m // tm, n // tn, nk),
        in_specs=[pl.BlockSpec((tm, tk), lambda i, j, kk: (i, kk))] * npair
        + [pl.BlockSpec((tn, tk), lambda i, j, kk, o=o: (j, kk + o)) for o in koff],
        out_specs=pl.BlockSpec((tm, tn), lambda i, j, kk: (i, j)),
        out_shape=jax.ShapeDtypeStruct((m, n), out_dtype),
        scratch_shapes=[pltpu.VMEM((tm, tn), F32)],
        compiler_params=_cparams(("parallel", "parallel", "arbitrary")),
    )(*a_list, *b_list)


def mm_tn(a, b, name, tt=512):
    t, ka = a.shape
    n = b.shape[1]
    ta = _pick(ka, (1024, 1408, 512, 256))
    tn = _pick(n, (2048, 1024, 1408, 512, 256))
    nt = t // tt

    def body(a_ref, b_ref, o_ref, acc):
        s = pl.program_id(2)

        @pl.when(s == 0)
        def _():
            acc[...] = jnp.zeros_like(acc)

        acc[...] += lax.dot_general(a_ref[...], b_ref[...], (((0,), (0,)), ((), ())), preferred_element_type=F32)

        @pl.when(s == nt - 1)
        def _():
            o_ref[...] = acc[...].astype(BF16)

    return pl.pallas_call(
        body, name=name, grid=(ka // ta, n // tn, nt),
        in_specs=[pl.BlockSpec((tt, ta), lambda i, j, s: (s, i)), pl.BlockSpec((tt, tn), lambda i, j, s: (s, j))],
        out_specs=pl.BlockSpec((ta, tn), lambda i, j, s: (i, j)),
        out_shape=jax.ShapeDtypeStruct((ka, n), BF16),
        scratch_shapes=[pltpu.VMEM((ta, tn), F32)],
        compiler_params=_cparams(("parallel", "parallel", "arbitrary")),
    )(a, b)


def mm_swiglu_fwd(h, w_ffn_in, name, tm=512, tn=1408):
    m, k = h.shape
    nh = FFN_HIDDEN // tn

    def body(h_ref, wg_ref, wu_ref, gt_ref, up_ref, a_ref):
        hv = h_ref[...]
        gt = jnp.dot(hv, wg_ref[...], preferred_element_type=F32)
        up = jnp.dot(hv, wu_ref[...], preferred_element_type=F32)
        gt_ref[...] = gt.astype(BF16)
        up_ref[...] = up.astype(BF16)
        a_ref[...] = (_silu(gt) * up).astype(BF16)

    shp = jax.ShapeDtypeStruct((m, FFN_HIDDEN), BF16)
    ospec = pl.BlockSpec((tm, tn), lambda i, j: (i, j))
    return pl.pallas_call(
        body, name=name, grid=(m // tm, nh),
        in_specs=[pl.BlockSpec((tm, k), lambda i, j: (i, 0)), pl.BlockSpec((k, tn), lambda i, j: (0, j)),
                  pl.BlockSpec((k, tn), lambda i, j: (0, j + nh))],
        out_specs=[ospec, ospec, ospec], out_shape=[shp, shp, shp],
        compiler_params=_cparams(("parallel", "parallel")),
    )(h, w_ffn_in, w_ffn_in)


def mm_swiglu_bwd(df, w_out, gt, up, name, tm=256, sub=256):
    m, k = df.shape
    n_sub = FFN_HIDDEN // sub

    def body(df_ref, w_ref, gt_ref, up_ref, dgt_ref, dup_ref):
        dfv = df_ref[...]

        def chunk_dot(c):
            return lax.dot_general(dfv, w_ref[c * sub:(c + 1) * sub, :], (((1,), (1,)), ((), ())), preferred_element_type=F32)

        da_next = chunk_dot(0)
        for c in range(n_sub):
            da = da_next
            if c + 1 < n_sub:
                da_next = chunk_dot(c + 1)
            cols = slice(c * sub, (c + 1) * sub)
            gtv = gt_ref[:, cols].astype(F32)
            sg = _sigmoid(gtv)
            dgt_ref[:, cols] = (da * up_ref[:, cols].astype(F32) * (sg * (1.0 + gtv * (1.0 - sg)))).astype(BF16)
            dup_ref[:, cols] = (da * (gtv * sg)).astype(BF16)

    shp = jax.ShapeDtypeStruct((m, FFN_HIDDEN), BF16)
    tile = pl.BlockSpec((tm, FFN_HIDDEN), lambda i: (i, 0))
    return pl.pallas_call(
        body, name=name, grid=(m // tm,),
        in_specs=[pl.BlockSpec((tm, k), lambda i: (i, 0)), pl.BlockSpec((FFN_HIDDEN, k), lambda i: (0, 0)), tile, tile],
        out_specs=[tile, tile], out_shape=[shp, shp],
        compiler_params=_cparams(("parallel",)),
    )(df, w_out, gt, up)


def _shift_down(x, prev8, j):
    if j == 0:
        return x
    xr = pltpu.roll(x, j, 0)
    pr = pltpu.roll(prev8, j, 0)
    row = lax.broadcasted_iota(jnp.int32, (8, x.shape[1]), 0)
    head = jnp.where(row < j, pr, xr[:8])
    return head if x.shape[0] == 8 else jnp.concatenate([head, xr[8:]], axis=0)


def _shift_up(x, next8, j):
    if j == 0:
        return x
    n = x.shape[0]
    xr = pltpu.roll(x, n - j, 0)
    nr = pltpu.roll(next8, 8 - j, 0)
    row = lax.broadcasted_iota(jnp.int32, (8, x.shape[1]), 0)
    return jnp.concatenate([xr[:n - 8], jnp.where(row >= 8 - j, nr, xr[n - 8:])], axis=0)


def _conv_taps(x, prev8, w_ref, taps):
    out = None
    for k in range(taps):
        term = w_ref[k:k + 1, :] * _shift_down(x, prev8, taps - 1 - k)
        out = term if out is None else out + term
    return out


def post_inproj(p, sc_w, ssm_w, ssm_b, name, tr=512):
    t = p.shape[0]

    def body(sc_ref, scp_ref, qkv_ref, xbc_ref, xbcp_ref, scw_ref, sw_ref, sb_ref, ya_ref, qkvo_ref, act_ref):
        first = (pl.program_id(0) > 0).astype(F32)
        sc = sc_ref[...].astype(F32)
        scp = _prev8(scp_ref) * first
        u = sc[:, 256:512] * sc[:, 512:768]
        up = scp[:, 256:512] * scp[:, 512:768]
        ya_ref[...] = (sc[:, 0:256] * _conv_taps(u, up, scw_ref, 3)).astype(BF16)
        qkv = qkv_ref[...]
        qkvo_ref[:, 0:256] = (qkv[:, 0:256].astype(F32) * 0.125).astype(BF16)
        qkvo_ref[:, 256:768] = qkv[:, 256:768].astype(BF16)
        xc = _conv_taps(xbc_ref[...].astype(F32), _prev8(xbcp_ref) * first, sw_ref, 4) + sb_ref[...]
        act_ref[...] = _silu(xc)

    return rows_call(
        name, body, t, tr,
        [(p, ("row", 768, OFF_SC // 768)), (p, ("prev8", 768, OFF_SC // 768)), (p, ("row", 768, OFF_QKV // 768)),
         (p, ("row", 768, OFF_XBC // 768)), (p, ("prev8", 768, OFF_XBC // 768)),
         (sc_w, "full"), (ssm_w, "full"), (ssm_b, "full")],
        [((t, 256), BF16, "row"), ((t, 768), BF16, "row"), ((t, 768), F32, "row")])


def branch_out_fwd(ya, yb, yc, p, w_cat, name, tr=256):
    t = p.shape[0]

    def body(ya_ref, yb_ref, yc_ref, gl_ref, w_ref, o_ref):
        y_a = jnp.dot(ya_ref[...], w_ref[0:256, :], preferred_element_type=F32)
        y_b = jnp.dot(yb_ref[...].astype(BF16), w_ref[256:512, :], preferred_element_type=F32)
        y_c = jnp.dot(yc_ref[...], w_ref[512:1024, :], preferred_element_type=F32)
        m = (_sigmoid(gl_ref[:, 0:1024].astype(F32)) * y_a + _sigmoid(gl_ref[:, 1024:2048].astype(F32)) * y_b
             + _sigmoid(gl_ref[:, 2048:3072].astype(F32)) * y_c)
        o_ref[...] = m.astype(BF16)

    return rows_call(name, body, t, tr,
                     [(ya, "row"), (yb, "row"), (yc, "row"), (p, ("row", 3072, 0)), (w_cat, "full")],
                     [((t, D_MODEL), BF16, "row")])[0]


def branch_out_bwd(dm, ya, yb, yc, p, w_cat, name, tr=256):
    t = p.shape[0]
    tn_dims = (((0,), (0,)), ((), ()))

    def body(dm_ref, ya_ref, yb_ref, yc_ref, gl_ref, w_ref, dgl_ref, dya_ref, dyb_ref, dyc_ref, dw_ref):
        @pl.when(pl.program_id(0) == 0)
        def _():
            dw_ref[...] = jnp.zeros_like(dw_ref)

        dmv = dm_ref[...]
        ins = (ya_ref[...], yb_ref[...].astype(BF16), yc_ref[...])
        rows = ((0, 256), (256, 512), (512, 1024))
        outs = (dya_ref, dyb_ref, dyc_ref)
        for i in range(3):
            r0, r1 = rows[i]
            y = jnp.dot(ins[i], w_ref[r0:r1, :], preferred_element_type=F32)
            s = _sigmoid(gl_ref[:, 1024 * i:1024 * (i + 1)].astype(F32))
            dgl_ref[:, 1024 * i:1024 * (i + 1)] = (dmv * y * s * (1.0 - s)).astype(BF16)
            dy = (dmv * s).astype(BF16)
            outs[i][...] = lax.dot_general(dy, w_ref[r0:r1, :], _NT, preferred_element_type=F32)
            dw_ref[r0:r1, :] += lax.dot_general(ins[i], dy, tn_dims, preferred_element_type=F32)

    return rows_call(name, body, t, tr,
                     [(dm, "row"), (ya, "row"), (yb, "row"), (yc, "row"), (p, ("row", 3072, 0)), (w_cat, "full")],
                     [((t, IN_PAD), BF16, ("row", 3072, 0)), ((t, 256), F32, "row"), ((t, 256), F32, "row"),
                      ((t, 512), F32, "row"), ((D_MODEL, D_MODEL), F32, "acc")])


def assemble_dp(dp, dya, p, sc_w, dq, dk, dv, dact, ssm_w, ssm_b, ddt, dz, name, tr=256):
    t = p.shape[0]
    n_tiles = t // tr
    sci, xi = OFF_SC // 768, OFF_XBC // 768
    base = OFF_SC
    assert base == IN_PAD - base
    o_sc, o_qkv, o_xbc, o_dt, o_z, o_end = (c - base for c in (OFF_SC, OFF_QKV, OFF_XBC, OFF_DT, OFF_Z, IN_PAD))

    def body(dp_ref, dya_ref, dyan_ref, sc_ref, scp_ref, scn_ref, scw_ref, dq_ref, dk_ref, dv_ref,
             dact_ref, dactn_ref, xbc_ref, xbcp_ref, xbcn_ref, sw_ref, sb_ref, ddt_ref, dz_ref,
             o_ref, dscw_ref, dsw_ref, dsb_ref):
        i = pl.program_id(0)

        @pl.when(i == 0)
        def _():
            dscw_ref[...] = jnp.zeros_like(dscw_ref)
            dsw_ref[...] = jnp.zeros_like(dsw_ref)
            dsb_ref[...] = jnp.zeros_like(dsb_ref)

        first = (i > 0).astype(F32)
        last = (i < n_tiles - 1).astype(F32)
        del dp_ref
        sc = sc_ref[...].astype(F32)
        scp = _prev8(scp_ref) * first
        scn = _next8(scn_ref) * last
        u = sc[:, 256:512] * sc[:, 512:768]
        up = scp[:, 256:512] * scp[:, 512:768]
        dya_v = dya_ref[...]
        cv = _conv_taps(u, up, scw_ref, 3)
        o_ref[:, o_sc:o_sc + 256] = (dya_v * cv).astype(BF16)
        dcv = dya_v * sc[:, 0:256]
        dcvn = _next8(dyan_ref) * last * scn[:, 0:256]
        du = None
        for k in range(3):
            sh = 2 - k
            term = scw_ref[k:k + 1, :] * _shift_up(dcv, dcvn, sh)
            du = term if du is None else du + term
            dscw_ref[k:k + 1, :] += jnp.sum(dcv * _shift_down(u, up, sh), axis=0, keepdims=True)
        o_ref[:, o_sc + 256:o_sc + 512] = (du * sc[:, 512:768]).astype(BF16)
        o_ref[:, o_sc + 512:o_sc + 768] = (du * sc[:, 256:512]).astype(BF16)
        o_ref[:, o_qkv:o_qkv + 256] = (dq_ref[...] * 0.125).astype(BF16)
        o_ref[:, o_qkv + 256:o_qkv + 512] = dk_ref[...].astype(BF16)
        o_ref[:, o_qkv + 512:o_qkv + 768] = dv_ref[...].astype(BF16)
        xb = xbc_ref[...].astype(F32)
        xbp = _prev8(xbcp_ref) * first
        xbn = _next8(xbcn_ref)
        xc = _conv_taps(xb, xbp, sw_ref, 4) + sb_ref[...]
        xcn = _conv_taps(xbn, xb[tr - 8:, :], sw_ref, 4) + sb_ref[...]
        dxc = dact_ref[...] * _dsilu(xc)
        dxcn = _next8(dactn_ref) * _dsilu(xcn) * last
        dxb = None
        for k in range(4):
            sh = 3 - k
            term = sw_ref[k:k + 1, :] * _shift_up(dxc, dxcn, sh)
            dxb = term if dxb is None else dxb + term
            dsw_ref[k:k + 1, :] += jnp.sum(dxc * _shift_down(xb, xbp, sh), axis=0, keepdims=True)
        dsb_ref[...] += jnp.sum(dxc, axis=0, keepdims=True)
        o_ref[:, o_xbc:o_xbc + 768] = dxb.astype(BF16)
        o_ref[:, o_dt:o_dt + 128] = ddt_ref[...].astype(BF16)
        o_ref[:, o_dt + 128:o_z] = jnp.zeros((tr, o_z - o_dt - 128), BF16)
        o_ref[:, o_z:o_end] = dz_ref[...].astype(BF16)

    return rows_call(
        name, body, t, tr,
        [(dp, "any"), (dya, "row"), (dya, ("next8", 256, 0)),
         (p, ("row", 768, sci)), (p, ("prev8", 768, sci)), (p, ("next8", 768, sci)), (sc_w, "full"),
         (dq, "row"), (dk, "row"), (dv, "row"),
         (dact, "row"), (dact, ("next8", 768, 0)),
         (p, ("row", 768, xi)), (p, ("prev8", 768, xi)), (p, ("next8", 768, xi)), (ssm_w, "full"), (ssm_b, "full"),
         (ddt, "row"), (dz, "row")],
        [((t, IN_PAD), BF16, ("row", IN_PAD - base, 1)), ((8, 256), F32, "acc"), ((8, 768), F32, "acc"),
         ((1, 768), F32, "acc")], aliases={0: 0})


def adamw_flat(slots, w, m, v, name, tr=512):
    n_slots, rows, lanes = slots.shape
    tr = max(d for d in range(8, min(tr, rows) + 1, 8) if rows % d == 0) if rows % 8 == 0 else rows
    bc1 = 1.0 - ADAM_B1 ** ADAM_STEP
    bc2 = 1.0 - ADAM_B2 ** ADAM_STEP

    def body(s_ref, w_ref, m_ref, v_ref, g_ref, d_ref, mo_ref, vo_ref):
        g = s_ref[0].astype(F32)
        for k in range(1, n_slots):
            g = g + s_ref[k].astype(F32)
        mn = ADAM_B1 * m_ref[...] + (1.0 - ADAM_B1) * g
        vn = ADAM_B2 * v_ref[...] + (1.0 - ADAM_B2) * (g * g)
        m_hat = mn / bc1
        v_hat = vn / bc2
        g_ref[...] = g
        d_ref[...] = -ADAM_LR * (m_hat / (jnp.sqrt(v_hat) + ADAM_EPS) + ADAM_WD * w_ref[...])
        mo_ref[...] = mn
        vo_ref[...] = vn

    tile = pl.BlockSpec((tr, lanes), lambda i: (i, 0))
    shp = jax.ShapeDtypeStruct((rows, lanes), F32)
    return pl.pallas_call(
        body, name=name, grid=(rows // tr,),
        in_specs=[pl.BlockSpec((n_slots, tr, lanes), lambda i: (0, i, 0)), tile, tile, tile],
        out_specs=[tile] * 4, out_shape=[shp] * 4,
        compiler_params=_cparams(("parallel",)),
    )(slots, w, m, v)


def _split_dot(x, tri):
    hi = x.astype(BF16)
    lo = (x - hi.astype(F32)).astype(BF16)
    return jnp.dot(hi, tri, preferred_element_type=F32) + jnp.dot(lo, tri, preferred_element_type=F32)


_NT = (((1,), (1,)), ((), ()))
_TN = (((0,), (0,)), ((), ()))

SBA_EXP_ZERO = -104.0
SBA_SKIPPED = -1e30


def sba_fwd(qkv, name, bq=256, bk=256):
    t = qkv.shape[0]
    ratio = bq // bk
    assert bq == ratio * bk and t // bk <= LANES

    def body(q_ref, k_ref, v_ref, o_ref, runs_ref, acc_s, run_s):
        i = pl.program_id(1)
        lane = lax.broadcasted_iota(jnp.int32, (1, LANES), 1)
        lane_q = lax.broadcasted_iota(jnp.int32, (bq, LANES), 1)
        qi = lax.broadcasted_iota(jnp.int32, (bq, bk), 0) + i * bq
        kj = lax.broadcasted_iota(jnp.int32, (bq, bk), 1)
        later = (lax.broadcasted_iota(jnp.int32, (bk, bk), 0) > lax.broadcasted_iota(jnp.int32, (bk, bk), 1)).astype(BF16)
        qv = q_ref[...]
        qms = [jnp.where(hm, qv, jnp.zeros_like(qv)) for hm in (lane < 64, lane >= 64)]
        acc_s[...] = jnp.zeros_like(acc_s)
        run_s[...] = jnp.zeros_like(run_s)
        runs_ref[...] = jnp.full(runs_ref.shape, SBA_SKIPPED, F32)

        def tiles(specs):
            chains = [(ti, hh) for ti in range(len(specs)) for hh in range(2)]
            kb = [k_ref[pl.ds(pl.multiple_of(j * bk, bk), bk), :] for j, _ in specs]
            vb = [v_ref[pl.ds(pl.multiple_of(j * bk, bk), bk), :] for j, _ in specs]
            mask = [(kj + j * bk) < qi if masked else None for j, masked in specs]
            s = {c: lax.dot_general(qms[c[1]], kb[c[0]], _NT, preferred_element_type=F32) for c in chains}
            lk = {c: _log_sigmoid_neg(s[c]) for c in chains}
            lk = {c: lk[c] if mask[c[0]] is None else jnp.where(mask[c[0]], lk[c], 0.0) for c in chains}
            w = {c: jnp.dot(lk[c].astype(BF16), later, preferred_element_type=F32) for c in chains}
            run = {}
            for hh in range(2):
                carry = run_s[hh]
                for ti in range(len(specs)):
                    run[ti, hh] = carry
                    carry = carry + jnp.sum(lk[ti, hh], axis=1, keepdims=True)
                run_s[hh] = carry
            a = {c: jnp.exp(s[c] + lk[c] + w[c] + run[c]) for c in chains}
            a = {c: a[c] if mask[c[0]] is None else jnp.where(mask[c[0]], a[c], 0.0) for c in chains}
            for ti, hh in chains:
                acc_s[hh] += jnp.dot(a[ti, hh].astype(BF16), vb[ti], preferred_element_type=F32)
                runs_ref[hh] = jnp.where(lane_q == specs[ti][0], run[ti, hh], runs_ref[hh])

        for d in range(ratio):
            tiles([((i + 1) * ratio - 1 - d, True)])

        def live():
            return jnp.maximum(jnp.max(run_s[0]), jnp.max(run_s[1])) >= SBA_EXP_ZERO

        def cond(state):
            n, go = state
            return jnp.logical_and(n < i * ratio, go)

        def step(state):
            n, _ = state
            tiles([(i * ratio - 1 - n, False)])
            return n + 1, live()

        lax.while_loop(cond, step, (jnp.int32(0), live()))
        o_ref[...] = jnp.where(lane < 64, acc_s[0], acc_s[1])

    return pl.pallas_call(
        body, name=name, grid=(2, t // bq),
        in_specs=[pl.BlockSpec((bq, LANES), lambda p, i: (i, p)), pl.BlockSpec((t, LANES), lambda p, i: (0, 2 + p)),
                  pl.BlockSpec((t, LANES), lambda p, i: (0, 4 + p))],
        out_specs=[pl.BlockSpec((bq, LANES), lambda p, i: (i, p)), pl.BlockSpec((2, bq, LANES), lambda p, i: (p, i, 0))],
        out_shape=[jax.ShapeDtypeStruct((t, SB_WIDTH), F32), jax.ShapeDtypeStruct((4, t, LANES), F32)],
        scratch_shapes=[pltpu.VMEM((2, bq, LANES), F32), pltpu.VMEM((2, bq, 1), F32)],
        compiler_params=_cparams(("parallel", "parallel")),
    )(qkv, qkv, qkv)


def sba_bwd(qkv, runs, do, name, bq=256, bk=256):
    t = qkv.shape[0]
    ratio = bq // bk
    assert bq == ratio * bk
    nq = t // bq

    def body(q_ref, k_ref, v_ref, runs_ref, do_ref, dq_ref, dk_hbm, dv_hbm, dk_s, dv_s, sem, dq_s, rg_s):
        p = pl.program_id(0)
        i = pl.program_id(1)

        @pl.when(i == 0)
        def _():
            dk_s[...] = jnp.zeros_like(dk_s)
            dv_s[...] = jnp.zeros_like(dv_s)

        lane = lax.broadcasted_iota(jnp.int32, (1, LANES), 1)
        qi = lax.broadcasted_iota(jnp.int32, (bq, bk), 0) + i * bq
        kj = lax.broadcasted_iota(jnp.int32, (bq, bk), 1)
        r2 = lax.broadcasted_iota(jnp.int32, (bk, bk), 0)
        c2 = lax.broadcasted_iota(jnp.int32, (bk, bk), 1)
        later = (r2 > c2).astype(BF16)
        earlier = (r2 < c2).astype(BF16)
        qv = q_ref[...]
        dov = do_ref[...]
        heads = range(2)
        hms = (lane < 64, lane >= 64)
        qms = [jnp.where(hm, qv, jnp.zeros_like(qv)) for hm in hms]
        doms = [jnp.where(hm, dov, 0.0).astype(BF16) for hm in hms]
        runs = [runs_ref[hh] for hh in heads]
        dq_s[...] = jnp.zeros_like(dq_s)
        rg_s[...] = jnp.zeros_like(rg_s)

        def tiles(specs):
            nt = len(specs)
            chains = [(ti, hh) for ti in range(nt) for hh in heads]
            starts = [pl.multiple_of(j * bk, bk) for j, _ in specs]
            kb = [k_ref[pl.ds(st, bk), :] for st in starts]
            vb = [v_ref[pl.ds(st, bk), :] for st in starts]
            mask = [(kj + j * bk) < qi if masked else None for j, masked in specs]
            s = {c: lax.dot_general(qms[c[1]], kb[c[0]], _NT, preferred_element_type=F32) for c in chains}
            da = {c: lax.dot_general(doms[c[1]], vb[c[0]], _NT, preferred_element_type=F32) for c in chains}
            lk_raw = {c: _log_sigmoid_neg(s[c]) for c in chains}
            lk = {c: lk_raw[c] if mask[c[0]] is None else jnp.where(mask[c[0]], lk_raw[c], 0.0) for c in chains}
            w = {c: jnp.dot(lk[c].astype(BF16), later, preferred_element_type=F32) for c in chains}
            run = {c: jnp.sum(jnp.where(lane == specs[c[0]][0], runs[c[1]], 0.0), axis=1, keepdims=True) for c in chains}
            a = {c: jnp.exp(s[c] + lk[c] + w[c] + run[c]) for c in chains}
            a = {c: a[c] if mask[c[0]] is None else jnp.where(mask[c[0]], a[c], 0.0) for c in chains}
            g = {c: a[c] * da[c] for c in chains}
            rg = {}
            for hh in heads:
                carry = rg_s[hh]
                for ti in range(nt):
                    rg[ti, hh] = carry
                    carry = carry + jnp.sum(g[ti, hh], axis=1, keepdims=True)
                rg_s[hh] = carry
            cpre = {c: rg[c] + _split_dot(g[c], earlier) for c in chains}
            dz = {c: g[c] - jnp.exp(s[c] + lk_raw[c]) * (g[c] + cpre[c]) for c in chains}
            dz = {c: (dz[c] if mask[c[0]] is None else jnp.where(mask[c[0]], dz[c], 0.0)).astype(BF16) for c in chains}
            for ti, hh in chains:
                dq_s[hh] += jnp.dot(dz[ti, hh], kb[ti], preferred_element_type=F32)
            for ti in range(nt):
                dk = lax.dot_general(dz[ti, 0], qms[0], _TN, preferred_element_type=F32)
                dk_s[pl.ds(starts[ti], bk), :] += dk + lax.dot_general(dz[ti, 1], qms[1], _TN, preferred_element_type=F32)
                dv = lax.dot_general(a[ti, 0].astype(BF16), doms[0], _TN, preferred_element_type=F32)
                dv_s[pl.ds(starts[ti], bk), :] += dv + lax.dot_general(a[ti, 1].astype(BF16), doms[1], _TN,
                                                                       preferred_element_type=F32)

        live = jnp.maximum(jnp.max(runs[0], axis=0, keepdims=True), jnp.max(runs[1], axis=0, keepdims=True)) >= SBA_EXP_ZERO
        first = jnp.minimum(jnp.min(jnp.where(live, lane, LANES)), i * ratio)

        def step(j, carry):
            tiles([(j, False)])
            return carry

        if ratio == 1:
            lax.fori_loop(first, i - 1, step, 0)

            @pl.when(i == 0)
            def _():
                tiles([(i, True)])

            @pl.when(i > 0)
            def _():
                tiles([(i - 1, False), (i, True)])
        else:
            lax.fori_loop(first, i * ratio, step, 0)
            for d in range(ratio):
                tiles([(i * ratio + d, True)])

        dq_ref[...] = jnp.where(lane < 64, dq_s[0], dq_s[1])

        @pl.when(i == nq - 1)
        def _():
            col = pl.multiple_of(p * LANES, LANES)
            ck = pltpu.make_async_copy(dk_s, dk_hbm.at[:, pl.ds(col, LANES)], sem.at[0])
            cv = pltpu.make_async_copy(dv_s, dv_hbm.at[:, pl.ds(col, LANES)], sem.at[1])
            ck.start()
            cv.start()
            ck.wait()
            cv.wait()

    shp = jax.ShapeDtypeStruct((t, SB_WIDTH), F32)
    tile = pl.BlockSpec((bq, LANES), lambda p, i: (i, p))
    return pl.pallas_call(
        body, name=name, grid=(2, nq),
        in_specs=[tile, pl.BlockSpec((t, LANES), lambda p, i: (0, 2 + p)), pl.BlockSpec((t, LANES), lambda p, i: (0, 4 + p)),
                  pl.BlockSpec((2, bq, LANES), lambda p, i: (p, i, 0)), tile],
        out_specs=[tile, pl.BlockSpec(memory_space=pl.ANY), pl.BlockSpec(memory_space=pl.ANY)],
        out_shape=[shp, shp, shp],
        scratch_shapes=[pltpu.VMEM((t, LANES), F32), pltpu.VMEM((t, LANES), F32), pltpu.SemaphoreType.DMA((2,)),
                        pltpu.VMEM((2, bq, LANES), F32), pltpu.VMEM((2, bq, 1), F32)],
        compiler_params=_cparams(("arbitrary", "arbitrary")),
    )(qkv, qkv, qkv, runs, do)


def _ssd_consts():
    ln = SSM_CHUNK
    ri = lax.broadcasted_iota(jnp.int32, (ln, ln), 0)
    ci = lax.broadcasted_iota(jnp.int32, (ln, ln), 1)
    eh = lax.broadcasted_iota(jnp.int32, (LANES, SSM_INNER), 0)
    el = lax.broadcasted_iota(jnp.int32, (LANES, SSM_INNER), 1)
    expand = (jnp.right_shift(el, 6) == eh).astype(BF16)
    th = lax.broadcasted_iota(jnp.int32, (SSM_INNER, LANES), 1)
    tl = lax.broadcasted_iota(jnp.int32, (SSM_INNER, LANES), 0)
    reduce = (jnp.right_shift(tl, 6) == th).astype(BF16)
    return ri, ci, expand, reduce


def _dot_f32(a, b):
    return jnp.dot(a, b, precision=HI, preferred_element_type=F32)


def _split3(x):
    hi = x.astype(BF16)
    r1 = x - hi.astype(F32)
    mid = r1.astype(BF16)
    lo = (r1 - mid.astype(F32)).astype(BF16)
    return hi, mid, lo


def _dot_hi(a, b):
    if a.dtype == BF16:
        return sum(jnp.dot(a, t, preferred_element_type=F32) for t in _split3(b))
    return sum(jnp.dot(t, b, preferred_element_type=F32) for t in _split3(a))


def _ssd_prelude(xbc_ref, dt_ref, dtt_ref, hpr_ref, hpc_ref, ri, ci, expand):
    ln = SSM_CHUNK
    xs = xbc_ref[:, 0:512]
    bm = xbc_ref[:, 512:640]
    cm = xbc_ref[:, 640:768]
    dtb_r = hpr_ref[0:1, :]
    aneg_r = -jnp.exp(hpr_ref[1:2, :])
    pre = dt_ref[...] + dtb_r
    dt = _softplus(pre)
    a = dt * aneg_r
    dtt = _softplus(dtt_ref[...] + hpc_ref[0:8, :])
    att = dtt * (-jnp.exp(hpc_ref[8:16, :]))
    tril = (ri >= ci).astype(BF16)
    triu = (ri <= ci).astype(BF16)
    acs = _dot_hi(tril, a)
    acst = _dot_hi(att, triu)
    acs_e = _dot_hi(acs, expand)
    dt_e = _dot_hi(dt, expand)
    last_e = acs_e[ln - 1:ln, :]
    e_e = jnp.exp(acs_e)
    w_e = jnp.exp(last_e - acs_e)
    dec_e = jnp.exp(last_e)
    xdt = xs * dt_e
    return dict(xs=xs, bm=bm, cm=cm, pre=pre, dt=dt, aneg_r=aneg_r, acs=acs, acst=acst, dt_e=dt_e, e_e=e_e,
                w_e=w_e, dec_e=dec_e, xdt=xdt, triu=triu)


def ssd_fwd(act, p, dt32, dtt, hp_rows, hp_cols, d_e, norm_w, name):
    t = act.shape[0]
    ln = SSM_CHUNK
    nc = t // ln

    def body(xbc_ref, dt_ref, z_ref, dtt_ref, hpr_ref, hpc_ref, d_ref, nw_ref, yc_ref, y_ref, sto_ref, st):
        @pl.when(pl.program_id(0) == 0)
        def _():
            st[...] = jnp.zeros_like(st)

        ri, ci, expand, _ = _ssd_consts()
        q = _ssd_prelude(xbc_ref, dt_ref, dtt_ref, hpr_ref, hpc_ref, ri, ci, expand)
        lane = lax.broadcasted_iota(jnp.int32, (1, LANES), 1)
        rown = lax.broadcasted_iota(jnp.int32, (LANES, 1), 0)
        low = lane < 64
        mask = ri >= ci
        xdt_b = q["xdt"].astype(BF16)
        xw_b = (q["xdt"] * q["w_e"]).astype(BF16)
        bt = q["bm"].T.astype(BF16)
        cb_ = q["cm"].astype(BF16)
        y_pairs = []
        for g in range(2):
            gm = low if g == 0 else jnp.logical_not(low)
            rm = (rown < 64) if g == 0 else (rown >= 64)
            cg = jnp.where(gm, cb_, jnp.zeros_like(cb_))
            cb = jnp.dot(cg, bt, preferred_element_type=F32)
            for pp in range(2):
                pi = 2 * g + pp
                sl = slice(LANES * pi, LANES * (pi + 1))
                xp = xdt_b[:, sl]
                yd = []
                for hh in range(2):
                    h = 2 * pi + hh
                    diff = q["acs"][:, h:h + 1] - q["acst"][h:h + 1, :]
                    lam = jnp.exp(jnp.where(mask, diff, -jnp.inf))
                    yd.append(jnp.dot((cb * lam).astype(BF16), xp, preferred_element_type=F32))
                sp = st[pi]
                sto_ref[0, pi] = sp
                yoff = jnp.dot(cg, sp.astype(BF16), preferred_element_type=F32) * q["e_e"][:, sl]
                upd = jnp.dot(bt, xw_b[:, sl], preferred_element_type=F32)
                st[pi] = q["dec_e"][:, sl] * sp + jnp.where(rm, upd, 0.0)
                y_pairs.append(jnp.where(low, yd[0], yd[1]) + yoff)
        y = jnp.concatenate(y_pairs, axis=1) + q["xs"] * d_ref[...]
        y_ref[...] = y
        yg = y * _silu(z_ref[...].astype(F32))
        for g in range(2):
            sl = slice(256 * g, 256 * (g + 1))
            seg = yg[:, sl]
            yc_ref[:, sl] = (seg * _rstd(seg) * nw_ref[:, sl]).astype(BF16)

    return pl.pallas_call(
        body, name=name, grid=(nc,),
        in_specs=[pl.BlockSpec((ln, 768), lambda c: (c, 0)), pl.BlockSpec((ln, LANES), lambda c: (c, 0)),
                  pl.BlockSpec((ln, 512), lambda c: (c, OFF_Z // 512)), pl.BlockSpec((8, ln), lambda c: (0, c)),
                  pl.BlockSpec((8, LANES), lambda c: (0, 0)), pl.BlockSpec((16, ln), lambda c: (0, 0)),
                  pl.BlockSpec((1, 512), lambda c: (0, 0)), pl.BlockSpec((1, 512), lambda c: (0, 0))],
        out_specs=[pl.BlockSpec((ln, 512), lambda c: (c, 0)), pl.BlockSpec((ln, 512), lambda c: (c, 0)),
                   pl.BlockSpec((1, 4, LANES, LANES), lambda c: (c, 0, 0, 0))],
        out_shape=[jax.ShapeDtypeStruct((t, 512), BF16), jax.ShapeDtypeStruct((t, 512), F32),
                   jax.ShapeDtypeStruct((nc, 4, LANES, LANES), F32)],
        scratch_shapes=[pltpu.VMEM((4, LANES, LANES), F32)],
        compiler_params=_cparams(("arbitrary",)),
    )(act, dt32, p, dtt, hp_rows, hp_cols, d_e, norm_w)


def ssd_bwd(dyc, y, act, p, dt32, dtt, states, hp_rows, hp_cols, d_e, norm_w, name):
    t = act.shape[0]
    ln = SSM_CHUNK
    nc = t // ln

    def body(dyc_ref, y_ref, xbc_ref, dt_ref, z_ref, dtt_ref, st_ref, hpr_ref, hpc_ref, d_ref, nw_ref,
             dz_ref, dact_ref, ddt_ref, dnw_ref, dd_ref, dhp_ref, ds):
        @pl.when(pl.program_id(0) == 0)
        def _():
            ds[...] = jnp.zeros_like(ds)
            dnw_ref[...] = jnp.zeros_like(dnw_ref)
            dd_ref[...] = jnp.zeros_like(dd_ref)
            dhp_ref[...] = jnp.zeros_like(dhp_ref)

        ri, ci, expand, reduce = _ssd_consts()
        q = _ssd_prelude(xbc_ref, dt_ref, dtt_ref, hpr_ref, hpc_ref, ri, ci, expand)
        lane = lax.broadcasted_iota(jnp.int32, (1, LANES), 1)
        rown = lax.broadcasted_iota(jnp.int32, (LANES, 1), 0)
        low = lane < 64
        mask = ri >= ci
        mask_t = ci >= ri
        xs, xdt, acs, acst = q["xs"], q["xdt"], q["acs"], q["acst"]
        yv, zv, nw = y_ref[...], z_ref[...].astype(F32), nw_ref[...]
        sg = _sigmoid(zv)
        zz = zv * sg
        yg = yv * zz
        dycv = dyc_ref[...]
        u = dycv * nw
        dyg_parts, dnw_parts = [], []
        for g in range(2):
            sl = slice(256 * g, 256 * (g + 1))
            seg = yg[:, sl]
            rr = _rstd(seg)
            nrm = seg * rr
            dyg_parts.append(rr * (u[:, sl] - nrm * jnp.mean(nrm * u[:, sl], axis=-1, keepdims=True)))
            dnw_parts.append(jnp.sum(dycv[:, sl] * nrm, axis=0, keepdims=True))
        dyg = jnp.concatenate(dyg_parts, axis=1)
        dnw_ref[...] += jnp.concatenate(dnw_parts, axis=1)
        dy = dyg * zz
        dz_ref[...] = dyg * yv * (sg * (1.0 + zv * (1.0 - sg)))
        dd_ref[...] += jnp.sum(dy * xs, axis=0, keepdims=True)
        dxs = dy * d_ref[...]
        dy_b = dy.astype(BF16)
        xdt_b = xdt.astype(BF16)
        xw_b = (xdt * q["w_e"]).astype(BF16)
        bt = q["bm"].T.astype(BF16)
        ct = q["cm"].T.astype(BF16)
        cb_ = q["cm"].astype(BF16)
        bb_ = q["bm"].astype(BF16)
        dacs = jnp.zeros((ln, LANES), F32)
        dc = jnp.zeros((ln, LANES), F32)
        db = jnp.zeros((ln, LANES), F32)
        dxdt_pairs, yoffdy_pairs, dwe_pairs, ddec_pairs = [], [], [], []
        for g in range(2):
            gm = low if g == 0 else jnp.logical_not(low)
            rm = (rown < 64) if g == 0 else (rown >= 64)
            cg = jnp.where(gm, cb_, jnp.zeros_like(cb_))
            bg = jnp.where(gm, bb_, jnp.zeros_like(bb_))
            cb = jnp.dot(cg, bt, preferred_element_type=F32)
            cbt = jnp.dot(bg, ct, preferred_element_type=F32)
            dcb = jnp.zeros((ln, ln), F32)
            dcbt = jnp.zeros((ln, ln), F32)
            for pp in range(2):
                pi = 2 * g + pp
                sl = slice(LANES * pi, LANES * (pi + 1))
                xp = xdt_b[:, sl]
                dyp = dy_b[:, sl]
                xpt = xdt[:, sl].T.astype(BF16)
                dypt = dy[:, sl].T.astype(BF16)
                dxdt_p = jnp.zeros((ln, LANES), F32)
                for hh in range(2):
                    h = 2 * pi + hh
                    hm = low if hh == 0 else jnp.logical_not(low)
                    col = acs[:, h:h + 1]
                    row = acst[h:h + 1, :]
                    lam = jnp.exp(jnp.where(mask, col - row, -jnp.inf))
                    lam_t = jnp.exp(jnp.where(mask_t, row - col, -jnp.inf))
                    m = cb * lam
                    m_t = cbt * lam_t
                    dyh = jnp.where(hm, dyp, jnp.zeros_like(dyp))
                    xh = jnp.where(hm, xp, jnp.zeros_like(xp))
                    dm = jnp.dot(dyh, xpt, preferred_element_type=F32)
                    dm_t = jnp.dot(xh, dypt, preferred_element_type=F32)
                    dcb = dcb + dm * lam
                    dcbt = dcbt + dm_t * lam_t
                    rs = jnp.sum(dm * m, axis=1, keepdims=True) - jnp.sum(dm_t * m_t, axis=1, keepdims=True)
                    dacs = dacs + jnp.where(lane == h, rs, 0.0)
                    dxdt_p = dxdt_p + jnp.dot(m_t.astype(BF16), dyh, preferred_element_type=F32)
                sp = st_ref[0, pi]
                sp_b = sp.astype(BF16)
                dsn = ds[pi]
                dsn_b = dsn.astype(BF16)
                e_p, w_p, dec_p = q["e_e"][:, sl], q["w_e"][:, sl], q["dec_e"][:, sl]
                yoff = jnp.dot(cg, sp_b, preferred_element_type=F32) * e_p
                dyo = dy[:, sl] * e_p
                dyo_b = dyo.astype(BF16)
                dc = dc + lax.dot_general(dyo_b, sp_b, _NT, preferred_element_type=F32)
                ds_prev = dec_p * dsn + jnp.where(rm, jnp.dot(ct, dyo_b, preferred_element_type=F32), 0.0)
                yoffdy_pairs.append(dy[:, sl] * yoff)
                dxw = jnp.dot(bg, dsn_b, preferred_element_type=F32)
                db = db + lax.dot_general(xw_b[:, sl], dsn_b, _NT, preferred_element_type=F32)
                dxdt_p = dxdt_p + dxw * w_p
                dwe_pairs.append(dxw * xdt[:, sl])
                ddec_pairs.append(jnp.sum(dsn * sp, axis=0, keepdims=True))
                ds[pi] = ds_prev
                dxdt_pairs.append(dxdt_p)
            dc = dc + jnp.dot(dcb.astype(BF16), bg, preferred_element_type=F32)
            db = db + jnp.dot(dcbt.astype(BF16), cg, preferred_element_type=F32)
        dxdt = jnp.concatenate(dxdt_pairs, axis=1)
        yoffdy = jnp.concatenate(yoffdy_pairs, axis=1)
        dwe = jnp.concatenate(dwe_pairs, axis=1)
        ddec_e = jnp.broadcast_to(jnp.concatenate(ddec_pairs, axis=1), (8, SSM_INNER))
        last = acs[ln - 1:ln, :]
        w_col = jnp.exp(last - acs)
        dw_col = _dot_hi(dwe, reduce) * w_col
        dacs = dacs + _dot_hi(yoffdy, reduce) - dw_col
        dlast = jnp.sum(dw_col, axis=0, keepdims=True) + jnp.exp(last) * _dot_hi(ddec_e, reduce)[0:1, :]
        rowi = lax.broadcasted_iota(jnp.int32, (ln, 1), 0)
        dacs = dacs + jnp.where(rowi == ln - 1, dlast, 0.0)
        da = _dot_hi(q["triu"], dacs)
        ddt = da * q["aneg_r"] + _dot_hi(dxdt * xs, reduce)
        ddt_raw = jnp.where(lane < SSM_HEADS, ddt * _sigmoid(q["pre"]), 0.0)
        ddt_ref[...] = ddt_raw
        dhp_ref[0:1, :] += jnp.sum(ddt_raw, axis=0, keepdims=True)
        dhp_ref[1:2, :] += jnp.where(lane < SSM_HEADS, jnp.sum(da * q["dt"], axis=0, keepdims=True) * q["aneg_r"], 0.0)
        dact_ref[:, 0:512] = dxs + dxdt * q["dt_e"]
        dact_ref[:, 512:640] = db
        dact_ref[:, 640:768] = dc

    rev = lambda c: nc - 1 - c
    return pl.pallas_call(
        body, name=name, grid=(nc,),
        in_specs=[pl.BlockSpec((ln, 512), lambda c: (rev(c), 0)), pl.BlockSpec((ln, 512), lambda c: (rev(c), 0)),
                  pl.BlockSpec((ln, 768), lambda c: (rev(c), 0)),
                  pl.BlockSpec((ln, LANES), lambda c: (rev(c), 0)),
                  pl.BlockSpec((ln, 512), lambda c: (rev(c), OFF_Z // 512)), pl.BlockSpec((8, ln), lambda c: (0, rev(c))),
                  pl.BlockSpec((1, 4, LANES, LANES), lambda c: (rev(c), 0, 0, 0)),
                  pl.BlockSpec((8, LANES), lambda c: (0, 0)), pl.BlockSpec((16, ln), lambda c: (0, 0)),
                  pl.BlockSpec((1, 512), lambda c: (0, 0)), pl.BlockSpec((1, 512), lambda c: (0, 0))],
        out_specs=[pl.BlockSpec((ln, 512), lambda c: (rev(c), 0)), pl.BlockSpec((ln, 768), lambda c: (rev(c), 0)),
                   pl.BlockSpec((ln, LANES), lambda c: (rev(c), 0)), pl.BlockSpec((1, 512), lambda c: (0, 0)),
                   pl.BlockSpec((1, 512), lambda c: (0, 0)), pl.BlockSpec((8, LANES), lambda c: (0, 0))],
        out_shape=[jax.ShapeDtypeStruct((t, 512), F32), jax.ShapeDtypeStruct((t, 768), F32),
                   jax.ShapeDtypeStruct((t, LANES), F32), jax.ShapeDtypeStruct((1, 512), F32),
                   jax.ShapeDtypeStruct((1, 512), F32), jax.ShapeDtypeStruct((8, LANES), F32)],
        scratch_shapes=[pltpu.VMEM((4, LANES, LANES), F32)],
        compiler_params=_cparams(("arbitrary",)),
    )(dyc, y, act, dt32, p, dtt, states, hp_rows, hp_cols, d_e, norm_w)


def mod_shard_fwd(c_all, mod_w, mod_b_shard, name):
    def body(c_ref, w_ref, b_ref, o_ref):
        sc = _silu(c_ref[...])
        for l in range(DEPTH):
            o_ref[l] = _dot_f32(sc, w_ref[l]) + b_ref[l]

    return pl.pallas_call(body, name=name, out_shape=jax.ShapeDtypeStruct((DEPTH, N_DEV, mod_w.shape[2]), F32),
                          compiler_params=_cparams())(c_all, mod_w, mod_b_shard)


def mod_w_grad(c_all, dmod_shard, name):
    def body(c_ref, d_ref, o_ref):
        sc = _silu(c_ref[...])
        for l in range(DEPTH):
            o_ref[l] = lax.dot_general(sc, d_ref[l], _TN, precision=HI, preferred_element_type=F32)

    return pl.pallas_call(body, name=name, out_shape=jax.ShapeDtypeStruct((DEPTH, D_MODEL, dmod_shard.shape[2]), F32),
                          compiler_params=_cparams())(c_all, dmod_shard)


_BIG = ("w_in", "sc_conv_w", "ssm_conv_w", "w_sc_out", "w_sb_out", "w_ssm_out", "w_o", "w_ffn_in", "w_ffn_out")
_ROW_SHARDED = ("w_o", "w_ffn_out")
_CONV = ("sc_conv_w", "ssm_conv_w")
_SMALL = ("mod_b", "g_pre_mix", "g_post_mix", "g_pre_ffn", "g_post_ffn", "ssm_conv_b", "ssm_dt_bias", "ssm_a_log",
          "ssm_d", "ssm_norm_w")
_WEIGHTS = ("mod_w", "mod_b", "g_pre_mix", "g_post_mix", "g_pre_ffn", "g_post_ffn", "w_in", "sc_conv_w", "ssm_conv_w",
            "ssm_conv_b", "ssm_dt_bias", "ssm_a_log", "ssm_d", "ssm_norm_w", "w_sc_out", "w_sb_out", "w_ssm_out", "w_o",
            "w_ffn_in", "w_ffn_out")


def _gathered_to_full(g, row_sharded):
    _, dep, r, c = g.shape
    if row_sharded:
        return g.transpose(1, 0, 2, 3).reshape(dep, N_DEV * r, c)
    return g.transpose(1, 2, 0, 3).reshape(dep, r, N_DEV * c)


def _full_to_slots(w, row_sharded):
    dep, r, c = w.shape
    if row_sharded:
        return w.reshape(dep, N_DEV, r // N_DEV, c).transpose(1, 0, 2, 3).reshape(N_DEV, dep * (r // N_DEV), c)
    return w.reshape(dep, r, N_DEV, c // N_DEV).transpose(2, 0, 1, 3).reshape(N_DEV, dep * r, c // N_DEV)


def _pad_in_proj(shards):
    width = shards.shape[2]

    def cols(lo, hi):
        out = []
        while lo < hi:
            dev, a = divmod(lo, width)
            b = min(width, a + hi - lo)
            out.append(shards[dev, :, a:b])
            lo += b - a
        return out

    pad = jnp.zeros((shards.shape[1], OFF_Z - OFF_DT - 8), shards.dtype)
    return jnp.concatenate(cols(2824, 5896) + cols(0, 768) + cols(768, 1536) + cols(2048, 2816) + cols(2816, 2824)
                           + [pad] + cols(1536, 2048), axis=1)


def _unpad_in_proj(w):
    return jnp.concatenate([w[:, OFF_SC:OFF_SC + 768], w[:, OFF_QKV:OFF_QKV + 768], w[:, OFF_Z:OFF_Z + 512],
                            w[:, OFF_XBC:OFF_XBC + 768], w[:, OFF_DT:OFF_DT + 8], w[:, 0:3072]], axis=1)


def _row(v):
    return v.reshape(1, -1)


def _local_step(x, target, mod, small, conv, big):
    lw, saved = [], []
    for l in range(DEPTH):
        w_in_p = _pad_in_proj(big["w_in_shards"][:, l])
        w_cat = jnp.concatenate([big["w_sc_out"][l], big["w_sb_out"][l], big["w_ssm_out"][l]], axis=0)
        hp_rows = jnp.zeros((8, LANES), F32).at[0, :SSM_HEADS].set(small["ssm_dt_bias"][l]).at[1, :SSM_HEADS].set(
            small["ssm_a_log"][l])
        hp_cols = jnp.concatenate([jnp.broadcast_to(small["ssm_dt_bias"][l][:, None], (SSM_HEADS, SSM_CHUNK)),
                                   jnp.broadcast_to(small["ssm_a_log"][l][:, None], (SSM_HEADS, SSM_CHUNK))], axis=0)
        lw.append(dict(
            w_in_p=w_in_p, w_cat=w_cat, w_o=big["w_o"][l], w_ffn_in=big["w_ffn_in"][l], w_ffn_out=big["w_ffn_out"][l],
            sc_w8=jnp.pad(conv["sc_conv_w"][l], ((0, 5), (0, 0))), ssm_w8=jnp.pad(conv["ssm_conv_w"][l], ((0, 4), (0, 0))),
            ssm_b=_row(small["ssm_conv_b"][l]), hp_rows=hp_rows, hp_cols=hp_cols,
            d_e=_row(jnp.repeat(small["ssm_d"][l], SSM_HEAD_DIM)), norm_w=_row(small["ssm_norm_w"][l]),
            g_pre_mix=_row(small["g_pre_mix"][l]), g_post_mix=_row(small["g_post_mix"][l]),
            g_pre_ffn=_row(small["g_pre_ffn"][l]), g_post_ffn=_row(small["g_post_ffn"][l]),
            shift1=mod[l, 0:1], scale1=mod[l, 1:2], gate1=mod[l, 2:3], shift2=mod[l, 3:4], scale2=mod[l, 4:5],
            gate2=mod[l, 5:6]))

    xl = x
    h = normmod_fwd(xl, lw[0]["g_pre_mix"], lw[0]["scale1"], lw[0]["shift1"], "normmod_fwd_0")
    dy = loss = None
    for l in range(DEPTH):
        w = lw[l]
        p = mm_nn([h], [w["w_in_p"]], BF16, f"in_proj_{l}")
        dt32 = mm_nn([h], [w["w_in_p"][:, OFF_DT:OFF_DT + LANES]], F32, f"in_proj_dt_{l}")
        ya, qkv, act = post_inproj(p, w["sc_w8"], w["ssm_w8"], w["ssm_b"], f"post_inproj_{l}")
        o, runs = sba_fwd(qkv, f"sba_fwd_{l}")
        dtt = dt32[:, :SSM_HEADS].T
        yc, ypre, states = ssd_fwd(act, p, dt32, dtt, w["hp_rows"], w["hp_cols"], w["d_e"], w["norm_w"], f"ssd_fwd_{l}")
        merged = branch_out_fwd(ya, o, yc, p, w["w_cat"], f"branch_fwd_{l}")
        mix = mm_nn([merged], [w["w_o"]], F32, f"out_proj_{l}")
        x1, h2 = resid_normmod_fwd(xl, mix, w["gate1"], w["g_post_mix"], w["g_pre_ffn"], w["scale2"], w["shift2"],
                                   f"resid_mix_{l}")
        gt, up, a = mm_swiglu_fwd(h2, w["w_ffn_in"], f"ffn_in_{l}")
        f = mm_nn([a], [w["w_ffn_out"]], F32, f"ffn_out_{l}")
        saved.append(dict(x=xl, h=h, p=p, ya=ya, qkv=qkv, act=act, o=o, runs=runs, dt32=dt32, dtt=dtt, yc=yc, ypre=ypre, states=states,
                          merged=merged, mix=mix, x1=x1, h2=h2, gt=gt, up=up, a=a, f=f))
        if l + 1 < DEPTH:
            nw = lw[l + 1]
            xl, h = resid_normmod_fwd(x1, f, w["gate2"], w["g_post_ffn"], nw["g_pre_mix"], nw["scale1"], nw["shift1"],
                                      f"resid_ffn_{l}")
        else:
            dy, loss = resid_loss(x1, f, w["gate2"], w["g_post_ffn"], target, "resid_loss")

    dmod = [None] * DEPTH
    gs = {k: [None] * DEPTH for k in _SMALL + _BIG}
    dxo = dy
    top, stl = lw[DEPTH - 1], saved[DEPTH - 1]
    df, dgate2, gs["g_post_ffn"][DEPTH - 1] = resid_bwd(dxo, stl["f"], top["gate2"], top["g_post_ffn"],
                                                        f"resid_ffn_bwd_{DEPTH - 1}")
    for l in reversed(range(DEPTH)):
        w, s = lw[l], saved[l]
        dgt, dup = mm_swiglu_bwd(df, w["w_ffn_out"], s["gt"], s["up"], f"ffn_out_bwd_{l}")
        gs["w_ffn_out"][l] = mm_tn(s["a"], df, f"dw_ffn_out_{l}")
        dh2 = mm_nt([dgt, dup], [w["w_ffn_in"], w["w_ffn_in"]], [0, FFN_HIDDEN], F32, f"ffn_in_bwd_{l}")
        gs["w_ffn_in"][l] = jnp.concatenate([mm_tn(s["h2"], dgt, f"dw_ffn_gate_{l}"), mm_tn(s["h2"], dup, f"dw_ffn_up_{l}")],
                                            axis=1)
        dx1, dmix, dscale2, dshift2, gs["g_pre_ffn"][l], dgate1, gs["g_post_mix"][l] = normmod_resid_bwd(
            dh2, s["x1"], dxo, w["g_pre_ffn"], w["scale2"], s["mix"], w["gate1"], w["g_post_mix"], f"ffn_mix_bwd_{l}")
        dmerged = mm_nt([dmix], [w["w_o"]], [0], F32, f"out_proj_bwd_{l}")
        gs["w_o"][l] = mm_tn(s["merged"], dmix, f"dw_o_{l}")
        dp_gates, dya, dyb, dyc, dw_cat = branch_out_bwd(dmerged, s["ya"], s["o"], s["yc"], s["p"], w["w_cat"],
                                                         f"branch_bwd_{l}")
        gs["w_sc_out"][l], gs["w_sb_out"][l], gs["w_ssm_out"][l] = dw_cat[0:256], dw_cat[256:512], dw_cat[512:1024]
        dz, dact, ddt, dnw, dd_e, dhp = ssd_bwd(dyc, s["ypre"], s["act"], s["p"], s["dt32"], s["dtt"], s["states"], w["hp_rows"],
                                                w["hp_cols"], w["d_e"], w["norm_w"], f"ssd_bwd_{l}")
        gs["ssm_norm_w"][l] = dnw[0]
        gs["ssm_d"][l] = dd_e.reshape(SSM_HEADS, SSM_HEAD_DIM).sum(axis=1)
        gs["ssm_dt_bias"][l] = dhp[0, :SSM_HEADS]
        gs["ssm_a_log"][l] = dhp[1, :SSM_HEADS]
        dq, dk, dv = sba_bwd(s["qkv"], s["runs"], dyb, f"sba_bwd_{l}")
        dp, dscw, dssw, dssb = assemble_dp(dp_gates, dya, s["p"], w["sc_w8"], dq, dk, dv, dact, w["ssm_w8"], w["ssm_b"], ddt,
                                           dz, f"assemble_dp_{l}")
        gs["sc_conv_w"][l], gs["ssm_conv_w"][l], gs["ssm_conv_b"][l] = dscw[0:3], dssw[0:4], dssb[0]
        dh = mm_nt([dp], [w["w_in_p"]], [0], F32, f"in_proj_bwd_{l}")
        gs["w_in"][l] = _unpad_in_proj(mm_tn(s["h"], dp, f"dw_in_{l}"))
        if l > 0:
            below, sb = lw[l - 1], saved[l - 1]
            dxo, df, dscale1, dshift1, gs["g_pre_mix"][l], dgate2_below, gs["g_post_ffn"][l - 1] = normmod_resid_bwd(
                dh, s["x"], dx1, w["g_pre_mix"], w["scale1"], sb["f"], below["gate2"], below["g_post_ffn"],
                f"mix_ffn_bwd_{l}")
        else:
            dxo, dscale1, dshift1, gs["g_pre_mix"][l] = normmod_bwd(dh, s["x"], dx1, w["g_pre_mix"], w["scale1"],
                                                                    f"normmod_mix_bwd_{l}")
            dgate2_below = None
        dmod[l] = jnp.concatenate([dshift1, dscale1, dgate1, dshift2, dscale2, dgate2], axis=0)
        dgate2 = dgate2_below
    for k in ("g_pre_mix", "g_post_mix", "g_pre_ffn", "g_post_ffn"):
        gs[k] = [g[0] for g in gs[k]]
    grads = {k: jnp.stack(v) for k, v in gs.items() if k != "mod_b"}
    return loss[0, 0], dxo, jnp.stack(dmod), grads


def kernel(x, c, mod_w, mod_b, g_pre_mix, g_post_mix, g_pre_ffn, g_post_ffn, w_in, sc_conv_w, ssm_conv_w, ssm_conv_b, ssm_dt_bias, ssm_a_log, ssm_d, ssm_norm_w, w_sc_out, w_sb_out, w_ssm_out, w_o, w_ffn_in, w_ffn_out, loss_target, m_mod_w, m_mod_b, m_g_pre_mix, m_g_post_mix, m_g_pre_ffn, m_g_post_ffn, m_w_in, m_sc_conv_w, m_ssm_conv_w, m_ssm_conv_b, m_ssm_dt_bias, m_ssm_a_log, m_ssm_d, m_ssm_norm_w, m_w_sc_out, m_w_sb_out, m_w_ssm_out, m_w_o, m_w_ffn_in, m_w_ffn_out, v_mod_w, v_mod_b, v_g_pre_mix, v_g_post_mix, v_g_pre_ffn, v_g_post_ffn, v_w_in, v_sc_conv_w, v_ssm_conv_w, v_ssm_conv_b, v_ssm_dt_bias, v_ssm_a_log, v_ssm_d, v_ssm_norm_w, v_w_sc_out, v_w_sb_out, v_w_ssm_out, v_w_o, v_w_ffn_in, v_w_ffn_out):
    wts = dict(mod_w=mod_w, mod_b=mod_b, g_pre_mix=g_pre_mix, g_post_mix=g_post_mix, g_pre_ffn=g_pre_ffn,
               g_post_ffn=g_post_ffn, w_in=w_in, sc_conv_w=sc_conv_w, ssm_conv_w=ssm_conv_w, ssm_conv_b=ssm_conv_b,
               ssm_dt_bias=ssm_dt_bias, ssm_a_log=ssm_a_log, ssm_d=ssm_d, ssm_norm_w=ssm_norm_w, w_sc_out=w_sc_out,
               w_sb_out=w_sb_out, w_ssm_out=w_ssm_out, w_o=w_o, w_ffn_in=w_ffn_in, w_ffn_out=w_ffn_out)
    ms = dict(mod_w=m_mod_w, mod_b=m_mod_b, g_pre_mix=m_g_pre_mix, g_post_mix=m_g_post_mix, g_pre_ffn=m_g_pre_ffn,
              g_post_ffn=m_g_post_ffn, w_in=m_w_in, sc_conv_w=m_sc_conv_w, ssm_conv_w=m_ssm_conv_w,
              ssm_conv_b=m_ssm_conv_b, ssm_dt_bias=m_ssm_dt_bias, ssm_a_log=m_ssm_a_log, ssm_d=m_ssm_d,
              ssm_norm_w=m_ssm_norm_w, w_sc_out=m_w_sc_out, w_sb_out=m_w_sb_out, w_ssm_out=m_w_ssm_out, w_o=m_w_o,
              w_ffn_in=m_w_ffn_in, w_ffn_out=m_w_ffn_out)
    vs = dict(mod_w=v_mod_w, mod_b=v_mod_b, g_pre_mix=v_g_pre_mix, g_post_mix=v_g_post_mix, g_pre_ffn=v_g_pre_ffn,
              g_post_ffn=v_g_post_ffn, w_in=v_w_in, sc_conv_w=v_sc_conv_w, ssm_conv_w=v_ssm_conv_w,
              ssm_conv_b=v_ssm_conv_b, ssm_dt_bias=v_ssm_dt_bias, ssm_a_log=v_ssm_a_log, ssm_d=v_ssm_d,
              ssm_norm_w=v_ssm_norm_w, w_sc_out=v_w_sc_out, w_sb_out=v_w_sb_out, w_ssm_out=v_w_ssm_out, w_o=v_w_o,
              w_ffn_in=v_w_ffn_in, w_ffn_out=v_w_ffn_out)
    me = 4 * lax.axis_index("x") + 2 * lax.axis_index("y") + lax.axis_index("c")
    mod_cols = mod_w.shape[2]

    pack1, sizes1 = _pack_rows([c, sc_conv_w, ssm_conv_w], F32, 8)
    got1 = all_gather_multi([pack1], "gather_c_conv")[0].reshape(N_DEV, -1)
    c_all, sc_g, ssm_g = _unpack(got1, sizes1, [(D_MODEL,), sc_conv_w.shape, ssm_conv_w.shape])
    conv = dict(sc_conv_w=_gathered_to_full(sc_g, False), ssm_conv_w=_gathered_to_full(ssm_g, False))

    mod_b_shard = lax.dynamic_slice_in_dim(mod_b, me * mod_cols, mod_cols, axis=1).reshape(DEPTH, 1, mod_cols)
    mod_sh = mod_shard_fwd(c_all, mod_w, mod_b_shard, "mod_shard_fwd")
    pack2, sizes2 = _pack_rows([mod_sh], F32, 8)
    got2 = all_gather_multi([pack2], "gather_mod")[0].reshape(N_DEV, -1)
    mod_all = _unpack(got2, sizes2, [mod_sh.shape])[0]
    mod_mine = lax.dynamic_index_in_dim(mod_all, me, axis=2, keepdims=False)
    mod = mod_mine.transpose(1, 0, 2).reshape(DEPTH, 6, D_MODEL)

    mm_names = [k for k in _BIG if k not in _CONV]
    gathered = all_gather_multi([wts[k].astype(BF16) for k in mm_names], "gather_weights")
    big = {k: _gathered_to_full(g, k in _ROW_SHARDED) for k, g in zip(mm_names, gathered) if k != "w_in"}
    big["w_in_shards"] = gathered[mm_names.index("w_in")]

    small = {k: wts[k] for k in _SMALL}
    loss_part, dx, dmod, grads = _local_step(x[0], loss_target[0], mod, small, conv, big)
    loss = lax.psum(loss_part, ("x", "y", "c"))

    small_parts = [dmod.reshape(DEPTH, 6 * D_MODEL)] + [grads[k] for k in _SMALL[1:]]
    pack5, sizes5 = _pack_rows(small_parts, F32, 8)
    pack_conv, sizes_conv = _pack_rows([grads[k] for k in _CONV], F32, 8)
    got5, got_conv = all_gather_multi([pack5, pack_conv], "gather_small_grads")
    w5, _ = _pack_rows([wts[k] for k in _SMALL], F32, 8)
    m5, _ = _pack_rows([ms[k] for k in _SMALL], F32, 8)
    v5, _ = _pack_rows([vs[k] for k in _SMALL], F32, 8)
    res5 = adamw_flat(got5, w5, m5, v5, "adamw_small")
    small_out = [_unpack(r.reshape(-1), sizes5, [wts[k].shape for k in _SMALL]) for r in res5]

    conv_full = _unpack(got_conv.reshape(N_DEV, -1), sizes_conv, [grads[k].shape for k in _CONV])
    conv_mine = [lax.dynamic_slice_in_dim(g, me * wts[k].shape[2], wts[k].shape[2], axis=3)
                 for k, g in zip(_CONV, conv_full)]
    conv_slot_sizes = [math.prod(wts[k].shape) for k in _CONV]
    conv_slots = jnp.concatenate([g.reshape(N_DEV, -1) for g in conv_mine], axis=1)
    pad_c = -conv_slots.shape[1] % (8 * LANES)
    conv_slots = jnp.pad(conv_slots, ((0, 0), (0, pad_c))).reshape(N_DEV, -1, LANES)
    wc, _ = _pack_rows([wts[k] for k in _CONV], F32, 8)
    mc, _ = _pack_rows([ms[k] for k in _CONV], F32, 8)
    vc, _ = _pack_rows([vs[k] for k in _CONV], F32, 8)
    res_c = adamw_flat(conv_slots, wc, mc, vc, "adamw_conv")
    conv_out = [_unpack(r.reshape(-1), conv_slot_sizes, [wts[k].shape for k in _CONV]) for r in res_c]

    dmod_all = got5.reshape(N_DEV, -1)[:, :DEPTH * 6 * D_MODEL].reshape(N_DEV, DEPTH, 6 * D_MODEL)
    dmod_shard = lax.dynamic_slice_in_dim(dmod_all, me * mod_cols, mod_cols, axis=2).transpose(1, 0, 2)
    g_mod_w = mod_w_grad(c_all, dmod_shard, "mod_w_grad")
    rows2 = lambda a: a.reshape(a.shape[0] * a.shape[1], a.shape[2])
    res_mw = adamw_flat(rows2(g_mod_w)[None], rows2(mod_w), rows2(m_mod_w), rows2(v_mod_w), "adamw_mod_w")
    mod_w_out = [r.reshape(mod_w.shape) for r in res_mw]

    cidx = lax.axis_index("c")
    keeps, gives = [], []
    for k in mm_names:
        slots = _full_to_slots(grads[k].astype(BF16), k in _ROW_SHARDED)
        by_chip = slots.reshape(4, 2, *slots.shape[1:])
        keeps.append(lax.dynamic_index_in_dim(by_chip, cidx, 1, keepdims=False))
        gives.append(lax.dynamic_index_in_dim(by_chip, 1 - cidx, 1, keepdims=False))
    gots = swap_with_sibling(gives, "swap_grads")
    pairs = []
    for k, keep, got in zip(mm_names, keeps, gots):
        rows4 = (4 * keep.shape[1], keep.shape[2])
        pairs.append(add_pairs(keep.reshape(rows4), got.reshape(rows4), f"add_pairs_{k}").reshape(keep.shape))
    recvs = exchange_chips(pairs, "exchange_grads")
    big_out = {}
    for k, recv in zip(mm_names, recvs):
        res = adamw_flat(recv, rows2(wts[k]), rows2(ms[k]), rows2(vs[k]), f"adamw_{k}")
        big_out[k] = [r.reshape(wts[k].shape) for r in res]

    outs = []
    for kind in range(4):
        by_name = {"mod_w": mod_w_out[kind]}
        by_name.update(zip(_SMALL, small_out[kind]))
        by_name.update(zip(_CONV, conv_out[kind]))
        by_name.update({k: v[kind] for k, v in big_out.items()})
        outs.extend(by_name[k] for k in _WEIGHTS)
    return (loss, dx[None], *outs)
```

```python
import math

import jax
import jax.numpy as jnp
from jax import lax
from jax.experimental import pallas as pl
from jax.experimental.pallas import tpu as pltpu

F32 = jnp.float32
BF16 = jnp.bfloat16
HI = lax.Precision.HIGHEST

N_DEV = 8
D_MODEL = 1024
DEPTH = 2
SC_WIDTH = 256
SB_WIDTH = 256
SB_HEAD_DIM = 64
SSM_INNER = 512
SSM_HEADS = 8
SSM_HEAD_DIM = 64
SSM_GROUPS = 2
SSM_STATE = 64
SSM_CHUNK = 256
SSM_CONV_DIM = 768
FFN_HIDDEN = 2816
NORM_EPS = 1e-6
IN_PROJ = 5896
LANES = 128
VMEM_LIMIT = 56 * 1024 * 1024

OFF_GATES = 0
OFF_SC = 3072
OFF_QKV = 3840
OFF_XBC = 4608
OFF_DT = 5376
OFF_Z = 5632
IN_PAD = 6144

ADAM_LR = 0.001
ADAM_B1 = 0.9
ADAM_B2 = 0.999
ADAM_EPS = 1e-08
ADAM_WD = 0.01
ADAM_STEP = 10

MESH_ID = pl.DeviceIdType.MESH


def _cparams(sem=None):
    return pltpu.CompilerParams(dimension_semantics=sem, vmem_limit_bytes=VMEM_LIMIT)


def _my_pos():
    return lax.axis_index("x"), lax.axis_index("y"), lax.axis_index("c")


def all_gather_multi(blocks, name):
    n = len(blocks)

    def body(*refs):
        x_refs, o_refs = refs[:n], refs[n:2 * n]
        send_sems, recv_sems, local_sems = refs[2 * n:]
        x, y, c = _my_pos()
        me, sibling = (x, y, c), (x, y, 1 - c)
        chips = [(1 - x, y), (x, 1 - y), (1 - x, 1 - y)]

        def slot(a, px, py, pc):
            return o_refs[a].at[4 * px + 2 * py + pc]

        def copy(a, k, blk, to, src=None):
            return pltpu.make_async_remote_copy(
                src_ref=slot(a, *blk) if src is None else src, dst_ref=slot(a, *blk),
                send_sem=send_sems.at[7 * a + k], recv_sem=recv_sems.at[7 * a + k], device_id=to, device_id_type=MESH_ID)

        mine = [pltpu.make_async_copy(x_refs[a], slot(a, *me), local_sems.at[a]) for a in range(n)]
        for cp in mine:
            cp.start()
        first = [copy(a, 1 + j, me, (*chip, c), src=x_refs[a]) for j, chip in enumerate(chips) for a in range(n)]
        first += [copy(a, 0, me, sibling, src=x_refs[a]) for a in range(n)]
        for cp in first:
            cp.start()
        passed = []
        for j, chip in enumerate(chips):
            for a in range(n):
                copy(a, 1 + j, (*chip, c), me).wait_recv()
                fwd = copy(a, 4 + j, (*chip, c), sibling)
                fwd.start()
                passed.append(fwd)
        for a in range(n):
            copy(a, 0, sibling, me).wait_recv()
            for j, chip in enumerate(chips):
                copy(a, 4 + j, (*chip, 1 - c), me).wait_recv()
        for cp in first + passed:
            cp.wait_send()
        for cp in mine:
            cp.wait()

    any_spec = pl.BlockSpec(memory_space=pl.ANY)
    return pl.pallas_call(
        body, name=name,
        out_shape=[jax.ShapeDtypeStruct((N_DEV,) + b.shape, b.dtype) for b in blocks],
        in_specs=[any_spec] * n, out_specs=[any_spec] * n,
        scratch_shapes=[pltpu.SemaphoreType.DMA((7 * n,)), pltpu.SemaphoreType.DMA((7 * n,)), pltpu.SemaphoreType.DMA((n,))],
    )(*blocks)


def swap_with_sibling(gives, name):
    n = len(gives)

    def body(*refs):
        g_refs, r_refs = refs[:n], refs[n:2 * n]
        send_sems, recv_sems = refs[2 * n:]
        x, y, c = _my_pos()
        copies = [pltpu.make_async_remote_copy(
            src_ref=g_refs[a], dst_ref=r_refs[a], send_sem=send_sems.at[a], recv_sem=recv_sems.at[a],
            device_id=(x, y, 1 - c), device_id_type=MESH_ID) for a in range(n)]
        for cp in copies:
            cp.start()
        for cp in copies:
            cp.wait_recv()
        for cp in copies:
            cp.wait_send()

    any_spec = pl.BlockSpec(memory_space=pl.ANY)
    return pl.pallas_call(
        body, name=name, out_shape=[jax.ShapeDtypeStruct(g.shape, g.dtype) for g in gives],
        in_specs=[any_spec] * n, out_specs=[any_spec] * n,
        scratch_shapes=[pltpu.SemaphoreType.DMA((n,)), pltpu.SemaphoreType.DMA((n,))],
    )(*gives)


def exchange_chips(sends, name):
    n = len(sends)

    def body(*refs):
        s_refs, r_refs = refs[:n], refs[n:2 * n]
        send_sems, recv_sems, local_sems = refs[2 * n:]
        x, y, c = _my_pos()
        me = 2 * x + y
        mine = [pltpu.make_async_copy(s_refs[a].at[me], r_refs[a].at[me], local_sems.at[a]) for a in range(n)]
        for cp in mine:
            cp.start()
        copies = []
        for k in (2, 1, 3):
            px, py = x ^ (k >> 1), y ^ (k & 1)
            for a in range(n):
                cp = pltpu.make_async_remote_copy(
                    src_ref=s_refs[a].at[2 * px + py], dst_ref=r_refs[a].at[me],
                    send_sem=send_sems.at[3 * a + k - 1], recv_sem=recv_sems.at[3 * a + k - 1],
                    device_id=(px, py, c), device_id_type=MESH_ID)
                cp.start()
                copies.append(cp)
        for cp in copies:
            cp.wait_recv()
        for cp in copies:
            cp.wait_send()
        for cp in mine:
            cp.wait()

    any_spec = pl.BlockSpec(memory_space=pl.ANY)
    return pl.pallas_call(
        body, name=name, out_shape=[jax.ShapeDtypeStruct(s.shape, s.dtype) for s in sends],
        in_specs=[any_spec] * n, out_specs=[any_spec] * n,
        scratch_shapes=[pltpu.SemaphoreType.DMA((3 * n,)), pltpu.SemaphoreType.DMA((3 * n,)), pltpu.SemaphoreType.DMA((n,))],
    )(*sends)


def add_pairs(a, b, name, tr=512):
    rows, cols = a.shape
    tr = max(d for d in range(16, min(tr, rows) + 1, 16) if rows % d == 0)

    def body(a_ref, b_ref, o_ref):
        o_ref[...] = (a_ref[...].astype(F32) + b_ref[...].astype(F32)).astype(BF16)

    tile = pl.BlockSpec((tr, cols), lambda i: (i, 0))
    return pl.pallas_call(body, name=name, grid=(rows // tr,), in_specs=[tile, tile], out_specs=tile,
                          out_shape=jax.ShapeDtypeStruct((rows, cols), BF16), compiler_params=_cparams(("parallel",)))(a, b)


def _pack_rows(parts, dtype, row_multiple):
    flat = [p.astype(dtype).reshape(-1) for p in parts]
    sizes = [f.shape[0] for f in flat]
    total = sum(sizes)
    quantum = LANES * row_multiple
    padded = -(-total // quantum) * quantum
    if padded > total:
        flat.append(jnp.zeros((padded - total,), dtype))
    return jnp.concatenate(flat).reshape(padded // LANES, LANES), sizes


def _unpack(flat, sizes, shapes):
    out, off = [], 0
    lead = flat.shape[:-1]
    for n, shp in zip(sizes, shapes):
        out.append(flat[..., off:off + n].reshape(lead + tuple(shp)))
        off += n
    return out


def rows_call(name, body, n_rows, tr, ins, outs, scratch=(), aliases=None):
    n_tiles = n_rows // tr
    assert n_tiles * tr == n_rows
    in_specs, arrays = [], []
    for arr, kind in ins:
        arrays.append(arr)
        if kind == "row":
            in_specs.append(pl.BlockSpec((tr, arr.shape[1]), lambda i: (i, 0)))
        elif kind == "any":
            in_specs.append(pl.BlockSpec(memory_space=pl.ANY))
        elif kind == "full":
            in_specs.append(pl.BlockSpec(arr.shape, lambda i, nd=arr.ndim: (0,) * nd))
        elif kind[0] == "row":
            _, w, ci = kind
            in_specs.append(pl.BlockSpec((tr, w), lambda i, ci=ci: (i, ci)))
        elif kind[0] == "prev8":
            _, w, ci = kind
            hr = 8 * (4 // arr.dtype.itemsize)
            in_specs.append(pl.BlockSpec((hr, w), lambda i, ci=ci, hr=hr: (jnp.maximum(i * (tr // hr) - 1, 0), ci)))
        elif kind[0] == "next8":
            _, w, ci = kind
            hr = 8 * (4 // arr.dtype.itemsize)
            last = n_rows // hr - 1
            in_specs.append(pl.BlockSpec((hr, w), lambda i, ci=ci, last=last, hr=hr: (jnp.minimum((i + 1) * (tr // hr), last), ci)))
        else:
            raise ValueError(kind)
    out_specs, out_shapes = [], []
    for shape, dtype, kind in outs:
        out_shapes.append(jax.ShapeDtypeStruct(shape, dtype))
        if kind == "row":
            out_specs.append(pl.BlockSpec((tr, shape[1]), lambda i: (i, 0)))
        elif kind[0] == "row":
            _, w, ci = kind
            out_specs.append(pl.BlockSpec((tr, w), lambda i, ci=ci: (i, ci)))
        else:
            out_specs.append(pl.BlockSpec(shape, lambda i, nd=len(shape): (0,) * nd))
    has_acc = any(k == "acc" for _, _, k in outs)
    return pl.pallas_call(
        body, name=name, grid=(n_tiles,), in_specs=in_specs, out_specs=out_specs, out_shape=out_shapes,
        scratch_shapes=list(scratch), input_output_aliases=dict(aliases or {}),
        compiler_params=_cparams(("arbitrary",) if has_acc else ("parallel",)),
    )(*arrays)


def _prev8(ref):
    n = ref.shape[0]
    return ref[n - 8:n, :].astype(F32)


def _next8(ref):
    return ref[0:8, :].astype(F32)


def _acc(ref, val):
    @pl.when(pl.program_id(0) == 0)
    def _():
        ref[...] = jnp.zeros_like(ref)
    ref[...] += val


def _rstd(x):
    return lax.rsqrt(jnp.mean(x * x, axis=-1, keepdims=True) + NORM_EPS)


def _sigmoid(x):
    return 1.0 / (1.0 + jnp.exp(-x))


def _silu(x):
    return x * _sigmoid(x)


def _dsilu(x):
    s = _sigmoid(x)
    return s * (1.0 + x * (1.0 - s))


def _softplus(x):
    return jnp.maximum(x, 0.0) + jnp.log(1.0 + jnp.exp(-jnp.abs(x)))


def _log_sigmoid_neg(x):
    t = -x
    return jnp.minimum(t, 0.0) - jnp.log(1.0 + jnp.exp(jnp.minimum(x, t)))


def normmod_fwd(x, g, scale, shift, name):
    t, d = x.shape

    def body(x_ref, g_ref, sc_ref, sh_ref, h_ref):
        xv = x_ref[...]
        h = xv * _rstd(xv) * g_ref[...] * (1.0 + sc_ref[...]) + sh_ref[...]
        h_ref[...] = h.astype(BF16)

    return rows_call(name, body, t, 512, [(x, "row"), (g, "full"), (scale, "full"), (shift, "full")],
                     [((t, d), BF16, "row")])[0]


def resid_normmod_fwd(x, f, gate, g_post, g_pre, scale, shift, name):
    t, d = x.shape

    def body(x_ref, f_ref, gate_ref, gp_ref, g_ref, sc_ref, sh_ref, xo_ref, h_ref):
        fv = f_ref[...]
        xn = x_ref[...] + gate_ref[...] * (fv * _rstd(fv) * gp_ref[...])
        xo_ref[...] = xn
        h = xn * _rstd(xn) * g_ref[...] * (1.0 + sc_ref[...]) + sh_ref[...]
        h_ref[...] = h.astype(BF16)

    return rows_call(name, body, t, 512,
                     [(x, "row"), (f, "row"), (gate, "full"), (g_post, "full"), (g_pre, "full"), (scale, "full"),
                      (shift, "full")],
                     [((t, d), F32, "row"), ((t, d), BF16, "row")])


def resid_loss(x, f, gate, g_post, target, name):
    t, d = x.shape

    def body(x_ref, f_ref, gate_ref, gp_ref, tg_ref, dy_ref, loss_ref):
        fv = f_ref[...]
        yv = x_ref[...] + gate_ref[...] * (fv * _rstd(fv) * gp_ref[...])
        err = yv - tg_ref[...]
        dy_ref[...] = err * (1.0 / d)
        part = 0.5 * jnp.sum(jnp.mean(err * err, axis=-1, keepdims=True), axis=0, keepdims=True)
        _acc(loss_ref, jnp.broadcast_to(part, loss_ref.shape))

    return rows_call(name, body, t, 512,
                     [(x, "row"), (f, "row"), (gate, "full"), (g_post, "full"), (target, "row")],
                     [((t, d), F32, "row"), ((8, LANES), F32, "acc")])


def resid_bwd(dx, f, gate, g_post, name):
    t, d = dx.shape

    def body(dx_ref, f_ref, gate_ref, gp_ref, df_ref, dgate_ref, dg_ref):
        fv, dxv, gp = f_ref[...], dx_ref[...], gp_ref[...]
        r = _rstd(fv)
        fn = fv * r
        _acc(dgate_ref, jnp.sum(dxv * (fn * gp), axis=0, keepdims=True))
        dn = dxv * gate_ref[...]
        _acc(dg_ref, jnp.sum(dn * fn, axis=0, keepdims=True))
        u = dn * gp
        df = r * (u - fn * jnp.mean(fn * u, axis=-1, keepdims=True))
        df_ref[...] = df.astype(BF16)

    return rows_call(name, body, t, 512, [(dx, "row"), (f, "row"), (gate, "full"), (g_post, "full")],
                     [((t, d), BF16, "row"), ((1, d), F32, "acc"), ((1, d), F32, "acc")])


def normmod_bwd(dh, x, dx_in, g, scale, name):
    t, d = x.shape

    def body(dh_ref, x_ref, dxi_ref, g_ref, sc_ref, dx_ref, dsc_ref, dsh_ref, dg_ref):
        xv, dhv, gv = x_ref[...], dh_ref[...], g_ref[...]
        r = _rstd(xv)
        xn = xv * r
        _acc(dsc_ref, jnp.sum(dhv * (xn * gv), axis=0, keepdims=True))
        _acc(dsh_ref, jnp.sum(dhv, axis=0, keepdims=True))
        dn = dhv * (1.0 + sc_ref[...])
        _acc(dg_ref, jnp.sum(dn * xn, axis=0, keepdims=True))
        u = dn * gv
        dx_ref[...] = dxi_ref[...] + r * (u - xn * jnp.mean(xn * u, axis=-1, keepdims=True))

    return rows_call(name, body, t, 512, [(dh, "row"), (x, "row"), (dx_in, "row"), (g, "full"), (scale, "full")],
                     [((t, d), F32, "row"), ((1, d), F32, "acc"), ((1, d), F32, "acc"), ((1, d), F32, "acc")])


def normmod_resid_bwd(dh, x, dx_in, g, scale, f, gate, g_post, name):
    t, d = x.shape

    def body(dh_ref, x_ref, dxi_ref, g_ref, sc_ref, f_ref, gate_ref, gp_ref,
             dx_ref, df_ref, dsc_ref, dsh_ref, dg_ref, dgate_ref, dgp_ref):
        xv, dhv, gv = x_ref[...], dh_ref[...], g_ref[...]
        r = _rstd(xv)
        xn = xv * r
        _acc(dsc_ref, jnp.sum(dhv * (xn * gv), axis=0, keepdims=True))
        _acc(dsh_ref, jnp.sum(dhv, axis=0, keepdims=True))
        dn = dhv * (1.0 + sc_ref[...])
        _acc(dg_ref, jnp.sum(dn * xn, axis=0, keepdims=True))
        u = dn * gv
        dxv = dxi_ref[...] + r * (u - xn * jnp.mean(xn * u, axis=-1, keepdims=True))
        dx_ref[...] = dxv
        fv, gp = f_ref[...], gp_ref[...]
        rf = _rstd(fv)
        fn = fv * rf
        _acc(dgate_ref, jnp.sum(dxv * (fn * gp), axis=0, keepdims=True))
        dnf = dxv * gate_ref[...]
        _acc(dgp_ref, jnp.sum(dnf * fn, axis=0, keepdims=True))
        uf = dnf * gp
        df_ref[...] = (rf * (uf - fn * jnp.mean(fn * uf, axis=-1, keepdims=True))).astype(BF16)

    vec = ((1, d), F32, "acc")
    return rows_call(name, body, t, 512,
                     [(dh, "row"), (x, "row"), (dx_in, "row"), (g, "full"), (scale, "full"), (f, "row"), (gate, "full"),
                      (g_post, "full")],
                     [((t, d), F32, "row"), ((t, d), BF16, "row"), vec, vec, vec, vec, vec])


def _pick(n, prefs):
    for p in prefs:
        if n % p == 0:
            return p
    return n


def mm_nn(a_list, b_list, out_dtype, name, tm=1024, tn=None, tk=None):
    m, k = a_list[0].shape
    n = b_list[0].shape[1]
    tm = min(tm, m)
    tn = tn or _pick(n, (1024, 768, 512, 256, 128))
    tk = tk or _pick(k, (1024, 1408, 512, 256))
    nk = k // tk
    npair = len(a_list)

    if nk == 1 and npair == 1:
        def body1(a_ref, b_ref, o_ref):
            o_ref[...] = jnp.dot(a_ref[...], b_ref[...], preferred_element_type=F32).astype(o_ref.dtype)

        return pl.pallas_call(
            body1, name=name, grid=(m // tm, n // tn),
            in_specs=[pl.BlockSpec((tm, k), lambda i, j: (i, 0)), pl.BlockSpec((k, tn), lambda i, j: (0, j))],
            out_specs=pl.BlockSpec((tm, tn), lambda i, j: (i, j)),
            out_shape=jax.ShapeDtypeStruct((m, n), out_dtype),
            compiler_params=_cparams(("parallel", "parallel")),
        )(a_list[0], b_list[0])

    def body(*refs):
        a_refs, b_refs = refs[:npair], refs[npair:2 * npair]
        o_ref, acc = refs[2 * npair], refs[2 * npair + 1]
        kk = pl.program_id(2)

        @pl.when(kk == 0)
        def _():
            acc[...] = jnp.zeros_like(acc)

        s = acc[...]
        for a_ref, b_ref in zip(a_refs, b_refs):
            s = s + jnp.dot(a_ref[...], b_ref[...], preferred_element_type=F32)
        acc[...] = s

        @pl.when(kk == nk - 1)
        def _():
            o_ref[...] = acc[...].astype(o_ref.dtype)

    return pl.pallas_call(
        body, name=name, grid=(m // tm, n // tn, nk),
        in_specs=[pl.BlockSpec((tm, tk), lambda i, j, kk: (i, kk))] * npair
        + [pl.BlockSpec((tk, tn), lambda i, j, kk: (kk, j))] * npair,
        out_specs=pl.BlockSpec((tm, tn), lambda i, j, kk: (i, j)),
        out_shape=jax.ShapeDtypeStruct((m, n), out_dtype),
        scratch_shapes=[pltpu.VMEM((tm, tn), F32)],
        compiler_params=_cparams(("parallel", "parallel", "arbitrary")),
    )(*a_list, *b_list)


def mm_nt(a_list, b_list, b_koff, out_dtype, name, tm=1024):
    m, k = a_list[0].shape
    n = b_list[0].shape[0]
    tm = min(tm, m)
    tn = _pick(n, (1024, 512, 256))
    tk = _pick(k, (1024, 1408, 512, 256))
    nk = k // tk
    npair = len(a_list)
    koff = [o // tk for o in b_koff]
    nt_dims = (((1,), (1,)), ((), ()))

    def body(*refs):
        a_refs, b_refs = refs[:npair], refs[npair:2 * npair]
        o_ref, acc = refs[2 * npair], refs[2 * npair + 1]
        kk = pl.program_id(2)

        @pl.when(kk == 0)
        def _():
            acc[...] = jnp.zeros_like(acc)

        s = acc[...]
        for a_ref, b_ref in zip(a_refs, b_refs):
            s = s + lax.dot_general(a_ref[...], b_ref[...], nt_dims, preferred_element_type=F32)
        acc[...] = s

        @pl.when(kk == nk - 1)
        def _():
            o_ref[...] = acc[...].astype(o_ref.dtype)

    return pl.pallas_call(
        body, name=name, grid=(m // tm, n // tn, nk),
        in_specs=[pl.BlockSpec((tm, tk), lambda i, j, kk: (i, kk))] * npair
        + [pl.BlockSpec((tn, tk), lambda i, j, kk, o=o: (j, kk + o)) for o in koff],
        out_specs=pl.BlockSpec((tm, tn), lambda i, j, kk: (i, j)),
        out_shape=jax.ShapeDtypeStruct((m, n), out_dtype),
        scratch_shapes=[pltpu.VMEM((tm, tn), F32)],
        compiler_params=_cparams(("parallel", "parallel", "arbitrary")),
    )(*a_list, *b_list)


def mm_tn(a, b, name, tt=512):
    t, ka = a.shape
    n = b.shape[1]
    ta = _pick(ka, (1024, 1408, 512, 256))
    tn = _pick(n, (2048, 1024, 1408, 512, 256))
    nt = t // tt

    def body(a_ref, b_ref, o_ref, acc):
        s = pl.program_id(2)

        @pl.when(s == 0)
        def _():
            acc[...] = jnp.zeros_like(acc)

        acc[...] += lax.dot_general(a_ref[...], b_ref[...], (((0,), (0,)), ((), ())), preferred_element_type=F32)

        @pl.when(s == nt - 1)
        def _():
            o_ref[...] = acc[...].astype(BF16)

    return pl.pallas_call(
        body, name=name, grid=(ka // ta, n // tn, nt),
        in_specs=[pl.BlockSpec((tt, ta), lambda i, j, s: (s, i)), pl.BlockSpec((tt, tn), lambda i, j, s: (s, j))],
        out_specs=pl.BlockSpec((ta, tn), lambda i, j, s: (i, j)),
        out_shape=jax.ShapeDtypeStruct((ka, n), BF16),
        scratch_shapes=[pltpu.VMEM((ta, tn), F32)],
        compiler_params=_cparams(("parallel", "parallel", "arbitrary")),
    )(a, b)


def mm_swiglu_fwd(h, w_ffn_in, name, tm=512, tn=1408):
    m, k = h.shape
    nh = FFN_HIDDEN // tn

    def body(h_ref, wg_ref, wu_ref, gt_ref, up_ref, a_ref):
        hv = h_ref[...]
        gt = jnp.dot(hv, wg_ref[...], preferred_element_type=F32)
        up = jnp.dot(hv, wu_ref[...], preferred_element_type=F32)
        gt_ref[...] = gt.astype(BF16)
        up_ref[...] = up.astype(BF16)
        a_ref[...] = (_silu(gt) * up).astype(BF16)

    shp = jax.ShapeDtypeStruct((m, FFN_HIDDEN), BF16)
    ospec = pl.BlockSpec((tm, tn), lambda i, j: (i, j))
    return pl.pallas_call(
        body, name=name, grid=(m // tm, nh),
        in_specs=[pl.BlockSpec((tm, k), lambda i, j: (i, 0)), pl.BlockSpec((k, tn), lambda i, j: (0, j)),
                  pl.BlockSpec((k, tn), lambda i, j: (0, j + nh))],
        out_specs=[ospec, ospec, ospec], out_shape=[shp, shp, shp],
        compiler_params=_cparams(("parallel", "parallel")),
    )(h, w_ffn_in, w_ffn_in)


def mm_swiglu_bwd(df, w_out, gt, up, name, tm=256, sub=256):
    m, k = df.shape
    n_sub = FFN_HIDDEN // sub

    def body(df_ref, w_ref, gt_ref, up_ref, dgt_ref, dup_ref):
        dfv = df_ref[...]

        def chunk_dot(c):
            return lax.dot_general(dfv, w_ref[c * sub:(c + 1) * sub, :], (((1,), (1,)), ((), ())), preferred_element_type=F32)

        da_next = chunk_dot(0)
        for c in range(n_sub):
            da = da_next
            if c + 1 < n_sub:
                da_next = chunk_dot(c + 1)
            cols = slice(c * sub, (c + 1) * sub)
            gtv = gt_ref[:, cols].astype(F32)
            sg = _sigmoid(gtv)
            dgt_ref[:, cols] = (da * up_ref[:, cols].astype(F32) * (sg * (1.0 + gtv * (1.0 - sg)))).astype(BF16)
            dup_ref[:, cols] = (da * (gtv * sg)).astype(BF16)

    shp = jax.ShapeDtypeStruct((m, FFN_HIDDEN), BF16)
    tile = pl.BlockSpec((tm, FFN_HIDDEN), lambda i: (i, 0))
    return pl.pallas_call(
        body, name=name, grid=(m // tm,),
        in_specs=[pl.BlockSpec((tm, k), lambda i: (i, 0)), pl.BlockSpec((FFN_HIDDEN, k), lambda i: (0, 0)), tile, tile],
        out_specs=[tile, tile], out_shape=[shp, shp],
        compiler_params=_cparams(("parallel",)),
    )(df, w_out, gt, up)


def _shift_down(x, prev8, j):
    if j == 0:
        return x
    xr = pltpu.roll(x, j, 0)
    pr = pltpu.roll(prev8, j, 0)
    row = lax.broadcasted_iota(jnp.int32, (8, x.shape[1]), 0)
    head = jnp.where(row < j, pr, xr[:8])
    return head if x.shape[0] == 8 else jnp.concatenate([head, xr[8:]], axis=0)


def _shift_up(x, next8, j):
    if j == 0:
        return x
    n = x.shape[0]
    xr = pltpu.roll(x, n - j, 0)
    nr = pltpu.roll(next8, 8 - j, 0)
    row = lax.broadcasted_iota(jnp.int32, (8, x.shape[1]), 0)
    return jnp.concatenate([xr[:n - 8], jnp.where(row >= 8 - j, nr, xr[n - 8:])], axis=0)


def _conv_taps(x, prev8, w_ref, taps):
    out = None
    for k in range(taps):
        term = w_ref[k:k + 1, :] * _shift_down(x, prev8, taps - 1 - k)
        out = term if out is None else out + term
    return out


def post_inproj(p, sc_w, ssm_w, ssm_b, name, tr=512):
    t = p.shape[0]

    def body(sc_ref, scp_ref, qkv_ref, xbc_ref, xbcp_ref, scw_ref, sw_ref, sb_ref, ya_ref, qkvo_ref, act_ref):
        first = (pl.program_id(0) > 0).astype(F32)
        sc = sc_ref[...].astype(F32)
        scp = _prev8(scp_ref) * first
        u = sc[:, 256:512] * sc[:, 512:768]
        up = scp[:, 256:512] * scp[:, 512:768]
        ya_ref[...] = (sc[:, 0:256] * _conv_taps(u, up, scw_ref, 3)).astype(BF16)
        qkv = qkv_ref[...]
        qkvo_ref[:, 0:256] = (qkv[:, 0:256].astype(F32) * 0.125).astype(BF16)
        qkvo_ref[:, 256:768] = qkv[:, 256:768].astype(BF16)
        xc = _conv_taps(xbc_ref[...].astype(F32), _prev8(xbcp_ref) * first, sw_ref, 4) + sb_ref[...]
        act_ref[...] = _silu(xc)

    return rows_call(
        name, body, t, tr,
        [(p, ("row", 768, OFF_SC // 768)), (p, ("prev8", 768, OFF_SC // 768)), (p, ("row", 768, OFF_QKV // 768)),
         (p, ("row", 768, OFF_XBC // 768)), (p, ("prev8", 768, OFF_XBC // 768)),
         (sc_w, "full"), (ssm_w, "full"), (ssm_b, "full")],
        [((t, 256), BF16, "row"), ((t, 768), BF16, "row"), ((t, 768), F32, "row")])


def branch_out_fwd(ya, yb, yc, p, w_cat, name, tr=256):
    t = p.shape[0]

    def body(ya_ref, yb_ref, yc_ref, gl_ref, w_ref, o_ref):
        y_a = jnp.dot(ya_ref[...], w_ref[0:256, :], preferred_element_type=F32)
        y_b = jnp.dot(yb_ref[...].astype(BF16), w_ref[256:512, :], preferred_element_type=F32)
        y_c = jnp.dot(yc_ref[...], w_ref[512:1024, :], preferred_element_type=F32)
        m = (_sigmoid(gl_ref[:, 0:1024].astype(F32)) * y_a + _sigmoid(gl_ref[:, 1024:2048].astype(F32)) * y_b
             + _sigmoid(gl_ref[:, 2048:3072].astype(F32)) * y_c)
        o_ref[...] = m.astype(BF16)

    return rows_call(name, body, t, tr,
                     [(ya, "row"), (yb, "row"), (yc, "row"), (p, ("row", 3072, 0)), (w_cat, "full")],
                     [((t, D_MODEL), BF16, "row")])[0]


def branch_out_bwd(dm, ya, yb, yc, p, w_cat, name, tr=256):
    t = p.shape[0]
    tn_dims = (((0,), (0,)), ((), ()))

    def body(dm_ref, ya_ref, yb_ref, yc_ref, gl_ref, w_ref, dgl_ref, dya_ref, dyb_ref, dyc_ref, dw_ref):
        @pl.when(pl.program_id(0) == 0)
        def _():
            dw_ref[...] = jnp.zeros_like(dw_ref)

        dmv = dm_ref[...]
        ins = (ya_ref[...], yb_ref[...].astype(BF16), yc_ref[...])
        rows = ((0, 256), (256, 512), (512, 1024))
        outs = (dya_ref, dyb_ref, dyc_ref)
        for i in range(3):
            r0, r1 = rows[i]
            y = jnp.dot(ins[i], w_ref[r0:r1, :], preferred_element_type=F32)
            s = _sigmoid(gl_ref[:, 1024 * i:1024 * (i + 1)].astype(F32))
            dgl_ref[:, 1024 * i:1024 * (i + 1)] = (dmv * y * s * (1.0 - s)).astype(BF16)
            dy = (dmv * s).astype(BF16)
            outs[i][...] = lax.dot_general(dy, w_ref[r0:r1, :], _NT, preferred_element_type=F32)
            dw_ref[r0:r1, :] += lax.dot_general(ins[i], dy, tn_dims, preferred_element_type=F32)

    return rows_call(name, body, t, tr,
                     [(dm, "row"), (ya, "row"), (yb, "row"), (yc, "row"), (p, ("row", 3072, 0)), (w_cat, "full")],
                     [((t, IN_PAD), BF16, ("row", 3072, 0)), ((t, 256), F32, "row"), ((t, 256), F32, "row"),
                      ((t, 512), F32, "row"), ((D_MODEL, D_MODEL), F32, "acc")])


def assemble_dp(dp, dya, p, sc_w, dq, dk, dv, dact, ssm_w, ssm_b, ddt, dz, name, tr=256):
    t = p.shape[0]
    n_tiles = t // tr
    sci, xi = OFF_SC // 768, OFF_XBC // 768
    base = OFF_SC
    assert base == IN_PAD - base
    o_sc, o_qkv, o_xbc, o_dt, o_z, o_end = (c - base for c in (OFF_SC, OFF_QKV, OFF_XBC, OFF_DT, OFF_Z, IN_PAD))

    def body(dp_ref, dya_ref, dyan_ref, sc_ref, scp_ref, scn_ref, scw_ref, dq_ref, dk_ref, dv_ref,
             dact_ref, dactn_ref, xbc_ref, xbcp_ref, xbcn_ref, sw_ref, sb_ref, ddt_ref, dz_ref,
             o_ref, dscw_ref, dsw_ref, dsb_ref):
        i = pl.program_id(0)

        @pl.when(i == 0)
        def _():
            dscw_ref[...] = jnp.zeros_like(dscw_ref)
            dsw_ref[...] = jnp.zeros_like(dsw_ref)
            dsb_ref[...] = jnp.zeros_like(dsb_ref)

        first = (i > 0).astype(F32)
        last = (i < n_tiles - 1).astype(F32)
        del dp_ref
        sc = sc_ref[...].astype(F32)
        scp = _prev8(scp_ref) * first
        scn = _next8(scn_ref) * last
        u = sc[:, 256:512] * sc[:, 512:768]
        up = scp[:, 256:512] * scp[:, 512:768]
        dya_v = dya_ref[...]
        cv = _conv_taps(u, up, scw_ref, 3)
        o_ref[:, o_sc:o_sc + 256] = (dya_v * cv).astype(BF16)
        dcv = dya_v * sc[:, 0:256]
        dcvn = _next8(dyan_ref) * last * scn[:, 0:256]
        du = None
        for k in range(3):
            sh = 2 - k
            term = scw_ref[k:k + 1, :] * _shift_up(dcv, dcvn, sh)
            du = term if du is None else du + term
            dscw_ref[k:k + 1, :] += jnp.sum(dcv * _shift_down(u, up, sh), axis=0, keepdims=True)
        o_ref[:, o_sc + 256:o_sc + 512] = (du * sc[:, 512:768]).astype(BF16)
        o_ref[:, o_sc + 512:o_sc + 768] = (du * sc[:, 256:512]).astype(BF16)
        o_ref[:, o_qkv:o_qkv + 256] = (dq_ref[...] * 0.125).astype(BF16)
        o_ref[:, o_qkv + 256:o_qkv + 512] = dk_ref[...].astype(BF16)
        o_ref[:, o_qkv + 512:o_qkv + 768] = dv_ref[...].astype(BF16)
        xb = xbc_ref[...].astype(F32)
        xbp = _prev8(xbcp_ref) * first
        xbn = _next8(xbcn_ref)
        xc = _conv_taps(xb, xbp, sw_ref, 4) + sb_ref[...]
        xcn = _conv_taps(xbn, xb[tr - 8:, :], sw_ref, 4) + sb_ref[...]
        dxc = dact_ref[...] * _dsilu(xc)
        dxcn = _next8(dactn_ref) * _dsilu(xcn) * last
        dxb = None
        for k in range(4):
            sh = 3 - k
            term = sw_ref[k:k + 1, :] * _shift_up(dxc, dxcn, sh)
            dxb = term if dxb is None else dxb + term
            dsw_ref[k:k + 1, :] += jnp.sum(dxc * _shift_down(xb, xbp, sh), axis=0, keepdims=True)
        dsb_ref[...] += jnp.sum(dxc, axis=0, keepdims=True)
        o_ref[:, o_xbc:o_xbc + 768] = dxb.astype(BF16)
        o_ref[:, o_dt:o_dt + 128] = ddt_ref[...].astype(BF16)
        o_ref[:, o_dt + 128:o_z] = jnp.zeros((tr, o_z - o_dt - 128), BF16)
        o_ref[:, o_z:o_end] = dz_ref[...].astype(BF16)

    return rows_call(
        name, body, t, tr,
        [(dp, "any"), (dya, "row"), (dya, ("next8", 256, 0)),
         (p, ("row", 768, sci)), (p, ("prev8", 768, sci)), (p, ("next8", 768, sci)), (sc_w, "full"),
         (dq, "row"), (dk, "row"), (dv, "row"),
         (dact, "row"), (dact, ("next8", 768, 0)),
         (p, ("row", 768, xi)), (p, ("prev8", 768, xi)), (p, ("next8", 768, xi)), (ssm_w, "full"), (ssm_b, "full"),
         (ddt, "row"), (dz, "row")],
        [((t, IN_PAD), BF16, ("row", IN_PAD - base, 1)), ((8, 256), F32, "acc"), ((8, 768), F32, "acc"),
         ((1, 768), F32, "acc")], aliases={0: 0})


def adamw_flat(slots, w, m, v, name, tr=512):
    n_slots, rows, lanes = slots.shape
    tr = max(d for d in range(8, min(tr, rows) + 1, 8) if rows % d == 0) if rows % 8 == 0 else rows
    bc1 = 1.0 - ADAM_B1 ** ADAM_STEP
    bc2 = 1.0 - ADAM_B2 ** ADAM_STEP

    def body(s_ref, w_ref, m_ref, v_ref, g_ref, d_ref, mo_ref, vo_ref):
        g = s_ref[0].astype(F32)
        for k in range(1, n_slots):
            g = g + s_ref[k].astype(F32)
        mn = ADAM_B1 * m_ref[...] + (1.0 - ADAM_B1) * g
        vn = ADAM_B2 * v_ref[...] + (1.0 - ADAM_B2) * (g * g)
        m_hat = mn / bc1
        v_hat = vn / bc2
        g_ref[...] = g
        d_ref[...] = -ADAM_LR * (m_hat / (jnp.sqrt(v_hat) + ADAM_EPS) + ADAM_WD * w_ref[...])
        mo_ref[...] = mn
        vo_ref[...] = vn

    tile = pl.BlockSpec((tr, lanes), lambda i: (i, 0))
    shp = jax.ShapeDtypeStruct((rows, lanes), F32)
    return pl.pallas_call(
        body, name=name, grid=(rows // tr,),
        in_specs=[pl.BlockSpec((n_slots, tr, lanes), lambda i: (0, i, 0)), tile, tile, tile],
        out_specs=[tile] * 4, out_shape=[shp] * 4,
        compiler_params=_cparams(("parallel",)),
    )(slots, w, m, v)


def _split_dot(x, tri):
    hi = x.astype(BF16)
    lo = (x - hi.astype(F32)).astype(BF16)
    return jnp.dot(hi, tri, preferred_element_type=F32) + jnp.dot(lo, tri, preferred_element_type=F32)


_NT = (((1,), (1,)), ((), ()))
_TN = (((0,), (0,)), ((), ()))

SBA_EXP_ZERO = -104.0
SBA_SKIPPED = -1e30


def sba_fwd(qkv, name, bq=256, bk=256):
    t = qkv.shape[0]
    ratio = bq // bk
    assert bq == ratio * bk and t // bk <= LANES

    def body(q_ref, k_ref, v_ref, o_ref, runs_ref, acc_s, run_s):
        i = pl.program_id(1)
        lane = lax.broadcasted_iota(jnp.int32, (1, LANES), 1)
        lane_q = lax.broadcasted_iota(jnp.int32, (bq, LANES), 1)
        qi = lax.broadcasted_iota(jnp.int32, (bq, bk), 0) + i * bq
        kj = lax.broadcasted_iota(jnp.int32, (bq, bk), 1)
        later = (lax.broadcasted_iota(jnp.int32, (bk, bk), 0) > lax.broadcasted_iota(jnp.int32, (bk, bk), 1)).astype(BF16)
        qv = q_ref[...]
        qms = [jnp.where(hm, qv, jnp.zeros_like(qv)) for hm in (lane < 64, lane >= 64)]
        acc_s[...] = jnp.zeros_like(acc_s)
        run_s[...] = jnp.zeros_like(run_s)
        runs_ref[...] = jnp.full(runs_ref.shape, SBA_SKIPPED, F32)

        def tiles(specs):
            chains = [(ti, hh) for ti in range(len(specs)) for hh in range(2)]
            kb = [k_ref[pl.ds(pl.multiple_of(j * bk, bk), bk), :] for j, _ in specs]
            vb = [v_ref[pl.ds(pl.multiple_of(j * bk, bk), bk), :] for j, _ in specs]
            mask = [(kj + j * bk) < qi if masked else None for j, masked in specs]
            s = {c: lax.dot_general(qms[c[1]], kb[c[0]], _NT, preferred_element_type=F32) for c in chains}
            lk = {c: _log_sigmoid_neg(s[c]) for c in chains}
            lk = {c: lk[c] if mask[c[0]] is None else jnp.where(mask[c[0]], lk[c], 0.0) for c in chains}
            w = {c: jnp.dot(lk[c].astype(BF16), later, preferred_element_type=F32) for c in chains}
            run = {}
            for hh in range(2):
                carry = run_s[hh]
                for ti in range(len(specs)):
                    run[ti, hh] = carry
                    carry = carry + jnp.sum(lk[ti, hh], axis=1, keepdims=True)
                run_s[hh] = carry
            a = {c: jnp.exp(s[c] + lk[c] + w[c] + run[c]) for c in chains}
            a = {c: a[c] if mask[c[0]] is None else jnp.where(mask[c[0]], a[c], 0.0) for c in chains}
            for ti, hh in chains:
                acc_s[hh] += jnp.dot(a[ti, hh].astype(BF16), vb[ti], preferred_element_type=F32)
                runs_ref[hh] = jnp.where(lane_q == specs[ti][0], run[ti, hh], runs_ref[hh])

        for d in range(ratio):
            tiles([((i + 1) * ratio - 1 - d, True)])

        def live():
            return jnp.maximum(jnp.max(run_s[0]), jnp.max(run_s[1])) >= SBA_EXP_ZERO

        def cond(state):
            n, go = state
            return jnp.logical_and(n < i * ratio, go)

        def step(state):
            n, _ = state
            tiles([(i * ratio - 1 - n, False)])
            return n + 1, live()

        lax.while_loop(cond, step, (jnp.int32(0), live()))
        o_ref[...] = jnp.where(lane < 64, acc_s[0], acc_s[1])

    return pl.pallas_call(
        body, name=name, grid=(2, t // bq),
        in_specs=[pl.BlockSpec((bq, LANES), lambda p, i: (i, p)), pl.BlockSpec((t, LANES), lambda p, i: (0, 2 + p)),
                  pl.BlockSpec((t, LANES), lambda p, i: (0, 4 + p))],
        out_specs=[pl.BlockSpec((bq, LANES), lambda p, i: (i, p)), pl.BlockSpec((2, bq, LANES), lambda p, i: (p, i, 0))],
        out_shape=[jax.ShapeDtypeStruct((t, SB_WIDTH), F32), jax.ShapeDtypeStruct((4, t, LANES), F32)],
        scratch_shapes=[pltpu.VMEM((2, bq, LANES), F32), pltpu.VMEM((2, bq, 1), F32)],
        compiler_params=_cparams(("parallel", "parallel")),
    )(qkv, qkv, qkv)


def sba_bwd(qkv, runs, do, name, bq=256, bk=256):
    t = qkv.shape[0]
    ratio = bq // bk
    assert bq == ratio * bk
    nq = t // bq

    def body(q_ref, k_ref, v_ref, runs_ref, do_ref, dq_ref, dk_hbm, dv_hbm, dk_s, dv_s, sem, dq_s, rg_s):
        p = pl.program_id(0)
        i = pl.program_id(1)

        @pl.when(i == 0)
        def _():
            dk_s[...] = jnp.zeros_like(dk_s)
            dv_s[...] = jnp.zeros_like(dv_s)

        lane = lax.broadcasted_iota(jnp.int32, (1, LANES), 1)
        qi = lax.broadcasted_iota(jnp.int32, (bq, bk), 0) + i * bq
        kj = lax.broadcasted_iota(jnp.int32, (bq, bk), 1)
        r2 = lax.broadcasted_iota(jnp.int32, (bk, bk), 0)
        c2 = lax.broadcasted_iota(jnp.int32, (bk, bk), 1)
        later = (r2 > c2).astype(BF16)
        earlier = (r2 < c2).astype(BF16)
        qv = q_ref[...]
        dov = do_ref[...]
        heads = range(2)
        hms = (lane < 64, lane >= 64)
        qms = [jnp.where(hm, qv, jnp.zeros_like(qv)) for hm in hms]
        doms = [jnp.where(hm, dov, 0.0).astype(BF16) for hm in hms]
        runs = [runs_ref[hh] for hh in heads]
        dq_s[...] = jnp.zeros_like(dq_s)
        rg_s[...] = jnp.zeros_like(rg_s)

        def tiles(specs):
            nt = len(specs)
            chains = [(ti, hh) for ti in range(nt) for hh in heads]
            starts = [pl.multiple_of(j * bk, bk) for j, _ in specs]
            kb = [k_ref[pl.ds(st, bk), :] for st in starts]
            vb = [v_ref[pl.ds(st, bk), :] for st in starts]
            mask = [(kj + j * bk) < qi if masked else None for j, masked in specs]
            s = {c: lax.dot_general(qms[c[1]], kb[c[0]], _NT, preferred_element_type=F32) for c in chains}
            da = {c: lax.dot_general(doms[c[1]], vb[c[0]], _NT, preferred_element_type=F32) for c in chains}
            lk_raw = {c: _log_sigmoid_neg(s[c]) for c in chains}
            lk = {c: lk_raw[c] if mask[c[0]] is None else jnp.where(mask[c[0]], lk_raw[c], 0.0) for c in chains}
            w = {c: jnp.dot(lk[c].astype(BF16), later, preferred_element_type=F32) for c in chains}
            run = {c: jnp.sum(jnp.where(lane == specs[c[0]][0], runs[c[1]], 0.0), axis=1, keepdims=True) for c in chains}
            a = {c: jnp.exp(s[c] + lk[c] + w[c] + run[c]) for c in chains}
            a = {c: a[c] if mask[c[0]] is None else jnp.where(mask[c[0]], a[c], 0.0) for c in chains}
            g = {c: a[c] * da[c] for c in chains}
            rg = {}
            for hh in heads:
                carry = rg_s[hh]
                for ti in range(nt):
                    rg[ti, hh] = carry
                    carry = carry + jnp.sum(g[ti, hh], axis=1, keepdims=True)
                rg_s[hh] = carry
            cpre = {c: rg[c] + _split_dot(g[c], earlier) for c in chains}
            dz = {c: g[c] - jnp.exp(s[c] + lk_raw[c]) * (g[c] + cpre[c]) for c in chains}
            dz = {c: (dz[c] if mask[c[0]] is None else jnp.where(mask[c[0]], dz[c], 0.0)).astype(BF16) for c in chains}
            for ti, hh in chains:
                dq_s[hh] += jnp.dot(dz[ti, hh], kb[ti], preferred_element_type=F32)
            for ti in range(nt):
                dk = lax.dot_general(dz[ti, 0], qms[0], _TN, preferred_element_type=F32)
                dk_s[pl.ds(starts[ti], bk), :] += dk + lax.dot_general(dz[ti, 1], qms[1], _TN, preferred_element_type=F32)
                dv = lax.dot_general(a[ti, 0].astype(BF16), doms[0], _TN, preferred_element_type=F32)
                dv_s[pl.ds(starts[ti], bk), :] += dv + lax.dot_general(a[ti, 1].astype(BF16), doms[1], _TN,
                                                                       preferred_element_type=F32)

        live = jnp.maximum(jnp.max(runs[0], axis=0, keepdims=True), jnp.max(runs[1], axis=0, keepdims=True)) >= SBA_EXP_ZERO
        first = jnp.minimum(jnp.min(jnp.where(live, lane, LANES)), i * ratio)

        def step(j, carry):
            tiles([(j, False)])
            return carry

        if ratio == 1:
            lax.fori_loop(first, i - 1, step, 0)

            @pl.when(i == 0)
            def _():
                tiles([(i, True)])

            @pl.when(i > 0)
            def _():
                tiles([(i - 1, False), (i, True)])
        else:
            lax.fori_loop(first, i * ratio, step, 0)
            for d in range(ratio):
                tiles([(i * ratio + d, True)])

        dq_ref[...] = jnp.where(lane < 64, dq_s[0], dq_s[1])

        @pl.when(i == nq - 1)
        def _():
            col = pl.multiple_of(p * LANES, LANES)
            ck = pltpu.make_async_copy(dk_s, dk_hbm.at[:, pl.ds(col, LANES)], sem.at[0])
            cv = pltpu.make_async_copy(dv_s, dv_hbm.at[:, pl.ds(col, LANES)], sem.at[1])
            ck.start()
            cv.start()
            ck.wait()
            cv.wait()

    shp = jax.ShapeDtypeStruct((t, SB_WIDTH), F32)
    tile = pl.BlockSpec((bq, LANES), lambda p, i: (i, p))
    return pl.pallas_call(
        body, name=name, grid=(2, nq),
        in_specs=[tile, pl.BlockSpec((t, LANES), lambda p, i: (0, 2 + p)), pl.BlockSpec((t, LANES), lambda p, i: (0, 4 + p)),
                  pl.BlockSpec((2, bq, LANES), lambda p, i: (p, i, 0)), tile],
        out_specs=[tile, pl.BlockSpec(memory_space=pl.ANY), pl.BlockSpec(memory_space=pl.ANY)],
        out_shape=[shp, shp, shp],
        scratch_shapes=[pltpu.VMEM((t, LANES), F32), pltpu.VMEM((t, LANES), F32), pltpu.SemaphoreType.DMA((2,)),
                        pltpu.VMEM((2, bq, LANES), F32), pltpu.VMEM((2, bq, 1), F32)],
        compiler_params=_cparams(("arbitrary", "arbitrary")),
    )(qkv, qkv, qkv, runs, do)


def _ssd_consts():
    ln = SSM_CHUNK
    ri = lax.broadcasted_iota(jnp.int32, (ln, ln), 0)
    ci = lax.broadcasted_iota(jnp.int32, (ln, ln), 1)
    eh = lax.broadcasted_iota(jnp.int32, (LANES, SSM_INNER), 0)
    el = lax.broadcasted_iota(jnp.int32, (LANES, SSM_INNER), 1)
    expand = (jnp.right_shift(el, 6) == eh).astype(BF16)
    th = lax.broadcasted_iota(jnp.int32, (SSM_INNER, LANES), 1)
    tl = lax.broadcasted_iota(jnp.int32, (SSM_INNER, LANES), 0)
    reduce = (jnp.right_shift(tl, 6) == th).astype(BF16)
    return ri, ci, expand, reduce


def _dot_f32(a, b):
    return jnp.dot(a, b, precision=HI, preferred_element_type=F32)


def _split3(x):
    hi = x.astype(BF16)
    r1 = x - hi.astype(F32)
    mid = r1.astype(BF16)
    lo = (r1 - mid.astype(F32)).astype(BF16)
    return hi, mid, lo


def _dot_hi(a, b):
    if a.dtype == BF16:
        return sum(jnp.dot(a, t, preferred_element_type=F32) for t in _split3(b))
    return sum(jnp.dot(t, b, preferred_element_type=F32) for t in _split3(a))


def _ssd_prelude(xbc_ref, dt_ref, dtt_ref, hpr_ref, hpc_ref, ri, ci, expand):
    ln = SSM_CHUNK
    xs = xbc_ref[:, 0:512]
    bm = xbc_ref[:, 512:640]
    cm = xbc_ref[:, 640:768]
    dtb_r = hpr_ref[0:1, :]
    aneg_r = -jnp.exp(hpr_ref[1:2, :])
    pre = dt_ref[...] + dtb_r
    dt = _softplus(pre)
    a = dt * aneg_r
    dtt = _softplus(dtt_ref[...] + hpc_ref[0:8, :])
    att = dtt * (-jnp.exp(hpc_ref[8:16, :]))
    tril = (ri >= ci).astype(BF16)
    triu = (ri <= ci).astype(BF16)
    acs = _dot_hi(tril, a)
    acst = _dot_hi(att, triu)
    acs_e = _dot_hi(acs, expand)
    dt_e = _dot_hi(dt, expand)
    last_e = acs_e[ln - 1:ln, :]
    e_e = jnp.exp(acs_e)
    w_e = jnp.exp(last_e - acs_e)
    dec_e = jnp.exp(last_e)
    xdt = xs * dt_e
    return dict(xs=xs, bm=bm, cm=cm, pre=pre, dt=dt, aneg_r=aneg_r, acs=acs, acst=acst, dt_e=dt_e, e_e=e_e,
                w_e=w_e, dec_e=dec_e, xdt=xdt, triu=triu)


def ssd_fwd(act, p, dt32, dtt, hp_rows, hp_cols, d_e, norm_w, name):
    t = act.shape[0]
    ln = SSM_CHUNK
    nc = t // ln

    def body(xbc_ref, dt_ref, z_ref, dtt_ref, hpr_ref, hpc_ref, d_ref, nw_ref, yc_ref, y_ref, sto_ref, st):
        @pl.when(pl.program_id(0) == 0)
        def _():
            st[...] = jnp.zeros_like(st)

        ri, ci, expand, _ = _ssd_consts()
        q = _ssd_prelude(xbc_ref, dt_ref, dtt_ref, hpr_ref, hpc_ref, ri, ci, expand)
        lane = lax.broadcasted_iota(jnp.int32, (1, LANES), 1)
        rown = lax.broadcasted_iota(jnp.int32, (LANES, 1), 0)
        low = lane < 64
        mask = ri >= ci
        xdt_b = q["xdt"].astype(BF16)
        xw_b = (q["xdt"] * q["w_e"]).astype(BF16)
        bt = q["bm"].T.astype(BF16)
        cb_ = q["cm"].astype(BF16)
        y_pairs = []
        for g in range(2):
            gm = low if g == 0 else jnp.logical_not(low)
            rm = (rown < 64) if g == 0 else (rown >= 64)
            cg = jnp.where(gm, cb_, jnp.zeros_like(cb_))
            cb = jnp.dot(cg, bt, preferred_element_type=F32)
            for pp in range(2):
                pi = 2 * g + pp
                sl = slice(LANES * pi, LANES * (pi + 1))
                xp = xdt_b[:, sl]
                yd = []
                for hh in range(2):
                    h = 2 * pi + hh
                    diff = q["acs"][:, h:h + 1] - q["acst"][h:h + 1, :]
                    lam = jnp.exp(jnp.where(mask, diff, -jnp.inf))
                    yd.append(jnp.dot((cb * lam).astype(BF16), xp, preferred_element_type=F32))
                sp = st[pi]
                sto_ref[0, pi] = sp
                yoff = jnp.dot(cg, sp.astype(BF16), preferred_element_type=F32) * q["e_e"][:, sl]
                upd = jnp.dot(bt, xw_b[:, sl], preferred_element_type=F32)
                st[pi] = q["dec_e"][:, sl] * sp + jnp.where(rm, upd, 0.0)
                y_pairs.append(jnp.where(low, yd[0], yd[1]) + yoff)
        y = jnp.concatenate(y_pairs, axis=1) + q["xs"] * d_ref[...]
        y_ref[...] = y
        yg = y * _silu(z_ref[...].astype(F32))
        for g in range(2):
            sl = slice(256 * g, 256 * (g + 1))
            seg = yg[:, sl]
            yc_ref[:, sl] = (seg * _rstd(seg) * nw_ref[:, sl]).astype(BF16)

    return pl.pallas_call(
        body, name=name, grid=(nc,),
        in_specs=[pl.BlockSpec((ln, 768), lambda c: (c, 0)), pl.BlockSpec((ln, LANES), lambda c: (c, 0)),
                  pl.BlockSpec((ln, 512), lambda c: (c, OFF_Z // 512)), pl.BlockSpec((8, ln), lambda c: (0, c)),
                  pl.BlockSpec((8, LANES), lambda c: (0, 0)), pl.BlockSpec((16, ln), lambda c: (0, 0)),
                  pl.BlockSpec((1, 512), lambda c: (0, 0)), pl.BlockSpec((1, 512), lambda c: (0, 0))],
        out_specs=[pl.BlockSpec((ln, 512), lambda c: (c, 0)), pl.BlockSpec((ln, 512), lambda c: (c, 0)),
                   pl.BlockSpec((1, 4, LANES, LANES), lambda c: (c, 0, 0, 0))],
        out_shape=[jax.ShapeDtypeStruct((t, 512), BF16), jax.ShapeDtypeStruct((t, 512), F32),
                   jax.ShapeDtypeStruct((nc, 4, LANES, LANES), F32)],
        scratch_shapes=[pltpu.VMEM((4, LANES, LANES), F32)],
        compiler_params=_cparams(("arbitrary",)),
    )(act, dt32, p, dtt, hp_rows, hp_cols, d_e, norm_w)


def ssd_bwd(dyc, y, act, p, dt32, dtt, states, hp_rows, hp_cols, d_e, norm_w, name):
    t = act.shape[0]
    ln = SSM_CHUNK
    nc = t // ln

    def body(dyc_ref, y_ref, xbc_ref, dt_ref, z_ref, dtt_ref, st_ref, hpr_ref, hpc_ref, d_ref, nw_ref,
             dz_ref, dact_ref, ddt_ref, dnw_ref, dd_ref, dhp_ref, ds):
        @pl.when(pl.program_id(0) == 0)
        def _():
            ds[...] = jnp.zeros_like(ds)
            dnw_ref[...] = jnp.zeros_like(dnw_ref)
            dd_ref[...] = jnp.zeros_like(dd_ref)
            dhp_ref[...] = jnp.zeros_like(dhp_ref)

        ri, ci, expand, reduce = _ssd_consts()
        q = _ssd_prelude(xbc_ref, dt_ref, dtt_ref, hpr_ref, hpc_ref, ri, ci, expand)
        lane = lax.broadcasted_iota(jnp.int32, (1, LANES), 1)
        rown = lax.broadcasted_iota(jnp.int32, (LANES, 1), 0)
        low = lane < 64
        mask = ri >= ci
        mask_t = ci >= ri
        xs, xdt, acs, acst = q["xs"], q["xdt"], q["acs"], q["acst"]
        yv, zv, nw = y_ref[...], z_ref[...].astype(F32), nw_ref[...]
        sg = _sigmoid(zv)
        zz = zv * sg
        yg = yv * zz
        dycv = dyc_ref[...]
        u = dycv * nw
        dyg_parts, dnw_parts = [], []
        for g in range(2):
            sl = slice(256 * g, 256 * (g + 1))
            seg = yg[:, sl]
            rr = _rstd(seg)
            nrm = seg * rr
            dyg_parts.append(rr * (u[:, sl] - nrm * jnp.mean(nrm * u[:, sl], axis=-1, keepdims=True)))
            dnw_parts.append(jnp.sum(dycv[:, sl] * nrm, axis=0, keepdims=True))
        dyg = jnp.concatenate(dyg_parts, axis=1)
        dnw_ref[...] += jnp.concatenate(dnw_parts, axis=1)
        dy = dyg * zz
        dz_ref[...] = dyg * yv * (sg * (1.0 + zv * (1.0 - sg)))
        dd_ref[...] += jnp.sum(dy * xs, axis=0, keepdims=True)
        dxs = dy * d_ref[...]
        dy_b = dy.astype(BF16)
        xdt_b = xdt.astype(BF16)
        xw_b = (xdt * q["w_e"]).astype(BF16)
        bt = q["bm"].T.astype(BF16)
        ct = q["cm"].T.astype(BF16)
        cb_ = q["cm"].astype(BF16)
        bb_ = q["bm"].astype(BF16)
        dacs = jnp.zeros((ln, LANES), F32)
        dc = jnp.zeros((ln, LANES), F32)
        db = jnp.zeros((ln, LANES), F32)
        dxdt_pairs, yoffdy_pairs, dwe_pairs, ddec_pairs = [], [], [], []
        for g in range(2):
            gm = low if g == 0 else jnp.logical_not(low)
            rm = (rown < 64) if g == 0 else (rown >= 64)
            cg = jnp.where(gm, cb_, jnp.zeros_like(cb_))
            bg = jnp.where(gm, bb_, jnp.zeros_like(bb_))
            cb = jnp.dot(cg, bt, preferred_element_type=F32)
            cbt = jnp.dot(bg, ct, preferred_element_type=F32)
            dcb = jnp.zeros((ln, ln), F32)
            dcbt = jnp.zeros((ln, ln), F32)
            for pp in range(2):
                pi = 2 * g + pp
                sl = slice(LANES * pi, LANES * (pi + 1))
                xp = xdt_b[:, sl]
                dyp = dy_b[:, sl]
                xpt = xdt[:, sl].T.astype(BF16)
                dypt = dy[:, sl].T.astype(BF16)
                dxdt_p = jnp.zeros((ln, LANES), F32)
                for hh in range(2):
                    h = 2 * pi + hh
                    hm = low if hh == 0 else jnp.logical_not(low)
                    col = acs[:, h:h + 1]
                    row = acst[h:h + 1, :]
                    lam = jnp.exp(jnp.where(mask, col - row, -jnp.inf))
                    lam_t = jnp.exp(jnp.where(mask_t, row - col, -jnp.inf))
                    m = cb * lam
                    m_t = cbt * lam_t
                    dyh = jnp.where(hm, dyp, jnp.zeros_like(dyp))
                    xh = jnp.where(hm, xp, jnp.zeros_like(xp))
                    dm = jnp.dot(dyh, xpt, preferred_element_type=F32)
                    dm_t = jnp.dot(xh, dypt, preferred_element_type=F32)
                    dcb = dcb + dm * lam
                    dcbt = dcbt + dm_t * lam_t
                    rs = jnp.sum(dm * m, axis=1, keepdims=True) - jnp.sum(dm_t * m_t, axis=1, keepdims=True)
                    dacs = dacs + jnp.where(lane == h, rs, 0.0)
                    dxdt_p = dxdt_p + jnp.dot(m_t.astype(BF16), dyh, preferred_element_type=F32)
                sp = st_ref[0, pi]
                sp_b = sp.astype(BF16)
                dsn = ds[pi]
                dsn_b = dsn.astype(BF16)
                e_p, w_p, dec_p = q["e_e"][:, sl], q["w_e"][:, sl], q["dec_e"][:, sl]
                yoff = jnp.dot(cg, sp_b, preferred_element_type=F32) * e_p
                dyo = dy[:, sl] * e_p
                dyo_b = dyo.astype(BF16)
                dc = dc + lax.dot_general(dyo_b, sp_b, _NT, preferred_element_type=F32)
                ds_prev = dec_p * dsn + jnp.where(rm, jnp.dot(ct, dyo_b, preferred_element_type=F32), 0.0)
                yoffdy_pairs.append(dy[:, sl] * yoff)
                dxw = jnp.dot(bg, dsn_b, preferred_element_type=F32)
                db = db + lax.dot_general(xw_b[:, sl], dsn_b, _NT, preferred_element_type=F32)
                dxdt_p = dxdt_p + dxw * w_p
                dwe_pairs.append(dxw * xdt[:, sl])
                ddec_pairs.append(jnp.sum(dsn * sp, axis=0, keepdims=True))
                ds[pi] = ds_prev
                dxdt_pairs.append(dxdt_p)
            dc = dc + jnp.dot(dcb.astype(BF16), bg, preferred_element_type=F32)
            db = db + jnp.dot(dcbt.astype(BF16), cg, preferred_element_type=F32)
        dxdt = jnp.concatenate(dxdt_pairs, axis=1)
        yoffdy = jnp.concatenate(yoffdy_pairs, axis=1)
        dwe = jnp.concatenate(dwe_pairs, axis=1)
        ddec_e = jnp.broadcast_to(jnp.concatenate(ddec_pairs, axis=1), (8, SSM_INNER))
        last = acs[ln - 1:ln, :]
        w_col = jnp.exp(last - acs)
        dw_col = _dot_hi(dwe, reduce) * w_col
        dacs = dacs + _dot_hi(yoffdy, reduce) - dw_col
        dlast = jnp.sum(dw_col, axis=0, keepdims=True) + jnp.exp(last) * _dot_hi(ddec_e, reduce)[0:1, :]
        rowi = lax.broadcasted_iota(jnp.int32, (ln, 1), 0)
        dacs = dacs + jnp.where(rowi == ln - 1, dlast, 0.0)
        da = _dot_hi(q["triu"], dacs)
        ddt = da * q["aneg_r"] + _dot_hi(dxdt * xs, reduce)
        ddt_raw = jnp.where(lane < SSM_HEADS, ddt * _sigmoid(q["pre"]), 0.0)
        ddt_ref[...] = ddt_raw
        dhp_ref[0:1, :] += jnp.sum(ddt_raw, axis=0, keepdims=True)
        dhp_ref[1:2, :] += jnp.where(lane < SSM_HEADS, jnp.sum(da * q["dt"], axis=0, keepdims=True) * q["aneg_r"], 0.0)
        dact_ref[:, 0:512] = dxs + dxdt * q["dt_e"]
        dact_ref[:, 512:640] = db
        dact_ref[:, 640:768] = dc

    rev = lambda c: nc - 1 - c
    return pl.pallas_call(
        body, name=name, grid=(nc,),
        in_specs=[pl.BlockSpec((ln, 512), lambda c: (rev(c), 0)), pl.BlockSpec((ln, 512), lambda c: (rev(c), 0)),
                  pl.BlockSpec((ln, 768), lambda c: (rev(c), 0)),
                  pl.BlockSpec((ln, LANES), lambda c: (rev(c), 0)),
                  pl.BlockSpec((ln, 512), lambda c: (rev(c), OFF_Z // 512)), pl.BlockSpec((8, ln), lambda c: (0, rev(c))),
                  pl.BlockSpec((1, 4, LANES, LANES), lambda c: (rev(c), 0, 0, 0)),
                  pl.BlockSpec((8, LANES), lambda c: (0, 0)), pl.BlockSpec((16, ln), lambda c: (0, 0)),
                  pl.BlockSpec((1, 512), lambda c: (0, 0)), pl.BlockSpec((1, 512), lambda c: (0, 0))],
        out_specs=[pl.BlockSpec((ln, 512), lambda c: (rev(c), 0)), pl.BlockSpec((ln, 768), lambda c: (rev(c), 0)),
                   pl.BlockSpec((ln, LANES), lambda c: (rev(c), 0)), pl.BlockSpec((1, 512), lambda c: (0, 0)),
                   pl.BlockSpec((1, 512), lambda c: (0, 0)), pl.BlockSpec((8, LANES), lambda c: (0, 0))],
        out_shape=[jax.ShapeDtypeStruct((t, 512), F32), jax.ShapeDtypeStruct((t, 768), F32),
                   jax.ShapeDtypeStruct((t, LANES), F32), jax.ShapeDtypeStruct((1, 512), F32),
                   jax.ShapeDtypeStruct((1, 512), F32), jax.ShapeDtypeStruct((8, LANES), F32)],
        scratch_shapes=[pltpu.VMEM((4, LANES, LANES), F32)],
        compiler_params=_cparams(("arbitrary",)),
    )(dyc, y, act, dt32, p, dtt, states, hp_rows, hp_cols, d_e, norm_w)


def mod_shard_fwd(c_all, mod_w, mod_b_shard, name):
    def body(c_ref, w_ref, b_ref, o_ref):
        sc = _silu(c_ref[...])
        for l in range(DEPTH):
            o_ref[l] = _dot_f32(sc, w_ref[l]) + b_ref[l]

    return pl.pallas_call(body, name=name, out_shape=jax.ShapeDtypeStruct((DEPTH, N_DEV, mod_w.shape[2]), F32),
                          compiler_params=_cparams())(c_all, mod_w, mod_b_shard)


def mod_w_grad(c_all, dmod_shard, name):
    def body(c_ref, d_ref, o_ref):
        sc = _silu(c_ref[...])
        for l in range(DEPTH):
            o_ref[l] = lax.dot_general(sc, d_ref[l], _TN, precision=HI, preferred_element_type=F32)

    return pl.pallas_call(body, name=name, out_shape=jax.ShapeDtypeStruct((DEPTH, D_MODEL, dmod_shard.shape[2]), F32),
                          compiler_params=_cparams())(c_all, dmod_shard)


_BIG = ("w_in", "sc_conv_w", "ssm_conv_w", "w_sc_out", "w_sb_out", "w_ssm_out", "w_o", "w_ffn_in", "w_ffn_out")
_ROW_SHARDED = ("w_o", "w_ffn_out")
_CONV = ("sc_conv_w", "ssm_conv_w")
_SMALL = ("mod_b", "g_pre_mix", "g_post_mix", "g_pre_ffn", "g_post_ffn", "ssm_conv_b", "ssm_dt_bias", "ssm_a_log",
          "ssm_d", "ssm_norm_w")
_WEIGHTS = ("mod_w", "mod_b", "g_pre_mix", "g_post_mix", "g_pre_ffn", "g_post_ffn", "w_in", "sc_conv_w", "ssm_conv_w",
            "ssm_conv_b", "ssm_dt_bias", "ssm_a_log", "ssm_d", "ssm_norm_w", "w_sc_out", "w_sb_out", "w_ssm_out", "w_o",
            "w_ffn_in", "w_ffn_out")


def _gathered_to_full(g, row_sharded):
    _, dep, r, c = g.shape
    if row_sharded:
        return g.transpose(1, 0, 2, 3).reshape(dep, N_DEV * r, c)
    return g.transpose(1, 2, 0, 3).reshape(dep, r, N_DEV * c)


def _full_to_slots(w, row_sharded):
    dep, r, c = w.shape
    if row_sharded:
        return w.reshape(dep, N_DEV, r // N_DEV, c).transpose(1, 0, 2, 3).reshape(N_DEV, dep * (r // N_DEV), c)
    return w.reshape(dep, r, N_DEV, c // N_DEV).transpose(2, 0, 1, 3).reshape(N_DEV, dep * r, c // N_DEV)


def _pad_in_proj(shards):
    width = shards.shape[2]

    def cols(lo, hi):
        out = []
        while lo < hi:
            dev, a = divmod(lo, width)
            b = min(width, a + hi - lo)
            out.append(shards[dev, :, a:b])
            lo += b - a
        return out

    pad = jnp.zeros((shards.shape[1], OFF_Z - OFF_DT - 8), shards.dtype)
    return jnp.concatenate(cols(2824, 5896) + cols(0, 768) + cols(768, 1536) + cols(2048, 2816) + cols(2816, 2824)
                           + [pad] + cols(1536, 2048), axis=1)


def _in_proj_slots(dw_layers):
    width = IN_PROJ // N_DEV
    segments = ((0, 768, OFF_SC), (768, 1536, OFF_QKV), (1536, 2048, OFF_Z), (2048, 2816, OFF_XBC),
                (2816, 2824, OFF_DT), (2824, IN_PROJ, OFF_GATES))

    def internal(lo, hi):
        out = []
        for s0, s1, off in segments:
            a, b = max(lo, s0), min(hi, s1)
            if a < b:
                out.append((off + a - s0, off + b - s0))
        return out

    slots = []
    for d in range(N_DEV):
        pieces = internal(width * d, width * (d + 1))
        slots.append(jnp.concatenate([jnp.concatenate([w[:, a:b] for a, b in pieces], axis=1) for w in dw_layers], axis=0))
    return jnp.stack(slots)


def _ffn_in_keep_give(halves, cidx):
    width = 2 * FFN_HIDDEN // N_DEV
    keep, give = [], []
    for chip in range(4):
        src = 0 if chip < 2 else 1
        for out, core in ((keep, cidx), (give, 1 - cidx)):
            col0 = width * ((2 * chip) % 4 + core)
            out.append(jnp.concatenate([lax.dynamic_slice_in_dim(h[src], col0, width, axis=1) for h in halves], axis=0))
    return jnp.stack(keep), jnp.stack(give)


def _row(v):
    return v.reshape(1, -1)


def _local_step(x, target, mod, small, conv, big):
    lw, saved = [], []
    for l in range(DEPTH):
        w_in_p = _pad_in_proj(big["w_in_shards"][:, l])
        w_cat = jnp.concatenate([big["w_sc_out"][l], big["w_sb_out"][l], big["w_ssm_out"][l]], axis=0)
        hp_rows = jnp.zeros((8, LANES), F32).at[0, :SSM_HEADS].set(small["ssm_dt_bias"][l]).at[1, :SSM_HEADS].set(
            small["ssm_a_log"][l])
        hp_cols = jnp.concatenate([jnp.broadcast_to(small["ssm_dt_bias"][l][:, None], (SSM_HEADS, SSM_CHUNK)),
                                   jnp.broadcast_to(small["ssm_a_log"][l][:, None], (SSM_HEADS, SSM_CHUNK))], axis=0)
        lw.append(dict(
            w_in_p=w_in_p, w_cat=w_cat, w_o=big["w_o"][l], w_ffn_in=big["w_ffn_in"][l], w_ffn_out=big["w_ffn_out"][l],
            sc_w8=jnp.pad(conv["sc_conv_w"][l], ((0, 5), (0, 0))), ssm_w8=jnp.pad(conv["ssm_conv_w"][l], ((0, 4), (0, 0))),
            ssm_b=_row(small["ssm_conv_b"][l]), hp_rows=hp_rows, hp_cols=hp_cols,
            d_e=_row(jnp.repeat(small["ssm_d"][l], SSM_HEAD_DIM)), norm_w=_row(small["ssm_norm_w"][l]),
            g_pre_mix=_row(small["g_pre_mix"][l]), g_post_mix=_row(small["g_post_mix"][l]),
            g_pre_ffn=_row(small["g_pre_ffn"][l]), g_post_ffn=_row(small["g_post_ffn"][l]),
            shift1=mod[l, 0:1], scale1=mod[l, 1:2], gate1=mod[l, 2:3], shift2=mod[l, 3:4], scale2=mod[l, 4:5],
            gate2=mod[l, 5:6]))

    xl = x
    h = normmod_fwd(xl, lw[0]["g_pre_mix"], lw[0]["scale1"], lw[0]["shift1"], "normmod_fwd_0")
    dy = loss = None
    for l in range(DEPTH):
        w = lw[l]
        p = mm_nn([h], [w["w_in_p"]], BF16, f"in_proj_{l}")
        dt32 = mm_nn([h], [w["w_in_p"][:, OFF_DT:OFF_DT + LANES]], F32, f"in_proj_dt_{l}")
        ya, qkv, act = post_inproj(p, w["sc_w8"], w["ssm_w8"], w["ssm_b"], f"post_inproj_{l}")
        o, runs = sba_fwd(qkv, f"sba_fwd_{l}")
        dtt = dt32[:, :SSM_HEADS].T
        yc, ypre, states = ssd_fwd(act, p, dt32, dtt, w["hp_rows"], w["hp_cols"], w["d_e"], w["norm_w"], f"ssd_fwd_{l}")
        merged = branch_out_fwd(ya, o, yc, p, w["w_cat"], f"branch_fwd_{l}")
        mix = mm_nn([merged], [w["w_o"]], F32, f"out_proj_{l}")
        x1, h2 = resid_normmod_fwd(xl, mix, w["gate1"], w["g_post_mix"], w["g_pre_ffn"], w["scale2"], w["shift2"],
                                   f"resid_mix_{l}")
        gt, up, a = mm_swiglu_fwd(h2, w["w_ffn_in"], f"ffn_in_{l}")
        f = mm_nn([a], [w["w_ffn_out"]], F32, f"ffn_out_{l}")
        saved.append(dict(x=xl, h=h, p=p, ya=ya, qkv=qkv, act=act, o=o, runs=runs, dt32=dt32, dtt=dtt, yc=yc, ypre=ypre, states=states,
                          merged=merged, mix=mix, x1=x1, h2=h2, gt=gt, up=up, a=a, f=f))
        if l + 1 < DEPTH:
            nw = lw[l + 1]
            xl, h = resid_normmod_fwd(x1, f, w["gate2"], w["g_post_ffn"], nw["g_pre_mix"], nw["scale1"], nw["shift1"],
                                      f"resid_ffn_{l}")
        else:
            dy, loss = resid_loss(x1, f, w["gate2"], w["g_post_ffn"], target, "resid_loss")

    dmod = [None] * DEPTH
    gs = {k: [None] * DEPTH for k in _SMALL + _BIG}
    dxo = dy
    top, stl = lw[DEPTH - 1], saved[DEPTH - 1]
    df, dgate2, gs["g_post_ffn"][DEPTH - 1] = resid_bwd(dxo, stl["f"], top["gate2"], top["g_post_ffn"],
                                                        f"resid_ffn_bwd_{DEPTH - 1}")
    for l in reversed(range(DEPTH)):
        w, s = lw[l], saved[l]
        dgt, dup = mm_swiglu_bwd(df, w["w_ffn_out"], s["gt"], s["up"], f"ffn_out_bwd_{l}")
        gs["w_ffn_out"][l] = mm_tn(s["a"], df, f"dw_ffn_out_{l}")
        dh2 = mm_nt([dgt, dup], [w["w_ffn_in"], w["w_ffn_in"]], [0, FFN_HIDDEN], F32, f"ffn_in_bwd_{l}")
        gs["w_ffn_in"][l] = (mm_tn(s["h2"], dgt, f"dw_ffn_gate_{l}"), mm_tn(s["h2"], dup, f"dw_ffn_up_{l}"))
        dx1, dmix, dscale2, dshift2, gs["g_pre_ffn"][l], dgate1, gs["g_post_mix"][l] = normmod_resid_bwd(
            dh2, s["x1"], dxo, w["g_pre_ffn"], w["scale2"], s["mix"], w["gate1"], w["g_post_mix"], f"ffn_mix_bwd_{l}")
        dmerged = mm_nt([dmix], [w["w_o"]], [0], F32, f"out_proj_bwd_{l}")
        gs["w_o"][l] = mm_tn(s["merged"], dmix, f"dw_o_{l}")
        dp_gates, dya, dyb, dyc, dw_cat = branch_out_bwd(dmerged, s["ya"], s["o"], s["yc"], s["p"], w["w_cat"],
                                                         f"branch_bwd_{l}")
        gs["w_sc_out"][l], gs["w_sb_out"][l], gs["w_ssm_out"][l] = dw_cat[0:256], dw_cat[256:512], dw_cat[512:1024]
        dz, dact, ddt, dnw, dd_e, dhp = ssd_bwd(dyc, s["ypre"], s["act"], s["p"], s["dt32"], s["dtt"], s["states"], w["hp_rows"],
                                                w["hp_cols"], w["d_e"], w["norm_w"], f"ssd_bwd_{l}")
        gs["ssm_norm_w"][l] = dnw[0]
        gs["ssm_d"][l] = dd_e.reshape(SSM_HEADS, SSM_HEAD_DIM).sum(axis=1)
        gs["ssm_dt_bias"][l] = dhp[0, :SSM_HEADS]
        gs["ssm_a_log"][l] = dhp[1, :SSM_HEADS]
        dq, dk, dv = sba_bwd(s["qkv"], s["runs"], dyb, f"sba_bwd_{l}")
        dp, dscw, dssw, dssb = assemble_dp(dp_gates, dya, s["p"], w["sc_w8"], dq, dk, dv, dact, w["ssm_w8"], w["ssm_b"], ddt,
                                           dz, f"assemble_dp_{l}")
        gs["sc_conv_w"][l], gs["ssm_conv_w"][l], gs["ssm_conv_b"][l] = dscw[0:3], dssw[0:4], dssb[0]
        dh = mm_nt([dp], [w["w_in_p"]], [0], F32, f"in_proj_bwd_{l}")
        gs["w_in"][l] = mm_tn(s["h"], dp, f"dw_in_{l}")
        if l > 0:
            below, sb = lw[l - 1], saved[l - 1]
            dxo, df, dscale1, dshift1, gs["g_pre_mix"][l], dgate2_below, gs["g_post_ffn"][l - 1] = normmod_resid_bwd(
                dh, s["x"], dx1, w["g_pre_mix"], w["scale1"], sb["f"], below["gate2"], below["g_post_ffn"],
                f"mix_ffn_bwd_{l}")
        else:
            dxo, dscale1, dshift1, gs["g_pre_mix"][l] = normmod_bwd(dh, s["x"], dx1, w["g_pre_mix"], w["scale1"],
                                                                    f"normmod_mix_bwd_{l}")
            dgate2_below = None
        dmod[l] = jnp.concatenate([dshift1, dscale1, dgate1, dshift2, dscale2, dgate2], axis=0)
        dgate2 = dgate2_below
    for k in ("g_pre_mix", "g_post_mix", "g_pre_ffn", "g_post_ffn"):
        gs[k] = [g[0] for g in gs[k]]
    per_layer = ("w_in", "w_ffn_in")
    grads = {k: (v if k in per_layer else jnp.stack(v)) for k, v in gs.items() if k != "mod_b"}
    return loss[0, 0], dxo, jnp.stack(dmod), grads


def kernel(x, c, mod_w, mod_b, g_pre_mix, g_post_mix, g_pre_ffn, g_post_ffn, w_in, sc_conv_w, ssm_conv_w, ssm_conv_b, ssm_dt_bias, ssm_a_log, ssm_d, ssm_norm_w, w_sc_out, w_sb_out, w_ssm_out, w_o, w_ffn_in, w_ffn_out, loss_target, m_mod_w, m_mod_b, m_g_pre_mix, m_g_post_mix, m_g_pre_ffn, m_g_post_ffn, m_w_in, m_sc_conv_w, m_ssm_conv_w, m_ssm_conv_b, m_ssm_dt_bias, m_ssm_a_log, m_ssm_d, m_ssm_norm_w, m_w_sc_out, m_w_sb_out, m_w_ssm_out, m_w_o, m_w_ffn_in, m_w_ffn_out, v_mod_w, v_mod_b, v_g_pre_mix, v_g_post_mix, v_g_pre_ffn, v_g_post_ffn, v_w_in, v_sc_conv_w, v_ssm_conv_w, v_ssm_conv_b, v_ssm_dt_bias, v_ssm_a_log, v_ssm_d, v_ssm_norm_w, v_w_sc_out, v_w_sb_out, v_w_ssm_out, v_w_o, v_w_ffn_in, v_w_ffn_out):
    wts = dict(mod_w=mod_w, mod_b=mod_b, g_pre_mix=g_pre_mix, g_post_mix=g_post_mix, g_pre_ffn=g_pre_ffn,
               g_post_ffn=g_post_ffn, w_in=w_in, sc_conv_w=sc_conv_w, ssm_conv_w=ssm_conv_w, ssm_conv_b=ssm_conv_b,
               ssm_dt_bias=ssm_dt_bias, ssm_a_log=ssm_a_log, ssm_d=ssm_d, ssm_norm_w=ssm_norm_w, w_sc_out=w_sc_out,
               w_sb_out=w_sb_out, w_ssm_out=w_ssm_out, w_o=w_o, w_ffn_in=w_ffn_in, w_ffn_out=w_ffn_out)
    ms = dict(mod_w=m_mod_w, mod_b=m_mod_b, g_pre_mix=m_g_pre_mix, g_post_mix=m_g_post_mix, g_pre_ffn=m_g_pre_ffn,
              g_post_ffn=m_g_post_ffn, w_in=m_w_in, sc_conv_w=m_sc_conv_w, ssm_conv_w=m_ssm_conv_w,
              ssm_conv_b=m_ssm_conv_b, ssm_dt_bias=m_ssm_dt_bias, ssm_a_log=m_ssm_a_log, ssm_d=m_ssm_d,
              ssm_norm_w=m_ssm_norm_w, w_sc_out=m_w_sc_out, w_sb_out=m_w_sb_out, w_ssm_out=m_w_ssm_out, w_o=m_w_o,
              w_ffn_in=m_w_ffn_in, w_ffn_out=m_w_ffn_out)
    vs = dict(mod_w=v_mod_w, mod_b=v_mod_b, g_pre_mix=v_g_pre_mix, g_post_mix=v_g_post_mix, g_pre_ffn=v_g_pre_ffn,
              g_post_ffn=v_g_post_ffn, w_in=v_w_in, sc_conv_w=v_sc_conv_w, ssm_conv_w=v_ssm_conv_w,
              ssm_conv_b=v_ssm_conv_b, ssm_dt_bias=v_ssm_dt_bias, ssm_a_log=v_ssm_a_log, ssm_d=v_ssm_d,
              ssm_norm_w=v_ssm_norm_w, w_sc_out=v_w_sc_out, w_sb_out=v_w_sb_out, w_ssm_out=v_w_ssm_out, w_o=v_w_o,
              w_ffn_in=v_w_ffn_in, w_ffn_out=v_w_ffn_out)
    me = 4 * lax.axis_index("x") + 2 * lax.axis_index("y") + lax.axis_index("c")
    mod_cols = mod_w.shape[2]

    pack1, sizes1 = _pack_rows([c, sc_conv_w, ssm_conv_w], F32, 8)
    got1 = all_gather_multi([pack1], "gather_c_conv")[0].reshape(N_DEV, -1)
    c_all, sc_g, ssm_g = _unpack(got1, sizes1, [(D_MODEL,), sc_conv_w.shape, ssm_conv_w.shape])
    conv = dict(sc_conv_w=_gathered_to_full(sc_g, False), ssm_conv_w=_gathered_to_full(ssm_g, False))

    mod_b_shard = lax.dynamic_slice_in_dim(mod_b, me * mod_cols, mod_cols, axis=1).reshape(DEPTH, 1, mod_cols)
    mod_sh = mod_shard_fwd(c_all, mod_w, mod_b_shard, "mod_shard_fwd")
    pack2, sizes2 = _pack_rows([mod_sh], F32, 8)
    got2 = all_gather_multi([pack2], "gather_mod")[0].reshape(N_DEV, -1)
    mod_all = _unpack(got2, sizes2, [mod_sh.shape])[0]
    mod_mine = lax.dynamic_index_in_dim(mod_all, me, axis=2, keepdims=False)
    mod = mod_mine.transpose(1, 0, 2).reshape(DEPTH, 6, D_MODEL)

    mm_names = [k for k in _BIG if k not in _CONV]
    gathered = all_gather_multi([wts[k].astype(BF16) for k in mm_names], "gather_weights")
    big = {k: _gathered_to_full(g, k in _ROW_SHARDED) for k, g in zip(mm_names, gathered) if k != "w_in"}
    big["w_in_shards"] = gathered[mm_names.index("w_in")]

    small = {k: wts[k] for k in _SMALL}
    loss_part, dx, dmod, grads = _local_step(x[0], loss_target[0], mod, small, conv, big)
    loss = lax.psum(loss_part, ("x", "y", "c"))

    small_parts = [dmod.reshape(DEPTH, 6 * D_MODEL)] + [grads[k] for k in _SMALL[1:]]
    pack5, sizes5 = _pack_rows(small_parts, F32, 8)
    pack_conv, sizes_conv = _pack_rows([grads[k] for k in _CONV], F32, 8)
    got5, got_conv = all_gather_multi([pack5, pack_conv], "gather_small_grads")
    w5, _ = _pack_rows([wts[k] for k in _SMALL], F32, 8)
    m5, _ = _pack_rows([ms[k] for k in _SMALL], F32, 8)
    v5, _ = _pack_rows([vs[k] for k in _SMALL], F32, 8)
    res5 = adamw_flat(got5, w5, m5, v5, "adamw_small")
    small_out = [_unpack(r.reshape(-1), sizes5, [wts[k].shape for k in _SMALL]) for r in res5]

    conv_full = _unpack(got_conv.reshape(N_DEV, -1), sizes_conv, [grads[k].shape for k in _CONV])
    conv_mine = [lax.dynamic_slice_in_dim(g, me * wts[k].shape[2], wts[k].shape[2], axis=3)
                 for k, g in zip(_CONV, conv_full)]
    conv_slot_sizes = [math.prod(wts[k].shape) for k in _CONV]
    conv_slots = jnp.concatenate([g.reshape(N_DEV, -1) for g in conv_mine], axis=1)
    pad_c = -conv_slots.shape[1] % (8 * LANES)
    conv_slots = jnp.pad(conv_slots, ((0, 0), (0, pad_c))).reshape(N_DEV, -1, LANES)
    wc, _ = _pack_rows([wts[k] for k in _CONV], F32, 8)
    mc, _ = _pack_rows([ms[k] for k in _CONV], F32, 8)
    vc, _ = _pack_rows([vs[k] for k in _CONV], F32, 8)
    res_c = adamw_flat(conv_slots, wc, mc, vc, "adamw_conv")
    conv_out = [_unpack(r.reshape(-1), conv_slot_sizes, [wts[k].shape for k in _CONV]) for r in res_c]

    dmod_all = got5.reshape(N_DEV, -1)[:, :DEPTH * 6 * D_MODEL].reshape(N_DEV, DEPTH, 6 * D_MODEL)
    dmod_shard = lax.dynamic_slice_in_dim(dmod_all, me * mod_cols, mod_cols, axis=2).transpose(1, 0, 2)
    g_mod_w = mod_w_grad(c_all, dmod_shard, "mod_w_grad")
    rows2 = lambda a: a.reshape(a.shape[0] * a.shape[1], a.shape[2])
    res_mw = adamw_flat(rows2(g_mod_w)[None], rows2(mod_w), rows2(m_mod_w), rows2(v_mod_w), "adamw_mod_w")
    mod_w_out = [r.reshape(mod_w.shape) for r in res_mw]

    cidx = lax.axis_index("c")
    keeps, gives = [], []
    for k in mm_names:
        if k == "w_ffn_in":
            keep, give = _ffn_in_keep_give(grads[k], cidx)
        else:
            slots = (_in_proj_slots(grads[k]) if k == "w_in"
                     else _full_to_slots(grads[k].astype(BF16), k in _ROW_SHARDED))
            by_chip = slots.reshape(4, 2, *slots.shape[1:])
            keep = lax.dynamic_index_in_dim(by_chip, cidx, 1, keepdims=False)
            give = lax.dynamic_index_in_dim(by_chip, 1 - cidx, 1, keepdims=False)
        keeps.append(keep)
        gives.append(give)
    gots = swap_with_sibling(gives, "swap_grads")
    pairs = []
    for k, keep, got in zip(mm_names, keeps, gots):
        rows4 = (4 * keep.shape[1], keep.shape[2])
        pairs.append(add_pairs(keep.reshape(rows4), got.reshape(rows4), f"add_pairs_{k}").reshape(keep.shape))
    recvs = exchange_chips(pairs, "exchange_grads")
    big_out = {}
    for k, recv in zip(mm_names, recvs):
        res = adamw_flat(recv, rows2(wts[k]), rows2(ms[k]), rows2(vs[k]), f"adamw_{k}")
        big_out[k] = [r.reshape(wts[k].shape) for r in res]

    outs = []
    for kind in range(4):
        by_name = {"mod_w": mod_w_out[kind]}
        by_name.update(zip(_SMALL, small_out[kind]))
        by_name.update(zip(_CONV, conv_out[kind]))
        by_name.update({k: v[kind] for k, v in big_out.items()})
        outs.extend(by_name[k] for k in _WEIGHTS)
    return (loss, dx[None], *outs)
```

```python
import math

import jax
import jax.numpy as jnp
from jax import lax
from jax.experimental import pallas as pl
from jax.experimental.pallas import tpu as pltpu

F32 = jnp.float32
BF16 = jnp.bfloat16
HI = lax.Precision.HIGHEST

N_DEV = 8
D_MODEL = 1024
DEPTH = 2
SC_WIDTH = 256
SB_WIDTH = 256
SB_HEAD_DIM = 64
SSM_INNER = 512
SSM_HEADS = 8
SSM_HEAD_DIM = 64
SSM_GROUPS = 2
SSM_STATE = 64
SSM_CHUNK = 256
SSM_CONV_DIM = 768
FFN_HIDDEN = 2816
NORM_EPS = 1e-6
IN_PROJ = 5896
LANES = 128
VMEM_LIMIT = 56 * 1024 * 1024

OFF_GATES = 0
OFF_SC = 3072
OFF_QKV = 3840
OFF_XBC = 4608
OFF_DT = 5376
OFF_Z = 5632
IN_PAD = 6144

ADAM_LR = 0.001
ADAM_B1 = 0.9
ADAM_B2 = 0.999
ADAM_EPS = 1e-08
ADAM_WD = 0.01
ADAM_STEP = 10

MESH_ID = pl.DeviceIdType.MESH


def _cparams(sem=None):
    return pltpu.CompilerParams(dimension_semantics=sem, vmem_limit_bytes=VMEM_LIMIT)


def _my_pos():
    return lax.axis_index("x"), lax.axis_index("y"), lax.axis_index("c")


def all_gather_multi(blocks, name):
    n = len(blocks)

    def body(*refs):
        x_refs, o_refs = refs[:n], refs[n:2 * n]
        send_sems, recv_sems, local_sems = refs[2 * n:]
        x, y, c = _my_pos()
        me, sibling = (x, y, c), (x, y, 1 - c)
        chips = [(1 - x, y), (x, 1 - y), (1 - x, 1 - y)]

        def slot(a, px, py, pc):
            return o_refs[a].at[4 * px + 2 * py + pc]

        def copy(a, k, blk, to, src=None):
            return pltpu.make_async_remote_copy(
                src_ref=slot(a, *blk) if src is None else src, dst_ref=slot(a, *blk),
                send_sem=send_sems.at[7 * a + k], recv_sem=recv_sems.at[7 * a + k], device_id=to, device_id_type=MESH_ID)

        mine = [pltpu.make_async_copy(x_refs[a], slot(a, *me), local_sems.at[a]) for a in range(n)]
        for cp in mine:
            cp.start()
        first = [copy(a, 1 + j, me, (*chip, c), src=x_refs[a]) for j, chip in enumerate(chips) for a in range(n)]
        first += [copy(a, 0, me, sibling, src=x_refs[a]) for a in range(n)]
        for cp in first:
            cp.start()
        passed = []
        for j, chip in enumerate(chips):
            for a in range(n):
                copy(a, 1 + j, (*chip, c), me).wait_recv()
                fwd = copy(a, 4 + j, (*chip, c), sibling)
                fwd.start()
                passed.append(fwd)
        for a in range(n):
            copy(a, 0, sibling, me).wait_recv()
            for j, chip in enumerate(chips):
                copy(a, 4 + j, (*chip, 1 - c), me).wait_recv()
        for cp in first + passed:
            cp.wait_send()
        for cp in mine:
            cp.wait()

    any_spec = pl.BlockSpec(memory_space=pl.ANY)
    return pl.pallas_call(
        body, name=name,
        out_shape=[jax.ShapeDtypeStruct((N_DEV,) + b.shape, b.dtype) for b in blocks],
        in_specs=[any_spec] * n, out_specs=[any_spec] * n,
        scratch_shapes=[pltpu.SemaphoreType.DMA((7 * n,)), pltpu.SemaphoreType.DMA((7 * n,)), pltpu.SemaphoreType.DMA((n,))],
    )(*blocks)


def swap_with_sibling(gives, name):
    n = len(gives)

    def body(*refs):
        g_refs, r_refs = refs[:n], refs[n:2 * n]
        send_sems, recv_sems = refs[2 * n:]
        x, y, c = _my_pos()
        copies = [pltpu.make_async_remote_copy(
            src_ref=g_refs[a], dst_ref=r_refs[a], send_sem=send_sems.at[a], recv_sem=recv_sems.at[a],
            device_id=(x, y, 1 - c), device_id_type=MESH_ID) for a in range(n)]
        for cp in copies:
            cp.start()
        for cp in copies:
            cp.wait_recv()
        for cp in copies:
            cp.wait_send()

    any_spec = pl.BlockSpec(memory_space=pl.ANY)
    return pl.pallas_call(
        body, name=name, out_shape=[jax.ShapeDtypeStruct(g.shape, g.dtype) for g in gives],
        in_specs=[any_spec] * n, out_specs=[any_spec] * n,
        scratch_shapes=[pltpu.SemaphoreType.DMA((n,)), pltpu.SemaphoreType.DMA((n,))],
    )(*gives)


def exchange_chips(sends, name):
    n = len(sends)

    def body(*refs):
        s_refs, r_refs = refs[:n], refs[n:2 * n]
        send_sems, recv_sems, local_sems = refs[2 * n:]
        x, y, c = _my_pos()
        me = 2 * x + y
        mine = [pltpu.make_async_copy(s_refs[a].at[me], r_refs[a].at[me], local_sems.at[a]) for a in range(n)]
        for cp in mine:
            cp.start()
        copies = []
        for k in (2, 1, 3):
            px, py = x ^ (k >> 1), y ^ (k & 1)
            for a in range(n):
                cp = pltpu.make_async_remote_copy(
                    src_ref=s_refs[a].at[2 * px + py], dst_ref=r_refs[a].at[me],
                    send_sem=send_sems.at[3 * a + k - 1], recv_sem=recv_sems.at[3 * a + k - 1],
                    device_id=(px, py, c), device_id_type=MESH_ID)
                cp.start()
                copies.append(cp)
        for cp in copies:
            cp.wait_recv()
        for cp in copies:
            cp.wait_send()
        for cp in mine:
            cp.wait()

    any_spec = pl.BlockSpec(memory_space=pl.ANY)
    return pl.pallas_call(
        body, name=name, out_shape=[jax.ShapeDtypeStruct(s.shape, s.dtype) for s in sends],
        in_specs=[any_spec] * n, out_specs=[any_spec] * n,
        scratch_shapes=[pltpu.SemaphoreType.DMA((3 * n,)), pltpu.SemaphoreType.DMA((3 * n,)), pltpu.SemaphoreType.DMA((n,))],
    )(*sends)


def add_pairs(a, b, name, tr=512):
    rows, cols = a.shape
    tr = max(d for d in range(16, min(tr, rows) + 1, 16) if rows % d == 0)

    def body(a_ref, b_ref, o_ref):
        o_ref[...] = (a_ref[...].astype(F32) + b_ref[...].astype(F32)).astype(BF16)

    tile = pl.BlockSpec((tr, cols), lambda i: (i, 0))
    return pl.pallas_call(body, name=name, grid=(rows // tr,), in_specs=[tile, tile], out_specs=tile,
                          out_shape=jax.ShapeDtypeStruct((rows, cols), BF16), compiler_params=_cparams(("parallel",)))(a, b)


def _pack_rows(parts, dtype, row_multiple):
    flat = [p.astype(dtype).reshape(-1) for p in parts]
    sizes = [f.shape[0] for f in flat]
    total = sum(sizes)
    quantum = LANES * row_multiple
    padded = -(-total // quantum) * quantum
    if padded > total:
        flat.append(jnp.zeros((padded - total,), dtype))
    return jnp.concatenate(flat).reshape(padded // LANES, LANES), sizes


def _unpack(flat, sizes, shapes):
    out, off = [], 0
    lead = flat.shape[:-1]
    for n, shp in zip(sizes, shapes):
        out.append(flat[..., off:off + n].reshape(lead + tuple(shp)))
        off += n
    return out


def rows_call(name, body, n_rows, tr, ins, outs, scratch=(), aliases=None):
    n_tiles = n_rows // tr
    assert n_tiles * tr == n_rows
    in_specs, arrays = [], []
    for arr, kind in ins:
        arrays.append(arr)
        if kind == "row":
            in_specs.append(pl.BlockSpec((tr, arr.shape[1]), lambda i: (i, 0)))
        elif kind == "any":
            in_specs.append(pl.BlockSpec(memory_space=pl.ANY))
        elif kind == "full":
            in_specs.append(pl.BlockSpec(arr.shape, lambda i, nd=arr.ndim: (0,) * nd))
        elif kind[0] == "row":
            _, w, ci = kind
            in_specs.append(pl.BlockSpec((tr, w), lambda i, ci=ci: (i, ci)))
        elif kind[0] == "prev8":
            _, w, ci = kind
            hr = 8 * (4 // arr.dtype.itemsize)
            in_specs.append(pl.BlockSpec((hr, w), lambda i, ci=ci, hr=hr: (jnp.maximum(i * (tr // hr) - 1, 0), ci)))
        elif kind[0] == "next8":
            _, w, ci = kind
            hr = 8 * (4 // arr.dtype.itemsize)
            last = n_rows // hr - 1
            in_specs.append(pl.BlockSpec((hr, w), lambda i, ci=ci, last=last, hr=hr: (jnp.minimum((i + 1) * (tr // hr), last), ci)))
        else:
            raise ValueError(kind)
    out_specs, out_shapes = [], []
    for shape, dtype, kind in outs:
        out_shapes.append(jax.ShapeDtypeStruct(shape, dtype))
        if kind == "row":
            out_specs.append(pl.BlockSpec((tr, shape[1]), lambda i: (i, 0)))
        elif kind[0] == "row":
            _, w, ci = kind
            out_specs.append(pl.BlockSpec((tr, w), lambda i, ci=ci: (i, ci)))
        else:
            out_specs.append(pl.BlockSpec(shape, lambda i, nd=len(shape): (0,) * nd))
    has_acc = any(k == "acc" for _, _, k in outs)
    return pl.pallas_call(
        body, name=name, grid=(n_tiles,), in_specs=in_specs, out_specs=out_specs, out_shape=out_shapes,
        scratch_shapes=list(scratch), input_output_aliases=dict(aliases or {}),
        compiler_params=_cparams(("arbitrary",) if has_acc else ("parallel",)),
    )(*arrays)


def _prev8(ref):
    n = ref.shape[0]
    return ref[n - 8:n, :].astype(F32)


def _next8(ref):
    return ref[0:8, :].astype(F32)


def _acc(ref, val):
    @pl.when(pl.program_id(0) == 0)
    def _():
        ref[...] = jnp.zeros_like(ref)
    ref[...] += val


def _rstd(x):
    return lax.rsqrt(jnp.mean(x * x, axis=-1, keepdims=True) + NORM_EPS)


def _sigmoid(x):
    return 1.0 / (1.0 + jnp.exp(-x))


def _silu(x):
    return x * _sigmoid(x)


def _dsilu(x):
    s = _sigmoid(x)
    return s * (1.0 + x * (1.0 - s))


def _softplus(x):
    return jnp.maximum(x, 0.0) + jnp.log(1.0 + jnp.exp(-jnp.abs(x)))


def _log_sigmoid_neg(x):
    t = -x
    return jnp.minimum(t, 0.0) - jnp.log(1.0 + jnp.exp(jnp.minimum(x, t)))


def normmod_fwd(x, g, scale, shift, name):
    t, d = x.shape

    def body(x_ref, g_ref, sc_ref, sh_ref, h_ref):
        xv = x_ref[...]
        h = xv * _rstd(xv) * g_ref[...] * (1.0 + sc_ref[...]) + sh_ref[...]
        h_ref[...] = h.astype(BF16)

    return rows_call(name, body, t, 512, [(x, "row"), (g, "full"), (scale, "full"), (shift, "full")],
                     [((t, d), BF16, "row")])[0]


def resid_normmod_fwd(x, f, gate, g_post, g_pre, scale, shift, name):
    t, d = x.shape

    def body(x_ref, f_ref, gate_ref, gp_ref, g_ref, sc_ref, sh_ref, xo_ref, h_ref):
        fv = f_ref[...]
        xn = x_ref[...] + gate_ref[...] * (fv * _rstd(fv) * gp_ref[...])
        xo_ref[...] = xn
        h = xn * _rstd(xn) * g_ref[...] * (1.0 + sc_ref[...]) + sh_ref[...]
        h_ref[...] = h.astype(BF16)

    return rows_call(name, body, t, 512,
                     [(x, "row"), (f, "row"), (gate, "full"), (g_post, "full"), (g_pre, "full"), (scale, "full"),
                      (shift, "full")],
                     [((t, d), F32, "row"), ((t, d), BF16, "row")])


def resid_loss(x, f, gate, g_post, target, name):
    t, d = x.shape

    def body(x_ref, f_ref, gate_ref, gp_ref, tg_ref, dy_ref, loss_ref):
        fv = f_ref[...]
        yv = x_ref[...] + gate_ref[...] * (fv * _rstd(fv) * gp_ref[...])
        err = yv - tg_ref[...]
        dy_ref[...] = err * (1.0 / d)
        part = 0.5 * jnp.sum(jnp.mean(err * err, axis=-1, keepdims=True), axis=0, keepdims=True)
        _acc(loss_ref, jnp.broadcast_to(part, loss_ref.shape))

    return rows_call(name, body, t, 512,
                     [(x, "row"), (f, "row"), (gate, "full"), (g_post, "full"), (target, "row")],
                     [((t, d), F32, "row"), ((8, LANES), F32, "acc")])


def resid_bwd(dx, f, gate, g_post, name):
    t, d = dx.shape

    def body(dx_ref, f_ref, gate_ref, gp_ref, df_ref, dgate_ref, dg_ref):
        fv, dxv, gp = f_ref[...], dx_ref[...], gp_ref[...]
        r = _rstd(fv)
        fn = fv * r
        _acc(dgate_ref, jnp.sum(dxv * (fn * gp), axis=0, keepdims=True))
        dn = dxv * gate_ref[...]
        _acc(dg_ref, jnp.sum(dn * fn, axis=0, keepdims=True))
        u = dn * gp
        df = r * (u - fn * jnp.mean(fn * u, axis=-1, keepdims=True))
        df_ref[...] = df.astype(BF16)

    return rows_call(name, body, t, 512, [(dx, "row"), (f, "row"), (gate, "full"), (g_post, "full")],
                     [((t, d), BF16, "row"), ((1, d), F32, "acc"), ((1, d), F32, "acc")])


def normmod_bwd(dh, x, dx_in, g, scale, name):
    t, d = x.shape

    def body(dh_ref, x_ref, dxi_ref, g_ref, sc_ref, dx_ref, dsc_ref, dsh_ref, dg_ref):
        xv, dhv, gv = x_ref[...], dh_ref[...], g_ref[...]
        r = _rstd(xv)
        xn = xv * r
        _acc(dsc_ref, jnp.sum(dhv * (xn * gv), axis=0, keepdims=True))
        _acc(dsh_ref, jnp.sum(dhv, axis=0, keepdims=True))
        dn = dhv * (1.0 + sc_ref[...])
        _acc(dg_ref, jnp.sum(dn * xn, axis=0, keepdims=True))
        u = dn * gv
        dx_ref[...] = dxi_ref[...] + r * (u - xn * jnp.mean(xn * u, axis=-1, keepdims=True))

    return rows_call(name, body, t, 512, [(dh, "row"), (x, "row"), (dx_in, "row"), (g, "full"), (scale, "full")],
                     [((t, d), F32, "row"), ((1, d), F32, "acc"), ((1, d), F32, "acc"), ((1, d), F32, "acc")])


def normmod_resid_bwd(dh, x, dx_in, g, scale, f, gate, g_post, name):
    t, d = x.shape

    def body(dh_ref, x_ref, dxi_ref, g_ref, sc_ref, f_ref, gate_ref, gp_ref,
             dx_ref, df_ref, dsc_ref, dsh_ref, dg_ref, dgate_ref, dgp_ref):
        xv, dhv, gv = x_ref[...], dh_ref[...], g_ref[...]
        r = _rstd(xv)
        xn = xv * r
        _acc(dsc_ref, jnp.sum(dhv * (xn * gv), axis=0, keepdims=True))
        _acc(dsh_ref, jnp.sum(dhv, axis=0, keepdims=True))
        dn = dhv * (1.0 + sc_ref[...])
        _acc(dg_ref, jnp.sum(dn * xn, axis=0, keepdims=True))
        u = dn * gv
        dxv = dxi_ref[...] + r * (u - xn * jnp.mean(xn * u, axis=-1, keepdims=True))
        dx_ref[...] = dxv
        fv, gp = f_ref[...], gp_ref[...]
        rf = _rstd(fv)
        fn = fv * rf
        _acc(dgate_ref, jnp.sum(dxv * (fn * gp), axis=0, keepdims=True))
        dnf = dxv * gate_ref[...]
        _acc(dgp_ref, jnp.sum(dnf * fn, axis=0, keepdims=True))
        uf = dnf * gp
        df_ref[...] = (rf * (uf - fn * jnp.mean(fn * uf, axis=-1, keepdims=True))).astype(BF16)

    vec = ((1, d), F32, "acc")
    return rows_call(name, body, t, 512,
                     [(dh, "row"), (x, "row"), (dx_in, "row"), (g, "full"), (scale, "full"), (f, "row"), (gate, "full"),
                      (g_post, "full")],
                     [((t, d), F32, "row"), ((t, d), BF16, "row"), vec, vec, vec, vec, vec])


def _pick(n, prefs):
    for p in prefs:
        if n % p == 0:
            return p
    return n


def mm_nn(a_list, b_list, out_dtype, name, tm=1024, tn=None, tk=None):
    m, k = a_list[0].shape
    n = b_list[0].shape[1]
    tm = min(tm, m)
    tn = tn or _pick(n, (1024, 768, 512, 256, 128))
    tk = tk or _pick(k, (1024, 1408, 512, 256))
    nk = k // tk
    npair = len(a_list)

    if nk == 1 and npair == 1:
        def body1(a_ref, b_ref, o_ref):
            o_ref[...] = jnp.dot(a_ref[...], b_ref[...], preferred_element_type=F32).astype(o_ref.dtype)

        return pl.pallas_call(
            body1, name=name, grid=(m // tm, n // tn),
            in_specs=[pl.BlockSpec((tm, k), lambda i, j: (i, 0)), pl.BlockSpec((k, tn), lambda i, j: (0, j))],
            out_specs=pl.BlockSpec((tm, tn), lambda i, j: (i, j)),
            out_shape=jax.ShapeDtypeStruct((m, n), out_dtype),
            compiler_params=_cparams(("parallel", "parallel")),
        )(a_list[0], b_list[0])

    def body(*refs):
        a_refs, b_refs = refs[:npair], refs[npair:2 * npair]
        o_ref, acc = refs[2 * npair], refs[2 * npair + 1]
        kk = pl.program_id(2)

        @pl.when(kk == 0)
        def _():
            acc[...] = jnp.zeros_like(acc)

        s = acc[...]
        for a_ref, b_ref in zip(a_refs, b_refs):
            s = s + jnp.dot(a_ref[...], b_ref[...], preferred_element_type=F32)
        acc[...] = s

        @pl.when(kk == nk - 1)
        def _():
            o_ref[...] = acc[...].astype(o_ref.dtype)

    return pl.pallas_call(
        body, name=name, grid=(m // tm, n // tn, nk),
        in_specs=[pl.BlockSpec((tm, tk), lambda i, j, kk: (i, kk))] * npair
        + [pl.BlockSpec((tk, tn), lambda i, j, kk: (kk, j))] * npair,
        out_specs=pl.BlockSpec((tm, tn), lambda i, j, kk: (i, j)),
        out_shape=jax.ShapeDtypeStruct((m, n), out_dtype),
        scratch_shapes=[pltpu.VMEM((tm, tn), F32)],
        compiler_params=_cparams(("parallel", "parallel", "arbitrary")),
    )(*a_list, *b_list)


def mm_nt(a_list, b_list, b_koff, out_dtype, name, tm=1024):
    m, k = a_list[0].shape
    n = b_list[0].shape[0]
    tm = min(tm, m)
    tn = _pick(n, (1024, 512, 256))
    tk = _pick(k, (1024, 1408, 512, 256))
    nk = k // tk
    npair = len(a_list)
    koff = [o // tk for o in b_koff]
    nt_dims = (((1,), (1,)), ((), ()))

    def body(*refs):
        a_refs, b_refs = refs[:npair], refs[npair:2 * npair]
        o_ref, acc = refs[2 * npair], refs[2 * npair + 1]
        kk = pl.program_id(2)

        @pl.when(kk == 0)
        def _():
            acc[...] = jnp.zeros_like(acc)

        s = acc[...]
        for a_ref, b_ref in zip(a_refs, b_refs):
            s = s + lax.dot_general(a_ref[...], b_ref[...], nt_dims, preferred_element_type=F32)
        acc[...] = s

        @pl.when(kk == nk - 1)
        def _():
            o_ref[...] = acc[...].astype(o_ref.dtype)

    return pl.pallas_call(
        body, name=name, grid=(m // tm, n // tn, nk),
        in_specs=[pl.BlockSpec((tm, tk), lambda i, j, kk: (i, kk))] * npair
        + [pl.BlockSpec((tn, tk), lambda i, j, kk, o=o: (j, kk + o)) for o in koff],
        out_specs=pl.BlockSpec((tm, tn), lambda i, j, kk: (i, j)),
        out_shape=jax.ShapeDtypeStruct((m, n), out_dtype),
        scratch_shapes=[pltpu.VMEM((tm, tn), F32)],
        compiler_params=_cparams(("parallel", "parallel", "arbitrary")),
    )(*a_list, *b_list)


def mm_tn(a, b, name, tt=512):
    t, ka = a.shape
    n = b.shape[1]
    ta = _pick(ka, (1024, 1408, 512, 256))
    tn = _pick(n, (2048, 1024, 1408, 512, 256))
    nt = t // tt

    def body(a_ref, b_ref, o_ref, acc):
        s = pl.program_id(2)

        @pl.when(s == 0)
        def _():
            acc[...] = jnp.zeros_like(acc)

        acc[...] += lax.dot_general(a_ref[...], b_ref[...], (((0,), (0,)), ((), ())), preferred_element_type=F32)

        @pl.when(s == nt - 1)
        def _():
            o_ref[...] = acc[...].astype(BF16)

    return pl.pallas_call(
        body, name=name, grid=(ka // ta, n // tn, nt),
        in_specs=[pl.BlockSpec((tt, ta), lambda i, j, s: (s, i)), pl.BlockSpec((tt, tn), lambda i, j, s: (s, j))],
        out_specs=pl.BlockSpec((ta, tn), lambda i, j, s: (i, j)),
        out_shape=jax.ShapeDtypeStruct((ka, n), BF16),
        scratch_shapes=[pltpu.VMEM((ta, tn), F32)],
        compiler_params=_cparams(("parallel", "parallel", "arbitrary")),
    )(a, b)


def mm_swiglu_fwd(h, w_ffn_in, name, tm=512, tn=1408):
    m, k = h.shape
    nh = FFN_HIDDEN // tn

    def body(h_ref, wg_ref, wu_ref, gt_ref, up_ref, a_ref):
        hv = h_ref[...]
        gt = jnp.dot(hv, wg_ref[...], preferred_element_type=F32)
        up = jnp.dot(hv, wu_ref[...], preferred_element_type=F32)
        gt_ref[...] = gt.astype(BF16)
        up_ref[...] = up.astype(BF16)
        a_ref[...] = (_silu(gt) * up).astype(BF16)

    shp = jax.ShapeDtypeStruct((m, FFN_HIDDEN), BF16)
    ospec = pl.BlockSpec((tm, tn), lambda i, j: (i, j))
    return pl.pallas_call(
        body, name=name, grid=(m // tm, nh),
        in_specs=[pl.BlockSpec((tm, k), lambda i, j: (i, 0)), pl.BlockSpec((k, tn), lambda i, j: (0, j)),
                  pl.BlockSpec((k, tn), lambda i, j: (0, j + nh))],
        out_specs=[ospec, ospec, ospec], out_shape=[shp, shp, shp],
        compiler_params=_cparams(("parallel", "parallel")),
    )(h, w_ffn_in, w_ffn_in)


def mm_swiglu_bwd(df, w_out, gt, up, name, tm=256, sub=256):
    m, k = df.shape
    n_sub = FFN_HIDDEN // sub

    def body(df_ref, w_ref, gt_ref, up_ref, dgt_ref, dup_ref):
        dfv = df_ref[...]

        def chunk_dot(c):
            return lax.dot_general(dfv, w_ref[c * sub:(c + 1) * sub, :], (((1,), (1,)), ((), ())), preferred_element_type=F32)

        da_next = chunk_dot(0)
        for c in range(n_sub):
            da = da_next
            if c + 1 < n_sub:
                da_next = chunk_dot(c + 1)
            cols = slice(c * sub, (c + 1) * sub)
            gtv = gt_ref[:, cols].astype(F32)
            sg = _sigmoid(gtv)
            dgt_ref[:, cols] = (da * up_ref[:, cols].astype(F32) * (sg * (1.0 + gtv * (1.0 - sg)))).astype(BF16)
            dup_ref[:, cols] = (da * (gtv * sg)).astype(BF16)

    shp = jax.ShapeDtypeStruct((m, FFN_HIDDEN), BF16)
    tile = pl.BlockSpec((tm, FFN_HIDDEN), lambda i: (i, 0))
    return pl.pallas_call(
        body, name=name, grid=(m // tm,),
        in_specs=[pl.BlockSpec((tm, k), lambda i: (i, 0)), pl.BlockSpec((FFN_HIDDEN, k), lambda i: (0, 0)), tile, tile],
        out_specs=[tile, tile], out_shape=[shp, shp],
        compiler_params=_cparams(("parallel",)),
    )(df, w_out, gt, up)


def _shift_down(x, prev8, j):
    if j == 0:
        return x
    xr = pltpu.roll(x, j, 0)
    pr = pltpu.roll(prev8, j, 0)
    row = lax.broadcasted_iota(jnp.int32, (8, x.shape[1]), 0)
    head = jnp.where(row < j, pr, xr[:8])
    return head if x.shape[0] == 8 else jnp.concatenate([head, xr[8:]], axis=0)


def _shift_up(x, next8, j):
    if j == 0:
        return x
    n = x.shape[0]
    xr = pltpu.roll(x, n - j, 0)
    nr = pltpu.roll(next8, 8 - j, 0)
    row = lax.broadcasted_iota(jnp.int32, (8, x.shape[1]), 0)
    return jnp.concatenate([xr[:n - 8], jnp.where(row >= 8 - j, nr, xr[n - 8:])], axis=0)


def _conv_taps(x, prev8, w_ref, taps):
    out = None
    for k in range(taps):
        term = w_ref[k:k + 1, :] * _shift_down(x, prev8, taps - 1 - k)
        out = term if out is None else out + term
    return out


def post_inproj(p, sc_w, ssm_w, ssm_b, name, tr=512):
    t = p.shape[0]

    def body(sc_ref, scp_ref, qkv_ref, xbc_ref, xbcp_ref, scw_ref, sw_ref, sb_ref, ya_ref, qkvo_ref, act_ref):
        first = (pl.program_id(0) > 0).astype(F32)
        sc = sc_ref[...].astype(F32)
        scp = _prev8(scp_ref) * first
        u = sc[:, 256:512] * sc[:, 512:768]
        up = scp[:, 256:512] * scp[:, 512:768]
        ya_ref[...] = (sc[:, 0:256] * _conv_taps(u, up, scw_ref, 3)).astype(BF16)
        qkv = qkv_ref[...]
        qkvo_ref[:, 0:256] = (qkv[:, 0:256].astype(F32) * 0.125).astype(BF16)
        qkvo_ref[:, 256:768] = qkv[:, 256:768].astype(BF16)
        xc = _conv_taps(xbc_ref[...].astype(F32), _prev8(xbcp_ref) * first, sw_ref, 4) + sb_ref[...]
        act_ref[...] = _silu(xc)

    return rows_call(
        name, body, t, tr,
        [(p, ("row", 768, OFF_SC // 768)), (p, ("prev8", 768, OFF_SC // 768)), (p, ("row", 768, OFF_QKV // 768)),
         (p, ("row", 768, OFF_XBC // 768)), (p, ("prev8", 768, OFF_XBC // 768)),
         (sc_w, "full"), (ssm_w, "full"), (ssm_b, "full")],
        [((t, 256), BF16, "row"), ((t, 768), BF16, "row"), ((t, 768), F32, "row")])


def branch_out_fwd(ya, yb, yc, p, w_cat, name, tr=256):
    t = p.shape[0]

    def body(ya_ref, yb_ref, yc_ref, gl_ref, w_ref, o_ref):
        y_a = jnp.dot(ya_ref[...], w_ref[0:256, :], preferred_element_type=F32)
        y_b = jnp.dot(yb_ref[...].astype(BF16), w_ref[256:512, :], preferred_element_type=F32)
        y_c = jnp.dot(yc_ref[...], w_ref[512:1024, :], preferred_element_type=F32)
        m = (_sigmoid(gl_ref[:, 0:1024].astype(F32)) * y_a + _sigmoid(gl_ref[:, 1024:2048].astype(F32)) * y_b
             + _sigmoid(gl_ref[:, 2048:3072].astype(F32)) * y_c)
        o_ref[...] = m.astype(BF16)

    return rows_call(name, body, t, tr,
                     [(ya, "row"), (yb, "row"), (yc, "row"), (p, ("row", 3072, 0)), (w_cat, "full")],
                     [((t, D_MODEL), BF16, "row")])[0]


def branch_out_bwd(dm, ya, yb, yc, p, w_cat, name, tr=256):
    t = p.shape[0]
    tn_dims = (((0,), (0,)), ((), ()))

    def body(dm_ref, ya_ref, yb_ref, yc_ref, gl_ref, w_ref, dgl_ref, dya_ref, dyb_ref, dyc_ref, dw_ref):
        @pl.when(pl.program_id(0) == 0)
        def _():
            dw_ref[...] = jnp.zeros_like(dw_ref)

        dmv = dm_ref[...]
        ins = (ya_ref[...], yb_ref[...].astype(BF16), yc_ref[...])
        rows = ((0, 256), (256, 512), (512, 1024))
        outs = (dya_ref, dyb_ref, dyc_ref)
        for i in range(3):
            r0, r1 = rows[i]
            y = jnp.dot(ins[i], w_ref[r0:r1, :], preferred_element_type=F32)
            s = _sigmoid(gl_ref[:, 1024 * i:1024 * (i + 1)].astype(F32))
            dgl_ref[:, 1024 * i:1024 * (i + 1)] = (dmv * y * s * (1.0 - s)).astype(BF16)
            dy = (dmv * s).astype(BF16)
            outs[i][...] = lax.dot_general(dy, w_ref[r0:r1, :], _NT, preferred_element_type=F32)
            dw_ref[r0:r1, :] += lax.dot_general(ins[i], dy, tn_dims, preferred_element_type=F32)

    return rows_call(name, body, t, tr,
                     [(dm, "row"), (ya, "row"), (yb, "row"), (yc, "row"), (p, ("row", 3072, 0)), (w_cat, "full")],
                     [((t, IN_PAD), BF16, ("row", 3072, 0)), ((t, 256), F32, "row"), ((t, 256), F32, "row"),
                      ((t, 512), F32, "row"), ((D_MODEL, D_MODEL), F32, "acc")])


def assemble_dp(dp, dya, p, sc_w, dq, dk, dv, dact, ssm_w, ssm_b, ddt, dz, name, tr=256):
    t = p.shape[0]
    n_tiles = t // tr
    sci, xi = OFF_SC // 768, OFF_XBC // 768
    base = OFF_SC
    assert base == IN_PAD - base
    o_sc, o_qkv, o_xbc, o_dt, o_z, o_end = (c - base for c in (OFF_SC, OFF_QKV, OFF_XBC, OFF_DT, OFF_Z, IN_PAD))

    def body(dp_ref, dya_ref, dyan_ref, sc_ref, scp_ref, scn_ref, scw_ref, dq_ref, dk_ref, dv_ref,
             dact_ref, dactn_ref, xbc_ref, xbcp_ref, xbcn_ref, sw_ref, sb_ref, ddt_ref, dz_ref,
             o_ref, dscw_ref, dsw_ref, dsb_ref):
        i = pl.program_id(0)

        @pl.when(i == 0)
        def _():
            dscw_ref[...] = jnp.zeros_like(dscw_ref)
            dsw_ref[...] = jnp.zeros_like(dsw_ref)
            dsb_ref[...] = jnp.zeros_like(dsb_ref)

        first = (i > 0).astype(F32)
        last = (i < n_tiles - 1).astype(F32)
        del dp_ref
        sc = sc_ref[...].astype(F32)
        scp = _prev8(scp_ref) * first
        scn = _next8(scn_ref) * last
        u = sc[:, 256:512] * sc[:, 512:768]
        up = scp[:, 256:512] * scp[:, 512:768]
        dya_v = dya_ref[...]
        cv = _conv_taps(u, up, scw_ref, 3)
        o_ref[:, o_sc:o_sc + 256] = (dya_v * cv).astype(BF16)
        dcv = dya_v * sc[:, 0:256]
        dcvn = _next8(dyan_ref) * last * scn[:, 0:256]
        du = None
        for k in range(3):
            sh = 2 - k
            term = scw_ref[k:k + 1, :] * _shift_up(dcv, dcvn, sh)
            du = term if du is None else du + term
            dscw_ref[k:k + 1, :] += jnp.sum(dcv * _shift_down(u, up, sh), axis=0, keepdims=True)
        o_ref[:, o_sc + 256:o_sc + 512] = (du * sc[:, 512:768]).astype(BF16)
        o_ref[:, o_sc + 512:o_sc + 768] = (du * sc[:, 256:512]).astype(BF16)
        o_ref[:, o_qkv:o_qkv + 256] = (dq_ref[...] * 0.125).astype(BF16)
        o_ref[:, o_qkv + 256:o_qkv + 512] = dk_ref[...].astype(BF16)
        o_ref[:, o_qkv + 512:o_qkv + 768] = dv_ref[...].astype(BF16)
        xb = xbc_ref[...].astype(F32)
        xbp = _prev8(xbcp_ref) * first
        xbn = _next8(xbcn_ref)
        xc = _conv_taps(xb, xbp, sw_ref, 4) + sb_ref[...]
        xcn = _conv_taps(xbn, xb[tr - 8:, :], sw_ref, 4) + sb_ref[...]
        dxc = dact_ref[...] * _dsilu(xc)
        dxcn = _next8(dactn_ref) * _dsilu(xcn) * last
        dxb = None
        for k in range(4):
            sh = 3 - k
            term = sw_ref[k:k + 1, :] * _shift_up(dxc, dxcn, sh)
            dxb = term if dxb is None else dxb + term
            dsw_ref[k:k + 1, :] += jnp.sum(dxc * _shift_down(xb, xbp, sh), axis=0, keepdims=True)
        dsb_ref[...] += jnp.sum(dxc, axis=0, keepdims=True)
        o_ref[:, o_xbc:o_xbc + 768] = dxb.astype(BF16)
        o_ref[:, o_dt:o_dt + 128] = ddt_ref[...].astype(BF16)
        o_ref[:, o_dt + 128:o_z] = jnp.zeros((tr, o_z - o_dt - 128), BF16)
        o_ref[:, o_z:o_end] = dz_ref[...].astype(BF16)

    return rows_call(
        name, body, t, tr,
        [(dp, "any"), (dya, "row"), (dya, ("next8", 256, 0)),
         (p, ("row", 768, sci)), (p, ("prev8", 768, sci)), (p, ("next8", 768, sci)), (sc_w, "full"),
         (dq, "row"), (dk, "row"), (dv, "row"),
         (dact, "row"), (dact, ("next8", 768, 0)),
         (p, ("row", 768, xi)), (p, ("prev8", 768, xi)), (p, ("next8", 768, xi)), (ssm_w, "full"), (ssm_b, "full"),
         (ddt, "row"), (dz, "row")],
        [((t, IN_PAD), BF16, ("row", IN_PAD - base, 1)), ((8, 256), F32, "acc"), ((8, 768), F32, "acc"),
         ((1, 768), F32, "acc")], aliases={0: 0})


def adamw_flat(slots, w, m, v, name, tr=512):
    n_slots, rows, lanes = slots.shape
    tr = max(d for d in range(8, min(tr, rows) + 1, 8) if rows % d == 0) if rows % 8 == 0 else rows
    bc1 = 1.0 - ADAM_B1 ** ADAM_STEP
    bc2 = 1.0 - ADAM_B2 ** ADAM_STEP

    def body(s_ref, w_ref, m_ref, v_ref, g_ref, d_ref, mo_ref, vo_ref):
        g = s_ref[0].astype(F32)
        for k in range(1, n_slots):
            g = g + s_ref[k].astype(F32)
        mn = ADAM_B1 * m_ref[...] + (1.0 - ADAM_B1) * g
        vn = ADAM_B2 * v_ref[...] + (1.0 - ADAM_B2) * (g * g)
        m_hat = mn / bc1
        v_hat = vn / bc2
        g_ref[...] = g
        d_ref[...] = -ADAM_LR * (m_hat / (jnp.sqrt(v_hat) + ADAM_EPS) + ADAM_WD * w_ref[...])
        mo_ref[...] = mn
        vo_ref[...] = vn

    tile = pl.BlockSpec((tr, lanes), lambda i: (i, 0))
    shp = jax.ShapeDtypeStruct((rows, lanes), F32)
    return pl.pallas_call(
        body, name=name, grid=(rows // tr,),
        in_specs=[pl.BlockSpec((n_slots, tr, lanes), lambda i: (0, i, 0)), tile, tile, tile],
        out_specs=[tile] * 4, out_shape=[shp] * 4,
        compiler_params=_cparams(("parallel",)),
    )(slots, w, m, v)


def _split_dot(x, tri):
    hi = x.astype(BF16)
    lo = (x - hi.astype(F32)).astype(BF16)
    return jnp.dot(hi, tri, preferred_element_type=F32) + jnp.dot(lo, tri, preferred_element_type=F32)


_NT = (((1,), (1,)), ((), ()))
_TN = (((0,), (0,)), ((), ()))

SBA_EXP_ZERO = -104.0
SBA_SKIPPED = -1e30


def sba_fwd(qkv, name, bq=256, bk=256):
    t = qkv.shape[0]
    ratio = bq // bk
    assert bq == ratio * bk and t // bk <= LANES

    def body(q_ref, k_ref, v_ref, o_ref, runs_ref, acc_s, run_s):
        i = pl.program_id(1)
        lane = lax.broadcasted_iota(jnp.int32, (1, LANES), 1)
        lane_q = lax.broadcasted_iota(jnp.int32, (bq, LANES), 1)
        qi = lax.broadcasted_iota(jnp.int32, (bq, bk), 0) + i * bq
        kj = lax.broadcasted_iota(jnp.int32, (bq, bk), 1)
        later = (lax.broadcasted_iota(jnp.int32, (bk, bk), 0) > lax.broadcasted_iota(jnp.int32, (bk, bk), 1)).astype(BF16)
        qv = q_ref[...]
        qms = [jnp.where(hm, qv, jnp.zeros_like(qv)) for hm in (lane < 64, lane >= 64)]
        acc_s[...] = jnp.zeros_like(acc_s)
        run_s[...] = jnp.zeros_like(run_s)
        runs_ref[...] = jnp.full(runs_ref.shape, SBA_SKIPPED, F32)

        def tiles(specs):
            chains = [(ti, hh) for ti in range(len(specs)) for hh in range(2)]
            kb = [k_ref[pl.ds(pl.multiple_of(j * bk, bk), bk), :] for j, _ in specs]
            vb = [v_ref[pl.ds(pl.multiple_of(j * bk, bk), bk), :] for j, _ in specs]
            mask = [(kj + j * bk) < qi if masked else None for j, masked in specs]
            s = {c: lax.dot_general(qms[c[1]], kb[c[0]], _NT, preferred_element_type=F32) for c in chains}
            lk = {c: _log_sigmoid_neg(s[c]) for c in chains}
            lk = {c: lk[c] if mask[c[0]] is None else jnp.where(mask[c[0]], lk[c], 0.0) for c in chains}
            w = {c: jnp.dot(lk[c].astype(BF16), later, preferred_element_type=F32) for c in chains}
            run = {}
            for hh in range(2):
                carry = run_s[hh]
                for ti in range(len(specs)):
                    run[ti, hh] = carry
                    carry = carry + jnp.sum(lk[ti, hh], axis=1, keepdims=True)
                run_s[hh] = carry
            a = {c: jnp.exp(s[c] + lk[c] + w[c] + run[c]) for c in chains}
            a = {c: a[c] if mask[c[0]] is None else jnp.where(mask[c[0]], a[c], 0.0) for c in chains}
            for ti, hh in chains:
                acc_s[hh] += jnp.dot(a[ti, hh].astype(BF16), vb[ti], preferred_element_type=F32)
                runs_ref[hh] = jnp.where(lane_q == specs[ti][0], run[ti, hh], runs_ref[hh])

        for d in range(ratio):
            tiles([((i + 1) * ratio - 1 - d, True)])

        def live():
            return jnp.maximum(jnp.max(run_s[0]), jnp.max(run_s[1])) >= SBA_EXP_ZERO

        def cond(state):
            n, go = state
            return jnp.logical_and(n < i * ratio, go)

        def step(state):
            n, _ = state
            tiles([(i * ratio - 1 - n, False)])
            return n + 1, live()

        lax.while_loop(cond, step, (jnp.int32(0), live()))
        o_ref[...] = jnp.where(lane < 64, acc_s[0], acc_s[1])

    return pl.pallas_call(
        body, name=name, grid=(2, t // bq),
        in_specs=[pl.BlockSpec((bq, LANES), lambda p, i: (i, p)), pl.BlockSpec((t, LANES), lambda p, i: (0, 2 + p)),
                  pl.BlockSpec((t, LANES), lambda p, i: (0, 4 + p))],
        out_specs=[pl.BlockSpec((bq, LANES), lambda p, i: (i, p)), pl.BlockSpec((2, bq, LANES), lambda p, i: (p, i, 0))],
        out_shape=[jax.ShapeDtypeStruct((t, SB_WIDTH), F32), jax.ShapeDtypeStruct((4, t, LANES), F32)],
        scratch_shapes=[pltpu.VMEM((2, bq, LANES), F32), pltpu.VMEM((2, bq, 1), F32)],
        compiler_params=_cparams(("parallel", "parallel")),
    )(qkv, qkv, qkv)


def sba_bwd(qkv, runs, do, name, bq=256, bk=256):
    t = qkv.shape[0]
    ratio = bq // bk
    assert bq == ratio * bk
    nq = t // bq

    def body(q_ref, k_ref, v_ref, runs_ref, do_ref, dq_ref, dk_hbm, dv_hbm, dk_s, dv_s, sem, dq_s, rg_s):
        p = pl.program_id(0)
        i = pl.program_id(1)

        @pl.when(i == 0)
        def _():
            dk_s[...] = jnp.zeros_like(dk_s)
            dv_s[...] = jnp.zeros_like(dv_s)

        lane = lax.broadcasted_iota(jnp.int32, (1, LANES), 1)
        qi = lax.broadcasted_iota(jnp.int32, (bq, bk), 0) + i * bq
        kj = lax.broadcasted_iota(jnp.int32, (bq, bk), 1)
        r2 = lax.broadcasted_iota(jnp.int32, (bk, bk), 0)
        c2 = lax.broadcasted_iota(jnp.int32, (bk, bk), 1)
        later = (r2 > c2).astype(BF16)
        earlier = (r2 < c2).astype(BF16)
        qv = q_ref[...]
        dov = do_ref[...]
        heads = range(2)
        hms = (lane < 64, lane >= 64)
        qms = [jnp.where(hm, qv, jnp.zeros_like(qv)) for hm in hms]
        doms = [jnp.where(hm, dov, 0.0).astype(BF16) for hm in hms]
        runs = [runs_ref[hh] for hh in heads]
        dq_s[...] = jnp.zeros_like(dq_s)
        rg_s[...] = jnp.zeros_like(rg_s)

        def tiles(specs):
            nt = len(specs)
            chains = [(ti, hh) for ti in range(nt) for hh in heads]
            starts = [pl.multiple_of(j * bk, bk) for j, _ in specs]
            kb = [k_ref[pl.ds(st, bk), :] for st in starts]
            vb = [v_ref[pl.ds(st, bk), :] for st in starts]
            mask = [(kj + j * bk) < qi if masked else None for j, masked in specs]
            s = {c: lax.dot_general(qms[c[1]], kb[c[0]], _NT, preferred_element_type=F32) for c in chains}
            da = {c: lax.dot_general(doms[c[1]], vb[c[0]], _NT, preferred_element_type=F32) for c in chains}
            lk_raw = {c: _log_sigmoid_neg(s[c]) for c in chains}
            lk = {c: lk_raw[c] if mask[c[0]] is None else jnp.where(mask[c[0]], lk_raw[c], 0.0) for c in chains}
            w = {c: jnp.dot(lk[c].astype(BF16), later, preferred_element_type=F32) for c in chains}
            run = {c: jnp.sum(jnp.where(lane == specs[c[0]][0], runs[c[1]], 0.0), axis=1, keepdims=True) for c in chains}
            a = {c: jnp.exp(s[c] + lk[c] + w[c] + run[c]) for c in chains}
            a = {c: a[c] if mask[c[0]] is None else jnp.where(mask[c[0]], a[c], 0.0) for c in chains}
            g = {c: a[c] * da[c] for c in chains}
            rg = {}
            for hh in heads:
                carry = rg_s[hh]
                for ti in range(nt):
                    rg[ti, hh] = carry
                    carry = carry + jnp.sum(g[ti, hh], axis=1, keepdims=True)
                rg_s[hh] = carry
            cpre = {c: rg[c] + _split_dot(g[c], earlier) for c in chains}
            dz = {c: g[c] - jnp.exp(s[c] + lk_raw[c]) * (g[c] + cpre[c]) for c in chains}
            dz = {c: (dz[c] if mask[c[0]] is None else jnp.where(mask[c[0]], dz[c], 0.0)).astype(BF16) for c in chains}
            for ti, hh in chains:
                dq_s[hh] += jnp.dot(dz[ti, hh], kb[ti], preferred_element_type=F32)
            for ti in range(nt):
                dk = lax.dot_general(dz[ti, 0], qms[0], _TN, preferred_element_type=F32)
                dk_s[pl.ds(starts[ti], bk), :] += dk + lax.dot_general(dz[ti, 1], qms[1], _TN, preferred_element_type=F32)
                dv = lax.dot_general(a[ti, 0].astype(BF16), doms[0], _TN, preferred_element_type=F32)
                dv_s[pl.ds(starts[ti], bk), :] += dv + lax.dot_general(a[ti, 1].astype(BF16), doms[1], _TN,
                                                                       preferred_element_type=F32)

        live = jnp.maximum(jnp.max(runs[0], axis=0, keepdims=True), jnp.max(runs[1], axis=0, keepdims=True)) >= SBA_EXP_ZERO
        first = jnp.minimum(jnp.min(jnp.where(live, lane, LANES)), i * ratio)

        def step(j, carry):
            tiles([(j, False)])
            return carry

        if ratio == 1:
            lax.fori_loop(first, i - 1, step, 0)

            @pl.when(i == 0)
            def _():
                tiles([(i, True)])

            @pl.when(i > 0)
            def _():
                tiles([(i - 1, False), (i, True)])
        else:
            lax.fori_loop(first, i * ratio, step, 0)
            for d in range(ratio):
                tiles([(i * ratio + d, True)])

        dq_ref[...] = jnp.where(lane < 64, dq_s[0], dq_s[1])

        @pl.when(i == nq - 1)
        def _():
            col = pl.multiple_of(p * LANES, LANES)
            ck = pltpu.make_async_copy(dk_s, dk_hbm.at[:, pl.ds(col, LANES)], sem.at[0])
            cv = pltpu.make_async_copy(dv_s, dv_hbm.at[:, pl.ds(col, LANES)], sem.at[1])
            ck.start()
            cv.start()
            ck.wait()
            cv.wait()

    shp = jax.ShapeDtypeStruct((t, SB_WIDTH), F32)
    tile = pl.BlockSpec((bq, LANES), lambda p, i: (i, p))
    return pl.pallas_call(
        body, name=name, grid=(2, nq),
        in_specs=[tile, pl.BlockSpec((t, LANES), lambda p, i: (0, 2 + p)), pl.BlockSpec((t, LANES), lambda p, i: (0, 4 + p)),
                  pl.BlockSpec((2, bq, LANES), lambda p, i: (p, i, 0)), tile],
        out_specs=[tile, pl.BlockSpec(memory_space=pl.ANY), pl.BlockSpec(memory_space=pl.ANY)],
        out_shape=[shp, shp, shp],
        scratch_shapes=[pltpu.VMEM((t, LANES), F32), pltpu.VMEM((t, LANES), F32), pltpu.SemaphoreType.DMA((2,)),
                        pltpu.VMEM((2, bq, LANES), F32), pltpu.VMEM((2, bq, 1), F32)],
        compiler_params=_cparams(("arbitrary", "arbitrary")),
    )(qkv, qkv, qkv, runs, do)


def _ssd_consts():
    ln = SSM_CHUNK
    ri = lax.broadcasted_iota(jnp.int32, (ln, ln), 0)
    ci = lax.broadcasted_iota(jnp.int32, (ln, ln), 1)
    eh = lax.broadcasted_iota(jnp.int32, (LANES, SSM_INNER), 0)
    el = lax.broadcasted_iota(jnp.int32, (LANES, SSM_INNER), 1)
    expand = (jnp.right_shift(el, 6) == eh).astype(BF16)
    th = lax.broadcasted_iota(jnp.int32, (SSM_INNER, LANES), 1)
    tl = lax.broadcasted_iota(jnp.int32, (SSM_INNER, LANES), 0)
    reduce = (jnp.right_shift(tl, 6) == th).astype(BF16)
    return ri, ci, expand, reduce


def _dot_f32(a, b):
    return jnp.dot(a, b, precision=HI, preferred_element_type=F32)


def _split3(x):
    hi = x.astype(BF16)
    r1 = x - hi.astype(F32)
    mid = r1.astype(BF16)
    lo = (r1 - mid.astype(F32)).astype(BF16)
    return hi, mid, lo


def _dot_hi(a, b):
    if a.dtype == BF16:
        return sum(jnp.dot(a, t, preferred_element_type=F32) for t in _split3(b))
    return sum(jnp.dot(t, b, preferred_element_type=F32) for t in _split3(a))


def _ssd_prelude(xbc_ref, dt_ref, dtt_ref, hpr_ref, hpc_ref, ri, ci, expand):
    ln = SSM_CHUNK
    xs = xbc_ref[:, 0:512]
    bm = xbc_ref[:, 512:640]
    cm = xbc_ref[:, 640:768]
    dtb_r = hpr_ref[0:1, :]
    aneg_r = -jnp.exp(hpr_ref[1:2, :])
    pre = dt_ref[...] + dtb_r
    dt = _softplus(pre)
    a = dt * aneg_r
    dtt = _softplus(dtt_ref[...] + hpc_ref[0:8, :])
    att = dtt * (-jnp.exp(hpc_ref[8:16, :]))
    tril = (ri >= ci).astype(BF16)
    triu = (ri <= ci).astype(BF16)
    acs = _dot_hi(tril, a)
    acst = _dot_hi(att, triu)
    acs_e = _dot_hi(acs, expand)
    dt_e = _dot_hi(dt, expand)
    last_e = acs_e[ln - 1:ln, :]
    e_e = jnp.exp(acs_e)
    w_e = jnp.exp(last_e - acs_e)
    dec_e = jnp.exp(last_e)
    xdt = xs * dt_e
    return dict(xs=xs, bm=bm, cm=cm, pre=pre, dt=dt, aneg_r=aneg_r, acs=acs, acst=acst, dt_e=dt_e, e_e=e_e,
                w_e=w_e, dec_e=dec_e, xdt=xdt, triu=triu)


def ssd_fwd(act, p, dt32, dtt, hp_rows, hp_cols, d_e, norm_w, name):
    t = act.shape[0]
    ln = SSM_CHUNK
    nc = t // ln

    def body(xbc_ref, dt_ref, z_ref, dtt_ref, hpr_ref, hpc_ref, d_ref, nw_ref, yc_ref, y_ref, sto_ref, st):
        @pl.when(pl.program_id(0) == 0)
        def _():
            st[...] = jnp.zeros_like(st)

        ri, ci, expand, _ = _ssd_consts()
        q = _ssd_prelude(xbc_ref, dt_ref, dtt_ref, hpr_ref, hpc_ref, ri, ci, expand)
        lane = lax.broadcasted_iota(jnp.int32, (1, LANES), 1)
        rown = lax.broadcasted_iota(jnp.int32, (LANES, 1), 0)
        low = lane < 64
        mask = ri >= ci
        xdt_b = q["xdt"].astype(BF16)
        xw_b = (q["xdt"] * q["w_e"]).astype(BF16)
        bt = q["bm"].T.astype(BF16)
        cb_ = q["cm"].astype(BF16)
        y_pairs = []
        for g in range(2):
            gm = low if g == 0 else jnp.logical_not(low)
            rm = (rown < 64) if g == 0 else (rown >= 64)
            cg = jnp.where(gm, cb_, jnp.zeros_like(cb_))
            cb = jnp.dot(cg, bt, preferred_element_type=F32)
            for pp in range(2):
                pi = 2 * g + pp
                sl = slice(LANES * pi, LANES * (pi + 1))
                xp = xdt_b[:, sl]
                yd = []
                for hh in range(2):
                    h = 2 * pi + hh
                    diff = q["acs"][:, h:h + 1] - q["acst"][h:h + 1, :]
                    lam = jnp.exp(jnp.where(mask, diff, -jnp.inf))
                    yd.append(jnp.dot((cb * lam).astype(BF16), xp, preferred_element_type=F32))
                sp = st[pi]
                sto_ref[0, pi] = sp
                yoff = jnp.dot(cg, sp.astype(BF16), preferred_element_type=F32) * q["e_e"][:, sl]
                upd = jnp.dot(bt, xw_b[:, sl], preferred_element_type=F32)
                st[pi] = q["dec_e"][:, sl] * sp + jnp.where(rm, upd, 0.0)
                y_pairs.append(jnp.where(low, yd[0], yd[1]) + yoff)
        y = jnp.concatenate(y_pairs, axis=1) + q["xs"] * d_ref[...]
        y_ref[...] = y
        yg = y * _silu(z_ref[...].astype(F32))
        for g in range(2):
            sl = slice(256 * g, 256 * (g + 1))
            seg = yg[:, sl]
            yc_ref[:, sl] = (seg * _rstd(seg) * nw_ref[:, sl]).astype(BF16)

    return pl.pallas_call(
        body, name=name, grid=(nc,),
        in_specs=[pl.BlockSpec((ln, 768), lambda c: (c, 0)), pl.BlockSpec((ln, LANES), lambda c: (c, 0)),
                  pl.BlockSpec((ln, 512), lambda c: (c, OFF_Z // 512)), pl.BlockSpec((8, ln), lambda c: (0, c)),
                  pl.BlockSpec((8, LANES), lambda c: (0, 0)), pl.BlockSpec((16, ln), lambda c: (0, 0)),
                  pl.BlockSpec((1, 512), lambda c: (0, 0)), pl.BlockSpec((1, 512), lambda c: (0, 0))],
        out_specs=[pl.BlockSpec((ln, 512), lambda c: (c, 0)), pl.BlockSpec((ln, 512), lambda c: (c, 0)),
                   pl.BlockSpec((1, 4, LANES, LANES), lambda c: (c, 0, 0, 0))],
        out_shape=[jax.ShapeDtypeStruct((t, 512), BF16), jax.ShapeDtypeStruct((t, 512), F32),
                   jax.ShapeDtypeStruct((nc, 4, LANES, LANES), F32)],
        scratch_shapes=[pltpu.VMEM((4, LANES, LANES), F32)],
        compiler_params=_cparams(("arbitrary",)),
    )(act, dt32, p, dtt, hp_rows, hp_cols, d_e, norm_w)


def ssd_bwd(dyc, y, act, p, dt32, dtt, states, hp_rows, hp_cols, d_e, norm_w, name):
    t = act.shape[0]
    ln = SSM_CHUNK
    nc = t // ln

    def body(dyc_ref, y_ref, xbc_ref, dt_ref, z_ref, dtt_ref, st_ref, hpr_ref, hpc_ref, d_ref, nw_ref,
             dz_ref, dact_ref, ddt_ref, dnw_ref, dd_ref, dhp_ref, ds):
        @pl.when(pl.program_id(0) == 0)
        def _():
            ds[...] = jnp.zeros_like(ds)
            dnw_ref[...] = jnp.zeros_like(dnw_ref)
            dd_ref[...] = jnp.zeros_like(dd_ref)
            dhp_ref[...] = jnp.zeros_like(dhp_ref)

        ri, ci, expand, reduce = _ssd_consts()
        q = _ssd_prelude(xbc_ref, dt_ref, dtt_ref, hpr_ref, hpc_ref, ri, ci, expand)
        lane = lax.broadcasted_iota(jnp.int32, (1, LANES), 1)
        rown = lax.broadcasted_iota(jnp.int32, (LANES, 1), 0)
        low = lane < 64
        mask = ri >= ci
        mask_t = ci >= ri
        xs, xdt, acs, acst = q["xs"], q["xdt"], q["acs"], q["acst"]
        yv, zv, nw = y_ref[...], z_ref[...].astype(F32), nw_ref[...]
        sg = _sigmoid(zv)
        zz = zv * sg
        yg = yv * zz
        dycv = dyc_ref[...]
        u = dycv * nw
        dyg_parts, dnw_parts = [], []
        for g in range(2):
            sl = slice(256 * g, 256 * (g + 1))
            seg = yg[:, sl]
            rr = _rstd(seg)
            nrm = seg * rr
            dyg_parts.append(rr * (u[:, sl] - nrm * jnp.mean(nrm * u[:, sl], axis=-1, keepdims=True)))
            dnw_parts.append(jnp.sum(dycv[:, sl] * nrm, axis=0, keepdims=True))
        dyg = jnp.concatenate(dyg_parts, axis=1)
        dnw_ref[...] += jnp.concatenate(dnw_parts, axis=1)
        dy = dyg * zz
        dz_ref[...] = dyg * yv * (sg * (1.0 + zv * (1.0 - sg)))
        dd_ref[...] += jnp.sum(dy * xs, axis=0, keepdims=True)
        dxs = dy * d_ref[...]
        dy_b = dy.astype(BF16)
        xdt_b = xdt.astype(BF16)
        xw_b = (xdt * q["w_e"]).astype(BF16)
        bt = q["bm"].T.astype(BF16)
        ct = q["cm"].T.astype(BF16)
        cb_ = q["cm"].astype(BF16)
        bb_ = q["bm"].astype(BF16)
        dacs = jnp.zeros((ln, LANES), F32)
        dc = jnp.zeros((ln, LANES), F32)
        db = jnp.zeros((ln, LANES), F32)
        dxdt_pairs, yoffdy_pairs, dwe_pairs, ddec_pairs = [], [], [], []
        for g in range(2):
            gm = low if g == 0 else jnp.logical_not(low)
            rm = (rown < 64) if g == 0 else (rown >= 64)
            cg = jnp.where(gm, cb_, jnp.zeros_like(cb_))
            bg = jnp.where(gm, bb_, jnp.zeros_like(bb_))
            cb = jnp.dot(cg, bt, preferred_element_type=F32)
            cbt = jnp.dot(bg, ct, preferred_element_type=F32)
            dcb = jnp.zeros((ln, ln), F32)
            dcbt = jnp.zeros((ln, ln), F32)
            for pp in range(2):
                pi = 2 * g + pp
                sl = slice(LANES * pi, LANES * (pi + 1))
                xp = xdt_b[:, sl]
                dyp = dy_b[:, sl]
                xpt = xdt[:, sl].T.astype(BF16)
                dypt = dy[:, sl].T.astype(BF16)
                dxdt_p = jnp.zeros((ln, LANES), F32)
                for hh in range(2):
                    h = 2 * pi + hh
                    hm = low if hh == 0 else jnp.logical_not(low)
                    col = acs[:, h:h + 1]
                    row = acst[h:h + 1, :]
                    lam = jnp.exp(jnp.where(mask, col - row, -jnp.inf))
                    lam_t = jnp.exp(jnp.where(mask_t, row - col, -jnp.inf))
                    m = cb * lam
                    m_t = cbt * lam_t
                    dyh = jnp.where(hm, dyp, jnp.zeros_like(dyp))
                    xh = jnp.where(hm, xp, jnp.zeros_like(xp))
                    dm = jnp.dot(dyh, xpt, preferred_element_type=F32)
                    dm_t = jnp.dot(xh, dypt, preferred_element_type=F32)
                    dcb = dcb + dm * lam
                    dcbt = dcbt + dm_t * lam_t
                    rs = jnp.sum(dm * m, axis=1, keepdims=True) - jnp.sum(dm_t * m_t, axis=1, keepdims=True)
                    dacs = dacs + jnp.where(lane == h, rs, 0.0)
                    dxdt_p = dxdt_p + jnp.dot(m_t.astype(BF16), dyh, preferred_element_type=F32)
                sp = st_ref[0, pi]
                sp_b = sp.astype(BF16)
                dsn = ds[pi]
                dsn_b = dsn.astype(BF16)
                e_p, w_p, dec_p = q["e_e"][:, sl], q["w_e"][:, sl], q["dec_e"][:, sl]
                yoff = jnp.dot(cg, sp_b, preferred_element_type=F32) * e_p
                dyo = dy[:, sl] * e_p
                dyo_b = dyo.astype(BF16)
                dc = dc + lax.dot_general(dyo_b, sp_b, _NT, preferred_element_type=F32)
                ds_prev = dec_p * dsn + jnp.where(rm, jnp.dot(ct, dyo_b, preferred_element_type=F32), 0.0)
                yoffdy_pairs.append(dy[:, sl] * yoff)
                dxw = jnp.dot(bg, dsn_b, preferred_element_type=F32)
                db = db + lax.dot_general(xw_b[:, sl], dsn_b, _NT, preferred_element_type=F32)
                dxdt_p = dxdt_p + dxw * w_p
                dwe_pairs.append(dxw * xdt[:, sl])
                ddec_pairs.append(jnp.sum(dsn * sp, axis=0, keepdims=True))
                ds[pi] = ds_prev
                dxdt_pairs.append(dxdt_p)
            dc = dc + jnp.dot(dcb.astype(BF16), bg, preferred_element_type=F32)
            db = db + jnp.dot(dcbt.astype(BF16), cg, preferred_element_type=F32)
        dxdt = jnp.concatenate(dxdt_pairs, axis=1)
        yoffdy = jnp.concatenate(yoffdy_pairs, axis=1)
        dwe = jnp.concatenate(dwe_pairs, axis=1)
        ddec_e = jnp.broadcast_to(jnp.concatenate(ddec_pairs, axis=1), (8, SSM_INNER))
        last = acs[ln - 1:ln, :]
        w_col = jnp.exp(last - acs)
        dw_col = _dot_hi(dwe, reduce) * w_col
        dacs = dacs + _dot_hi(yoffdy, reduce) - dw_col
        dlast = jnp.sum(dw_col, axis=0, keepdims=True) + jnp.exp(last) * _dot_hi(ddec_e, reduce)[0:1, :]
        rowi = lax.broadcasted_iota(jnp.int32, (ln, 1), 0)
        dacs = dacs + jnp.where(rowi == ln - 1, dlast, 0.0)
        da = _dot_hi(q["triu"], dacs)
        ddt = da * q["aneg_r"] + _dot_hi(dxdt * xs, reduce)
        ddt_raw = jnp.where(lane < SSM_HEADS, ddt * _sigmoid(q["pre"]), 0.0)
        ddt_ref[...] = ddt_raw
        dhp_ref[0:1, :] += jnp.sum(ddt_raw, axis=0, keepdims=True)
        dhp_ref[1:2, :] += jnp.where(lane < SSM_HEADS, jnp.sum(da * q["dt"], axis=0, keepdims=True) * q["aneg_r"], 0.0)
        dact_ref[:, 0:512] = dxs + dxdt * q["dt_e"]
        dact_ref[:, 512:640] = db
        dact_ref[:, 640:768] = dc

    rev = lambda c: nc - 1 - c
    return pl.pallas_call(
        body, name=name, grid=(nc,),
        in_specs=[pl.BlockSpec((ln, 512), lambda c: (rev(c), 0)), pl.BlockSpec((ln, 512), lambda c: (rev(c), 0)),
                  pl.BlockSpec((ln, 768), lambda c: (rev(c), 0)),
                  pl.BlockSpec((ln, LANES), lambda c: (rev(c), 0)),
                  pl.BlockSpec((ln, 512), lambda c: (rev(c), OFF_Z // 512)), pl.BlockSpec((8, ln), lambda c: (0, rev(c))),
                  pl.BlockSpec((1, 4, LANES, LANES), lambda c: (rev(c), 0, 0, 0)),
                  pl.BlockSpec((8, LANES), lambda c: (0, 0)), pl.BlockSpec((16, ln), lambda c: (0, 0)),
                  pl.BlockSpec((1, 512), lambda c: (0, 0)), pl.BlockSpec((1, 512), lambda c: (0, 0))],
        out_specs=[pl.BlockSpec((ln, 512), lambda c: (rev(c), 0)), pl.BlockSpec((ln, 768), lambda c: (rev(c), 0)),
                   pl.BlockSpec((ln, LANES), lambda c: (rev(c), 0)), pl.BlockSpec((1, 512), lambda c: (0, 0)),
                   pl.BlockSpec((1, 512), lambda c: (0, 0)), pl.BlockSpec((8, LANES), lambda c: (0, 0))],
        out_shape=[jax.ShapeDtypeStruct((t, 512), F32), jax.ShapeDtypeStruct((t, 768), F32),
                   jax.ShapeDtypeStruct((t, LANES), F32), jax.ShapeDtypeStruct((1, 512), F32),
                   jax.ShapeDtypeStruct((1, 512), F32), jax.ShapeDtypeStruct((8, LANES), F32)],
        scratch_shapes=[pltpu.VMEM((4, LANES, LANES), F32)],
        compiler_params=_cparams(("arbitrary",)),
    )(dyc, y, act, dt32, p, dtt, states, hp_rows, hp_cols, d_e, norm_w)


def mod_shard_fwd(c_all, mod_w, mod_b_shard, name):
    def body(c_ref, w_ref, b_ref, o_ref):
        sc = _silu(c_ref[...])
        for l in range(DEPTH):
            o_ref[l] = _dot_f32(sc, w_ref[l]) + b_ref[l]

    return pl.pallas_call(body, name=name, out_shape=jax.ShapeDtypeStruct((DEPTH, N_DEV, mod_w.shape[2]), F32),
                          compiler_params=_cparams())(c_all, mod_w, mod_b_shard)


def mod_w_grad(c_all, dmod_shard, name):
    def body(c_ref, d_ref, o_ref):
        sc = _silu(c_ref[...])
        for l in range(DEPTH):
            o_ref[l] = lax.dot_general(sc, d_ref[l], _TN, precision=HI, preferred_element_type=F32)

    return pl.pallas_call(body, name=name, out_shape=jax.ShapeDtypeStruct((DEPTH, D_MODEL, dmod_shard.shape[2]), F32),
                          compiler_params=_cparams())(c_all, dmod_shard)


_BIG = ("w_in", "sc_conv_w", "ssm_conv_w", "w_sc_out", "w_sb_out", "w_ssm_out", "w_o", "w_ffn_in", "w_ffn_out")
_ROW_SHARDED = ("w_o", "w_ffn_out")
_CONV = ("sc_conv_w", "ssm_conv_w")
_SMALL = ("mod_b", "g_pre_mix", "g_post_mix", "g_pre_ffn", "g_post_ffn", "ssm_conv_b", "ssm_dt_bias", "ssm_a_log",
          "ssm_d", "ssm_norm_w")
_WEIGHTS = ("mod_w", "mod_b", "g_pre_mix", "g_post_mix", "g_pre_ffn", "g_post_ffn", "w_in", "sc_conv_w", "ssm_conv_w",
            "ssm_conv_b", "ssm_dt_bias", "ssm_a_log", "ssm_d", "ssm_norm_w", "w_sc_out", "w_sb_out", "w_ssm_out", "w_o",
            "w_ffn_in", "w_ffn_out")


def _gathered_to_full(g, row_sharded):
    _, dep, r, c = g.shape
    if row_sharded:
        return g.transpose(1, 0, 2, 3).reshape(dep, N_DEV * r, c)
    return jnp.concatenate([g[d] for d in range(N_DEV)], axis=2)


def _full_to_slots(w, row_sharded):
    dep, r, c = w.shape
    if row_sharded:
        return w.reshape(dep, N_DEV, r // N_DEV, c).transpose(1, 0, 2, 3).reshape(N_DEV, dep * (r // N_DEV), c)
    return w.reshape(dep, r, N_DEV, c // N_DEV).transpose(2, 0, 1, 3).reshape(N_DEV, dep * r, c // N_DEV)


def _pad_in_proj(shards):
    width = shards.shape[2]

    def cols(lo, hi):
        out = []
        while lo < hi:
            dev, a = divmod(lo, width)
            b = min(width, a + hi - lo)
            out.append(shards[dev, :, a:b])
            lo += b - a
        return out

    pad = jnp.zeros((shards.shape[1], OFF_Z - OFF_DT - 8), shards.dtype)
    return jnp.concatenate(cols(2824, 5896) + cols(0, 768) + cols(768, 1536) + cols(2048, 2816) + cols(2816, 2824)
                           + [pad] + cols(1536, 2048), axis=1)


def _in_proj_slots(dw_layers):
    width = IN_PROJ // N_DEV
    segments = ((0, 768, OFF_SC), (768, 1536, OFF_QKV), (1536, 2048, OFF_Z), (2048, 2816, OFF_XBC),
                (2816, 2824, OFF_DT), (2824, IN_PROJ, OFF_GATES))

    def internal(lo, hi):
        out = []
        for s0, s1, off in segments:
            a, b = max(lo, s0), min(hi, s1)
            if a < b:
                out.append((off + a - s0, off + b - s0))
        return out

    slots = []
    for d in range(N_DEV):
        pieces = internal(width * d, width * (d + 1))
        slots.append(jnp.concatenate([jnp.concatenate([w[:, a:b] for a, b in pieces], axis=1) for w in dw_layers], axis=0))
    return jnp.stack(slots)


def _ffn_in_keep_give(halves, cidx):
    width = 2 * FFN_HIDDEN // N_DEV
    keep, give = [], []
    for chip in range(4):
        src = 0 if chip < 2 else 1
        for out, core in ((keep, cidx), (give, 1 - cidx)):
            col0 = width * ((2 * chip) % 4 + core)
            out.append(jnp.concatenate([lax.dynamic_slice_in_dim(h[src], col0, width, axis=1) for h in halves], axis=0))
    return jnp.stack(keep), jnp.stack(give)


def _row(v):
    return v.reshape(1, -1)


def _local_step(x, target, mod, small, conv, big):
    lw, saved = [], []
    for l in range(DEPTH):
        w_in_p = _pad_in_proj(big["w_in_shards"][:, l])
        w_cat = jnp.concatenate([big["w_sc_out"][l], big["w_sb_out"][l], big["w_ssm_out"][l]], axis=0)
        hp_rows = jnp.zeros((8, LANES), F32).at[0, :SSM_HEADS].set(small["ssm_dt_bias"][l]).at[1, :SSM_HEADS].set(
            small["ssm_a_log"][l])
        hp_cols = jnp.concatenate([jnp.broadcast_to(small["ssm_dt_bias"][l][:, None], (SSM_HEADS, SSM_CHUNK)),
                                   jnp.broadcast_to(small["ssm_a_log"][l][:, None], (SSM_HEADS, SSM_CHUNK))], axis=0)
        lw.append(dict(
            w_in_p=w_in_p, w_cat=w_cat, w_o=big["w_o"][l], w_ffn_in=big["w_ffn_in"][l], w_ffn_out=big["w_ffn_out"][l],
            sc_w8=jnp.pad(conv["sc_conv_w"][l], ((0, 5), (0, 0))), ssm_w8=jnp.pad(conv["ssm_conv_w"][l], ((0, 4), (0, 0))),
            ssm_b=_row(small["ssm_conv_b"][l]), hp_rows=hp_rows, hp_cols=hp_cols,
            d_e=_row(jnp.repeat(small["ssm_d"][l], SSM_HEAD_DIM)), norm_w=_row(small["ssm_norm_w"][l]),
            g_pre_mix=_row(small["g_pre_mix"][l]), g_post_mix=_row(small["g_post_mix"][l]),
            g_pre_ffn=_row(small["g_pre_ffn"][l]), g_post_ffn=_row(small["g_post_ffn"][l]),
            shift1=mod[l, 0:1], scale1=mod[l, 1:2], gate1=mod[l, 2:3], shift2=mod[l, 3:4], scale2=mod[l, 4:5],
            gate2=mod[l, 5:6]))

    xl = x
    h = normmod_fwd(xl, lw[0]["g_pre_mix"], lw[0]["scale1"], lw[0]["shift1"], "normmod_fwd_0")
    dy = loss = None
    for l in range(DEPTH):
        w = lw[l]
        p = mm_nn([h], [w["w_in_p"]], BF16, f"in_proj_{l}")
        dt32 = mm_nn([h], [w["w_in_p"][:, OFF_DT:OFF_DT + LANES]], F32, f"in_proj_dt_{l}")
        ya, qkv, act = post_inproj(p, w["sc_w8"], w["ssm_w8"], w["ssm_b"], f"post_inproj_{l}")
        o, runs = sba_fwd(qkv, f"sba_fwd_{l}")
        dtt = dt32[:, :SSM_HEADS].T
        yc, ypre, states = ssd_fwd(act, p, dt32, dtt, w["hp_rows"], w["hp_cols"], w["d_e"], w["norm_w"], f"ssd_fwd_{l}")
        merged = branch_out_fwd(ya, o, yc, p, w["w_cat"], f"branch_fwd_{l}")
        mix = mm_nn([merged], [w["w_o"]], F32, f"out_proj_{l}")
        x1, h2 = resid_normmod_fwd(xl, mix, w["gate1"], w["g_post_mix"], w["g_pre_ffn"], w["scale2"], w["shift2"],
                                   f"resid_mix_{l}")
        gt, up, a = mm_swiglu_fwd(h2, w["w_ffn_in"], f"ffn_in_{l}")
        f = mm_nn([a], [w["w_ffn_out"]], F32, f"ffn_out_{l}")
        saved.append(dict(x=xl, h=h, p=p, ya=ya, qkv=qkv, act=act, o=o, runs=runs, dt32=dt32, dtt=dtt, yc=yc, ypre=ypre, states=states,
                          merged=merged, mix=mix, x1=x1, h2=h2, gt=gt, up=up, a=a, f=f))
        if l + 1 < DEPTH:
            nw = lw[l + 1]
            xl, h = resid_normmod_fwd(x1, f, w["gate2"], w["g_post_ffn"], nw["g_pre_mix"], nw["scale1"], nw["shift1"],
                                      f"resid_ffn_{l}")
        else:
            dy, loss = resid_loss(x1, f, w["gate2"], w["g_post_ffn"], target, "resid_loss")

    dmod = [None] * DEPTH
    gs = {k: [None] * DEPTH for k in _SMALL + _BIG}
    dxo = dy
    top, stl = lw[DEPTH - 1], saved[DEPTH - 1]
    df, dgate2, gs["g_post_ffn"][DEPTH - 1] = resid_bwd(dxo, stl["f"], top["gate2"], top["g_post_ffn"],
                                                        f"resid_ffn_bwd_{DEPTH - 1}")
    for l in reversed(range(DEPTH)):
        w, s = lw[l], saved[l]
        dgt, dup = mm_swiglu_bwd(df, w["w_ffn_out"], s["gt"], s["up"], f"ffn_out_bwd_{l}")
        gs["w_ffn_out"][l] = mm_tn(s["a"], df, f"dw_ffn_out_{l}")
        dh2 = mm_nt([dgt, dup], [w["w_ffn_in"], w["w_ffn_in"]], [0, FFN_HIDDEN], F32, f"ffn_in_bwd_{l}")
        gs["w_ffn_in"][l] = (mm_tn(s["h2"], dgt, f"dw_ffn_gate_{l}"), mm_tn(s["h2"], dup, f"dw_ffn_up_{l}"))
        dx1, dmix, dscale2, dshift2, gs["g_pre_ffn"][l], dgate1, gs["g_post_mix"][l] = normmod_resid_bwd(
            dh2, s["x1"], dxo, w["g_pre_ffn"], w["scale2"], s["mix"], w["gate1"], w["g_post_mix"], f"ffn_mix_bwd_{l}")
        dmerged = mm_nt([dmix], [w["w_o"]], [0], F32, f"out_proj_bwd_{l}")
        gs["w_o"][l] = mm_tn(s["merged"], dmix, f"dw_o_{l}")
        dp_gates, dya, dyb, dyc, dw_cat = branch_out_bwd(dmerged, s["ya"], s["o"], s["yc"], s["p"], w["w_cat"],
                                                         f"branch_bwd_{l}")
        gs["w_sc_out"][l], gs["w_sb_out"][l], gs["w_ssm_out"][l] = dw_cat[0:256], dw_cat[256:512], dw_cat[512:1024]
        dz, dact, ddt, dnw, dd_e, dhp = ssd_bwd(dyc, s["ypre"], s["act"], s["p"], s["dt32"], s["dtt"], s["states"], w["hp_rows"],
                                                w["hp_cols"], w["d_e"], w["norm_w"], f"ssd_bwd_{l}")
        gs["ssm_norm_w"][l] = dnw[0]
        gs["ssm_d"][l] = dd_e.reshape(SSM_HEADS, SSM_HEAD_DIM).sum(axis=1)
        gs["ssm_dt_bias"][l] = dhp[0, :SSM_HEADS]
        gs["ssm_a_log"][l] = dhp[1, :SSM_HEADS]
        dq, dk, dv = sba_bwd(s["qkv"], s["runs"], dyb, f"sba_bwd_{l}")
        dp, dscw, dssw, dssb = assemble_dp(dp_gates, dya, s["p"], w["sc_w8"], dq, dk, dv, dact, w["ssm_w8"], w["ssm_b"], ddt,
                                           dz, f"assemble_dp_{l}")
        gs["sc_conv_w"][l], gs["ssm_conv_w"][l], gs["ssm_conv_b"][l] = dscw[0:3], dssw[0:4], dssb[0]
        dh = mm_nt([dp], [w["w_in_p"]], [0], F32, f"in_proj_bwd_{l}")
        gs["w_in"][l] = mm_tn(s["h"], dp, f"dw_in_{l}")
        if l > 0:
            below, sb = lw[l - 1], saved[l - 1]
            dxo, df, dscale1, dshift1, gs["g_pre_mix"][l], dgate2_below, gs["g_post_ffn"][l - 1] = normmod_resid_bwd(
                dh, s["x"], dx1, w["g_pre_mix"], w["scale1"], sb["f"], below["gate2"], below["g_post_ffn"],
                f"mix_ffn_bwd_{l}")
        else:
            dxo, dscale1, dshift1, gs["g_pre_mix"][l] = normmod_bwd(dh, s["x"], dx1, w["g_pre_mix"], w["scale1"],
                                                                    f"normmod_mix_bwd_{l}")
            dgate2_below = None
        dmod[l] = jnp.concatenate([dshift1, dscale1, dgate1, dshift2, dscale2, dgate2], axis=0)
        dgate2 = dgate2_below
    for k in ("g_pre_mix", "g_post_mix", "g_pre_ffn", "g_post_ffn"):
        gs[k] = [g[0] for g in gs[k]]
    per_layer = ("w_in", "w_ffn_in")
    grads = {k: (v if k in per_layer else jnp.stack(v)) for k, v in gs.items() if k != "mod_b"}
    return loss[0, 0], dxo, jnp.stack(dmod), grads


def kernel(x, c, mod_w, mod_b, g_pre_mix, g_post_mix, g_pre_ffn, g_post_ffn, w_in, sc_conv_w, ssm_conv_w, ssm_conv_b, ssm_dt_bias, ssm_a_log, ssm_d, ssm_norm_w, w_sc_out, w_sb_out, w_ssm_out, w_o, w_ffn_in, w_ffn_out, loss_target, m_mod_w, m_mod_b, m_g_pre_mix, m_g_post_mix, m_g_pre_ffn, m_g_post_ffn, m_w_in, m_sc_conv_w, m_ssm_conv_w, m_ssm_conv_b, m_ssm_dt_bias, m_ssm_a_log, m_ssm_d, m_ssm_norm_w, m_w_sc_out, m_w_sb_out, m_w_ssm_out, m_w_o, m_w_ffn_in, m_w_ffn_out, v_mod_w, v_mod_b, v_g_pre_mix, v_g_post_mix, v_g_pre_ffn, v_g_post_ffn, v_w_in, v_sc_conv_w, v_ssm_conv_w, v_ssm_conv_b, v_ssm_dt_bias, v_ssm_a_log, v_ssm_d, v_ssm_norm_w, v_w_sc_out, v_w_sb_out, v_w_ssm_out, v_w_o, v_w_ffn_in, v_w_ffn_out):
    wts = dict(mod_w=mod_w, mod_b=mod_b, g_pre_mix=g_pre_mix, g_post_mix=g_post_mix, g_pre_ffn=g_pre_ffn,
               g_post_ffn=g_post_ffn, w_in=w_in, sc_conv_w=sc_conv_w, ssm_conv_w=ssm_conv_w, ssm_conv_b=ssm_conv_b,
               ssm_dt_bias=ssm_dt_bias, ssm_a_log=ssm_a_log, ssm_d=ssm_d, ssm_norm_w=ssm_norm_w, w_sc_out=w_sc_out,
               w_sb_out=w_sb_out, w_ssm_out=w_ssm_out, w_o=w_o, w_ffn_in=w_ffn_in, w_ffn_out=w_ffn_out)
    ms = dict(mod_w=m_mod_w, mod_b=m_mod_b, g_pre_mix=m_g_pre_mix, g_post_mix=m_g_post_mix, g_pre_ffn=m_g_pre_ffn,
              g_post_ffn=m_g_post_ffn, w_in=m_w_in, sc_conv_w=m_sc_conv_w, ssm_conv_w=m_ssm_conv_w,
              ssm_conv_b=m_ssm_conv_b, ssm_dt_bias=m_ssm_dt_bias, ssm_a_log=m_ssm_a_log, ssm_d=m_ssm_d,
              ssm_norm_w=m_ssm_norm_w, w_sc_out=m_w_sc_out, w_sb_out=m_w_sb_out, w_ssm_out=m_w_ssm_out, w_o=m_w_o,
              w_ffn_in=m_w_ffn_in, w_ffn_out=m_w_ffn_out)
    vs = dict(mod_w=v_mod_w, mod_b=v_mod_b, g_pre_mix=v_g_pre_mix, g_post_mix=v_g_post_mix, g_pre_ffn=v_g_pre_ffn,
              g_post_ffn=v_g_post_ffn, w_in=v_w_in, sc_conv_w=v_sc_conv_w, ssm_conv_w=v_ssm_conv_w,
              ssm_conv_b=v_ssm_conv_b, ssm_dt_bias=v_ssm_dt_bias, ssm_a_log=v_ssm_a_log, ssm_d=v_ssm_d,
              ssm_norm_w=v_ssm_norm_w, w_sc_out=v_w_sc_out, w_sb_out=v_w_sb_out, w_ssm_out=v_w_ssm_out, w_o=v_w_o,
              w_ffn_in=v_w_ffn_in, w_ffn_out=v_w_ffn_out)
    me = 4 * lax.axis_index("x") + 2 * lax.axis_index("y") + lax.axis_index("c")
    mod_cols = mod_w.shape[2]

    pack1, sizes1 = _pack_rows([c, sc_conv_w, ssm_conv_w], F32, 8)
    got1 = all_gather_multi([pack1], "gather_c_conv")[0].reshape(N_DEV, -1)
    c_all, sc_g, ssm_g = _unpack(got1, sizes1, [(D_MODEL,), sc_conv_w.shape, ssm_conv_w.shape])
    conv = dict(sc_conv_w=_gathered_to_full(sc_g, False), ssm_conv_w=_gathered_to_full(ssm_g, False))

    mod_b_shard = lax.dynamic_slice_in_dim(mod_b, me * mod_cols, mod_cols, axis=1).reshape(DEPTH, 1, mod_cols)
    mod_sh = mod_shard_fwd(c_all, mod_w, mod_b_shard, "mod_shard_fwd")
    pack2, sizes2 = _pack_rows([mod_sh], F32, 8)
    got2 = all_gather_multi([pack2], "gather_mod")[0].reshape(N_DEV, -1)
    mod_all = _unpack(got2, sizes2, [mod_sh.shape])[0]
    mod_mine = lax.dynamic_index_in_dim(mod_all, me, axis=2, keepdims=False)
    mod = mod_mine.transpose(1, 0, 2).reshape(DEPTH, 6, D_MODEL)

    mm_names = [k for k in _BIG if k not in _CONV]
    gathered = all_gather_multi([wts[k].astype(BF16) for k in mm_names], "gather_weights")
    big = {k: _gathered_to_full(g, k in _ROW_SHARDED) for k, g in zip(mm_names, gathered) if k != "w_in"}
    big["w_in_shards"] = gathered[mm_names.index("w_in")]

    small = {k: wts[k] for k in _SMALL}
    loss_part, dx, dmod, grads = _local_step(x[0], loss_target[0], mod, small, conv, big)
    loss = lax.psum(loss_part, ("x", "y", "c"))

    small_parts = [dmod.reshape(DEPTH, 6 * D_MODEL)] + [grads[k] for k in _SMALL[1:]]
    pack5, sizes5 = _pack_rows(small_parts, F32, 8)
    pack_conv, sizes_conv = _pack_rows([grads[k] for k in _CONV], F32, 8)
    got5, got_conv = all_gather_multi([pack5, pack_conv], "gather_small_grads")
    w5, _ = _pack_rows([wts[k] for k in _SMALL], F32, 8)
    m5, _ = _pack_rows([ms[k] for k in _SMALL], F32, 8)
    v5, _ = _pack_rows([vs[k] for k in _SMALL], F32, 8)
    res5 = adamw_flat(got5, w5, m5, v5, "adamw_small")
    small_out = [_unpack(r.reshape(-1), sizes5, [wts[k].shape for k in _SMALL]) for r in res5]

    conv_full = _unpack(got_conv.reshape(N_DEV, -1), sizes_conv, [grads[k].shape for k in _CONV])
    conv_mine = [lax.dynamic_slice_in_dim(g, me * wts[k].shape[2], wts[k].shape[2], axis=3)
                 for k, g in zip(_CONV, conv_full)]
    conv_slot_sizes = [math.prod(wts[k].shape) for k in _CONV]
    conv_slots = jnp.concatenate([g.reshape(N_DEV, -1) for g in conv_mine], axis=1)
    pad_c = -conv_slots.shape[1] % (8 * LANES)
    conv_slots = jnp.pad(conv_slots, ((0, 0), (0, pad_c))).reshape(N_DEV, -1, LANES)
    wc, _ = _pack_rows([wts[k] for k in _CONV], F32, 8)
    mc, _ = _pack_rows([ms[k] for k in _CONV], F32, 8)
    vc, _ = _pack_rows([vs[k] for k in _CONV], F32, 8)
    res_c = adamw_flat(conv_slots, wc, mc, vc, "adamw_conv")
    conv_out = [_unpack(r.reshape(-1), conv_slot_sizes, [wts[k].shape for k in _CONV]) for r in res_c]

    dmod_all = got5.reshape(N_DEV, -1)[:, :DEPTH * 6 * D_MODEL].reshape(N_DEV, DEPTH, 6 * D_MODEL)
    dmod_shard = lax.dynamic_slice_in_dim(dmod_all, me * mod_cols, mod_cols, axis=2).transpose(1, 0, 2)
    g_mod_w = mod_w_grad(c_all, dmod_shard, "mod_w_grad")
    rows2 = lambda a: a.reshape(a.shape[0] * a.shape[1], a.shape[2])
    res_mw = adamw_flat(rows2(g_mod_w)[None], rows2(mod_w), rows2(m_mod_w), rows2(v_mod_w), "adamw_mod_w")
    mod_w_out = [r.reshape(mod_w.shape) for r in res_mw]

    cidx = lax.axis_index("c")
    keeps, gives = [], []
    for k in mm_names:
        if k == "w_ffn_in":
            keep, give = _ffn_in_keep_give(grads[k], cidx)
        else:
            slots = (_in_proj_slots(grads[k]) if k == "w_in"
                     else _full_to_slots(grads[k].astype(BF16), k in _ROW_SHARDED))
            by_chip = slots.reshape(4, 2, *slots.shape[1:])
            keep = lax.dynamic_index_in_dim(by_chip, cidx, 1, keepdims=False)
            give = lax.dynamic_index_in_dim(by_chip, 1 - cidx, 1, keepdims=False)
        keeps.append(keep)
        gives.append(give)
    gots = swap_with_sibling(gives, "swap_grads")
    pairs = []
    for k, keep, got in zip(mm_names, keeps, gots):
        rows4 = (4 * keep.shape[1], keep.shape[2])
        pairs.append(add_pairs(keep.reshape(rows4), got.reshape(rows4), f"add_pairs_{k}").reshape(keep.shape))
    recvs = exchange_chips(pairs, "exchange_grads")
    big_out = {}
    for k, recv in zip(mm_names, recvs):
        res = adamw_flat(recv, rows2(wts[k]), rows2(ms[k]), rows2(vs[k]), f"adamw_{k}")
        big_out[k] = [r.reshape(wts[k].shape) for r in res]

    outs = []
    for kind in range(4):
        by_name = {"mod_w": mod_w_out[kind]}
        by_name.update(zip(_SMALL, small_out[kind]))
        by_name.update(zip(_CONV, conv_out[kind]))
        by_name.update({k: v[kind] for k, v in big_out.items()})
        outs.extend(by_name[k] for k in _WEIGHTS)
    return (loss, dx[None], *outs)
```

```python
import math

import jax
import jax.numpy as jnp
from jax import lax
from jax.experimental import pallas as pl
from jax.experimental.pallas import tpu as pltpu

F32 = jnp.float32
BF16 = jnp.bfloat16
HI = lax.Precision.HIGHEST

N_DEV = 8
D_MODEL = 1024
DEPTH = 2
SC_WIDTH = 256
SB_WIDTH = 256
SB_HEAD_DIM = 64
SSM_INNER = 512
SSM_HEADS = 8
SSM_HEAD_DIM = 64
SSM_GROUPS = 2
SSM_STATE = 64
SSM_CHUNK = 256
SSM_CONV_DIM = 768
FFN_HIDDEN = 2816
NORM_EPS = 1e-6
IN_PROJ = 5896
LANES = 128
VMEM_LIMIT = 56 * 1024 * 1024

OFF_GATES = 0
OFF_SC = 3072
OFF_QKV = 3840
OFF_XBC = 4608
OFF_DT = 5376
OFF_Z = 5632
IN_PAD = 6144

ADAM_LR = 0.001
ADAM_B1 = 0.9
ADAM_B2 = 0.999
ADAM_EPS = 1e-08
ADAM_WD = 0.01
ADAM_STEP = 10

MESH_ID = pl.DeviceIdType.MESH


def _cparams(sem=None):
    return pltpu.CompilerParams(dimension_semantics=sem, vmem_limit_bytes=VMEM_LIMIT)


def _my_pos():
    return lax.axis_index("x"), lax.axis_index("y"), lax.axis_index("c")


def all_gather_multi(blocks, name):
    n = len(blocks)

    def body(*refs):
        x_refs, o_refs = refs[:n], refs[n:2 * n]
        send_sems, recv_sems, local_sems = refs[2 * n:]
        x, y, c = _my_pos()
        me, sibling = (x, y, c), (x, y, 1 - c)
        chips = [(1 - x, y), (x, 1 - y), (1 - x, 1 - y)]

        def slot(a, px, py, pc):
            return o_refs[a].at[4 * px + 2 * py + pc]

        def copy(a, k, blk, to, src=None):
            return pltpu.make_async_remote_copy(
                src_ref=slot(a, *blk) if src is None else src, dst_ref=slot(a, *blk),
                send_sem=send_sems.at[7 * a + k], recv_sem=recv_sems.at[7 * a + k], device_id=to, device_id_type=MESH_ID)

        mine = [pltpu.make_async_copy(x_refs[a], slot(a, *me), local_sems.at[a]) for a in range(n)]
        for cp in mine:
            cp.start()
        first = [copy(a, 1 + j, me, (*chip, c), src=x_refs[a]) for j, chip in enumerate(chips) for a in range(n)]
        first += [copy(a, 0, me, sibling, src=x_refs[a]) for a in range(n)]
        for cp in first:
            cp.start()
        passed = []
        for j, chip in enumerate(chips):
            for a in range(n):
                copy(a, 1 + j, (*chip, c), me).wait_recv()
                fwd = copy(a, 4 + j, (*chip, c), sibling)
                fwd.start()
                passed.append(fwd)
        for a in range(n):
            copy(a, 0, sibling, me).wait_recv()
            for j, chip in enumerate(chips):
                copy(a, 4 + j, (*chip, 1 - c), me).wait_recv()
        for cp in first + passed:
            cp.wait_send()
        for cp in mine:
            cp.wait()

    any_spec = pl.BlockSpec(memory_space=pl.ANY)
    return pl.pallas_call(
        body, name=name,
        out_shape=[jax.ShapeDtypeStruct((N_DEV,) + b.shape, b.dtype) for b in blocks],
        in_specs=[any_spec] * n, out_specs=[any_spec] * n,
        scratch_shapes=[pltpu.SemaphoreType.DMA((7 * n,)), pltpu.SemaphoreType.DMA((7 * n,)), pltpu.SemaphoreType.DMA((n,))],
    )(*blocks)


def swap_with_sibling(gives, name):
    n = len(gives)

    def body(*refs):
        g_refs, r_refs = refs[:n], refs[n:2 * n]
        send_sems, recv_sems = refs[2 * n:]
        x, y, c = _my_pos()
        copies = [pltpu.make_async_remote_copy(
            src_ref=g_refs[a], dst_ref=r_refs[a], send_sem=send_sems.at[a], recv_sem=recv_sems.at[a],
            device_id=(x, y, 1 - c), device_id_type=MESH_ID) for a in range(n)]
        for cp in copies:
            cp.start()
        for cp in copies:
            cp.wait_recv()
        for cp in copies:
            cp.wait_send()

    any_spec = pl.BlockSpec(memory_space=pl.ANY)
    return pl.pallas_call(
        body, name=name, out_shape=[jax.ShapeDtypeStruct(g.shape, g.dtype) for g in gives],
        in_specs=[any_spec] * n, out_specs=[any_spec] * n,
        scratch_shapes=[pltpu.SemaphoreType.DMA((n,)), pltpu.SemaphoreType.DMA((n,))],
    )(*gives)


def exchange_chips(sends, name):
    n = len(sends)

    def body(*refs):
        s_refs, r_refs = refs[:n], refs[n:2 * n]
        send_sems, recv_sems, local_sems = refs[2 * n:]
        x, y, c = _my_pos()
        me = 2 * x + y
        mine = [pltpu.make_async_copy(s_refs[a].at[me], r_refs[a].at[me], local_sems.at[a]) for a in range(n)]
        for cp in mine:
            cp.start()
        copies = []
        for k in (2, 1, 3):
            px, py = x ^ (k >> 1), y ^ (k & 1)
            for a in range(n):
                cp = pltpu.make_async_remote_copy(
                    src_ref=s_refs[a].at[2 * px + py], dst_ref=r_refs[a].at[me],
                    send_sem=send_sems.at[3 * a + k - 1], recv_sem=recv_sems.at[3 * a + k - 1],
                    device_id=(px, py, c), device_id_type=MESH_ID)
                cp.start()
                copies.append(cp)
        for cp in copies:
            cp.wait_recv()
        for cp in copies:
            cp.wait_send()
        for cp in mine:
            cp.wait()

    any_spec = pl.BlockSpec(memory_space=pl.ANY)
    return pl.pallas_call(
        body, name=name, out_shape=[jax.ShapeDtypeStruct(s.shape, s.dtype) for s in sends],
        in_specs=[any_spec] * n, out_specs=[any_spec] * n,
        scratch_shapes=[pltpu.SemaphoreType.DMA((3 * n,)), pltpu.SemaphoreType.DMA((3 * n,)), pltpu.SemaphoreType.DMA((n,))],
    )(*sends)


def add_pairs(a, b, name, tr=512):
    rows, cols = a.shape
    tr = max(d for d in range(16, min(tr, rows) + 1, 16) if rows % d == 0)

    def body(a_ref, b_ref, o_ref):
        o_ref[...] = (a_ref[...].astype(F32) + b_ref[...].astype(F32)).astype(BF16)

    tile = pl.BlockSpec((tr, cols), lambda i: (i, 0))
    return pl.pallas_call(body, name=name, grid=(rows // tr,), in_specs=[tile, tile], out_specs=tile,
                          out_shape=jax.ShapeDtypeStruct((rows, cols), BF16), compiler_params=_cparams(("parallel",)))(a, b)


def _pack_rows(parts, dtype, row_multiple):
    flat = [p.astype(dtype).reshape(-1) for p in parts]
    sizes = [f.shape[0] for f in flat]
    total = sum(sizes)
    quantum = LANES * row_multiple
    padded = -(-total // quantum) * quantum
    if padded > total:
        flat.append(jnp.zeros((padded - total,), dtype))
    return jnp.concatenate(flat).reshape(padded // LANES, LANES), sizes


def _unpack(flat, sizes, shapes):
    out, off = [], 0
    lead = flat.shape[:-1]
    for n, shp in zip(sizes, shapes):
        out.append(flat[..., off:off + n].reshape(lead + tuple(shp)))
        off += n
    return out


def rows_call(name, body, n_rows, tr, ins, outs, scratch=(), aliases=None):
    n_tiles = n_rows // tr
    assert n_tiles * tr == n_rows
    in_specs, arrays = [], []
    for arr, kind in ins:
        arrays.append(arr)
        if kind == "row":
            in_specs.append(pl.BlockSpec((tr, arr.shape[1]), lambda i: (i, 0)))
        elif kind == "any":
            in_specs.append(pl.BlockSpec(memory_space=pl.ANY))
        elif kind == "full":
            in_specs.append(pl.BlockSpec(arr.shape, lambda i, nd=arr.ndim: (0,) * nd))
        elif kind[0] == "row":
            _, w, ci = kind
            in_specs.append(pl.BlockSpec((tr, w), lambda i, ci=ci: (i, ci)))
        elif kind[0] == "prev8":
            _, w, ci = kind
            hr = 8 * (4 // arr.dtype.itemsize)
            in_specs.append(pl.BlockSpec((hr, w), lambda i, ci=ci, hr=hr: (jnp.maximum(i * (tr // hr) - 1, 0), ci)))
        elif kind[0] == "next8":
            _, w, ci = kind
            hr = 8 * (4 // arr.dtype.itemsize)
            last = n_rows // hr - 1
            in_specs.append(pl.BlockSpec((hr, w), lambda i, ci=ci, last=last, hr=hr: (jnp.minimum((i + 1) * (tr // hr), last), ci)))
        else:
            raise ValueError(kind)
    out_specs, out_shapes = [], []
    for shape, dtype, kind in outs:
        out_shapes.append(jax.ShapeDtypeStruct(shape, dtype))
        if kind == "row":
            out_specs.append(pl.BlockSpec((tr, shape[1]), lambda i: (i, 0)))
        elif kind[0] == "row":
            _, w, ci = kind
            out_specs.append(pl.BlockSpec((tr, w), lambda i, ci=ci: (i, ci)))
        else:
            out_specs.append(pl.BlockSpec(shape, lambda i, nd=len(shape): (0,) * nd))
    has_acc = any(k == "acc" for _, _, k in outs)
    return pl.pallas_call(
        body, name=name, grid=(n_tiles,), in_specs=in_specs, out_specs=out_specs, out_shape=out_shapes,
        scratch_shapes=list(scratch), input_output_aliases=dict(aliases or {}),
        compiler_params=_cparams(("arbitrary",) if has_acc else ("parallel",)),
    )(*arrays)


def _prev8(ref):
    n = ref.shape[0]
    return ref[n - 8:n, :].astype(F32)


def _next8(ref):
    return ref[0:8, :].astype(F32)


def _acc(ref, val):
    @pl.when(pl.program_id(0) == 0)
    def _():
        ref[...] = jnp.zeros_like(ref)
    ref[...] += val


def _rstd(x):
    return lax.rsqrt(jnp.mean(x * x, axis=-1, keepdims=True) + NORM_EPS)


def _sigmoid(x):
    return 1.0 / (1.0 + jnp.exp(-x))


def _silu(x):
    return x * _sigmoid(x)


def _dsilu(x):
    s = _sigmoid(x)
    return s * (1.0 + x * (1.0 - s))


def _softplus(x):
    return jnp.maximum(x, 0.0) + jnp.log(1.0 + jnp.exp(-jnp.abs(x)))


def _log_sigmoid_neg(x):
    t = -x
    return jnp.minimum(t, 0.0) - jnp.log(1.0 + jnp.exp(jnp.minimum(x, t)))


def normmod_fwd(x, g, scale, shift, name):
    t, d = x.shape

    def body(x_ref, g_ref, sc_ref, sh_ref, h_ref):
        xv = x_ref[...]
        h = xv * _rstd(xv) * g_ref[...] * (1.0 + sc_ref[...]) + sh_ref[...]
        h_ref[...] = h.astype(BF16)

    return rows_call(name, body, t, 512, [(x, "row"), (g, "full"), (scale, "full"), (shift, "full")],
                     [((t, d), BF16, "row")])[0]


def resid_normmod_fwd(x, f, gate, g_post, g_pre, scale, shift, name):
    t, d = x.shape

    def body(x_ref, f_ref, gate_ref, gp_ref, g_ref, sc_ref, sh_ref, xo_ref, h_ref):
        fv = f_ref[...]
        xn = x_ref[...] + gate_ref[...] * (fv * _rstd(fv) * gp_ref[...])
        xo_ref[...] = xn
        h = xn * _rstd(xn) * g_ref[...] * (1.0 + sc_ref[...]) + sh_ref[...]
        h_ref[...] = h.astype(BF16)

    return rows_call(name, body, t, 512,
                     [(x, "row"), (f, "row"), (gate, "full"), (g_post, "full"), (g_pre, "full"), (scale, "full"),
                      (shift, "full")],
                     [((t, d), F32, "row"), ((t, d), BF16, "row")])


def resid_loss(x, f, gate, g_post, target, name):
    t, d = x.shape

    def body(x_ref, f_ref, gate_ref, gp_ref, tg_ref, dy_ref, loss_ref):
        fv = f_ref[...]
        yv = x_ref[...] + gate_ref[...] * (fv * _rstd(fv) * gp_ref[...])
        err = yv - tg_ref[...]
        dy_ref[...] = err * (1.0 / d)
        part = 0.5 * jnp.sum(jnp.mean(err * err, axis=-1, keepdims=True), axis=0, keepdims=True)
        _acc(loss_ref, jnp.broadcast_to(part, loss_ref.shape))

    return rows_call(name, body, t, 512,
                     [(x, "row"), (f, "row"), (gate, "full"), (g_post, "full"), (target, "row")],
                     [((t, d), F32, "row"), ((8, LANES), F32, "acc")])


def resid_bwd(dx, f, gate, g_post, name):
    t, d = dx.shape

    def body(dx_ref, f_ref, gate_ref, gp_ref, df_ref, dgate_ref, dg_ref):
        fv, dxv, gp = f_ref[...], dx_ref[...], gp_ref[...]
        r = _rstd(fv)
        fn = fv * r
        _acc(dgate_ref, jnp.sum(dxv * (fn * gp), axis=0, keepdims=True))
        dn = dxv * gate_ref[...]
        _acc(dg_ref, jnp.sum(dn * fn, axis=0, keepdims=True))
        u = dn * gp
        df = r * (u - fn * jnp.mean(fn * u, axis=-1, keepdims=True))
        df_ref[...] = df.astype(BF16)

    return rows_call(name, body, t, 512, [(dx, "row"), (f, "row"), (gate, "full"), (g_post, "full")],
                     [((t, d), BF16, "row"), ((1, d), F32, "acc"), ((1, d), F32, "acc")])


def normmod_bwd(dh, x, dx_in, g, scale, name):
    t, d = x.shape

    def body(dh_ref, x_ref, dxi_ref, g_ref, sc_ref, dx_ref, dsc_ref, dsh_ref, dg_ref):
        xv, dhv, gv = x_ref[...], dh_ref[...].astype(F32), g_ref[...]
        r = _rstd(xv)
        xn = xv * r
        _acc(dsc_ref, jnp.sum(dhv * (xn * gv), axis=0, keepdims=True))
        _acc(dsh_ref, jnp.sum(dhv, axis=0, keepdims=True))
        dn = dhv * (1.0 + sc_ref[...])
        _acc(dg_ref, jnp.sum(dn * xn, axis=0, keepdims=True))
        u = dn * gv
        dx_ref[...] = dxi_ref[...] + r * (u - xn * jnp.mean(xn * u, axis=-1, keepdims=True))

    return rows_call(name, body, t, 512, [(dh, "row"), (x, "row"), (dx_in, "row"), (g, "full"), (scale, "full")],
                     [((t, d), F32, "row"), ((1, d), F32, "acc"), ((1, d), F32, "acc"), ((1, d), F32, "acc")])


def normmod_resid_bwd(dh, x, dx_in, g, scale, f, gate, g_post, name):
    t, d = x.shape

    def body(dh_ref, x_ref, dxi_ref, g_ref, sc_ref, f_ref, gate_ref, gp_ref,
             dx_ref, df_ref, dsc_ref, dsh_ref, dg_ref, dgate_ref, dgp_ref):
        xv, dhv, gv = x_ref[...], dh_ref[...].astype(F32), g_ref[...]
        r = _rstd(xv)
        xn = xv * r
        _acc(dsc_ref, jnp.sum(dhv * (xn * gv), axis=0, keepdims=True))
        _acc(dsh_ref, jnp.sum(dhv, axis=0, keepdims=True))
        dn = dhv * (1.0 + sc_ref[...])
        _acc(dg_ref, jnp.sum(dn * xn, axis=0, keepdims=True))
        u = dn * gv
        dxv = dxi_ref[...] + r * (u - xn * jnp.mean(xn * u, axis=-1, keepdims=True))
        dx_ref[...] = dxv
        fv, gp = f_ref[...], gp_ref[...]
        rf = _rstd(fv)
        fn = fv * rf
        _acc(dgate_ref, jnp.sum(dxv * (fn * gp), axis=0, keepdims=True))
        dnf = dxv * gate_ref[...]
        _acc(dgp_ref, jnp.sum(dnf * fn, axis=0, keepdims=True))
        uf = dnf * gp
        df_ref[...] = (rf * (uf - fn * jnp.mean(fn * uf, axis=-1, keepdims=True))).astype(BF16)

    vec = ((1, d), F32, "acc")
    return rows_call(name, body, t, 512,
                     [(dh, "row"), (x, "row"), (dx_in, "row"), (g, "full"), (scale, "full"), (f, "row"), (gate, "full"),
                      (g_post, "full")],
                     [((t, d), F32, "row"), ((t, d), BF16, "row"), vec, vec, vec, vec, vec])


def _pick(n, prefs):
    for p in prefs:
        if n % p == 0:
            return p
    return n


def mm_nn(a_list, b_list, out_dtype, name, tm=1024, tn=None, tk=None):
    m, k = a_list[0].shape
    n = b_list[0].shape[1]
    tm = min(tm, m)
    tn = tn or _pick(n, (1024, 768, 512, 256, 128))
    tk = tk or _pick(k, (1024, 1408, 512, 256))
    nk = k // tk
    npair = len(a_list)

    if nk == 1 and npair == 1:
        def body1(a_ref, b_ref, o_ref):
            o_ref[...] = jnp.dot(a_ref[...], b_ref[...], preferred_element_type=F32).astype(o_ref.dtype)

        return pl.pallas_call(
            body1, name=name, grid=(m // tm, n // tn),
            in_specs=[pl.BlockSpec((tm, k), lambda i, j: (i, 0)), pl.BlockSpec((k, tn), lambda i, j: (0, j))],
            out_specs=pl.BlockSpec((tm, tn), lambda i, j: (i, j)),
            out_shape=jax.ShapeDtypeStruct((m, n), out_dtype),
            compiler_params=_cparams(("parallel", "parallel")),
        )(a_list[0], b_list[0])

    def body(*refs):
        a_refs, b_refs = refs[:npair], refs[npair:2 * npair]
        o_ref, acc = refs[2 * npair], refs[2 * npair + 1]
        kk = pl.program_id(2)

        @pl.when(kk == 0)
        def _():
            acc[...] = jnp.zeros_like(acc)

        s = acc[...]
        for a_ref, b_ref in zip(a_refs, b_refs):
            s = s + jnp.dot(a_ref[...], b_ref[...], preferred_element_type=F32)
        acc[...] = s

        @pl.when(kk == nk - 1)
        def _():
            o_ref[...] = acc[...].astype(o_ref.dtype)

    return pl.pallas_call(
        body, name=name, grid=(m // tm, n // tn, nk),
        in_specs=[pl.BlockSpec((tm, tk), lambda i, j, kk: (i, kk))] * npair
        + [pl.BlockSpec((tk, tn), lambda i, j, kk: (kk, j))] * npair,
        out_specs=pl.BlockSpec((tm, tn), lambda i, j, kk: (i, j)),
        out_shape=jax.ShapeDtypeStruct((m, n), out_dtype),
        scratch_shapes=[pltpu.VMEM((tm, tn), F32)],
        compiler_params=_cparams(("parallel", "parallel", "arbitrary")),
    )(*a_list, *b_list)


def mm_nt(a_list, b_list, b_koff, out_dtype, name, tm=1024):
    m, k = a_list[0].shape
    n = b_list[0].shape[0]
    tm = min(tm, m)
    tn = _pick(n, (1024, 512, 256))
    tk = _pick(k, (1024, 1408, 512, 256))
    nk = k // tk
    npair = len(a_list)
    koff = [o // tk for o in b_koff]
    nt_dims = (((1,), (1,)), ((), ()))

    def body(*refs):
        a_refs, b_refs = refs[:npair], refs[npair:2 * npair]
        o_ref, acc = refs[2 * npair], refs[2 * npair + 1]
        kk = pl.program_id(2)

        @pl.when(kk == 0)
        def _():
            acc[...] = jnp.zeros_like(acc)

        s = acc[...]
        for a_ref, b_ref in zip(a_refs, b_refs):
            s = s + lax.dot_general(a_ref[...], b_ref[...], nt_dims, preferred_element_type=F32)
        acc[...] = s

        @pl.when(kk == nk - 1)
        def _():
            o_ref[...] = acc[...].astype(o_ref.dtype)

    return pl.pallas_call(
        body, name=name, grid=(m // tm, n // tn, nk),
        in_specs=[pl.BlockSpec((tm, tk), lambda i, j, kk: (i, kk))] * npair
        + [pl.BlockSpec((tn, tk), lambda i, j, kk, o=o: (j, kk + o)) for o in koff],
        out_specs=pl.BlockSpec((tm, tn), lambda i, j, kk: (i, j)),
        out_shape=jax.ShapeDtypeStruct((m, n), out_dtype),
        scratch_shapes=[pltpu.VMEM((tm, tn), F32)],
        compiler_params=_cparams(("parallel", "parallel", "arbitrary")),
    )(*a_list, *b_list)


def mm_tn(a, b, name, tt=512):
    t, ka = a.shape
    n = b.shape[1]
    ta = _pick(ka, (1024, 1408, 512, 256))
    tn = _pick(n, (2048, 1024, 1408, 512, 256))
    nt = t // tt

    def body(a_ref, b_ref, o_ref, acc):
        s = pl.program_id(2)

        @pl.when(s == 0)
        def _():
            acc[...] = jnp.zeros_like(acc)

        acc[...] += lax.dot_general(a_ref[...], b_ref[...], (((0,), (0,)), ((), ())), preferred_element_type=F32)

        @pl.when(s == nt - 1)
        def _():
            o_ref[...] = acc[...].astype(BF16)

    return pl.pallas_call(
        body, name=name, grid=(ka // ta, n // tn, nt),
        in_specs=[pl.BlockSpec((tt, ta), lambda i, j, s: (s, i)), pl.BlockSpec((tt, tn), lambda i, j, s: (s, j))],
        out_specs=pl.BlockSpec((ta, tn), lambda i, j, s: (i, j)),
        out_shape=jax.ShapeDtypeStruct((ka, n), BF16),
        scratch_shapes=[pltpu.VMEM((ta, tn), F32)],
        compiler_params=_cparams(("parallel", "parallel", "arbitrary")),
    )(a, b)


def mm_swiglu_fwd(h, w_ffn_in, name, tm=512, tn=1408):
    m, k = h.shape
    nh = FFN_HIDDEN // tn

    def body(h_ref, wg_ref, wu_ref, gt_ref, up_ref, a_ref):
        hv = h_ref[...]
        gt = jnp.dot(hv, wg_ref[...], preferred_element_type=F32)
        up = jnp.dot(hv, wu_ref[...], preferred_element_type=F32)
        gt_ref[...] = gt.astype(BF16)
        up_ref[...] = up.astype(BF16)
        a_ref[...] = (_silu(gt) * up).astype(BF16)

    shp = jax.ShapeDtypeStruct((m, FFN_HIDDEN), BF16)
    ospec = pl.BlockSpec((tm, tn), lambda i, j: (i, j))
    return pl.pallas_call(
        body, name=name, grid=(m // tm, nh),
        in_specs=[pl.BlockSpec((tm, k), lambda i, j: (i, 0)), pl.BlockSpec((k, tn), lambda i, j: (0, j)),
                  pl.BlockSpec((k, tn), lambda i, j: (0, j + nh))],
        out_specs=[ospec, ospec, ospec], out_shape=[shp, shp, shp],
        compiler_params=_cparams(("parallel", "parallel")),
    )(h, w_ffn_in, w_ffn_in)


def mm_swiglu_bwd(df, w_out, gt, up, name, tm=256, sub=256):
    m, k = df.shape
    n_sub = FFN_HIDDEN // sub

    def body(df_ref, w_ref, gt_ref, up_ref, dgt_ref, dup_ref):
        dfv = df_ref[...]

        def chunk_dot(c):
            return lax.dot_general(dfv, w_ref[c * sub:(c + 1) * sub, :], (((1,), (1,)), ((), ())), preferred_element_type=F32)

        da_next = chunk_dot(0)
        for c in range(n_sub):
            da = da_next
            if c + 1 < n_sub:
                da_next = chunk_dot(c + 1)
            cols = slice(c * sub, (c + 1) * sub)
            gtv = gt_ref[:, cols].astype(F32)
            sg = _sigmoid(gtv)
            dgt_ref[:, cols] = (da * up_ref[:, cols].astype(F32) * (sg * (1.0 + gtv * (1.0 - sg)))).astype(BF16)
            dup_ref[:, cols] = (da * (gtv * sg)).astype(BF16)

    shp = jax.ShapeDtypeStruct((m, FFN_HIDDEN), BF16)
    tile = pl.BlockSpec((tm, FFN_HIDDEN), lambda i: (i, 0))
    return pl.pallas_call(
        body, name=name, grid=(m // tm,),
        in_specs=[pl.BlockSpec((tm, k), lambda i: (i, 0)), pl.BlockSpec((FFN_HIDDEN, k), lambda i: (0, 0)), tile, tile],
        out_specs=[tile, tile], out_shape=[shp, shp],
        compiler_params=_cparams(("parallel",)),
    )(df, w_out, gt, up)


def _shift_down(x, prev8, j):
    if j == 0:
        return x
    xr = pltpu.roll(x, j, 0)
    pr = pltpu.roll(prev8, j, 0)
    row = lax.broadcasted_iota(jnp.int32, (8, x.shape[1]), 0)
    head = jnp.where(row < j, pr, xr[:8])
    return head if x.shape[0] == 8 else jnp.concatenate([head, xr[8:]], axis=0)


def _shift_up(x, next8, j):
    if j == 0:
        return x
    n = x.shape[0]
    xr = pltpu.roll(x, n - j, 0)
    nr = pltpu.roll(next8, 8 - j, 0)
    row = lax.broadcasted_iota(jnp.int32, (8, x.shape[1]), 0)
    return jnp.concatenate([xr[:n - 8], jnp.where(row >= 8 - j, nr, xr[n - 8:])], axis=0)


def _conv_taps(x, prev8, w_ref, taps):
    out = None
    for k in range(taps):
        term = w_ref[k:k + 1, :] * _shift_down(x, prev8, taps - 1 - k)
        out = term if out is None else out + term
    return out


def post_inproj(p, sc_w, ssm_w, ssm_b, name, tr=512):
    t = p.shape[0]

    def body(sc_ref, scp_ref, qkv_ref, xbc_ref, xbcp_ref, scw_ref, sw_ref, sb_ref, ya_ref, qkvo_ref, act_ref):
        first = (pl.program_id(0) > 0).astype(F32)
        sc = sc_ref[...].astype(F32)
        scp = _prev8(scp_ref) * first
        u = sc[:, 256:512] * sc[:, 512:768]
        up = scp[:, 256:512] * scp[:, 512:768]
        ya_ref[...] = (sc[:, 0:256] * _conv_taps(u, up, scw_ref, 3)).astype(BF16)
        qkv = qkv_ref[...]
        qkvo_ref[:, 0:256] = (qkv[:, 0:256].astype(F32) * 0.125).astype(BF16)
        qkvo_ref[:, 256:768] = qkv[:, 256:768].astype(BF16)
        xc = _conv_taps(xbc_ref[...].astype(F32), _prev8(xbcp_ref) * first, sw_ref, 4) + sb_ref[...]
        act_ref[...] = _silu(xc)

    return rows_call(
        name, body, t, tr,
        [(p, ("row", 768, OFF_SC // 768)), (p, ("prev8", 768, OFF_SC // 768)), (p, ("row", 768, OFF_QKV // 768)),
         (p, ("row", 768, OFF_XBC // 768)), (p, ("prev8", 768, OFF_XBC // 768)),
         (sc_w, "full"), (ssm_w, "full"), (ssm_b, "full")],
        [((t, 256), BF16, "row"), ((t, 768), BF16, "row"), ((t, 768), F32, "row")])


def branch_out_fwd(ya, yb, yc, p, w_cat, name, tr=256):
    t = p.shape[0]

    def body(ya_ref, yb_ref, yc_ref, gl_ref, w_ref, o_ref):
        y_a = jnp.dot(ya_ref[...], w_ref[0:256, :], preferred_element_type=F32)
        y_b = jnp.dot(yb_ref[...].astype(BF16), w_ref[256:512, :], preferred_element_type=F32)
        y_c = jnp.dot(yc_ref[...], w_ref[512:1024, :], preferred_element_type=F32)
        m = (_sigmoid(gl_ref[:, 0:1024].astype(F32)) * y_a + _sigmoid(gl_ref[:, 1024:2048].astype(F32)) * y_b
             + _sigmoid(gl_ref[:, 2048:3072].astype(F32)) * y_c)
        o_ref[...] = m.astype(BF16)

    return rows_call(name, body, t, tr,
                     [(ya, "row"), (yb, "row"), (yc, "row"), (p, ("row", 3072, 0)), (w_cat, "full")],
                     [((t, D_MODEL), BF16, "row")])[0]


def branch_out_bwd(dm, ya, yb, yc, p, w_cat, name, tr=256):
    t = p.shape[0]
    tn_dims = (((0,), (0,)), ((), ()))

    def body(dm_ref, ya_ref, yb_ref, yc_ref, gl_ref, w_ref, dgl_ref, dya_ref, dyb_ref, dyc_ref, dw_ref):
        @pl.when(pl.program_id(0) == 0)
        def _():
            dw_ref[...] = jnp.zeros_like(dw_ref)

        dmv = dm_ref[...].astype(F32)
        ins = (ya_ref[...], yb_ref[...].astype(BF16), yc_ref[...])
        rows = ((0, 256), (256, 512), (512, 1024))
        outs = (dya_ref, dyb_ref, dyc_ref)
        for i in range(3):
            r0, r1 = rows[i]
            y = jnp.dot(ins[i], w_ref[r0:r1, :], preferred_element_type=F32)
            s = _sigmoid(gl_ref[:, 1024 * i:1024 * (i + 1)].astype(F32))
            dgl_ref[:, 1024 * i:1024 * (i + 1)] = (dmv * y * s * (1.0 - s)).astype(BF16)
            dy = (dmv * s).astype(BF16)
            outs[i][...] = lax.dot_general(dy, w_ref[r0:r1, :], _NT, preferred_element_type=F32)
            dw_ref[r0:r1, :] += lax.dot_general(ins[i], dy, tn_dims, preferred_element_type=F32)

    return rows_call(name, body, t, tr,
                     [(dm, "row"), (ya, "row"), (yb, "row"), (yc, "row"), (p, ("row", 3072, 0)), (w_cat, "full")],
                     [((t, IN_PAD), BF16, ("row", 3072, 0)), ((t, 256), F32, "row"), ((t, 256), F32, "row"),
                      ((t, 512), F32, "row"), ((D_MODEL, D_MODEL), F32, "acc")])


def assemble_dp(dp, dya, p, sc_w, dq, dk, dv, dact, ssm_w, ssm_b, ddt, dz, name, tr=256):
    t = p.shape[0]
    n_tiles = t // tr
    sci, xi = OFF_SC // 768, OFF_XBC // 768
    base = OFF_SC
    assert base == IN_PAD - base
    o_sc, o_qkv, o_xbc, o_dt, o_z, o_end = (c - base for c in (OFF_SC, OFF_QKV, OFF_XBC, OFF_DT, OFF_Z, IN_PAD))

    def body(dp_ref, dya_ref, dyan_ref, sc_ref, scp_ref, scn_ref, scw_ref, dq_ref, dk_ref, dv_ref,
             dact_ref, dactn_ref, xbc_ref, xbcp_ref, xbcn_ref, sw_ref, sb_ref, ddt_ref, dz_ref,
             o_ref, dscw_ref, dsw_ref, dsb_ref):
        i = pl.program_id(0)

        @pl.when(i == 0)
        def _():
            dscw_ref[...] = jnp.zeros_like(dscw_ref)
            dsw_ref[...] = jnp.zeros_like(dsw_ref)
            dsb_ref[...] = jnp.zeros_like(dsb_ref)

        first = (i > 0).astype(F32)
        last = (i < n_tiles - 1).astype(F32)
        del dp_ref
        sc = sc_ref[...].astype(F32)
        scp = _prev8(scp_ref) * first
        scn = _next8(scn_ref) * last
        u = sc[:, 256:512] * sc[:, 512:768]
        up = scp[:, 256:512] * scp[:, 512:768]
        dya_v = dya_ref[...]
        cv = _conv_taps(u, up, scw_ref, 3)
        o_ref[:, o_sc:o_sc + 256] = (dya_v * cv).astype(BF16)
        dcv = dya_v * sc[:, 0:256]
        dcvn = _next8(dyan_ref) * last * scn[:, 0:256]
        du = None
        for k in range(3):
            sh = 2 - k
            term = scw_ref[k:k + 1, :] * _shift_up(dcv, dcvn, sh)
            du = term if du is None else du + term
            dscw_ref[k:k + 1, :] += jnp.sum(dcv * _shift_down(u, up, sh), axis=0, keepdims=True)
        o_ref[:, o_sc + 256:o_sc + 512] = (du * sc[:, 512:768]).astype(BF16)
        o_ref[:, o_sc + 512:o_sc + 768] = (du * sc[:, 256:512]).astype(BF16)
        o_ref[:, o_qkv:o_qkv + 256] = (dq_ref[...] * 0.125).astype(BF16)
        o_ref[:, o_qkv + 256:o_qkv + 512] = dk_ref[...].astype(BF16)
        o_ref[:, o_qkv + 512:o_qkv + 768] = dv_ref[...].astype(BF16)
        xb = xbc_ref[...].astype(F32)
        xbp = _prev8(xbcp_ref) * first
        xbn = _next8(xbcn_ref)
        xc = _conv_taps(xb, xbp, sw_ref, 4) + sb_ref[...]
        xcn = _conv_taps(xbn, xb[tr - 8:, :], sw_ref, 4) + sb_ref[...]
        dxc = dact_ref[...] * _dsilu(xc)
        dxcn = _next8(dactn_ref) * _dsilu(xcn) * last
        dxb = None
        for k in range(4):
            sh = 3 - k
            term = sw_ref[k:k + 1, :] * _shift_up(dxc, dxcn, sh)
            dxb = term if dxb is None else dxb + term
            dsw_ref[k:k + 1, :] += jnp.sum(dxc * _shift_down(xb, xbp, sh), axis=0, keepdims=True)
        dsb_ref[...] += jnp.sum(dxc, axis=0, keepdims=True)
        o_ref[:, o_xbc:o_xbc + 768] = dxb.astype(BF16)
        o_ref[:, o_dt:o_dt + 128] = ddt_ref[...].astype(BF16)
        o_ref[:, o_dt + 128:o_z] = jnp.zeros((tr, o_z - o_dt - 128), BF16)
        o_ref[:, o_z:o_end] = dz_ref[...].astype(BF16)

    return rows_call(
        name, body, t, tr,
        [(dp, "any"), (dya, "row"), (dya, ("next8", 256, 0)),
         (p, ("row", 768, sci)), (p, ("prev8", 768, sci)), (p, ("next8", 768, sci)), (sc_w, "full"),
         (dq, "row"), (dk, "row"), (dv, "row"),
         (dact, "row"), (dact, ("next8", 768, 0)),
         (p, ("row", 768, xi)), (p, ("prev8", 768, xi)), (p, ("next8", 768, xi)), (ssm_w, "full"), (ssm_b, "full"),
         (ddt, "row"), (dz, "row")],
        [((t, IN_PAD), BF16, ("row", IN_PAD - base, 1)), ((8, 256), F32, "acc"), ((8, 768), F32, "acc"),
         ((1, 768), F32, "acc")], aliases={0: 0})


def adamw_flat(slots, w, m, v, name, tr=512):
    n_slots, rows, lanes = slots.shape
    tr = max(d for d in range(8, min(tr, rows) + 1, 8) if rows % d == 0) if rows % 8 == 0 else rows
    bc1 = 1.0 - ADAM_B1 ** ADAM_STEP
    bc2 = 1.0 - ADAM_B2 ** ADAM_STEP

    def body(s_ref, w_ref, m_ref, v_ref, g_ref, d_ref, mo_ref, vo_ref):
        g = s_ref[0].astype(F32)
        for k in range(1, n_slots):
            g = g + s_ref[k].astype(F32)
        mn = ADAM_B1 * m_ref[...] + (1.0 - ADAM_B1) * g
        vn = ADAM_B2 * v_ref[...] + (1.0 - ADAM_B2) * (g * g)
        m_hat = mn / bc1
        v_hat = vn / bc2
        g_ref[...] = g
        d_ref[...] = -ADAM_LR * (m_hat / (jnp.sqrt(v_hat) + ADAM_EPS) + ADAM_WD * w_ref[...])
        mo_ref[...] = mn
        vo_ref[...] = vn

    tile = pl.BlockSpec((tr, lanes), lambda i: (i, 0))
    shp = jax.ShapeDtypeStruct((rows, lanes), F32)
    return pl.pallas_call(
        body, name=name, grid=(rows // tr,),
        in_specs=[pl.BlockSpec((n_slots, tr, lanes), lambda i: (0, i, 0)), tile, tile, tile],
        out_specs=[tile] * 4, out_shape=[shp] * 4,
        compiler_params=_cparams(("parallel",)),
    )(slots, w, m, v)


def _split_dot(x, tri):
    hi = x.astype(BF16)
    lo = (x - hi.astype(F32)).astype(BF16)
    return jnp.dot(hi, tri, preferred_element_type=F32) + jnp.dot(lo, tri, preferred_element_type=F32)


_NT = (((1,), (1,)), ((), ()))
_TN = (((0,), (0,)), ((), ()))

SBA_EXP_ZERO = -104.0
SBA_SKIPPED = -1e30


def sba_fwd(qkv, name, bq=256, bk=256):
    t = qkv.shape[0]
    ratio = bq // bk
    assert bq == ratio * bk and t // bk <= LANES

    def body(q_ref, k_ref, v_ref, o_ref, runs_ref, acc_s, run_s):
        i = pl.program_id(1)
        lane = lax.broadcasted_iota(jnp.int32, (1, LANES), 1)
        lane_q = lax.broadcasted_iota(jnp.int32, (bq, LANES), 1)
        qi = lax.broadcasted_iota(jnp.int32, (bq, bk), 0) + i * bq
        kj = lax.broadcasted_iota(jnp.int32, (bq, bk), 1)
        later = (lax.broadcasted_iota(jnp.int32, (bk, bk), 0) > lax.broadcasted_iota(jnp.int32, (bk, bk), 1)).astype(BF16)
        qv = q_ref[...]
        qms = [jnp.where(hm, qv, jnp.zeros_like(qv)) for hm in (lane < 64, lane >= 64)]
        acc_s[...] = jnp.zeros_like(acc_s)
        run_s[...] = jnp.zeros_like(run_s)
        runs_ref[...] = jnp.full(runs_ref.shape, SBA_SKIPPED, F32)

        def tiles(specs):
            chains = [(ti, hh) for ti in range(len(specs)) for hh in range(2)]
            kb = [k_ref[pl.ds(pl.multiple_of(j * bk, bk), bk), :] for j, _ in specs]
            vb = [v_ref[pl.ds(pl.multiple_of(j * bk, bk), bk), :] for j, _ in specs]
            mask = [(kj + j * bk) < qi if masked else None for j, masked in specs]
            s = {c: lax.dot_general(qms[c[1]], kb[c[0]], _NT, preferred_element_type=F32) for c in chains}
            lk = {c: _log_sigmoid_neg(s[c]) for c in chains}
            lk = {c: lk[c] if mask[c[0]] is None else jnp.where(mask[c[0]], lk[c], 0.0) for c in chains}
            w = {c: jnp.dot(lk[c].astype(BF16), later, preferred_element_type=F32) for c in chains}
            run = {}
            for hh in range(2):
                carry = run_s[hh]
                for ti in range(len(specs)):
                    run[ti, hh] = carry
                    carry = carry + jnp.sum(lk[ti, hh], axis=1, keepdims=True)
                run_s[hh] = carry
            a = {c: jnp.exp(s[c] + lk[c] + w[c] + run[c]) for c in chains}
            a = {c: a[c] if mask[c[0]] is None else jnp.where(mask[c[0]], a[c], 0.0) for c in chains}
            for ti, hh in chains:
                acc_s[hh] += jnp.dot(a[ti, hh].astype(BF16), vb[ti], preferred_element_type=F32)
                runs_ref[hh] = jnp.where(lane_q == specs[ti][0], run[ti, hh], runs_ref[hh])

        for d in range(ratio):
            tiles([((i + 1) * ratio - 1 - d, True)])

        def live():
            return jnp.maximum(jnp.max(run_s[0]), jnp.max(run_s[1])) >= SBA_EXP_ZERO

        def cond(state):
            n, go = state
            return jnp.logical_and(n < i * ratio, go)

        def step(state):
            n, _ = state
            tiles([(i * ratio - 1 - n, False)])
            return n + 1, live()

        lax.while_loop(cond, step, (jnp.int32(0), live()))
        o_ref[...] = jnp.where(lane < 64, acc_s[0], acc_s[1])

    return pl.pallas_call(
        body, name=name, grid=(2, t // bq),
        in_specs=[pl.BlockSpec((bq, LANES), lambda p, i: (i, p)), pl.BlockSpec((t, LANES), lambda p, i: (0, 2 + p)),
                  pl.BlockSpec((t, LANES), lambda p, i: (0, 4 + p))],
        out_specs=[pl.BlockSpec((bq, LANES), lambda p, i: (i, p)), pl.BlockSpec((2, bq, LANES), lambda p, i: (p, i, 0))],
        out_shape=[jax.ShapeDtypeStruct((t, SB_WIDTH), F32), jax.ShapeDtypeStruct((4, t, LANES), F32)],
        scratch_shapes=[pltpu.VMEM((2, bq, LANES), F32), pltpu.VMEM((2, bq, 1), F32)],
        compiler_params=_cparams(("parallel", "parallel")),
    )(qkv, qkv, qkv)


def sba_bwd(qkv, runs, do, name, bq=256, bk=256):
    t = qkv.shape[0]
    ratio = bq // bk
    assert bq == ratio * bk
    nq = t // bq

    def body(q_ref, k_ref, v_ref, runs_ref, do_ref, dq_ref, dk_hbm, dv_hbm, dk_s, dv_s, sem, dq_s, rg_s):
        p = pl.program_id(0)
        i = pl.program_id(1)

        @pl.when(i == 0)
        def _():
            dk_s[...] = jnp.zeros_like(dk_s)
            dv_s[...] = jnp.zeros_like(dv_s)

        lane = lax.broadcasted_iota(jnp.int32, (1, LANES), 1)
        qi = lax.broadcasted_iota(jnp.int32, (bq, bk), 0) + i * bq
        kj = lax.broadcasted_iota(jnp.int32, (bq, bk), 1)
        r2 = lax.broadcasted_iota(jnp.int32, (bk, bk), 0)
        c2 = lax.broadcasted_iota(jnp.int32, (bk, bk), 1)
        later = (r2 > c2).astype(BF16)
        earlier = (r2 < c2).astype(BF16)
        qv = q_ref[...]
        dov = do_ref[...]
        heads = range(2)
        hms = (lane < 64, lane >= 64)
        qms = [jnp.where(hm, qv, jnp.zeros_like(qv)) for hm in hms]
        doms = [jnp.where(hm, dov, 0.0).astype(BF16) for hm in hms]
        runs = [runs_ref[hh] for hh in heads]
        dq_s[...] = jnp.zeros_like(dq_s)
        rg_s[...] = jnp.zeros_like(rg_s)

        def tiles(specs):
            nt = len(specs)
            chains = [(ti, hh) for ti in range(nt) for hh in heads]
            starts = [pl.multiple_of(j * bk, bk) for j, _ in specs]
            kb = [k_ref[pl.ds(st, bk), :] for st in starts]
            vb = [v_ref[pl.ds(st, bk), :] for st in starts]
            mask = [(kj + j * bk) < qi if masked else None for j, masked in specs]
            s = {c: lax.dot_general(qms[c[1]], kb[c[0]], _NT, preferred_element_type=F32) for c in chains}
            da = {c: lax.dot_general(doms[c[1]], vb[c[0]], _NT, preferred_element_type=F32) for c in chains}
            lk_raw = {c: _log_sigmoid_neg(s[c]) for c in chains}
            lk = {c: lk_raw[c] if mask[c[0]] is None else jnp.where(mask[c[0]], lk_raw[c], 0.0) for c in chains}
            w = {c: jnp.dot(lk[c].astype(BF16), later, preferred_element_type=F32) for c in chains}
            run = {c: jnp.sum(jnp.where(lane == specs[c[0]][0], runs[c[1]], 0.0), axis=1, keepdims=True) for c in chains}
            a = {c: jnp.exp(s[c] + lk[c] + w[c] + run[c]) for c in chains}
            a = {c: a[c] if mask[c[0]] is None else jnp.where(mask[c[0]], a[c], 0.0) for c in chains}
            g = {c: a[c] * da[c] for c in chains}
            rg = {}
            for hh in heads:
                carry = rg_s[hh]
                for ti in range(nt):
                    rg[ti, hh] = carry
                    carry = carry + jnp.sum(g[ti, hh], axis=1, keepdims=True)
                rg_s[hh] = carry
            cpre = {c: rg[c] + _split_dot(g[c], earlier) for c in chains}
            dz = {c: g[c] - jnp.exp(s[c] + lk_raw[c]) * (g[c] + cpre[c]) for c in chains}
            dz = {c: (dz[c] if mask[c[0]] is None else jnp.where(mask[c[0]], dz[c], 0.0)).astype(BF16) for c in chains}
            for ti, hh in chains:
                dq_s[hh] += jnp.dot(dz[ti, hh], kb[ti], preferred_element_type=F32)
            for ti in range(nt):
                dk = lax.dot_general(dz[ti, 0], qms[0], _TN, preferred_element_type=F32)
                dk_s[pl.ds(starts[ti], bk), :] += dk + lax.dot_general(dz[ti, 1], qms[1], _TN, preferred_element_type=F32)
                dv = lax.dot_general(a[ti, 0].astype(BF16), doms[0], _TN, preferred_element_type=F32)
                dv_s[pl.ds(starts[ti], bk), :] += dv + lax.dot_general(a[ti, 1].astype(BF16), doms[1], _TN,
                                                                       preferred_element_type=F32)

        live = jnp.maximum(jnp.max(runs[0], axis=0, keepdims=True), jnp.max(runs[1], axis=0, keepdims=True)) >= SBA_EXP_ZERO
        first = jnp.minimum(jnp.min(jnp.where(live, lane, LANES)), i * ratio)

        def step(j, carry):
            tiles([(j, False)])
            return carry

        if ratio == 1:
            lax.fori_loop(first, i - 1, step, 0)

            @pl.when(i == 0)
            def _():
                tiles([(i, True)])

            @pl.when(i > 0)
            def _():
                tiles([(i - 1, False), (i, True)])
        else:
            lax.fori_loop(first, i * ratio, step, 0)
            for d in range(ratio):
                tiles([(i * ratio + d, True)])

        dq_ref[...] = jnp.where(lane < 64, dq_s[0], dq_s[1])

        @pl.when(i == nq - 1)
        def _():
            col = pl.multiple_of(p * LANES, LANES)
            ck = pltpu.make_async_copy(dk_s, dk_hbm.at[:, pl.ds(col, LANES)], sem.at[0])
            cv = pltpu.make_async_copy(dv_s, dv_hbm.at[:, pl.ds(col, LANES)], sem.at[1])
            ck.start()
            cv.start()
            ck.wait()
            cv.wait()

    shp = jax.ShapeDtypeStruct((t, SB_WIDTH), F32)
    tile = pl.BlockSpec((bq, LANES), lambda p, i: (i, p))
    return pl.pallas_call(
        body, name=name, grid=(2, nq),
        in_specs=[tile, pl.BlockSpec((t, LANES), lambda p, i: (0, 2 + p)), pl.BlockSpec((t, LANES), lambda p, i: (0, 4 + p)),
                  pl.BlockSpec((2, bq, LANES), lambda p, i: (p, i, 0)), tile],
        out_specs=[tile, pl.BlockSpec(memory_space=pl.ANY), pl.BlockSpec(memory_space=pl.ANY)],
        out_shape=[shp, shp, shp],
        scratch_shapes=[pltpu.VMEM((t, LANES), F32), pltpu.VMEM((t, LANES), F32), pltpu.SemaphoreType.DMA((2,)),
                        pltpu.VMEM((2, bq, LANES), F32), pltpu.VMEM((2, bq, 1), F32)],
        compiler_params=_cparams(("arbitrary", "arbitrary")),
    )(qkv, qkv, qkv, runs, do)


def _ssd_consts():
    ln = SSM_CHUNK
    ri = lax.broadcasted_iota(jnp.int32, (ln, ln), 0)
    ci = lax.broadcasted_iota(jnp.int32, (ln, ln), 1)
    eh = lax.broadcasted_iota(jnp.int32, (LANES, SSM_INNER), 0)
    el = lax.broadcasted_iota(jnp.int32, (LANES, SSM_INNER), 1)
    expand = (jnp.right_shift(el, 6) == eh).astype(BF16)
    th = lax.broadcasted_iota(jnp.int32, (SSM_INNER, LANES), 1)
    tl = lax.broadcasted_iota(jnp.int32, (SSM_INNER, LANES), 0)
    reduce = (jnp.right_shift(tl, 6) == th).astype(BF16)
    return ri, ci, expand, reduce


def _dot_f32(a, b):
    return jnp.dot(a, b, precision=HI, preferred_element_type=F32)


def _split3(x):
    hi = x.astype(BF16)
    r1 = x - hi.astype(F32)
    mid = r1.astype(BF16)
    lo = (r1 - mid.astype(F32)).astype(BF16)
    return hi, mid, lo


def _dot_hi(a, b):
    if a.dtype == BF16:
        return sum(jnp.dot(a, t, preferred_element_type=F32) for t in _split3(b))
    return sum(jnp.dot(t, b, preferred_element_type=F32) for t in _split3(a))


def _ssd_prelude(xbc_ref, dt_ref, dtt_ref, hpr_ref, hpc_ref, ri, ci, expand):
    ln = SSM_CHUNK
    xs = xbc_ref[:, 0:512]
    bm = xbc_ref[:, 512:640]
    cm = xbc_ref[:, 640:768]
    dtb_r = hpr_ref[0:1, :]
    aneg_r = -jnp.exp(hpr_ref[1:2, :])
    pre = dt_ref[...] + dtb_r
    dt = _softplus(pre)
    a = dt * aneg_r
    dtt = _softplus(dtt_ref[...] + hpc_ref[0:8, :])
    att = dtt * (-jnp.exp(hpc_ref[8:16, :]))
    tril = (ri >= ci).astype(BF16)
    triu = (ri <= ci).astype(BF16)
    acs = _dot_hi(tril, a)
    acst = _dot_hi(att, triu)
    acs_e = _dot_hi(acs, expand)
    dt_e = _dot_hi(dt, expand)
    last_e = acs_e[ln - 1:ln, :]
    e_e = jnp.exp(acs_e)
    w_e = jnp.exp(last_e - acs_e)
    dec_e = jnp.exp(last_e)
    xdt = xs * dt_e
    return dict(xs=xs, bm=bm, cm=cm, pre=pre, dt=dt, aneg_r=aneg_r, acs=acs, acst=acst, dt_e=dt_e, e_e=e_e,
                w_e=w_e, dec_e=dec_e, xdt=xdt, triu=triu)


def ssd_fwd(act, p, dt32, dtt, hp_rows, hp_cols, d_e, norm_w, name):
    t = act.shape[0]
    ln = SSM_CHUNK
    nc = t // ln

    def body(xbc_ref, dt_ref, z_ref, dtt_ref, hpr_ref, hpc_ref, d_ref, nw_ref, yc_ref, y_ref, sto_ref, st):
        @pl.when(pl.program_id(0) == 0)
        def _():
            st[...] = jnp.zeros_like(st)

        ri, ci, expand, _ = _ssd_consts()
        q = _ssd_prelude(xbc_ref, dt_ref, dtt_ref, hpr_ref, hpc_ref, ri, ci, expand)
        lane = lax.broadcasted_iota(jnp.int32, (1, LANES), 1)
        rown = lax.broadcasted_iota(jnp.int32, (LANES, 1), 0)
        low = lane < 64
        mask = ri >= ci
        xdt_b = q["xdt"].astype(BF16)
        xw_b = (q["xdt"] * q["w_e"]).astype(BF16)
        bt = q["bm"].T.astype(BF16)
        cb_ = q["cm"].astype(BF16)
        y_pairs = []
        for g in range(2):
            gm = low if g == 0 else jnp.logical_not(low)
            rm = (rown < 64) if g == 0 else (rown >= 64)
            cg = jnp.where(gm, cb_, jnp.zeros_like(cb_))
            cb = jnp.dot(cg, bt, preferred_element_type=F32)
            for pp in range(2):
                pi = 2 * g + pp
                sl = slice(LANES * pi, LANES * (pi + 1))
                xp = xdt_b[:, sl]
                yd = []
                for hh in range(2):
                    h = 2 * pi + hh
                    diff = q["acs"][:, h:h + 1] - q["acst"][h:h + 1, :]
                    lam = jnp.exp(jnp.where(mask, diff, -jnp.inf))
                    yd.append(jnp.dot((cb * lam).astype(BF16), xp, preferred_element_type=F32))
                sp = st[pi]
                sto_ref[0, pi] = sp
                yoff = jnp.dot(cg, sp.astype(BF16), preferred_element_type=F32) * q["e_e"][:, sl]
                upd = jnp.dot(bt, xw_b[:, sl], preferred_element_type=F32)
                st[pi] = q["dec_e"][:, sl] * sp + jnp.where(rm, upd, 0.0)
                y_pairs.append(jnp.where(low, yd[0], yd[1]) + yoff)
        y = jnp.concatenate(y_pairs, axis=1) + q["xs"] * d_ref[...]
        y_ref[...] = y
        yg = y * _silu(z_ref[...].astype(F32))
        for g in range(2):
            sl = slice(256 * g, 256 * (g + 1))
            seg = yg[:, sl]
            yc_ref[:, sl] = (seg * _rstd(seg) * nw_ref[:, sl]).astype(BF16)

    return pl.pallas_call(
        body, name=name, grid=(nc,),
        in_specs=[pl.BlockSpec((ln, 768), lambda c: (c, 0)), pl.BlockSpec((ln, LANES), lambda c: (c, 0)),
                  pl.BlockSpec((ln, 512), lambda c: (c, OFF_Z // 512)), pl.BlockSpec((8, ln), lambda c: (0, c)),
                  pl.BlockSpec((8, LANES), lambda c: (0, 0)), pl.BlockSpec((16, ln), lambda c: (0, 0)),
                  pl.BlockSpec((1, 512), lambda c: (0, 0)), pl.BlockSpec((1, 512), lambda c: (0, 0))],
        out_specs=[pl.BlockSpec((ln, 512), lambda c: (c, 0)), pl.BlockSpec((ln, 512), lambda c: (c, 0)),
                   pl.BlockSpec((1, 4, LANES, LANES), lambda c: (c, 0, 0, 0))],
        out_shape=[jax.ShapeDtypeStruct((t, 512), BF16), jax.ShapeDtypeStruct((t, 512), F32),
                   jax.ShapeDtypeStruct((nc, 4, LANES, LANES), F32)],
        scratch_shapes=[pltpu.VMEM((4, LANES, LANES), F32)],
        compiler_params=_cparams(("arbitrary",)),
    )(act, dt32, p, dtt, hp_rows, hp_cols, d_e, norm_w)


def ssd_bwd(dyc, y, act, p, dt32, dtt, states, hp_rows, hp_cols, d_e, norm_w, name):
    t = act.shape[0]
    ln = SSM_CHUNK
    nc = t // ln

    def body(dyc_ref, y_ref, xbc_ref, dt_ref, z_ref, dtt_ref, st_ref, hpr_ref, hpc_ref, d_ref, nw_ref,
             dz_ref, dact_ref, ddt_ref, dnw_ref, dd_ref, dhp_ref, ds):
        @pl.when(pl.program_id(0) == 0)
        def _():
            ds[...] = jnp.zeros_like(ds)
            dnw_ref[...] = jnp.zeros_like(dnw_ref)
            dd_ref[...] = jnp.zeros_like(dd_ref)
            dhp_ref[...] = jnp.zeros_like(dhp_ref)

        ri, ci, expand, reduce = _ssd_consts()
        q = _ssd_prelude(xbc_ref, dt_ref, dtt_ref, hpr_ref, hpc_ref, ri, ci, expand)
        lane = lax.broadcasted_iota(jnp.int32, (1, LANES), 1)
        rown = lax.broadcasted_iota(jnp.int32, (LANES, 1), 0)
        low = lane < 64
        mask = ri >= ci
        mask_t = ci >= ri
        xs, xdt, acs, acst = q["xs"], q["xdt"], q["acs"], q["acst"]
        yv, zv, nw = y_ref[...], z_ref[...].astype(F32), nw_ref[...]
        sg = _sigmoid(zv)
        zz = zv * sg
        yg = yv * zz
        dycv = dyc_ref[...]
        u = dycv * nw
        dyg_parts, dnw_parts = [], []
        for g in range(2):
            sl = slice(256 * g, 256 * (g + 1))
            seg = yg[:, sl]
            rr = _rstd(seg)
            nrm = seg * rr
            dyg_parts.append(rr * (u[:, sl] - nrm * jnp.mean(nrm * u[:, sl], axis=-1, keepdims=True)))
            dnw_parts.append(jnp.sum(dycv[:, sl] * nrm, axis=0, keepdims=True))
        dyg = jnp.concatenate(dyg_parts, axis=1)
        dnw_ref[...] += jnp.concatenate(dnw_parts, axis=1)
        dy = dyg * zz
        dz_ref[...] = dyg * yv * (sg * (1.0 + zv * (1.0 - sg)))
        dd_ref[...] += jnp.sum(dy * xs, axis=0, keepdims=True)
        dxs = dy * d_ref[...]
        dy_b = dy.astype(BF16)
        xdt_b = xdt.astype(BF16)
        xw_b = (xdt * q["w_e"]).astype(BF16)
        bt = q["bm"].T.astype(BF16)
        ct = q["cm"].T.astype(BF16)
        cb_ = q["cm"].astype(BF16)
        bb_ = q["bm"].astype(BF16)
        dacs = jnp.zeros((ln, LANES), F32)
        dc = jnp.zeros((ln, LANES), F32)
        db = jnp.zeros((ln, LANES), F32)
        dxdt_pairs, yoffdy_pairs, dwe_pairs, ddec_pairs = [], [], [], []
        for g in range(2):
            gm = low if g == 0 else jnp.logical_not(low)
            rm = (rown < 64) if g == 0 else (rown >= 64)
            cg = jnp.where(gm, cb_, jnp.zeros_like(cb_))
            bg = jnp.where(gm, bb_, jnp.zeros_like(bb_))
            cb = jnp.dot(cg, bt, preferred_element_type=F32)
            cbt = jnp.dot(bg, ct, preferred_element_type=F32)
            dcb = jnp.zeros((ln, ln), F32)
            dcbt = jnp.zeros((ln, ln), F32)
            for pp in range(2):
                pi = 2 * g + pp
                sl = slice(LANES * pi, LANES * (pi + 1))
                xp = xdt_b[:, sl]
                dyp = dy_b[:, sl]
                xpt = xdt[:, sl].T.astype(BF16)
                dypt = dy[:, sl].T.astype(BF16)
                dxdt_p = jnp.zeros((ln, LANES), F32)
                for hh in range(2):
                    h = 2 * pi + hh
                    hm = low if hh == 0 else jnp.logical_not(low)
                    col = acs[:, h:h + 1]
                    row = acst[h:h + 1, :]
                    lam = jnp.exp(jnp.where(mask, col - row, -jnp.inf))
                    lam_t = jnp.exp(jnp.where(mask_t, row - col, -jnp.inf))
                    m = cb * lam
                    m_t = cbt * lam_t
                    dyh = jnp.where(hm, dyp, jnp.zeros_like(dyp))
                    xh = jnp.where(hm, xp, jnp.zeros_like(xp))
                    dm = jnp.dot(dyh, xpt, preferred_element_type=F32)
                    dm_t = jnp.dot(xh, dypt, preferred_element_type=F32)
                    dcb = dcb + dm * lam
                    dcbt = dcbt + dm_t * lam_t
                    rs = jnp.sum(dm * m, axis=1, keepdims=True) - jnp.sum(dm_t * m_t, axis=1, keepdims=True)
                    dacs = dacs + jnp.where(lane == h, rs, 0.0)
                    dxdt_p = dxdt_p + jnp.dot(m_t.astype(BF16), dyh, preferred_element_type=F32)
                sp = st_ref[0, pi]
                sp_b = sp.astype(BF16)
                dsn = ds[pi]
                dsn_b = dsn.astype(BF16)
                e_p, w_p, dec_p = q["e_e"][:, sl], q["w_e"][:, sl], q["dec_e"][:, sl]
                yoff = jnp.dot(cg, sp_b, preferred_element_type=F32) * e_p
                dyo = dy[:, sl] * e_p
                dyo_b = dyo.astype(BF16)
                dc = dc + lax.dot_general(dyo_b, sp_b, _NT, preferred_element_type=F32)
                ds_prev = dec_p * dsn + jnp.where(rm, jnp.dot(ct, dyo_b, preferred_element_type=F32), 0.0)
                yoffdy_pairs.append(dy[:, sl] * yoff)
                dxw = jnp.dot(bg, dsn_b, preferred_element_type=F32)
                db = db + lax.dot_general(xw_b[:, sl], dsn_b, _NT, preferred_element_type=F32)
                dxdt_p = dxdt_p + dxw * w_p
                dwe_pairs.append(dxw * xdt[:, sl])
                ddec_pairs.append(jnp.sum(dsn * sp, axis=0, keepdims=True))
                ds[pi] = ds_prev
                dxdt_pairs.append(dxdt_p)
            dc = dc + jnp.dot(dcb.astype(BF16), bg, preferred_element_type=F32)
            db = db + jnp.dot(dcbt.astype(BF16), cg, preferred_element_type=F32)
        dxdt = jnp.concatenate(dxdt_pairs, axis=1)
        yoffdy = jnp.concatenate(yoffdy_pairs, axis=1)
        dwe = jnp.concatenate(dwe_pairs, axis=1)
        ddec_e = jnp.broadcast_to(jnp.concatenate(ddec_pairs, axis=1), (8, SSM_INNER))
        last = acs[ln - 1:ln, :]
        w_col = jnp.exp(last - acs)
        dw_col = _dot_hi(dwe, reduce) * w_col
        dacs = dacs + _dot_hi(yoffdy, reduce) - dw_col
        dlast = jnp.sum(dw_col, axis=0, keepdims=True) + jnp.exp(last) * _dot_hi(ddec_e, reduce)[0:1, :]
        rowi = lax.broadcasted_iota(jnp.int32, (ln, 1), 0)
        dacs = dacs + jnp.where(rowi == ln - 1, dlast, 0.0)
        da = _dot_hi(q["triu"], dacs)
        ddt = da * q["aneg_r"] + _dot_hi(dxdt * xs, reduce)
        ddt_raw = jnp.where(lane < SSM_HEADS, ddt * _sigmoid(q["pre"]), 0.0)
        ddt_ref[...] = ddt_raw
        dhp_ref[0:1, :] += jnp.sum(ddt_raw, axis=0, keepdims=True)
        dhp_ref[1:2, :] += jnp.where(lane < SSM_HEADS, jnp.sum(da * q["dt"], axis=0, keepdims=True) * q["aneg_r"], 0.0)
        dact_ref[:, 0:512] = dxs + dxdt * q["dt_e"]
        dact_ref[:, 512:640] = db
        dact_ref[:, 640:768] = dc

    rev = lambda c: nc - 1 - c
    return pl.pallas_call(
        body, name=name, grid=(nc,),
        in_specs=[pl.BlockSpec((ln, 512), lambda c: (rev(c), 0)), pl.BlockSpec((ln, 512), lambda c: (rev(c), 0)),
                  pl.BlockSpec((ln, 768), lambda c: (rev(c), 0)),
                  pl.BlockSpec((ln, LANES), lambda c: (rev(c), 0)),
                  pl.BlockSpec((ln, 512), lambda c: (rev(c), OFF_Z // 512)), pl.BlockSpec((8, ln), lambda c: (0, rev(c))),
                  pl.BlockSpec((1, 4, LANES, LANES), lambda c: (rev(c), 0, 0, 0)),
                  pl.BlockSpec((8, LANES), lambda c: (0, 0)), pl.BlockSpec((16, ln), lambda c: (0, 0)),
                  pl.BlockSpec((1, 512), lambda c: (0, 0)), pl.BlockSpec((1, 512), lambda c: (0, 0))],
        out_specs=[pl.BlockSpec((ln, 512), lambda c: (rev(c), 0)), pl.BlockSpec((ln, 768), lambda c: (rev(c), 0)),
                   pl.BlockSpec((ln, LANES), lambda c: (rev(c), 0)), pl.BlockSpec((1, 512), lambda c: (0, 0)),
                   pl.BlockSpec((1, 512), lambda c: (0, 0)), pl.BlockSpec((8, LANES), lambda c: (0, 0))],
        out_shape=[jax.ShapeDtypeStruct((t, 512), F32), jax.ShapeDtypeStruct((t, 768), F32),
                   jax.ShapeDtypeStruct((t, LANES), F32), jax.ShapeDtypeStruct((1, 512), F32),
                   jax.ShapeDtypeStruct((1, 512), F32), jax.ShapeDtypeStruct((8, LANES), F32)],
        scratch_shapes=[pltpu.VMEM((4, LANES, LANES), F32)],
        compiler_params=_cparams(("arbitrary",)),
    )(dyc, y, act, dt32, p, dtt, states, hp_rows, hp_cols, d_e, norm_w)


def mod_shard_fwd(c_all, mod_w, mod_b_shard, name):
    def body(c_ref, w_ref, b_ref, o_ref):
        sc = _silu(c_ref[...])
        for l in range(DEPTH):
            o_ref[l] = _dot_f32(sc, w_ref[l]) + b_ref[l]

    return pl.pallas_call(body, name=name, out_shape=jax.ShapeDtypeStruct((DEPTH, N_DEV, mod_w.shape[2]), F32),
                          compiler_params=_cparams())(c_all, mod_w, mod_b_shard)


def mod_w_grad(c_all, dmod_shard, name):
    def body(c_ref, d_ref, o_ref):
        sc = _silu(c_ref[...])
        for l in range(DEPTH):
            o_ref[l] = lax.dot_general(sc, d_ref[l], _TN, precision=HI, preferred_element_type=F32)

    return pl.pallas_call(body, name=name, out_shape=jax.ShapeDtypeStruct((DEPTH, D_MODEL, dmod_shard.shape[2]), F32),
                          compiler_params=_cparams())(c_all, dmod_shard)


_BIG = ("w_in", "sc_conv_w", "ssm_conv_w", "w_sc_out", "w_sb_out", "w_ssm_out", "w_o", "w_ffn_in", "w_ffn_out")
_ROW_SHARDED = ("w_o", "w_ffn_out")
_CONV = ("sc_conv_w", "ssm_conv_w")
_SMALL = ("mod_b", "g_pre_mix", "g_post_mix", "g_pre_ffn", "g_post_ffn", "ssm_conv_b", "ssm_dt_bias", "ssm_a_log",
          "ssm_d", "ssm_norm_w")
_WEIGHTS = ("mod_w", "mod_b", "g_pre_mix", "g_post_mix", "g_pre_ffn", "g_post_ffn", "w_in", "sc_conv_w", "ssm_conv_w",
            "ssm_conv_b", "ssm_dt_bias", "ssm_a_log", "ssm_d", "ssm_norm_w", "w_sc_out", "w_sb_out", "w_ssm_out", "w_o",
            "w_ffn_in", "w_ffn_out")


def _gathered_to_full(g, row_sharded):
    _, dep, r, c = g.shape
    if row_sharded:
        return g.transpose(1, 0, 2, 3).reshape(dep, N_DEV * r, c)
    return jnp.concatenate([g[d] for d in range(N_DEV)], axis=2)


def _full_to_slots(w, row_sharded):
    dep, r, c = w.shape
    if row_sharded:
        return w.reshape(dep, N_DEV, r // N_DEV, c).transpose(1, 0, 2, 3).reshape(N_DEV, dep * (r // N_DEV), c)
    return w.reshape(dep, r, N_DEV, c // N_DEV).transpose(2, 0, 1, 3).reshape(N_DEV, dep * r, c // N_DEV)


def _pad_in_proj(shards):
    width = shards.shape[2]

    def cols(lo, hi):
        out = []
        while lo < hi:
            dev, a = divmod(lo, width)
            b = min(width, a + hi - lo)
            out.append(shards[dev, :, a:b])
            lo += b - a
        return out

    pad = jnp.zeros((shards.shape[1], OFF_Z - OFF_DT - 8), shards.dtype)
    return jnp.concatenate(cols(2824, 5896) + cols(0, 768) + cols(768, 1536) + cols(2048, 2816) + cols(2816, 2824)
                           + [pad] + cols(1536, 2048), axis=1)


def _in_proj_slots(dw_layers):
    width = IN_PROJ // N_DEV
    segments = ((0, 768, OFF_SC), (768, 1536, OFF_QKV), (1536, 2048, OFF_Z), (2048, 2816, OFF_XBC),
                (2816, 2824, OFF_DT), (2824, IN_PROJ, OFF_GATES))

    def internal(lo, hi):
        out = []
        for s0, s1, off in segments:
            a, b = max(lo, s0), min(hi, s1)
            if a < b:
                out.append((off + a - s0, off + b - s0))
        return out

    slots = []
    for d in range(N_DEV):
        pieces = internal(width * d, width * (d + 1))
        slots.append(jnp.concatenate([jnp.concatenate([w[:, a:b] for a, b in pieces], axis=1) for w in dw_layers], axis=0))
    return jnp.stack(slots)


def _ffn_in_keep_give(halves, cidx):
    width = 2 * FFN_HIDDEN // N_DEV
    keep, give = [], []
    for chip in range(4):
        src = 0 if chip < 2 else 1
        for out, core in ((keep, cidx), (give, 1 - cidx)):
            col0 = width * ((2 * chip) % 4 + core)
            out.append(jnp.concatenate([lax.dynamic_slice_in_dim(h[src], col0, width, axis=1) for h in halves], axis=0))
    return jnp.stack(keep), jnp.stack(give)


def _row(v):
    return v.reshape(1, -1)


def _local_step(x, target, mod, small, conv, big):
    lw, saved = [], []
    for l in range(DEPTH):
        w_in_p = _pad_in_proj(big["w_in_shards"][:, l])
        w_cat = jnp.concatenate([big["w_sc_out"][l], big["w_sb_out"][l], big["w_ssm_out"][l]], axis=0)
        hp_rows = jnp.zeros((8, LANES), F32).at[0, :SSM_HEADS].set(small["ssm_dt_bias"][l]).at[1, :SSM_HEADS].set(
            small["ssm_a_log"][l])
        hp_cols = jnp.concatenate([jnp.broadcast_to(small["ssm_dt_bias"][l][:, None], (SSM_HEADS, SSM_CHUNK)),
                                   jnp.broadcast_to(small["ssm_a_log"][l][:, None], (SSM_HEADS, SSM_CHUNK))], axis=0)
        lw.append(dict(
            w_in_p=w_in_p, w_cat=w_cat, w_o=big["w_o"][l], w_ffn_in=big["w_ffn_in"][l], w_ffn_out=big["w_ffn_out"][l],
            sc_w8=jnp.pad(conv["sc_conv_w"][l], ((0, 5), (0, 0))), ssm_w8=jnp.pad(conv["ssm_conv_w"][l], ((0, 4), (0, 0))),
            ssm_b=_row(small["ssm_conv_b"][l]), hp_rows=hp_rows, hp_cols=hp_cols,
            d_e=_row(jnp.repeat(small["ssm_d"][l], SSM_HEAD_DIM)), norm_w=_row(small["ssm_norm_w"][l]),
            g_pre_mix=_row(small["g_pre_mix"][l]), g_post_mix=_row(small["g_post_mix"][l]),
            g_pre_ffn=_row(small["g_pre_ffn"][l]), g_post_ffn=_row(small["g_post_ffn"][l]),
            shift1=mod[l, 0:1], scale1=mod[l, 1:2], gate1=mod[l, 2:3], shift2=mod[l, 3:4], scale2=mod[l, 4:5],
            gate2=mod[l, 5:6]))

    xl = x
    h = normmod_fwd(xl, lw[0]["g_pre_mix"], lw[0]["scale1"], lw[0]["shift1"], "normmod_fwd_0")
    dy = loss = None
    for l in range(DEPTH):
        w = lw[l]
        p = mm_nn([h], [w["w_in_p"]], BF16, f"in_proj_{l}")
        dt32 = mm_nn([h], [w["w_in_p"][:, OFF_DT:OFF_DT + LANES]], F32, f"in_proj_dt_{l}")
        ya, qkv, act = post_inproj(p, w["sc_w8"], w["ssm_w8"], w["ssm_b"], f"post_inproj_{l}")
        o, runs = sba_fwd(qkv, f"sba_fwd_{l}")
        dtt = dt32[:, :SSM_HEADS].T
        yc, ypre, states = ssd_fwd(act, p, dt32, dtt, w["hp_rows"], w["hp_cols"], w["d_e"], w["norm_w"], f"ssd_fwd_{l}")
        merged = branch_out_fwd(ya, o, yc, p, w["w_cat"], f"branch_fwd_{l}")
        mix = mm_nn([merged], [w["w_o"]], F32, f"out_proj_{l}")
        x1, h2 = resid_normmod_fwd(xl, mix, w["gate1"], w["g_post_mix"], w["g_pre_ffn"], w["scale2"], w["shift2"],
                                   f"resid_mix_{l}")
        gt, up, a = mm_swiglu_fwd(h2, w["w_ffn_in"], f"ffn_in_{l}")
        f = mm_nn([a], [w["w_ffn_out"]], F32, f"ffn_out_{l}")
        saved.append(dict(x=xl, h=h, p=p, ya=ya, qkv=qkv, act=act, o=o, runs=runs, dt32=dt32, dtt=dtt, yc=yc, ypre=ypre, states=states,
                          merged=merged, mix=mix, x1=x1, h2=h2, gt=gt, up=up, a=a, f=f))
        if l + 1 < DEPTH:
            nw = lw[l + 1]
            xl, h = resid_normmod_fwd(x1, f, w["gate2"], w["g_post_ffn"], nw["g_pre_mix"], nw["scale1"], nw["shift1"],
                                      f"resid_ffn_{l}")
        else:
            dy, loss = resid_loss(x1, f, w["gate2"], w["g_post_ffn"], target, "resid_loss")

    dmod = [None] * DEPTH
    gs = {k: [None] * DEPTH for k in _SMALL + _BIG}
    dxo = dy
    top, stl = lw[DEPTH - 1], saved[DEPTH - 1]
    df, dgate2, gs["g_post_ffn"][DEPTH - 1] = resid_bwd(dxo, stl["f"], top["gate2"], top["g_post_ffn"],
                                                        f"resid_ffn_bwd_{DEPTH - 1}")
    for l in reversed(range(DEPTH)):
        w, s = lw[l], saved[l]
        dgt, dup = mm_swiglu_bwd(df, w["w_ffn_out"], s["gt"], s["up"], f"ffn_out_bwd_{l}")
        gs["w_ffn_out"][l] = mm_tn(s["a"], df, f"dw_ffn_out_{l}")
        dh2 = mm_nt([dgt, dup], [w["w_ffn_in"], w["w_ffn_in"]], [0, FFN_HIDDEN], BF16, f"ffn_in_bwd_{l}")
        gs["w_ffn_in"][l] = (mm_tn(s["h2"], dgt, f"dw_ffn_gate_{l}"), mm_tn(s["h2"], dup, f"dw_ffn_up_{l}"))
        dx1, dmix, dscale2, dshift2, gs["g_pre_ffn"][l], dgate1, gs["g_post_mix"][l] = normmod_resid_bwd(
            dh2, s["x1"], dxo, w["g_pre_ffn"], w["scale2"], s["mix"], w["gate1"], w["g_post_mix"], f"ffn_mix_bwd_{l}")
        dmerged = mm_nt([dmix], [w["w_o"]], [0], BF16, f"out_proj_bwd_{l}")
        gs["w_o"][l] = mm_tn(s["merged"], dmix, f"dw_o_{l}")
        dp_gates, dya, dyb, dyc, dw_cat = branch_out_bwd(dmerged, s["ya"], s["o"], s["yc"], s["p"], w["w_cat"],
                                                         f"branch_bwd_{l}")
        gs["w_sc_out"][l], gs["w_sb_out"][l], gs["w_ssm_out"][l] = dw_cat[0:256], dw_cat[256:512], dw_cat[512:1024]
        dz, dact, ddt, dnw, dd_e, dhp = ssd_bwd(dyc, s["ypre"], s["act"], s["p"], s["dt32"], s["dtt"], s["states"], w["hp_rows"],
                                                w["hp_cols"], w["d_e"], w["norm_w"], f"ssd_bwd_{l}")
        gs["ssm_norm_w"][l] = dnw[0]
        gs["ssm_d"][l] = dd_e.reshape(SSM_HEADS, SSM_HEAD_DIM).sum(axis=1)
        gs["ssm_dt_bias"][l] = dhp[0, :SSM_HEADS]
        gs["ssm_a_log"][l] = dhp[1, :SSM_HEADS]
        dq, dk, dv = sba_bwd(s["qkv"], s["runs"], dyb, f"sba_bwd_{l}")
        dp, dscw, dssw, dssb = assemble_dp(dp_gates, dya, s["p"], w["sc_w8"], dq, dk, dv, dact, w["ssm_w8"], w["ssm_b"], ddt,
                                           dz, f"assemble_dp_{l}")
        gs["sc_conv_w"][l], gs["ssm_conv_w"][l], gs["ssm_conv_b"][l] = dscw[0:3], dssw[0:4], dssb[0]
        dh = mm_nt([dp], [w["w_in_p"]], [0], BF16, f"in_proj_bwd_{l}")
        gs["w_in"][l] = mm_tn(s["h"], dp, f"dw_in_{l}")
        if l > 0:
            below, sb = lw[l - 1], saved[l - 1]
            dxo, df, dscale1, dshift1, gs["g_pre_mix"][l], dgate2_below, gs["g_post_ffn"][l - 1] = normmod_resid_bwd(
                dh, s["x"], dx1, w["g_pre_mix"], w["scale1"], sb["f"], below["gate2"], below["g_post_ffn"],
                f"mix_ffn_bwd_{l}")
        else:
            dxo, dscale1, dshift1, gs["g_pre_mix"][l] = normmod_bwd(dh, s["x"], dx1, w["g_pre_mix"], w["scale1"],
                                                                    f"normmod_mix_bwd_{l}")
            dgate2_below = None
        dmod[l] = jnp.concatenate([dshift1, dscale1, dgate1, dshift2, dscale2, dgate2], axis=0)
        dgate2 = dgate2_below
    for k in ("g_pre_mix", "g_post_mix", "g_pre_ffn", "g_post_ffn"):
        gs[k] = [g[0] for g in gs[k]]
    per_layer = ("w_in", "w_ffn_in")
    grads = {k: (v if k in per_layer else jnp.stack(v)) for k, v in gs.items() if k != "mod_b"}
    return loss[0, 0], dxo, jnp.stack(dmod), grads


def kernel(x, c, mod_w, mod_b, g_pre_mix, g_post_mix, g_pre_ffn, g_post_ffn, w_in, sc_conv_w, ssm_conv_w, ssm_conv_b, ssm_dt_bias, ssm_a_log, ssm_d, ssm_norm_w, w_sc_out, w_sb_out, w_ssm_out, w_o, w_ffn_in, w_ffn_out, loss_target, m_mod_w, m_mod_b, m_g_pre_mix, m_g_post_mix, m_g_pre_ffn, m_g_post_ffn, m_w_in, m_sc_conv_w, m_ssm_conv_w, m_ssm_conv_b, m_ssm_dt_bias, m_ssm_a_log, m_ssm_d, m_ssm_norm_w, m_w_sc_out, m_w_sb_out, m_w_ssm_out, m_w_o, m_w_ffn_in, m_w_ffn_out, v_mod_w, v_mod_b, v_g_pre_mix, v_g_post_mix, v_g_pre_ffn, v_g_post_ffn, v_w_in, v_sc_conv_w, v_ssm_conv_w, v_ssm_conv_b, v_ssm_dt_bias, v_ssm_a_log, v_ssm_d, v_ssm_norm_w, v_w_sc_out, v_w_sb_out, v_w_ssm_out, v_w_o, v_w_ffn_in, v_w_ffn_out):
    wts = dict(mod_w=mod_w, mod_b=mod_b, g_pre_mix=g_pre_mix, g_post_mix=g_post_mix, g_pre_ffn=g_pre_ffn,
               g_post_ffn=g_post_ffn, w_in=w_in, sc_conv_w=sc_conv_w, ssm_conv_w=ssm_conv_w, ssm_conv_b=ssm_conv_b,
               ssm_dt_bias=ssm_dt_bias, ssm_a_log=ssm_a_log, ssm_d=ssm_d, ssm_norm_w=ssm_norm_w, w_sc_out=w_sc_out,
               w_sb_out=w_sb_out, w_ssm_out=w_ssm_out, w_o=w_o, w_ffn_in=w_ffn_in, w_ffn_out=w_ffn_out)
    ms = dict(mod_w=m_mod_w, mod_b=m_mod_b, g_pre_mix=m_g_pre_mix, g_post_mix=m_g_post_mix, g_pre_ffn=m_g_pre_ffn,
              g_post_ffn=m_g_post_ffn, w_in=m_w_in, sc_conv_w=m_sc_conv_w, ssm_conv_w=m_ssm_conv_w,
              ssm_conv_b=m_ssm_conv_b, ssm_dt_bias=m_ssm_dt_bias, ssm_a_log=m_ssm_a_log, ssm_d=m_ssm_d,
              ssm_norm_w=m_ssm_norm_w, w_sc_out=m_w_sc_out, w_sb_out=m_w_sb_out, w_ssm_out=m_w_ssm_out, w_o=m_w_o,
              w_ffn_in=m_w_ffn_in, w_ffn_out=m_w_ffn_out)
    vs = dict(mod_w=v_mod_w, mod_b=v_mod_b, g_pre_mix=v_g_pre_mix, g_post_mix=v_g_post_mix, g_pre_ffn=v_g_pre_ffn,
              g_post_ffn=v_g_post_ffn, w_in=v_w_in, sc_conv_w=v_sc_conv_w, ssm_conv_w=v_ssm_conv_w,
              ssm_conv_b=v_ssm_conv_b, ssm_dt_bias=v_ssm_dt_bias, ssm_a_log=v_ssm_a_log, ssm_d=v_ssm_d,
              ssm_norm_w=v_ssm_norm_w, w_sc_out=v_w_sc_out, w_sb_out=v_w_sb_out, w_ssm_out=v_w_ssm_out, w_o=v_w_o,
              w_ffn_in=v_w_ffn_in, w_ffn_out=v_w_ffn_out)
    me = 4 * lax.axis_index("x") + 2 * lax.axis_index("y") + lax.axis_index("c")
    mod_cols = mod_w.shape[2]

    pack1, sizes1 = _pack_rows([c, sc_conv_w, ssm_conv_w], F32, 8)
    got1 = all_gather_multi([pack1], "gather_c_conv")[0].reshape(N_DEV, -1)
    c_all, sc_g, ssm_g = _unpack(got1, sizes1, [(D_MODEL,), sc_conv_w.shape, ssm_conv_w.shape])
    conv = dict(sc_conv_w=_gathered_to_full(sc_g, False), ssm_conv_w=_gathered_to_full(ssm_g, False))

    mod_b_shard = lax.dynamic_slice_in_dim(mod_b, me * mod_cols, mod_cols, axis=1).reshape(DEPTH, 1, mod_cols)
    mod_sh = mod_shard_fwd(c_all, mod_w, mod_b_shard, "mod_shard_fwd")
    pack2, sizes2 = _pack_rows([mod_sh], F32, 8)
    got2 = all_gather_multi([pack2], "gather_mod")[0].reshape(N_DEV, -1)
    mod_all = _unpack(got2, sizes2, [mod_sh.shape])[0]
    mod_mine = lax.dynamic_index_in_dim(mod_all, me, axis=2, keepdims=False)
    mod = mod_mine.transpose(1, 0, 2).reshape(DEPTH, 6, D_MODEL)

    mm_names = [k for k in _BIG if k not in _CONV]
    gathered = all_gather_multi([wts[k].astype(BF16) for k in mm_names], "gather_weights")
    big = {k: _gathered_to_full(g, k in _ROW_SHARDED) for k, g in zip(mm_names, gathered) if k != "w_in"}
    big["w_in_shards"] = gathered[mm_names.index("w_in")]

    small = {k: wts[k] for k in _SMALL}
    loss_part, dx, dmod, grads = _local_step(x[0], loss_target[0], mod, small, conv, big)
    loss = lax.psum(loss_part, ("x", "y", "c"))

    small_parts = [dmod.reshape(DEPTH, 6 * D_MODEL)] + [grads[k] for k in _SMALL[1:]]
    pack5, sizes5 = _pack_rows(small_parts, F32, 8)
    pack_conv, sizes_conv = _pack_rows([grads[k] for k in _CONV], F32, 8)
    got5, got_conv = all_gather_multi([pack5, pack_conv], "gather_small_grads")
    w5, _ = _pack_rows([wts[k] for k in _SMALL], F32, 8)
    m5, _ = _pack_rows([ms[k] for k in _SMALL], F32, 8)
    v5, _ = _pack_rows([vs[k] for k in _SMALL], F32, 8)
    res5 = adamw_flat(got5, w5, m5, v5, "adamw_small")
    small_out = [_unpack(r.reshape(-1), sizes5, [wts[k].shape for k in _SMALL]) for r in res5]

    conv_full = _unpack(got_conv.reshape(N_DEV, -1), sizes_conv, [grads[k].shape for k in _CONV])
    conv_mine = [lax.dynamic_slice_in_dim(g, me * wts[k].shape[2], wts[k].shape[2], axis=3)
                 for k, g in zip(_CONV, conv_full)]
    conv_slot_sizes = [math.prod(wts[k].shape) for k in _CONV]
    conv_slots = jnp.concatenate([g.reshape(N_DEV, -1) for g in conv_mine], axis=1)
    pad_c = -conv_slots.shape[1] % (8 * LANES)
    conv_slots = jnp.pad(conv_slots, ((0, 0), (0, pad_c))).reshape(N_DEV, -1, LANES)
    wc, _ = _pack_rows([wts[k] for k in _CONV], F32, 8)
    mc, _ = _pack_rows([ms[k] for k in _CONV], F32, 8)
    vc, _ = _pack_rows([vs[k] for k in _CONV], F32, 8)
    res_c = adamw_flat(conv_slots, wc, mc, vc, "adamw_conv")
    conv_out = [_unpack(r.reshape(-1), conv_slot_sizes, [wts[k].shape for k in _CONV]) for r in res_c]

    dmod_all = got5.reshape(N_DEV, -1)[:, :DEPTH * 6 * D_MODEL].reshape(N_DEV, DEPTH, 6 * D_MODEL)
    dmod_shard = lax.dynamic_slice_in_dim(dmod_all, me * mod_cols, mod_cols, axis=2).transpose(1, 0, 2)
    g_mod_w = mod_w_grad(c_all, dmod_shard, "mod_w_grad")
    rows2 = lambda a: a.reshape(a.shape[0] * a.shape[1], a.shape[2])
    res_mw = adamw_flat(rows2(g_mod_w)[None], rows2(mod_w), rows2(m_mod_w), rows2(v_mod_w), "adamw_mod_w")
    mod_w_out = [r.reshape(mod_w.shape) for r in res_mw]

    cidx = lax.axis_index("c")
    keeps, gives = [], []
    for k in mm_names:
        if k == "w_ffn_in":
            keep, give = _ffn_in_keep_give(grads[k], cidx)
        else:
            slots = (_in_proj_slots(grads[k]) if k == "w_in"
                     else _full_to_slots(grads[k].astype(BF16), k in _ROW_SHARDED))
            by_chip = slots.reshape(4, 2, *slots.shape[1:])
            keep = lax.dynamic_index_in_dim(by_chip, cidx, 1, keepdims=False)
            give = lax.dynamic_index_in_dim(by_chip, 1 - cidx, 1, keepdims=False)
        keeps.append(keep)
        gives.append(give)
    gots = swap_with_sibling(gives, "swap_grads")
    pairs = []
    for k, keep, got in zip(mm_names, keeps, gots):
        rows4 = (4 * keep.shape[1], keep.shape[2])
        pairs.append(add_pairs(keep.reshape(rows4), got.reshape(rows4), f"add_pairs_{k}").reshape(keep.shape))
    recvs = exchange_chips(pairs, "exchange_grads")
    big_out = {}
    for k, recv in zip(mm_names, recvs):
        res = adamw_flat(recv, rows2(wts[k]), rows2(ms[k]), rows2(vs[k]), f"adamw_{k}")
        big_out[k] = [r.reshape(wts[k].shape) for r in res]

    outs = []
    for kind in range(4):
        by_name = {"mod_w": mod_w_out[kind]}
        by_name.update(zip(_SMALL, small_out[kind]))
        by_name.update(zip(_CONV, conv_out[kind]))
        by_name.update({k: v[kind] for k, v in big_out.items()})
        outs.extend(by_name[k] for k in _WEIGHTS)
    return (loss, dx[None], *outs)
```

```python
import math

import jax
import jax.numpy as jnp
from jax import lax
from jax.experimental import pallas as pl
from jax.experimental.pallas import tpu as pltpu

F32 = jnp.float32
BF16 = jnp.bfloat16
HI = lax.Precision.HIGHEST

N_DEV = 8
D_MODEL = 1024
DEPTH = 2
SC_WIDTH = 256
SB_WIDTH = 256
SB_HEAD_DIM = 64
SSM_INNER = 512
SSM_HEADS = 8
SSM_HEAD_DIM = 64
SSM_GROUPS = 2
SSM_STATE = 64
SSM_CHUNK = 256
SSM_CONV_DIM = 768
FFN_HIDDEN = 2816
NORM_EPS = 1e-6
IN_PROJ = 5896
LANES = 128
VMEM_LIMIT = 56 * 1024 * 1024

OFF_GATES = 0
OFF_SC = 3072
OFF_QKV = 3840
OFF_XBC = 4608
OFF_DT = 5376
OFF_Z = 5632
IN_PAD = 6144

ADAM_LR = 0.001
ADAM_B1 = 0.9
ADAM_B2 = 0.999
ADAM_EPS = 1e-08
ADAM_WD = 0.01
ADAM_STEP = 10

MESH_ID = pl.DeviceIdType.MESH


def _cparams(sem=None):
    return pltpu.CompilerParams(dimension_semantics=sem, vmem_limit_bytes=VMEM_LIMIT)


def _my_pos():
    return lax.axis_index("x"), lax.axis_index("y"), lax.axis_index("c")


def all_gather_multi(blocks, name):
    n = len(blocks)

    def body(*refs):
        x_refs, o_refs = refs[:n], refs[n:2 * n]
        send_sems, recv_sems, local_sems = refs[2 * n:]
        x, y, c = _my_pos()
        me, sibling = (x, y, c), (x, y, 1 - c)
        chips = [(1 - x, y), (x, 1 - y), (1 - x, 1 - y)]

        def slot(a, px, py, pc):
            return o_refs[a].at[4 * px + 2 * py + pc]

        def copy(a, k, blk, to, src=None):
            return pltpu.make_async_remote_copy(
                src_ref=slot(a, *blk) if src is None else src, dst_ref=slot(a, *blk),
                send_sem=send_sems.at[7 * a + k], recv_sem=recv_sems.at[7 * a + k], device_id=to, device_id_type=MESH_ID)

        mine = [pltpu.make_async_copy(x_refs[a], slot(a, *me), local_sems.at[a]) for a in range(n)]
        for cp in mine:
            cp.start()
        first = [copy(a, 1 + j, me, (*chip, c), src=x_refs[a]) for j, chip in enumerate(chips) for a in range(n)]
        first += [copy(a, 0, me, sibling, src=x_refs[a]) for a in range(n)]
        for cp in first:
            cp.start()
        passed = []
        for j, chip in enumerate(chips):
            for a in range(n):
                copy(a, 1 + j, (*chip, c), me).wait_recv()
                fwd = copy(a, 4 + j, (*chip, c), sibling)
                fwd.start()
                passed.append(fwd)
        for a in range(n):
            copy(a, 0, sibling, me).wait_recv()
            for j, chip in enumerate(chips):
                copy(a, 4 + j, (*chip, 1 - c), me).wait_recv()
        for cp in first + passed:
            cp.wait_send()
        for cp in mine:
            cp.wait()

    any_spec = pl.BlockSpec(memory_space=pl.ANY)
    return pl.pallas_call(
        body, name=name,
        out_shape=[jax.ShapeDtypeStruct((N_DEV,) + b.shape, b.dtype) for b in blocks],
        in_specs=[any_spec] * n, out_specs=[any_spec] * n,
        scratch_shapes=[pltpu.SemaphoreType.DMA((7 * n,)), pltpu.SemaphoreType.DMA((7 * n,)), pltpu.SemaphoreType.DMA((n,))],
    )(*blocks)


def swap_with_sibling(gives, name):
    n = len(gives)

    def body(*refs):
        g_refs, r_refs = refs[:n], refs[n:2 * n]
        send_sems, recv_sems = refs[2 * n:]
        x, y, c = _my_pos()
        copies = [pltpu.make_async_remote_copy(
            src_ref=g_refs[a], dst_ref=r_refs[a], send_sem=send_sems.at[a], recv_sem=recv_sems.at[a],
            device_id=(x, y, 1 - c), device_id_type=MESH_ID) for a in range(n)]
        for cp in copies:
            cp.start()
        for cp in copies:
            cp.wait_recv()
        for cp in copies:
            cp.wait_send()

    any_spec = pl.BlockSpec(memory_space=pl.ANY)
    return pl.pallas_call(
        body, name=name, out_shape=[jax.ShapeDtypeStruct(g.shape, g.dtype) for g in gives],
        in_specs=[any_spec] * n, out_specs=[any_spec] * n,
        scratch_shapes=[pltpu.SemaphoreType.DMA((n,)), pltpu.SemaphoreType.DMA((n,))],
    )(*gives)


def exchange_chips(sends, name):
    n = len(sends)

    def body(*refs):
        s_refs, r_refs = refs[:n], refs[n:2 * n]
        send_sems, recv_sems, local_sems = refs[2 * n:]
        x, y, c = _my_pos()
        me = 2 * x + y
        mine = [pltpu.make_async_copy(s_refs[a].at[me], r_refs[a].at[me], local_sems.at[a]) for a in range(n)]
        for cp in mine:
            cp.start()
        copies = []
        for k in (2, 1, 3):
            px, py = x ^ (k >> 1), y ^ (k & 1)
            for a in range(n):
                cp = pltpu.make_async_remote_copy(
                    src_ref=s_refs[a].at[2 * px + py], dst_ref=r_refs[a].at[me],
                    send_sem=send_sems.at[3 * a + k - 1], recv_sem=recv_sems.at[3 * a + k - 1],
                    device_id=(px, py, c), device_id_type=MESH_ID)
                cp.start()
                copies.append(cp)
        for cp in copies:
            cp.wait_recv()
        for cp in copies:
            cp.wait_send()
        for cp in mine:
            cp.wait()

    any_spec = pl.BlockSpec(memory_space=pl.ANY)
    return pl.pallas_call(
        body, name=name, out_shape=[jax.ShapeDtypeStruct(s.shape, s.dtype) for s in sends],
        in_specs=[any_spec] * n, out_specs=[any_spec] * n,
        scratch_shapes=[pltpu.SemaphoreType.DMA((3 * n,)), pltpu.SemaphoreType.DMA((3 * n,)), pltpu.SemaphoreType.DMA((n,))],
    )(*sends)


def add_pairs(a, b, name, tr=512):
    rows, cols = a.shape
    tr = max(d for d in range(16, min(tr, rows) + 1, 16) if rows % d == 0)

    def body(a_ref, b_ref, o_ref):
        o_ref[...] = (a_ref[...].astype(F32) + b_ref[...].astype(F32)).astype(BF16)

    tile = pl.BlockSpec((tr, cols), lambda i: (i, 0))
    return pl.pallas_call(body, name=name, grid=(rows // tr,), in_specs=[tile, tile], out_specs=tile,
                          out_shape=jax.ShapeDtypeStruct((rows, cols), BF16), compiler_params=_cparams(("parallel",)))(a, b)


def _pack_rows(parts, dtype, row_multiple):
    flat = [p.astype(dtype).reshape(-1) for p in parts]
    sizes = [f.shape[0] for f in flat]
    total = sum(sizes)
    quantum = LANES * row_multiple
    padded = -(-total // quantum) * quantum
    if padded > total:
        flat.append(jnp.zeros((padded - total,), dtype))
    return jnp.concatenate(flat).reshape(padded // LANES, LANES), sizes


def _unpack(flat, sizes, shapes):
    out, off = [], 0
    lead = flat.shape[:-1]
    for n, shp in zip(sizes, shapes):
        out.append(flat[..., off:off + n].reshape(lead + tuple(shp)))
        off += n
    return out


def rows_call(name, body, n_rows, tr, ins, outs, scratch=(), aliases=None):
    n_tiles = n_rows // tr
    assert n_tiles * tr == n_rows
    in_specs, arrays = [], []
    for arr, kind in ins:
        arrays.append(arr)
        if kind == "row":
            in_specs.append(pl.BlockSpec((tr, arr.shape[1]), lambda i: (i, 0)))
        elif kind == "any":
            in_specs.append(pl.BlockSpec(memory_space=pl.ANY))
        elif kind == "full":
            in_specs.append(pl.BlockSpec(arr.shape, lambda i, nd=arr.ndim: (0,) * nd))
        elif kind[0] == "row":
            _, w, ci = kind
            in_specs.append(pl.BlockSpec((tr, w), lambda i, ci=ci: (i, ci)))
        elif kind[0] == "prev8":
            _, w, ci = kind
            hr = 8 * (4 // arr.dtype.itemsize)
            in_specs.append(pl.BlockSpec((hr, w), lambda i, ci=ci, hr=hr: (jnp.maximum(i * (tr // hr) - 1, 0), ci)))
        elif kind[0] == "next8":
            _, w, ci = kind
            hr = 8 * (4 // arr.dtype.itemsize)
            last = n_rows // hr - 1
            in_specs.append(pl.BlockSpec((hr, w), lambda i, ci=ci, last=last, hr=hr: (jnp.minimum((i + 1) * (tr // hr), last), ci)))
        else:
            raise ValueError(kind)
    out_specs, out_shapes = [], []
    for shape, dtype, kind in outs:
        out_shapes.append(jax.ShapeDtypeStruct(shape, dtype))
        if kind == "row":
            out_specs.append(pl.BlockSpec((tr, shape[1]), lambda i: (i, 0)))
        elif kind[0] == "row":
            _, w, ci = kind
            out_specs.append(pl.BlockSpec((tr, w), lambda i, ci=ci: (i, ci)))
        else:
            out_specs.append(pl.BlockSpec(shape, lambda i, nd=len(shape): (0,) * nd))
    has_acc = any(k == "acc" for _, _, k in outs)
    return pl.pallas_call(
        body, name=name, grid=(n_tiles,), in_specs=in_specs, out_specs=out_specs, out_shape=out_shapes,
        scratch_shapes=list(scratch), input_output_aliases=dict(aliases or {}),
        compiler_params=_cparams(("arbitrary",) if has_acc else ("parallel",)),
    )(*arrays)


def _prev8(ref):
    n = ref.shape[0]
    return ref[n - 8:n, :].astype(F32)


def _next8(ref):
    return ref[0:8, :].astype(F32)


def _acc(ref, val):
    @pl.when(pl.program_id(0) == 0)
    def _():
        ref[...] = jnp.zeros_like(ref)
    ref[...] += val


def _rstd(x):
    return lax.rsqrt(jnp.mean(x * x, axis=-1, keepdims=True) + NORM_EPS)


def _sigmoid(x):
    return 1.0 / (1.0 + jnp.exp(-x))


def _silu(x):
    return x * _sigmoid(x)


def _dsilu(x):
    s = _sigmoid(x)
    return s * (1.0 + x * (1.0 - s))


def _softplus(x):
    return jnp.maximum(x, 0.0) + jnp.log(1.0 + jnp.exp(-jnp.abs(x)))


def _log_sigmoid_neg(x):
    t = -x
    return jnp.minimum(t, 0.0) - jnp.log(1.0 + jnp.exp(jnp.minimum(x, t)))


def normmod_fwd(x, g, scale, shift, name):
    t, d = x.shape

    def body(x_ref, g_ref, sc_ref, sh_ref, h_ref):
        xv = x_ref[...]
        h = xv * _rstd(xv) * g_ref[...] * (1.0 + sc_ref[...]) + sh_ref[...]
        h_ref[...] = h.astype(BF16)

    return rows_call(name, body, t, 512, [(x, "row"), (g, "full"), (scale, "full"), (shift, "full")],
                     [((t, d), BF16, "row")])[0]


def resid_normmod_fwd(x, f, gate, g_post, g_pre, scale, shift, name):
    t, d = x.shape

    def body(x_ref, f_ref, gate_ref, gp_ref, g_ref, sc_ref, sh_ref, xo_ref, h_ref):
        fv = f_ref[...]
        xn = x_ref[...] + gate_ref[...] * (fv * _rstd(fv) * gp_ref[...])
        xo_ref[...] = xn
        h = xn * _rstd(xn) * g_ref[...] * (1.0 + sc_ref[...]) + sh_ref[...]
        h_ref[...] = h.astype(BF16)

    return rows_call(name, body, t, 512,
                     [(x, "row"), (f, "row"), (gate, "full"), (g_post, "full"), (g_pre, "full"), (scale, "full"),
                      (shift, "full")],
                     [((t, d), F32, "row"), ((t, d), BF16, "row")])


def resid_loss(x, f, gate, g_post, target, name):
    t, d = x.shape

    def body(x_ref, f_ref, gate_ref, gp_ref, tg_ref, dy_ref, loss_ref):
        fv = f_ref[...]
        yv = x_ref[...] + gate_ref[...] * (fv * _rstd(fv) * gp_ref[...])
        err = yv - tg_ref[...]
        dy_ref[...] = err * (1.0 / d)
        part = 0.5 * jnp.sum(jnp.mean(err * err, axis=-1, keepdims=True), axis=0, keepdims=True)
        _acc(loss_ref, jnp.broadcast_to(part, loss_ref.shape))

    return rows_call(name, body, t, 512,
                     [(x, "row"), (f, "row"), (gate, "full"), (g_post, "full"), (target, "row")],
                     [((t, d), F32, "row"), ((8, LANES), F32, "acc")])


def resid_bwd(dx, f, gate, g_post, name):
    t, d = dx.shape

    def body(dx_ref, f_ref, gate_ref, gp_ref, df_ref, dgate_ref, dg_ref):
        fv, dxv, gp = f_ref[...], dx_ref[...], gp_ref[...]
        r = _rstd(fv)
        fn = fv * r
        _acc(dgate_ref, jnp.sum(dxv * (fn * gp), axis=0, keepdims=True))
        dn = dxv * gate_ref[...]
        _acc(dg_ref, jnp.sum(dn * fn, axis=0, keepdims=True))
        u = dn * gp
        df = r * (u - fn * jnp.mean(fn * u, axis=-1, keepdims=True))
        df_ref[...] = df.astype(BF16)

    return rows_call(name, body, t, 512, [(dx, "row"), (f, "row"), (gate, "full"), (g_post, "full")],
                     [((t, d), BF16, "row"), ((1, d), F32, "acc"), ((1, d), F32, "acc")])


def normmod_bwd(dh, x, dx_in, g, scale, name):
    t, d = x.shape

    def body(dh_ref, x_ref, dxi_ref, g_ref, sc_ref, dx_ref, dsc_ref, dsh_ref, dg_ref):
        xv, dhv, gv = x_ref[...], dh_ref[...], g_ref[...]
        r = _rstd(xv)
        xn = xv * r
        _acc(dsc_ref, jnp.sum(dhv * (xn * gv), axis=0, keepdims=True))
        _acc(dsh_ref, jnp.sum(dhv, axis=0, keepdims=True))
        dn = dhv * (1.0 + sc_ref[...])
        _acc(dg_ref, jnp.sum(dn * xn, axis=0, keepdims=True))
        u = dn * gv
        dx_ref[...] = dxi_ref[...] + r * (u - xn * jnp.mean(xn * u, axis=-1, keepdims=True))

    return rows_call(name, body, t, 512, [(dh, "row"), (x, "row"), (dx_in, "row"), (g, "full"), (scale, "full")],
                     [((t, d), F32, "row"), ((1, d), F32, "acc"), ((1, d), F32, "acc"), ((1, d), F32, "acc")])


def normmod_resid_bwd(dh, x, dx_in, g, scale, f, gate, g_post, name):
    t, d = x.shape

    def body(dh_ref, x_ref, dxi_ref, g_ref, sc_ref, f_ref, gate_ref, gp_ref,
             dx_ref, df_ref, dsc_ref, dsh_ref, dg_ref, dgate_ref, dgp_ref):
        xv, dhv, gv = x_ref[...], dh_ref[...], g_ref[...]
        r = _rstd(xv)
        xn = xv * r
        _acc(dsc_ref, jnp.sum(dhv * (xn * gv), axis=0, keepdims=True))
        _acc(dsh_ref, jnp.sum(dhv, axis=0, keepdims=True))
        dn = dhv * (1.0 + sc_ref[...])
        _acc(dg_ref, jnp.sum(dn * xn, axis=0, keepdims=True))
        u = dn * gv
        dxv = dxi_ref[...] + r * (u - xn * jnp.mean(xn * u, axis=-1, keepdims=True))
        dx_ref[...] = dxv
        fv, gp = f_ref[...], gp_ref[...]
        rf = _rstd(fv)
        fn = fv * rf
        _acc(dgate_ref, jnp.sum(dxv * (fn * gp), axis=0, keepdims=True))
        dnf = dxv * gate_ref[...]
        _acc(dgp_ref, jnp.sum(dnf * fn, axis=0, keepdims=True))
        uf = dnf * gp
        df_ref[...] = (rf * (uf - fn * jnp.mean(fn * uf, axis=-1, keepdims=True))).astype(BF16)

    vec = ((1, d), F32, "acc")
    return rows_call(name, body, t, 512,
                     [(dh, "row"), (x, "row"), (dx_in, "row"), (g, "full"), (scale, "full"), (f, "row"), (gate, "full"),
                      (g_post, "full")],
                     [((t, d), F32, "row"), ((t, d), BF16, "row"), vec, vec, vec, vec, vec])


def _pick(n, prefs):
    for p in prefs:
        if n % p == 0:
            return p
    return n


def mm_nn(a_list, b_list, out_dtype, name, tm=1024, tn=None, tk=None):
    m, k = a_list[0].shape
    n = b_list[0].shape[1]
    tm = min(tm, m)
    tn = tn or _pick(n, (1024, 768, 512, 256, 128))
    tk = tk or _pick(k, (1024, 1408, 512, 256))
    nk = k // tk
    npair = len(a_list)

    if nk == 1 and npair == 1:
        def body1(a_ref, b_ref, o_ref):
            o_ref[...] = jnp.dot(a_ref[...], b_ref[...], preferred_element_type=F32).astype(o_ref.dtype)

        return pl.pallas_call(
            body1, name=name, grid=(m // tm, n // tn),
            in_specs=[pl.BlockSpec((tm, k), lambda i, j: (i, 0)), pl.BlockSpec((k, tn), lambda i, j: (0, j))],
            out_specs=pl.BlockSpec((tm, tn), lambda i, j: (i, j)),
            out_shape=jax.ShapeDtypeStruct((m, n), out_dtype),
            compiler_params=_cparams(("parallel", "parallel")),
        )(a_list[0], b_list[0])

    def body(*refs):
        a_refs, b_refs = refs[:npair], refs[npair:2 * npair]
        o_ref, acc = refs[2 * npair], refs[2 * npair + 1]
        kk = pl.program_id(2)

        @pl.when(kk == 0)
        def _():
            acc[...] = jnp.zeros_like(acc)

        s = acc[...]
        for a_ref, b_ref in zip(a_refs, b_refs):
            s = s + jnp.dot(a_ref[...], b_ref[...], preferred_element_type=F32)
        acc[...] = s

        @pl.when(kk == nk - 1)
        def _():
            o_ref[...] = acc[...].astype(o_ref.dtype)

    return pl.pallas_call(
        body, name=name, grid=(m // tm, n // tn, nk),
        in_specs=[pl.BlockSpec((tm, tk), lambda i, j, kk: (i, kk))] * npair
        + [pl.BlockSpec((tk, tn), lambda i, j, kk: (kk, j))] * npair,
        out_specs=pl.BlockSpec((tm, tn), lambda i, j, kk: (i, j)),
        out_shape=jax.ShapeDtypeStruct((m, n), out_dtype),
        scratch_shapes=[pltpu.VMEM((tm, tn), F32)],
        compiler_params=_cparams(("parallel", "parallel", "arbitrary")),
    )(*a_list, *b_list)


def mm_nt(a_list, b_list, b_koff, out_dtype, name, tm=1024):
    m, k = a_list[0].shape
    n = b_list[0].shape[0]
    tm = min(tm, m)
    tn = _pick(n, (1024, 512, 256))
    tk = _pick(k, (1024, 1408, 512, 256))
    nk = k // tk
    npair = len(a_list)
    koff = [o // tk for o in b_koff]
    nt_dims = (((1,), (1,)), ((), ()))

    def body(*refs):
        a_refs, b_refs = refs[:npair], refs[npair:2 * npair]
        o_ref, acc = refs[2 * npair], refs[2 * npair + 1]
        kk = pl.program_id(2)

        @pl.when(kk == 0)
        def _():
            acc[...] = jnp.zeros_like(acc)

        s = acc[...]
        for a_ref, b_ref in zip(a_refs, b_refs):
            s = s + lax.dot_general(a_ref[...], b_ref[...], nt_dims, preferred_element_type=F32)
        acc[...] = s

        @pl.when(kk == nk - 1)
        def _():
            o_ref[...] = acc[...].astype(o_ref.dtype)

    return pl.pallas_call(
        body, name=name, grid=(m // tm, n // tn, nk),
        in_specs=[pl.BlockSpec((tm, tk), lambda i, j, kk: (i, kk))] * npair
        + [pl.BlockSpec((tn, tk), lambda i, j, kk, o=o: (j, kk + o)) for o in koff],
        out_specs=pl.BlockSpec((tm, tn), lambda i, j, kk: (i, j)),
        out_shape=jax.ShapeDtypeStruct((m, n), out_dtype),
        scratch_shapes=[pltpu.VMEM((tm, tn), F32)],
        compiler_params=_cparams(("parallel", "parallel", "arbitrary")),
    )(*a_list, *b_list)


def mm_tn(a, b, name, tt=512):
    t, ka = a.shape
    n = b.shape[1]
    ta = _pick(ka, (1024, 1408, 512, 256))
    tn = _pick(n, (2048, 1024, 1408, 512, 256))
    nt = t // tt

    def body(a_ref, b_ref, o_ref, acc):
        s = pl.program_id(2)

        @pl.when(s == 0)
        def _():
            acc[...] = jnp.zeros_like(acc)

        acc[...] += lax.dot_general(a_ref[...], b_ref[...], (((0,), (0,)), ((), ())), preferred_element_type=F32)

        @pl.when(s == nt - 1)
        def _():
            o_ref[...] = acc[...].astype(BF16)

    return pl.pallas_call(
        body, name=name, grid=(ka // ta, n // tn, nt),
        in_specs=[pl.BlockSpec((tt, ta), lambda i, j, s: (s, i)), pl.BlockSpec((tt, tn), lambda i, j, s: (s, j))],
        out_specs=pl.BlockSpec((ta, tn), lambda i, j, s: (i, j)),
        out_shape=jax.ShapeDtypeStruct((ka, n), BF16),
        scratch_shapes=[pltpu.VMEM((ta, tn), F32)],
        compiler_params=_cparams(("parallel", "parallel", "arbitrary")),
    )(a, b)


def mm_swiglu_fwd(h, w_ffn_in, name, tm=1024, tn=1408):
    m, k = h.shape
    nh = FFN_HIDDEN // tn

    def body(h_ref, wg_ref, wu_ref, gt_ref, up_ref, a_ref):
        hv = h_ref[...]
        gt = jnp.dot(hv, wg_ref[...], preferred_element_type=F32)
        up = jnp.dot(hv, wu_ref[...], preferred_element_type=F32)
        gt_ref[...] = gt.astype(BF16)
        up_ref[...] = up.astype(BF16)
        a_ref[...] = (_silu(gt) * up).astype(BF16)

    shp = jax.ShapeDtypeStruct((m, FFN_HIDDEN), BF16)
    ospec = pl.BlockSpec((tm, tn), lambda i, j: (i, j))
    return pl.pallas_call(
        body, name=name, grid=(m // tm, nh),
        in_specs=[pl.BlockSpec((tm, k), lambda i, j: (i, 0)), pl.BlockSpec((k, tn), lambda i, j: (0, j)),
                  pl.BlockSpec((k, tn), lambda i, j: (0, j + nh))],
        out_specs=[ospec, ospec, ospec], out_shape=[shp, shp, shp],
        compiler_params=_cparams(("parallel", "parallel")),
    )(h, w_ffn_in, w_ffn_in)


def mm_swiglu_bwd(df, w_out, gt, up, name, tm=256, sub=256):
    m, k = df.shape
    n_sub = FFN_HIDDEN // sub

    def body(df_ref, w_ref, gt_ref, up_ref, dgt_ref, dup_ref):
        dfv = df_ref[...]

        def chunk_dot(c):
            return lax.dot_general(dfv, w_ref[c * sub:(c + 1) * sub, :], (((1,), (1,)), ((), ())), preferred_element_type=F32)

        da_next = chunk_dot(0)
        for c in range(n_sub):
            da = da_next
            if c + 1 < n_sub:
                da_next = chunk_dot(c + 1)
            cols = slice(c * sub, (c + 1) * sub)
            gtv = gt_ref[:, cols].astype(F32)
            sg = _sigmoid(gtv)
            dgt_ref[:, cols] = (da * up_ref[:, cols].astype(F32) * (sg * (1.0 + gtv * (1.0 - sg)))).astype(BF16)
            dup_ref[:, cols] = (da * (gtv * sg)).astype(BF16)

    shp = jax.ShapeDtypeStruct((m, FFN_HIDDEN), BF16)
    tile = pl.BlockSpec((tm, FFN_HIDDEN), lambda i: (i, 0))
    return pl.pallas_call(
        body, name=name, grid=(m // tm,),
        in_specs=[pl.BlockSpec((tm, k), lambda i: (i, 0)), pl.BlockSpec((FFN_HIDDEN, k), lambda i: (0, 0)), tile, tile],
        out_specs=[tile, tile], out_shape=[shp, shp],
        compiler_params=_cparams(("parallel",)),
    )(df, w_out, gt, up)


def _shift_down(x, prev8, j):
    if j == 0:
        return x
    xr = pltpu.roll(x, j, 0)
    pr = pltpu.roll(prev8, j, 0)
    row = lax.broadcasted_iota(jnp.int32, (8, x.shape[1]), 0)
    head = jnp.where(row < j, pr, xr[:8])
    return head if x.shape[0] == 8 else jnp.concatenate([head, xr[8:]], axis=0)


def _shift_up(x, next8, j):
    if j == 0:
        return x
    n = x.shape[0]
    xr = pltpu.roll(x, n - j, 0)
    nr = pltpu.roll(next8, 8 - j, 0)
    row = lax.broadcasted_iota(jnp.int32, (8, x.shape[1]), 0)
    return jnp.concatenate([xr[:n - 8], jnp.where(row >= 8 - j, nr, xr[n - 8:])], axis=0)


def _conv_taps(x, prev8, w_ref, taps):
    out = None
    for k in range(taps):
        term = w_ref[k:k + 1, :] * _shift_down(x, prev8, taps - 1 - k)
        out = term if out is None else out + term
    return out


def post_inproj(p, sc_w, ssm_w, ssm_b, name, tr=512):
    t = p.shape[0]

    def body(sc_ref, scp_ref, qkv_ref, xbc_ref, xbcp_ref, scw_ref, sw_ref, sb_ref, ya_ref, qkvo_ref, act_ref):
        first = (pl.program_id(0) > 0).astype(F32)
        sc = sc_ref[...].astype(F32)
        scp = _prev8(scp_ref) * first
        u = sc[:, 256:512] * sc[:, 512:768]
        up = scp[:, 256:512] * scp[:, 512:768]
        ya_ref[...] = (sc[:, 0:256] * _conv_taps(u, up, scw_ref, 3)).astype(BF16)
        qkv = qkv_ref[...]
        qkvo_ref[:, 0:256] = (qkv[:, 0:256].astype(F32) * 0.125).astype(BF16)
        qkvo_ref[:, 256:768] = qkv[:, 256:768].astype(BF16)
        xc = _conv_taps(xbc_ref[...].astype(F32), _prev8(xbcp_ref) * first, sw_ref, 4) + sb_ref[...]
        act_ref[...] = _silu(xc)

    return rows_call(
        name, body, t, tr,
        [(p, ("row", 768, OFF_SC // 768)), (p, ("prev8", 768, OFF_SC // 768)), (p, ("row", 768, OFF_QKV // 768)),
         (p, ("row", 768, OFF_XBC // 768)), (p, ("prev8", 768, OFF_XBC // 768)),
         (sc_w, "full"), (ssm_w, "full"), (ssm_b, "full")],
        [((t, 256), BF16, "row"), ((t, 768), BF16, "row"), ((t, 768), F32, "row")])


def branch_out_fwd(ya, yb, yc, p, w_cat, name, tr=256):
    t = p.shape[0]

    def body(ya_ref, yb_ref, yc_ref, gl_ref, w_ref, o_ref):
        y_a = jnp.dot(ya_ref[...], w_ref[0:256, :], preferred_element_type=F32)
        y_b = jnp.dot(yb_ref[...].astype(BF16), w_ref[256:512, :], preferred_element_type=F32)
        y_c = jnp.dot(yc_ref[...], w_ref[512:1024, :], preferred_element_type=F32)
        m = (_sigmoid(gl_ref[:, 0:1024].astype(F32)) * y_a + _sigmoid(gl_ref[:, 1024:2048].astype(F32)) * y_b
             + _sigmoid(gl_ref[:, 2048:3072].astype(F32)) * y_c)
        o_ref[...] = m.astype(BF16)

    return rows_call(name, body, t, tr,
                     [(ya, "row"), (yb, "row"), (yc, "row"), (p, ("row", 3072, 0)), (w_cat, "full")],
                     [((t, D_MODEL), BF16, "row")])[0]


def branch_out_bwd(dm, ya, yb, yc, p, w_cat, name, tr=256):
    t = p.shape[0]
    tn_dims = (((0,), (0,)), ((), ()))

    def body(dm_ref, ya_ref, yb_ref, yc_ref, gl_ref, w_ref, dgl_ref, dya_ref, dyb_ref, dyc_ref, dw_ref):
        @pl.when(pl.program_id(0) == 0)
        def _():
            dw_ref[...] = jnp.zeros_like(dw_ref)

        dmv = dm_ref[...]
        ins = (ya_ref[...], yb_ref[...].astype(BF16), yc_ref[...])
        rows = ((0, 256), (256, 512), (512, 1024))
        outs = (dya_ref, dyb_ref, dyc_ref)
        for i in range(3):
            r0, r1 = rows[i]
            y = jnp.dot(ins[i], w_ref[r0:r1, :], preferred_element_type=F32)
            s = _sigmoid(gl_ref[:, 1024 * i:1024 * (i + 1)].astype(F32))
            dgl_ref[:, 1024 * i:1024 * (i + 1)] = (dmv * y * s * (1.0 - s)).astype(BF16)
            dy = (dmv * s).astype(BF16)
            outs[i][...] = lax.dot_general(dy, w_ref[r0:r1, :], _NT, preferred_element_type=F32)
            dw_ref[r0:r1, :] += lax.dot_general(ins[i], dy, tn_dims, preferred_element_type=F32)

    return rows_call(name, body, t, tr,
                     [(dm, "row"), (ya, "row"), (yb, "row"), (yc, "row"), (p, ("row", 3072, 0)), (w_cat, "full")],
                     [((t, IN_PAD), BF16, ("row", 3072, 0)), ((t, 256), F32, "row"), ((t, 256), F32, "row"),
                      ((t, 512), F32, "row"), ((D_MODEL, D_MODEL), F32, "acc")])


def assemble_dp(dp, dya, p, sc_w, dq, dk, dv, dact, ssm_w, ssm_b, ddt, dz, name, tr=256):
    t = p.shape[0]
    n_tiles = t // tr
    sci, xi = OFF_SC // 768, OFF_XBC // 768
    base = OFF_SC
    assert base == IN_PAD - base
    o_sc, o_qkv, o_xbc, o_dt, o_z, o_end = (c - base for c in (OFF_SC, OFF_QKV, OFF_XBC, OFF_DT, OFF_Z, IN_PAD))

    def body(dp_ref, dya_ref, dyan_ref, sc_ref, scp_ref, scn_ref, scw_ref, dq_ref, dk_ref, dv_ref,
             dact_ref, dactn_ref, xbc_ref, xbcp_ref, xbcn_ref, sw_ref, sb_ref, ddt_ref, dz_ref,
             o_ref, dscw_ref, dsw_ref, dsb_ref):
        i = pl.program_id(0)

        @pl.when(i == 0)
        def _():
            dscw_ref[...] = jnp.zeros_like(dscw_ref)
            dsw_ref[...] = jnp.zeros_like(dsw_ref)
            dsb_ref[...] = jnp.zeros_like(dsb_ref)

        first = (i > 0).astype(F32)
        last = (i < n_tiles - 1).astype(F32)
        del dp_ref
        sc = sc_ref[...].astype(F32)
        scp = _prev8(scp_ref) * first
        scn = _next8(scn_ref) * last
        u = sc[:, 256:512] * sc[:, 512:768]
        up = scp[:, 256:512] * scp[:, 512:768]
        dya_v = dya_ref[...]
        cv = _conv_taps(u, up, scw_ref, 3)
        o_ref[:, o_sc:o_sc + 256] = (dya_v * cv).astype(BF16)
        dcv = dya_v * sc[:, 0:256]
        dcvn = _next8(dyan_ref) * last * scn[:, 0:256]
        du = None
        for k in range(3):
            sh = 2 - k
            term = scw_ref[k:k + 1, :] * _shift_up(dcv, dcvn, sh)
            du = term if du is None else du + term
            dscw_ref[k:k + 1, :] += jnp.sum(dcv * _shift_down(u, up, sh), axis=0, keepdims=True)
        o_ref[:, o_sc + 256:o_sc + 512] = (du * sc[:, 512:768]).astype(BF16)
        o_ref[:, o_sc + 512:o_sc + 768] = (du * sc[:, 256:512]).astype(BF16)
        o_ref[:, o_qkv:o_qkv + 256] = (dq_ref[...] * 0.125).astype(BF16)
        o_ref[:, o_qkv + 256:o_qkv + 512] = dk_ref[...].astype(BF16)
        o_ref[:, o_qkv + 512:o_qkv + 768] = dv_ref[...].astype(BF16)
        xb = xbc_ref[...].astype(F32)
        xbp = _prev8(xbcp_ref) * first
        xbn = _next8(xbcn_ref)
        xc = _conv_taps(xb, xbp, sw_ref, 4) + sb_ref[...]
        xcn = _conv_taps(xbn, xb[tr - 8:, :], sw_ref, 4) + sb_ref[...]
        dxc = dact_ref[...] * _dsilu(xc)
        dxcn = _next8(dactn_ref) * _dsilu(xcn) * last
        dxb = None
        for k in range(4):
            sh = 3 - k
            term = sw_ref[k:k + 1, :] * _shift_up(dxc, dxcn, sh)
            dxb = term if dxb is None else dxb + term
            dsw_ref[k:k + 1, :] += jnp.sum(dxc * _shift_down(xb, xbp, sh), axis=0, keepdims=True)
        dsb_ref[...] += jnp.sum(dxc, axis=0, keepdims=True)
        o_ref[:, o_xbc:o_xbc + 768] = dxb.astype(BF16)
        o_ref[:, o_dt:o_dt + 128] = ddt_ref[...].astype(BF16)
        o_ref[:, o_dt + 128:o_z] = jnp.zeros((tr, o_z - o_dt - 128), BF16)
        o_ref[:, o_z:o_end] = dz_ref[...].astype(BF16)

    return rows_call(
        name, body, t, tr,
        [(dp, "any"), (dya, "row"), (dya, ("next8", 256, 0)),
         (p, ("row", 768, sci)), (p, ("prev8", 768, sci)), (p, ("next8", 768, sci)), (sc_w, "full"),
         (dq, "row"), (dk, "row"), (dv, "row"),
         (dact, "row"), (dact, ("next8", 768, 0)),
         (p, ("row", 768, xi)), (p, ("prev8", 768, xi)), (p, ("next8", 768, xi)), (ssm_w, "full"), (ssm_b, "full"),
         (ddt, "row"), (dz, "row")],
        [((t, IN_PAD), BF16, ("row", IN_PAD - base, 1)), ((8, 256), F32, "acc"), ((8, 768), F32, "acc"),
         ((1, 768), F32, "acc")], aliases={0: 0})


def adamw_flat(slots, w, m, v, name, tr=512):
    n_slots, rows, lanes = slots.shape
    tr = max(d for d in range(8, min(tr, rows) + 1, 8) if rows % d == 0) if rows % 8 == 0 else rows
    bc1 = 1.0 - ADAM_B1 ** ADAM_STEP
    bc2 = 1.0 - ADAM_B2 ** ADAM_STEP

    def body(s_ref, w_ref, m_ref, v_ref, g_ref, d_ref, mo_ref, vo_ref):
        g = s_ref[0].astype(F32)
        for k in range(1, n_slots):
            g = g + s_ref[k].astype(F32)
        mn = ADAM_B1 * m_ref[...] + (1.0 - ADAM_B1) * g
        vn = ADAM_B2 * v_ref[...] + (1.0 - ADAM_B2) * (g * g)
        m_hat = mn / bc1
        v_hat = vn / bc2
        g_ref[...] = g
        d_ref[...] = -ADAM_LR * (m_hat / (jnp.sqrt(v_hat) + ADAM_EPS) + ADAM_WD * w_ref[...])
        mo_ref[...] = mn
        vo_ref[...] = vn

    tile = pl.BlockSpec((tr, lanes), lambda i: (i, 0))
    shp = jax.ShapeDtypeStruct((rows, lanes), F32)
    return pl.pallas_call(
        body, name=name, grid=(rows // tr,),
        in_specs=[pl.BlockSpec((n_slots, tr, lanes), lambda i: (0, i, 0)), tile, tile, tile],
        out_specs=[tile] * 4, out_shape=[shp] * 4,
        compiler_params=_cparams(("parallel",)),
    )(slots, w, m, v)


def _split_dot(x, tri):
    hi = x.astype(BF16)
    lo = (x - hi.astype(F32)).astype(BF16)
    return jnp.dot(hi, tri, preferred_element_type=F32) + jnp.dot(lo, tri, preferred_element_type=F32)


_NT = (((1,), (1,)), ((), ()))
_TN = (((0,), (0,)), ((), ()))

SBA_EXP_ZERO = -104.0
SBA_SKIPPED = -1e30


def sba_fwd(qkv, name, bq=256, bk=256):
    t = qkv.shape[0]
    ratio = bq // bk
    assert bq == ratio * bk and t // bk <= LANES

    def body(q_ref, k_ref, v_ref, o_ref, runs_ref, acc_s, run_s):
        i = pl.program_id(1)
        lane = lax.broadcasted_iota(jnp.int32, (1, LANES), 1)
        lane_q = lax.broadcasted_iota(jnp.int32, (bq, LANES), 1)
        qi = lax.broadcasted_iota(jnp.int32, (bq, bk), 0) + i * bq
        kj = lax.broadcasted_iota(jnp.int32, (bq, bk), 1)
        later = (lax.broadcasted_iota(jnp.int32, (bk, bk), 0) > lax.broadcasted_iota(jnp.int32, (bk, bk), 1)).astype(BF16)
        qv = q_ref[...]
        qms = [jnp.where(hm, qv, jnp.zeros_like(qv)) for hm in (lane < 64, lane >= 64)]
        acc_s[...] = jnp.zeros_like(acc_s)
        run_s[...] = jnp.zeros_like(run_s)
        runs_ref[...] = jnp.full(runs_ref.shape, SBA_SKIPPED, F32)

        def tiles(specs):
            chains = [(ti, hh) for ti in range(len(specs)) for hh in range(2)]
            kb = [k_ref[pl.ds(pl.multiple_of(j * bk, bk), bk), :] for j, _ in specs]
            vb = [v_ref[pl.ds(pl.multiple_of(j * bk, bk), bk), :] for j, _ in specs]
            mask = [(kj + j * bk) < qi if masked else None for j, masked in specs]
            s = {c: lax.dot_general(qms[c[1]], kb[c[0]], _NT, preferred_element_type=F32) for c in chains}
            lk = {c: _log_sigmoid_neg(s[c]) for c in chains}
            lk = {c: lk[c] if mask[c[0]] is None else jnp.where(mask[c[0]], lk[c], 0.0) for c in chains}
            w = {c: jnp.dot(lk[c].astype(BF16), later, preferred_element_type=F32) for c in chains}
            run = {}
            for hh in range(2):
                carry = run_s[hh]
                for ti in range(len(specs)):
                    run[ti, hh] = carry
                    carry = carry + jnp.sum(lk[ti, hh], axis=1, keepdims=True)
                run_s[hh] = carry
            a = {c: jnp.exp(s[c] + lk[c] + w[c] + run[c]) for c in chains}
            a = {c: a[c] if mask[c[0]] is None else jnp.where(mask[c[0]], a[c], 0.0) for c in chains}
            for ti, hh in chains:
                acc_s[hh] += jnp.dot(a[ti, hh].astype(BF16), vb[ti], preferred_element_type=F32)
                runs_ref[hh] = jnp.where(lane_q == specs[ti][0], run[ti, hh], runs_ref[hh])

        for d in range(ratio):
            tiles([((i + 1) * ratio - 1 - d, True)])

        def live():
            return jnp.maximum(jnp.max(run_s[0]), jnp.max(run_s[1])) >= SBA_EXP_ZERO

        def cond(state):
            n, go = state
            return jnp.logical_and(n < i * ratio, go)

        def step(state):
            n, _ = state
            tiles([(i * ratio - 1 - n, False)])
            return n + 1, live()

        lax.while_loop(cond, step, (jnp.int32(0), live()))
        o_ref[...] = jnp.where(lane < 64, acc_s[0], acc_s[1])

    return pl.pallas_call(
        body, name=name, grid=(2, t // bq),
        in_specs=[pl.BlockSpec((bq, LANES), lambda p, i: (i, p)), pl.BlockSpec((t, LANES), lambda p, i: (0, 2 + p)),
                  pl.BlockSpec((t, LANES), lambda p, i: (0, 4 + p))],
        out_specs=[pl.BlockSpec((bq, LANES), lambda p, i: (i, p)), pl.BlockSpec((2, bq, LANES), lambda p, i: (p, i, 0))],
        out_shape=[jax.ShapeDtypeStruct((t, SB_WIDTH), F32), jax.ShapeDtypeStruct((4, t, LANES), F32)],
        scratch_shapes=[pltpu.VMEM((2, bq, LANES), F32), pltpu.VMEM((2, bq, 1), F32)],
        compiler_params=_cparams(("parallel", "parallel")),
    )(qkv, qkv, qkv)


def sba_bwd(qkv, runs, do, name, bq=256, bk=256):
    t = qkv.shape[0]
    ratio = bq // bk
    assert bq == ratio * bk
    nq = t // bq

    def body(q_ref, k_ref, v_ref, runs_ref, do_ref, dq_ref, dk_hbm, dv_hbm, dk_s, dv_s, sem, dq_s, rg_s):
        p = pl.program_id(0)
        i = pl.program_id(1)

        @pl.when(i == 0)
        def _():
            dk_s[...] = jnp.zeros_like(dk_s)
            dv_s[...] = jnp.zeros_like(dv_s)

        lane = lax.broadcasted_iota(jnp.int32, (1, LANES), 1)
        qi = lax.broadcasted_iota(jnp.int32, (bq, bk), 0) + i * bq
        kj = lax.broadcasted_iota(jnp.int32, (bq, bk), 1)
        r2 = lax.broadcasted_iota(jnp.int32, (bk, bk), 0)
        c2 = lax.broadcasted_iota(jnp.int32, (bk, bk), 1)
        later = (r2 > c2).astype(BF16)
        earlier = (r2 < c2).astype(BF16)
        qv = q_ref[...]
        dov = do_ref[...]
        heads = range(2)
        hms = (lane < 64, lane >= 64)
        qms = [jnp.where(hm, qv, jnp.zeros_like(qv)) for hm in hms]
        doms = [jnp.where(hm, dov, 0.0).astype(BF16) for hm in hms]
        runs = [runs_ref[hh] for hh in heads]
        dq_s[...] = jnp.zeros_like(dq_s)
        rg_s[...] = jnp.zeros_like(rg_s)

        def tiles(specs):
            nt = len(specs)
            chains = [(ti, hh) for ti in range(nt) for hh in heads]
            starts = [pl.multiple_of(j * bk, bk) for j, _ in specs]
            kb = [k_ref[pl.ds(st, bk), :] for st in starts]
            vb = [v_ref[pl.ds(st, bk), :] for st in starts]
            mask = [(kj + j * bk) < qi if masked else None for j, masked in specs]
            s = {c: lax.dot_general(qms[c[1]], kb[c[0]], _NT, preferred_element_type=F32) for c in chains}
            da = {c: lax.dot_general(doms[c[1]], vb[c[0]], _NT, preferred_element_type=F32) for c in chains}
            lk_raw = {c: _log_sigmoid_neg(s[c]) for c in chains}
            lk = {c: lk_raw[c] if mask[c[0]] is None else jnp.where(mask[c[0]], lk_raw[c], 0.0) for c in chains}
            w = {c: jnp.dot(lk[c].astype(BF16), later, preferred_element_type=F32) for c in chains}
            run = {c: jnp.sum(jnp.where(lane == specs[c[0]][0], runs[c[1]], 0.0), axis=1, keepdims=True) for c in chains}
            a = {c: jnp.exp(s[c] + lk[c] + w[c] + run[c]) for c in chains}
            a = {c: a[c] if mask[c[0]] is None else jnp.where(mask[c[0]], a[c], 0.0) for c in chains}
            g = {c: a[c] * da[c] for c in chains}
            rg = {}
            for hh in heads:
                carry = rg_s[hh]
                for ti in range(nt):
                    rg[ti, hh] = carry
                    carry = carry + jnp.sum(g[ti, hh], axis=1, keepdims=True)
                rg_s[hh] = carry
            cpre = {c: rg[c] + _split_dot(g[c], earlier) for c in chains}
            dz = {c: g[c] - jnp.exp(s[c] + lk_raw[c]) * (g[c] + cpre[c]) for c in chains}
            dz = {c: (dz[c] if mask[c[0]] is None else jnp.where(mask[c[0]], dz[c], 0.0)).astype(BF16) for c in chains}
            for ti, hh in chains:
                dq_s[hh] += jnp.dot(dz[ti, hh], kb[ti], preferred_element_type=F32)
            for ti in range(nt):
                dk = lax.dot_general(dz[ti, 0], qms[0], _TN, preferred_element_type=F32)
                dk_s[pl.ds(starts[ti], bk), :] += dk + lax.dot_general(dz[ti, 1], qms[1], _TN, preferred_element_type=F32)
                dv = lax.dot_general(a[ti, 0].astype(BF16), doms[0], _TN, preferred_element_type=F32)
                dv_s[pl.ds(starts[ti], bk), :] += dv + lax.dot_general(a[ti, 1].astype(BF16), doms[1], _TN,
                                                                       preferred_element_type=F32)

        live = jnp.maximum(jnp.max(runs[0], axis=0, keepdims=True), jnp.max(runs[1], axis=0, keepdims=True)) >= SBA_EXP_ZERO
        first = jnp.minimum(jnp.min(jnp.where(live, lane, LANES)), i * ratio)

        def step(j, carry):
            tiles([(j, False)])
            return carry

        if ratio == 1:
            lax.fori_loop(first, i - 1, step, 0)

            @pl.when(i == 0)
            def _():
                tiles([(i, True)])

            @pl.when(i > 0)
            def _():
                tiles([(i - 1, False), (i, True)])
        else:
            lax.fori_loop(first, i * ratio, step, 0)
            for d in range(ratio):
                tiles([(i * ratio + d, True)])

        dq_ref[...] = jnp.where(lane < 64, dq_s[0], dq_s[1])

        @pl.when(i == nq - 1)
        def _():
            col = pl.multiple_of(p * LANES, LANES)
            ck = pltpu.make_async_copy(dk_s, dk_hbm.at[:, pl.ds(col, LANES)], sem.at[0])
            cv = pltpu.make_async_copy(dv_s, dv_hbm.at[:, pl.ds(col, LANES)], sem.at[1])
            ck.start()
            cv.start()
            ck.wait()
            cv.wait()

    shp = jax.ShapeDtypeStruct((t, SB_WIDTH), F32)
    tile = pl.BlockSpec((bq, LANES), lambda p, i: (i, p))
    return pl.pallas_call(
        body, name=name, grid=(2, nq),
        in_specs=[tile, pl.BlockSpec((t, LANES), lambda p, i: (0, 2 + p)), pl.BlockSpec((t, LANES), lambda p, i: (0, 4 + p)),
                  pl.BlockSpec((2, bq, LANES), lambda p, i: (p, i, 0)), tile],
        out_specs=[tile, pl.BlockSpec(memory_space=pl.ANY), pl.BlockSpec(memory_space=pl.ANY)],
        out_shape=[shp, shp, shp],
        scratch_shapes=[pltpu.VMEM((t, LANES), F32), pltpu.VMEM((t, LANES), F32), pltpu.SemaphoreType.DMA((2,)),
                        pltpu.VMEM((2, bq, LANES), F32), pltpu.VMEM((2, bq, 1), F32)],
        compiler_params=_cparams(("arbitrary", "arbitrary")),
    )(qkv, qkv, qkv, runs, do)


def _ssd_consts():
    ln = SSM_CHUNK
    ri = lax.broadcasted_iota(jnp.int32, (ln, ln), 0)
    ci = lax.broadcasted_iota(jnp.int32, (ln, ln), 1)
    eh = lax.broadcasted_iota(jnp.int32, (LANES, SSM_INNER), 0)
    el = lax.broadcasted_iota(jnp.int32, (LANES, SSM_INNER), 1)
    expand = (jnp.right_shift(el, 6) == eh).astype(BF16)
    th = lax.broadcasted_iota(jnp.int32, (SSM_INNER, LANES), 1)
    tl = lax.broadcasted_iota(jnp.int32, (SSM_INNER, LANES), 0)
    reduce = (jnp.right_shift(tl, 6) == th).astype(BF16)
    return ri, ci, expand, reduce


def _dot_f32(a, b):
    return jnp.dot(a, b, precision=HI, preferred_element_type=F32)


def _split3(x):
    hi = x.astype(BF16)
    r1 = x - hi.astype(F32)
    mid = r1.astype(BF16)
    lo = (r1 - mid.astype(F32)).astype(BF16)
    return hi, mid, lo


def _dot_hi(a, b):
    if a.dtype == BF16:
        return sum(jnp.dot(a, t, preferred_element_type=F32) for t in _split3(b))
    return sum(jnp.dot(t, b, preferred_element_type=F32) for t in _split3(a))


def _ssd_prelude(xbc_ref, dt_ref, dtt_ref, hpr_ref, hpc_ref, ri, ci, expand):
    ln = SSM_CHUNK
    xs = xbc_ref[:, 0:512]
    bm = xbc_ref[:, 512:640]
    cm = xbc_ref[:, 640:768]
    dtb_r = hpr_ref[0:1, :]
    aneg_r = -jnp.exp(hpr_ref[1:2, :])
    pre = dt_ref[...] + dtb_r
    dt = _softplus(pre)
    a = dt * aneg_r
    dtt = _softplus(dtt_ref[...] + hpc_ref[0:8, :])
    att = dtt * (-jnp.exp(hpc_ref[8:16, :]))
    tril = (ri >= ci).astype(BF16)
    triu = (ri <= ci).astype(BF16)
    acs = _dot_hi(tril, a)
    acst = _dot_hi(att, triu)
    acs_e = _dot_hi(acs, expand)
    dt_e = _dot_hi(dt, expand)
    last_e = acs_e[ln - 1:ln, :]
    e_e = jnp.exp(acs_e)
    w_e = jnp.exp(last_e - acs_e)
    dec_e = jnp.exp(last_e)
    xdt = xs * dt_e
    return dict(xs=xs, bm=bm, cm=cm, pre=pre, dt=dt, aneg_r=aneg_r, acs=acs, acst=acst, dt_e=dt_e, e_e=e_e,
                w_e=w_e, dec_e=dec_e, xdt=xdt, triu=triu)


def ssd_fwd(act, p, dt32, dtt, hp_rows, hp_cols, d_e, norm_w, name):
    t = act.shape[0]
    ln = SSM_CHUNK
    nc = t // ln

    def body(xbc_ref, dt_ref, z_ref, dtt_ref, hpr_ref, hpc_ref, d_ref, nw_ref, yc_ref, y_ref, sto_ref, st):
        @pl.when(pl.program_id(0) == 0)
        def _():
            st[...] = jnp.zeros_like(st)

        ri, ci, expand, _ = _ssd_consts()
        q = _ssd_prelude(xbc_ref, dt_ref, dtt_ref, hpr_ref, hpc_ref, ri, ci, expand)
        lane = lax.broadcasted_iota(jnp.int32, (1, LANES), 1)
        rown = lax.broadcasted_iota(jnp.int32, (LANES, 1), 0)
        low = lane < 64
        mask = ri >= ci
        xdt_b = q["xdt"].astype(BF16)
        xw_b = (q["xdt"] * q["w_e"]).astype(BF16)
        bt = q["bm"].T.astype(BF16)
        cb_ = q["cm"].astype(BF16)
        y_pairs = []
        for g in range(2):
            gm = low if g == 0 else jnp.logical_not(low)
            rm = (rown < 64) if g == 0 else (rown >= 64)
            cg = jnp.where(gm, cb_, jnp.zeros_like(cb_))
            cb = jnp.dot(cg, bt, preferred_element_type=F32)
            for pp in range(2):
                pi = 2 * g + pp
                sl = slice(LANES * pi, LANES * (pi + 1))
                xp = xdt_b[:, sl]
                yd = []
                for hh in range(2):
                    h = 2 * pi + hh
                    diff = q["acs"][:, h:h + 1] - q["acst"][h:h + 1, :]
                    lam = jnp.exp(jnp.where(mask, diff, -jnp.inf))
                    yd.append(jnp.dot((cb * lam).astype(BF16), xp, preferred_element_type=F32))
                sp = st[pi]
                sto_ref[0, pi] = sp
                yoff = jnp.dot(cg, sp.astype(BF16), preferred_element_type=F32) * q["e_e"][:, sl]
                upd = jnp.dot(bt, xw_b[:, sl], preferred_element_type=F32)
                st[pi] = q["dec_e"][:, sl] * sp + jnp.where(rm, upd, 0.0)
                y_pairs.append(jnp.where(low, yd[0], yd[1]) + yoff)
        y = jnp.concatenate(y_pairs, axis=1) + q["xs"] * d_ref[...]
        y_ref[...] = y
        yg = y * _silu(z_ref[...].astype(F32))
        for g in range(2):
            sl = slice(256 * g, 256 * (g + 1))
            seg = yg[:, sl]
            yc_ref[:, sl] = (seg * _rstd(seg) * nw_ref[:, sl]).astype(BF16)

    return pl.pallas_call(
        body, name=name, grid=(nc,),
        in_specs=[pl.BlockSpec((ln, 768), lambda c: (c, 0)), pl.BlockSpec((ln, LANES), lambda c: (c, 0)),
                  pl.BlockSpec((ln, 512), lambda c: (c, OFF_Z // 512)), pl.BlockSpec((8, ln), lambda c: (0, c)),
                  pl.BlockSpec((8, LANES), lambda c: (0, 0)), pl.BlockSpec((16, ln), lambda c: (0, 0)),
                  pl.BlockSpec((1, 512), lambda c: (0, 0)), pl.BlockSpec((1, 512), lambda c: (0, 0))],
        out_specs=[pl.BlockSpec((ln, 512), lambda c: (c, 0)), pl.BlockSpec((ln, 512), lambda c: (c, 0)),
                   pl.BlockSpec((1, 4, LANES, LANES), lambda c: (c, 0, 0, 0))],
        out_shape=[jax.ShapeDtypeStruct((t, 512), BF16), jax.ShapeDtypeStruct((t, 512), F32),
                   jax.ShapeDtypeStruct((nc, 4, LANES, LANES), F32)],
        scratch_shapes=[pltpu.VMEM((4, LANES, LANES), F32)],
        compiler_params=_cparams(("arbitrary",)),
    )(act, dt32, p, dtt, hp_rows, hp_cols, d_e, norm_w)


def ssd_bwd(dyc, y, act, p, dt32, dtt, states, hp_rows, hp_cols, d_e, norm_w, name):
    t = act.shape[0]
    ln = SSM_CHUNK
    nc = t // ln

    def body(dyc_ref, y_ref, xbc_ref, dt_ref, z_ref, dtt_ref, st_ref, hpr_ref, hpc_ref, d_ref, nw_ref,
             dz_ref, dact_ref, ddt_ref, dnw_ref, dd_ref, dhp_ref, ds):
        @pl.when(pl.program_id(0) == 0)
        def _():
            ds[...] = jnp.zeros_like(ds)
            dnw_ref[...] = jnp.zeros_like(dnw_ref)
            dd_ref[...] = jnp.zeros_like(dd_ref)
            dhp_ref[...] = jnp.zeros_like(dhp_ref)

        ri, ci, expand, reduce = _ssd_consts()
        q = _ssd_prelude(xbc_ref, dt_ref, dtt_ref, hpr_ref, hpc_ref, ri, ci, expand)
        lane = lax.broadcasted_iota(jnp.int32, (1, LANES), 1)
        rown = lax.broadcasted_iota(jnp.int32, (LANES, 1), 0)
        low = lane < 64
        mask = ri >= ci
        mask_t = ci >= ri
        xs, xdt, acs, acst = q["xs"], q["xdt"], q["acs"], q["acst"]
        yv, zv, nw = y_ref[...], z_ref[...].astype(F32), nw_ref[...]
        sg = _sigmoid(zv)
        zz = zv * sg
        yg = yv * zz
        dycv = dyc_ref[...]
        u = dycv * nw
        dyg_parts, dnw_parts = [], []
        for g in range(2):
            sl = slice(256 * g, 256 * (g + 1))
            seg = yg[:, sl]
            rr = _rstd(seg)
            nrm = seg * rr
            dyg_parts.append(rr * (u[:, sl] - nrm * jnp.mean(nrm * u[:, sl], axis=-1, keepdims=True)))
            dnw_parts.append(jnp.sum(dycv[:, sl] * nrm, axis=0, keepdims=True))
        dyg = jnp.concatenate(dyg_parts, axis=1)
        dnw_ref[...] += jnp.concatenate(dnw_parts, axis=1)
        dy = dyg * zz
        dz_ref[...] = dyg * yv * (sg * (1.0 + zv * (1.0 - sg)))
        dd_ref[...] += jnp.sum(dy * xs, axis=0, keepdims=True)
        dxs = dy * d_ref[...]
        dy_b = dy.astype(BF16)
        xdt_b = xdt.astype(BF16)
        xw_b = (xdt * q["w_e"]).astype(BF16)
        bt = q["bm"].T.astype(BF16)
        ct = q["cm"].T.astype(BF16)
        cb_ = q["cm"].astype(BF16)
        bb_ = q["bm"].astype(BF16)
        dacs = jnp.zeros((ln, LANES), F32)
        dc = jnp.zeros((ln, LANES), F32)
        db = jnp.zeros((ln, LANES), F32)
        dxdt_pairs, yoffdy_pairs, dwe_pairs, ddec_pairs = [], [], [], []
        for g in range(2):
            gm = low if g == 0 else jnp.logical_not(low)
            rm = (rown < 64) if g == 0 else (rown >= 64)
            cg = jnp.where(gm, cb_, jnp.zeros_like(cb_))
            bg = jnp.where(gm, bb_, jnp.zeros_like(bb_))
            cb = jnp.dot(cg, bt, preferred_element_type=F32)
            cbt = jnp.dot(bg, ct, preferred_element_type=F32)
            dcb = jnp.zeros((ln, ln), F32)
            dcbt = jnp.zeros((ln, ln), F32)
            for pp in range(2):
                pi = 2 * g + pp
                sl = slice(LANES * pi, LANES * (pi + 1))
                xp = xdt_b[:, sl]
                dyp = dy_b[:, sl]
                xpt = xdt[:, sl].T.astype(BF16)
                dypt = dy[:, sl].T.astype(BF16)
                dxdt_p = jnp.zeros((ln, LANES), F32)
                for hh in range(2):
                    h = 2 * pi + hh
                    hm = low if hh == 0 else jnp.logical_not(low)
                    col = acs[:, h:h + 1]
                    row = acst[h:h + 1, :]
                    lam = jnp.exp(jnp.where(mask, col - row, -jnp.inf))
                    lam_t = jnp.exp(jnp.where(mask_t, row - col, -jnp.inf))
                    m = cb * lam
                    m_t = cbt * lam_t
                    dyh = jnp.where(hm, dyp, jnp.zeros_like(dyp))
                    xh = jnp.where(hm, xp, jnp.zeros_like(xp))
                    dm = jnp.dot(dyh, xpt, preferred_element_type=F32)
                    dm_t = jnp.dot(xh, dypt, preferred_element_type=F32)
                    dcb = dcb + dm * lam
                    dcbt = dcbt + dm_t * lam_t
                    rs = jnp.sum(dm * m, axis=1, keepdims=True) - jnp.sum(dm_t * m_t, axis=1, keepdims=True)
                    dacs = dacs + jnp.where(lane == h, rs, 0.0)
                    dxdt_p = dxdt_p + jnp.dot(m_t.astype(BF16), dyh, preferred_element_type=F32)
                sp = st_ref[0, pi]
                sp_b = sp.astype(BF16)
                dsn = ds[pi]
                dsn_b = dsn.astype(BF16)
                e_p, w_p, dec_p = q["e_e"][:, sl], q["w_e"][:, sl], q["dec_e"][:, sl]
                yoff = jnp.dot(cg, sp_b, preferred_element_type=F32) * e_p
                dyo = dy[:, sl] * e_p
                dyo_b = dyo.astype(BF16)
                dc = dc + lax.dot_general(dyo_b, sp_b, _NT, preferred_element_type=F32)
                ds_prev = dec_p * dsn + jnp.where(rm, jnp.dot(ct, dyo_b, preferred_element_type=F32), 0.0)
                yoffdy_pairs.append(dy[:, sl] * yoff)
                dxw = jnp.dot(bg, dsn_b, preferred_element_type=F32)
                db = db + lax.dot_general(xw_b[:, sl], dsn_b, _NT, preferred_element_type=F32)
                dxdt_p = dxdt_p + dxw * w_p
                dwe_pairs.append(dxw * xdt[:, sl])
                ddec_pairs.append(jnp.sum(dsn * sp, axis=0, keepdims=True))
                ds[pi] = ds_prev
                dxdt_pairs.append(dxdt_p)
            dc = dc + jnp.dot(dcb.astype(BF16), bg, preferred_element_type=F32)
            db = db + jnp.dot(dcbt.astype(BF16), cg, preferred_element_type=F32)
        dxdt = jnp.concatenate(dxdt_pairs, axis=1)
        yoffdy = jnp.concatenate(yoffdy_pairs, axis=1)
        dwe = jnp.concatenate(dwe_pairs, axis=1)
        ddec_e = jnp.broadcast_to(jnp.concatenate(ddec_pairs, axis=1), (8, SSM_INNER))
        last = acs[ln - 1:ln, :]
        w_col = jnp.exp(last - acs)
        dw_col = _dot_hi(dwe, reduce) * w_col
        dacs = dacs + _dot_hi(yoffdy, reduce) - dw_col
        dlast = jnp.sum(dw_col, axis=0, keepdims=True) + jnp.exp(last) * _dot_hi(ddec_e, reduce)[0:1, :]
        rowi = lax.broadcasted_iota(jnp.int32, (ln, 1), 0)
        dacs = dacs + jnp.where(rowi == ln - 1, dlast, 0.0)
        da = _dot_hi(q["triu"], dacs)
        ddt = da * q["aneg_r"] + _dot_hi(dxdt * xs, reduce)
        ddt_raw = jnp.where(lane < SSM_HEADS, ddt * _sigmoid(q["pre"]), 0.0)
        ddt_ref[...] = ddt_raw
        dhp_ref[0:1, :] += jnp.sum(ddt_raw, axis=0, keepdims=True)
        dhp_ref[1:2, :] += jnp.where(lane < SSM_HEADS, jnp.sum(da * q["dt"], axis=0, keepdims=True) * q["aneg_r"], 0.0)
        dact_ref[:, 0:512] = dxs + dxdt * q["dt_e"]
        dact_ref[:, 512:640] = db
        dact_ref[:, 640:768] = dc

    rev = lambda c: nc - 1 - c
    return pl.pallas_call(
        body, name=name, grid=(nc,),
        in_specs=[pl.BlockSpec((ln, 512), lambda c: (rev(c), 0)), pl.BlockSpec((ln, 512), lambda c: (rev(c), 0)),
                  pl.BlockSpec((ln, 768), lambda c: (rev(c), 0)),
                  pl.BlockSpec((ln, LANES), lambda c: (rev(c), 0)),
                  pl.BlockSpec((ln, 512), lambda c: (rev(c), OFF_Z // 512)), pl.BlockSpec((8, ln), lambda c: (0, rev(c))),
                  pl.BlockSpec((1, 4, LANES, LANES), lambda c: (rev(c), 0, 0, 0)),
                  pl.BlockSpec((8, LANES), lambda c: (0, 0)), pl.BlockSpec((16, ln), lambda c: (0, 0)),
                  pl.BlockSpec((1, 512), lambda c: (0, 0)), pl.BlockSpec((1, 512), lambda c: (0, 0))],
        out_specs=[pl.BlockSpec((ln, 512), lambda c: (rev(c), 0)), pl.BlockSpec((ln, 768), lambda c: (rev(c), 0)),
                   pl.BlockSpec((ln, LANES), lambda c: (rev(c), 0)), pl.BlockSpec((1, 512), lambda c: (0, 0)),
                   pl.BlockSpec((1, 512), lambda c: (0, 0)), pl.BlockSpec((8, LANES), lambda c: (0, 0))],
        out_shape=[jax.ShapeDtypeStruct((t, 512), F32), jax.ShapeDtypeStruct((t, 768), F32),
                   jax.ShapeDtypeStruct((t, LANES), F32), jax.ShapeDtypeStruct((1, 512), F32),
                   jax.ShapeDtypeStruct((1, 512), F32), jax.ShapeDtypeStruct((8, LANES), F32)],
        scratch_shapes=[pltpu.VMEM((4, LANES, LANES), F32)],
        compiler_params=_cparams(("arbitrary",)),
    )(dyc, y, act, dt32, p, dtt, states, hp_rows, hp_cols, d_e, norm_w)


def mod_shard_fwd(c_all, mod_w, mod_b_shard, name):
    def body(c_ref, w_ref, b_ref, o_ref):
        sc = _silu(c_ref[...])
        for l in range(DEPTH):
            o_ref[l] = _dot_f32(sc, w_ref[l]) + b_ref[l]

    return pl.pallas_call(body, name=name, out_shape=jax.ShapeDtypeStruct((DEPTH, N_DEV, mod_w.shape[2]), F32),
                          compiler_params=_cparams())(c_all, mod_w, mod_b_shard)


def mod_w_grad(c_all, dmod_shard, name):
    def body(c_ref, d_ref, o_ref):
        sc = _silu(c_ref[...])
        for l in range(DEPTH):
            o_ref[l] = lax.dot_general(sc, d_ref[l], _TN, precision=HI, preferred_element_type=F32)

    return pl.pallas_call(body, name=name, out_shape=jax.ShapeDtypeStruct((DEPTH, D_MODEL, dmod_shard.shape[2]), F32),
                          compiler_params=_cparams())(c_all, dmod_shard)


_BIG = ("w_in", "sc_conv_w", "ssm_conv_w", "w_sc_out", "w_sb_out", "w_ssm_out", "w_o", "w_ffn_in", "w_ffn_out")
_ROW_SHARDED = ("w_o", "w_ffn_out")
_CONV = ("sc_conv_w", "ssm_conv_w")
_SMALL = ("mod_b", "g_pre_mix", "g_post_mix", "g_pre_ffn", "g_post_ffn", "ssm_conv_b", "ssm_dt_bias", "ssm_a_log",
          "ssm_d", "ssm_norm_w")
_WEIGHTS = ("mod_w", "mod_b", "g_pre_mix", "g_post_mix", "g_pre_ffn", "g_post_ffn", "w_in", "sc_conv_w", "ssm_conv_w",
            "ssm_conv_b", "ssm_dt_bias", "ssm_a_log", "ssm_d", "ssm_norm_w", "w_sc_out", "w_sb_out", "w_ssm_out", "w_o",
            "w_ffn_in", "w_ffn_out")


def _gathered_to_full(g, row_sharded):
    _, dep, r, c = g.shape
    if row_sharded:
        return g.transpose(1, 0, 2, 3).reshape(dep, N_DEV * r, c)
    return jnp.concatenate([g[d] for d in range(N_DEV)], axis=2)


def _full_to_slots(w, row_sharded):
    dep, r, c = w.shape
    if row_sharded:
        return w.reshape(dep, N_DEV, r // N_DEV, c).transpose(1, 0, 2, 3).reshape(N_DEV, dep * (r // N_DEV), c)
    return w.reshape(dep, r, N_DEV, c // N_DEV).transpose(2, 0, 1, 3).reshape(N_DEV, dep * r, c // N_DEV)


def _pad_in_proj(shards):
    width = shards.shape[2]

    def cols(lo, hi):
        out = []
        while lo < hi:
            dev, a = divmod(lo, width)
            b = min(width, a + hi - lo)
            out.append(shards[dev, :, a:b])
            lo += b - a
        return out

    pad = jnp.zeros((shards.shape[1], OFF_Z - OFF_DT - 8), shards.dtype)
    return jnp.concatenate(cols(2824, 5896) + cols(0, 768) + cols(768, 1536) + cols(2048, 2816) + cols(2816, 2824)
                           + [pad] + cols(1536, 2048), axis=1)


def _in_proj_slots(dw_layers):
    width = IN_PROJ // N_DEV
    segments = ((0, 768, OFF_SC), (768, 1536, OFF_QKV), (1536, 2048, OFF_Z), (2048, 2816, OFF_XBC),
                (2816, 2824, OFF_DT), (2824, IN_PROJ, OFF_GATES))

    def internal(lo, hi):
        out = []
        for s0, s1, off in segments:
            a, b = max(lo, s0), min(hi, s1)
            if a < b:
                out.append((off + a - s0, off + b - s0))
        return out

    slots = []
    for d in range(N_DEV):
        pieces = internal(width * d, width * (d + 1))
        slots.append(jnp.concatenate([jnp.concatenate([w[:, a:b] for a, b in pieces], axis=1) for w in dw_layers], axis=0))
    return jnp.stack(slots)


def _ffn_in_keep_give(halves, cidx):
    width = 2 * FFN_HIDDEN // N_DEV
    keep, give = [], []
    for chip in range(4):
        src = 0 if chip < 2 else 1
        for out, core in ((keep, cidx), (give, 1 - cidx)):
            col0 = width * ((2 * chip) % 4 + core)
            out.append(jnp.concatenate([lax.dynamic_slice_in_dim(h[src], col0, width, axis=1) for h in halves], axis=0))
    return jnp.stack(keep), jnp.stack(give)


def _row(v):
    return v.reshape(1, -1)


def _local_step(x, target, mod, small, conv, big):
    lw, saved = [], []
    for l in range(DEPTH):
        w_in_p = _pad_in_proj(big["w_in_shards"][:, l])
        w_cat = jnp.concatenate([big["w_sc_out"][l], big["w_sb_out"][l], big["w_ssm_out"][l]], axis=0)
        hp_rows = jnp.zeros((8, LANES), F32).at[0, :SSM_HEADS].set(small["ssm_dt_bias"][l]).at[1, :SSM_HEADS].set(
            small["ssm_a_log"][l])
        hp_cols = jnp.concatenate([jnp.broadcast_to(small["ssm_dt_bias"][l][:, None], (SSM_HEADS, SSM_CHUNK)),
                                   jnp.broadcast_to(small["ssm_a_log"][l][:, None], (SSM_HEADS, SSM_CHUNK))], axis=0)
        lw.append(dict(
            w_in_p=w_in_p, w_cat=w_cat, w_o=big["w_o"][l], w_ffn_in=big["w_ffn_in"][l], w_ffn_out=big["w_ffn_out"][l],
            sc_w8=jnp.pad(conv["sc_conv_w"][l], ((0, 5), (0, 0))), ssm_w8=jnp.pad(conv["ssm_conv_w"][l], ((0, 4), (0, 0))),
            ssm_b=_row(small["ssm_conv_b"][l]), hp_rows=hp_rows, hp_cols=hp_cols,
            d_e=_row(jnp.repeat(small["ssm_d"][l], SSM_HEAD_DIM)), norm_w=_row(small["ssm_norm_w"][l]),
            g_pre_mix=_row(small["g_pre_mix"][l]), g_post_mix=_row(small["g_post_mix"][l]),
            g_pre_ffn=_row(small["g_pre_ffn"][l]), g_post_ffn=_row(small["g_post_ffn"][l]),
            shift1=mod[l, 0:1], scale1=mod[l, 1:2], gate1=mod[l, 2:3], shift2=mod[l, 3:4], scale2=mod[l, 4:5],
            gate2=mod[l, 5:6]))

    xl = x
    h = normmod_fwd(xl, lw[0]["g_pre_mix"], lw[0]["scale1"], lw[0]["shift1"], "normmod_fwd_0")
    dy = loss = None
    for l in range(DEPTH):
        w = lw[l]
        p = mm_nn([h], [w["w_in_p"]], BF16, f"in_proj_{l}")
        dt32 = mm_nn([h], [w["w_in_p"][:, OFF_DT:OFF_DT + LANES]], F32, f"in_proj_dt_{l}")
        ya, qkv, act = post_inproj(p, w["sc_w8"], w["ssm_w8"], w["ssm_b"], f"post_inproj_{l}")
        o, runs = sba_fwd(qkv, f"sba_fwd_{l}")
        dtt = dt32[:, :SSM_HEADS].T
        yc, ypre, states = ssd_fwd(act, p, dt32, dtt, w["hp_rows"], w["hp_cols"], w["d_e"], w["norm_w"], f"ssd_fwd_{l}")
        merged = branch_out_fwd(ya, o, yc, p, w["w_cat"], f"branch_fwd_{l}")
        mix = mm_nn([merged], [w["w_o"]], F32, f"out_proj_{l}")
        x1, h2 = resid_normmod_fwd(xl, mix, w["gate1"], w["g_post_mix"], w["g_pre_ffn"], w["scale2"], w["shift2"],
                                   f"resid_mix_{l}")
        gt, up, a = mm_swiglu_fwd(h2, w["w_ffn_in"], f"ffn_in_{l}")
        f = mm_nn([a], [w["w_ffn_out"]], F32, f"ffn_out_{l}")
        saved.append(dict(x=xl, h=h, p=p, ya=ya, qkv=qkv, act=act, o=o, runs=runs, dt32=dt32, dtt=dtt, yc=yc, ypre=ypre, states=states,
                          merged=merged, mix=mix, x1=x1, h2=h2, gt=gt, up=up, a=a, f=f))
        if l + 1 < DEPTH:
            nw = lw[l + 1]
            xl, h = resid_normmod_fwd(x1, f, w["gate2"], w["g_post_ffn"], nw["g_pre_mix"], nw["scale1"], nw["shift1"],
                                      f"resid_ffn_{l}")
        else:
            dy, loss = resid_loss(x1, f, w["gate2"], w["g_post_ffn"], target, "resid_loss")

    dmod = [None] * DEPTH
    gs = {k: [None] * DEPTH for k in _SMALL + _BIG}
    dxo = dy
    top, stl = lw[DEPTH - 1], saved[DEPTH - 1]
    df, dgate2, gs["g_post_ffn"][DEPTH - 1] = resid_bwd(dxo, stl["f"], top["gate2"], top["g_post_ffn"],
                                                        f"resid_ffn_bwd_{DEPTH - 1}")
    for l in reversed(range(DEPTH)):
        w, s = lw[l], saved[l]
        dgt, dup = mm_swiglu_bwd(df, w["w_ffn_out"], s["gt"], s["up"], f"ffn_out_bwd_{l}")
        gs["w_ffn_out"][l] = mm_tn(s["a"], df, f"dw_ffn_out_{l}")
        dh2 = mm_nt([dgt, dup], [w["w_ffn_in"], w["w_ffn_in"]], [0, FFN_HIDDEN], F32, f"ffn_in_bwd_{l}")
        gs["w_ffn_in"][l] = (mm_tn(s["h2"], dgt, f"dw_ffn_gate_{l}"), mm_tn(s["h2"], dup, f"dw_ffn_up_{l}"))
        dx1, dmix, dscale2, dshift2, gs["g_pre_ffn"][l], dgate1, gs["g_post_mix"][l] = normmod_resid_bwd(
            dh2, s["x1"], dxo, w["g_pre_ffn"], w["scale2"], s["mix"], w["gate1"], w["g_post_mix"], f"ffn_mix_bwd_{l}")
        dmerged = mm_nt([dmix], [w["w_o"]], [0], F32, f"out_proj_bwd_{l}")
        gs["w_o"][l] = mm_tn(s["merged"], dmix, f"dw_o_{l}")
        dp_gates, dya, dyb, dyc, dw_cat = branch_out_bwd(dmerged, s["ya"], s["o"], s["yc"], s["p"], w["w_cat"],
                                                         f"branch_bwd_{l}")
        gs["w_sc_out"][l], gs["w_sb_out"][l], gs["w_ssm_out"][l] = dw_cat[0:256], dw_cat[256:512], dw_cat[512:1024]
        dz, dact, ddt, dnw, dd_e, dhp = ssd_bwd(dyc, s["ypre"], s["act"], s["p"], s["dt32"], s["dtt"], s["states"], w["hp_rows"],
                                                w["hp_cols"], w["d_e"], w["norm_w"], f"ssd_bwd_{l}")
        gs["ssm_norm_w"][l] = dnw[0]
        gs["ssm_d"][l] = dd_e.reshape(SSM_HEADS, SSM_HEAD_DIM).sum(axis=1)
        gs["ssm_dt_bias"][l] = dhp[0, :SSM_HEADS]
        gs["ssm_a_log"][l] = dhp[1, :SSM_HEADS]
        dq, dk, dv = sba_bwd(s["qkv"], s["runs"], dyb, f"sba_bwd_{l}")
        dp, dscw, dssw, dssb = assemble_dp(dp_gates, dya, s["p"], w["sc_w8"], dq, dk, dv, dact, w["ssm_w8"], w["ssm_b"], ddt,
                                           dz, f"assemble_dp_{l}")
        gs["sc_conv_w"][l], gs["ssm_conv_w"][l], gs["ssm_conv_b"][l] = dscw[0:3], dssw[0:4], dssb[0]
        dh = mm_nt([dp], [w["w_in_p"]], [0], F32, f"in_proj_bwd_{l}")
        gs["w_in"][l] = mm_tn(s["h"], dp, f"dw_in_{l}")
        if l > 0:
            below, sb = lw[l - 1], saved[l - 1]
            dxo, df, dscale1, dshift1, gs["g_pre_mix"][l], dgate2_below, gs["g_post_ffn"][l - 1] = normmod_resid_bwd(
                dh, s["x"], dx1, w["g_pre_mix"], w["scale1"], sb["f"], below["gate2"], below["g_post_ffn"],
                f"mix_ffn_bwd_{l}")
        else:
            dxo, dscale1, dshift1, gs["g_pre_mix"][l] = normmod_bwd(dh, s["x"], dx1, w["g_pre_mix"], w["scale1"],
                                                                    f"normmod_mix_bwd_{l}")
            dgate2_below = None
        dmod[l] = jnp.concatenate([dshift1, dscale1, dgate1, dshift2, dscale2, dgate2], axis=0)
        dgate2 = dgate2_below
    for k in ("g_pre_mix", "g_post_mix", "g_pre_ffn", "g_post_ffn"):
        gs[k] = [g[0] for g in gs[k]]
    per_layer = ("w_in", "w_ffn_in")
    grads = {k: (v if k in per_layer else jnp.stack(v)) for k, v in gs.items() if k != "mod_b"}
    return loss[0, 0], dxo, jnp.stack(dmod), grads


def kernel(x, c, mod_w, mod_b, g_pre_mix, g_post_mix, g_pre_ffn, g_post_ffn, w_in, sc_conv_w, ssm_conv_w, ssm_conv_b, ssm_dt_bias, ssm_a_log, ssm_d, ssm_norm_w, w_sc_out, w_sb_out, w_ssm_out, w_o, w_ffn_in, w_ffn_out, loss_target, m_mod_w, m_mod_b, m_g_pre_mix, m_g_post_mix, m_g_pre_ffn, m_g_post_ffn, m_w_in, m_sc_conv_w, m_ssm_conv_w, m_ssm_conv_b, m_ssm_dt_bias, m_ssm_a_log, m_ssm_d, m_ssm_norm_w, m_w_sc_out, m_w_sb_out, m_w_ssm_out, m_w_o, m_w_ffn_in, m_w_ffn_out, v_mod_w, v_mod_b, v_g_pre_mix, v_g_post_mix, v_g_pre_ffn, v_g_post_ffn, v_w_in, v_sc_conv_w, v_ssm_conv_w, v_ssm_conv_b, v_ssm_dt_bias, v_ssm_a_log, v_ssm_d, v_ssm_norm_w, v_w_sc_out, v_w_sb_out, v_w_ssm_out, v_w_o, v_w_ffn_in, v_w_ffn_out):
    wts = dict(mod_w=mod_w, mod_b=mod_b, g_pre_mix=g_pre_mix, g_post_mix=g_post_mix, g_pre_ffn=g_pre_ffn,
               g_post_ffn=g_post_ffn, w_in=w_in, sc_conv_w=sc_conv_w, ssm_conv_w=ssm_conv_w, ssm_conv_b=ssm_conv_b,
               ssm_dt_bias=ssm_dt_bias, ssm_a_log=ssm_a_log, ssm_d=ssm_d, ssm_norm_w=ssm_norm_w, w_sc_out=w_sc_out,
               w_sb_out=w_sb_out, w_ssm_out=w_ssm_out, w_o=w_o, w_ffn_in=w_ffn_in, w_ffn_out=w_ffn_out)
    ms = dict(mod_w=m_mod_w, mod_b=m_mod_b, g_pre_mix=m_g_pre_mix, g_post_mix=m_g_post_mix, g_pre_ffn=m_g_pre_ffn,
              g_post_ffn=m_g_post_ffn, w_in=m_w_in, sc_conv_w=m_sc_conv_w, ssm_conv_w=m_ssm_conv_w,
              ssm_conv_b=m_ssm_conv_b, ssm_dt_bias=m_ssm_dt_bias, ssm_a_log=m_ssm_a_log, ssm_d=m_ssm_d,
              ssm_norm_w=m_ssm_norm_w, w_sc_out=m_w_sc_out, w_sb_out=m_w_sb_out, w_ssm_out=m_w_ssm_out, w_o=m_w_o,
              w_ffn_in=m_w_ffn_in, w_ffn_out=m_w_ffn_out)
    vs = dict(mod_w=v_mod_w, mod_b=v_mod_b, g_pre_mix=v_g_pre_mix, g_post_mix=v_g_post_mix, g_pre_ffn=v_g_pre_ffn,
              g_post_ffn=v_g_post_ffn, w_in=v_w_in, sc_conv_w=v_sc_conv_w, ssm_conv_w=v_ssm_conv_w,
              ssm_conv_b=v_ssm_conv_b, ssm_dt_bias=v_ssm_dt_bias, ssm_a_log=v_ssm_a_log, ssm_d=v_ssm_d,
              ssm_norm_w=v_ssm_norm_w, w_sc_out=v_w_sc_out, w_sb_out=v_w_sb_out, w_ssm_out=v_w_ssm_out, w_o=v_w_o,
              w_ffn_in=v_w_ffn_in, w_ffn_out=v_w_ffn_out)
    me = 4 * lax.axis_index("x") + 2 * lax.axis_index("y") + lax.axis_index("c")
    mod_cols = mod_w.shape[2]

    pack1, sizes1 = _pack_rows([c, sc_conv_w, ssm_conv_w], F32, 8)
    got1 = all_gather_multi([pack1], "gather_c_conv")[0].reshape(N_DEV, -1)
    c_all, sc_g, ssm_g = _unpack(got1, sizes1, [(D_MODEL,), sc_conv_w.shape, ssm_conv_w.shape])
    conv = dict(sc_conv_w=_gathered_to_full(sc_g, False), ssm_conv_w=_gathered_to_full(ssm_g, False))

    mod_b_shard = lax.dynamic_slice_in_dim(mod_b, me * mod_cols, mod_cols, axis=1).reshape(DEPTH, 1, mod_cols)
    mod_sh = mod_shard_fwd(c_all, mod_w, mod_b_shard, "mod_shard_fwd")
    pack2, sizes2 = _pack_rows([mod_sh], F32, 8)
    got2 = all_gather_multi([pack2], "gather_mod")[0].reshape(N_DEV, -1)
    mod_all = _unpack(got2, sizes2, [mod_sh.shape])[0]
    mod_mine = lax.dynamic_index_in_dim(mod_all, me, axis=2, keepdims=False)
    mod = mod_mine.transpose(1, 0, 2).reshape(DEPTH, 6, D_MODEL)

    mm_names = [k for k in _BIG if k not in _CONV]
    gathered = all_gather_multi([wts[k].astype(BF16) for k in mm_names], "gather_weights")
    big = {k: _gathered_to_full(g, k in _ROW_SHARDED) for k, g in zip(mm_names, gathered) if k != "w_in"}
    big["w_in_shards"] = gathered[mm_names.index("w_in")]

    small = {k: wts[k] for k in _SMALL}
    loss_part, dx, dmod, grads = _local_step(x[0], loss_target[0], mod, small, conv, big)
    loss = lax.psum(loss_part, ("x", "y", "c"))

    small_parts = [dmod.reshape(DEPTH, 6 * D_MODEL)] + [grads[k] for k in _SMALL[1:]]
    pack5, sizes5 = _pack_rows(small_parts, F32, 8)
    pack_conv, sizes_conv = _pack_rows([grads[k] for k in _CONV], F32, 8)
    got5, got_conv = all_gather_multi([pack5, pack_conv], "gather_small_grads")
    w5, _ = _pack_rows([wts[k] for k in _SMALL], F32, 8)
    m5, _ = _pack_rows([ms[k] for k in _SMALL], F32, 8)
    v5, _ = _pack_rows([vs[k] for k in _SMALL], F32, 8)
    res5 = adamw_flat(got5, w5, m5, v5, "adamw_small")
    small_out = [_unpack(r.reshape(-1), sizes5, [wts[k].shape for k in _SMALL]) for r in res5]

    conv_full = _unpack(got_conv.reshape(N_DEV, -1), sizes_conv, [grads[k].shape for k in _CONV])
    conv_mine = [lax.dynamic_slice_in_dim(g, me * wts[k].shape[2], wts[k].shape[2], axis=3)
                 for k, g in zip(_CONV, conv_full)]
    conv_slot_sizes = [math.prod(wts[k].shape) for k in _CONV]
    conv_slots = jnp.concatenate([g.reshape(N_DEV, -1) for g in conv_mine], axis=1)
    pad_c = -conv_slots.shape[1] % (8 * LANES)
    conv_slots = jnp.pad(conv_slots, ((0, 0), (0, pad_c))).reshape(N_DEV, -1, LANES)
    wc, _ = _pack_rows([wts[k] for k in _CONV], F32, 8)
    mc, _ = _pack_rows([ms[k] for k in _CONV], F32, 8)
    vc, _ = _pack_rows([vs[k] for k in _CONV], F32, 8)
    res_c = adamw_flat(conv_slots, wc, mc, vc, "adamw_conv")
    conv_out = [_unpack(r.reshape(-1), conv_slot_sizes, [wts[k].shape for k in _CONV]) for r in res_c]

    dmod_all = got5.reshape(N_DEV, -1)[:, :DEPTH * 6 * D_MODEL].reshape(N_DEV, DEPTH, 6 * D_MODEL)
    dmod_shard = lax.dynamic_slice_in_dim(dmod_all, me * mod_cols, mod_cols, axis=2).transpose(1, 0, 2)
    g_mod_w = mod_w_grad(c_all, dmod_shard, "mod_w_grad")
    rows2 = lambda a: a.reshape(a.shape[0] * a.shape[1], a.shape[2])
    res_mw = adamw_flat(rows2(g_mod_w)[None], rows2(mod_w), rows2(m_mod_w), rows2(v_mod_w), "adamw_mod_w")
    mod_w_out = [r.reshape(mod_w.shape) for r in res_mw]

    cidx = lax.axis_index("c")
    keeps, gives = [], []
    for k in mm_names:
        if k == "w_ffn_in":
            keep, give = _ffn_in_keep_give(grads[k], cidx)
        else:
            slots = (_in_proj_slots(grads[k]) if k == "w_in"
                     else _full_to_slots(grads[k].astype(BF16), k in _ROW_SHARDED))
            by_chip = slots.reshape(4, 2, *slots.shape[1:])
            keep = lax.dynamic_index_in_dim(by_chip, cidx, 1, keepdims=False)
            give = lax.dynamic_index_in_dim(by_chip, 1 - cidx, 1, keepdims=False)
        keeps.append(keep)
        gives.append(give)
    gots = swap_with_sibling(gives, "swap_grads")
    pairs = []
    for k, keep, got in zip(mm_names, keeps, gots):
        rows4 = (4 * keep.shape[1], keep.shape[2])
        pairs.append(add_pairs(keep.reshape(rows4), got.reshape(rows4), f"add_pairs_{k}").reshape(keep.shape))
    recvs = exchange_chips(pairs, "exchange_grads")
    big_out = {}
    for k, recv in zip(mm_names, recvs):
        res = adamw_flat(recv, rows2(wts[k]), rows2(ms[k]), rows2(vs[k]), f"adamw_{k}")
        big_out[k] = [r.reshape(wts[k].shape) for r in res]

    outs = []
    for kind in range(4):
        by_name = {"mod_w": mod_w_out[kind]}
        by_name.update(zip(_SMALL, small_out[kind]))
        by_name.update(zip(_CONV, conv_out[kind]))
        by_name.update({k: v[kind] for k, v in big_out.items()})
        outs.extend(by_name[k] for k in _WEIGHTS)
    return (loss, dx[None], *outs)
```

```python
import math

import jax
import jax.numpy as jnp
from jax import lax
from jax.experimental import pallas as pl
from jax.experimental.pallas import tpu as pltpu

F32 = jnp.float32
BF16 = jnp.bfloat16
HI = lax.Precision.HIGHEST

N_DEV = 8
D_MODEL = 1024
DEPTH = 2
SC_WIDTH = 256
SB_WIDTH = 256
SB_HEAD_DIM = 64
SSM_INNER = 512
SSM_HEADS = 8
SSM_HEAD_DIM = 64
SSM_GROUPS = 2
SSM_STATE = 64
SSM_CHUNK = 256
SSM_CONV_DIM = 768
FFN_HIDDEN = 2816
NORM_EPS = 1e-6
IN_PROJ = 5896
LANES = 128
VMEM_LIMIT = 56 * 1024 * 1024

OFF_GATES = 0
OFF_SC = 3072
OFF_QKV = 3840
OFF_XBC = 4608
OFF_DT = 5376
OFF_Z = 5632
IN_PAD = 6144

ADAM_LR = 0.001
ADAM_B1 = 0.9
ADAM_B2 = 0.999
ADAM_EPS = 1e-08
ADAM_WD = 0.01
ADAM_STEP = 10

MESH_ID = pl.DeviceIdType.MESH


def _cparams(sem=None):
    return pltpu.CompilerParams(dimension_semantics=sem, vmem_limit_bytes=VMEM_LIMIT)


def _my_pos():
    return lax.axis_index("x"), lax.axis_index("y"), lax.axis_index("c")


def all_gather_multi(blocks, name):
    n = len(blocks)

    def body(*refs):
        x_refs, o_refs = refs[:n], refs[n:2 * n]
        send_sems, recv_sems, local_sems = refs[2 * n:]
        x, y, c = _my_pos()
        me, sibling = (x, y, c), (x, y, 1 - c)
        chips = [(1 - x, y), (x, 1 - y), (1 - x, 1 - y)]

        def slot(a, px, py, pc):
            return o_refs[a].at[4 * px + 2 * py + pc]

        def copy(a, k, blk, to, src=None):
            return pltpu.make_async_remote_copy(
                src_ref=slot(a, *blk) if src is None else src, dst_ref=slot(a, *blk),
                send_sem=send_sems.at[7 * a + k], recv_sem=recv_sems.at[7 * a + k], device_id=to, device_id_type=MESH_ID)

        mine = [pltpu.make_async_copy(x_refs[a], slot(a, *me), local_sems.at[a]) for a in range(n)]
        for cp in mine:
            cp.start()
        first = [copy(a, 1 + j, me, (*chip, c), src=x_refs[a]) for j, chip in enumerate(chips) for a in range(n)]
        first += [copy(a, 0, me, sibling, src=x_refs[a]) for a in range(n)]
        for cp in first:
            cp.start()
        passed = []
        for j, chip in enumerate(chips):
            for a in range(n):
                copy(a, 1 + j, (*chip, c), me).wait_recv()
                fwd = copy(a, 4 + j, (*chip, c), sibling)
                fwd.start()
                passed.append(fwd)
        for a in range(n):
            copy(a, 0, sibling, me).wait_recv()
            for j, chip in enumerate(chips):
                copy(a, 4 + j, (*chip, 1 - c), me).wait_recv()
        for cp in first + passed:
            cp.wait_send()
        for cp in mine:
            cp.wait()

    any_spec = pl.BlockSpec(memory_space=pl.ANY)
    return pl.pallas_call(
        body, name=name,
        out_shape=[jax.ShapeDtypeStruct((N_DEV,) + b.shape, b.dtype) for b in blocks],
        in_specs=[any_spec] * n, out_specs=[any_spec] * n,
        scratch_shapes=[pltpu.SemaphoreType.DMA((7 * n,)), pltpu.SemaphoreType.DMA((7 * n,)), pltpu.SemaphoreType.DMA((n,))],
    )(*blocks)


def swap_with_sibling(gives, name):
    n = len(gives)

    def body(*refs):
        g_refs, r_refs = refs[:n], refs[n:2 * n]
        send_sems, recv_sems = refs[2 * n:]
        x, y, c = _my_pos()
        copies = [pltpu.make_async_remote_copy(
            src_ref=g_refs[a], dst_ref=r_refs[a], send_sem=send_sems.at[a], recv_sem=recv_sems.at[a],
            device_id=(x, y, 1 - c), device_id_type=MESH_ID) for a in range(n)]
        for cp in copies:
            cp.start()
        for cp in copies:
            cp.wait_recv()
        for cp in copies:
            cp.wait_send()

    any_spec = pl.BlockSpec(memory_space=pl.ANY)
    return pl.pallas_call(
        body, name=name, out_shape=[jax.ShapeDtypeStruct(g.shape, g.dtype) for g in gives],
        in_specs=[any_spec] * n, out_specs=[any_spec] * n,
        scratch_shapes=[pltpu.SemaphoreType.DMA((n,)), pltpu.SemaphoreType.DMA((n,))],
    )(*gives)


def exchange_chips(sends, name):
    n = len(sends)

    def body(*refs):
        s_refs, r_refs = refs[:n], refs[n:2 * n]
        send_sems, recv_sems, local_sems = refs[2 * n:]
        x, y, c = _my_pos()
        me = 2 * x + y
        mine = [pltpu.make_async_copy(s_refs[a].at[me], r_refs[a].at[me], local_sems.at[a]) for a in range(n)]
        for cp in mine:
            cp.start()
        copies = []
        for k in (2, 1, 3):
            px, py = x ^ (k >> 1), y ^ (k & 1)
            for a in range(n):
                cp = pltpu.make_async_remote_copy(
                    src_ref=s_refs[a].at[2 * px + py], dst_ref=r_refs[a].at[me],
                    send_sem=send_sems.at[3 * a + k - 1], recv_sem=recv_sems.at[3 * a + k - 1],
                    device_id=(px, py, c), device_id_type=MESH_ID)
                cp.start()
                copies.append(cp)
        for cp in copies:
            cp.wait_recv()
        for cp in copies:
            cp.wait_send()
        for cp in mine:
            cp.wait()

    any_spec = pl.BlockSpec(memory_space=pl.ANY)
    return pl.pallas_call(
        body, name=name, out_shape=[jax.ShapeDtypeStruct(s.shape, s.dtype) for s in sends],
        in_specs=[any_spec] * n, out_specs=[any_spec] * n,
        scratch_shapes=[pltpu.SemaphoreType.DMA((3 * n,)), pltpu.SemaphoreType.DMA((3 * n,)), pltpu.SemaphoreType.DMA((n,))],
    )(*sends)


def add_pairs(a, b, name, tr=512):
    rows, cols = a.shape
    tr = max(d for d in range(16, min(tr, rows) + 1, 16) if rows % d == 0)

    def body(a_ref, b_ref, o_ref):
        o_ref[...] = (a_ref[...].astype(F32) + b_ref[...].astype(F32)).astype(BF16)

    tile = pl.BlockSpec((tr, cols), lambda i: (i, 0))
    return pl.pallas_call(body, name=name, grid=(rows // tr,), in_specs=[tile, tile], out_specs=tile,
                          out_shape=jax.ShapeDtypeStruct((rows, cols), BF16), compiler_params=_cparams(("parallel",)))(a, b)


def _pack_rows(parts, dtype, row_multiple):
    flat = [p.astype(dtype).reshape(-1) for p in parts]
    sizes = [f.shape[0] for f in flat]
    total = sum(sizes)
    quantum = LANES * row_multiple
    padded = -(-total // quantum) * quantum
    if padded > total:
        flat.append(jnp.zeros((padded - total,), dtype))
    return jnp.concatenate(flat).reshape(padded // LANES, LANES), sizes


def _unpack(flat, sizes, shapes):
    out, off = [], 0
    lead = flat.shape[:-1]
    for n, shp in zip(sizes, shapes):
        out.append(flat[..., off:off + n].reshape(lead + tuple(shp)))
        off += n
    return out


def rows_call(name, body, n_rows, tr, ins, outs, scratch=(), aliases=None):
    n_tiles = n_rows // tr
    assert n_tiles * tr == n_rows
    in_specs, arrays = [], []
    for arr, kind in ins:
        arrays.append(arr)
        if kind == "row":
            in_specs.append(pl.BlockSpec((tr, arr.shape[1]), lambda i: (i, 0)))
        elif kind == "any":
            in_specs.append(pl.BlockSpec(memory_space=pl.ANY))
        elif kind == "full":
            in_specs.append(pl.BlockSpec(arr.shape, lambda i, nd=arr.ndim: (0,) * nd))
        elif kind[0] == "row":
            _, w, ci = kind
            in_specs.append(pl.BlockSpec((tr, w), lambda i, ci=ci: (i, ci)))
        elif kind[0] == "prev8":
            _, w, ci = kind
            hr = 8 * (4 // arr.dtype.itemsize)
            in_specs.append(pl.BlockSpec((hr, w), lambda i, ci=ci, hr=hr: (jnp.maximum(i * (tr // hr) - 1, 0), ci)))
        elif kind[0] == "next8":
            _, w, ci = kind
            hr = 8 * (4 // arr.dtype.itemsize)
            last = n_rows // hr - 1
            in_specs.append(pl.BlockSpec((hr, w), lambda i, ci=ci, last=last, hr=hr: (jnp.minimum((i + 1) * (tr // hr), last), ci)))
        else:
            raise ValueError(kind)
    out_specs, out_shapes = [], []
    for shape, dtype, kind in outs:
        out_shapes.append(jax.ShapeDtypeStruct(shape, dtype))
        if kind == "row":
            out_specs.append(pl.BlockSpec((tr, shape[1]), lambda i: (i, 0)))
        elif kind[0] == "row":
            _, w, ci = kind
            out_specs.append(pl.BlockSpec((tr, w), lambda i, ci=ci: (i, ci)))
        else:
            out_specs.append(pl.BlockSpec(shape, lambda i, nd=len(shape): (0,) * nd))
    has_acc = any(k == "acc" for _, _, k in outs)
    return pl.pallas_call(
        body, name=name, grid=(n_tiles,), in_specs=in_specs, out_specs=out_specs, out_shape=out_shapes,
        scratch_shapes=list(scratch), input_output_aliases=dict(aliases or {}),
        compiler_params=_cparams(("arbitrary",) if has_acc else ("parallel",)),
    )(*arrays)


def _prev8(ref):
    n = ref.shape[0]
    return ref[n - 8:n, :].astype(F32)


def _next8(ref):
    return ref[0:8, :].astype(F32)


def _acc(ref, val):
    @pl.when(pl.program_id(0) == 0)
    def _():
        ref[...] = jnp.zeros_like(ref)
    ref[...] += val


def _rstd(x):
    return lax.rsqrt(jnp.mean(x * x, axis=-1, keepdims=True) + NORM_EPS)


def _sigmoid(x):
    return 1.0 / (1.0 + jnp.exp(-x))


def _silu(x):
    return x * _sigmoid(x)


def _dsilu(x):
    s = _sigmoid(x)
    return s * (1.0 + x * (1.0 - s))


def _softplus(x):
    return jnp.maximum(x, 0.0) + jnp.log(1.0 + jnp.exp(-jnp.abs(x)))


def _log_sigmoid_neg(x):
    t = -x
    return jnp.minimum(t, 0.0) - jnp.log(1.0 + jnp.exp(jnp.minimum(x, t)))


def normmod_fwd(x, g, scale, shift, name):
    t, d = x.shape

    def body(x_ref, g_ref, sc_ref, sh_ref, h_ref):
        xv = x_ref[...]
        h = xv * _rstd(xv) * g_ref[...] * (1.0 + sc_ref[...]) + sh_ref[...]
        h_ref[...] = h.astype(BF16)

    return rows_call(name, body, t, 512, [(x, "row"), (g, "full"), (scale, "full"), (shift, "full")],
                     [((t, d), BF16, "row")])[0]


def resid_normmod_fwd(x, f, gate, g_post, g_pre, scale, shift, name):
    t, d = x.shape

    def body(x_ref, f_ref, gate_ref, gp_ref, g_ref, sc_ref, sh_ref, xo_ref, h_ref):
        fv = f_ref[...]
        xn = x_ref[...] + gate_ref[...] * (fv * _rstd(fv) * gp_ref[...])
        xo_ref[...] = xn
        h = xn * _rstd(xn) * g_ref[...] * (1.0 + sc_ref[...]) + sh_ref[...]
        h_ref[...] = h.astype(BF16)

    return rows_call(name, body, t, 512,
                     [(x, "row"), (f, "row"), (gate, "full"), (g_post, "full"), (g_pre, "full"), (scale, "full"),
                      (shift, "full")],
                     [((t, d), F32, "row"), ((t, d), BF16, "row")])


def resid_loss(x, f, gate, g_post, target, name):
    t, d = x.shape

    def body(x_ref, f_ref, gate_ref, gp_ref, tg_ref, dy_ref, loss_ref):
        fv = f_ref[...]
        yv = x_ref[...] + gate_ref[...] * (fv * _rstd(fv) * gp_ref[...])
        err = yv - tg_ref[...]
        dy_ref[...] = err * (1.0 / d)
        part = 0.5 * jnp.sum(jnp.mean(err * err, axis=-1, keepdims=True), axis=0, keepdims=True)
        _acc(loss_ref, jnp.broadcast_to(part, loss_ref.shape))

    return rows_call(name, body, t, 512,
                     [(x, "row"), (f, "row"), (gate, "full"), (g_post, "full"), (target, "row")],
                     [((t, d), F32, "row"), ((8, LANES), F32, "acc")])


def resid_bwd(dx, f, gate, g_post, name):
    t, d = dx.shape

    def body(dx_ref, f_ref, gate_ref, gp_ref, df_ref, dgate_ref, dg_ref):
        fv, dxv, gp = f_ref[...], dx_ref[...], gp_ref[...]
        r = _rstd(fv)
        fn = fv * r
        _acc(dgate_ref, jnp.sum(dxv * (fn * gp), axis=0, keepdims=True))
        dn = dxv * gate_ref[...]
        _acc(dg_ref, jnp.sum(dn * fn, axis=0, keepdims=True))
        u = dn * gp
        df = r * (u - fn * jnp.mean(fn * u, axis=-1, keepdims=True))
        df_ref[...] = df.astype(BF16)

    return rows_call(name, body, t, 512, [(dx, "row"), (f, "row"), (gate, "full"), (g_post, "full")],
                     [((t, d), BF16, "row"), ((1, d), F32, "acc"), ((1, d), F32, "acc")])


def normmod_bwd(dh, x, dx_in, g, scale, name):
    t, d = x.shape

    def body(dh_ref, x_ref, dxi_ref, g_ref, sc_ref, dx_ref, dsc_ref, dsh_ref, dg_ref):
        xv, dhv, gv = x_ref[...], dh_ref[...], g_ref[...]
        r = _rstd(xv)
        xn = xv * r
        _acc(dsc_ref, jnp.sum(dhv * (xn * gv), axis=0, keepdims=True))
        _acc(dsh_ref, jnp.sum(dhv, axis=0, keepdims=True))
        dn = dhv * (1.0 + sc_ref[...])
        _acc(dg_ref, jnp.sum(dn * xn, axis=0, keepdims=True))
        u = dn * gv
        dx_ref[...] = dxi_ref[...] + r * (u - xn * jnp.mean(xn * u, axis=-1, keepdims=True))

    return rows_call(name, body, t, 512, [(dh, "row"), (x, "row"), (dx_in, "row"), (g, "full"), (scale, "full")],
                     [((t, d), F32, "row"), ((1, d), F32, "acc"), ((1, d), F32, "acc"), ((1, d), F32, "acc")])


def normmod_resid_bwd(dh, x, dx_in, g, scale, f, gate, g_post, name):
    t, d = x.shape

    def body(dh_ref, x_ref, dxi_ref, g_ref, sc_ref, f_ref, gate_ref, gp_ref,
             dx_ref, df_ref, dsc_ref, dsh_ref, dg_ref, dgate_ref, dgp_ref):
        xv, dhv, gv = x_ref[...], dh_ref[...], g_ref[...]
        r = _rstd(xv)
        xn = xv * r
        _acc(dsc_ref, jnp.sum(dhv * (xn * gv), axis=0, keepdims=True))
        _acc(dsh_ref, jnp.sum(dhv, axis=0, keepdims=True))
        dn = dhv * (1.0 + sc_ref[...])
        _acc(dg_ref, jnp.sum(dn * xn, axis=0, keepdims=True))
        u = dn * gv
        dxv = dxi_ref[...] + r * (u - xn * jnp.mean(xn * u, axis=-1, keepdims=True))
        dx_ref[...] = dxv
        fv, gp = f_ref[...], gp_ref[...]
        rf = _rstd(fv)
        fn = fv * rf
        _acc(dgate_ref, jnp.sum(dxv * (fn * gp), axis=0, keepdims=True))
        dnf = dxv * gate_ref[...]
        _acc(dgp_ref, jnp.sum(dnf * fn, axis=0, keepdims=True))
        uf = dnf * gp
        df_ref[...] = (rf * (uf - fn * jnp.mean(fn * uf, axis=-1, keepdims=True))).astype(BF16)

    vec = ((1, d), F32, "acc")
    return rows_call(name, body, t, 512,
                     [(dh, "row"), (x, "row"), (dx_in, "row"), (g, "full"), (scale, "full"), (f, "row"), (gate, "full"),
                      (g_post, "full")],
                     [((t, d), F32, "row"), ((t, d), BF16, "row"), vec, vec, vec, vec, vec])


def _pick(n, prefs):
    for p in prefs:
        if n % p == 0:
            return p
    return n


def mm_nn(a_list, b_list, out_dtype, name, tm=1024, tn=None, tk=None):
    m, k = a_list[0].shape
    n = b_list[0].shape[1]
    tm = min(tm, m)
    tn = tn or _pick(n, (1024, 768, 512, 256, 128))
    tk = tk or _pick(k, (1024, 1408, 512, 256))
    nk = k // tk
    npair = len(a_list)

    if nk == 1 and npair == 1:
        def body1(a_ref, b_ref, o_ref):
            o_ref[...] = jnp.dot(a_ref[...], b_ref[...], preferred_element_type=F32).astype(o_ref.dtype)

        return pl.pallas_call(
            body1, name=name, grid=(m // tm, n // tn),
            in_specs=[pl.BlockSpec((tm, k), lambda i, j: (i, 0)), pl.BlockSpec((k, tn), lambda i, j: (0, j))],
            out_specs=pl.BlockSpec((tm, tn), lambda i, j: (i, j)),
            out_shape=jax.ShapeDtypeStruct((m, n), out_dtype),
            compiler_params=_cparams(("parallel", "parallel")),
        )(a_list[0], b_list[0])

    def body(*refs):
        a_refs, b_refs = refs[:npair], refs[npair:2 * npair]
        o_ref, acc = refs[2 * npair], refs[2 * npair + 1]
        kk = pl.program_id(2)

        @pl.when(kk == 0)
        def _():
            acc[...] = jnp.zeros_like(acc)

        s = acc[...]
        for a_ref, b_ref in zip(a_refs, b_refs):
            s = s + jnp.dot(a_ref[...], b_ref[...], preferred_element_type=F32)
        acc[...] = s

        @pl.when(kk == nk - 1)
        def _():
            o_ref[...] = acc[...].astype(o_ref.dtype)

    return pl.pallas_call(
        body, name=name, grid=(m // tm, n // tn, nk),
        in_specs=[pl.BlockSpec((tm, tk), lambda i, j, kk: (i, kk))] * npair
        + [pl.BlockSpec((tk, tn), lambda i, j, kk: (kk, j))] * npair,
        out_specs=pl.BlockSpec((tm, tn), lambda i, j, kk: (i, j)),
        out_shape=jax.ShapeDtypeStruct((m, n), out_dtype),
        scratch_shapes=[pltpu.VMEM((tm, tn), F32)],
        compiler_params=_cparams(("parallel", "parallel", "arbitrary")),
    )(*a_list, *b_list)


def mm_nt(a_list, b_list, b_koff, out_dtype, name, tm=1024):
    m, k = a_list[0].shape
    n = b_list[0].shape[0]
    tm = min(tm, m)
    tn = _pick(n, (1024, 512, 256))
    tk = _pick(k, (1024, 1408, 512, 256))
    nk = k // tk
    npair = len(a_list)
    koff = [o // tk for o in b_koff]
    nt_dims = (((1,), (1,)), ((), ()))

    def body(*refs):
        a_refs, b_refs = refs[:npair], refs[npair:2 * npair]
        o_ref, acc = refs[2 * npair], refs[2 * npair + 1]
        kk = pl.program_id(2)

        @pl.when(kk == 0)
        def _():
            acc[...] = jnp.zeros_like(acc)

        s = acc[...]
        for a_ref, b_ref in zip(a_refs, b_refs):
            s = s + lax.dot_general(a_ref[...], b_ref[...], nt_dims, preferred_element_type=F32)
        acc[...] = s

        @pl.when(kk == nk - 1)
        def _():
            o_ref[...] = acc[...].astype(o_ref.dtype)

    return pl.pallas_call(
        body, name=name, grid=(m // tm, n // tn, nk),
        in_specs=[pl.BlockSpec((tm, tk), lambda i, j, kk: (i, kk))] * npair
        + [pl.BlockSpec((tn, tk), lambda i, j, kk, o=o: (j, kk + o)) for o in koff],
        out_specs=pl.BlockSpec((tm, tn), lambda i, j, kk: (i, j)),
        out_shape=jax.ShapeDtypeStruct((m, n), out_dtype),
        scratch_shapes=[pltpu.VMEM((tm, tn), F32)],
        compiler_params=_cparams(("parallel", "parallel", "arbitrary")),
    )(*a_list, *b_list)


def mm_tn(a, b, name, tt=512):
    t, ka = a.shape
    n = b.shape[1]
    ta = _pick(ka, (1024, 1408, 512, 256))
    tn = _pick(n, (2048, 1024, 1408, 512, 256))
    nt = t // tt

    def body(a_ref, b_ref, o_ref, acc):
        s = pl.program_id(2)

        @pl.when(s == 0)
        def _():
            acc[...] = jnp.zeros_like(acc)

        acc[...] += lax.dot_general(a_ref[...], b_ref[...], (((0,), (0,)), ((), ())), preferred_element_type=F32)

        @pl.when(s == nt - 1)
        def _():
            o_ref[...] = acc[...].astype(BF16)

    return pl.pallas_call(
        body, name=name, grid=(ka // ta, n // tn, nt),
        in_specs=[pl.BlockSpec((tt, ta), lambda i, j, s: (s, i)), pl.BlockSpec((tt, tn), lambda i, j, s: (s, j))],
        out_specs=pl.BlockSpec((ta, tn), lambda i, j, s: (i, j)),
        out_shape=jax.ShapeDtypeStruct((ka, n), BF16),
        scratch_shapes=[pltpu.VMEM((ta, tn), F32)],
        compiler_params=_cparams(("parallel", "parallel", "arbitrary")),
    )(a, b)


def mm_swiglu_fwd(h, w_ffn_in, name, tm=1024, tn=1408):
    m, k = h.shape
    nh = FFN_HIDDEN // tn

    def body(h_ref, wg_ref, wu_ref, gt_ref, up_ref, a_ref):
        hv = h_ref[...]
        gt = jnp.dot(hv, wg_ref[...], preferred_element_type=F32)
        up = jnp.dot(hv, wu_ref[...], preferred_element_type=F32)
        gt_ref[...] = gt.astype(BF16)
        up_ref[...] = up.astype(BF16)
        a_ref[...] = (_silu(gt) * up).astype(BF16)

    shp = jax.ShapeDtypeStruct((m, FFN_HIDDEN), BF16)
    ospec = pl.BlockSpec((tm, tn), lambda i, j: (i, j))
    return pl.pallas_call(
        body, name=name, grid=(m // tm, nh),
        in_specs=[pl.BlockSpec((tm, k), lambda i, j: (i, 0)), pl.BlockSpec((k, tn), lambda i, j: (0, j)),
                  pl.BlockSpec((k, tn), lambda i, j: (0, j + nh))],
        out_specs=[ospec, ospec, ospec], out_shape=[shp, shp, shp],
        compiler_params=_cparams(("parallel", "parallel")),
    )(h, w_ffn_in, w_ffn_in)


def mm_swiglu_bwd(df, w_out, gt, up, name, tm=256, sub=256):
    m, k = df.shape
    n_sub = FFN_HIDDEN // sub

    def body(df_ref, w_ref, gt_ref, up_ref, dgt_ref, dup_ref):
        dfv = df_ref[...]

        def chunk_dot(c):
            return lax.dot_general(dfv, w_ref[c * sub:(c + 1) * sub, :], (((1,), (1,)), ((), ())), preferred_element_type=F32)

        da_next = chunk_dot(0)
        for c in range(n_sub):
            da = da_next
            if c + 1 < n_sub:
                da_next = chunk_dot(c + 1)
            cols = slice(c * sub, (c + 1) * sub)
            gtv = gt_ref[:, cols].astype(F32)
            sg = _sigmoid(gtv)
            dgt_ref[:, cols] = (da * up_ref[:, cols].astype(F32) * (sg * (1.0 + gtv * (1.0 - sg)))).astype(BF16)
            dup_ref[:, cols] = (da * (gtv * sg)).astype(BF16)

    shp = jax.ShapeDtypeStruct((m, FFN_HIDDEN), BF16)
    tile = pl.BlockSpec((tm, FFN_HIDDEN), lambda i: (i, 0))
    return pl.pallas_call(
        body, name=name, grid=(m // tm,),
        in_specs=[pl.BlockSpec((tm, k), lambda i: (i, 0)), pl.BlockSpec((FFN_HIDDEN, k), lambda i: (0, 0)), tile, tile],
        out_specs=[tile, tile], out_shape=[shp, shp],
        compiler_params=_cparams(("parallel",)),
    )(df, w_out, gt, up)


def _shift_down(x, prev8, j):
    if j == 0:
        return x
    xr = pltpu.roll(x, j, 0)
    pr = pltpu.roll(prev8, j, 0)
    row = lax.broadcasted_iota(jnp.int32, (8, x.shape[1]), 0)
    head = jnp.where(row < j, pr, xr[:8])
    return head if x.shape[0] == 8 else jnp.concatenate([head, xr[8:]], axis=0)


def _shift_up(x, next8, j):
    if j == 0:
        return x
    n = x.shape[0]
    xr = pltpu.roll(x, n - j, 0)
    nr = pltpu.roll(next8, 8 - j, 0)
    row = lax.broadcasted_iota(jnp.int32, (8, x.shape[1]), 0)
    return jnp.concatenate([xr[:n - 8], jnp.where(row >= 8 - j, nr, xr[n - 8:])], axis=0)


def _conv_taps(x, prev8, w_ref, taps):
    out = None
    for k in range(taps):
        term = w_ref[k:k + 1, :] * _shift_down(x, prev8, taps - 1 - k)
        out = term if out is None else out + term
    return out


def post_inproj(p, sc_w, ssm_w, ssm_b, name, tr=512):
    t = p.shape[0]

    def body(sc_ref, scp_ref, qkv_ref, xbc_ref, xbcp_ref, scw_ref, sw_ref, sb_ref, ya_ref, qkvo_ref, act_ref):
        first = (pl.program_id(0) > 0).astype(F32)
        sc = sc_ref[...].astype(F32)
        scp = _prev8(scp_ref) * first
        u = sc[:, 256:512] * sc[:, 512:768]
        up = scp[:, 256:512] * scp[:, 512:768]
        ya_ref[...] = (sc[:, 0:256] * _conv_taps(u, up, scw_ref, 3)).astype(BF16)
        qkv = qkv_ref[...]
        qkvo_ref[:, 0:256] = (qkv[:, 0:256].astype(F32) * 0.125).astype(BF16)
        qkvo_ref[:, 256:768] = qkv[:, 256:768].astype(BF16)
        xc = _conv_taps(xbc_ref[...].astype(F32), _prev8(xbcp_ref) * first, sw_ref, 4) + sb_ref[...]
        act_ref[...] = _silu(xc)

    return rows_call(
        name, body, t, tr,
        [(p, ("row", 768, OFF_SC // 768)), (p, ("prev8", 768, OFF_SC // 768)), (p, ("row", 768, OFF_QKV // 768)),
         (p, ("row", 768, OFF_XBC // 768)), (p, ("prev8", 768, OFF_XBC // 768)),
         (sc_w, "full"), (ssm_w, "full"), (ssm_b, "full")],
        [((t, 256), BF16, "row"), ((t, 768), BF16, "row"), ((t, 768), F32, "row")])


def branch_out_fwd(ya, yb, yc, p, w_cat, name, tr=256):
    t = p.shape[0]

    def body(ya_ref, yb_ref, yc_ref, gl_ref, w_ref, o_ref):
        y_a = jnp.dot(ya_ref[...], w_ref[0:256, :], preferred_element_type=F32)
        y_b = jnp.dot(yb_ref[...].astype(BF16), w_ref[256:512, :], preferred_element_type=F32)
        y_c = jnp.dot(yc_ref[...], w_ref[512:1024, :], preferred_element_type=F32)
        m = (_sigmoid(gl_ref[:, 0:1024].astype(F32)) * y_a + _sigmoid(gl_ref[:, 1024:2048].astype(F32)) * y_b
             + _sigmoid(gl_ref[:, 2048:3072].astype(F32)) * y_c)
        o_ref[...] = m.astype(BF16)

    return rows_call(name, body, t, tr,
                     [(ya, "row"), (yb, "row"), (yc, "row"), (p, ("row", 3072, 0)), (w_cat, "full")],
                     [((t, D_MODEL), BF16, "row")])[0]


def branch_out_bwd(dm, ya, yb, yc, p, w_cat, name, tr=256):
    t = p.shape[0]
    tn_dims = (((0,), (0,)), ((), ()))

    def body(dm_ref, ya_ref, yb_ref, yc_ref, gl_ref, w_ref, dgl_ref, dya_ref, dyb_ref, dyc_ref, dw_ref):
        @pl.when(pl.program_id(0) == 0)
        def _():
            dw_ref[...] = jnp.zeros_like(dw_ref)

        dmv = dm_ref[...]
        ins = (ya_ref[...], yb_ref[...].astype(BF16), yc_ref[...])
        rows = ((0, 256), (256, 512), (512, 1024))
        outs = (dya_ref, dyb_ref, dyc_ref)
        for i in range(3):
            r0, r1 = rows[i]
            y = jnp.dot(ins[i], w_ref[r0:r1, :], preferred_element_type=F32)
            s = _sigmoid(gl_ref[:, 1024 * i:1024 * (i + 1)].astype(F32))
            dgl_ref[:, 1024 * i:1024 * (i + 1)] = (dmv * y * s * (1.0 - s)).astype(BF16)
            dy = (dmv * s).astype(BF16)
            outs[i][...] = lax.dot_general(dy, w_ref[r0:r1, :], _NT, preferred_element_type=F32)
            dw_ref[r0:r1, :] += lax.dot_general(ins[i], dy, tn_dims, preferred_element_type=F32)

    return rows_call(name, body, t, tr,
                     [(dm, "row"), (ya, "row"), (yb, "row"), (yc, "row"), (p, ("row", 3072, 0)), (w_cat, "full")],
                     [((t, IN_PAD), BF16, ("row", 3072, 0)), ((t, 256), F32, "row"), ((t, 256), F32, "row"),
                      ((t, 512), F32, "row"), ((D_MODEL, D_MODEL), F32, "acc")])


def assemble_dp(dp, dya, p, sc_w, dq, dk, dv, dact, ssm_w, ssm_b, ddt, dz, name, tr=256):
    t = p.shape[0]
    n_tiles = t // tr
    sci, xi = OFF_SC // 768, OFF_XBC // 768
    base = OFF_SC
    assert base == IN_PAD - base
    o_sc, o_qkv, o_xbc, o_dt, o_z, o_end = (c - base for c in (OFF_SC, OFF_QKV, OFF_XBC, OFF_DT, OFF_Z, IN_PAD))

    def body(dp_ref, dya_ref, dyan_ref, sc_ref, scp_ref, scn_ref, scw_ref, dq_ref, dk_ref, dv_ref,
             dact_ref, dactn_ref, xbc_ref, xbcp_ref, xbcn_ref, sw_ref, sb_ref, ddt_ref, dz_ref,
             o_ref, dscw_ref, dsw_ref, dsb_ref):
        i = pl.program_id(0)

        @pl.when(i == 0)
        def _():
            dscw_ref[...] = jnp.zeros_like(dscw_ref)
            dsw_ref[...] = jnp.zeros_like(dsw_ref)
            dsb_ref[...] = jnp.zeros_like(dsb_ref)

        first = (i > 0).astype(F32)
        last = (i < n_tiles - 1).astype(F32)
        del dp_ref
        sc = sc_ref[...].astype(F32)
        scp = _prev8(scp_ref) * first
        scn = _next8(scn_ref) * last
        u = sc[:, 256:512] * sc[:, 512:768]
        up = scp[:, 256:512] * scp[:, 512:768]
        dya_v = dya_ref[...]
        cv = _conv_taps(u, up, scw_ref, 3)
        o_ref[:, o_sc:o_sc + 256] = (dya_v * cv).astype(BF16)
        dcv = dya_v * sc[:, 0:256]
        dcvn = _next8(dyan_ref) * last * scn[:, 0:256]
        du = None
        for k in range(3):
            sh = 2 - k
            term = scw_ref[k:k + 1, :] * _shift_up(dcv, dcvn, sh)
            du = term if du is None else du + term
            dscw_ref[k:k + 1, :] += jnp.sum(dcv * _shift_down(u, up, sh), axis=0, keepdims=True)
        o_ref[:, o_sc + 256:o_sc + 512] = (du * sc[:, 512:768]).astype(BF16)
        o_ref[:, o_sc + 512:o_sc + 768] = (du * sc[:, 256:512]).astype(BF16)
        o_ref[:, o_qkv:o_qkv + 256] = (dq_ref[...] * 0.125).astype(BF16)
        o_ref[:, o_qkv + 256:o_qkv + 512] = dk_ref[...].astype(BF16)
        o_ref[:, o_qkv + 512:o_qkv + 768] = dv_ref[...].astype(BF16)
        xb = xbc_ref[...].astype(F32)
        xbp = _prev8(xbcp_ref) * first
        xbn = _next8(xbcn_ref)
        xc = _conv_taps(xb, xbp, sw_ref, 4) + sb_ref[...]
        xcn = _conv_taps(xbn, xb[tr - 8:, :], sw_ref, 4) + sb_ref[...]
        dxc = dact_ref[...] * _dsilu(xc)
        dxcn = _next8(dactn_ref) * _dsilu(xcn) * last
        dxb = None
        for k in range(4):
            sh = 3 - k
            term = sw_ref[k:k + 1, :] * _shift_up(dxc, dxcn, sh)
            dxb = term if dxb is None else dxb + term
            dsw_ref[k:k + 1, :] += jnp.sum(dxc * _shift_down(xb, xbp, sh), axis=0, keepdims=True)
        dsb_ref[...] += jnp.sum(dxc, axis=0, keepdims=True)
        o_ref[:, o_xbc:o_xbc + 768] = dxb.astype(BF16)
        o_ref[:, o_dt:o_dt + 128] = ddt_ref[...].astype(BF16)
        o_ref[:, o_dt + 128:o_z] = jnp.zeros((tr, o_z - o_dt - 128), BF16)
        o_ref[:, o_z:o_end] = dz_ref[...].astype(BF16)

    return rows_call(
        name, body, t, tr,
        [(dp, "any"), (dya, "row"), (dya, ("next8", 256, 0)),
         (p, ("row", 768, sci)), (p, ("prev8", 768, sci)), (p, ("next8", 768, sci)), (sc_w, "full"),
         (dq, "row"), (dk, "row"), (dv, "row"),
         (dact, "row"), (dact, ("next8", 768, 0)),
         (p, ("row", 768, xi)), (p, ("prev8", 768, xi)), (p, ("next8", 768, xi)), (ssm_w, "full"), (ssm_b, "full"),
         (ddt, "row"), (dz, "row")],
        [((t, IN_PAD), BF16, ("row", IN_PAD - base, 1)), ((8, 256), F32, "acc"), ((8, 768), F32, "acc"),
         ((1, 768), F32, "acc")], aliases={0: 0})


def adamw_flat(slots, w, m, v, name, tr=512):
    n_slots, rows, lanes = slots.shape
    tr = max(d for d in range(8, min(tr, rows) + 1, 8) if rows % d == 0) if rows % 8 == 0 else rows
    bc1 = 1.0 - ADAM_B1 ** ADAM_STEP
    bc2 = 1.0 - ADAM_B2 ** ADAM_STEP

    def body(s_ref, w_ref, m_ref, v_ref, g_ref, d_ref, mo_ref, vo_ref):
        g = s_ref[0].astype(F32)
        for k in range(1, n_slots):
            g = g + s_ref[k].astype(F32)
        mn = ADAM_B1 * m_ref[...] + (1.0 - ADAM_B1) * g
        vn = ADAM_B2 * v_ref[...] + (1.0 - ADAM_B2) * (g * g)
        m_hat = mn / bc1
        v_hat = vn / bc2
        g_ref[...] = g
        d_ref[...] = -ADAM_LR * (m_hat / (jnp.sqrt(v_hat) + ADAM_EPS) + ADAM_WD * w_ref[...])
        mo_ref[...] = mn
        vo_ref[...] = vn

    tile = pl.BlockSpec((tr, lanes), lambda i: (i, 0))
    shp = jax.ShapeDtypeStruct((rows, lanes), F32)
    return pl.pallas_call(
        body, name=name, grid=(rows // tr,),
        in_specs=[pl.BlockSpec((n_slots, tr, lanes), lambda i: (0, i, 0)), tile, tile, tile],
        out_specs=[tile] * 4, out_shape=[shp] * 4,
        compiler_params=_cparams(("parallel",)),
    )(slots, w, m, v)


def _split_dot(x, tri):
    hi = x.astype(BF16)
    lo = (x - hi.astype(F32)).astype(BF16)
    return jnp.dot(hi, tri, preferred_element_type=F32) + jnp.dot(lo, tri, preferred_element_type=F32)


_NT = (((1,), (1,)), ((), ()))
_TN = (((0,), (0,)), ((), ()))

SBA_EXP_ZERO = -104.0
SBA_SKIPPED = -1e30


def sba_fwd(qkv, name, bq=256, bk=256):
    t = qkv.shape[0]
    ratio = bq // bk
    assert bq == ratio * bk and t // bk <= LANES

    def body(q_ref, k_ref, v_ref, o_ref, runs_ref, acc_s, run_s):
        i = pl.program_id(1)
        lane = lax.broadcasted_iota(jnp.int32, (1, LANES), 1)
        lane_q = lax.broadcasted_iota(jnp.int32, (bq, LANES), 1)
        qi = lax.broadcasted_iota(jnp.int32, (bq, bk), 0) + i * bq
        kj = lax.broadcasted_iota(jnp.int32, (bq, bk), 1)
        later = (lax.broadcasted_iota(jnp.int32, (bk, bk), 0) > lax.broadcasted_iota(jnp.int32, (bk, bk), 1)).astype(BF16)
        qv = q_ref[...]
        qms = [jnp.where(hm, qv, jnp.zeros_like(qv)) for hm in (lane < 64, lane >= 64)]
        acc_s[...] = jnp.zeros_like(acc_s)
        run_s[...] = jnp.zeros_like(run_s)
        runs_ref[...] = jnp.full(runs_ref.shape, SBA_SKIPPED, F32)

        def tiles(specs):
            chains = [(ti, hh) for ti in range(len(specs)) for hh in range(2)]
            kb = [k_ref[pl.ds(pl.multiple_of(j * bk, bk), bk), :] for j, _ in specs]
            vb = [v_ref[pl.ds(pl.multiple_of(j * bk, bk), bk), :] for j, _ in specs]
            mask = [(kj + j * bk) < qi if masked else None for j, masked in specs]
            s = {c: lax.dot_general(qms[c[1]], kb[c[0]], _NT, preferred_element_type=F32) for c in chains}
            lk = {c: _log_sigmoid_neg(s[c]) for c in chains}
            lk = {c: lk[c] if mask[c[0]] is None else jnp.where(mask[c[0]], lk[c], 0.0) for c in chains}
            w = {c: jnp.dot(lk[c].astype(BF16), later, preferred_element_type=F32) for c in chains}
            run = {}
            for hh in range(2):
                carry = run_s[hh]
                for ti in range(len(specs)):
                    run[ti, hh] = carry
                    carry = carry + jnp.sum(lk[ti, hh], axis=1, keepdims=True)
                run_s[hh] = carry
            a = {c: jnp.exp(s[c] + lk[c] + w[c] + run[c]) for c in chains}
            a = {c: a[c] if mask[c[0]] is None else jnp.where(mask[c[0]], a[c], 0.0) for c in chains}
            for ti, hh in chains:
                acc_s[hh] += jnp.dot(a[ti, hh].astype(BF16), vb[ti], preferred_element_type=F32)
                runs_ref[hh] = jnp.where(lane_q == specs[ti][0], run[ti, hh], runs_ref[hh])

        for d in range(ratio):
            tiles([((i + 1) * ratio - 1 - d, True)])

        def live():
            return jnp.maximum(jnp.max(run_s[0]), jnp.max(run_s[1])) >= SBA_EXP_ZERO

        def cond(state):
            n, go = state
            return jnp.logical_and(n < i * ratio, go)

        def step(state):
            n, _ = state
            tiles([(i * ratio - 1 - n, False)])
            return n + 1, live()

        lax.while_loop(cond, step, (jnp.int32(0), live()))
        o_ref[...] = jnp.where(lane < 64, acc_s[0], acc_s[1])

    return pl.pallas_call(
        body, name=name, grid=(2, t // bq),
        in_specs=[pl.BlockSpec((bq, LANES), lambda p, i: (i, p)), pl.BlockSpec((t, LANES), lambda p, i: (0, 2 + p)),
                  pl.BlockSpec((t, LANES), lambda p, i: (0, 4 + p))],
        out_specs=[pl.BlockSpec((bq, LANES), lambda p, i: (i, p)), pl.BlockSpec((2, bq, LANES), lambda p, i: (p, i, 0))],
        out_shape=[jax.ShapeDtypeStruct((t, SB_WIDTH), F32), jax.ShapeDtypeStruct((4, t, LANES), F32)],
        scratch_shapes=[pltpu.VMEM((2, bq, LANES), F32), pltpu.VMEM((2, bq, 1), F32)],
        compiler_params=_cparams(("parallel", "parallel")),
    )(qkv, qkv, qkv)


def sba_bwd(qkv, runs, do, name, bq=256, bk=256):
    t = qkv.shape[0]
    ratio = bq // bk
    assert bq == ratio * bk
    nq = t // bq

    def body(q_ref, k_ref, v_ref, runs_ref, do_ref, dq_ref, dk_hbm, dv_hbm, dk_s, dv_s, sem, dq_s, rg_s):
        p = pl.program_id(0)
        i = pl.program_id(1)

        @pl.when(i == 0)
        def _():
            dk_s[...] = jnp.zeros_like(dk_s)
            dv_s[...] = jnp.zeros_like(dv_s)

        lane = lax.broadcasted_iota(jnp.int32, (1, LANES), 1)
        qi = lax.broadcasted_iota(jnp.int32, (bq, bk), 0) + i * bq
        kj = lax.broadcasted_iota(jnp.int32, (bq, bk), 1)
        r2 = lax.broadcasted_iota(jnp.int32, (bk, bk), 0)
        c2 = lax.broadcasted_iota(jnp.int32, (bk, bk), 1)
        later = (r2 > c2).astype(BF16)
        earlier = (r2 < c2).astype(BF16)
        qv = q_ref[...]
        dov = do_ref[...]
        heads = range(2)
        hms = (lane < 64, lane >= 64)
        qms = [jnp.where(hm, qv, jnp.zeros_like(qv)) for hm in hms]
        doms = [jnp.where(hm, dov, 0.0).astype(BF16) for hm in hms]
        runs = [runs_ref[hh] for hh in heads]
        dq_s[...] = jnp.zeros_like(dq_s)
        rg_s[...] = jnp.zeros_like(rg_s)

        def tiles(specs):
            nt = len(specs)
            chains = [(ti, hh) for ti in range(nt) for hh in heads]
            starts = [pl.multiple_of(j * bk, bk) for j, _ in specs]
            kb = [k_ref[pl.ds(st, bk), :] for st in starts]
            vb = [v_ref[pl.ds(st, bk), :] for st in starts]
            mask = [(kj + j * bk) < qi if masked else None for j, masked in specs]
            s = {c: lax.dot_general(qms[c[1]], kb[c[0]], _NT, preferred_element_type=F32) for c in chains}
            da = {c: lax.dot_general(doms[c[1]], vb[c[0]], _NT, preferred_element_type=F32) for c in chains}
            lk_raw = {c: _log_sigmoid_neg(s[c]) for c in chains}
            lk = {c: lk_raw[c] if mask[c[0]] is None else jnp.where(mask[c[0]], lk_raw[c], 0.0) for c in chains}
            w = {c: jnp.dot(lk[c].astype(BF16), later, preferred_element_type=F32) for c in chains}
            run = {c: jnp.sum(jnp.where(lane == specs[c[0]][0], runs[c[1]], 0.0), axis=1, keepdims=True) for c in chains}
            a = {c: jnp.exp(s[c] + lk[c] + w[c] + run[c]) for c in chains}
            a = {c: a[c] if mask[c[0]] is None else jnp.where(mask[c[0]], a[c], 0.0) for c in chains}
            g = {c: a[c] * da[c] for c in chains}
            rg = {}
            for hh in heads:
                carry = rg_s[hh]
                for ti in range(nt):
                    rg[ti, hh] = carry
                    carry = carry + jnp.sum(g[ti, hh], axis=1, keepdims=True)
                rg_s[hh] = carry
            cpre = {c: rg[c] + _split_dot(g[c], earlier) for c in chains}
            dz = {c: g[c] - jnp.exp(s[c] + lk_raw[c]) * (g[c] + cpre[c]) for c in chains}
            dz = {c: (dz[c] if mask[c[0]] is None else jnp.where(mask[c[0]], dz[c], 0.0)).astype(BF16) for c in chains}
            for ti, hh in chains:
                dq_s[hh] += jnp.dot(dz[ti, hh], kb[ti], preferred_element_type=F32)
            for ti in range(nt):
                dk = lax.dot_general(dz[ti, 0], qms[0], _TN, preferred_element_type=F32)
                dk_s[pl.ds(starts[ti], bk), :] += dk + lax.dot_general(dz[ti, 1], qms[1], _TN, preferred_element_type=F32)
                dv = lax.dot_general(a[ti, 0].astype(BF16), doms[0], _TN, preferred_element_type=F32)
                dv_s[pl.ds(starts[ti], bk), :] += dv + lax.dot_general(a[ti, 1].astype(BF16), doms[1], _TN,
                                                                       preferred_element_type=F32)

        live = jnp.maximum(jnp.max(runs[0], axis=0, keepdims=True), jnp.max(runs[1], axis=0, keepdims=True)) >= SBA_EXP_ZERO
        first = jnp.minimum(jnp.min(jnp.where(live, lane, LANES)), i * ratio)

        def step(j, carry):
            tiles([(j, False)])
            return carry

        if ratio == 1:
            lax.fori_loop(first, i - 1, step, 0)

            @pl.when(i == 0)
            def _():
                tiles([(i, True)])

            @pl.when(i > 0)
            def _():
                tiles([(i - 1, False), (i, True)])
        else:
            lax.fori_loop(first, i * ratio, step, 0)
            for d in range(ratio):
                tiles([(i * ratio + d, True)])

        dq_ref[...] = jnp.where(lane < 64, dq_s[0], dq_s[1])

        @pl.when(i == nq - 1)
        def _():
            col = pl.multiple_of(p * LANES, LANES)
            ck = pltpu.make_async_copy(dk_s, dk_hbm.at[:, pl.ds(col, LANES)], sem.at[0])
            cv = pltpu.make_async_copy(dv_s, dv_hbm.at[:, pl.ds(col, LANES)], sem.at[1])
            ck.start()
            cv.start()
            ck.wait()
            cv.wait()

    shp = jax.ShapeDtypeStruct((t, SB_WIDTH), F32)
    tile = pl.BlockSpec((bq, LANES), lambda p, i: (i, p))
    return pl.pallas_call(
        body, name=name, grid=(2, nq),
        in_specs=[tile, pl.BlockSpec((t, LANES), lambda p, i: (0, 2 + p)), pl.BlockSpec((t, LANES), lambda p, i: (0, 4 + p)),
                  pl.BlockSpec((2, bq, LANES), lambda p, i: (p, i, 0)), tile],
        out_specs=[tile, pl.BlockSpec(memory_space=pl.ANY), pl.BlockSpec(memory_space=pl.ANY)],
        out_shape=[shp, shp, shp],
        scratch_shapes=[pltpu.VMEM((t, LANES), F32), pltpu.VMEM((t, LANES), F32), pltpu.SemaphoreType.DMA((2,)),
                        pltpu.VMEM((2, bq, LANES), F32), pltpu.VMEM((2, bq, 1), F32)],
        compiler_params=_cparams(("arbitrary", "arbitrary")),
    )(qkv, qkv, qkv, runs, do)


def _ssd_consts():
    ln = SSM_CHUNK
    ri = lax.broadcasted_iota(jnp.int32, (ln, ln), 0)
    ci = lax.broadcasted_iota(jnp.int32, (ln, ln), 1)
    eh = lax.broadcasted_iota(jnp.int32, (LANES, SSM_INNER), 0)
    el = lax.broadcasted_iota(jnp.int32, (LANES, SSM_INNER), 1)
    expand = (jnp.right_shift(el, 6) == eh).astype(BF16)
    th = lax.broadcasted_iota(jnp.int32, (SSM_INNER, LANES), 1)
    tl = lax.broadcasted_iota(jnp.int32, (SSM_INNER, LANES), 0)
    reduce = (jnp.right_shift(tl, 6) == th).astype(BF16)
    return ri, ci, expand, reduce


def _dot_f32(a, b):
    return jnp.dot(a, b, precision=HI, preferred_element_type=F32)


def _split3(x):
    hi = x.astype(BF16)
    r1 = x - hi.astype(F32)
    mid = r1.astype(BF16)
    lo = (r1 - mid.astype(F32)).astype(BF16)
    return hi, mid, lo


def _dot_hi(a, b):
    if a.dtype == BF16:
        return sum(jnp.dot(a, t, preferred_element_type=F32) for t in _split3(b))
    return sum(jnp.dot(t, b, preferred_element_type=F32) for t in _split3(a))


def _ssd_prelude(xbc_ref, dt_ref, dtt_ref, hpr_ref, hpc_ref, ri, ci, expand):
    ln = SSM_CHUNK
    xs = xbc_ref[:, 0:512]
    bm = xbc_ref[:, 512:640]
    cm = xbc_ref[:, 640:768]
    dtb_r = hpr_ref[0:1, :]
    aneg_r = -jnp.exp(hpr_ref[1:2, :])
    pre = dt_ref[...] + dtb_r
    dt = _softplus(pre)
    a = dt * aneg_r
    dtt = _softplus(dtt_ref[...] + hpc_ref[0:8, :])
    att = dtt * (-jnp.exp(hpc_ref[8:16, :]))
    tril = (ri >= ci).astype(BF16)
    triu = (ri <= ci).astype(BF16)
    acs = _dot_hi(tril, a)
    acst = _dot_hi(att, triu)
    acs_e = _dot_hi(acs, expand)
    dt_e = _dot_hi(dt, expand)
    last_e = acs_e[ln - 1:ln, :]
    e_e = jnp.exp(acs_e)
    w_e = jnp.exp(last_e - acs_e)
    dec_e = jnp.exp(last_e)
    xdt = xs * dt_e
    return dict(xs=xs, bm=bm, cm=cm, pre=pre, dt=dt, aneg_r=aneg_r, acs=acs, acst=acst, dt_e=dt_e, e_e=e_e,
                w_e=w_e, dec_e=dec_e, xdt=xdt, triu=triu)


def ssd_fwd(act, p, dt32, dtt, hp_rows, hp_cols, d_e, norm_w, name):
    t = act.shape[0]
    ln = SSM_CHUNK
    nc = t // ln

    def body(xbc_ref, dt_ref, z_ref, dtt_ref, hpr_ref, hpc_ref, d_ref, nw_ref, yc_ref, y_ref, sto_ref, st):
        @pl.when(pl.program_id(0) == 0)
        def _():
            st[...] = jnp.zeros_like(st)

        ri, ci, expand, _ = _ssd_consts()
        q = _ssd_prelude(xbc_ref, dt_ref, dtt_ref, hpr_ref, hpc_ref, ri, ci, expand)
        lane = lax.broadcasted_iota(jnp.int32, (1, LANES), 1)
        rown = lax.broadcasted_iota(jnp.int32, (LANES, 1), 0)
        low = lane < 64
        mask = ri >= ci
        xdt_b = q["xdt"].astype(BF16)
        xw_b = (q["xdt"] * q["w_e"]).astype(BF16)
        bt = q["bm"].T.astype(BF16)
        cb_ = q["cm"].astype(BF16)
        y_pairs = []
        for g in range(2):
            gm = low if g == 0 else jnp.logical_not(low)
            rm = (rown < 64) if g == 0 else (rown >= 64)
            cg = jnp.where(gm, cb_, jnp.zeros_like(cb_))
            cb = jnp.dot(cg, bt, preferred_element_type=F32)
            for pp in range(2):
                pi = 2 * g + pp
                sl = slice(LANES * pi, LANES * (pi + 1))
                xp = xdt_b[:, sl]
                yd = []
                for hh in range(2):
                    h = 2 * pi + hh
                    diff = q["acs"][:, h:h + 1] - q["acst"][h:h + 1, :]
                    lam = jnp.exp(jnp.where(mask, diff, -jnp.inf))
                    yd.append(jnp.dot((cb * lam).astype(BF16), xp, preferred_element_type=F32))
                sp = st[pi]
                sto_ref[0, pi] = sp
                yoff = jnp.dot(cg, sp.astype(BF16), preferred_element_type=F32) * q["e_e"][:, sl]
                upd = jnp.dot(bt, xw_b[:, sl], preferred_element_type=F32)
                st[pi] = q["dec_e"][:, sl] * sp + jnp.where(rm, upd, 0.0)
                y_pairs.append(jnp.where(low, yd[0], yd[1]) + yoff)
        y = jnp.concatenate(y_pairs, axis=1) + q["xs"] * d_ref[...]
        y_ref[...] = y
        yg = y * _silu(z_ref[...].astype(F32))
        for g in range(2):
            sl = slice(256 * g, 256 * (g + 1))
            seg = yg[:, sl]
            yc_ref[:, sl] = (seg * _rstd(seg) * nw_ref[:, sl]).astype(BF16)

    return pl.pallas_call(
        body, name=name, grid=(nc,),
        in_specs=[pl.BlockSpec((ln, 768), lambda c: (c, 0)), pl.BlockSpec((ln, LANES), lambda c: (c, 0)),
                  pl.BlockSpec((ln, 512), lambda c: (c, OFF_Z // 512)), pl.BlockSpec((8, ln), lambda c: (0, c)),
                  pl.BlockSpec((8, LANES), lambda c: (0, 0)), pl.BlockSpec((16, ln), lambda c: (0, 0)),
                  pl.BlockSpec((1, 512), lambda c: (0, 0)), pl.BlockSpec((1, 512), lambda c: (0, 0))],
        out_specs=[pl.BlockSpec((ln, 512), lambda c: (c, 0)), pl.BlockSpec((ln, 512), lambda c: (c, 0)),
                   pl.BlockSpec((1, 4, LANES, LANES), lambda c: (c, 0, 0, 0))],
        out_shape=[jax.ShapeDtypeStruct((t, 512), BF16), jax.ShapeDtypeStruct((t, 512), F32),
                   jax.ShapeDtypeStruct((nc, 4, LANES, LANES), F32)],
        scratch_shapes=[pltpu.VMEM((4, LANES, LANES), F32)],
        compiler_params=_cparams(("arbitrary",)),
    )(act, dt32, p, dtt, hp_rows, hp_cols, d_e, norm_w)


def ssd_bwd(dyc, y, act, p, dt32, dtt, states, hp_rows, hp_cols, d_e, norm_w, name):
    t = act.shape[0]
    ln = SSM_CHUNK
    nc = t // ln

    def body(dyc_ref, y_ref, xbc_ref, dt_ref, z_ref, dtt_ref, st_ref, hpr_ref, hpc_ref, d_ref, nw_ref,
             dz_ref, dact_ref, ddt_ref, dnw_ref, dd_ref, dhp_ref, ds):
        @pl.when(pl.program_id(0) == 0)
        def _():
            ds[...] = jnp.zeros_like(ds)
            dnw_ref[...] = jnp.zeros_like(dnw_ref)
            dd_ref[...] = jnp.zeros_like(dd_ref)
            dhp_ref[...] = jnp.zeros_like(dhp_ref)

        ri, ci, expand, reduce = _ssd_consts()
        q = _ssd_prelude(xbc_ref, dt_ref, dtt_ref, hpr_ref, hpc_ref, ri, ci, expand)
        lane = lax.broadcasted_iota(jnp.int32, (1, LANES), 1)
        rown = lax.broadcasted_iota(jnp.int32, (LANES, 1), 0)
        low = lane < 64
        mask = ri >= ci
        mask_t = ci >= ri
        xs, xdt, acs, acst = q["xs"], q["xdt"], q["acs"], q["acst"]
        yv, zv, nw = y_ref[...], z_ref[...].astype(F32), nw_ref[...]
        sg = _sigmoid(zv)
        zz = zv * sg
        yg = yv * zz
        dycv = dyc_ref[...]
        u = dycv * nw
        dyg_parts, dnw_parts = [], []
        for g in range(2):
            sl = slice(256 * g, 256 * (g + 1))
            seg = yg[:, sl]
            rr = _rstd(seg)
            nrm = seg * rr
            dyg_parts.append(rr * (u[:, sl] - nrm * jnp.mean(nrm * u[:, sl], axis=-1, keepdims=True)))
            dnw_parts.append(jnp.sum(dycv[:, sl] * nrm, axis=0, keepdims=True))
        dyg = jnp.concatenate(dyg_parts, axis=1)
        dnw_ref[...] += jnp.concatenate(dnw_parts, axis=1)
        dy = dyg * zz
        dz_ref[...] = dyg * yv * (sg * (1.0 + zv * (1.0 - sg)))
        dd_ref[...] += jnp.sum(dy * xs, axis=0, keepdims=True)
        dxs = dy * d_ref[...]
        dy_b = dy.astype(BF16)
        xdt_b = xdt.astype(BF16)
        xw_b = (xdt * q["w_e"]).astype(BF16)
        bt = q["bm"].T.astype(BF16)
        ct = q["cm"].T.astype(BF16)
        cb_ = q["cm"].astype(BF16)
        bb_ = q["bm"].astype(BF16)
        dacs = jnp.zeros((ln, LANES), F32)
        dc = jnp.zeros((ln, LANES), F32)
        db = jnp.zeros((ln, LANES), F32)
        dxdt_pairs, yoffdy_pairs, dwe_pairs, ddec_pairs = [], [], [], []
        for g in range(2):
            gm = low if g == 0 else jnp.logical_not(low)
            rm = (rown < 64) if g == 0 else (rown >= 64)
            cg = jnp.where(gm, cb_, jnp.zeros_like(cb_))
            bg = jnp.where(gm, bb_, jnp.zeros_like(bb_))
            cb = jnp.dot(cg, bt, preferred_element_type=F32)
            cbt = jnp.dot(bg, ct, preferred_element_type=F32)
            dcb = jnp.zeros((ln, ln), F32)
            dcbt = jnp.zeros((ln, ln), F32)
            for pp in range(2):
                pi = 2 * g + pp
                sl = slice(LANES * pi, LANES * (pi + 1))
                xp = xdt_b[:, sl]
                dyp = dy_b[:, sl]
                xpt = xdt[:, sl].T.astype(BF16)
                dypt = dy[:, sl].T.astype(BF16)
                dxdt_p = jnp.zeros((ln, LANES), F32)
                for hh in range(2):
                    h = 2 * pi + hh
                    hm = low if hh == 0 else jnp.logical_not(low)
                    col = acs[:, h:h + 1]
                    row = acst[h:h + 1, :]
                    lam = jnp.exp(jnp.where(mask, col - row, -jnp.inf))
                    lam_t = jnp.exp(jnp.where(mask_t, row - col, -jnp.inf))
                    m = cb * lam
                    m_t = cbt * lam_t
                    dyh = jnp.where(hm, dyp, jnp.zeros_like(dyp))
                    xh = jnp.where(hm, xp, jnp.zeros_like(xp))
                    dm = jnp.dot(dyh, xpt, preferred_element_type=F32)
                    dm_t = jnp.dot(xh, dypt, preferred_element_type=F32)
                    dcb = dcb + dm * lam
                    dcbt = dcbt + dm_t * lam_t
                    rs = jnp.sum(dm * m, axis=1, keepdims=True) - jnp.sum(dm_t * m_t, axis=1, keepdims=True)
                    dacs = dacs + jnp.where(lane == h, rs, 0.0)
                    dxdt_p = dxdt_p + jnp.dot(m_t.astype(BF16), dyh, preferred_element_type=F32)
                sp = st_ref[0, pi]
                sp_b = sp.astype(BF16)
                dsn = ds[pi]
                dsn_b = dsn.astype(BF16)
                e_p, w_p, dec_p = q["e_e"][:, sl], q["w_e"][:, sl], q["dec_e"][:, sl]
                yoff = jnp.dot(cg, sp_b, preferred_element_type=F32) * e_p
                dyo = dy[:, sl] * e_p
                dyo_b = dyo.astype(BF16)
                dc = dc + lax.dot_general(dyo_b, sp_b, _NT, preferred_element_type=F32)
                ds_prev = dec_p * dsn + jnp.where(rm, jnp.dot(ct, dyo_b, preferred_element_type=F32), 0.0)
                yoffdy_pairs.append(dy[:, sl] * yoff)
                dxw = jnp.dot(bg, dsn_b, preferred_element_type=F32)
                db = db + lax.dot_general(xw_b[:, sl], dsn_b, _NT, preferred_element_type=F32)
                dxdt_p = dxdt_p + dxw * w_p
                dwe_pairs.append(dxw * xdt[:, sl])
                ddec_pairs.append(jnp.sum(dsn * sp, axis=0, keepdims=True))
                ds[pi] = ds_prev
                dxdt_pairs.append(dxdt_p)
            dc = dc + jnp.dot(dcb.astype(BF16), bg, preferred_element_type=F32)
            db = db + jnp.dot(dcbt.astype(BF16), cg, preferred_element_type=F32)
        dxdt = jnp.concatenate(dxdt_pairs, axis=1)
        yoffdy = jnp.concatenate(yoffdy_pairs, axis=1)
        dwe = jnp.concatenate(dwe_pairs, axis=1)
        ddec_e = jnp.broadcast_to(jnp.concatenate(ddec_pairs, axis=1), (8, SSM_INNER))
        last = acs[ln - 1:ln, :]
        w_col = jnp.exp(last - acs)
        dw_col = _dot_hi(dwe, reduce) * w_col
        dacs = dacs + _dot_hi(yoffdy, reduce) - dw_col
        dlast = jnp.sum(dw_col, axis=0, keepdims=True) + jnp.exp(last) * _dot_hi(ddec_e, reduce)[0:1, :]
        rowi = lax.broadcasted_iota(jnp.int32, (ln, 1), 0)
        dacs = dacs + jnp.where(rowi == ln - 1, dlast, 0.0)
        da = _dot_hi(q["triu"], dacs)
        ddt = da * q["aneg_r"] + _dot_hi(dxdt * xs, reduce)
        ddt_raw = jnp.where(lane < SSM_HEADS, ddt * _sigmoid(q["pre"]), 0.0)
        ddt_ref[...] = ddt_raw
        dhp_ref[0:1, :] += jnp.sum(ddt_raw, axis=0, keepdims=True)
        dhp_ref[1:2, :] += jnp.where(lane < SSM_HEADS, jnp.sum(da * q["dt"], axis=0, keepdims=True) * q["aneg_r"], 0.0)
        dact_ref[:, 0:512] = dxs + dxdt * q["dt_e"]
        dact_ref[:, 512:640] = db
        dact_ref[:, 640:768] = dc

    rev = lambda c: nc - 1 - c
    return pl.pallas_call(
        body, name=name, grid=(nc,),
        in_specs=[pl.BlockSpec((ln, 512), lambda c: (rev(c), 0)), pl.BlockSpec((ln, 512), lambda c: (rev(c), 0)),
                  pl.BlockSpec((ln, 768), lambda c: (rev(c), 0)),
                  pl.BlockSpec((ln, LANES), lambda c: (rev(c), 0)),
                  pl.BlockSpec((ln, 512), lambda c: (rev(c), OFF_Z // 512)), pl.BlockSpec((8, ln), lambda c: (0, rev(c))),
                  pl.BlockSpec((1, 4, LANES, LANES), lambda c: (rev(c), 0, 0, 0)),
                  pl.BlockSpec((8, LANES), lambda c: (0, 0)), pl.BlockSpec((16, ln), lambda c: (0, 0)),
                  pl.BlockSpec((1, 512), lambda c: (0, 0)), pl.BlockSpec((1, 512), lambda c: (0, 0))],
        out_specs=[pl.BlockSpec((ln, 512), lambda c: (rev(c), 0)), pl.BlockSpec((ln, 768), lambda c: (rev(c), 0)),
                   pl.BlockSpec((ln, LANES), lambda c: (rev(c), 0)), pl.BlockSpec((1, 512), lambda c: (0, 0)),
                   pl.BlockSpec((1, 512), lambda c: (0, 0)), pl.BlockSpec((8, LANES), lambda c: (0, 0))],
        out_shape=[jax.ShapeDtypeStruct((t, 512), F32), jax.ShapeDtypeStruct((t, 768), F32),
                   jax.ShapeDtypeStruct((t, LANES), F32), jax.ShapeDtypeStruct((1, 512), F32),
                   jax.ShapeDtypeStruct((1, 512), F32), jax.ShapeDtypeStruct((8, LANES), F32)],
        scratch_shapes=[pltpu.VMEM((4, LANES, LANES), F32)],
        compiler_params=_cparams(("arbitrary",)),
    )(dyc, y, act, dt32, p, dtt, states, hp_rows, hp_cols, d_e, norm_w)


def mod_shard_fwd(c_all, mod_w, mod_b_shard, name):
    def body(c_ref, w_ref, b_ref, o_ref):
        sc = _silu(c_ref[...])
        for l in range(DEPTH):
            o_ref[l] = _dot_f32(sc, w_ref[l]) + b_ref[l]

    return pl.pallas_call(body, name=name, out_shape=jax.ShapeDtypeStruct((DEPTH, N_DEV, mod_w.shape[2]), F32),
                          compiler_params=_cparams())(c_all, mod_w, mod_b_shard)


def mod_w_grad(c_all, dmod_shard, name):
    def body(c_ref, d_ref, o_ref):
        sc = _silu(c_ref[...])
        for l in range(DEPTH):
            o_ref[l] = lax.dot_general(sc, d_ref[l], _TN, precision=HI, preferred_element_type=F32)

    return pl.pallas_call(body, name=name, out_shape=jax.ShapeDtypeStruct((DEPTH, D_MODEL, dmod_shard.shape[2]), F32),
                          compiler_params=_cparams())(c_all, dmod_shard)


_BIG = ("w_in", "sc_conv_w", "ssm_conv_w", "w_sc_out", "w_sb_out", "w_ssm_out", "w_o", "w_ffn_in", "w_ffn_out")
_ROW_SHARDED = ("w_o", "w_ffn_out")
_CONV = ("sc_conv_w", "ssm_conv_w")
_SMALL = ("mod_b", "g_pre_mix", "g_post_mix", "g_pre_ffn", "g_post_ffn", "ssm_conv_b", "ssm_dt_bias", "ssm_a_log",
          "ssm_d", "ssm_norm_w")
_WEIGHTS = ("mod_w", "mod_b", "g_pre_mix", "g_post_mix", "g_pre_ffn", "g_post_ffn", "w_in", "sc_conv_w", "ssm_conv_w",
            "ssm_conv_b", "ssm_dt_bias", "ssm_a_log", "ssm_d", "ssm_norm_w", "w_sc_out", "w_sb_out", "w_ssm_out", "w_o",
            "w_ffn_in", "w_ffn_out")


def _gathered_to_full(g, row_sharded):
    _, dep, r, c = g.shape
    if row_sharded:
        return g.transpose(1, 0, 2, 3).reshape(dep, N_DEV * r, c)
    return jnp.concatenate([g[d] for d in range(N_DEV)], axis=2)


def _full_to_slots(w, row_sharded):
    dep, r, c = w.shape
    if row_sharded:
        return w.reshape(dep, N_DEV, r // N_DEV, c).transpose(1, 0, 2, 3).reshape(N_DEV, dep * (r // N_DEV), c)
    return w.reshape(dep, r, N_DEV, c // N_DEV).transpose(2, 0, 1, 3).reshape(N_DEV, dep * r, c // N_DEV)


def _pad_in_proj(shards):
    width = shards.shape[2]

    def cols(lo, hi):
        out = []
        while lo < hi:
            dev, a = divmod(lo, width)
            b = min(width, a + hi - lo)
            out.append(shards[dev, :, a:b])
            lo += b - a
        return out

    pad = jnp.zeros((shards.shape[1], OFF_Z - OFF_DT - 8), shards.dtype)
    return jnp.concatenate(cols(2824, 5896) + cols(0, 768) + cols(768, 1536) + cols(2048, 2816) + cols(2816, 2824)
                           + [pad] + cols(1536, 2048), axis=1)


def _in_proj_slots(dw_layers):
    width = IN_PROJ // N_DEV
    segments = ((0, 768, OFF_SC), (768, 1536, OFF_QKV), (1536, 2048, OFF_Z), (2048, 2816, OFF_XBC),
                (2816, 2824, OFF_DT), (2824, IN_PROJ, OFF_GATES))

    def internal(lo, hi):
        out = []
        for s0, s1, off in segments:
            a, b = max(lo, s0), min(hi, s1)
            if a < b:
                out.append((off + a - s0, off + b - s0))
        return out

    slots = []
    for d in range(N_DEV):
        pieces = internal(width * d, width * (d + 1))
        slots.append(jnp.concatenate([jnp.concatenate([w[:, a:b] for a, b in pieces], axis=1) for w in dw_layers], axis=0))
    return jnp.stack(slots)


def _ffn_in_keep_give(halves, cidx):
    width = 2 * FFN_HIDDEN // N_DEV
    keep, give = [], []
    for chip in range(4):
        src = 0 if chip < 2 else 1
        for out, core in ((keep, cidx), (give, 1 - cidx)):
            col0 = width * ((2 * chip) % 4 + core)
            out.append(jnp.concatenate([lax.dynamic_slice_in_dim(h[src], col0, width, axis=1) for h in halves], axis=0))
    return jnp.stack(keep), jnp.stack(give)


def _row(v):
    return v.reshape(1, -1)


def _local_step(x, target, mod, small, conv, big):
    lw, saved = [], []
    for l in range(DEPTH):
        w_in_p = _pad_in_proj(big["w_in_shards"][:, l])
        w_cat = jnp.concatenate([big["w_sc_out"][l], big["w_sb_out"][l], big["w_ssm_out"][l]], axis=0)
        hp_rows = jnp.zeros((8, LANES), F32).at[0, :SSM_HEADS].set(small["ssm_dt_bias"][l]).at[1, :SSM_HEADS].set(
            small["ssm_a_log"][l])
        hp_cols = jnp.concatenate([jnp.broadcast_to(small["ssm_dt_bias"][l][:, None], (SSM_HEADS, SSM_CHUNK)),
                                   jnp.broadcast_to(small["ssm_a_log"][l][:, None], (SSM_HEADS, SSM_CHUNK))], axis=0)
        lw.append(dict(
            w_in_p=w_in_p, w_cat=w_cat, w_o=big["w_o"][l], w_ffn_in=big["w_ffn_in"][l], w_ffn_out=big["w_ffn_out"][l],
            sc_w8=jnp.pad(conv["sc_conv_w"][l], ((0, 5), (0, 0))), ssm_w8=jnp.pad(conv["ssm_conv_w"][l], ((0, 4), (0, 0))),
            ssm_b=_row(small["ssm_conv_b"][l]), hp_rows=hp_rows, hp_cols=hp_cols,
            d_e=_row(jnp.repeat(small["ssm_d"][l], SSM_HEAD_DIM)), norm_w=_row(small["ssm_norm_w"][l]),
            g_pre_mix=_row(small["g_pre_mix"][l]), g_post_mix=_row(small["g_post_mix"][l]),
            g_pre_ffn=_row(small["g_pre_ffn"][l]), g_post_ffn=_row(small["g_post_ffn"][l]),
            shift1=mod[l, 0:1], scale1=mod[l, 1:2], gate1=mod[l, 2:3], shift2=mod[l, 3:4], scale2=mod[l, 4:5],
            gate2=mod[l, 5:6]))

    xl = x
    h = normmod_fwd(xl, lw[0]["g_pre_mix"], lw[0]["scale1"], lw[0]["shift1"], "normmod_fwd_0")
    dy = loss = None
    for l in range(DEPTH):
        w = lw[l]
        p = mm_nn([h], [w["w_in_p"]], BF16, f"in_proj_{l}")
        dt32 = mm_nn([h], [w["w_in_p"][:, OFF_DT:OFF_DT + LANES]], F32, f"in_proj_dt_{l}")
        ya, qkv, act = post_inproj(p, w["sc_w8"], w["ssm_w8"], w["ssm_b"], f"post_inproj_{l}")
        o, runs = sba_fwd(qkv, f"sba_fwd_{l}")
        dtt = dt32[:, :SSM_HEADS].T
        yc, ypre, states = ssd_fwd(act, p, dt32, dtt, w["hp_rows"], w["hp_cols"], w["d_e"], w["norm_w"], f"ssd_fwd_{l}")
        merged = branch_out_fwd(ya, o, yc, p, w["w_cat"], f"branch_fwd_{l}")
        mix = mm_nn([merged], [w["w_o"]], F32, f"out_proj_{l}")
        x1, h2 = resid_normmod_fwd(xl, mix, w["gate1"], w["g_post_mix"], w["g_pre_ffn"], w["scale2"], w["shift2"],
                                   f"resid_mix_{l}")
        gt, up, a = mm_swiglu_fwd(h2, w["w_ffn_in"], f"ffn_in_{l}")
        f = mm_nn([a], [w["w_ffn_out"]], F32, f"ffn_out_{l}")
        saved.append(dict(x=xl, h=h, p=p, ya=ya, qkv=qkv, act=act, o=o, runs=runs, dt32=dt32, dtt=dtt, yc=yc, ypre=ypre, states=states,
                          merged=merged, mix=mix, x1=x1, h2=h2, gt=gt, up=up, a=a, f=f))
        if l + 1 < DEPTH:
            nw = lw[l + 1]
            xl, h = resid_normmod_fwd(x1, f, w["gate2"], w["g_post_ffn"], nw["g_pre_mix"], nw["scale1"], nw["shift1"],
                                      f"resid_ffn_{l}")
        else:
            dy, loss = resid_loss(x1, f, w["gate2"], w["g_post_ffn"], target, "resid_loss")

    dmod = [None] * DEPTH
    gs = {k: [None] * DEPTH for k in _SMALL + _BIG}
    dxo = dy
    top, stl = lw[DEPTH - 1], saved[DEPTH - 1]
    df, dgate2, gs["g_post_ffn"][DEPTH - 1] = resid_bwd(dxo, stl["f"], top["gate2"], top["g_post_ffn"],
                                                        f"resid_ffn_bwd_{DEPTH - 1}")
    for l in reversed(range(DEPTH)):
        w, s = lw[l], saved[l]
        dgt, dup = mm_swiglu_bwd(df, w["w_ffn_out"], s["gt"], s["up"], f"ffn_out_bwd_{l}")
        gs["w_ffn_out"][l] = mm_tn(s["a"], df, f"dw_ffn_out_{l}")
        dh2 = mm_nt([dgt, dup], [w["w_ffn_in"], w["w_ffn_in"]], [0, FFN_HIDDEN], F32, f"ffn_in_bwd_{l}")
        gs["w_ffn_in"][l] = (mm_tn(s["h2"], dgt, f"dw_ffn_gate_{l}"), mm_tn(s["h2"], dup, f"dw_ffn_up_{l}"))
        dx1, dmix, dscale2, dshift2, gs["g_pre_ffn"][l], dgate1, gs["g_post_mix"][l] = normmod_resid_bwd(
            dh2, s["x1"], dxo, w["g_pre_ffn"], w["scale2"], s["mix"], w["gate1"], w["g_post_mix"], f"ffn_mix_bwd_{l}")
        dmerged = mm_nt([dmix], [w["w_o"]], [0], F32, f"out_proj_bwd_{l}")
        gs["w_o"][l] = mm_tn(s["merged"], dmix, f"dw_o_{l}")
        dp_gates, dya, dyb, dyc, dw_cat = branch_out_bwd(dmerged, s["ya"], s["o"], s["yc"], s["p"], w["w_cat"],
                                                         f"branch_bwd_{l}")
        gs["w_sc_out"][l], gs["w_sb_out"][l], gs["w_ssm_out"][l] = dw_cat[0:256], dw_cat[256:512], dw_cat[512:1024]
        dz, dact, ddt, dnw, dd_e, dhp = ssd_bwd(dyc, s["ypre"], s["act"], s["p"], s["dt32"], s["dtt"], s["states"], w["hp_rows"],
                                                w["hp_cols"], w["d_e"], w["norm_w"], f"ssd_bwd_{l}")
        gs["ssm_norm_w"][l] = dnw[0]
        gs["ssm_d"][l] = dd_e.reshape(SSM_HEADS, SSM_HEAD_DIM).sum(axis=1)
        gs["ssm_dt_bias"][l] = dhp[0, :SSM_HEADS]
        gs["ssm_a_log"][l] = dhp[1, :SSM_HEADS]
        dq, dk, dv = sba_bwd(s["qkv"], s["runs"], dyb, f"sba_bwd_{l}")
        dp, dscw, dssw, dssb = assemble_dp(dp_gates, dya, s["p"], w["sc_w8"], dq, dk, dv, dact, w["ssm_w8"], w["ssm_b"], ddt,
                                           dz, f"assemble_dp_{l}")
        gs["sc_conv_w"][l], gs["ssm_conv_w"][l], gs["ssm_conv_b"][l] = dscw[0:3], dssw[0:4], dssb[0]
        dh = mm_nt([dp], [w["w_in_p"]], [0], F32, f"in_proj_bwd_{l}")
        gs["w_in"][l] = mm_tn(s["h"], dp, f"dw_in_{l}")
        if l > 0:
            below, sb = lw[l - 1], saved[l - 1]
            dxo, df, dscale1, dshift1, gs["g_pre_mix"][l], dgate2_below, gs["g_post_ffn"][l - 1] = normmod_resid_bwd(
                dh, s["x"], dx1, w["g_pre_mix"], w["scale1"], sb["f"], below["gate2"], below["g_post_ffn"],
                f"mix_ffn_bwd_{l}")
        else:
            dxo, dscale1, dshift1, gs["g_pre_mix"][l] = normmod_bwd(dh, s["x"], dx1, w["g_pre_mix"], w["scale1"],
                                                                    f"normmod_mix_bwd_{l}")
            dgate2_below = None
        dmod[l] = jnp.concatenate([dshift1, dscale1, dgate1, dshift2, dscale2, dgate2], axis=0)
        dgate2 = dgate2_below
    for k in ("g_pre_mix", "g_post_mix", "g_pre_ffn", "g_post_ffn"):
        gs[k] = [g[0] for g in gs[k]]
    per_layer = ("w_in", "w_ffn_in")
    grads = {k: (v if k in per_layer else jnp.stack(v)) for k, v in gs.items() if k != "mod_b"}
    return loss[0, 0], dxo, jnp.stack(dmod), grads


def kernel(x, c, mod_w, mod_b, g_pre_mix, g_post_mix, g_pre_ffn, g_post_ffn, w_in, sc_conv_w, ssm_conv_w, ssm_conv_b, ssm_dt_bias, ssm_a_log, ssm_d, ssm_norm_w, w_sc_out, w_sb_out, w_ssm_out, w_o, w_ffn_in, w_ffn_out, loss_target, m_mod_w, m_mod_b, m_g_pre_mix, m_g_post_mix, m_g_pre_ffn, m_g_post_ffn, m_w_in, m_sc_conv_w, m_ssm_conv_w, m_ssm_conv_b, m_ssm_dt_bias, m_ssm_a_log, m_ssm_d, m_ssm_norm_w, m_w_sc_out, m_w_sb_out, m_w_ssm_out, m_w_o, m_w_ffn_in, m_w_ffn_out, v_mod_w, v_mod_b, v_g_pre_mix, v_g_post_mix, v_g_pre_ffn, v_g_post_ffn, v_w_in, v_sc_conv_w, v_ssm_conv_w, v_ssm_conv_b, v_ssm_dt_bias, v_ssm_a_log, v_ssm_d, v_ssm_norm_w, v_w_sc_out, v_w_sb_out, v_w_ssm_out, v_w_o, v_w_ffn_in, v_w_ffn_out):
    wts = dict(mod_w=mod_w, mod_b=mod_b, g_pre_mix=g_pre_mix, g_post_mix=g_post_mix, g_pre_ffn=g_pre_ffn,
               g_post_ffn=g_post_ffn, w_in=w_in, sc_conv_w=sc_conv_w, ssm_conv_w=ssm_conv_w, ssm_conv_b=ssm_conv_b,
               ssm_dt_bias=ssm_dt_bias, ssm_a_log=ssm_a_log, ssm_d=ssm_d, ssm_norm_w=ssm_norm_w, w_sc_out=w_sc_out,
               w_sb_out=w_sb_out, w_ssm_out=w_ssm_out, w_o=w_o, w_ffn_in=w_ffn_in, w_ffn_out=w_ffn_out)
    ms = dict(mod_w=m_mod_w, mod_b=m_mod_b, g_pre_mix=m_g_pre_mix, g_post_mix=m_g_post_mix, g_pre_ffn=m_g_pre_ffn,
              g_post_ffn=m_g_post_ffn, w_in=m_w_in, sc_conv_w=m_sc_conv_w, ssm_conv_w=m_ssm_conv_w,
              ssm_conv_b=m_ssm_conv_b, ssm_dt_bias=m_ssm_dt_bias, ssm_a_log=m_ssm_a_log, ssm_d=m_ssm_d,
              ssm_norm_w=m_ssm_norm_w, w_sc_out=m_w_sc_out, w_sb_out=m_w_sb_out, w_ssm_out=m_w_ssm_out, w_o=m_w_o,
              w_ffn_in=m_w_ffn_in, w_ffn_out=m_w_ffn_out)
    vs = dict(mod_w=v_mod_w, mod_b=v_mod_b, g_pre_mix=v_g_pre_mix, g_post_mix=v_g_post_mix, g_pre_ffn=v_g_pre_ffn,
              g_post_ffn=v_g_post_ffn, w_in=v_w_in, sc_conv_w=v_sc_conv_w, ssm_conv_w=v_ssm_conv_w,
              ssm_conv_b=v_ssm_conv_b, ssm_dt_bias=v_ssm_dt_bias, ssm_a_log=v_ssm_a_log, ssm_d=v_ssm_d,
              ssm_norm_w=v_ssm_norm_w, w_sc_out=v_w_sc_out, w_sb_out=v_w_sb_out, w_ssm_out=v_w_ssm_out, w_o=v_w_o,
              w_ffn_in=v_w_ffn_in, w_ffn_out=v_w_ffn_out)
    me = 4 * lax.axis_index("x") + 2 * lax.axis_index("y") + lax.axis_index("c")
    mod_cols = mod_w.shape[2]

    pack1, sizes1 = _pack_rows([c, sc_conv_w, ssm_conv_w], F32, 8)
    mm_names = [k for k in _BIG if k not in _CONV]
    got1, *gathered = all_gather_multi([pack1] + [wts[k].astype(BF16) for k in mm_names], "gather_weights")
    got1 = got1.reshape(N_DEV, -1)
    c_all, sc_g, ssm_g = _unpack(got1, sizes1, [(D_MODEL,), sc_conv_w.shape, ssm_conv_w.shape])
    conv = dict(sc_conv_w=_gathered_to_full(sc_g, False), ssm_conv_w=_gathered_to_full(ssm_g, False))

    mod_b_shard = lax.dynamic_slice_in_dim(mod_b, me * mod_cols, mod_cols, axis=1).reshape(DEPTH, 1, mod_cols)
    mod_sh = mod_shard_fwd(c_all, mod_w, mod_b_shard, "mod_shard_fwd")
    pack2, sizes2 = _pack_rows([mod_sh], F32, 8)
    got2 = all_gather_multi([pack2], "gather_mod")[0].reshape(N_DEV, -1)
    mod_all = _unpack(got2, sizes2, [mod_sh.shape])[0]
    mod_mine = lax.dynamic_index_in_dim(mod_all, me, axis=2, keepdims=False)
    mod = mod_mine.transpose(1, 0, 2).reshape(DEPTH, 6, D_MODEL)

    big ={k: _gathered_to_full(g, k in _ROW_SHARDED) for k, g in zip(mm_names, gathered) if k != "w_in"}
    big["w_in_shards"] = gathered[mm_names.index("w_in")]

    small = {k: wts[k] for k in _SMALL}
    loss_part, dx, dmod, grads = _local_step(x[0], loss_target[0], mod, small, conv, big)
    loss = lax.psum(loss_part, ("x", "y", "c"))

    small_parts = [dmod.reshape(DEPTH, 6 * D_MODEL)] + [grads[k] for k in _SMALL[1:]]
    pack5, sizes5 = _pack_rows(small_parts, F32, 8)
    pack_conv, sizes_conv = _pack_rows([grads[k] for k in _CONV], F32, 8)
    got5, got_conv = all_gather_multi([pack5, pack_conv], "gather_small_grads")
    w5, _ = _pack_rows([wts[k] for k in _SMALL], F32, 8)
    m5, _ = _pack_rows([ms[k] for k in _SMALL], F32, 8)
    v5, _ = _pack_rows([vs[k] for k in _SMALL], F32, 8)
    res5 = adamw_flat(got5, w5, m5, v5, "adamw_small")
    small_out = [_unpack(r.reshape(-1), sizes5, [wts[k].shape for k in _SMALL]) for r in res5]

    conv_full = _unpack(got_conv.reshape(N_DEV, -1), sizes_conv, [grads[k].shape for k in _CONV])
    conv_mine = [lax.dynamic_slice_in_dim(g, me * wts[k].shape[2], wts[k].shape[2], axis=3)
                 for k, g in zip(_CONV, conv_full)]
    conv_slot_sizes = [math.prod(wts[k].shape) for k in _CONV]
    conv_slots = jnp.concatenate([g.reshape(N_DEV, -1) for g in conv_mine], axis=1)
    pad_c = -conv_slots.shape[1] % (8 * LANES)
    conv_slots = jnp.pad(conv_slots, ((0, 0), (0, pad_c))).reshape(N_DEV, -1, LANES)
    wc, _ = _pack_rows([wts[k] for k in _CONV], F32, 8)
    mc, _ = _pack_rows([ms[k] for k in _CONV], F32, 8)
    vc, _ = _pack_rows([vs[k] for k in _CONV], F32, 8)
    res_c = adamw_flat(conv_slots, wc, mc, vc, "adamw_conv")
    conv_out = [_unpack(r.reshape(-1), conv_slot_sizes, [wts[k].shape for k in _CONV]) for r in res_c]

    dmod_all = got5.reshape(N_DEV, -1)[:, :DEPTH * 6 * D_MODEL].reshape(N_DEV, DEPTH, 6 * D_MODEL)
    dmod_shard = lax.dynamic_slice_in_dim(dmod_all, me * mod_cols, mod_cols, axis=2).transpose(1, 0, 2)
    g_mod_w = mod_w_grad(c_all, dmod_shard, "mod_w_grad")
    rows2 = lambda a: a.reshape(a.shape[0] * a.shape[1], a.shape[2])
    res_mw = adamw_flat(rows2(g_mod_w)[None], rows2(mod_w), rows2(m_mod_w), rows2(v_mod_w), "adamw_mod_w")
    mod_w_out = [r.reshape(mod_w.shape) for r in res_mw]

    cidx = lax.axis_index("c")
    keeps, gives = [], []
    for k in mm_names:
        if k == "w_ffn_in":
            keep, give = _ffn_in_keep_give(grads[k], cidx)
        else:
            slots = (_in_proj_slots(grads[k]) if k == "w_in"
                     else _full_to_slots(grads[k].astype(BF16), k in _ROW_SHARDED))
            by_chip = slots.reshape(4, 2, *slots.shape[1:])
            keep = lax.dynamic_index_in_dim(by_chip, cidx, 1, keepdims=False)
            give = lax.dynamic_index_in_dim(by_chip, 1 - cidx, 1, keepdims=False)
        keeps.append(keep)
        gives.append(give)
    gots = swap_with_sibling(gives, "swap_grads")
    pairs = []
    for k, keep, got in zip(mm_names, keeps, gots):
        rows4 = (4 * keep.shape[1], keep.shape[2])
        pairs.append(add_pairs(keep.reshape(rows4), got.reshape(rows4), f"add_pairs_{k}").reshape(keep.shape))
    recvs = exchange_chips(pairs, "exchange_grads")
    big_out = {}
    for k, recv in zip(mm_names, recvs):
        res = adamw_flat(recv, rows2(wts[k]), rows2(ms[k]), rows2(vs[k]), f"adamw_{k}")
        big_out[k] = [r.reshape(wts[k].shape) for r in res]

    outs = []
    for kind in range(4):
        by_name = {"mod_w": mod_w_out[kind]}
        by_name.update(zip(_SMALL, small_out[kind]))
        by_name.update(zip(_CONV, conv_out[kind]))
        by_name.update({k: v[kind] for k, v in big_out.items()})
        outs.extend(by_name[k] for k in _WEIGHTS)
    return (loss, dx[None], *outs)
```
